```python
import jax, jax.numpy as jnp
from jax import lax
import numpy as np

D_MODEL = 1024
BATCH = 32
SEQ = 2048
DEPTH = 1

D_MIX = D_MODEL
MLA_HEADS = 4
MLA_NOPE = 128
MLA_ROPE = 64
MLA_V = 128
MLA_WIDTH = MLA_HEADS * MLA_V
Q_LORA = 256
KV_LORA = 128
ROPE_THETA = 10000.0
Q_BLOCK = 128
RW_HEAD = 64
RW_WIDTH = D_MIX - MLA_WIDTH
RW_HEADS = RW_WIDTH // RW_HEAD
W_LORA = 64
A_LORA = 64
RW_GN_EPS = 64e-5
NORM_EPS = 1e-6
MLA_COLS = Q_LORA + KV_LORA + MLA_ROPE
RW_SHIFT_COLS = 3 * RW_WIDTH + W_LORA + A_LORA
GATE_COLS = D_MIX
D_IN = MLA_COLS + RW_SHIFT_COLS + GATE_COLS

kernel_name = 'hymba_mla_rwkv7_sandwich'


def rmsnorm(x, g):
    xf = x.astype(jnp.float32)
    y = xf * lax.rsqrt(jnp.mean(xf * xf, axis=-1, keepdims=True) + NORM_EPS)
    return (y * g.astype(jnp.float32)).astype(x.dtype)


def rope_tables(positions):
    inv_freq = ROPE_THETA ** (-jnp.arange(0, MLA_ROPE, 2, dtype=jnp.float32) / MLA_ROPE)
    ang = positions.astype(jnp.float32)[..., None] * inv_freq
    ang = jnp.concatenate([ang, ang], axis=-1)
    return jnp.cos(ang), jnp.sin(ang)


def apply_rope(x, cos, sin):
    x1, x2 = jnp.split(x, 2, axis=-1)
    rot = jnp.concatenate([-x2, x1], axis=-1)
    return (x.astype(jnp.float32) * cos + rot.astype(jnp.float32) * sin).astype(x.dtype)


def token_shift(p):
    return jnp.pad(p[:, :-1], ((0, 0), (1, 0), (0, 0)))


def mla_attention(q_nope, q_rope, k_nope, k_rope, v):
    T = q_nope.shape[1]
    scale = (MLA_NOPE + MLA_ROPE) ** -0.5
    outs = []
    for i in range(T // Q_BLOCK):
        s, e = i * Q_BLOCK, (i + 1) * Q_BLOCK
        scores = (jnp.einsum('bqhd,bkhd->bhqk', q_nope[:, s:e], k_nope[:, :e])
                  + jnp.einsum('bqhr,bkr->bhqk', q_rope[:, s:e], k_rope[:, :e])).astype(jnp.float32) * scale
        mask = (s + jnp.arange(Q_BLOCK))[:, None] >= jnp.arange(e)[None, :]
        scores = jnp.where(mask, scores, -jnp.inf)
        probs = jax.nn.softmax(scores, axis=-1).astype(v.dtype)
        outs.append(jnp.einsum('bhqk,bkhd->bqhd', probs, v[:, :e]))
    return jnp.concatenate(outs, axis=1)


def wkv7_scan(r, w, k, v, kk, a):
    B, T, H, N = r.shape

    def step(S, inp):
        r_t, w_t, k_t, v_t, kk_t, a_t = inp
        sa = jnp.einsum('bhvk,bhk->bhv', S, -kk_t)
        S = (S * w_t[:, :, None, :] + sa[..., None] * (kk_t * a_t)[:, :, None, :]
             + v_t[..., None] * k_t[:, :, None, :])
        return S, jnp.einsum('bhvk,bhk->bhv', S, r_t)

    xs = tuple(jnp.moveaxis(t.astype(jnp.float32), 1, 0) for t in (r, w, k, v, kk, a))
    S0 = jnp.zeros((B, H, N, N), jnp.float32)
    _, ys = lax.scan(step, S0, xs)
    return jnp.moveaxis(ys, 0, 1)


def _fwd_setup_inputs(seed: int = 0) -> dict:
    key = jax.random.key(seed)
    ks = jax.random.split(key, 24)
    L = DEPTH
    f32 = jnp.float32

    def nrm(k, shape, scale):
        return jax.random.normal(k, shape, f32) * scale

    x = nrm(ks[0], (BATCH, SEQ, D_MODEL), 1.0)
    offset = jax.random.randint(ks[1], (BATCH, 1), 0, 4096, dtype=jnp.int32)
    positions = offset + jnp.arange(SEQ, dtype=jnp.int32)[None, :]
    return {
        'x': x,
        'positions': positions,
        'norm_pre_g': 1.0 + nrm(ks[2], (L, D_MODEL), 0.02),
        'w_in': nrm(ks[3], (L, D_MODEL, D_IN), D_MODEL ** -0.5),
        'mla_q_norm_g': 1.0 + nrm(ks[4], (L, Q_LORA), 0.02),
        'mla_w_uq': nrm(ks[5], (L, Q_LORA, MLA_HEADS * (MLA_NOPE + MLA_ROPE)), Q_LORA ** -0.5),
        'mla_kv_norm_g': 1.0 + nrm(ks[6], (L, KV_LORA), 0.02),
        'mla_w_ukv': nrm(ks[7], (L, KV_LORA, MLA_HEADS * (MLA_NOPE + MLA_V)), KV_LORA ** -0.5),
        'rw_mu': jax.random.uniform(ks[8], (L, RW_SHIFT_COLS), f32),
        'rw_w0': -2.5 + nrm(ks[9], (L, RW_WIDTH), 0.5),
        'rw_w2': nrm(ks[10], (L, W_LORA, RW_WIDTH), 0.5 * W_LORA ** -0.5),
        'rw_a0': nrm(ks[11], (L, RW_WIDTH), 0.1),
        'rw_a2': nrm(ks[12], (L, A_LORA, RW_WIDTH), 0.5 * A_LORA ** -0.5),
        'rw_k_k': 0.85 + nrm(ks[13], (L, RW_WIDTH), 0.05),
        'rw_k_a': 1.0 + nrm(ks[14], (L, RW_WIDTH), 0.05),
        'rw_r_k': nrm(ks[15], (L, RW_HEADS, RW_HEAD), 0.1),
        'rw_ln_g': 1.0 + nrm(ks[16], (L, RW_WIDTH), 0.02),
        'rw_ln_b': nrm(ks[17], (L, RW_WIDTH), 0.02),
        'w_out': nrm(ks[18], (L, D_MIX, D_MODEL), D_MIX ** -0.5),
        'norm_post_g': 1.0 + nrm(ks[19], (L, D_MODEL), 0.02),
    }


def _fwd_reference(x, positions, norm_pre_g, w_in, mla_q_norm_g, mla_w_uq, mla_kv_norm_g, mla_w_ukv,
              rw_mu, rw_w0, rw_w2, rw_a0, rw_a2, rw_k_k, rw_k_a, rw_r_k, rw_ln_g, rw_ln_b,
              w_out, norm_post_g):
    B, T, _ = x.shape
    f32 = jnp.float32
    cos, sin = rope_tables(positions)
    h = x
    for l in range(DEPTH):
        u = rmsnorm(h, norm_pre_g[l])
        p = u @ w_in[l]
        p_mla, p_rw, z = jnp.split(p, [MLA_COLS, MLA_COLS + RW_SHIFT_COLS], axis=-1)

        c_q, c_kv, k_r = jnp.split(p_mla, [Q_LORA, Q_LORA + KV_LORA], axis=-1)
        q = (rmsnorm(c_q, mla_q_norm_g[l]) @ mla_w_uq[l]).reshape(B, T, MLA_HEADS, MLA_NOPE + MLA_ROPE)
        q_nope, q_rope = jnp.split(q, [MLA_NOPE], axis=-1)
        kv = (rmsnorm(c_kv, mla_kv_norm_g[l]) @ mla_w_ukv[l]).reshape(B, T, MLA_HEADS, MLA_NOPE + MLA_V)
        k_nope, v_mla = jnp.split(kv, [MLA_NOPE], axis=-1)
        q_rope = apply_rope(q_rope, cos[:, :, None, :], sin[:, :, None, :])
        k_r = apply_rope(k_r, cos, sin)
        y_mla = mla_attention(q_nope, q_rope, k_nope, k_r, v_mla).reshape(B, T, MLA_WIDTH)

        ps = p_rw + (token_shift(p_rw) - p_rw) * rw_mu[l]
        r, k, v, xw, xa = jnp.split(
            ps, [RW_WIDTH, 2 * RW_WIDTH, 3 * RW_WIDTH, 3 * RW_WIDTH + W_LORA], axis=-1)
        w_log = -jax.nn.softplus(-(rw_w0[l] + jnp.tanh(xw) @ rw_w2[l]).astype(f32)) - 0.5
        decay = jnp.exp(-jnp.exp(w_log))
        a = jax.nn.sigmoid((rw_a0[l] + xa @ rw_a2[l]).astype(f32))
        kk = (k * rw_k_k[l]).astype(f32).reshape(B, T, RW_HEADS, RW_HEAD)
        kk = kk / jnp.maximum(jnp.linalg.norm(kk, axis=-1, keepdims=True), 1e-12)
        k = k.astype(f32) * (1.0 + (a - 1.0) * rw_k_a[l].astype(f32))
        heads = lambda t: t.reshape(B, T, RW_HEADS, RW_HEAD)
        r_h, k_h, v_h = heads(r.astype(f32)), heads(k), heads(v.astype(f32))
        y = wkv7_scan(r_h, heads(decay), k_h, v_h, kk, heads(a))
        mean = jnp.mean(y, axis=-1, keepdims=True)
        var = jnp.mean(jnp.square(y - mean), axis=-1, keepdims=True)
        y = ((y - mean) * lax.rsqrt(var + RW_GN_EPS)).reshape(B, T, RW_WIDTH)
        y = y * rw_ln_g[l].astype(f32) + rw_ln_b[l].astype(f32)
        bonus = jnp.sum(r_h * k_h * rw_r_k[l].astype(f32), axis=-1, keepdims=True) * v_h
        y_rw = (y + bonus.reshape(B, T, RW_WIDTH)).astype(x.dtype)

        y_cat = jnp.concatenate([y_mla, y_rw], axis=-1) * jax.nn.silu(z)
        out = y_cat @ w_out[l]
        h = h + rmsnorm(out, norm_post_g[l])
    return h


import jax as _jax
import jax.numpy as _jnp

TWIN_FORMAT = 'train_step'
FWD_PARAMS = ['x', 'positions', 'norm_pre_g', 'w_in', 'mla_q_norm_g', 'mla_w_uq', 'mla_kv_norm_g', 'mla_w_ukv', 'rw_mu', 'rw_w0', 'rw_w2', 'rw_a0', 'rw_a2', 'rw_k_k', 'rw_k_a', 'rw_r_k', 'rw_ln_g', 'rw_ln_b', 'w_out', 'norm_post_g']
TWIN_WEIGHTS = ['norm_pre_g', 'w_in', 'mla_q_norm_g', 'mla_w_uq', 'mla_kv_norm_g', 'mla_w_ukv', 'rw_mu', 'rw_w0', 'rw_w2', 'rw_a0', 'rw_a2', 'rw_k_k', 'rw_k_a', 'rw_r_k', 'rw_ln_g', 'rw_ln_b', 'w_out', 'norm_post_g']
TWIN_DIFF_INPUT = 'x'
TWIN_INPUTS = ['x', 'positions', 'norm_pre_g', 'w_in', 'mla_q_norm_g', 'mla_w_uq', 'mla_kv_norm_g', 'mla_w_ukv', 'rw_mu', 'rw_w0', 'rw_w2', 'rw_a0', 'rw_a2', 'rw_k_k', 'rw_k_a', 'rw_r_k', 'rw_ln_g', 'rw_ln_b', 'w_out', 'norm_post_g', 'loss_target', 'm_norm_pre_g', 'm_w_in', 'm_mla_q_norm_g', 'm_mla_w_uq', 'm_mla_kv_norm_g', 'm_mla_w_ukv', 'm_rw_mu', 'm_rw_w0', 'm_rw_w2', 'm_rw_a0', 'm_rw_a2', 'm_rw_k_k', 'm_rw_k_a', 'm_rw_r_k', 'm_rw_ln_g', 'm_rw_ln_b', 'm_w_out', 'm_norm_post_g', 'v_norm_pre_g', 'v_w_in', 'v_mla_q_norm_g', 'v_mla_w_uq', 'v_mla_kv_norm_g', 'v_mla_w_ukv', 'v_rw_mu', 'v_rw_w0', 'v_rw_w2', 'v_rw_a0', 'v_rw_a2', 'v_rw_k_k', 'v_rw_k_a', 'v_rw_r_k', 'v_rw_ln_g', 'v_rw_ln_b', 'v_w_out', 'v_norm_post_g']
TWIN_OUTPUTS = ['loss', 'grad_x', 'grad_norm_pre_g', 'grad_w_in', 'grad_mla_q_norm_g', 'grad_mla_w_uq', 'grad_mla_kv_norm_g', 'grad_mla_w_ukv', 'grad_rw_mu', 'grad_rw_w0', 'grad_rw_w2', 'grad_rw_a0', 'grad_rw_a2', 'grad_rw_k_k', 'grad_rw_k_a', 'grad_rw_r_k', 'grad_rw_ln_g', 'grad_rw_ln_b', 'grad_w_out', 'grad_norm_post_g', 'delta_norm_pre_g', 'delta_w_in', 'delta_mla_q_norm_g', 'delta_mla_w_uq', 'delta_mla_kv_norm_g', 'delta_mla_w_ukv', 'delta_rw_mu', 'delta_rw_w0', 'delta_rw_w2', 'delta_rw_a0', 'delta_rw_a2', 'delta_rw_k_k', 'delta_rw_k_a', 'delta_rw_r_k', 'delta_rw_ln_g', 'delta_rw_ln_b', 'delta_w_out', 'delta_norm_post_g', 'new_m_norm_pre_g', 'new_m_w_in', 'new_m_mla_q_norm_g', 'new_m_mla_w_uq', 'new_m_mla_kv_norm_g', 'new_m_mla_w_ukv', 'new_m_rw_mu', 'new_m_rw_w0', 'new_m_rw_w2', 'new_m_rw_a0', 'new_m_rw_a2', 'new_m_rw_k_k', 'new_m_rw_k_a', 'new_m_rw_r_k', 'new_m_rw_ln_g', 'new_m_rw_ln_b', 'new_m_w_out', 'new_m_norm_post_g', 'new_v_norm_pre_g', 'new_v_w_in', 'new_v_mla_q_norm_g', 'new_v_mla_w_uq', 'new_v_mla_kv_norm_g', 'new_v_mla_w_ukv', 'new_v_rw_mu', 'new_v_rw_w0', 'new_v_rw_w2', 'new_v_rw_a0', 'new_v_rw_a2', 'new_v_rw_k_k', 'new_v_rw_k_a', 'new_v_rw_r_k', 'new_v_rw_ln_g', 'new_v_rw_ln_b', 'new_v_w_out', 'new_v_norm_post_g']
TWIN_LEAF_KINDS = {'loss': 'loss', 'grad_x': 'grad_x', 'grad_norm_pre_g': 'grad_w', 'grad_w_in': 'grad_w', 'grad_mla_q_norm_g': 'grad_w', 'grad_mla_w_uq': 'grad_w', 'grad_mla_kv_norm_g': 'grad_w', 'grad_mla_w_ukv': 'grad_w', 'grad_rw_mu': 'grad_w', 'grad_rw_w0': 'grad_w', 'grad_rw_w2': 'grad_w', 'grad_rw_a0': 'grad_w', 'grad_rw_a2': 'grad_w', 'grad_rw_k_k': 'grad_w', 'grad_rw_k_a': 'grad_w', 'grad_rw_r_k': 'grad_w', 'grad_rw_ln_g': 'grad_w', 'grad_rw_ln_b': 'grad_w', 'grad_w_out': 'grad_w', 'grad_norm_post_g': 'grad_w', 'delta_norm_pre_g': 'delta_w', 'delta_w_in': 'delta_w', 'delta_mla_q_norm_g': 'delta_w', 'delta_mla_w_uq': 'delta_w', 'delta_mla_kv_norm_g': 'delta_w', 'delta_mla_w_ukv': 'delta_w', 'delta_rw_mu': 'delta_w', 'delta_rw_w0': 'delta_w', 'delta_rw_w2': 'delta_w', 'delta_rw_a0': 'delta_w', 'delta_rw_a2': 'delta_w', 'delta_rw_k_k': 'delta_w', 'delta_rw_k_a': 'delta_w', 'delta_rw_r_k': 'delta_w', 'delta_rw_ln_g': 'delta_w', 'delta_rw_ln_b': 'delta_w', 'delta_w_out': 'delta_w', 'delta_norm_post_g': 'delta_w', 'new_m_norm_pre_g': 'new_m', 'new_m_w_in': 'new_m', 'new_m_mla_q_norm_g': 'new_m', 'new_m_mla_w_uq': 'new_m', 'new_m_mla_kv_norm_g': 'new_m', 'new_m_mla_w_ukv': 'new_m', 'new_m_rw_mu': 'new_m', 'new_m_rw_w0': 'new_m', 'new_m_rw_w2': 'new_m', 'new_m_rw_a0': 'new_m', 'new_m_rw_a2': 'new_m', 'new_m_rw_k_k': 'new_m', 'new_m_rw_k_a': 'new_m', 'new_m_rw_r_k': 'new_m', 'new_m_rw_ln_g': 'new_m', 'new_m_rw_ln_b': 'new_m', 'new_m_w_out': 'new_m', 'new_m_norm_post_g': 'new_m', 'new_v_norm_pre_g': 'new_v', 'new_v_w_in': 'new_v', 'new_v_mla_q_norm_g': 'new_v', 'new_v_mla_w_uq': 'new_v', 'new_v_mla_kv_norm_g': 'new_v', 'new_v_mla_w_ukv': 'new_v', 'new_v_rw_mu': 'new_v', 'new_v_rw_w0': 'new_v', 'new_v_rw_w2': 'new_v', 'new_v_rw_a0': 'new_v', 'new_v_rw_a2': 'new_v', 'new_v_rw_k_k': 'new_v', 'new_v_rw_k_a': 'new_v', 'new_v_rw_r_k': 'new_v', 'new_v_rw_ln_g': 'new_v', 'new_v_rw_ln_b': 'new_v', 'new_v_w_out': 'new_v', 'new_v_norm_post_g': 'new_v'}


def _forward(args):
    return _fwd_reference(*[args[k] for k in FWD_PARAMS])


def _output_shape():
    out = _jax.eval_shape(lambda: _forward(_fwd_setup_inputs(0)))
    return out.shape, out.dtype

N_MICROBATCH = 1
ADAM_LR = 0.001
ADAM_B1 = 0.9
ADAM_B2 = 0.999
ADAM_EPS = 1e-08
ADAM_WD = 0.01
ADAM_STEP = 10
PER_EXAMPLE_BATCH_AXIS = {'x': 0, 'positions': 0, 'loss_target': 0}
SHARED_INPUTS = []
_WEIGHT_DTYPES = {'norm_pre_g': _jnp.float32, 'w_in': _jnp.float32, 'mla_q_norm_g': _jnp.float32, 'mla_w_uq': _jnp.float32, 'mla_kv_norm_g': _jnp.float32, 'mla_w_ukv': _jnp.float32, 'rw_mu': _jnp.float32, 'rw_w0': _jnp.float32, 'rw_w2': _jnp.float32, 'rw_a0': _jnp.float32, 'rw_a2': _jnp.float32, 'rw_k_k': _jnp.float32, 'rw_k_a': _jnp.float32, 'rw_r_k': _jnp.float32, 'rw_ln_g': _jnp.float32, 'rw_ln_b': _jnp.float32, 'w_out': _jnp.float32, 'norm_post_g': _jnp.float32}
MOMENT_SCALE = {'norm_pre_g': 7.987812e-01, 'w_in': 4.491744e-01, 'mla_q_norm_g': 1.308782e-01, 'mla_w_uq': 7.418401e-02, 'mla_kv_norm_g': 2.859503e-01, 'mla_w_ukv': 9.192786e-02, 'rw_mu': 9.332345e-01, 'rw_w0': 3.659999e-01, 'rw_w2': 4.689042e-02, 'rw_a0': 2.736901e-01, 'rw_a2': 2.258121e-01, 'rw_k_k': 2.811597e-01, 'rw_k_a': 6.246710e-01, 'rw_r_k': 1.058417e+00, 'rw_ln_g': 5.183680e-01, 'rw_ln_b': 1.266267e+00, 'w_out': 3.595634e-01, 'norm_post_g': 6.396189e+01}


def _to_microbatches(a, axis):
    t = _jnp.moveaxis(a, axis, 0)
    t = t.reshape((N_MICROBATCH, t.shape[0] // N_MICROBATCH) + t.shape[1:])
    return _jnp.moveaxis(t, 1, axis + 1)


def setup_inputs(seed: int = 0) -> dict:
    inp = _fwd_setup_inputs(seed)
    key = _jax.random.fold_in(_jax.random.key(seed), 7919)
    shape, _ = _output_shape()
    out = dict(inp)
    out["loss_target"] = _jax.random.normal(_jax.random.fold_in(key, 0), shape, _jnp.float32)
    for i, name in enumerate(TWIN_WEIGHTS):
        w = inp[name].astype(_jnp.float32)
        if MOMENT_SCALE is None:
            s = _jnp.sqrt(_jnp.mean(_jnp.square(w)) + 1e-30)
        else:
            s = MOMENT_SCALE[name]
        km, kv = _jax.random.split(_jax.random.fold_in(key, i + 1))
        out[name] = w
        out["m_" + name] = s * _jax.random.normal(km, w.shape, _jnp.float32)
        out["v_" + name] = (s * s) * _jax.random.uniform(kv, w.shape, _jnp.float32, 0.5, 1.5)
    if N_MICROBATCH > 1:
        for name, axis in PER_EXAMPLE_BATCH_AXIS.items():
            out[name] = _to_microbatches(out[name], axis)
    return {'x': out['x'], 'positions': out['positions'], 'norm_pre_g': out['norm_pre_g'], 'w_in': out['w_in'], 'mla_q_norm_g': out['mla_q_norm_g'], 'mla_w_uq': out['mla_w_uq'], 'mla_kv_norm_g': out['mla_kv_norm_g'], 'mla_w_ukv': out['mla_w_ukv'], 'rw_mu': out['rw_mu'], 'rw_w0': out['rw_w0'], 'rw_w2': out['rw_w2'], 'rw_a0': out['rw_a0'], 'rw_a2': out['rw_a2'], 'rw_k_k': out['rw_k_k'], 'rw_k_a': out['rw_k_a'], 'rw_r_k': out['rw_r_k'], 'rw_ln_g': out['rw_ln_g'], 'rw_ln_b': out['rw_ln_b'], 'w_out': out['w_out'], 'norm_post_g': out['norm_post_g'], 'loss_target': out['loss_target'], 'm_norm_pre_g': out['m_norm_pre_g'], 'm_w_in': out['m_w_in'], 'm_mla_q_norm_g': out['m_mla_q_norm_g'], 'm_mla_w_uq': out['m_mla_w_uq'], 'm_mla_kv_norm_g': out['m_mla_kv_norm_g'], 'm_mla_w_ukv': out['m_mla_w_ukv'], 'm_rw_mu': out['m_rw_mu'], 'm_rw_w0': out['m_rw_w0'], 'm_rw_w2': out['m_rw_w2'], 'm_rw_a0': out['m_rw_a0'], 'm_rw_a2': out['m_rw_a2'], 'm_rw_k_k': out['m_rw_k_k'], 'm_rw_k_a': out['m_rw_k_a'], 'm_rw_r_k': out['m_rw_r_k'], 'm_rw_ln_g': out['m_rw_ln_g'], 'm_rw_ln_b': out['m_rw_ln_b'], 'm_w_out': out['m_w_out'], 'm_norm_post_g': out['m_norm_post_g'], 'v_norm_pre_g': out['v_norm_pre_g'], 'v_w_in': out['v_w_in'], 'v_mla_q_norm_g': out['v_mla_q_norm_g'], 'v_mla_w_uq': out['v_mla_w_uq'], 'v_mla_kv_norm_g': out['v_mla_kv_norm_g'], 'v_mla_w_ukv': out['v_mla_w_ukv'], 'v_rw_mu': out['v_rw_mu'], 'v_rw_w0': out['v_rw_w0'], 'v_rw_w2': out['v_rw_w2'], 'v_rw_a0': out['v_rw_a0'], 'v_rw_a2': out['v_rw_a2'], 'v_rw_k_k': out['v_rw_k_k'], 'v_rw_k_a': out['v_rw_k_a'], 'v_rw_r_k': out['v_rw_r_k'], 'v_rw_ln_g': out['v_rw_ln_g'], 'v_rw_ln_b': out['v_rw_ln_b'], 'v_w_out': out['v_w_out'], 'v_norm_post_g': out['v_norm_post_g']}


def _loss(weights, diff, rest, loss_target):
    with _jax.named_scope("forward"):
        args = {**rest, TWIN_DIFF_INPUT: diff, **{k: w.astype(_WEIGHT_DTYPES[k]) for k, w in weights.items()}}
        y = _forward(args)
    with _jax.named_scope("loss_head"):
        err = _jnp.square(y.astype(_jnp.float32) - loss_target)
        return 0.5 * _jnp.sum(_jnp.mean(err, axis=-1)) if err.ndim else 0.5 * err


def _adamw(w, g, m, v):
    m = ADAM_B1 * m + (1.0 - ADAM_B1) * g
    v = ADAM_B2 * v + (1.0 - ADAM_B2) * _jnp.square(g)
    m_hat = m / (1.0 - ADAM_B1 ** ADAM_STEP)
    v_hat = v / (1.0 - ADAM_B2 ** ADAM_STEP)
    delta = -ADAM_LR * (m_hat / (_jnp.sqrt(v_hat) + ADAM_EPS) + ADAM_WD * w)
    return delta, m, v


def reference(x, positions, norm_pre_g, w_in, mla_q_norm_g, mla_w_uq, mla_kv_norm_g, mla_w_ukv, rw_mu, rw_w0, rw_w2, rw_a0, rw_a2, rw_k_k, rw_k_a, rw_r_k, rw_ln_g, rw_ln_b, w_out, norm_post_g, loss_target, m_norm_pre_g, m_w_in, m_mla_q_norm_g, m_mla_w_uq, m_mla_kv_norm_g, m_mla_w_ukv, m_rw_mu, m_rw_w0, m_rw_w2, m_rw_a0, m_rw_a2, m_rw_k_k, m_rw_k_a, m_rw_r_k, m_rw_ln_g, m_rw_ln_b, m_w_out, m_norm_post_g, v_norm_pre_g, v_w_in, v_mla_q_norm_g, v_mla_w_uq, v_mla_kv_norm_g, v_mla_w_ukv, v_rw_mu, v_rw_w0, v_rw_w2, v_rw_a0, v_rw_a2, v_rw_k_k, v_rw_k_a, v_rw_r_k, v_rw_ln_g, v_rw_ln_b, v_w_out, v_norm_post_g):
    given = dict(x=x, positions=positions, norm_pre_g=norm_pre_g, w_in=w_in, mla_q_norm_g=mla_q_norm_g, mla_w_uq=mla_w_uq, mla_kv_norm_g=mla_kv_norm_g, mla_w_ukv=mla_w_ukv, rw_mu=rw_mu, rw_w0=rw_w0, rw_w2=rw_w2, rw_a0=rw_a0, rw_a2=rw_a2, rw_k_k=rw_k_k, rw_k_a=rw_k_a, rw_r_k=rw_r_k, rw_ln_g=rw_ln_g, rw_ln_b=rw_ln_b, w_out=w_out, norm_post_g=norm_post_g, loss_target=loss_target, m_norm_pre_g=m_norm_pre_g, m_w_in=m_w_in, m_mla_q_norm_g=m_mla_q_norm_g, m_mla_w_uq=m_mla_w_uq, m_mla_kv_norm_g=m_mla_kv_norm_g, m_mla_w_ukv=m_mla_w_ukv, m_rw_mu=m_rw_mu, m_rw_w0=m_rw_w0, m_rw_w2=m_rw_w2, m_rw_a0=m_rw_a0, m_rw_a2=m_rw_a2, m_rw_k_k=m_rw_k_k, m_rw_k_a=m_rw_k_a, m_rw_r_k=m_rw_r_k, m_rw_ln_g=m_rw_ln_g, m_rw_ln_b=m_rw_ln_b, m_w_out=m_w_out, m_norm_post_g=m_norm_post_g, v_norm_pre_g=v_norm_pre_g, v_w_in=v_w_in, v_mla_q_norm_g=v_mla_q_norm_g, v_mla_w_uq=v_mla_w_uq, v_mla_kv_norm_g=v_mla_kv_norm_g, v_mla_w_ukv=v_mla_w_ukv, v_rw_mu=v_rw_mu, v_rw_w0=v_rw_w0, v_rw_w2=v_rw_w2, v_rw_a0=v_rw_a0, v_rw_a2=v_rw_a2, v_rw_k_k=v_rw_k_k, v_rw_k_a=v_rw_k_a, v_rw_r_k=v_rw_r_k, v_rw_ln_g=v_rw_ln_g, v_rw_ln_b=v_rw_ln_b, v_w_out=v_w_out, v_norm_post_g=v_norm_post_g)
    weights = {n: given[n] for n in TWIN_WEIGHTS}
    shared = {n: given[n] for n in SHARED_INPUTS}
    per_example = {n: given[n] for n in ['x', 'positions']}
    grad_fn = _jax.value_and_grad(_loss, argnums=(0, 1))

    def one_microbatch(ex, loss_target):
        ex = dict(ex)
        diff = ex.pop(TWIN_DIFF_INPUT)
        return grad_fn(weights, diff, {**shared, **ex}, loss_target)

    if N_MICROBATCH == 1:
        loss, (grad_w, grad_x) = one_microbatch(per_example, given["loss_target"])
    else:
        def body(carry, xs):
            loss_sum, grad_sum = carry
            l_k, (gw_k, gx_k) = one_microbatch(xs[0], xs[1])
            with _jax.named_scope("update"):
                return (loss_sum + l_k, _jax.tree.map(_jnp.add, grad_sum, gw_k)), gx_k

        init = (_jnp.zeros((), _jnp.float32), _jax.tree.map(_jnp.zeros_like, weights))
        (loss, grad_w), grad_x = _jax.lax.scan(body, init, (per_example, given["loss_target"]))
    with _jax.named_scope("update"):
        delta_w, new_m, new_v = {}, {}, {}
        for n in TWIN_WEIGHTS:
            delta_w[n], new_m[n], new_v[n] = _adamw(weights[n], grad_w[n], given["m_" + n], given["v_" + n])
    return (loss, grad_x, *[grad_w[n] for n in TWIN_WEIGHTS], *[delta_w[n] for n in TWIN_WEIGHTS],
            *[new_m[n] for n in TWIN_WEIGHTS], *[new_v[n] for n in TWIN_WEIGHTS])
```

```python
import functools

import numpy as np
import jax
import jax.numpy as jnp
from jax import lax
from jax.experimental import pallas as pl
from jax.experimental.pallas import tpu as pltpu

F32, BF16 = jnp.float32, jnp.bfloat16
HIGHEST = lax.Precision.HIGHEST
MESH = pl.DeviceIdType.MESH

D = 1024
HEADS = 4
RW = 512
NORM_EPS = 1e-6
GN_EPS = 64e-5
ROPE_THETA = 10000.0
SCALE = (128 + 64) ** -0.5
D_IN = 3136
LR, B1, B2, ADAM_EPS, WD, STEP = 0.001, 0.9, 0.999, 1e-08, 0.01, 10

Z0, CQ0, CKV0, RW0, DP = 0, 1024, 1280, 1408, 3200
NRW = DP - RW0

LANES = 128
SUBLANES = 8
VMEM_LIMIT = 56 * 1024 * 1024

TT = 256
TQ = 256
TC = 256
CH = 64

N_SHARD = 4
PACK_ROWS = (1024 * 784 // 128, 256 * 192 // 128, 128 * 256 // 128, 64, 64, 256 * 1024 // 128)
PACK_TOTAL = sum(PACK_ROWS)
HALF = PACK_TOTAL // 2
SMALL_ROWS = 64


def _cparams(sem=None):
    return pltpu.CompilerParams(dimension_semantics=sem, vmem_limit_bytes=VMEM_LIMIT)


def _full(shape):
    n = len(shape)
    return pl.BlockSpec(shape, lambda *_: (0,) * n)


def _dot(a, b):
    return jnp.dot(a, b, preferred_element_type=F32)


def _dot_nt(a, b):
    return lax.dot_general(a, b, (((1,), (1,)), ((), ())), preferred_element_type=F32)


def _dot_tn(a, b):
    return lax.dot_general(a, b, (((0,), (0,)), ((), ())), preferred_element_type=F32)


def _seg(x, bo):
    parts = [jnp.dot(x[:, LANES * i:LANES * (i + 1)], bo, precision=HIGHEST, preferred_element_type=F32)
             for i in range(x.shape[1] // LANES)]
    return parts[0] if len(parts) == 1 else jnp.concatenate(parts, axis=1)


def _rms(x, g, n):
    rstd = lax.rsqrt(jnp.sum(x * x, axis=-1, keepdims=True) * (1.0 / n) + NORM_EPS)
    nx = x * rstd
    return nx * g, nx, rstd


def _rms_bwd(dy, nx, rstd, g, n):
    dn = dy * g
    dx = rstd * (dn - nx * (jnp.sum(dn * nx, axis=-1, keepdims=True) * (1.0 / n)))
    return dx, jnp.sum(dy * nx, axis=0, keepdims=True)


def _rot(x):
    lane = lax.broadcasted_iota(jnp.int32, x.shape, 1)
    return jnp.where((lane % 64) < 32, -pltpu.roll(x, x.shape[1] - 32, 1), pltpu.roll(x, 32, 1))


def _sigmoid(x):
    return 1.0 / (1.0 + jnp.exp(-x))


def _softplus(x):
    return jnp.maximum(x, 0.0) + jnp.log(1.0 + jnp.exp(-jnp.abs(x)))


def _rw_gates(ps, w0, w2p, a0, a2p, k_k, k_a, bo):
    r, k, v, misc = ps[:, 0:512], ps[:, 512:1024], ps[:, 1024:1536], ps[:, 1536:NRW]
    th = jnp.tanh(misc)
    wpre = w0 + _dot(th.astype(BF16), w2p)
    e = jnp.exp(-_softplus(-wpre) - 0.5)
    w = jnp.exp(-e)
    a = _sigmoid(a0 + _dot(misc.astype(BF16), a2p))
    m = k * k_k
    nrm = jnp.maximum(jnp.sqrt(_seg(m * m, bo)), 1e-12)
    kk = m / nrm
    kp = k * (1.0 + (a - 1.0) * k_a)
    return dict(r=r, k=k, v=v, misc=misc, th=th, wpre=wpre, e=e, w=w, a=a, nrm=nrm, kk=kk, kp=kp)


def _shift_mix(prw, prev_row, mu):
    row = lax.broadcasted_iota(jnp.int32, prw.shape, 0)
    sh = jnp.where(row == 0, prev_row, pltpu.roll(prw, 1, 0))
    return prw + (sh - prw) * mu, sh


def _ag_weights(shards):
    n = len(shards)

    def body(*refs):
        ins, outs = refs[:n], refs[n:2 * n]
        send_sems, recv_sems = refs[2 * n], refs[2 * n + 1]
        x, y, c = lax.axis_index("x"), lax.axis_index("y"), lax.axis_index("c")
        mine = 2 * x + y
        for w in range(n):
            outs[w][mine] = ins[w][...].astype(BF16)
        flips = ((1, 0), (0, 1), (1, 1))

        def copy(w, k):
            fx, fy = flips[k]
            return pltpu.make_async_remote_copy(
                src_ref=outs[w].at[mine], dst_ref=outs[w].at[mine],
                send_sem=send_sems.at[w * 3 + k], recv_sem=recv_sems.at[w * 3 + k],
                device_id=(x ^ fx, y ^ fy, c), device_id_type=MESH)

        def arrival(w, k):
            fx, fy = flips[k]
            theirs = 2 * (x ^ fx) + (y ^ fy)
            return pltpu.make_async_remote_copy(
                src_ref=outs[w].at[theirs], dst_ref=outs[w].at[theirs],
                send_sem=send_sems.at[w * 3 + k], recv_sem=recv_sems.at[w * 3 + k],
                device_id=(x ^ fx, y ^ fy, c), device_id_type=MESH)

        for w in range(n):
            for k in range(3):
                copy(w, k).start()
        for w in range(n):
            for k in range(3):
                arrival(w, k).wait_recv()
        for w in range(n):
            for k in range(3):
                copy(w, k).wait_send()

    vm = pl.BlockSpec(memory_space=pltpu.VMEM)
    return pl.pallas_call(
        body, name="ag_weights",
        out_shape=[jax.ShapeDtypeStruct((N_SHARD,) + s.shape, BF16) for s in shards],
        in_specs=[vm] * n, out_specs=[vm] * n,
        scratch_shapes=[pltpu.SemaphoreType.DMA((3 * n,)), pltpu.SemaphoreType.DMA((3 * n,))],
        compiler_params=pltpu.CompilerParams(vmem_limit_bytes=VMEM_LIMIT),
    )(*shards)


def _rs_pair_exchange(send_half):
    def body(src_ref, dst_ref, send_sem, recv_sem):
        x, y, c = lax.axis_index("x"), lax.axis_index("y"), lax.axis_index("c")
        cp = pltpu.make_async_remote_copy(src_ref=src_ref, dst_ref=dst_ref, send_sem=send_sem, recv_sem=recv_sem,
                                          device_id=(x, y, 1 - c), device_id_type=MESH)
        cp.start()
        cp.wait()

    hbm = pl.BlockSpec(memory_space=pl.ANY)
    return pl.pallas_call(
        body, name="rs_pair_exchange",
        out_shape=jax.ShapeDtypeStruct(send_half.shape, send_half.dtype),
        in_specs=[hbm], out_specs=hbm,
        scratch_shapes=[pltpu.SemaphoreType.DMA, pltpu.SemaphoreType.DMA],
    )(send_half)


def _rs_chip_exchange(part):
    def body(src_ref, dst_ref, send_sems, recv_sems):
        x, y, c = lax.axis_index("x"), lax.axis_index("y"), lax.axis_index("c")
        flips = ((1, 0), (0, 1), (1, 1))
        cps = []
        for k, (fx, fy) in enumerate(flips):
            theirs = 2 * (x ^ fx) + (y ^ fy)
            cps.append(pltpu.make_async_remote_copy(
                src_ref=src_ref.at[theirs], dst_ref=dst_ref.at[k],
                send_sem=send_sems.at[k], recv_sem=recv_sems.at[k],
                device_id=(x ^ fx, y ^ fy, c), device_id_type=MESH))
        for cp in cps:
            cp.start()
        for cp in cps:
            cp.wait()

    hbm = pl.BlockSpec(memory_space=pl.ANY)
    return pl.pallas_call(
        body, name="rs_chip_exchange",
        out_shape=jax.ShapeDtypeStruct((3,) + part.shape[1:], part.dtype),
        in_specs=[hbm], out_specs=hbm,
        scratch_shapes=[pltpu.SemaphoreType.DMA((3,)), pltpu.SemaphoreType.DMA((3,))],
    )(part)


def _rs_pair_gather(half):
    def body(src_ref, dst_ref, send_sem, recv_sem, local_sem):
        x, y, c = lax.axis_index("x"), lax.axis_index("y"), lax.axis_index("c")
        own = pltpu.make_async_copy(src_ref, dst_ref.at[c], local_sem)
        own.start()
        cp = pltpu.make_async_remote_copy(src_ref=src_ref, dst_ref=dst_ref.at[c], send_sem=send_sem, recv_sem=recv_sem,
                                          device_id=(x, y, 1 - c), device_id_type=MESH)
        cp.start()
        arrival = pltpu.make_async_remote_copy(src_ref=src_ref, dst_ref=dst_ref.at[1 - c], send_sem=send_sem,
                                               recv_sem=recv_sem, device_id=(x, y, 1 - c), device_id_type=MESH)
        arrival.wait_recv()
        cp.wait_send()
        own.wait()

    hbm = pl.BlockSpec(memory_space=pl.ANY)
    return pl.pallas_call(
        body, name="rs_pair_gather",
        out_shape=jax.ShapeDtypeStruct((2,) + half.shape, half.dtype),
        in_specs=[hbm], out_specs=hbm,
        scratch_shapes=[pltpu.SemaphoreType.DMA, pltpu.SemaphoreType.DMA, pltpu.SemaphoreType.DMA],
    )(half)


def _small_allreduce(vec):
    def body(in_ref, out_ref, recv, send_sems, recv_sems):
        x, y, c = lax.axis_index("x"), lax.axis_index("y"), lax.axis_index("c")
        me = 4 * x + 2 * y + c
        cps = []
        for k in range(1, 8):
            fx, fy, fc = (k >> 2) & 1, (k >> 1) & 1, k & 1
            cps.append(pltpu.make_async_remote_copy(
                src_ref=in_ref, dst_ref=recv.at[k - 1],
                send_sem=send_sems.at[k - 1], recv_sem=recv_sems.at[k - 1],
                device_id=(x ^ fx, y ^ fy, c ^ fc), device_id_type=MESH))
        for cp in cps:
            cp.start()
        for cp in cps:
            cp.wait()
        acc = jnp.zeros(in_ref.shape, F32)
        for j in range(8):
            slot = jnp.maximum((me ^ j) - 1, 0)
            acc = acc + jnp.where(me == j, in_ref[...], recv[slot])
        out_ref[...] = acc

    vm = pl.BlockSpec(memory_space=pltpu.VMEM)
    return pl.pallas_call(
        body, name="small_allreduce",
        out_shape=jax.ShapeDtypeStruct(vec.shape, F32),
        in_specs=[vm], out_specs=vm,
        scratch_shapes=[pltpu.VMEM((7,) + vec.shape, F32), pltpu.SemaphoreType.DMA((7,)),
                        pltpu.SemaphoreType.DMA((7,))],
    )(vec)


def _add_n(arrs, name, rows=568):
    n = len(arrs)
    r = arrs[0].shape[0]

    def body(*refs):
        acc = refs[0][...]
        for k in range(1, n):
            acc = acc + refs[k][...]
        refs[n][...] = acc

    spec = pl.BlockSpec((rows, LANES), lambda i: (i, 0))
    return pl.pallas_call(
        body, name=name, grid=(r // rows,),
        out_shape=jax.ShapeDtypeStruct(arrs[0].shape, F32),
        in_specs=[spec] * n, out_specs=spec,
        compiler_params=_cparams(("parallel",)),
    )(*arrs)


def _adamw(w, g, m, v, name, rows):
    r = w.shape[0]

    def body(w_ref, g_ref, m_ref, v_ref, d_ref, nm_ref, nv_ref):
        gg = g_ref[...]
        nm = B1 * m_ref[...] + (1.0 - B1) * gg
        nv = B2 * v_ref[...] + (1.0 - B2) * (gg * gg)
        m_hat = nm / (1.0 - B1 ** STEP)
        v_hat = nv / (1.0 - B2 ** STEP)
        d_ref[...] = -LR * (m_hat / (jnp.sqrt(v_hat) + ADAM_EPS) + WD * w_ref[...])
        nm_ref[...] = nm
        nv_ref[...] = nv

    spec = pl.BlockSpec((rows, LANES), lambda i: (i, 0))
    sds = jax.ShapeDtypeStruct(w.shape, F32)
    return pl.pallas_call(
        body, name=name, grid=(r // rows,),
        out_shape=[sds, sds, sds],
        in_specs=[spec] * 4, out_specs=[spec] * 3,
        compiler_params=_cparams(("parallel",)),
    )(w, g, m, v)


def _pre_fwd(x, pos, invf, gpre, wp, gq, wuq, gkv, wukv, mu, w0, w2p, a0, a2p, k_k, k_a, bo):
    bsz, t, _ = x.shape
    nt = t // TT

    def body(x_ref, pos_ref, invf_ref, gpre_ref, wp_ref, gq_ref, wuq_ref, gkv_ref, wukv_ref, mu_ref, w0_ref,
             w2p_ref, a0_ref, a2p_ref, kk_ref, ka_ref, bo_ref,
             u_ref, pp_ref, q_ref, k_ref, v_ref, r_o, w_o, kp_o, vv_o, al_o, be_o, carry):
        i = pl.program_id(1)
        u, _, _ = _rms(x_ref[0], gpre_ref[...], D)
        ub = u.astype(BF16)
        u_ref[0] = ub
        p = _dot(ub, wp_ref[...])
        pp_ref[0] = p
        prw = p[:, RW0:DP]

        @pl.when(i == 0)
        def _():
            carry[...] = jnp.zeros(carry.shape, F32)

        ps, _ = _shift_mix(prw, carry[7:8, :], mu_ref[...])
        carry[...] = prw[TT - 8:TT, :]

        g = _rw_gates(ps, w0_ref[...], w2p_ref[...], a0_ref[...], a2p_ref[...], kk_ref[...], ka_ref[...],
                      bo_ref[...])
        r_o[0] = g["r"]
        w_o[0] = g["w"]
        kp_o[0] = g["kp"]
        vv_o[0] = g["v"]
        al_o[0] = -g["kk"]
        be_o[0] = g["kk"] * g["a"]

        cqn, _, _ = _rms(p[:, CQ0:CQ0 + 256], gq_ref[...], 256)
        q = _dot(cqn.astype(BF16), wuq_ref[...])
        ckvn, _, _ = _rms(p[:, CKV0:CKV0 + 128], gkv_ref[...], 128)
        kv = _dot(ckvn.astype(BF16), wukv_ref[...])
        ang = pos_ref[0] * invf_ref[...]
        cs, sn = jnp.cos(ang), jnp.sin(ang)
        lane = lax.broadcasted_iota(jnp.int32, cs.shape, 1)
        kr = ps[:, 1536:1536 + LANES]
        kr = jnp.where(lane < 64, kr * cs + _rot(kr) * sn, 0.0).astype(BF16)
        for h in range(HEADS):
            qr = q[:, 256 * h + 128:256 * h + 256]
            q_ref[0, :, 256 * h:256 * h + 128] = q[:, 256 * h:256 * h + 128].astype(BF16)
            q_ref[0, :, 256 * h + 128:256 * h + 256] = (qr * cs + _rot(qr) * sn).astype(BF16)
            k_ref[0, :, 256 * h:256 * h + 128] = kv[:, 128 * h:128 * h + 128].astype(BF16)
            k_ref[0, :, 256 * h + 128:256 * h + 256] = kr
        v_ref[0] = kv[:, 512:1024].astype(BF16)

    tok = lambda c: pl.BlockSpec((1, TT, c), lambda b, i: (b, i, 0))
    full = lambda a: _full(a.shape)
    ins = (x, pos, invf, gpre, wp, gq, wuq, gkv, wukv, mu, w0, w2p, a0, a2p, k_k, k_a, bo)
    in_specs = [tok(D), tok(1)] + [full(a) for a in ins[2:]]
    sd = lambda c, dt: jax.ShapeDtypeStruct((bsz, t, c), dt)
    out_shape = [sd(D, BF16), sd(DP, F32), sd(1024, BF16), sd(1024, BF16), sd(512, BF16)] + [sd(RW, F32)] * 6
    out_specs = [tok(D), tok(DP), tok(1024), tok(1024), tok(512)] + [tok(RW)] * 6
    return pl.pallas_call(
        body, name="pre_fwd", grid=(bsz, nt), out_shape=out_shape, in_specs=in_specs, out_specs=out_specs,
        scratch_shapes=[pltpu.VMEM((8, NRW), F32)],
        compiler_params=_cparams(("arbitrary", "arbitrary")),
    )(*ins)


def _attn_fwd(q, k, v):
    bsz, t, _ = q.shape
    nq = t // TQ

    def body(q_ref, k_ref, v_ref, o_ref, lse_ref):
        i = pl.program_id(2)
        qt = q_ref[0]
        row = lax.broadcasted_iota(jnp.int32, (TQ, TQ), 0) + i * TQ
        col0 = lax.broadcasted_iota(jnp.int32, (TQ, TQ), 1)

        def step(j, carry):
            m, l, acc = carry
            at = pl.ds(pl.multiple_of(j * TQ, TQ), TQ)
            s = _dot_nt(qt, k_ref[0, at, :]) * SCALE
            s = jnp.where(col0 + j * TQ <= row, s, -1e30)
            mn = jnp.maximum(m, jnp.max(s, axis=1, keepdims=True))
            p = jnp.exp(s - mn)
            al = jnp.exp(m - mn)
            l = al * l + jnp.sum(p, axis=1, keepdims=True)
            acc = al * acc + _dot(p.astype(BF16), v_ref[0, at, :])
            return mn, l, acc

        m, l, acc = lax.fori_loop(
            0, i + 1, step,
            (jnp.full((TQ, 1), -1e30, F32), jnp.zeros((TQ, 1), F32), jnp.zeros((TQ, LANES), F32)))
        o_ref[0] = acc / l
        lse_ref[0, 0] = jnp.broadcast_to(m + jnp.log(l), (TQ, LANES))

    return pl.pallas_call(
        body, name="attn_fwd", grid=(bsz, HEADS, nq),
        out_shape=[jax.ShapeDtypeStruct((bsz, t, 512), F32), jax.ShapeDtypeStruct((bsz, HEADS, t, LANES), F32)],
        in_specs=[pl.BlockSpec((1, TQ, 256), lambda b, h, i: (b, i, h)),
                  pl.BlockSpec((1, t, 256), lambda b, h, i: (b, 0, h)),
                  pl.BlockSpec((1, t, LANES), lambda b, h, i: (b, 0, h))],
        out_specs=[pl.BlockSpec((1, TQ, LANES), lambda b, h, i: (b, i, h)),
                   pl.BlockSpec((1, 1, TQ, LANES), lambda b, h, i: (b, h, i, 0))],
        compiler_params=_cparams(("parallel", "parallel", "arbitrary")),
    )(q, k, v)


def _attn_bwd(q, k, v, o, lse, do):
    bsz, t, _ = q.shape
    nq = t // TQ

    def body(q_ref, k_ref, v_ref, o_ref, lse_ref, do_ref, dq_ref, dk_ref, dv_ref, dl_ref):
        def prep(i, _):
            at = pl.ds(pl.multiple_of(i * TQ, TQ), TQ)
            dl_ref[at, :] = jnp.broadcast_to(jnp.sum(do_ref[0, at, :] * o_ref[0, at, :], axis=1, keepdims=True),
                                             (TQ, LANES))
            return 0

        lax.fori_loop(0, nq, prep, 0)
        dq_ref[0] = jnp.zeros((t, 256), F32)
        row0 = lax.broadcasted_iota(jnp.int32, (TQ, TQ), 0)
        col0 = lax.broadcasted_iota(jnp.int32, (TQ, TQ), 1)

        def kv_tile(j, _):
            atk = pl.ds(pl.multiple_of(j * TQ, TQ), TQ)
            kt = k_ref[0, atk, :]
            vt = v_ref[0, atk, :]

            def q_tile(i, carry):
                dk, dv = carry
                atq = pl.ds(pl.multiple_of(i * TQ, TQ), TQ)
                qt = q_ref[0, atq, :]
                dob = do_ref[0, atq, :].astype(BF16)
                s = _dot_nt(qt, kt) * SCALE
                s = jnp.where(col0 + j * TQ <= row0 + i * TQ, s, -1e30)
                p = jnp.exp(s - lse_ref[0, 0, atq, :][:, 0:1])
                dv = dv + _dot_tn(p.astype(BF16), dob)
                dp = _dot_nt(dob, vt)
                ds = (p * (dp - dl_ref[atq, :][:, 0:1]) * SCALE).astype(BF16)
                dk = dk + _dot_tn(ds, qt)
                dq_ref[0, atq, :] += _dot(ds, kt)
                return dk, dv

            dk, dv = lax.fori_loop(j, nq, q_tile, (jnp.zeros((TQ, 256), F32), jnp.zeros((TQ, LANES), F32)))
            dk_ref[0, atk, :] = dk
            dv_ref[0, atk, :] = dv
            return 0

        lax.fori_loop(0, nq, kv_tile, 0)

    s256 = pl.BlockSpec((1, t, 256), lambda b, h: (b, 0, h))
    s128 = pl.BlockSpec((1, t, LANES), lambda b, h: (b, 0, h))
    return pl.pallas_call(
        body, name="attn_bwd", grid=(bsz, HEADS),
        out_shape=[jax.ShapeDtypeStruct((bsz, t, 1024), F32), jax.ShapeDtypeStruct((bsz, t, 1024), F32),
                   jax.ShapeDtypeStruct((bsz, t, 512), F32)],
        in_specs=[s256, s256, s128, s128, pl.BlockSpec((1, 1, t, LANES), lambda b, h: (b, h, 0, 0)), s128],
        out_specs=[s256, s256, s128],
        scratch_shapes=[pltpu.VMEM((t, LANES), F32)],
        compiler_params=_cparams(("parallel", "parallel")),
    )(q, k, v, o, lse, do)


def _rowb(tile, j, p):
    return jnp.broadcast_to(tile[j:j + 1, LANES * p:LANES * (p + 1)], (64, LANES))


def _wkv_fwd(r, w, kp, v, al, be, bo, eye):
    bsz, t, _ = r.shape
    nt = t // TC
    npair = RW // LANES

    def body(r_ref, w_ref, kp_ref, v_ref, al_ref, be_ref, bo_ref, eye_ref, y_ref, sck_ref, s_ref):
        i = pl.program_id(1)

        @pl.when(i == 0)
        def _():
            s_ref[...] = jnp.zeros(s_ref.shape, F32)

        bo_m = bo_ref[...]
        eye_m = eye_ref[...]
        colsum = lambda a: jnp.dot(a, bo_m, precision=HIGHEST, preferred_element_type=F32)

        def group(g, _):
            at = pl.ds(pl.multiple_of(g * 8, 8), 8)
            rt, wt, kt, vt, at_, bt = (ref[0, at, :] for ref in (r_ref, w_ref, kp_ref, v_ref, al_ref, be_ref))

            @pl.when(g % (CH // 8) == 0)
            def _():
                sck_ref[0, g // (CH // 8)] = s_ref[...]

            for p in range(npair):
                s = s_ref[p]
                rows = []
                for j in range(8):
                    sa = colsum(s * _rowb(at_, j, p))
                    vc = colsum(_rowb(vt, j, p) * eye_m)
                    s = s * _rowb(wt, j, p) + sa * _rowb(bt, j, p) + vc * _rowb(kt, j, p)
                    yc = colsum(s * _rowb(rt, j, p))
                    rows.append(jnp.sum(yc * eye_m, axis=0, keepdims=True))
                s_ref[p] = s
                y_ref[0, at, LANES * p:LANES * (p + 1)] = jnp.concatenate(rows, axis=0)
            return 0

        lax.fori_loop(0, TC // 8, group, 0)

    tok = pl.BlockSpec((1, TC, RW), lambda b, i: (b, i, 0))
    return pl.pallas_call(
        body, name="wkv_fwd", grid=(bsz, nt),
        out_shape=[jax.ShapeDtypeStruct((bsz, t, RW), F32),
                   jax.ShapeDtypeStruct((bsz, t // CH, npair, 64, LANES), F32)],
        in_specs=[tok] * 6 + [_full(bo.shape), _full(eye.shape)],
        out_specs=[tok, pl.BlockSpec((1, TC // CH, npair, 64, LANES), lambda b, i: (b, i, 0, 0, 0))],
        scratch_shapes=[pltpu.VMEM((npair, 64, LANES), F32)],
        compiler_params=_cparams(("arbitrary", "arbitrary")),
    )(r, w, kp, v, al, be, bo, eye)


def _wkv_bwd(r, w, kp, v, al, be, dy, sck, bo, eye):
    bsz, t, _ = r.shape
    nt = t // TC
    npair = RW // LANES
    ng = CH // 8

    def body(r_ref, w_ref, kp_ref, v_ref, al_ref, be_ref, dy_ref, sck_ref, bo_ref, eye_ref,
             dr_ref, dw_ref, dkp_ref, dv_ref, dal_ref, dbe_ref, ds_ref, sp_ref, u_ref, vc_ref):
        i = pl.program_id(1)

        @pl.when(i == 0)
        def _():
            ds_ref[...] = jnp.zeros(ds_ref.shape, F32)

        bo_m = bo_ref[...]
        eye_m = eye_ref[...]
        colsum = lambda a: jnp.dot(a, bo_m, precision=HIGHEST, preferred_element_type=F32)
        rowsum = lambda a: jnp.sum(a, axis=0, keepdims=True)

        def chunk(ci, _):
            cc = TC // CH - 1 - ci
            sp_ref[0] = sck_ref[0, cc]

            def recompute(g, _):
                at = pl.ds(pl.multiple_of(cc * CH + g * 8, 8), 8)
                wt, kt, vt, at_, bt = (ref[0, at, :] for ref in (w_ref, kp_ref, v_ref, al_ref, be_ref))
                for p in range(npair):
                    s = sp_ref[g * 8, p]
                    for j in range(8):
                        sa = colsum(s * _rowb(at_, j, p))
                        vc = colsum(_rowb(vt, j, p) * eye_m)
                        u_ref[g * 8 + j, p] = sa
                        vc_ref[g * 8 + j, p] = vc
                        s = s * _rowb(wt, j, p) + sa * _rowb(bt, j, p) + vc * _rowb(kt, j, p)
                        sp_ref[g * 8 + j + 1, p] = s
                return 0

            lax.fori_loop(0, ng, recompute, 0)

            def reverse(gi, _):
                g = ng - 1 - gi
                at = pl.ds(pl.multiple_of(cc * CH + g * 8, 8), 8)
                rt, wt, kt, at_, bt, dyt = (ref[0, at, :] for ref in (r_ref, w_ref, kp_ref, al_ref, be_ref, dy_ref))
                for p in range(npair):
                    ds = ds_ref[p]
                    rows = {n: [None] * 8 for n in ("dr", "dw", "dkp", "dv", "dal", "dbe")}
                    for j in reversed(range(8)):
                        tl = g * 8 + j
                        s_prev = sp_ref[tl, p]
                        s_now = sp_ref[tl + 1, p]
                        dyc = colsum(_rowb(dyt, j, p) * eye_m)
                        ds = ds + dyc * _rowb(rt, j, p)
                        rows["dr"][j] = rowsum(s_now * dyc)
                        rows["dbe"][j] = rowsum(u_ref[tl, p] * ds)
                        rows["dkp"][j] = rowsum(vc_ref[tl, p] * ds)
                        rows["dw"][j] = rowsum(ds * s_prev)
                        cb = colsum(ds * _rowb(bt, j, p))
                        rows["dv"][j] = rowsum(colsum(ds * _rowb(kt, j, p)) * eye_m)
                        rows["dal"][j] = rowsum(s_prev * cb)
                        ds = ds * _rowb(wt, j, p) + cb * _rowb(at_, j, p)
                    ds_ref[p] = ds
                    lanes = slice(LANES * p, LANES * (p + 1))
                    for n, ref in (("dr", dr_ref), ("dw", dw_ref), ("dkp", dkp_ref), ("dv", dv_ref),
                                   ("dal", dal_ref), ("dbe", dbe_ref)):
                        ref[0, at, lanes] = jnp.concatenate(rows[n], axis=0)
                return 0

            lax.fori_loop(0, ng, reverse, 0)
            return 0

        lax.fori_loop(0, TC // CH, chunk, 0)

    tok = pl.BlockSpec((1, TC, RW), lambda b, i: (b, nt - 1 - i, 0))
    sd = jax.ShapeDtypeStruct((bsz, t, RW), F32)
    st = pltpu.VMEM((CH, npair, 64, LANES), F32)
    return pl.pallas_call(
        body, name="wkv_bwd", grid=(bsz, nt),
        out_shape=[sd] * 6,
        in_specs=[tok] * 7 + [pl.BlockSpec((1, TC // CH, npair, 64, LANES), lambda b, i: (b, nt - 1 - i, 0, 0, 0)),
                              _full(bo.shape), _full(eye.shape)],
        out_specs=[tok] * 6,
        scratch_shapes=[pltpu.VMEM((npair, 64, LANES), F32), pltpu.VMEM((CH + 1, npair, 64, LANES), F32), st, st],
        compiler_params=_cparams(("arbitrary", "arbitrary")),
    )(r, w, kp, v, al, be, dy, sck, bo, eye)


def _post(x, tgt, pp, o, yw, r, kp, v, ln_g, ln_b, r_k, wo, wot, gpost, bo):
    bsz, t, _ = x.shape
    nt = t // TT

    def body(x_ref, tgt_ref, z_ref, o_ref, yw_ref, r_ref, kp_ref, v_ref, lng_ref, lnb_ref, rk_ref, wo_ref, wot_ref,
             gpost_ref, bo_ref,
             dh_ref, dz_ref, dym_ref, dyw_ref, dbon_ref, loss_ref, dwo_ref, dgpost_ref, dlng_ref, dlnb_ref, drk_ref):
        first = (pl.program_id(0) == 0) & (pl.program_id(1) == 0)

        @pl.when(first)
        def _():
            for ref in (loss_ref, dwo_ref, dgpost_ref, dlng_ref, dlnb_ref, drk_ref):
                ref[...] = jnp.zeros(ref.shape, F32)

        bo_m = bo_ref[...]
        seg = lambda a: _seg(a, bo_m)
        rowsum = lambda a: jnp.sum(a, axis=0, keepdims=True)
        ywv, rv, kpv, vv = yw_ref[0], r_ref[0], kp_ref[0], v_ref[0]
        ln_g, r_k = lng_ref[...], rk_ref[...]
        mean = seg(ywv) * (1.0 / 64)
        yc = ywv - mean
        rstd = lax.rsqrt(seg(yc * yc) * (1.0 / 64) + GN_EPS)
        yhat = yc * rstd
        sb = seg(rv * kpv * r_k)
        y_rw = yhat * ln_g + lnb_ref[...] + sb * vv
        z = z_ref[0]
        sig = _sigmoid(z)
        sz = z * sig
        ycat = jnp.concatenate([o_ref[0], y_rw], axis=1)
        ycg = (ycat * sz).astype(BF16)
        out = _dot(ycg, wo_ref[...])
        hn, nx, rstd_o = _rms(out, gpost_ref[...], D)
        err = x_ref[0] + hn - tgt_ref[0]
        loss_ref[...] += jnp.sum(err * err) * (0.5 / D)
        dh = err * (1.0 / D)
        dh_ref[0] = dh
        dout, dgp = _rms_bwd(dh, nx, rstd_o, gpost_ref[...], D)
        dgpost_ref[...] += dgp
        doutb = dout.astype(BF16)
        dwo_ref[...] += _dot_tn(ycg, doutb)
        dycg = _dot(doutb, wot_ref[...])
        dz_ref[0] = dycg * ycat * (sig * (1.0 + z * (1.0 - sig)))
        dycat = dycg * sz
        dym_ref[0] = dycat[:, 0:512]
        dy_rw = dycat[:, 512:1024]
        dlnb_ref[...] += rowsum(dy_rw)
        dlng_ref[...] += rowsum(dy_rw * yhat)
        dyhat = dy_rw * ln_g
        dyw_ref[0] = rstd * (dyhat - seg(dyhat) * (1.0 / 64) - yhat * (seg(dyhat * yhat) * (1.0 / 64)))
        dsb = seg(dy_rw * vv)
        drk_ref[...] += rowsum(dsb * rv * kpv)
        dbon_ref[0, :, 0:512] = dsb * kpv * r_k
        dbon_ref[0, :, 512:1024] = dsb * rv * r_k
        dbon_ref[0, :, 1024:1536] = dy_rw * sb

    tok = lambda c: pl.BlockSpec((1, TT, c), lambda b, i: (b, i, 0))
    full = lambda a: _full(a.shape)
    ins = (x, tgt, pp, o, yw, r, kp, v, ln_g, ln_b, r_k, wo, wot, gpost, bo)
    in_specs = [tok(D), tok(D), tok(1024)] + [tok(512)] * 5 + [full(a) for a in ins[8:]]
    sd = lambda c: jax.ShapeDtypeStruct((bsz, t, c), F32)
    vec = lambda c: jax.ShapeDtypeStruct((1, c), F32)
    out_shape = [sd(D), sd(1024), sd(512), sd(512), sd(1536), jax.ShapeDtypeStruct((8, LANES), F32),
                 jax.ShapeDtypeStruct((1024, 1024), F32), vec(D), vec(512), vec(512), vec(512)]
    out_specs = [tok(D), tok(1024), tok(512), tok(512), tok(1536), _full((8, LANES)), _full((1024, 1024)),
                 _full((1, D)), _full((1, 512)), _full((1, 512)), _full((1, 512))]
    return pl.pallas_call(
        body, name="post", grid=(bsz, nt), out_shape=out_shape, in_specs=in_specs, out_specs=out_specs,
        compiler_params=_cparams(("arbitrary", "arbitrary")),
    )(*ins)


def _pre_bwd_a(pp, pos, invf, cqkv_w, mu, w0, w2p, w2pt, a0, a2p, a2pt, k_k, k_a, bo,
               dq, dk, dva, dwkv, dbon):
    gq, wuqt, gkv, wukvt = cqkv_w
    bsz, t, _ = pp.shape
    nt = t // TT
    dr_w, dw_w, dkp_w, dv_w, dal_w, dbe_w = dwkv

    def body(pp_ref, pos_ref, invf_ref, gq_ref, wuqt_ref, gkv_ref, wukvt_ref, mu_ref, w0_ref, w2p_ref, w2pt_ref,
             a0_ref, a2p_ref, a2pt_ref, kk_ref, ka_ref, bo_ref, dq_ref, dk_ref, dva_ref,
             dr_ref, dw_ref, dkp_ref, dv_ref, dal_ref, dbe_ref, dbon_ref,
             da_ref, dwuq_ref, dwukv_ref, dw2p_ref, da2p_ref, dgq_ref, dgkv_ref, dmu_ref, dw0_ref, da0_ref,
             dkk_ref, dka_ref, carry):
        i = pl.program_id(1)
        first = (pl.program_id(0) == 0) & (i == 0)

        @pl.when(first)
        def _():
            for ref in (dwuq_ref, dwukv_ref, dw2p_ref, da2p_ref, dgq_ref, dgkv_ref, dmu_ref, dw0_ref, da0_ref,
                        dkk_ref, dka_ref):
                ref[...] = jnp.zeros(ref.shape, F32)

        bo_m = bo_ref[...]
        rowsum = lambda a: jnp.sum(a, axis=0, keepdims=True)
        prw = pp_ref[0, :, RW0:DP]

        @pl.when(i == 0)
        def _():
            carry[...] = jnp.zeros(carry.shape, F32)

        ps, sh = _shift_mix(prw, carry[7:8, :], mu_ref[...])
        carry[...] = prw[TT - 8:TT, :]
        k_k, k_a = kk_ref[...], ka_ref[...]
        g = _rw_gates(ps, w0_ref[...], w2p_ref[...], a0_ref[...], a2p_ref[...], k_k, k_a, bo_m)
        a, kk, k = g["a"], g["kk"], g["k"]
        dr = dr_ref[0] + dbon_ref[0, :, 0:512]
        dkp = dkp_ref[0] + dbon_ref[0, :, 512:1024]
        dv = dv_ref[0] + dbon_ref[0, :, 1024:1536]
        dbe = dbe_ref[0]
        dkk = dbe * a - dal_ref[0]
        da = dbe * kk + dkp * k * k_a
        dka_ref[...] += rowsum(dkp * k * (a - 1.0))
        dm = (dkk - kk * _seg(dkk * kk, bo_m)) / g["nrm"]
        dkk_ref[...] += rowsum(dm * k)
        dk_tot = dkp * (1.0 + (a - 1.0) * k_a) + dm * k_k
        dapre = da * a * (1.0 - a)
        da0_ref[...] += rowsum(dapre)
        dapb = dapre.astype(BF16)
        da2p_ref[...] += _dot_tn(g["misc"].astype(BF16), dapb)
        dwpre = dw_ref[0] * g["w"] * (-g["e"]) * _sigmoid(-g["wpre"])
        dw0_ref[...] += rowsum(dwpre)
        dwpb = dwpre.astype(BF16)
        th = g["th"]
        dw2p_ref[...] += _dot_tn(th.astype(BF16), dwpb)
        dmisc = _dot(dapb, a2pt_ref[...]) + _dot(dwpb, w2pt_ref[...]) * (1.0 - th * th)
        ang = pos_ref[0] * invf_ref[...]
        cs, sn = jnp.cos(ang), jnp.sin(ang)
        unrope = lambda gr: gr * cs - _rot(gr * sn)
        lane = lax.broadcasted_iota(jnp.int32, cs.shape, 1)
        dkr = dk_ref[0, :, 128:256]
        for h in range(1, HEADS):
            dkr = dkr + dk_ref[0, :, 256 * h + 128:256 * h + 256]
        dkr = jnp.where(lane < 64, unrope(dkr), 0.0)
        dmisc = dmisc + jnp.concatenate([dkr, jnp.zeros_like(dkr)], axis=1)
        dqp = jnp.concatenate(
            [blk for h in range(HEADS)
             for blk in (dq_ref[0, :, 256 * h:256 * h + 128], unrope(dq_ref[0, :, 256 * h + 128:256 * h + 256]))],
            axis=1).astype(BF16)
        dkvp = jnp.concatenate([dk_ref[0, :, 256 * h:256 * h + 128] for h in range(HEADS)] + [dva_ref[0]],
                               axis=1).astype(BF16)
        cqn, cq_nx, cq_rstd = _rms(pp_ref[0, :, CQ0:CQ0 + 256], gq_ref[...], 256)
        ckvn, ckv_nx, ckv_rstd = _rms(pp_ref[0, :, CKV0:CKV0 + 128], gkv_ref[...], 128)
        dwuq_ref[...] += _dot_tn(cqn.astype(BF16), dqp)
        dwukv_ref[...] += _dot_tn(ckvn.astype(BF16), dkvp)
        dcq, dgq = _rms_bwd(_dot(dqp, wuqt_ref[...]), cq_nx, cq_rstd, gq_ref[...], 256)
        dckv, dgkv = _rms_bwd(_dot(dkvp, wukvt_ref[...]), ckv_nx, ckv_rstd, gkv_ref[...], 128)
        dgq_ref[...] += dgq
        dgkv_ref[...] += dgkv
        dps = jnp.concatenate([dr, dk_tot, dv, dmisc], axis=1)
        dmu_ref[...] += rowsum(dps * (sh - prw))
        da_ref[0, :, 0:256] = dcq
        da_ref[0, :, 256:384] = dckv
        da_ref[0, :, 384:384 + NRW] = dps

    tok = lambda c: pl.BlockSpec((1, TT, c), lambda b, i: (b, i, 0))
    full = lambda a: _full(a.shape)
    ins = (pp, pos, invf, gq, wuqt, gkv, wukvt, mu, w0, w2p, w2pt, a0, a2p, a2pt, k_k, k_a, bo,
           dq, dk, dva, dr_w, dw_w, dkp_w, dv_w, dal_w, dbe_w, dbon)
    in_specs = ([tok(DP), tok(1)] + [full(a) for a in ins[2:17]] + [tok(1024), tok(1024), tok(512)]
                + [tok(512)] * 6 + [tok(1536)])
    shp = lambda *s: jax.ShapeDtypeStruct(s, F32)
    out_shape = [shp(bsz, t, 384 + NRW), shp(256, 1024), shp(128, 1024), shp(256, 512), shp(256, 512),
                 shp(1, 256), shp(1, 128), shp(1, NRW), shp(1, 512), shp(1, 512), shp(1, 512), shp(1, 512)]
    out_specs = [tok(384 + NRW)] + [_full(s.shape) for s in out_shape[1:]]
    return pl.pallas_call(
        body, name="pre_bwd_a", grid=(bsz, nt), out_shape=out_shape, in_specs=in_specs, out_specs=out_specs,
        scratch_shapes=[pltpu.VMEM((8, NRW), F32)],
        compiler_params=_cparams(("arbitrary", "arbitrary")),
    )(*ins)


def _pre_bwd_b(x, dh, dz, da, mu, wpt, gpre):
    bsz, t, _ = x.shape
    nt = t // TT
    nblk = t // 8

    def body(x_ref, dh_ref, dz_ref, da_ref, nxt_ref, mu_ref, wpt_ref, gpre_ref, gx_ref, dp_ref, dgpre_ref):
        i = pl.program_id(1)
        first = (pl.program_id(0) == 0) & (i == 0)

        @pl.when(first)
        def _():
            dgpre_ref[...] = jnp.zeros(dgpre_ref.shape, F32)

        mu_v = mu_ref[...]
        dps = da_ref[0, :, 384:384 + NRW]
        nxt = jnp.where(i < nt - 1, nxt_ref[0, 0:1, 384:384 + NRW], 0.0)
        row = lax.broadcasted_iota(jnp.int32, dps.shape, 0)
        up = jnp.where(row == TT - 1, nxt, pltpu.roll(dps, TT - 1, 0))
        dprw = dps * (1.0 - mu_v) + up * mu_v
        dp = jnp.concatenate([dz_ref[0], da_ref[0, :, 0:384], dprw], axis=1).astype(BF16)
        dp_ref[0] = dp
        du = _dot(dp, wpt_ref[...])
        _, nx, rstd = _rms(x_ref[0], gpre_ref[...], D)
        dx, dg = _rms_bwd(du, nx, rstd, gpre_ref[...], D)
        dgpre_ref[...] += dg
        gx_ref[0] = dh_ref[0] + dx

    tok = lambda c: pl.BlockSpec((1, TT, c), lambda b, i: (b, i, 0))
    nxt_spec = pl.BlockSpec((1, 8, 384 + NRW), lambda b, i: (b, jnp.minimum((i + 1) * (TT // 8), nblk - 1), 0))
    ins = (x, dh, dz, da, da, mu, wpt, gpre)
    return pl.pallas_call(
        body, name="pre_bwd_b", grid=(bsz, nt),
        out_shape=[jax.ShapeDtypeStruct((bsz, t, D), F32), jax.ShapeDtypeStruct((bsz, t, DP), BF16),
                   jax.ShapeDtypeStruct((1, D), F32)],
        in_specs=[tok(D), tok(D), tok(1024), tok(384 + NRW), nxt_spec, _full(mu.shape), _full(wpt.shape),
                  _full(gpre.shape)],
        out_specs=[tok(D), tok(DP), _full((1, D))],
        compiler_params=_cparams(("arbitrary", "arbitrary")),
    )(*ins)


def _tn_matmul(a, b, bn, name, bk=512):
    kdim, m = a.shape
    _, n = b.shape
    nk = kdim // bk

    def body(a_ref, b_ref, o_ref):
        @pl.when(pl.program_id(1) == 0)
        def _():
            o_ref[...] = jnp.zeros(o_ref.shape, F32)

        o_ref[...] += _dot_tn(a_ref[...], b_ref[...])

    return pl.pallas_call(
        body, name=name, grid=(n // bn, nk),
        out_shape=jax.ShapeDtypeStruct((m, n), F32),
        in_specs=[pl.BlockSpec((bk, m), lambda j, kk: (kk, 0)), pl.BlockSpec((bk, bn), lambda j, kk: (kk, j))],
        out_specs=pl.BlockSpec((m, bn), lambda j, kk: (0, j)),
        compiler_params=_cparams(("parallel", "arbitrary")),
    )(a, b)


SHARDED = ("w_in", "mla_w_uq", "mla_w_ukv", "rw_w2", "rw_a2", "w_out")
SMALL = ("norm_pre_g", "mla_q_norm_g", "mla_kv_norm_g", "rw_mu", "rw_w0", "rw_a0", "rw_k_k", "rw_k_a", "rw_r_k",
         "rw_ln_g", "rw_ln_b", "norm_post_g")
WEIGHTS = ("norm_pre_g", "w_in", "mla_q_norm_g", "mla_w_uq", "mla_kv_norm_g", "mla_w_ukv", "rw_mu", "rw_w0", "rw_w2",
           "rw_a0", "rw_a2", "rw_k_k", "rw_k_a", "rw_r_k", "rw_ln_g", "rw_ln_b", "w_out", "norm_post_g")


def _pack_small(d):
    flat = jnp.concatenate([d[n].reshape(1, -1) for n in SMALL], axis=1)
    return jnp.pad(flat, ((0, 0), (0, SMALL_ROWS * LANES - flat.shape[1]))).reshape(SMALL_ROWS, LANES)


def _unpack_small(packed, like):
    flat = packed.reshape(1, -1)
    out, at = {}, 0
    for n in SMALL:
        size = int(np.prod(like[n].shape))
        out[n] = flat[:, at:at + size].reshape(like[n].shape)
        at += size
    return out


def _pack_shard(d):
    return jnp.concatenate([d[n].reshape(-1, LANES) for n in SHARDED], axis=0)


def _unpack_shard(packed, like):
    out, at = {}, 0
    for n, rows in zip(SHARDED, PACK_ROWS):
        out[n] = packed[at:at + rows].reshape(like[n].shape)
        at += rows
    return out


def _constants():
    bo = np.kron(np.eye(2, dtype=np.float32), np.ones((64, 64), np.float32))
    eye = np.concatenate([np.eye(64, dtype=np.float32)] * 2, axis=1)
    inv = ROPE_THETA ** (-np.arange(0, 64, 2, dtype=np.float32) / 64)
    invf = np.concatenate([inv, inv, np.zeros(64, np.float32)]).astype(np.float32)[None, :]
    return jnp.asarray(bo), jnp.asarray(eye), jnp.asarray(invf)


def kernel(x, positions, norm_pre_g, w_in, mla_q_norm_g, mla_w_uq, mla_kv_norm_g, mla_w_ukv, rw_mu, rw_w0, rw_w2, rw_a0, rw_a2, rw_k_k, rw_k_a, rw_r_k, rw_ln_g, rw_ln_b, w_out, norm_post_g, loss_target, m_norm_pre_g, m_w_in, m_mla_q_norm_g, m_mla_w_uq, m_mla_kv_norm_g, m_mla_w_ukv, m_rw_mu, m_rw_w0, m_rw_w2, m_rw_a0, m_rw_a2, m_rw_k_k, m_rw_k_a, m_rw_r_k, m_rw_ln_g, m_rw_ln_b, m_w_out, m_norm_post_g, v_norm_pre_g, v_w_in, v_mla_q_norm_g, v_mla_w_uq, v_mla_kv_norm_g, v_mla_w_ukv, v_rw_mu, v_rw_w0, v_rw_w2, v_rw_a0, v_rw_a2, v_rw_k_k, v_rw_k_a, v_rw_r_k, v_rw_ln_g, v_rw_ln_b, v_w_out, v_norm_post_g):
    wts = dict(norm_pre_g=norm_pre_g, w_in=w_in, mla_q_norm_g=mla_q_norm_g, mla_w_uq=mla_w_uq,
               mla_kv_norm_g=mla_kv_norm_g, mla_w_ukv=mla_w_ukv, rw_mu=rw_mu, rw_w0=rw_w0, rw_w2=rw_w2, rw_a0=rw_a0,
               rw_a2=rw_a2, rw_k_k=rw_k_k, rw_k_a=rw_k_a, rw_r_k=rw_r_k, rw_ln_g=rw_ln_g, rw_ln_b=rw_ln_b, w_out=w_out,
               norm_post_g=norm_post_g)
    mom_m = dict(norm_pre_g=m_norm_pre_g, w_in=m_w_in, mla_q_norm_g=m_mla_q_norm_g, mla_w_uq=m_mla_w_uq,
                 mla_kv_norm_g=m_mla_kv_norm_g, mla_w_ukv=m_mla_w_ukv, rw_mu=m_rw_mu, rw_w0=m_rw_w0, rw_w2=m_rw_w2,
                 rw_a0=m_rw_a0, rw_a2=m_rw_a2, rw_k_k=m_rw_k_k, rw_k_a=m_rw_k_a, rw_r_k=m_rw_r_k, rw_ln_g=m_rw_ln_g,
                 rw_ln_b=m_rw_ln_b, w_out=m_w_out, norm_post_g=m_norm_post_g)
    mom_v = dict(norm_pre_g=v_norm_pre_g, w_in=v_w_in, mla_q_norm_g=v_mla_q_norm_g, mla_w_uq=v_mla_w_uq,
                 mla_kv_norm_g=v_mla_kv_norm_g, mla_w_ukv=v_mla_w_ukv, rw_mu=v_rw_mu, rw_w0=v_rw_w0, rw_w2=v_rw_w2,
                 rw_a0=v_rw_a0, rw_a2=v_rw_a2, rw_k_k=v_rw_k_k, rw_k_a=v_rw_k_a, rw_r_k=v_rw_r_k, rw_ln_g=v_rw_ln_g,
                 rw_ln_b=v_rw_ln_b, w_out=v_w_out, norm_post_g=v_norm_post_g)
    bsz, t, _ = x.shape
    bo, eye, invf = _constants()
    c_idx = lax.axis_index("c")
    shard_idx = 2 * lax.axis_index("x") + lax.axis_index("y")

    g_in, g_uq, g_ukv, g_w2, g_a2, g_out = _ag_weights([wts[n][0] for n in SHARDED])
    w_in_f = jnp.transpose(g_in, (1, 0, 2)).reshape(D, D_IN)
    wp = jnp.concatenate([w_in_f[:, 2112:3136], w_in_f[:, 0:384], w_in_f[:, 448:1984], w_in_f[:, 384:448],
                          w_in_f[:, 1984:2112], jnp.zeros((D, 64), BF16)], axis=1)
    wuq = jnp.pad(jnp.transpose(g_uq, (1, 0, 2)).reshape(256, HEADS, 192), ((0, 0), (0, 0), (0, 64))).reshape(256, 1024)
    wukv = jnp.transpose(jnp.transpose(g_ukv, (1, 0, 2)).reshape(128, HEADS, 2, 128), (0, 2, 1, 3)).reshape(128, 1024)
    w2 = jnp.transpose(g_w2, (1, 0, 2)).reshape(64, RW)
    a2 = jnp.transpose(g_a2, (1, 0, 2)).reshape(64, RW)
    w2p = jnp.pad(w2, ((64, 128), (0, 0)))
    a2p = jnp.pad(a2, ((128, 64), (0, 0)))
    wo = g_out.reshape(D, D)
    mu = jnp.concatenate([rw_mu[:, 0:1536], jnp.zeros((1, 64), F32), rw_mu[:, 1536:1664], jnp.zeros((1, 64), F32)],
                         axis=1)
    r_k = rw_r_k.reshape(1, RW)
    pos = positions.astype(F32)[:, :, None]

    (u, pp, q_att, k_att, v_att, r, w, kp, v, al, be) = _pre_fwd(
        x, pos, invf, norm_pre_g, wp, mla_q_norm_g, wuq, mla_kv_norm_g, wukv, mu, rw_w0, w2p, rw_a0, a2p, rw_k_k,
        rw_k_a, bo)
    o, lse = _attn_fwd(q_att, k_att, v_att)
    yw, sck = _wkv_fwd(r, w, kp, v, al, be, bo, eye)

    (dh, dz, dym, dyw, dbon, loss_acc, d_wo, d_gpost, d_lng, d_lnb, d_rk) = _post(
        x, loss_target, pp, o, yw, r, kp, v, rw_ln_g, rw_ln_b, r_k, wo, wo.T, norm_post_g, bo)
    loss = lax.psum(loss_acc[0, 0], ("x", "y", "c"))

    dwkv = _wkv_bwd(r, w, kp, v, al, be, dyw, sck, bo, eye)
    dq, dk, dva = _attn_bwd(q_att, k_att, v_att, o, lse, dym)

    (da, d_wuq, d_wukv, d_w2p, d_a2p, d_gq, d_gkv, d_mu, d_w0, d_a0, d_kk, d_ka) = _pre_bwd_a(
        pp, pos, invf, (mla_q_norm_g, wuq.T, mla_kv_norm_g, wukv.T), mu, rw_w0, w2p, w2p.T, rw_a0, a2p, a2p.T,
        rw_k_k, rw_k_a, bo, dq, dk, dva, dwkv, dbon)
    grad_x, dpb, d_gpre = _pre_bwd_b(x, dh, dz, da, mu, wp.T, norm_pre_g)
    d_wp = _tn_matmul(u.reshape(bsz * t, D), dpb.reshape(bsz * t, DP), 640, "dw_in")

    full_g = {
        "w_in": jnp.concatenate([d_wp[:, 1024:1408], d_wp[:, 2944:3008], d_wp[:, 1408:2944], d_wp[:, 3008:3136],
                                 d_wp[:, 0:1024]], axis=1),
        "mla_w_uq": d_wuq.reshape(256, HEADS, 256)[:, :, :192].reshape(256, 768),
        "mla_w_ukv": jnp.transpose(d_wukv.reshape(128, 2, HEADS, 128), (0, 2, 1, 3)).reshape(128, 1024),
        "rw_w2": d_w2p[64:128],
        "rw_a2": d_a2p[128:192],
        "w_out": d_wo,
    }
    small_g = {
        "norm_pre_g": d_gpre, "mla_q_norm_g": d_gq, "mla_kv_norm_g": d_gkv,
        "rw_mu": jnp.concatenate([d_mu[:, 0:1536], d_mu[:, 1600:1728]], axis=1),
        "rw_w0": d_w0, "rw_a0": d_a0, "rw_k_k": d_kk, "rw_k_a": d_ka, "rw_r_k": d_rk, "rw_ln_g": d_lng,
        "rw_ln_b": d_lnb, "norm_post_g": d_gpost,
    }

    def by_shard(name, g):
        if name == "w_out":
            return g.reshape(N_SHARD, -1, LANES)
        rows, cols = g.shape
        return jnp.transpose(g.reshape(rows, N_SHARD, cols // N_SHARD), (1, 0, 2)).reshape(N_SHARD, -1, LANES)

    packed = jnp.concatenate([by_shard(n, full_g[n]) for n in SHARDED], axis=1)
    halves = packed.reshape(N_SHARD, 2, HALF, LANES)
    keep = lax.dynamic_index_in_dim(halves, c_idx, 1, keepdims=False)
    give = lax.dynamic_index_in_dim(halves, 1 - c_idx, 1, keepdims=False)
    got = _rs_pair_exchange(give)
    pair_sum = _add_n([keep.reshape(-1, LANES), got.reshape(-1, LANES)], "rs_pair_sum").reshape(N_SHARD, HALF, LANES)
    arrived = _rs_chip_exchange(pair_sum)
    own = lax.dynamic_index_in_dim(pair_sum, shard_idx, 0, keepdims=False)
    reduced_half = _add_n([own, arrived[0], arrived[1], arrived[2]], "rs_chip_sum")
    g_shard = _rs_pair_gather(reduced_half).reshape(PACK_TOTAL, LANES)

    g_small = _small_allreduce(_pack_small(small_g))

    shard_like = {n: wts[n][0] for n in SHARDED}
    d_sh, nm_sh, nv_sh = _adamw(_pack_shard({n: wts[n][0] for n in SHARDED}), g_shard,
                                _pack_shard({n: mom_m[n][0] for n in SHARDED}),
                                _pack_shard({n: mom_v[n][0] for n in SHARDED}), "adamw_sharded", 568)
    d_sm, nm_sm, nv_sm = _adamw(_pack_small(wts), g_small, _pack_small(mom_m), _pack_small(mom_v), "adamw_small",
                                SMALL_ROWS)

    def unpack(sh, sm):
        out = {n: a[None] for n, a in _unpack_shard(sh, shard_like).items()}
        out.update(_unpack_small(sm, wts))
        return out

    grads, deltas, new_m, new_v = unpack(g_shard, g_small), unpack(d_sh, d_sm), unpack(nm_sh, nm_sm), unpack(nv_sh, nv_sm)
    return (loss, grad_x, *[grads[n] for n in WEIGHTS], *[deltas[n] for n in WEIGHTS],
            *[new_m[n] for n in WEIGHTS], *[new_v[n] for n in WEIGHTS])
```

```python
import functools

import numpy as np
import jax
import jax.numpy as jnp
from jax import lax
from jax.experimental import pallas as pl
from jax.experimental.pallas import tpu as pltpu

F32, BF16 = jnp.float32, jnp.bfloat16
HIGHEST = lax.Precision.HIGHEST
MESH = pl.DeviceIdType.MESH

D = 1024
HEADS = 4
RW = 512
NORM_EPS = 1e-6
GN_EPS = 64e-5
ROPE_THETA = 10000.0
SCALE = (128 + 64) ** -0.5
D_IN = 3136
LR, B1, B2, ADAM_EPS, WD, STEP = 0.001, 0.9, 0.999, 1e-08, 0.01, 10

Z0, CQ0, CKV0, RW0, DP = 0, 1024, 1280, 1408, 3200
NRW = DP - RW0

LANES = 128
SUBLANES = 8
VMEM_LIMIT = 56 * 1024 * 1024

TT = 256
TQ = 256

N_SHARD = 4
PACK_ROWS = (1024 * 784 // 128, 256 * 192 // 128, 128 * 256 // 128, 64, 64, 256 * 1024 // 128)
PACK_TOTAL = sum(PACK_ROWS)
HALF = PACK_TOTAL // 2
SMALL_ROWS = 64


def _cparams(sem=None):
    return pltpu.CompilerParams(dimension_semantics=sem, vmem_limit_bytes=VMEM_LIMIT)


def _full(shape):
    n = len(shape)
    return pl.BlockSpec(shape, lambda *_: (0,) * n)


def _dot(a, b):
    return jnp.dot(a, b, preferred_element_type=F32)


def _dot_nt(a, b):
    return lax.dot_general(a, b, (((1,), (1,)), ((), ())), preferred_element_type=F32)


def _dot_tn(a, b):
    return lax.dot_general(a, b, (((0,), (0,)), ((), ())), preferred_element_type=F32)


def _seg(x, bo):
    parts = [jnp.dot(x[:, LANES * i:LANES * (i + 1)], bo, precision=HIGHEST, preferred_element_type=F32)
             for i in range(x.shape[1] // LANES)]
    return parts[0] if len(parts) == 1 else jnp.concatenate(parts, axis=1)


def _rms(x, g, n):
    rstd = lax.rsqrt(jnp.sum(x * x, axis=-1, keepdims=True) * (1.0 / n) + NORM_EPS)
    nx = x * rstd
    return nx * g, nx, rstd


def _rms_bwd(dy, nx, rstd, g, n):
    dn = dy * g
    dx = rstd * (dn - nx * (jnp.sum(dn * nx, axis=-1, keepdims=True) * (1.0 / n)))
    return dx, jnp.sum(dy * nx, axis=0, keepdims=True)


def _rot(x):
    lane = lax.broadcasted_iota(jnp.int32, x.shape, 1)
    return jnp.where((lane % 64) < 32, -pltpu.roll(x, x.shape[1] - 32, 1), pltpu.roll(x, 32, 1))


def _sigmoid(x):
    return 1.0 / (1.0 + jnp.exp(-x))


def _softplus(x):
    return jnp.maximum(x, 0.0) + jnp.log(1.0 + jnp.exp(-jnp.abs(x)))


def _rw_gates(ps, w0, w2p, a0, a2p, k_k, k_a, bo):
    r, k, v, misc = ps[:, 0:512], ps[:, 512:1024], ps[:, 1024:1536], ps[:, 1536:NRW]
    th = jnp.tanh(misc)
    wpre = w0 + _dot(th.astype(BF16), w2p)
    e = jnp.exp(-_softplus(-wpre) - 0.5)
    w = jnp.exp(-e)
    a = _sigmoid(a0 + _dot(misc.astype(BF16), a2p))
    m = k * k_k
    nrm = jnp.maximum(jnp.sqrt(_seg(m * m, bo)), 1e-12)
    kk = m / nrm
    kp = k * (1.0 + (a - 1.0) * k_a)
    return dict(r=r, k=k, v=v, misc=misc, th=th, wpre=wpre, e=e, w=w, a=a, nrm=nrm, kk=kk, kp=kp)


def _shift_mix(prw, prev_row, mu):
    row = lax.broadcasted_iota(jnp.int32, prw.shape, 0)
    sh = jnp.where(row == 0, prev_row, pltpu.roll(prw, 1, 0))
    return prw + (sh - prw) * mu, sh


def _ag_weights(shards):
    n = len(shards)

    def body(*refs):
        ins, outs = refs[:n], refs[n:2 * n]
        send_sems, recv_sems = refs[2 * n], refs[2 * n + 1]
        x, y, c = lax.axis_index("x"), lax.axis_index("y"), lax.axis_index("c")
        mine = 2 * x + y
        for w in range(n):
            outs[w][mine] = ins[w][...].astype(BF16)
        flips = ((1, 0), (0, 1), (1, 1))

        def copy(w, k):
            fx, fy = flips[k]
            return pltpu.make_async_remote_copy(
                src_ref=outs[w].at[mine], dst_ref=outs[w].at[mine],
                send_sem=send_sems.at[w * 3 + k], recv_sem=recv_sems.at[w * 3 + k],
                device_id=(x ^ fx, y ^ fy, c), device_id_type=MESH)

        def arrival(w, k):
            fx, fy = flips[k]
            theirs = 2 * (x ^ fx) + (y ^ fy)
            return pltpu.make_async_remote_copy(
                src_ref=outs[w].at[theirs], dst_ref=outs[w].at[theirs],
                send_sem=send_sems.at[w * 3 + k], recv_sem=recv_sems.at[w * 3 + k],
                device_id=(x ^ fx, y ^ fy, c), device_id_type=MESH)

        for w in range(n):
            for k in range(3):
                copy(w, k).start()
        for w in range(n):
            for k in range(3):
                arrival(w, k).wait_recv()
        for w in range(n):
            for k in range(3):
                copy(w, k).wait_send()

    vm = pl.BlockSpec(memory_space=pltpu.VMEM)
    return pl.pallas_call(
        body, name="ag_weights",
        out_shape=[jax.ShapeDtypeStruct((N_SHARD,) + s.shape, BF16) for s in shards],
        in_specs=[vm] * n, out_specs=[vm] * n,
        scratch_shapes=[pltpu.SemaphoreType.DMA((3 * n,)), pltpu.SemaphoreType.DMA((3 * n,))],
        compiler_params=pltpu.CompilerParams(vmem_limit_bytes=VMEM_LIMIT),
    )(*shards)


def _rs_pair_exchange(send_half):
    def body(src_ref, dst_ref, send_sem, recv_sem):
        x, y, c = lax.axis_index("x"), lax.axis_index("y"), lax.axis_index("c")
        cp = pltpu.make_async_remote_copy(src_ref=src_ref, dst_ref=dst_ref, send_sem=send_sem, recv_sem=recv_sem,
                                          device_id=(x, y, 1 - c), device_id_type=MESH)
        cp.start()
        cp.wait()

    hbm = pl.BlockSpec(memory_space=pl.ANY)
    return pl.pallas_call(
        body, name="rs_pair_exchange",
        out_shape=jax.ShapeDtypeStruct(send_half.shape, send_half.dtype),
        in_specs=[hbm], out_specs=hbm,
        scratch_shapes=[pltpu.SemaphoreType.DMA, pltpu.SemaphoreType.DMA],
    )(send_half)


def _rs_chip_exchange(part):
    def body(src_ref, dst_ref, send_sems, recv_sems):
        x, y, c = lax.axis_index("x"), lax.axis_index("y"), lax.axis_index("c")
        flips = ((1, 0), (0, 1), (1, 1))
        cps = []
        for k, (fx, fy) in enumerate(flips):
            theirs = 2 * (x ^ fx) + (y ^ fy)
            cps.append(pltpu.make_async_remote_copy(
                src_ref=src_ref.at[theirs], dst_ref=dst_ref.at[k],
                send_sem=send_sems.at[k], recv_sem=recv_sems.at[k],
                device_id=(x ^ fx, y ^ fy, c), device_id_type=MESH))
        for cp in cps:
            cp.start()
        for cp in cps:
            cp.wait()

    hbm = pl.BlockSpec(memory_space=pl.ANY)
    return pl.pallas_call(
        body, name="rs_chip_exchange",
        out_shape=jax.ShapeDtypeStruct((3,) + part.shape[1:], part.dtype),
        in_specs=[hbm], out_specs=hbm,
        scratch_shapes=[pltpu.SemaphoreType.DMA((3,)), pltpu.SemaphoreType.DMA((3,))],
    )(part)


def _rs_pair_gather(half):
    def body(src_ref, dst_ref, send_sem, recv_sem, local_sem):
        x, y, c = lax.axis_index("x"), lax.axis_index("y"), lax.axis_index("c")
        own = pltpu.make_async_copy(src_ref, dst_ref.at[c], local_sem)
        own.start()
        cp = pltpu.make_async_remote_copy(src_ref=src_ref, dst_ref=dst_ref.at[c], send_sem=send_sem, recv_sem=recv_sem,
                                          device_id=(x, y, 1 - c), device_id_type=MESH)
        cp.start()
        arrival = pltpu.make_async_remote_copy(src_ref=src_ref, dst_ref=dst_ref.at[1 - c], send_sem=send_sem,
                                               recv_sem=recv_sem, device_id=(x, y, 1 - c), device_id_type=MESH)
        arrival.wait_recv()
        cp.wait_send()
        own.wait()

    hbm = pl.BlockSpec(memory_space=pl.ANY)
    return pl.pallas_call(
        body, name="rs_pair_gather",
        out_shape=jax.ShapeDtypeStruct((2,) + half.shape, half.dtype),
        in_specs=[hbm], out_specs=hbm,
        scratch_shapes=[pltpu.SemaphoreType.DMA, pltpu.SemaphoreType.DMA, pltpu.SemaphoreType.DMA],
    )(half)


def _small_allreduce(vec):
    def body(in_ref, out_ref, recv, send_sems, recv_sems):
        x, y, c = lax.axis_index("x"), lax.axis_index("y"), lax.axis_index("c")
        me = 4 * x + 2 * y + c
        cps = []
        for k in range(1, 8):
            fx, fy, fc = (k >> 2) & 1, (k >> 1) & 1, k & 1
            cps.append(pltpu.make_async_remote_copy(
                src_ref=in_ref, dst_ref=recv.at[k - 1],
                send_sem=send_sems.at[k - 1], recv_sem=recv_sems.at[k - 1],
                device_id=(x ^ fx, y ^ fy, c ^ fc), device_id_type=MESH))
        for cp in cps:
            cp.start()
        for cp in cps:
            cp.wait()
        acc = jnp.zeros(in_ref.shape, F32)
        for j in range(8):
            slot = jnp.maximum((me ^ j) - 1, 0)
            acc = acc + jnp.where(me == j, in_ref[...], recv[slot])
        out_ref[...] = acc

    vm = pl.BlockSpec(memory_space=pltpu.VMEM)
    return pl.pallas_call(
        body, name="small_allreduce",
        out_shape=jax.ShapeDtypeStruct(vec.shape, F32),
        in_specs=[vm], out_specs=vm,
        scratch_shapes=[pltpu.VMEM((7,) + vec.shape, F32), pltpu.SemaphoreType.DMA((7,)),
                        pltpu.SemaphoreType.DMA((7,))],
    )(vec)


def _add_n(arrs, name, rows=568):
    n = len(arrs)
    r = arrs[0].shape[0]

    def body(*refs):
        acc = refs[0][...]
        for k in range(1, n):
            acc = acc + refs[k][...]
        refs[n][...] = acc

    spec = pl.BlockSpec((rows, LANES), lambda i: (i, 0))
    return pl.pallas_call(
        body, name=name, grid=(r // rows,),
        out_shape=jax.ShapeDtypeStruct(arrs[0].shape, F32),
        in_specs=[spec] * n, out_specs=spec,
        compiler_params=_cparams(("parallel",)),
    )(*arrs)


def _adamw(w, g, m, v, name, rows):
    r = w.shape[0]

    def body(w_ref, g_ref, m_ref, v_ref, d_ref, nm_ref, nv_ref):
        gg = g_ref[...]
        nm = B1 * m_ref[...] + (1.0 - B1) * gg
        nv = B2 * v_ref[...] + (1.0 - B2) * (gg * gg)
        m_hat = nm / (1.0 - B1 ** STEP)
        v_hat = nv / (1.0 - B2 ** STEP)
        d_ref[...] = -LR * (m_hat / (jnp.sqrt(v_hat) + ADAM_EPS) + WD * w_ref[...])
        nm_ref[...] = nm
        nv_ref[...] = nv

    spec = pl.BlockSpec((rows, LANES), lambda i: (i, 0))
    sds = jax.ShapeDtypeStruct(w.shape, F32)
    return pl.pallas_call(
        body, name=name, grid=(r // rows,),
        out_shape=[sds, sds, sds],
        in_specs=[spec] * 4, out_specs=[spec] * 3,
        compiler_params=_cparams(("parallel",)),
    )(w, g, m, v)


def _pre_fwd(x, pos, invf, gpre, wp, gq, wuq, gkv, wukv, mu, w0, w2p, a0, a2p, k_k, k_a, bo):
    bsz, t, _ = x.shape
    nt = t // TT

    def body(x_ref, pos_ref, invf_ref, gpre_ref, wp_ref, gq_ref, wuq_ref, gkv_ref, wukv_ref, mu_ref, w0_ref,
             w2p_ref, a0_ref, a2p_ref, kk_ref, ka_ref, bo_ref,
             u_ref, pp_ref, q_ref, k_ref, v_ref, r_o, w_o, kp_o, vv_o, al_o, be_o, carry):
        i = pl.program_id(1)
        u, _, _ = _rms(x_ref[0], gpre_ref[...], D)
        ub = u.astype(BF16)
        u_ref[0] = ub
        p = _dot(ub, wp_ref[...])
        pp_ref[0] = p
        prw = p[:, RW0:DP]

        @pl.when(i == 0)
        def _():
            carry[...] = jnp.zeros(carry.shape, F32)

        ps, _ = _shift_mix(prw, carry[7:8, :], mu_ref[...])
        carry[...] = prw[TT - 8:TT, :]

        g = _rw_gates(ps, w0_ref[...], w2p_ref[...], a0_ref[...], a2p_ref[...], kk_ref[...], ka_ref[...],
                      bo_ref[...])
        r_o[0] = g["r"]
        w_o[0] = g["w"]
        kp_o[0] = g["kp"]
        vv_o[0] = g["v"]
        al_o[0] = -g["kk"]
        be_o[0] = g["kk"] * g["a"]

        cqn, _, _ = _rms(p[:, CQ0:CQ0 + 256], gq_ref[...], 256)
        q = _dot(cqn.astype(BF16), wuq_ref[...])
        ckvn, _, _ = _rms(p[:, CKV0:CKV0 + 128], gkv_ref[...], 128)
        kv = _dot(ckvn.astype(BF16), wukv_ref[...])
        ang = pos_ref[0] * invf_ref[...]
        cs, sn = jnp.cos(ang), jnp.sin(ang)
        lane = lax.broadcasted_iota(jnp.int32, cs.shape, 1)
        kr = ps[:, 1536:1536 + LANES]
        kr = jnp.where(lane < 64, kr * cs + _rot(kr) * sn, 0.0).astype(BF16)
        for h in range(HEADS):
            qr = q[:, 256 * h + 128:256 * h + 256]
            q_ref[0, :, 256 * h:256 * h + 128] = q[:, 256 * h:256 * h + 128].astype(BF16)
            q_ref[0, :, 256 * h + 128:256 * h + 256] = (qr * cs + _rot(qr) * sn).astype(BF16)
            k_ref[0, :, 256 * h:256 * h + 128] = kv[:, 128 * h:128 * h + 128].astype(BF16)
            k_ref[0, :, 256 * h + 128:256 * h + 256] = kr
        v_ref[0] = kv[:, 512:1024].astype(BF16)

    tok = lambda c: pl.BlockSpec((1, TT, c), lambda b, i: (b, i, 0))
    full = lambda a: _full(a.shape)
    ins = (x, pos, invf, gpre, wp, gq, wuq, gkv, wukv, mu, w0, w2p, a0, a2p, k_k, k_a, bo)
    in_specs = [tok(D), tok(1)] + [full(a) for a in ins[2:]]
    sd = lambda c, dt: jax.ShapeDtypeStruct((bsz, t, c), dt)
    out_shape = [sd(D, BF16), sd(DP, F32), sd(1024, BF16), sd(1024, BF16), sd(512, BF16)] + [sd(RW, F32)] * 6
    out_specs = [tok(D), tok(DP), tok(1024), tok(1024), tok(512)] + [tok(RW)] * 6
    return pl.pallas_call(
        body, name="pre_fwd", grid=(bsz, nt), out_shape=out_shape, in_specs=in_specs, out_specs=out_specs,
        scratch_shapes=[pltpu.VMEM((8, NRW), F32)],
        compiler_params=_cparams(("arbitrary", "arbitrary")),
    )(*ins)


def _attn_fwd(q, k, v):
    bsz, t, _ = q.shape
    nq = t // TQ

    def body(q_ref, k_ref, v_ref, o_ref, lse_ref):
        i = pl.program_id(2)
        qt = q_ref[0]
        row = lax.broadcasted_iota(jnp.int32, (TQ, TQ), 0) + i * TQ
        col0 = lax.broadcasted_iota(jnp.int32, (TQ, TQ), 1)

        def step(j, carry):
            m, l, acc = carry
            at = pl.ds(pl.multiple_of(j * TQ, TQ), TQ)
            s = _dot_nt(qt, k_ref[0, at, :]) * SCALE
            s = jnp.where(col0 + j * TQ <= row, s, -1e30)
            mn = jnp.maximum(m, jnp.max(s, axis=1, keepdims=True))
            p = jnp.exp(s - mn)
            al = jnp.exp(m - mn)
            l = al * l + jnp.sum(p, axis=1, keepdims=True)
            acc = al * acc + _dot(p.astype(BF16), v_ref[0, at, :])
            return mn, l, acc

        m, l, acc = lax.fori_loop(
            0, i + 1, step,
            (jnp.full((TQ, 1), -1e30, F32), jnp.zeros((TQ, 1), F32), jnp.zeros((TQ, LANES), F32)))
        o_ref[0] = acc / l
        lse_ref[0, 0] = jnp.broadcast_to(m + jnp.log(l), (TQ, LANES))

    return pl.pallas_call(
        body, name="attn_fwd", grid=(bsz, HEADS, nq),
        out_shape=[jax.ShapeDtypeStruct((bsz, t, 512), F32), jax.ShapeDtypeStruct((bsz, HEADS, t, LANES), F32)],
        in_specs=[pl.BlockSpec((1, TQ, 256), lambda b, h, i: (b, i, h)),
                  pl.BlockSpec((1, t, 256), lambda b, h, i: (b, 0, h)),
                  pl.BlockSpec((1, t, LANES), lambda b, h, i: (b, 0, h))],
        out_specs=[pl.BlockSpec((1, TQ, LANES), lambda b, h, i: (b, i, h)),
                   pl.BlockSpec((1, 1, TQ, LANES), lambda b, h, i: (b, h, i, 0))],
        compiler_params=_cparams(("parallel", "parallel", "arbitrary")),
    )(q, k, v)


def _attn_bwd(q, k, v, o, lse, do):
    bsz, t, _ = q.shape
    nq = t // TQ

    def body(q_ref, k_ref, v_ref, o_ref, lse_ref, do_ref, dq_ref, dk_ref, dv_ref, dl_ref):
        def prep(i, _):
            at = pl.ds(pl.multiple_of(i * TQ, TQ), TQ)
            dl_ref[at, :] = jnp.broadcast_to(jnp.sum(do_ref[0, at, :] * o_ref[0, at, :], axis=1, keepdims=True),
                                             (TQ, LANES))
            return 0

        lax.fori_loop(0, nq, prep, 0)
        dq_ref[0] = jnp.zeros((t, 256), F32)
        row0 = lax.broadcasted_iota(jnp.int32, (TQ, TQ), 0)
        col0 = lax.broadcasted_iota(jnp.int32, (TQ, TQ), 1)

        def kv_tile(j, _):
            atk = pl.ds(pl.multiple_of(j * TQ, TQ), TQ)
            kt = k_ref[0, atk, :]
            vt = v_ref[0, atk, :]

            def q_tile(i, carry):
                dk, dv = carry
                atq = pl.ds(pl.multiple_of(i * TQ, TQ), TQ)
                qt = q_ref[0, atq, :]
                dob = do_ref[0, atq, :].astype(BF16)
                s = _dot_nt(qt, kt) * SCALE
                s = jnp.where(col0 + j * TQ <= row0 + i * TQ, s, -1e30)
                p = jnp.exp(s - lse_ref[0, 0, atq, :][:, 0:1])
                dv = dv + _dot_tn(p.astype(BF16), dob)
                dp = _dot_nt(dob, vt)
                ds = (p * (dp - dl_ref[atq, :][:, 0:1]) * SCALE).astype(BF16)
                dk = dk + _dot_tn(ds, qt)
                dq_ref[0, atq, :] += _dot(ds, kt)
                return dk, dv

            dk, dv = lax.fori_loop(j, nq, q_tile, (jnp.zeros((TQ, 256), F32), jnp.zeros((TQ, LANES), F32)))
            dk_ref[0, atk, :] = dk
            dv_ref[0, atk, :] = dv
            return 0

        lax.fori_loop(0, nq, kv_tile, 0)

    s256 = pl.BlockSpec((1, t, 256), lambda b, h: (b, 0, h))
    s128 = pl.BlockSpec((1, t, LANES), lambda b, h: (b, 0, h))
    return pl.pallas_call(
        body, name="attn_bwd", grid=(bsz, HEADS),
        out_shape=[jax.ShapeDtypeStruct((bsz, t, 1024), F32), jax.ShapeDtypeStruct((bsz, t, 1024), F32),
                   jax.ShapeDtypeStruct((bsz, t, 512), F32)],
        in_specs=[s256, s256, s128, s128, pl.BlockSpec((1, 1, t, LANES), lambda b, h: (b, h, 0, 0)), s128],
        out_specs=[s256, s256, s128],
        scratch_shapes=[pltpu.VMEM((t, LANES), F32)],
        compiler_params=_cparams(("parallel", "parallel")),
    )(q, k, v, o, lse, do)


RW_HEADS = 8
TB = 32
CH = 16


def _lane_split(bsz):
    vs = LANES // (bsz * RW_HEADS)
    return vs, 64 // vs


def _to_k(x):
    bsz, t, _ = x.shape
    vs, _ = _lane_split(bsz)
    y = jnp.transpose(x.reshape(bsz, t, RW_HEADS, 64), (1, 3, 0, 2)).reshape(t, 64, bsz * RW_HEADS)
    return jnp.tile(y, (1, 1, vs))


def _from_k(y, bsz):
    t = y.shape[0]
    y = y[:, :, 0:bsz * RW_HEADS].reshape(t, 64, bsz, RW_HEADS)
    return jnp.transpose(y, (2, 0, 3, 1)).reshape(bsz, t, RW)


def _to_v(x):
    bsz, t, _ = x.shape
    vs, vq = _lane_split(bsz)
    return jnp.transpose(x.reshape(bsz, t, RW_HEADS, vq, vs), (1, 3, 4, 0, 2)).reshape(t, vq, LANES)


def _from_v(y, bsz):
    t = y.shape[0]
    vs, vq = _lane_split(bsz)
    return jnp.transpose(y.reshape(t, vq, vs, bsz, RW_HEADS), (3, 0, 4, 1, 2)).reshape(bsz, t, RW)


def _ksum(a):
    return jnp.sum(a, axis=0, keepdims=True)


def _wkv_fwd(r, w, kp, al, be, v):
    t, vq = v.shape[0], v.shape[1]

    def body(r_ref, w_ref, kp_ref, al_ref, be_ref, v_ref, y_ref, ck_ref, st_ref):
        @pl.when(pl.program_id(0) == 0)
        def _():
            st_ref[...] = jnp.zeros(st_ref.shape, F32)

        def step(tl, _):
            @pl.when(tl % CH == 0)
            def _():
                ck_ref[tl // CH] = st_ref[...]

            rv, wv, kv, av, bv = r_ref[tl], w_ref[tl], kp_ref[tl], al_ref[tl], be_ref[tl]
            vs = v_ref[tl]
            rows = []
            for q in range(vq):
                s = st_ref[q]
                u = _ksum(s * av)
                s = s * wv + bv * u + kv * vs[q:q + 1]
                st_ref[q] = s
                rows.append(_ksum(s * rv))
            y_ref[tl] = jnp.concatenate(rows, axis=0)
            return 0

        lax.fori_loop(0, TB, step, 0)

    kspec = pl.BlockSpec((TB, 64, LANES), lambda i: (i, 0, 0))
    vspec = pl.BlockSpec((TB, vq, LANES), lambda i: (i, 0, 0))
    return pl.pallas_call(
        body, name="wkv_fwd", grid=(t // TB,),
        out_shape=[jax.ShapeDtypeStruct((t, vq, LANES), F32), jax.ShapeDtypeStruct((t // CH, vq, 64, LANES), F32)],
        in_specs=[kspec] * 5 + [vspec],
        out_specs=[vspec, pl.BlockSpec((TB // CH, vq, 64, LANES), lambda i: (i, 0, 0, 0))],
        scratch_shapes=[pltpu.VMEM((vq, 64, LANES), F32)],
        compiler_params=_cparams(("arbitrary",)),
    )(r, w, kp, al, be, v)


def _wkv_bwd(r, w, kp, al, be, v, dy, ck, bsz):
    t, vq = v.shape[0], v.shape[1]
    n = t // CH

    def fold(a):
        sh = LANES // 2
        while sh >= bsz * RW_HEADS:
            a = a + pltpu.roll(a, sh, 1)
            sh //= 2
        return a

    def body(r_ref, w_ref, kp_ref, al_ref, be_ref, v_ref, dy_ref, ck_ref,
             dr_ref, dw_ref, dkp_ref, dal_ref, dbe_ref, dv_ref, ds_ref, sp_ref, u_ref):
        @pl.when(pl.program_id(0) == 0)
        def _():
            ds_ref[...] = jnp.zeros(ds_ref.shape, F32)

        sp_ref[0] = ck_ref[0]

        def recompute(tl, _):
            wv, kv, av, bv = w_ref[tl], kp_ref[tl], al_ref[tl], be_ref[tl]
            vs = v_ref[tl]
            rows = []
            for q in range(vq):
                s = sp_ref[tl, q]
                u = _ksum(s * av)
                rows.append(u)
                sp_ref[tl + 1, q] = s * wv + bv * u + kv * vs[q:q + 1]
            u_ref[tl] = jnp.concatenate(rows, axis=0)
            return 0

        lax.fori_loop(0, CH, recompute, 0)

        def reverse(i, _):
            tl = CH - 1 - i
            rv, wv, kv, av, bv = r_ref[tl], w_ref[tl], kp_ref[tl], al_ref[tl], be_ref[tl]
            vs, dys, us = v_ref[tl], dy_ref[tl], u_ref[tl]
            acc = None
            dvrows = []
            for q in range(vq):
                s_prev = sp_ref[tl, q]
                dyq = dys[q:q + 1]
                ds = ds_ref[q] + rv * dyq
                c = _ksum(ds * bv)
                dvrows.append(_ksum(ds * kv))
                terms = (sp_ref[tl + 1, q] * dyq, ds * s_prev, ds * vs[q:q + 1], s_prev * c, ds * us[q:q + 1])
                acc = terms if acc is None else tuple(a + b for a, b in zip(acc, terms))
                ds_ref[q] = ds * wv + av * c
            for ref, a in zip((dr_ref, dw_ref, dkp_ref, dal_ref, dbe_ref), acc):
                ref[tl] = fold(a)
            dv_ref[tl] = jnp.concatenate(dvrows, axis=0)
            return 0

        lax.fori_loop(0, CH, reverse, 0)

    kspec = pl.BlockSpec((CH, 64, LANES), lambda i: (n - 1 - i, 0, 0))
    vspec = pl.BlockSpec((CH, vq, LANES), lambda i: (n - 1 - i, 0, 0))
    ksd = jax.ShapeDtypeStruct((t, 64, LANES), F32)
    return pl.pallas_call(
        body, name="wkv_bwd", grid=(n,),
        out_shape=[ksd] * 5 + [jax.ShapeDtypeStruct((t, vq, LANES), F32)],
        in_specs=[kspec] * 5 + [vspec, vspec, pl.BlockSpec((1, vq, 64, LANES), lambda i: (n - 1 - i, 0, 0, 0))],
        out_specs=[kspec] * 5 + [vspec],
        scratch_shapes=[pltpu.VMEM((vq, 64, LANES), F32), pltpu.VMEM((CH + 1, vq, 64, LANES), F32),
                        pltpu.VMEM((CH, vq, LANES), F32)],
        compiler_params=_cparams(("arbitrary",)),
    )(r, w, kp, al, be, v, dy, ck)


def _post(x, tgt, pp, o, yw, r, kp, v, ln_g, ln_b, r_k, wo, wot, gpost, bo):
    bsz, t, _ = x.shape
    nt = t // TT

    def body(x_ref, tgt_ref, z_ref, o_ref, yw_ref, r_ref, kp_ref, v_ref, lng_ref, lnb_ref, rk_ref, wo_ref, wot_ref,
             gpost_ref, bo_ref,
             dh_ref, dz_ref, dym_ref, dyw_ref, dbon_ref, loss_ref, dwo_ref, dgpost_ref, dlng_ref, dlnb_ref, drk_ref):
        first = (pl.program_id(0) == 0) & (pl.program_id(1) == 0)

        @pl.when(first)
        def _():
            for ref in (loss_ref, dwo_ref, dgpost_ref, dlng_ref, dlnb_ref, drk_ref):
                ref[...] = jnp.zeros(ref.shape, F32)

        bo_m = bo_ref[...]
        seg = lambda a: _seg(a, bo_m)
        rowsum = lambda a: jnp.sum(a, axis=0, keepdims=True)
        ywv, rv, kpv, vv = yw_ref[0], r_ref[0], kp_ref[0], v_ref[0]
        ln_g, r_k = lng_ref[...], rk_ref[...]
        mean = seg(ywv) * (1.0 / 64)
        yc = ywv - mean
        rstd = lax.rsqrt(seg(yc * yc) * (1.0 / 64) + GN_EPS)
        yhat = yc * rstd
        sb = seg(rv * kpv * r_k)
        y_rw = yhat * ln_g + lnb_ref[...] + sb * vv
        z = z_ref[0]
        sig = _sigmoid(z)
        sz = z * sig
        ycat = jnp.concatenate([o_ref[0], y_rw], axis=1)
        ycg = (ycat * sz).astype(BF16)
        out = _dot(ycg, wo_ref[...])
        hn, nx, rstd_o = _rms(out, gpost_ref[...], D)
        err = x_ref[0] + hn - tgt_ref[0]
        loss_ref[...] += jnp.sum(err * err) * (0.5 / D)
        dh = err * (1.0 / D)
        dh_ref[0] = dh
        dout, dgp = _rms_bwd(dh, nx, rstd_o, gpost_ref[...], D)
        dgpost_ref[...] += dgp
        doutb = dout.astype(BF16)
        dwo_ref[...] += _dot_tn(ycg, doutb)
        dycg = _dot(doutb, wot_ref[...])
        dz_ref[0] = dycg * ycat * (sig * (1.0 + z * (1.0 - sig)))
        dycat = dycg * sz
        dym_ref[0] = dycat[:, 0:512]
        dy_rw = dycat[:, 512:1024]
        dlnb_ref[...] += rowsum(dy_rw)
        dlng_ref[...] += rowsum(dy_rw * yhat)
        dyhat = dy_rw * ln_g
        dyw_ref[0] = rstd * (dyhat - seg(dyhat) * (1.0 / 64) - yhat * (seg(dyhat * yhat) * (1.0 / 64)))
        dsb = seg(dy_rw * vv)
        drk_ref[...] += rowsum(dsb * rv * kpv)
        dbon_ref[0, :, 0:512] = dsb * kpv * r_k
        dbon_ref[0, :, 512:1024] = dsb * rv * r_k
        dbon_ref[0, :, 1024:1536] = dy_rw * sb

    tok = lambda c: pl.BlockSpec((1, TT, c), lambda b, i: (b, i, 0))
    full = lambda a: _full(a.shape)
    ins = (x, tgt, pp, o, yw, r, kp, v, ln_g, ln_b, r_k, wo, wot, gpost, bo)
    in_specs = [tok(D), tok(D), tok(1024)] + [tok(512)] * 5 + [full(a) for a in ins[8:]]
    sd = lambda c: jax.ShapeDtypeStruct((bsz, t, c), F32)
    vec = lambda c: jax.ShapeDtypeStruct((1, c), F32)
    out_shape = [sd(D), sd(1024), sd(512), sd(512), sd(1536), jax.ShapeDtypeStruct((8, LANES), F32),
                 jax.ShapeDtypeStruct((1024, 1024), F32), vec(D), vec(512), vec(512), vec(512)]
    out_specs = [tok(D), tok(1024), tok(512), tok(512), tok(1536), _full((8, LANES)), _full((1024, 1024)),
                 _full((1, D)), _full((1, 512)), _full((1, 512)), _full((1, 512))]
    return pl.pallas_call(
        body, name="post", grid=(bsz, nt), out_shape=out_shape, in_specs=in_specs, out_specs=out_specs,
        compiler_params=_cparams(("arbitrary", "arbitrary")),
    )(*ins)


def _pre_bwd_a(pp, pos, invf, cqkv_w, mu, w0, w2p, w2pt, a0, a2p, a2pt, k_k, k_a, bo,
               dq, dk, dva, dwkv, dbon):
    gq, wuqt, gkv, wukvt = cqkv_w
    bsz, t, _ = pp.shape
    nt = t // TT
    dr_w, dw_w, dkp_w, dv_w, dal_w, dbe_w = dwkv

    def body(pp_ref, pos_ref, invf_ref, gq_ref, wuqt_ref, gkv_ref, wukvt_ref, mu_ref, w0_ref, w2p_ref, w2pt_ref,
             a0_ref, a2p_ref, a2pt_ref, kk_ref, ka_ref, bo_ref, dq_ref, dk_ref, dva_ref,
             dr_ref, dw_ref, dkp_ref, dv_ref, dal_ref, dbe_ref, dbon_ref,
             da_ref, dwuq_ref, dwukv_ref, dw2p_ref, da2p_ref, dgq_ref, dgkv_ref, dmu_ref, dw0_ref, da0_ref,
             dkk_ref, dka_ref, carry):
        i = pl.program_id(1)
        first = (pl.program_id(0) == 0) & (i == 0)

        @pl.when(first)
        def _():
            for ref in (dwuq_ref, dwukv_ref, dw2p_ref, da2p_ref, dgq_ref, dgkv_ref, dmu_ref, dw0_ref, da0_ref,
                        dkk_ref, dka_ref):
                ref[...] = jnp.zeros(ref.shape, F32)

        bo_m = bo_ref[...]
        rowsum = lambda a: jnp.sum(a, axis=0, keepdims=True)
        prw = pp_ref[0, :, RW0:DP]

        @pl.when(i == 0)
        def _():
            carry[...] = jnp.zeros(carry.shape, F32)

        ps, sh = _shift_mix(prw, carry[7:8, :], mu_ref[...])
        carry[...] = prw[TT - 8:TT, :]
        k_k, k_a = kk_ref[...], ka_ref[...]
        g = _rw_gates(ps, w0_ref[...], w2p_ref[...], a0_ref[...], a2p_ref[...], k_k, k_a, bo_m)
        a, kk, k = g["a"], g["kk"], g["k"]
        dr = dr_ref[0] + dbon_ref[0, :, 0:512]
        dkp = dkp_ref[0] + dbon_ref[0, :, 512:1024]
        dv = dv_ref[0] + dbon_ref[0, :, 1024:1536]
        dbe = dbe_ref[0]
        dkk = dbe * a - dal_ref[0]
        da = dbe * kk + dkp * k * k_a
        dka_ref[...] += rowsum(dkp * k * (a - 1.0))
        dm = (dkk - kk * _seg(dkk * kk, bo_m)) / g["nrm"]
        dkk_ref[...] += rowsum(dm * k)
        dk_tot = dkp * (1.0 + (a - 1.0) * k_a) + dm * k_k
        dapre = da * a * (1.0 - a)
        da0_ref[...] += rowsum(dapre)
        dapb = dapre.astype(BF16)
        da2p_ref[...] += _dot_tn(g["misc"].astype(BF16), dapb)
        dwpre = dw_ref[0] * g["w"] * (-g["e"]) * _sigmoid(-g["wpre"])
        dw0_ref[...] += rowsum(dwpre)
        dwpb = dwpre.astype(BF16)
        th = g["th"]
        dw2p_ref[...] += _dot_tn(th.astype(BF16), dwpb)
        dmisc = _dot(dapb, a2pt_ref[...]) + _dot(dwpb, w2pt_ref[...]) * (1.0 - th * th)
        ang = pos_ref[0] * invf_ref[...]
        cs, sn = jnp.cos(ang), jnp.sin(ang)
        unrope = lambda gr: gr * cs - _rot(gr * sn)
        lane = lax.broadcasted_iota(jnp.int32, cs.shape, 1)
        dkr = dk_ref[0, :, 128:256]
        for h in range(1, HEADS):
            dkr = dkr + dk_ref[0, :, 256 * h + 128:256 * h + 256]
        dkr = jnp.where(lane < 64, unrope(dkr), 0.0)
        dmisc = dmisc + jnp.concatenate([dkr, jnp.zeros_like(dkr)], axis=1)
        dqp = jnp.concatenate(
            [blk for h in range(HEADS)
             for blk in (dq_ref[0, :, 256 * h:256 * h + 128], unrope(dq_ref[0, :, 256 * h + 128:256 * h + 256]))],
            axis=1).astype(BF16)
        dkvp = jnp.concatenate([dk_ref[0, :, 256 * h:256 * h + 128] for h in range(HEADS)] + [dva_ref[0]],
                               axis=1).astype(BF16)
        cqn, cq_nx, cq_rstd = _rms(pp_ref[0, :, CQ0:CQ0 + 256], gq_ref[...], 256)
        ckvn, ckv_nx, ckv_rstd = _rms(pp_ref[0, :, CKV0:CKV0 + 128], gkv_ref[...], 128)
        dwuq_ref[...] += _dot_tn(cqn.astype(BF16), dqp)
        dwukv_ref[...] += _dot_tn(ckvn.astype(BF16), dkvp)
        dcq, dgq = _rms_bwd(_dot(dqp, wuqt_ref[...]), cq_nx, cq_rstd, gq_ref[...], 256)
        dckv, dgkv = _rms_bwd(_dot(dkvp, wukvt_ref[...]), ckv_nx, ckv_rstd, gkv_ref[...], 128)
        dgq_ref[...] += dgq
        dgkv_ref[...] += dgkv
        dps = jnp.concatenate([dr, dk_tot, dv, dmisc], axis=1)
        dmu_ref[...] += rowsum(dps * (sh - prw))
        da_ref[0, :, 0:256] = dcq
        da_ref[0, :, 256:384] = dckv
        da_ref[0, :, 384:384 + NRW] = dps

    tok = lambda c: pl.BlockSpec((1, TT, c), lambda b, i: (b, i, 0))
    full = lambda a: _full(a.shape)
    ins = (pp, pos, invf, gq, wuqt, gkv, wukvt, mu, w0, w2p, w2pt, a0, a2p, a2pt, k_k, k_a, bo,
           dq, dk, dva, dr_w, dw_w, dkp_w, dv_w, dal_w, dbe_w, dbon)
    in_specs = ([tok(DP), tok(1)] + [full(a) for a in ins[2:17]] + [tok(1024), tok(1024), tok(512)]
                + [tok(512)] * 6 + [tok(1536)])
    shp = lambda *s: jax.ShapeDtypeStruct(s, F32)
    out_shape = [shp(bsz, t, 384 + NRW), shp(256, 1024), shp(128, 1024), shp(256, 512), shp(256, 512),
                 shp(1, 256), shp(1, 128), shp(1, NRW), shp(1, 512), shp(1, 512), shp(1, 512), shp(1, 512)]
    out_specs = [tok(384 + NRW)] + [_full(s.shape) for s in out_shape[1:]]
    return pl.pallas_call(
        body, name="pre_bwd_a", grid=(bsz, nt), out_shape=out_shape, in_specs=in_specs, out_specs=out_specs,
        scratch_shapes=[pltpu.VMEM((8, NRW), F32)],
        compiler_params=_cparams(("arbitrary", "arbitrary")),
    )(*ins)


def _pre_bwd_b(x, dh, dz, da, mu, wpt, gpre):
    bsz, t, _ = x.shape
    nt = t // TT
    nblk = t // 8

    def body(x_ref, dh_ref, dz_ref, da_ref, nxt_ref, mu_ref, wpt_ref, gpre_ref, gx_ref, dp_ref, dgpre_ref):
        i = pl.program_id(1)
        first = (pl.program_id(0) == 0) & (i == 0)

        @pl.when(first)
        def _():
            dgpre_ref[...] = jnp.zeros(dgpre_ref.shape, F32)

        mu_v = mu_ref[...]
        dps = da_ref[0, :, 384:384 + NRW]
        nxt = jnp.where(i < nt - 1, nxt_ref[0, 0:1, 384:384 + NRW], 0.0)
        row = lax.broadcasted_iota(jnp.int32, dps.shape, 0)
        up = jnp.where(row == TT - 1, nxt, pltpu.roll(dps, TT - 1, 0))
        dprw = dps * (1.0 - mu_v) + up * mu_v
        dp = jnp.concatenate([dz_ref[0], da_ref[0, :, 0:384], dprw], axis=1).astype(BF16)
        dp_ref[0] = dp
        du = _dot(dp, wpt_ref[...])
        _, nx, rstd = _rms(x_ref[0], gpre_ref[...], D)
        dx, dg = _rms_bwd(du, nx, rstd, gpre_ref[...], D)
        dgpre_ref[...] += dg
        gx_ref[0] = dh_ref[0] + dx

    tok = lambda c: pl.BlockSpec((1, TT, c), lambda b, i: (b, i, 0))
    nxt_spec = pl.BlockSpec((1, 8, 384 + NRW), lambda b, i: (b, jnp.minimum((i + 1) * (TT // 8), nblk - 1), 0))
    ins = (x, dh, dz, da, da, mu, wpt, gpre)
    return pl.pallas_call(
        body, name="pre_bwd_b", grid=(bsz, nt),
        out_shape=[jax.ShapeDtypeStruct((bsz, t, D), F32), jax.ShapeDtypeStruct((bsz, t, DP), BF16),
                   jax.ShapeDtypeStruct((1, D), F32)],
        in_specs=[tok(D), tok(D), tok(1024), tok(384 + NRW), nxt_spec, _full(mu.shape), _full(wpt.shape),
                  _full(gpre.shape)],
        out_specs=[tok(D), tok(DP), _full((1, D))],
        compiler_params=_cparams(("arbitrary", "arbitrary")),
    )(*ins)


def _tn_matmul(a, b, bn, name, bk=512):
    kdim, m = a.shape
    _, n = b.shape
    nk = kdim // bk

    def body(a_ref, b_ref, o_ref):
        @pl.when(pl.program_id(1) == 0)
        def _():
            o_ref[...] = jnp.zeros(o_ref.shape, F32)

        o_ref[...] += _dot_tn(a_ref[...], b_ref[...])

    return pl.pallas_call(
        body, name=name, grid=(n // bn, nk),
        out_shape=jax.ShapeDtypeStruct((m, n), F32),
        in_specs=[pl.BlockSpec((bk, m), lambda j, kk: (kk, 0)), pl.BlockSpec((bk, bn), lambda j, kk: (kk, j))],
        out_specs=pl.BlockSpec((m, bn), lambda j, kk: (0, j)),
        compiler_params=_cparams(("parallel", "arbitrary")),
    )(a, b)


SHARDED = ("w_in", "mla_w_uq", "mla_w_ukv", "rw_w2", "rw_a2", "w_out")
SMALL = ("norm_pre_g", "mla_q_norm_g", "mla_kv_norm_g", "rw_mu", "rw_w0", "rw_a0", "rw_k_k", "rw_k_a", "rw_r_k",
         "rw_ln_g", "rw_ln_b", "norm_post_g")
WEIGHTS = ("norm_pre_g", "w_in", "mla_q_norm_g", "mla_w_uq", "mla_kv_norm_g", "mla_w_ukv", "rw_mu", "rw_w0", "rw_w2",
           "rw_a0", "rw_a2", "rw_k_k", "rw_k_a", "rw_r_k", "rw_ln_g", "rw_ln_b", "w_out", "norm_post_g")


def _pack_small(d):
    flat = jnp.concatenate([d[n].reshape(1, -1) for n in SMALL], axis=1)
    return jnp.pad(flat, ((0, 0), (0, SMALL_ROWS * LANES - flat.shape[1]))).reshape(SMALL_ROWS, LANES)


def _unpack_small(packed, like):
    flat = packed.reshape(1, -1)
    out, at = {}, 0
    for n in SMALL:
        size = int(np.prod(like[n].shape))
        out[n] = flat[:, at:at + size].reshape(like[n].shape)
        at += size
    return out


def _pack_shard(d):
    return jnp.concatenate([d[n].reshape(-1, LANES) for n in SHARDED], axis=0)


def _unpack_shard(packed, like):
    out, at = {}, 0
    for n, rows in zip(SHARDED, PACK_ROWS):
        out[n] = packed[at:at + rows].reshape(like[n].shape)
        at += rows
    return out


def _constants():
    bo = np.kron(np.eye(2, dtype=np.float32), np.ones((64, 64), np.float32))
    inv = ROPE_THETA ** (-np.arange(0, 64, 2, dtype=np.float32) / 64)
    invf = np.concatenate([inv, inv, np.zeros(64, np.float32)]).astype(np.float32)[None, :]
    return jnp.asarray(bo), jnp.asarray(invf)


def kernel(x, positions, norm_pre_g, w_in, mla_q_norm_g, mla_w_uq, mla_kv_norm_g, mla_w_ukv, rw_mu, rw_w0, rw_w2, rw_a0, rw_a2, rw_k_k, rw_k_a, rw_r_k, rw_ln_g, rw_ln_b, w_out, norm_post_g, loss_target, m_norm_pre_g, m_w_in, m_mla_q_norm_g, m_mla_w_uq, m_mla_kv_norm_g, m_mla_w_ukv, m_rw_mu, m_rw_w0, m_rw_w2, m_rw_a0, m_rw_a2, m_rw_k_k, m_rw_k_a, m_rw_r_k, m_rw_ln_g, m_rw_ln_b, m_w_out, m_norm_post_g, v_norm_pre_g, v_w_in, v_mla_q_norm_g, v_mla_w_uq, v_mla_kv_norm_g, v_mla_w_ukv, v_rw_mu, v_rw_w0, v_rw_w2, v_rw_a0, v_rw_a2, v_rw_k_k, v_rw_k_a, v_rw_r_k, v_rw_ln_g, v_rw_ln_b, v_w_out, v_norm_post_g):
    wts = dict(norm_pre_g=norm_pre_g, w_in=w_in, mla_q_norm_g=mla_q_norm_g, mla_w_uq=mla_w_uq,
               mla_kv_norm_g=mla_kv_norm_g, mla_w_ukv=mla_w_ukv, rw_mu=rw_mu, rw_w0=rw_w0, rw_w2=rw_w2, rw_a0=rw_a0,
               rw_a2=rw_a2, rw_k_k=rw_k_k, rw_k_a=rw_k_a, rw_r_k=rw_r_k, rw_ln_g=rw_ln_g, rw_ln_b=rw_ln_b, w_out=w_out,
               norm_post_g=norm_post_g)
    mom_m = dict(norm_pre_g=m_norm_pre_g, w_in=m_w_in, mla_q_norm_g=m_mla_q_norm_g, mla_w_uq=m_mla_w_uq,
                 mla_kv_norm_g=m_mla_kv_norm_g, mla_w_ukv=m_mla_w_ukv, rw_mu=m_rw_mu, rw_w0=m_rw_w0, rw_w2=m_rw_w2,
                 rw_a0=m_rw_a0, rw_a2=m_rw_a2, rw_k_k=m_rw_k_k, rw_k_a=m_rw_k_a, rw_r_k=m_rw_r_k, rw_ln_g=m_rw_ln_g,
                 rw_ln_b=m_rw_ln_b, w_out=m_w_out, norm_post_g=m_norm_post_g)
    mom_v = dict(norm_pre_g=v_norm_pre_g, w_in=v_w_in, mla_q_norm_g=v_mla_q_norm_g, mla_w_uq=v_mla_w_uq,
                 mla_kv_norm_g=v_mla_kv_norm_g, mla_w_ukv=v_mla_w_ukv, rw_mu=v_rw_mu, rw_w0=v_rw_w0, rw_w2=v_rw_w2,
                 rw_a0=v_rw_a0, rw_a2=v_rw_a2, rw_k_k=v_rw_k_k, rw_k_a=v_rw_k_a, rw_r_k=v_rw_r_k, rw_ln_g=v_rw_ln_g,
                 rw_ln_b=v_rw_ln_b, w_out=v_w_out, norm_post_g=v_norm_post_g)
    bsz, t, _ = x.shape
    bo, invf = _constants()
    c_idx = lax.axis_index("c")
    shard_idx = 2 * lax.axis_index("x") + lax.axis_index("y")

    g_in, g_uq, g_ukv, g_w2, g_a2, g_out = _ag_weights([wts[n][0] for n in SHARDED])
    w_in_f = jnp.transpose(g_in, (1, 0, 2)).reshape(D, D_IN)
    wp = jnp.concatenate([w_in_f[:, 2112:3136], w_in_f[:, 0:384], w_in_f[:, 448:1984], w_in_f[:, 384:448],
                          w_in_f[:, 1984:2112], jnp.zeros((D, 64), BF16)], axis=1)
    wuq = jnp.pad(jnp.transpose(g_uq, (1, 0, 2)).reshape(256, HEADS, 192), ((0, 0), (0, 0), (0, 64))).reshape(256, 1024)
    wukv = jnp.transpose(jnp.transpose(g_ukv, (1, 0, 2)).reshape(128, HEADS, 2, 128), (0, 2, 1, 3)).reshape(128, 1024)
    w2 = jnp.transpose(g_w2, (1, 0, 2)).reshape(64, RW)
    a2 = jnp.transpose(g_a2, (1, 0, 2)).reshape(64, RW)
    w2p = jnp.pad(w2, ((64, 128), (0, 0)))
    a2p = jnp.pad(a2, ((128, 64), (0, 0)))
    wo = g_out.reshape(D, D)
    mu = jnp.concatenate([rw_mu[:, 0:1536], jnp.zeros((1, 64), F32), rw_mu[:, 1536:1664], jnp.zeros((1, 64), F32)],
                         axis=1)
    r_k = rw_r_k.reshape(1, RW)
    pos = positions.astype(F32)[:, :, None]

    (u, pp, q_att, k_att, v_att, r, w, kp, v, al, be) = _pre_fwd(
        x, pos, invf, norm_pre_g, wp, mla_q_norm_g, wuq, mla_kv_norm_g, wukv, mu, rw_w0, w2p, rw_a0, a2p, rw_k_k,
        rw_k_a, bo)
    o, lse = _attn_fwd(q_att, k_att, v_att)
    rw_k = [_to_k(a) for a in (r, w, kp, al, be)]
    v_v = _to_v(v)
    yw_v, ck = _wkv_fwd(*rw_k, v_v)
    yw = _from_v(yw_v, bsz)

    (dh, dz, dym, dyw, dbon, loss_acc, d_wo, d_gpost, d_lng, d_lnb, d_rk) = _post(
        x, loss_target, pp, o, yw, r, kp, v, rw_ln_g, rw_ln_b, r_k, wo, wo.T, norm_post_g, bo)
    loss = lax.psum(loss_acc[0, 0], ("x", "y", "c"))

    d_k = _wkv_bwd(*rw_k, v_v, _to_v(dyw), ck, bsz)
    dr_w, dw_w, dkp_w, dal_w, dbe_w = (_from_k(a, bsz) for a in d_k[:5])
    dwkv = (dr_w, dw_w, dkp_w, _from_v(d_k[5], bsz), dal_w, dbe_w)
    dq, dk, dva = _attn_bwd(q_att, k_att, v_att, o, lse, dym)

    (da, d_wuq, d_wukv, d_w2p, d_a2p, d_gq, d_gkv, d_mu, d_w0, d_a0, d_kk, d_ka) = _pre_bwd_a(
        pp, pos, invf, (mla_q_norm_g, wuq.T, mla_kv_norm_g, wukv.T), mu, rw_w0, w2p, w2p.T, rw_a0, a2p, a2p.T,
        rw_k_k, rw_k_a, bo, dq, dk, dva, dwkv, dbon)
    grad_x, dpb, d_gpre = _pre_bwd_b(x, dh, dz, da, mu, wp.T, norm_pre_g)
    d_wp = _tn_matmul(u.reshape(bsz * t, D), dpb.reshape(bsz * t, DP), 640, "dw_in")

    full_g = {
        "w_in": jnp.concatenate([d_wp[:, 1024:1408], d_wp[:, 2944:3008], d_wp[:, 1408:2944], d_wp[:, 3008:3136],
                                 d_wp[:, 0:1024]], axis=1),
        "mla_w_uq": d_wuq.reshape(256, HEADS, 256)[:, :, :192].reshape(256, 768),
        "mla_w_ukv": jnp.transpose(d_wukv.reshape(128, 2, HEADS, 128), (0, 2, 1, 3)).reshape(128, 1024),
        "rw_w2": d_w2p[64:128],
        "rw_a2": d_a2p[128:192],
        "w_out": d_wo,
    }
    small_g = {
        "norm_pre_g": d_gpre, "mla_q_norm_g": d_gq, "mla_kv_norm_g": d_gkv,
        "rw_mu": jnp.concatenate([d_mu[:, 0:1536], d_mu[:, 1600:1728]], axis=1),
        "rw_w0": d_w0, "rw_a0": d_a0, "rw_k_k": d_kk, "rw_k_a": d_ka, "rw_r_k": d_rk, "rw_ln_g": d_lng,
        "rw_ln_b": d_lnb, "norm_post_g": d_gpost,
    }

    def by_shard(name, g):
        if name == "w_out":
            return g.reshape(N_SHARD, -1, LANES)
        rows, cols = g.shape
        return jnp.transpose(g.reshape(rows, N_SHARD, cols // N_SHARD), (1, 0, 2)).reshape(N_SHARD, -1, LANES)

    packed = jnp.concatenate([by_shard(n, full_g[n]) for n in SHARDED], axis=1)
    halves = packed.reshape(N_SHARD, 2, HALF, LANES)
    keep = lax.dynamic_index_in_dim(halves, c_idx, 1, keepdims=False)
    give = lax.dynamic_index_in_dim(halves, 1 - c_idx, 1, keepdims=False)
    got = _rs_pair_exchange(give)
    pair_sum = _add_n([keep.reshape(-1, LANES), got.reshape(-1, LANES)], "rs_pair_sum").reshape(N_SHARD, HALF, LANES)
    arrived = _rs_chip_exchange(pair_sum)
    own = lax.dynamic_index_in_dim(pair_sum, shard_idx, 0, keepdims=False)
    reduced_half = _add_n([own, arrived[0], arrived[1], arrived[2]], "rs_chip_sum")
    g_shard = _rs_pair_gather(reduced_half).reshape(PACK_TOTAL, LANES)

    g_small = _small_allreduce(_pack_small(small_g))

    shard_like = {n: wts[n][0] for n in SHARDED}
    d_sh, nm_sh, nv_sh = _adamw(_pack_shard({n: wts[n][0] for n in SHARDED}), g_shard,
                                _pack_shard({n: mom_m[n][0] for n in SHARDED}),
                                _pack_shard({n: mom_v[n][0] for n in SHARDED}), "adamw_sharded", 568)
    d_sm, nm_sm, nv_sm = _adamw(_pack_small(wts), g_small, _pack_small(mom_m), _pack_small(mom_v), "adamw_small",
                                SMALL_ROWS)

    def unpack(sh, sm):
        out = {n: a[None] for n, a in _unpack_shard(sh, shard_like).items()}
        out.update(_unpack_small(sm, wts))
        return out

    grads, deltas, new_m, new_v = unpack(g_shard, g_small), unpack(d_sh, d_sm), unpack(nm_sh, nm_sm), unpack(nv_sh, nv_sm)
    return (loss, grad_x, *[grads[n] for n in WEIGHTS], *[deltas[n] for n in WEIGHTS],
            *[new_m[n] for n in WEIGHTS], *[new_v[n] for n in WEIGHTS])
```

```python
import functools

import numpy as np
import jax
import jax.numpy as jnp
from jax import lax
from jax.experimental import pallas as pl
from jax.experimental.pallas import tpu as pltpu

F32, BF16 = jnp.float32, jnp.bfloat16
HIGHEST = lax.Precision.HIGHEST
MESH = pl.DeviceIdType.MESH

D = 1024
HEADS = 4
RW = 512
NORM_EPS = 1e-6
GN_EPS = 64e-5
ROPE_THETA = 10000.0
SCALE = (128 + 64) ** -0.5
D_IN = 3136
LR, B1, B2, ADAM_EPS, WD, STEP = 0.001, 0.9, 0.999, 1e-08, 0.01, 10

Z0, CQ0, CKV0, RW0, DP = 0, 1024, 1280, 1408, 3200
NRW = DP - RW0

LANES = 128
SUBLANES = 8
VMEM_LIMIT = 56 * 1024 * 1024

TT = 256
TQ = 256

N_SHARD = 4
PACK_ROWS = (1024 * 784 // 128, 256 * 192 // 128, 128 * 256 // 128, 64, 64, 256 * 1024 // 128)
PACK_TOTAL = sum(PACK_ROWS)
HALF = PACK_TOTAL // 2
SMALL_ROWS = 64


def _cparams(sem=None):
    return pltpu.CompilerParams(dimension_semantics=sem, vmem_limit_bytes=VMEM_LIMIT)


def _full(shape):
    n = len(shape)
    return pl.BlockSpec(shape, lambda *_: (0,) * n)


def _dot(a, b):
    return jnp.dot(a, b, preferred_element_type=F32)


def _dot_nt(a, b):
    return lax.dot_general(a, b, (((1,), (1,)), ((), ())), preferred_element_type=F32)


def _dot_tn(a, b):
    return lax.dot_general(a, b, (((0,), (0,)), ((), ())), preferred_element_type=F32)


def _seg(x, bo):
    parts = [jnp.dot(x[:, LANES * i:LANES * (i + 1)], bo, precision=HIGHEST, preferred_element_type=F32)
             for i in range(x.shape[1] // LANES)]
    return parts[0] if len(parts) == 1 else jnp.concatenate(parts, axis=1)


def _rms(x, g, n):
    rstd = lax.rsqrt(jnp.sum(x * x, axis=-1, keepdims=True) * (1.0 / n) + NORM_EPS)
    nx = x * rstd
    return nx * g, nx, rstd


def _rms_bwd(dy, nx, rstd, g, n):
    dn = dy * g
    dx = rstd * (dn - nx * (jnp.sum(dn * nx, axis=-1, keepdims=True) * (1.0 / n)))
    return dx, jnp.sum(dy * nx, axis=0, keepdims=True)


def _rot(x):
    lane = lax.broadcasted_iota(jnp.int32, x.shape, 1)
    return jnp.where((lane % 64) < 32, -pltpu.roll(x, x.shape[1] - 32, 1), pltpu.roll(x, 32, 1))


def _sigmoid(x):
    return 1.0 / (1.0 + jnp.exp(-x))


def _softplus(x):
    return jnp.maximum(x, 0.0) + jnp.log(1.0 + jnp.exp(-jnp.abs(x)))


def _rw_gates(ps, w0, w2p, a0, a2p, k_k, k_a, bo):
    r, k, v, misc = ps[:, 0:512], ps[:, 512:1024], ps[:, 1024:1536], ps[:, 1536:NRW]
    th = jnp.tanh(misc)
    wpre = w0 + _dot(th.astype(BF16), w2p)
    e = jnp.exp(-_softplus(-wpre) - 0.5)
    w = jnp.exp(-e)
    a = _sigmoid(a0 + _dot(misc.astype(BF16), a2p))
    m = k * k_k
    nrm = jnp.maximum(jnp.sqrt(_seg(m * m, bo)), 1e-12)
    kk = m / nrm
    kp = k * (1.0 + (a - 1.0) * k_a)
    return dict(r=r, k=k, v=v, misc=misc, th=th, wpre=wpre, e=e, w=w, a=a, nrm=nrm, kk=kk, kp=kp)


def _shift_mix(prw, prev_row, mu):
    row = lax.broadcasted_iota(jnp.int32, prw.shape, 0)
    sh = jnp.where(row == 0, prev_row, pltpu.roll(prw, 1, 0))
    return prw + (sh - prw) * mu, sh


def _ag_weights(shards):
    n = len(shards)

    def body(*refs):
        ins, outs = refs[:n], refs[n:2 * n]
        send_sems, recv_sems = refs[2 * n], refs[2 * n + 1]
        x, y, c = lax.axis_index("x"), lax.axis_index("y"), lax.axis_index("c")
        mine = 2 * x + y
        for w in range(n):
            outs[w][mine] = ins[w][...].astype(BF16)
        flips = ((1, 0), (0, 1), (1, 1))

        def copy(w, k):
            fx, fy = flips[k]
            return pltpu.make_async_remote_copy(
                src_ref=outs[w].at[mine], dst_ref=outs[w].at[mine],
                send_sem=send_sems.at[w * 3 + k], recv_sem=recv_sems.at[w * 3 + k],
                device_id=(x ^ fx, y ^ fy, c), device_id_type=MESH)

        def arrival(w, k):
            fx, fy = flips[k]
            theirs = 2 * (x ^ fx) + (y ^ fy)
            return pltpu.make_async_remote_copy(
                src_ref=outs[w].at[theirs], dst_ref=outs[w].at[theirs],
                send_sem=send_sems.at[w * 3 + k], recv_sem=recv_sems.at[w * 3 + k],
                device_id=(x ^ fx, y ^ fy, c), device_id_type=MESH)

        for w in range(n):
            for k in range(3):
                copy(w, k).start()
        for w in range(n):
            for k in range(3):
                arrival(w, k).wait_recv()
        for w in range(n):
            for k in range(3):
                copy(w, k).wait_send()

    vm = pl.BlockSpec(memory_space=pltpu.VMEM)
    return pl.pallas_call(
        body, name="ag_weights",
        out_shape=[jax.ShapeDtypeStruct((N_SHARD,) + s.shape, BF16) for s in shards],
        in_specs=[vm] * n, out_specs=[vm] * n,
        scratch_shapes=[pltpu.SemaphoreType.DMA((3 * n,)), pltpu.SemaphoreType.DMA((3 * n,))],
        compiler_params=pltpu.CompilerParams(vmem_limit_bytes=VMEM_LIMIT),
    )(*shards)


def _rs_pair_exchange(send_half):
    def body(src_ref, dst_ref, send_sem, recv_sem):
        x, y, c = lax.axis_index("x"), lax.axis_index("y"), lax.axis_index("c")
        cp = pltpu.make_async_remote_copy(src_ref=src_ref, dst_ref=dst_ref, send_sem=send_sem, recv_sem=recv_sem,
                                          device_id=(x, y, 1 - c), device_id_type=MESH)
        cp.start()
        cp.wait()

    hbm = pl.BlockSpec(memory_space=pl.ANY)
    return pl.pallas_call(
        body, name="rs_pair_exchange",
        out_shape=jax.ShapeDtypeStruct(send_half.shape, send_half.dtype),
        in_specs=[hbm], out_specs=hbm,
        scratch_shapes=[pltpu.SemaphoreType.DMA, pltpu.SemaphoreType.DMA],
    )(send_half)


def _rs_chip_exchange(part):
    def body(src_ref, dst_ref, send_sems, recv_sems):
        x, y, c = lax.axis_index("x"), lax.axis_index("y"), lax.axis_index("c")
        flips = ((1, 0), (0, 1), (1, 1))
        cps = []
        for k, (fx, fy) in enumerate(flips):
            theirs = 2 * (x ^ fx) + (y ^ fy)
            cps.append(pltpu.make_async_remote_copy(
                src_ref=src_ref.at[theirs], dst_ref=dst_ref.at[k],
                send_sem=send_sems.at[k], recv_sem=recv_sems.at[k],
                device_id=(x ^ fx, y ^ fy, c), device_id_type=MESH))
        for cp in cps:
            cp.start()
        for cp in cps:
            cp.wait()

    hbm = pl.BlockSpec(memory_space=pl.ANY)
    return pl.pallas_call(
        body, name="rs_chip_exchange",
        out_shape=jax.ShapeDtypeStruct((3,) + part.shape[1:], part.dtype),
        in_specs=[hbm], out_specs=hbm,
        scratch_shapes=[pltpu.SemaphoreType.DMA((3,)), pltpu.SemaphoreType.DMA((3,))],
    )(part)


def _rs_pair_gather(half):
    def body(src_ref, dst_ref, send_sem, recv_sem, local_sem):
        x, y, c = lax.axis_index("x"), lax.axis_index("y"), lax.axis_index("c")
        own = pltpu.make_async_copy(src_ref, dst_ref.at[c], local_sem)
        own.start()
        cp = pltpu.make_async_remote_copy(src_ref=src_ref, dst_ref=dst_ref.at[c], send_sem=send_sem, recv_sem=recv_sem,
                                          device_id=(x, y, 1 - c), device_id_type=MESH)
        cp.start()
        arrival = pltpu.make_async_remote_copy(src_ref=src_ref, dst_ref=dst_ref.at[1 - c], send_sem=send_sem,
                                               recv_sem=recv_sem, device_id=(x, y, 1 - c), device_id_type=MESH)
        arrival.wait_recv()
        cp.wait_send()
        own.wait()

    hbm = pl.BlockSpec(memory_space=pl.ANY)
    return pl.pallas_call(
        body, name="rs_pair_gather",
        out_shape=jax.ShapeDtypeStruct((2,) + half.shape, half.dtype),
        in_specs=[hbm], out_specs=hbm,
        scratch_shapes=[pltpu.SemaphoreType.DMA, pltpu.SemaphoreType.DMA, pltpu.SemaphoreType.DMA],
    )(half)


def _small_allreduce(vec):
    def body(in_ref, out_ref, recv, send_sems, recv_sems):
        x, y, c = lax.axis_index("x"), lax.axis_index("y"), lax.axis_index("c")
        me = 4 * x + 2 * y + c
        cps = []
        for k in range(1, 8):
            fx, fy, fc = (k >> 2) & 1, (k >> 1) & 1, k & 1
            cps.append(pltpu.make_async_remote_copy(
                src_ref=in_ref, dst_ref=recv.at[k - 1],
                send_sem=send_sems.at[k - 1], recv_sem=recv_sems.at[k - 1],
                device_id=(x ^ fx, y ^ fy, c ^ fc), device_id_type=MESH))
        for cp in cps:
            cp.start()
        for cp in cps:
            cp.wait()
        acc = jnp.zeros(in_ref.shape, F32)
        for j in range(8):
            slot = jnp.maximum((me ^ j) - 1, 0)
            acc = acc + jnp.where(me == j, in_ref[...], recv[slot])
        out_ref[...] = acc

    vm = pl.BlockSpec(memory_space=pltpu.VMEM)
    return pl.pallas_call(
        body, name="small_allreduce",
        out_shape=jax.ShapeDtypeStruct(vec.shape, F32),
        in_specs=[vm], out_specs=vm,
        scratch_shapes=[pltpu.VMEM((7,) + vec.shape, F32), pltpu.SemaphoreType.DMA((7,)),
                        pltpu.SemaphoreType.DMA((7,))],
    )(vec)


def _add_n(arrs, name, rows=568):
    n = len(arrs)
    r = arrs[0].shape[0]

    def body(*refs):
        acc = refs[0][...]
        for k in range(1, n):
            acc = acc + refs[k][...]
        refs[n][...] = acc

    spec = pl.BlockSpec((rows, LANES), lambda i: (i, 0))
    return pl.pallas_call(
        body, name=name, grid=(r // rows,),
        out_shape=jax.ShapeDtypeStruct(arrs[0].shape, F32),
        in_specs=[spec] * n, out_specs=spec,
        compiler_params=_cparams(("parallel",)),
    )(*arrs)


def _adamw(w, g, m, v, name, rows):
    r = w.shape[0]

    def body(w_ref, g_ref, m_ref, v_ref, d_ref, nm_ref, nv_ref):
        gg = g_ref[...]
        nm = B1 * m_ref[...] + (1.0 - B1) * gg
        nv = B2 * v_ref[...] + (1.0 - B2) * (gg * gg)
        m_hat = nm / (1.0 - B1 ** STEP)
        v_hat = nv / (1.0 - B2 ** STEP)
        d_ref[...] = -LR * (m_hat / (jnp.sqrt(v_hat) + ADAM_EPS) + WD * w_ref[...])
        nm_ref[...] = nm
        nv_ref[...] = nv

    spec = pl.BlockSpec((rows, LANES), lambda i: (i, 0))
    sds = jax.ShapeDtypeStruct(w.shape, F32)
    return pl.pallas_call(
        body, name=name, grid=(r // rows,),
        out_shape=[sds, sds, sds],
        in_specs=[spec] * 4, out_specs=[spec] * 3,
        compiler_params=_cparams(("parallel",)),
    )(w, g, m, v)


def _pre_fwd(x, pos, invf, gpre, wp, gq, wuq, gkv, wukv, mu, w0, w2p, a0, a2p, k_k, k_a, bo):
    bsz, t, _ = x.shape
    nt = t // TT

    def body(x_ref, pos_ref, invf_ref, gpre_ref, wp_ref, gq_ref, wuq_ref, gkv_ref, wukv_ref, mu_ref, w0_ref,
             w2p_ref, a0_ref, a2p_ref, kk_ref, ka_ref, bo_ref,
             u_ref, pp_ref, q_ref, k_ref, v_ref, r_o, w_o, kp_o, vv_o, al_o, be_o, carry):
        i = pl.program_id(1)
        u, _, _ = _rms(x_ref[0], gpre_ref[...], D)
        ub = u.astype(BF16)
        u_ref[0] = ub
        p = _dot(ub, wp_ref[...])
        pp_ref[0] = p
        prw = p[:, RW0:DP]

        @pl.when(i == 0)
        def _():
            carry[...] = jnp.zeros(carry.shape, F32)

        ps, _ = _shift_mix(prw, carry[7:8, :], mu_ref[...])
        carry[...] = prw[TT - 8:TT, :]

        g = _rw_gates(ps, w0_ref[...], w2p_ref[...], a0_ref[...], a2p_ref[...], kk_ref[...], ka_ref[...],
                      bo_ref[...])
        r_o[0] = g["r"]
        w_o[0] = g["w"]
        kp_o[0] = g["kp"]
        vv_o[0] = g["v"]
        al_o[0] = -g["kk"]
        be_o[0] = g["kk"] * g["a"]

        cqn, _, _ = _rms(p[:, CQ0:CQ0 + 256], gq_ref[...], 256)
        q = _dot(cqn.astype(BF16), wuq_ref[...])
        ckvn, _, _ = _rms(p[:, CKV0:CKV0 + 128], gkv_ref[...], 128)
        kv = _dot(ckvn.astype(BF16), wukv_ref[...])
        ang = pos_ref[0] * invf_ref[...]
        cs, sn = jnp.cos(ang), jnp.sin(ang)
        lane = lax.broadcasted_iota(jnp.int32, cs.shape, 1)
        kr = ps[:, 1536:1536 + LANES]
        kr = jnp.where(lane < 64, kr * cs + _rot(kr) * sn, 0.0).astype(BF16)
        for h in range(HEADS):
            qr = q[:, 256 * h + 128:256 * h + 256]
            q_ref[0, :, 256 * h:256 * h + 128] = q[:, 256 * h:256 * h + 128].astype(BF16)
            q_ref[0, :, 256 * h + 128:256 * h + 256] = (qr * cs + _rot(qr) * sn).astype(BF16)
            k_ref[0, :, 256 * h:256 * h + 128] = kv[:, 128 * h:128 * h + 128].astype(BF16)
            k_ref[0, :, 256 * h + 128:256 * h + 256] = kr
        v_ref[0] = kv[:, 512:1024].astype(BF16)

    tok = lambda c: pl.BlockSpec((1, TT, c), lambda b, i: (b, i, 0))
    full = lambda a: _full(a.shape)
    ins = (x, pos, invf, gpre, wp, gq, wuq, gkv, wukv, mu, w0, w2p, a0, a2p, k_k, k_a, bo)
    in_specs = [tok(D), tok(1)] + [full(a) for a in ins[2:]]
    sd = lambda c, dt: jax.ShapeDtypeStruct((bsz, t, c), dt)
    out_shape = [sd(D, BF16), sd(DP, F32), sd(1024, BF16), sd(1024, BF16), sd(512, BF16)] + [sd(RW, F32)] * 6
    out_specs = [tok(D), tok(DP), tok(1024), tok(1024), tok(512)] + [tok(RW)] * 6
    return pl.pallas_call(
        body, name="pre_fwd", grid=(bsz, nt), out_shape=out_shape, in_specs=in_specs, out_specs=out_specs,
        scratch_shapes=[pltpu.VMEM((8, NRW), F32)],
        compiler_params=_cparams(("arbitrary", "arbitrary")),
    )(*ins)


def _attn_fwd(q, k, v):
    bsz, t, _ = q.shape
    nq = t // TQ

    def body(q_ref, k_ref, v_ref, o_ref, lse_ref):
        i = pl.program_id(2)
        qt = q_ref[0]
        row = lax.broadcasted_iota(jnp.int32, (TQ, TQ), 0) + i * TQ
        col0 = lax.broadcasted_iota(jnp.int32, (TQ, TQ), 1)

        def step(j, carry):
            m, l, acc = carry
            at = pl.ds(pl.multiple_of(j * TQ, TQ), TQ)
            s = _dot_nt(qt, k_ref[0, at, :]) * SCALE
            s = jnp.where(col0 + j * TQ <= row, s, -1e30)
            mn = jnp.maximum(m, jnp.max(s, axis=1, keepdims=True))
            p = jnp.exp(s - mn)
            al = jnp.exp(m - mn)
            l = al * l + jnp.sum(p, axis=1, keepdims=True)
            acc = al * acc + _dot(p.astype(BF16), v_ref[0, at, :])
            return mn, l, acc

        m, l, acc = lax.fori_loop(
            0, i + 1, step,
            (jnp.full((TQ, 1), -1e30, F32), jnp.zeros((TQ, 1), F32), jnp.zeros((TQ, LANES), F32)))
        o_ref[0] = acc / l
        lse_ref[0, 0] = jnp.broadcast_to(m + jnp.log(l), (TQ, LANES))

    return pl.pallas_call(
        body, name="attn_fwd", grid=(bsz, HEADS, nq),
        out_shape=[jax.ShapeDtypeStruct((bsz, t, 512), F32), jax.ShapeDtypeStruct((bsz, HEADS, t, LANES), F32)],
        in_specs=[pl.BlockSpec((1, TQ, 256), lambda b, h, i: (b, i, h)),
                  pl.BlockSpec((1, t, 256), lambda b, h, i: (b, 0, h)),
                  pl.BlockSpec((1, t, LANES), lambda b, h, i: (b, 0, h))],
        out_specs=[pl.BlockSpec((1, TQ, LANES), lambda b, h, i: (b, i, h)),
                   pl.BlockSpec((1, 1, TQ, LANES), lambda b, h, i: (b, h, i, 0))],
        compiler_params=_cparams(("parallel", "parallel", "arbitrary")),
    )(q, k, v)


def _attn_bwd(q, k, v, o, lse, do):
    bsz, t, _ = q.shape
    nq = t // TQ

    def body(q_ref, k_ref, v_ref, o_ref, lse_ref, do_ref, dq_ref, dk_ref, dv_ref, dl_ref):
        def prep(i, _):
            at = pl.ds(pl.multiple_of(i * TQ, TQ), TQ)
            dl_ref[at, :] = jnp.broadcast_to(jnp.sum(do_ref[0, at, :] * o_ref[0, at, :], axis=1, keepdims=True),
                                             (TQ, LANES))
            return 0

        lax.fori_loop(0, nq, prep, 0)
        dq_ref[0] = jnp.zeros((t, 256), F32)
        row0 = lax.broadcasted_iota(jnp.int32, (TQ, TQ), 0)
        col0 = lax.broadcasted_iota(jnp.int32, (TQ, TQ), 1)

        def kv_tile(j, _):
            atk = pl.ds(pl.multiple_of(j * TQ, TQ), TQ)
            kt = k_ref[0, atk, :]
            vt = v_ref[0, atk, :]

            def q_tile(i, carry):
                dk, dv = carry
                atq = pl.ds(pl.multiple_of(i * TQ, TQ), TQ)
                qt = q_ref[0, atq, :]
                dob = do_ref[0, atq, :].astype(BF16)
                s = _dot_nt(qt, kt) * SCALE
                s = jnp.where(col0 + j * TQ <= row0 + i * TQ, s, -1e30)
                p = jnp.exp(s - lse_ref[0, 0, atq, :][:, 0:1])
                dv = dv + _dot_tn(p.astype(BF16), dob)
                dp = _dot_nt(dob, vt)
                ds = (p * (dp - dl_ref[atq, :][:, 0:1]) * SCALE).astype(BF16)
                dk = dk + _dot_tn(ds, qt)
                dq_ref[0, atq, :] += _dot(ds, kt)
                return dk, dv

            dk, dv = lax.fori_loop(j, nq, q_tile, (jnp.zeros((TQ, 256), F32), jnp.zeros((TQ, LANES), F32)))
            dk_ref[0, atk, :] = dk
            dv_ref[0, atk, :] = dv
            return 0

        lax.fori_loop(0, nq, kv_tile, 0)

    s256 = pl.BlockSpec((1, t, 256), lambda b, h: (b, 0, h))
    s128 = pl.BlockSpec((1, t, LANES), lambda b, h: (b, 0, h))
    return pl.pallas_call(
        body, name="attn_bwd", grid=(bsz, HEADS),
        out_shape=[jax.ShapeDtypeStruct((bsz, t, 1024), F32), jax.ShapeDtypeStruct((bsz, t, 1024), F32),
                   jax.ShapeDtypeStruct((bsz, t, 512), F32)],
        in_specs=[s256, s256, s128, s128, pl.BlockSpec((1, 1, t, LANES), lambda b, h: (b, h, 0, 0)), s128],
        out_specs=[s256, s256, s128],
        scratch_shapes=[pltpu.VMEM((t, LANES), F32)],
        compiler_params=_cparams(("parallel", "parallel")),
    )(q, k, v, o, lse, do)


RW_HEADS = 8
TB = 32
CH = 16


def _lane_split(bsz):
    vs = LANES // (bsz * RW_HEADS)
    return vs, 64 // vs


def _to_k(x):
    bsz, t, _ = x.shape
    vs, _ = _lane_split(bsz)
    return jnp.transpose(x.reshape(bsz, t // vs, vs, RW_HEADS, 64), (1, 4, 2, 0, 3)).reshape(t // vs, 64, LANES)


def _from_k(y, bsz):
    vs, _ = _lane_split(bsz)
    tg = y.shape[0]
    return jnp.transpose(y.reshape(tg, 64, vs, bsz, RW_HEADS), (3, 0, 2, 4, 1)).reshape(bsz, tg * vs, RW)


def _to_v(x):
    bsz, t, _ = x.shape
    vs, vq = _lane_split(bsz)
    return jnp.transpose(x.reshape(bsz, t, RW_HEADS, vq, vs), (1, 3, 4, 0, 2)).reshape(t, vq, LANES)


def _from_v(y, bsz):
    t = y.shape[0]
    vs, vq = _lane_split(bsz)
    return jnp.transpose(y.reshape(t, vq, vs, bsz, RW_HEADS), (3, 0, 4, 1, 2)).reshape(bsz, t, RW)


def _ksum(a):
    return jnp.sum(a, axis=0, keepdims=True)


def _fold(a, group):
    sh = LANES // 2
    while sh >= group:
        a = a + pltpu.roll(a, sh, 1)
        sh //= 2
    return a


def _lane_group(shape, group):
    return lax.broadcasted_iota(jnp.int32, shape, 1) // group


def _spread(x, j, group):
    return _fold(jnp.where(_lane_group(x.shape, group) == j, x, 0.0), group)


def _wkv_fwd(r, w, kp, al, be, v):
    t, vq = v.shape[0], v.shape[1]
    vs = 64 // vq
    group = LANES // vs

    def body(r_ref, w_ref, kp_ref, al_ref, be_ref, v_ref, y_ref, ck_ref, st_ref):
        @pl.when(pl.program_id(0) == 0)
        def _():
            st_ref[...] = jnp.zeros(st_ref.shape, F32)

        def steps(g, _):
            @pl.when(g % (CH // vs) == 0)
            def _():
                ck_ref[g // (CH // vs)] = st_ref[...]

            packed = [ref[g] for ref in (r_ref, w_ref, kp_ref, al_ref, be_ref)]
            for j in range(vs):
                rv, wv, kv, av, bv = (_spread(x, j, group) for x in packed)
                vals = v_ref[g * vs + j]
                rows = []
                for q in range(vq):
                    s = st_ref[q]
                    u = _ksum(s * av)
                    s = s * wv + bv * u + kv * vals[q:q + 1]
                    st_ref[q] = s
                    rows.append(_ksum(s * rv))
                y_ref[g * vs + j] = jnp.concatenate(rows, axis=0)
            return 0

        lax.fori_loop(0, TB // vs, steps, 0)

    kspec = pl.BlockSpec((TB // vs, 64, LANES), lambda i: (i, 0, 0))
    vspec = pl.BlockSpec((TB, vq, LANES), lambda i: (i, 0, 0))
    return pl.pallas_call(
        body, name="wkv_fwd", grid=(t // TB,),
        out_shape=[jax.ShapeDtypeStruct((t, vq, LANES), F32), jax.ShapeDtypeStruct((t // CH, vq, 64, LANES), F32)],
        in_specs=[kspec] * 5 + [vspec],
        out_specs=[vspec, pl.BlockSpec((TB // CH, vq, 64, LANES), lambda i: (i, 0, 0, 0))],
        scratch_shapes=[pltpu.VMEM((vq, 64, LANES), F32)],
        compiler_params=_cparams(("arbitrary",)),
    )(r, w, kp, al, be, v)


def _wkv_bwd(r, w, kp, al, be, v, dy, ck):
    t, vq = v.shape[0], v.shape[1]
    vs = 64 // vq
    group = LANES // vs
    n = t // CH
    ng = CH // vs

    def body(r_ref, w_ref, kp_ref, al_ref, be_ref, v_ref, dy_ref, ck_ref,
             dr_ref, dw_ref, dkp_ref, dal_ref, dbe_ref, dv_ref, ds_ref, sp_ref, u_ref):
        @pl.when(pl.program_id(0) == 0)
        def _():
            ds_ref[...] = jnp.zeros(ds_ref.shape, F32)

        sp_ref[0] = ck_ref[0]

        def recompute(g, _):
            packed = [ref[g] for ref in (w_ref, kp_ref, al_ref, be_ref)]
            for j in range(vs):
                tl = g * vs + j
                wv, kv, av, bv = (_spread(x, j, group) for x in packed)
                vals = v_ref[tl]
                rows = []
                for q in range(vq):
                    s = sp_ref[tl, q]
                    u = _ksum(s * av)
                    rows.append(u)
                    sp_ref[tl + 1, q] = s * wv + bv * u + kv * vals[q:q + 1]
                u_ref[tl] = jnp.concatenate(rows, axis=0)
            return 0

        lax.fori_loop(0, ng, recompute, 0)

        def reverse(i, _):
            g = ng - 1 - i
            packed = [ref[g] for ref in (r_ref, w_ref, kp_ref, al_ref, be_ref)]
            grp = _lane_group((64, LANES), group)
            outs = None
            for j in reversed(range(vs)):
                tl = g * vs + j
                rv, wv, kv, av, bv = (_spread(x, j, group) for x in packed)
                vals, dys, us = v_ref[tl], dy_ref[tl], u_ref[tl]
                acc = None
                dvrows = []
                for q in range(vq):
                    s_prev = sp_ref[tl, q]
                    dyq = dys[q:q + 1]
                    ds = ds_ref[q] + rv * dyq
                    c = _ksum(ds * bv)
                    dvrows.append(_ksum(ds * kv))
                    terms = (sp_ref[tl + 1, q] * dyq, ds * s_prev, ds * vals[q:q + 1], s_prev * c, ds * us[q:q + 1])
                    acc = terms if acc is None else tuple(a + b for a, b in zip(acc, terms))
                    ds_ref[q] = ds * wv + av * c
                dv_ref[tl] = jnp.concatenate(dvrows, axis=0)
                summed = [_fold(a, group) for a in acc]
                outs = summed if outs is None else [jnp.where(grp == j, f, o) for f, o in zip(summed, outs)]
            for ref, o in zip((dr_ref, dw_ref, dkp_ref, dal_ref, dbe_ref), outs):
                ref[g] = o
            return 0

        lax.fori_loop(0, ng, reverse, 0)

    kspec = pl.BlockSpec((ng, 64, LANES), lambda i: (n - 1 - i, 0, 0))
    vspec = pl.BlockSpec((CH, vq, LANES), lambda i: (n - 1 - i, 0, 0))
    ksd = jax.ShapeDtypeStruct((t // vs, 64, LANES), F32)
    return pl.pallas_call(
        body, name="wkv_bwd", grid=(n,),
        out_shape=[ksd] * 5 + [jax.ShapeDtypeStruct((t, vq, LANES), F32)],
        in_specs=[kspec] * 5 + [vspec, vspec, pl.BlockSpec((1, vq, 64, LANES), lambda i: (n - 1 - i, 0, 0, 0))],
        out_specs=[kspec] * 5 + [vspec],
        scratch_shapes=[pltpu.VMEM((vq, 64, LANES), F32), pltpu.VMEM((CH + 1, vq, 64, LANES), F32),
                        pltpu.VMEM((CH, vq, LANES), F32)],
        compiler_params=_cparams(("arbitrary",)),
    )(r, w, kp, al, be, v, dy, ck)


def _post(x, tgt, pp, o, yw, r, kp, v, ln_g, ln_b, r_k, wo, wot, gpost, bo):
    bsz, t, _ = x.shape
    nt = t // TT

    def body(x_ref, tgt_ref, z_ref, o_ref, yw_ref, r_ref, kp_ref, v_ref, lng_ref, lnb_ref, rk_ref, wo_ref, wot_ref,
             gpost_ref, bo_ref,
             dh_ref, dz_ref, dym_ref, dyw_ref, dbon_ref, loss_ref, dwo_ref, dgpost_ref, dlng_ref, dlnb_ref, drk_ref):
        first = (pl.program_id(0) == 0) & (pl.program_id(1) == 0)

        @pl.when(first)
        def _():
            for ref in (loss_ref, dwo_ref, dgpost_ref, dlng_ref, dlnb_ref, drk_ref):
                ref[...] = jnp.zeros(ref.shape, F32)

        bo_m = bo_ref[...]
        seg = lambda a: _seg(a, bo_m)
        rowsum = lambda a: jnp.sum(a, axis=0, keepdims=True)
        ywv, rv, kpv, vv = yw_ref[0], r_ref[0], kp_ref[0], v_ref[0]
        ln_g, r_k = lng_ref[...], rk_ref[...]
        mean = seg(ywv) * (1.0 / 64)
        yc = ywv - mean
        rstd = lax.rsqrt(seg(yc * yc) * (1.0 / 64) + GN_EPS)
        yhat = yc * rstd
        sb = seg(rv * kpv * r_k)
        y_rw = yhat * ln_g + lnb_ref[...] + sb * vv
        z = z_ref[0]
        sig = _sigmoid(z)
        sz = z * sig
        ycat = jnp.concatenate([o_ref[0], y_rw], axis=1)
        ycg = (ycat * sz).astype(BF16)
        out = _dot(ycg, wo_ref[...])
        hn, nx, rstd_o = _rms(out, gpost_ref[...], D)
        err = x_ref[0] + hn - tgt_ref[0]
        loss_ref[...] += jnp.sum(err * err) * (0.5 / D)
        dh = err * (1.0 / D)
        dh_ref[0] = dh
        dout, dgp = _rms_bwd(dh, nx, rstd_o, gpost_ref[...], D)
        dgpost_ref[...] += dgp
        doutb = dout.astype(BF16)
        dwo_ref[...] += _dot_tn(ycg, doutb)
        dycg = _dot(doutb, wot_ref[...])
        dz_ref[0] = dycg * ycat * (sig * (1.0 + z * (1.0 - sig)))
        dycat = dycg * sz
        dym_ref[0] = dycat[:, 0:512]
        dy_rw = dycat[:, 512:1024]
        dlnb_ref[...] += rowsum(dy_rw)
        dlng_ref[...] += rowsum(dy_rw * yhat)
        dyhat = dy_rw * ln_g
        dyw_ref[0] = rstd * (dyhat - seg(dyhat) * (1.0 / 64) - yhat * (seg(dyhat * yhat) * (1.0 / 64)))
        dsb = seg(dy_rw * vv)
        drk_ref[...] += rowsum(dsb * rv * kpv)
        dbon_ref[0, :, 0:512] = dsb * kpv * r_k
        dbon_ref[0, :, 512:1024] = dsb * rv * r_k
        dbon_ref[0, :, 1024:1536] = dy_rw * sb

    tok = lambda c: pl.BlockSpec((1, TT, c), lambda b, i: (b, i, 0))
    full = lambda a: _full(a.shape)
    ins = (x, tgt, pp, o, yw, r, kp, v, ln_g, ln_b, r_k, wo, wot, gpost, bo)
    in_specs = [tok(D), tok(D), tok(1024)] + [tok(512)] * 5 + [full(a) for a in ins[8:]]
    sd = lambda c: jax.ShapeDtypeStruct((bsz, t, c), F32)
    vec = lambda c: jax.ShapeDtypeStruct((1, c), F32)
    out_shape = [sd(D), sd(1024), sd(512), sd(512), sd(1536), jax.ShapeDtypeStruct((8, LANES), F32),
                 jax.ShapeDtypeStruct((1024, 1024), F32), vec(D), vec(512), vec(512), vec(512)]
    out_specs = [tok(D), tok(1024), tok(512), tok(512), tok(1536), _full((8, LANES)), _full((1024, 1024)),
                 _full((1, D)), _full((1, 512)), _full((1, 512)), _full((1, 512))]
    return pl.pallas_call(
        body, name="post", grid=(bsz, nt), out_shape=out_shape, in_specs=in_specs, out_specs=out_specs,
        compiler_params=_cparams(("arbitrary", "arbitrary")),
    )(*ins)


def _pre_bwd_a(pp, pos, invf, cqkv_w, mu, w0, w2p, w2pt, a0, a2p, a2pt, k_k, k_a, bo,
               dq, dk, dva, dwkv, dbon):
    gq, wuqt, gkv, wukvt = cqkv_w
    bsz, t, _ = pp.shape
    nt = t // TT
    dr_w, dw_w, dkp_w, dv_w, dal_w, dbe_w = dwkv

    def body(pp_ref, pos_ref, invf_ref, gq_ref, wuqt_ref, gkv_ref, wukvt_ref, mu_ref, w0_ref, w2p_ref, w2pt_ref,
             a0_ref, a2p_ref, a2pt_ref, kk_ref, ka_ref, bo_ref, dq_ref, dk_ref, dva_ref,
             dr_ref, dw_ref, dkp_ref, dv_ref, dal_ref, dbe_ref, dbon_ref,
             da_ref, dwuq_ref, dwukv_ref, dw2p_ref, da2p_ref, dgq_ref, dgkv_ref, dmu_ref, dw0_ref, da0_ref,
             dkk_ref, dka_ref, carry):
        i = pl.program_id(1)
        first = (pl.program_id(0) == 0) & (i == 0)

        @pl.when(first)
        def _():
            for ref in (dwuq_ref, dwukv_ref, dw2p_ref, da2p_ref, dgq_ref, dgkv_ref, dmu_ref, dw0_ref, da0_ref,
                        dkk_ref, dka_ref):
                ref[...] = jnp.zeros(ref.shape, F32)

        bo_m = bo_ref[...]
        rowsum = lambda a: jnp.sum(a, axis=0, keepdims=True)
        prw = pp_ref[0, :, RW0:DP]

        @pl.when(i == 0)
        def _():
            carry[...] = jnp.zeros(carry.shape, F32)

        ps, sh = _shift_mix(prw, carry[7:8, :], mu_ref[...])
        carry[...] = prw[TT - 8:TT, :]
        k_k, k_a = kk_ref[...], ka_ref[...]
        g = _rw_gates(ps, w0_ref[...], w2p_ref[...], a0_ref[...], a2p_ref[...], k_k, k_a, bo_m)
        a, kk, k = g["a"], g["kk"], g["k"]
        dr = dr_ref[0] + dbon_ref[0, :, 0:512]
        dkp = dkp_ref[0] + dbon_ref[0, :, 512:1024]
        dv = dv_ref[0] + dbon_ref[0, :, 1024:1536]
        dbe = dbe_ref[0]
        dkk = dbe * a - dal_ref[0]
        da = dbe * kk + dkp * k * k_a
        dka_ref[...] += rowsum(dkp * k * (a - 1.0))
        dm = (dkk - kk * _seg(dkk * kk, bo_m)) / g["nrm"]
        dkk_ref[...] += rowsum(dm * k)
        dk_tot = dkp * (1.0 + (a - 1.0) * k_a) + dm * k_k
        dapre = da * a * (1.0 - a)
        da0_ref[...] += rowsum(dapre)
        dapb = dapre.astype(BF16)
        da2p_ref[...] += _dot_tn(g["misc"].astype(BF16), dapb)
        dwpre = dw_ref[0] * g["w"] * (-g["e"]) * _sigmoid(-g["wpre"])
        dw0_ref[...] += rowsum(dwpre)
        dwpb = dwpre.astype(BF16)
        th = g["th"]
        dw2p_ref[...] += _dot_tn(th.astype(BF16), dwpb)
        dmisc = _dot(dapb, a2pt_ref[...]) + _dot(dwpb, w2pt_ref[...]) * (1.0 - th * th)
        ang = pos_ref[0] * invf_ref[...]
        cs, sn = jnp.cos(ang), jnp.sin(ang)
        unrope = lambda gr: gr * cs - _rot(gr * sn)
        lane = lax.broadcasted_iota(jnp.int32, cs.shape, 1)
        dkr = dk_ref[0, :, 128:256]
        for h in range(1, HEADS):
            dkr = dkr + dk_ref[0, :, 256 * h + 128:256 * h + 256]
        dkr = jnp.where(lane < 64, unrope(dkr), 0.0)
        dmisc = dmisc + jnp.concatenate([dkr, jnp.zeros_like(dkr)], axis=1)
        dqp = jnp.concatenate(
            [blk for h in range(HEADS)
             for blk in (dq_ref[0, :, 256 * h:256 * h + 128], unrope(dq_ref[0, :, 256 * h + 128:256 * h + 256]))],
            axis=1).astype(BF16)
        dkvp = jnp.concatenate([dk_ref[0, :, 256 * h:256 * h + 128] for h in range(HEADS)] + [dva_ref[0]],
                               axis=1).astype(BF16)
        cqn, cq_nx, cq_rstd = _rms(pp_ref[0, :, CQ0:CQ0 + 256], gq_ref[...], 256)
        ckvn, ckv_nx, ckv_rstd = _rms(pp_ref[0, :, CKV0:CKV0 + 128], gkv_ref[...], 128)
        dwuq_ref[...] += _dot_tn(cqn.astype(BF16), dqp)
        dwukv_ref[...] += _dot_tn(ckvn.astype(BF16), dkvp)
        dcq, dgq = _rms_bwd(_dot(dqp, wuqt_ref[...]), cq_nx, cq_rstd, gq_ref[...], 256)
        dckv, dgkv = _rms_bwd(_dot(dkvp, wukvt_ref[...]), ckv_nx, ckv_rstd, gkv_ref[...], 128)
        dgq_ref[...] += dgq
        dgkv_ref[...] += dgkv
        dps = jnp.concatenate([dr, dk_tot, dv, dmisc], axis=1)
        dmu_ref[...] += rowsum(dps * (sh - prw))
        da_ref[0, :, 0:256] = dcq
        da_ref[0, :, 256:384] = dckv
        da_ref[0, :, 384:384 + NRW] = dps

    tok = lambda c: pl.BlockSpec((1, TT, c), lambda b, i: (b, i, 0))
    full = lambda a: _full(a.shape)
    ins = (pp, pos, invf, gq, wuqt, gkv, wukvt, mu, w0, w2p, w2pt, a0, a2p, a2pt, k_k, k_a, bo,
           dq, dk, dva, dr_w, dw_w, dkp_w, dv_w, dal_w, dbe_w, dbon)
    in_specs = ([tok(DP), tok(1)] + [full(a) for a in ins[2:17]] + [tok(1024), tok(1024), tok(512)]
                + [tok(512)] * 6 + [tok(1536)])
    shp = lambda *s: jax.ShapeDtypeStruct(s, F32)
    out_shape = [shp(bsz, t, 384 + NRW), shp(256, 1024), shp(128, 1024), shp(256, 512), shp(256, 512),
                 shp(1, 256), shp(1, 128), shp(1, NRW), shp(1, 512), shp(1, 512), shp(1, 512), shp(1, 512)]
    out_specs = [tok(384 + NRW)] + [_full(s.shape) for s in out_shape[1:]]
    return pl.pallas_call(
        body, name="pre_bwd_a", grid=(bsz, nt), out_shape=out_shape, in_specs=in_specs, out_specs=out_specs,
        scratch_shapes=[pltpu.VMEM((8, NRW), F32)],
        compiler_params=_cparams(("arbitrary", "arbitrary")),
    )(*ins)


def _pre_bwd_b(x, dh, dz, da, mu, wpt, gpre):
    bsz, t, _ = x.shape
    nt = t // TT
    nblk = t // 8

    def body(x_ref, dh_ref, dz_ref, da_ref, nxt_ref, mu_ref, wpt_ref, gpre_ref, gx_ref, dp_ref, dgpre_ref):
        i = pl.program_id(1)
        first = (pl.program_id(0) == 0) & (i == 0)

        @pl.when(first)
        def _():
            dgpre_ref[...] = jnp.zeros(dgpre_ref.shape, F32)

        mu_v = mu_ref[...]
        dps = da_ref[0, :, 384:384 + NRW]
        nxt = jnp.where(i < nt - 1, nxt_ref[0, 0:1, 384:384 + NRW], 0.0)
        row = lax.broadcasted_iota(jnp.int32, dps.shape, 0)
        up = jnp.where(row == TT - 1, nxt, pltpu.roll(dps, TT - 1, 0))
        dprw = dps * (1.0 - mu_v) + up * mu_v
        dp = jnp.concatenate([dz_ref[0], da_ref[0, :, 0:384], dprw], axis=1).astype(BF16)
        dp_ref[0] = dp
        du = _dot(dp, wpt_ref[...])
        _, nx, rstd = _rms(x_ref[0], gpre_ref[...], D)
        dx, dg = _rms_bwd(du, nx, rstd, gpre_ref[...], D)
        dgpre_ref[...] += dg
        gx_ref[0] = dh_ref[0] + dx

    tok = lambda c: pl.BlockSpec((1, TT, c), lambda b, i: (b, i, 0))
    nxt_spec = pl.BlockSpec((1, 8, 384 + NRW), lambda b, i: (b, jnp.minimum((i + 1) * (TT // 8), nblk - 1), 0))
    ins = (x, dh, dz, da, da, mu, wpt, gpre)
    return pl.pallas_call(
        body, name="pre_bwd_b", grid=(bsz, nt),
        out_shape=[jax.ShapeDtypeStruct((bsz, t, D), F32), jax.ShapeDtypeStruct((bsz, t, DP), BF16),
                   jax.ShapeDtypeStruct((1, D), F32)],
        in_specs=[tok(D), tok(D), tok(1024), tok(384 + NRW), nxt_spec, _full(mu.shape), _full(wpt.shape),
                  _full(gpre.shape)],
        out_specs=[tok(D), tok(DP), _full((1, D))],
        compiler_params=_cparams(("arbitrary", "arbitrary")),
    )(*ins)


def _tn_matmul(a, b, bn, name, bk=512):
    kdim, m = a.shape
    _, n = b.shape
    nk = kdim // bk

    def body(a_ref, b_ref, o_ref):
        @pl.when(pl.program_id(1) == 0)
        def _():
            o_ref[...] = jnp.zeros(o_ref.shape, F32)

        o_ref[...] += _dot_tn(a_ref[...], b_ref[...])

    return pl.pallas_call(
        body, name=name, grid=(n // bn, nk),
        out_shape=jax.ShapeDtypeStruct((m, n), F32),
        in_specs=[pl.BlockSpec((bk, m), lambda j, kk: (kk, 0)), pl.BlockSpec((bk, bn), lambda j, kk: (kk, j))],
        out_specs=pl.BlockSpec((m, bn), lambda j, kk: (0, j)),
        compiler_params=_cparams(("parallel", "arbitrary")),
    )(a, b)


SHARDED = ("w_in", "mla_w_uq", "mla_w_ukv", "rw_w2", "rw_a2", "w_out")
SMALL = ("norm_pre_g", "mla_q_norm_g", "mla_kv_norm_g", "rw_mu", "rw_w0", "rw_a0", "rw_k_k", "rw_k_a", "rw_r_k",
         "rw_ln_g", "rw_ln_b", "norm_post_g")
WEIGHTS = ("norm_pre_g", "w_in", "mla_q_norm_g", "mla_w_uq", "mla_kv_norm_g", "mla_w_ukv", "rw_mu", "rw_w0", "rw_w2",
           "rw_a0", "rw_a2", "rw_k_k", "rw_k_a", "rw_r_k", "rw_ln_g", "rw_ln_b", "w_out", "norm_post_g")


def _pack_small(d):
    flat = jnp.concatenate([d[n].reshape(1, -1) for n in SMALL], axis=1)
    return jnp.pad(flat, ((0, 0), (0, SMALL_ROWS * LANES - flat.shape[1]))).reshape(SMALL_ROWS, LANES)


def _unpack_small(packed, like):
    flat = packed.reshape(1, -1)
    out, at = {}, 0
    for n in SMALL:
        size = int(np.prod(like[n].shape))
        out[n] = flat[:, at:at + size].reshape(like[n].shape)
        at += size
    return out


def _pack_shard(d):
    return jnp.concatenate([d[n].reshape(-1, LANES) for n in SHARDED], axis=0)


def _unpack_shard(packed, like):
    out, at = {}, 0
    for n, rows in zip(SHARDED, PACK_ROWS):
        out[n] = packed[at:at + rows].reshape(like[n].shape)
        at += rows
    return out


def _constants():
    bo = np.kron(np.eye(2, dtype=np.float32), np.ones((64, 64), np.float32))
    inv = ROPE_THETA ** (-np.arange(0, 64, 2, dtype=np.float32) / 64)
    invf = np.concatenate([inv, inv, np.zeros(64, np.float32)]).astype(np.float32)[None, :]
    return jnp.asarray(bo), jnp.asarray(invf)


def kernel(x, positions, norm_pre_g, w_in, mla_q_norm_g, mla_w_uq, mla_kv_norm_g, mla_w_ukv, rw_mu, rw_w0, rw_w2, rw_a0, rw_a2, rw_k_k, rw_k_a, rw_r_k, rw_ln_g, rw_ln_b, w_out, norm_post_g, loss_target, m_norm_pre_g, m_w_in, m_mla_q_norm_g, m_mla_w_uq, m_mla_kv_norm_g, m_mla_w_ukv, m_rw_mu, m_rw_w0, m_rw_w2, m_rw_a0, m_rw_a2, m_rw_k_k, m_rw_k_a, m_rw_r_k, m_rw_ln_g, m_rw_ln_b, m_w_out, m_norm_post_g, v_norm_pre_g, v_w_in, v_mla_q_norm_g, v_mla_w_uq, v_mla_kv_norm_g, v_mla_w_ukv, v_rw_mu, v_rw_w0, v_rw_w2, v_rw_a0, v_rw_a2, v_rw_k_k, v_rw_k_a, v_rw_r_k, v_rw_ln_g, v_rw_ln_b, v_w_out, v_norm_post_g):
    wts = dict(norm_pre_g=norm_pre_g, w_in=w_in, mla_q_norm_g=mla_q_norm_g, mla_w_uq=mla_w_uq,
               mla_kv_norm_g=mla_kv_norm_g, mla_w_ukv=mla_w_ukv, rw_mu=rw_mu, rw_w0=rw_w0, rw_w2=rw_w2, rw_a0=rw_a0,
               rw_a2=rw_a2, rw_k_k=rw_k_k, rw_k_a=rw_k_a, rw_r_k=rw_r_k, rw_ln_g=rw_ln_g, rw_ln_b=rw_ln_b, w_out=w_out,
               norm_post_g=norm_post_g)
    mom_m = dict(norm_pre_g=m_norm_pre_g, w_in=m_w_in, mla_q_norm_g=m_mla_q_norm_g, mla_w_uq=m_mla_w_uq,
                 mla_kv_norm_g=m_mla_kv_norm_g, mla_w_ukv=m_mla_w_ukv, rw_mu=m_rw_mu, rw_w0=m_rw_w0, rw_w2=m_rw_w2,
                 rw_a0=m_rw_a0, rw_a2=m_rw_a2, rw_k_k=m_rw_k_k, rw_k_a=m_rw_k_a, rw_r_k=m_rw_r_k, rw_ln_g=m_rw_ln_g,
                 rw_ln_b=m_rw_ln_b, w_out=m_w_out, norm_post_g=m_norm_post_g)
    mom_v = dict(norm_pre_g=v_norm_pre_g, w_in=v_w_in, mla_q_norm_g=v_mla_q_norm_g, mla_w_uq=v_mla_w_uq,
                 mla_kv_norm_g=v_mla_kv_norm_g, mla_w_ukv=v_mla_w_ukv, rw_mu=v_rw_mu, rw_w0=v_rw_w0, rw_w2=v_rw_w2,
                 rw_a0=v_rw_a0, rw_a2=v_rw_a2, rw_k_k=v_rw_k_k, rw_k_a=v_rw_k_a, rw_r_k=v_rw_r_k, rw_ln_g=v_rw_ln_g,
                 rw_ln_b=v_rw_ln_b, w_out=v_w_out, norm_post_g=v_norm_post_g)
    bsz, t, _ = x.shape
    bo, invf = _constants()
    c_idx = lax.axis_index("c")
    shard_idx = 2 * lax.axis_index("x") + lax.axis_index("y")

    g_in, g_uq, g_ukv, g_w2, g_a2, g_out = _ag_weights([wts[n][0] for n in SHARDED])
    w_in_f = jnp.transpose(g_in, (1, 0, 2)).reshape(D, D_IN)
    wp = jnp.concatenate([w_in_f[:, 2112:3136], w_in_f[:, 0:384], w_in_f[:, 448:1984], w_in_f[:, 384:448],
                          w_in_f[:, 1984:2112], jnp.zeros((D, 64), BF16)], axis=1)
    wuq = jnp.pad(jnp.transpose(g_uq, (1, 0, 2)).reshape(256, HEADS, 192), ((0, 0), (0, 0), (0, 64))).reshape(256, 1024)
    wukv = jnp.transpose(jnp.transpose(g_ukv, (1, 0, 2)).reshape(128, HEADS, 2, 128), (0, 2, 1, 3)).reshape(128, 1024)
    w2 = jnp.transpose(g_w2, (1, 0, 2)).reshape(64, RW)
    a2 = jnp.transpose(g_a2, (1, 0, 2)).reshape(64, RW)
    w2p = jnp.pad(w2, ((64, 128), (0, 0)))
    a2p = jnp.pad(a2, ((128, 64), (0, 0)))
    wo = g_out.reshape(D, D)
    mu = jnp.concatenate([rw_mu[:, 0:1536], jnp.zeros((1, 64), F32), rw_mu[:, 1536:1664], jnp.zeros((1, 64), F32)],
                         axis=1)
    r_k = rw_r_k.reshape(1, RW)
    pos = positions.astype(F32)[:, :, None]

    (u, pp, q_att, k_att, v_att, r, w, kp, v, al, be) = _pre_fwd(
        x, pos, invf, norm_pre_g, wp, mla_q_norm_g, wuq, mla_kv_norm_g, wukv, mu, rw_w0, w2p, rw_a0, a2p, rw_k_k,
        rw_k_a, bo)
    o, lse = _attn_fwd(q_att, k_att, v_att)
    rw_k = [_to_k(a) for a in (r, w, kp, al, be)]
    v_v = _to_v(v)
    yw_v, ck = _wkv_fwd(*rw_k, v_v)
    yw = _from_v(yw_v, bsz)

    (dh, dz, dym, dyw, dbon, loss_acc, d_wo, d_gpost, d_lng, d_lnb, d_rk) = _post(
        x, loss_target, pp, o, yw, r, kp, v, rw_ln_g, rw_ln_b, r_k, wo, wo.T, norm_post_g, bo)
    loss = lax.psum(loss_acc[0, 0], ("x", "y", "c"))

    d_k = _wkv_bwd(*rw_k, v_v, _to_v(dyw), ck)
    dr_w, dw_w, dkp_w, dal_w, dbe_w = (_from_k(a, bsz) for a in d_k[:5])
    dwkv = (dr_w, dw_w, dkp_w, _from_v(d_k[5], bsz), dal_w, dbe_w)
    dq, dk, dva = _attn_bwd(q_att, k_att, v_att, o, lse, dym)

    (da, d_wuq, d_wukv, d_w2p, d_a2p, d_gq, d_gkv, d_mu, d_w0, d_a0, d_kk, d_ka) = _pre_bwd_a(
        pp, pos, invf, (mla_q_norm_g, wuq.T, mla_kv_norm_g, wukv.T), mu, rw_w0, w2p, w2p.T, rw_a0, a2p, a2p.T,
        rw_k_k, rw_k_a, bo, dq, dk, dva, dwkv, dbon)
    grad_x, dpb, d_gpre = _pre_bwd_b(x, dh, dz, da, mu, wp.T, norm_pre_g)
    d_wp = _tn_matmul(u.reshape(bsz * t, D), dpb.reshape(bsz * t, DP), 640, "dw_in")

    full_g = {
        "w_in": jnp.concatenate([d_wp[:, 1024:1408], d_wp[:, 2944:3008], d_wp[:, 1408:2944], d_wp[:, 3008:3136],
                                 d_wp[:, 0:1024]], axis=1),
        "mla_w_uq": d_wuq.reshape(256, HEADS, 256)[:, :, :192].reshape(256, 768),
        "mla_w_ukv": jnp.transpose(d_wukv.reshape(128, 2, HEADS, 128), (0, 2, 1, 3)).reshape(128, 1024),
        "rw_w2": d_w2p[64:128],
        "rw_a2": d_a2p[128:192],
        "w_out": d_wo,
    }
    small_g = {
        "norm_pre_g": d_gpre, "mla_q_norm_g": d_gq, "mla_kv_norm_g": d_gkv,
        "rw_mu": jnp.concatenate([d_mu[:, 0:1536], d_mu[:, 1600:1728]], axis=1),
        "rw_w0": d_w0, "rw_a0": d_a0, "rw_k_k": d_kk, "rw_k_a": d_ka, "rw_r_k": d_rk, "rw_ln_g": d_lng,
        "rw_ln_b": d_lnb, "norm_post_g": d_gpost,
    }

    def by_shard(name, g):
        if name == "w_out":
            return g.reshape(N_SHARD, -1, LANES)
        rows, cols = g.shape
        return jnp.transpose(g.reshape(rows, N_SHARD, cols // N_SHARD), (1, 0, 2)).reshape(N_SHARD, -1, LANES)

    packed = jnp.concatenate([by_shard(n, full_g[n]) for n in SHARDED], axis=1)
    halves = packed.reshape(N_SHARD, 2, HALF, LANES)
    keep = lax.dynamic_index_in_dim(halves, c_idx, 1, keepdims=False)
    give = lax.dynamic_index_in_dim(halves, 1 - c_idx, 1, keepdims=False)
    got = _rs_pair_exchange(give)
    pair_sum = _add_n([keep.reshape(-1, LANES), got.reshape(-1, LANES)], "rs_pair_sum").reshape(N_SHARD, HALF, LANES)
    arrived = _rs_chip_exchange(pair_sum)
    own = lax.dynamic_index_in_dim(pair_sum, shard_idx, 0, keepdims=False)
    reduced_half = _add_n([own, arrived[0], arrived[1], arrived[2]], "rs_chip_sum")
    g_shard = _rs_pair_gather(reduced_half).reshape(PACK_TOTAL, LANES)

    g_small = _small_allreduce(_pack_small(small_g))

    shard_like = {n: wts[n][0] for n in SHARDED}
    d_sh, nm_sh, nv_sh = _adamw(_pack_shard({n: wts[n][0] for n in SHARDED}), g_shard,
                                _pack_shard({n: mom_m[n][0] for n in SHARDED}),
                                _pack_shard({n: mom_v[n][0] for n in SHARDED}), "adamw_sharded", 568)
    d_sm, nm_sm, nv_sm = _adamw(_pack_small(wts), g_small, _pack_small(mom_m), _pack_small(mom_v), "adamw_small",
                                SMALL_ROWS)

    def unpack(sh, sm):
        out = {n: a[None] for n, a in _unpack_shard(sh, shard_like).items()}
        out.update(_unpack_small(sm, wts))
        return out

    grads, deltas, new_m, new_v = unpack(g_shard, g_small), unpack(d_sh, d_sm), unpack(nm_sh, nm_sm), unpack(nv_sh, nv_sm)
    return (loss, grad_x, *[grads[n] for n in WEIGHTS], *[deltas[n] for n in WEIGHTS],
            *[new_m[n] for n in WEIGHTS], *[new_v[n] for n in WEIGHTS])
```

```python
import functools

import numpy as np
import jax
import jax.numpy as jnp
from jax import lax
from jax.experimental import pallas as pl
from jax.experimental.pallas import tpu as pltpu

F32, BF16 = jnp.float32, jnp.bfloat16
HIGHEST = lax.Precision.HIGHEST
MESH = pl.DeviceIdType.MESH

D = 1024
HEADS = 4
RW = 512
NORM_EPS = 1e-6
GN_EPS = 64e-5
ROPE_THETA = 10000.0
SCALE = (128 + 64) ** -0.5
D_IN = 3136
LR, B1, B2, ADAM_EPS, WD, STEP = 0.001, 0.9, 0.999, 1e-08, 0.01, 10

Z0, CQ0, CKV0, RW0, DP = 0, 1024, 1280, 1408, 3200
NRW = DP - RW0

LANES = 128
SUBLANES = 8
VMEM_LIMIT = 56 * 1024 * 1024

TT = 256
TQ = 512

N_SHARD = 4
PACK_ROWS = (1024 * 784 // 128, 256 * 192 // 128, 128 * 256 // 128, 64, 64, 256 * 1024 // 128)
PACK_TOTAL = sum(PACK_ROWS)
HALF = PACK_TOTAL // 2
SMALL_ROWS = 64


def _cparams(sem=None):
    return pltpu.CompilerParams(dimension_semantics=sem, vmem_limit_bytes=VMEM_LIMIT)


def _full(shape):
    n = len(shape)
    return pl.BlockSpec(shape, lambda *_: (0,) * n)


def _dot(a, b):
    return jnp.dot(a, b, preferred_element_type=F32)


def _dot_nt(a, b):
    return lax.dot_general(a, b, (((1,), (1,)), ((), ())), preferred_element_type=F32)


def _dot_tn(a, b):
    return lax.dot_general(a, b, (((0,), (0,)), ((), ())), preferred_element_type=F32)


def _seg(x, bo):
    parts = [jnp.dot(x[:, LANES * i:LANES * (i + 1)], bo, precision=HIGHEST, preferred_element_type=F32)
             for i in range(x.shape[1] // LANES)]
    return parts[0] if len(parts) == 1 else jnp.concatenate(parts, axis=1)


def _rms(x, g, n):
    rstd = lax.rsqrt(jnp.sum(x * x, axis=-1, keepdims=True) * (1.0 / n) + NORM_EPS)
    nx = x * rstd
    return nx * g, nx, rstd


def _rms_bwd(dy, nx, rstd, g, n):
    dn = dy * g
    dx = rstd * (dn - nx * (jnp.sum(dn * nx, axis=-1, keepdims=True) * (1.0 / n)))
    return dx, jnp.sum(dy * nx, axis=0, keepdims=True)


def _rot(x):
    lane = lax.broadcasted_iota(jnp.int32, x.shape, 1)
    return jnp.where((lane % 64) < 32, -pltpu.roll(x, x.shape[1] - 32, 1), pltpu.roll(x, 32, 1))


def _sigmoid(x):
    return 1.0 / (1.0 + jnp.exp(-x))


def _softplus(x):
    return jnp.maximum(x, 0.0) + jnp.log(1.0 + jnp.exp(-jnp.abs(x)))


def _rw_gates(ps, w0, w2p, a0, a2p, k_k, k_a, bo):
    r, k, v, misc = ps[:, 0:512], ps[:, 512:1024], ps[:, 1024:1536], ps[:, 1536:NRW]
    th = jnp.tanh(misc)
    wpre = w0 + _dot(th.astype(BF16), w2p)
    e = jnp.exp(-_softplus(-wpre) - 0.5)
    w = jnp.exp(-e)
    a = _sigmoid(a0 + _dot(misc.astype(BF16), a2p))
    m = k * k_k
    nrm = jnp.maximum(jnp.sqrt(_seg(m * m, bo)), 1e-12)
    kk = m / nrm
    kp = k * (1.0 + (a - 1.0) * k_a)
    return dict(r=r, k=k, v=v, misc=misc, th=th, wpre=wpre, e=e, w=w, a=a, nrm=nrm, kk=kk, kp=kp)


def _shift_mix(prw, prev_row, mu):
    row = lax.broadcasted_iota(jnp.int32, prw.shape, 0)
    sh = jnp.where(row == 0, prev_row, pltpu.roll(prw, 1, 0))
    return prw + (sh - prw) * mu, sh


def _ag_weights(shards):
    n = len(shards)

    def body(*refs):
        ins, outs = refs[:n], refs[n:2 * n]
        send_sems, recv_sems = refs[2 * n], refs[2 * n + 1]
        x, y, c = lax.axis_index("x"), lax.axis_index("y"), lax.axis_index("c")
        mine = 2 * x + y
        for w in range(n):
            outs[w][mine] = ins[w][...].astype(BF16)
        flips = ((1, 0), (0, 1), (1, 1))

        def copy(w, k):
            fx, fy = flips[k]
            return pltpu.make_async_remote_copy(
                src_ref=outs[w].at[mine], dst_ref=outs[w].at[mine],
                send_sem=send_sems.at[w * 3 + k], recv_sem=recv_sems.at[w * 3 + k],
                device_id=(x ^ fx, y ^ fy, c), device_id_type=MESH)

        def arrival(w, k):
            fx, fy = flips[k]
            theirs = 2 * (x ^ fx) + (y ^ fy)
            return pltpu.make_async_remote_copy(
                src_ref=outs[w].at[theirs], dst_ref=outs[w].at[theirs],
                send_sem=send_sems.at[w * 3 + k], recv_sem=recv_sems.at[w * 3 + k],
                device_id=(x ^ fx, y ^ fy, c), device_id_type=MESH)

        for w in range(n):
            for k in range(3):
                copy(w, k).start()
        for w in range(n):
            for k in range(3):
                arrival(w, k).wait_recv()
        for w in range(n):
            for k in range(3):
                copy(w, k).wait_send()

    vm = pl.BlockSpec(memory_space=pltpu.VMEM)
    return pl.pallas_call(
        body, name="ag_weights",
        out_shape=[jax.ShapeDtypeStruct((N_SHARD,) + s.shape, BF16) for s in shards],
        in_specs=[vm] * n, out_specs=[vm] * n,
        scratch_shapes=[pltpu.SemaphoreType.DMA((3 * n,)), pltpu.SemaphoreType.DMA((3 * n,))],
        compiler_params=pltpu.CompilerParams(vmem_limit_bytes=VMEM_LIMIT),
    )(*shards)


def _rs_pair_exchange(send_half):
    def body(src_ref, dst_ref, send_sem, recv_sem):
        x, y, c = lax.axis_index("x"), lax.axis_index("y"), lax.axis_index("c")
        cp = pltpu.make_async_remote_copy(src_ref=src_ref, dst_ref=dst_ref, send_sem=send_sem, recv_sem=recv_sem,
                                          device_id=(x, y, 1 - c), device_id_type=MESH)
        cp.start()
        cp.wait()

    hbm = pl.BlockSpec(memory_space=pl.ANY)
    return pl.pallas_call(
        body, name="rs_pair_exchange",
        out_shape=jax.ShapeDtypeStruct(send_half.shape, send_half.dtype),
        in_specs=[hbm], out_specs=hbm,
        scratch_shapes=[pltpu.SemaphoreType.DMA, pltpu.SemaphoreType.DMA],
    )(send_half)


def _rs_chip_exchange(part):
    def body(src_ref, dst_ref, send_sems, recv_sems):
        x, y, c = lax.axis_index("x"), lax.axis_index("y"), lax.axis_index("c")
        flips = ((1, 0), (0, 1), (1, 1))
        cps = []
        for k, (fx, fy) in enumerate(flips):
            theirs = 2 * (x ^ fx) + (y ^ fy)
            cps.append(pltpu.make_async_remote_copy(
                src_ref=src_ref.at[theirs], dst_ref=dst_ref.at[k],
                send_sem=send_sems.at[k], recv_sem=recv_sems.at[k],
                device_id=(x ^ fx, y ^ fy, c), device_id_type=MESH))
        for cp in cps:
            cp.start()
        for cp in cps:
            cp.wait()

    hbm = pl.BlockSpec(memory_space=pl.ANY)
    return pl.pallas_call(
        body, name="rs_chip_exchange",
        out_shape=jax.ShapeDtypeStruct((3,) + part.shape[1:], part.dtype),
        in_specs=[hbm], out_specs=hbm,
        scratch_shapes=[pltpu.SemaphoreType.DMA((3,)), pltpu.SemaphoreType.DMA((3,))],
    )(part)


def _rs_pair_gather(half):
    def body(src_ref, dst_ref, send_sem, recv_sem, local_sem):
        x, y, c = lax.axis_index("x"), lax.axis_index("y"), lax.axis_index("c")
        own = pltpu.make_async_copy(src_ref, dst_ref.at[c], local_sem)
        own.start()
        cp = pltpu.make_async_remote_copy(src_ref=src_ref, dst_ref=dst_ref.at[c], send_sem=send_sem, recv_sem=recv_sem,
                                          device_id=(x, y, 1 - c), device_id_type=MESH)
        cp.start()
        arrival = pltpu.make_async_remote_copy(src_ref=src_ref, dst_ref=dst_ref.at[1 - c], send_sem=send_sem,
                                               recv_sem=recv_sem, device_id=(x, y, 1 - c), device_id_type=MESH)
        arrival.wait_recv()
        cp.wait_send()
        own.wait()

    hbm = pl.BlockSpec(memory_space=pl.ANY)
    return pl.pallas_call(
        body, name="rs_pair_gather",
        out_shape=jax.ShapeDtypeStruct((2,) + half.shape, half.dtype),
        in_specs=[hbm], out_specs=hbm,
        scratch_shapes=[pltpu.SemaphoreType.DMA, pltpu.SemaphoreType.DMA, pltpu.SemaphoreType.DMA],
    )(half)


def _small_allreduce(vec):
    def body(in_ref, out_ref, recv, send_sems, recv_sems):
        x, y, c = lax.axis_index("x"), lax.axis_index("y"), lax.axis_index("c")
        me = 4 * x + 2 * y + c
        cps = []
        for k in range(1, 8):
            fx, fy, fc = (k >> 2) & 1, (k >> 1) & 1, k & 1
            cps.append(pltpu.make_async_remote_copy(
                src_ref=in_ref, dst_ref=recv.at[k - 1],
                send_sem=send_sems.at[k - 1], recv_sem=recv_sems.at[k - 1],
                device_id=(x ^ fx, y ^ fy, c ^ fc), device_id_type=MESH))
        for cp in cps:
            cp.start()
        for cp in cps:
            cp.wait()
        acc = jnp.zeros(in_ref.shape, F32)
        for j in range(8):
            slot = jnp.maximum((me ^ j) - 1, 0)
            acc = acc + jnp.where(me == j, in_ref[...], recv[slot])
        out_ref[...] = acc

    vm = pl.BlockSpec(memory_space=pltpu.VMEM)
    return pl.pallas_call(
        body, name="small_allreduce",
        out_shape=jax.ShapeDtypeStruct(vec.shape, F32),
        in_specs=[vm], out_specs=vm,
        scratch_shapes=[pltpu.VMEM((7,) + vec.shape, F32), pltpu.SemaphoreType.DMA((7,)),
                        pltpu.SemaphoreType.DMA((7,))],
    )(vec)


def _add_n(arrs, name, rows=568):
    n = len(arrs)
    r = arrs[0].shape[0]

    def body(*refs):
        acc = refs[0][...]
        for k in range(1, n):
            acc = acc + refs[k][...]
        refs[n][...] = acc

    spec = pl.BlockSpec((rows, LANES), lambda i: (i, 0))
    return pl.pallas_call(
        body, name=name, grid=(r // rows,),
        out_shape=jax.ShapeDtypeStruct(arrs[0].shape, F32),
        in_specs=[spec] * n, out_specs=spec,
        compiler_params=_cparams(("parallel",)),
    )(*arrs)


def _adamw(w, g, m, v, name, rows):
    r = w.shape[0]

    def body(w_ref, g_ref, m_ref, v_ref, d_ref, nm_ref, nv_ref):
        gg = g_ref[...]
        nm = B1 * m_ref[...] + (1.0 - B1) * gg
        nv = B2 * v_ref[...] + (1.0 - B2) * (gg * gg)
        m_hat = nm / (1.0 - B1 ** STEP)
        v_hat = nv / (1.0 - B2 ** STEP)
        d_ref[...] = -LR * (m_hat / (jnp.sqrt(v_hat) + ADAM_EPS) + WD * w_ref[...])
        nm_ref[...] = nm
        nv_ref[...] = nv

    spec = pl.BlockSpec((rows, LANES), lambda i: (i, 0))
    sds = jax.ShapeDtypeStruct(w.shape, F32)
    return pl.pallas_call(
        body, name=name, grid=(r // rows,),
        out_shape=[sds, sds, sds],
        in_specs=[spec] * 4, out_specs=[spec] * 3,
        compiler_params=_cparams(("parallel",)),
    )(w, g, m, v)


def _pre_fwd(x, pos, invf, gpre, wp, gq, wuq, gkv, wukv, mu, w0, w2p, a0, a2p, k_k, k_a, bo):
    bsz, t, _ = x.shape
    nt = t // TT

    def body(x_ref, pos_ref, invf_ref, gpre_ref, wp_ref, gq_ref, wuq_ref, gkv_ref, wukv_ref, mu_ref, w0_ref,
             w2p_ref, a0_ref, a2p_ref, kk_ref, ka_ref, bo_ref,
             u_ref, pp_ref, q_ref, k_ref, v_ref, r_o, w_o, kp_o, vv_o, al_o, be_o, carry):
        i = pl.program_id(1)
        u, _, _ = _rms(x_ref[0], gpre_ref[...], D)
        ub = u.astype(BF16)
        u_ref[0] = ub
        p = _dot(ub, wp_ref[...])
        pp_ref[0] = p
        prw = p[:, RW0:DP]

        @pl.when(i == 0)
        def _():
            carry[...] = jnp.zeros(carry.shape, F32)

        ps, _ = _shift_mix(prw, carry[7:8, :], mu_ref[...])
        carry[...] = prw[TT - 8:TT, :]

        g = _rw_gates(ps, w0_ref[...], w2p_ref[...], a0_ref[...], a2p_ref[...], kk_ref[...], ka_ref[...],
                      bo_ref[...])
        r_o[0] = g["r"]
        w_o[0] = g["w"]
        kp_o[0] = g["kp"]
        vv_o[0] = g["v"]
        al_o[0] = -g["kk"]
        be_o[0] = g["kk"] * g["a"]

        cqn, _, _ = _rms(p[:, CQ0:CQ0 + 256], gq_ref[...], 256)
        q = _dot(cqn.astype(BF16), wuq_ref[...])
        ckvn, _, _ = _rms(p[:, CKV0:CKV0 + 128], gkv_ref[...], 128)
        kv = _dot(ckvn.astype(BF16), wukv_ref[...])
        ang = pos_ref[0] * invf_ref[...]
        cs, sn = jnp.cos(ang), jnp.sin(ang)
        lane = lax.broadcasted_iota(jnp.int32, cs.shape, 1)
        kr = ps[:, 1536:1536 + LANES]
        kr = jnp.where(lane < 64, kr * cs + _rot(kr) * sn, 0.0).astype(BF16)
        for h in range(HEADS):
            qr = q[:, 256 * h + 128:256 * h + 256]
            q_ref[0, :, 256 * h:256 * h + 128] = q[:, 256 * h:256 * h + 128].astype(BF16)
            q_ref[0, :, 256 * h + 128:256 * h + 256] = (qr * cs + _rot(qr) * sn).astype(BF16)
            k_ref[0, :, 256 * h:256 * h + 128] = kv[:, 128 * h:128 * h + 128].astype(BF16)
            k_ref[0, :, 256 * h + 128:256 * h + 256] = kr
        v_ref[0] = kv[:, 512:1024].astype(BF16)

    tok = lambda c: pl.BlockSpec((1, TT, c), lambda b, i: (b, i, 0))
    full = lambda a: _full(a.shape)
    ins = (x, pos, invf, gpre, wp, gq, wuq, gkv, wukv, mu, w0, w2p, a0, a2p, k_k, k_a, bo)
    in_specs = [tok(D), tok(1)] + [full(a) for a in ins[2:]]
    sd = lambda c, dt: jax.ShapeDtypeStruct((bsz, t, c), dt)
    out_shape = [sd(D, BF16), sd(DP, F32), sd(1024, BF16), sd(1024, BF16), sd(512, BF16)] + [sd(RW, F32)] * 6
    out_specs = [tok(D), tok(DP), tok(1024), tok(1024), tok(512)] + [tok(RW)] * 6
    return pl.pallas_call(
        body, name="pre_fwd", grid=(bsz, nt), out_shape=out_shape, in_specs=in_specs, out_specs=out_specs,
        scratch_shapes=[pltpu.VMEM((8, NRW), F32)],
        compiler_params=_cparams(("arbitrary", "arbitrary")),
    )(*ins)


def _attn_fwd(q, k, v):
    bsz, t, _ = q.shape
    nq = t // TQ

    def body(q_ref, k_ref, v_ref, o_ref, lse_ref):
        i = pl.program_id(2)
        qt = q_ref[0]
        row = lax.broadcasted_iota(jnp.int32, (TQ, TQ), 0) + i * TQ
        col0 = lax.broadcasted_iota(jnp.int32, (TQ, TQ), 1)

        def step(j, carry):
            m, l, acc = carry
            at = pl.ds(pl.multiple_of(j * TQ, TQ), TQ)
            s = _dot_nt(qt, k_ref[0, at, :]) * SCALE
            s = jnp.where(col0 + j * TQ <= row, s, -1e30)
            mn = jnp.maximum(m, jnp.max(s, axis=1, keepdims=True))
            p = jnp.exp(s - mn)
            al = jnp.exp(m - mn)
            l = al * l + jnp.sum(p, axis=1, keepdims=True)
            acc = al * acc + _dot(p.astype(BF16), v_ref[0, at, :])
            return mn, l, acc

        m, l, acc = lax.fori_loop(
            0, i + 1, step,
            (jnp.full((TQ, 1), -1e30, F32), jnp.zeros((TQ, 1), F32), jnp.zeros((TQ, LANES), F32)))
        o_ref[0] = acc / l
        lse_ref[0, 0] = jnp.broadcast_to(m + jnp.log(l), (TQ, LANES))

    return pl.pallas_call(
        body, name="attn_fwd", grid=(bsz, HEADS, nq),
        out_shape=[jax.ShapeDtypeStruct((bsz, t, 512), F32), jax.ShapeDtypeStruct((bsz, HEADS, t, LANES), F32)],
        in_specs=[pl.BlockSpec((1, TQ, 256), lambda b, h, i: (b, i, h)),
                  pl.BlockSpec((1, t, 256), lambda b, h, i: (b, 0, h)),
                  pl.BlockSpec((1, t, LANES), lambda b, h, i: (b, 0, h))],
        out_specs=[pl.BlockSpec((1, TQ, LANES), lambda b, h, i: (b, i, h)),
                   pl.BlockSpec((1, 1, TQ, LANES), lambda b, h, i: (b, h, i, 0))],
        compiler_params=_cparams(("parallel", "parallel", "arbitrary")),
    )(q, k, v)


def _attn_bwd(q, k, v, o, lse, do):
    bsz, t, _ = q.shape
    nq = t // TQ

    def body(q_ref, k_ref, v_ref, o_ref, lse_ref, do_ref, dq_ref, dk_ref, dv_ref, dl_ref):
        def prep(i, _):
            at = pl.ds(pl.multiple_of(i * TQ, TQ), TQ)
            dl_ref[at, :] = jnp.broadcast_to(jnp.sum(do_ref[0, at, :] * o_ref[0, at, :], axis=1, keepdims=True),
                                             (TQ, LANES))
            return 0

        lax.fori_loop(0, nq, prep, 0)
        dq_ref[0] = jnp.zeros((t, 256), F32)
        row0 = lax.broadcasted_iota(jnp.int32, (TQ, TQ), 0)
        col0 = lax.broadcasted_iota(jnp.int32, (TQ, TQ), 1)

        def kv_tile(j, _):
            atk = pl.ds(pl.multiple_of(j * TQ, TQ), TQ)
            kt = k_ref[0, atk, :]
            vt = v_ref[0, atk, :]

            def q_tile(i, carry):
                dk, dv = carry
                atq = pl.ds(pl.multiple_of(i * TQ, TQ), TQ)
                qt = q_ref[0, atq, :]
                dob = do_ref[0, atq, :].astype(BF16)
                s = _dot_nt(qt, kt) * SCALE
                s = jnp.where(col0 + j * TQ <= row0 + i * TQ, s, -1e30)
                p = jnp.exp(s - lse_ref[0, 0, atq, :][:, 0:1])
                dv = dv + _dot_tn(p.astype(BF16), dob)
                dp = _dot_nt(dob, vt)
                ds = (p * (dp - dl_ref[atq, :][:, 0:1]) * SCALE).astype(BF16)
                dk = dk + _dot_tn(ds, qt)
                dq_ref[0, atq, :] += _dot(ds, kt)
                return dk, dv

            dk, dv = lax.fori_loop(j, nq, q_tile, (jnp.zeros((TQ, 256), F32), jnp.zeros((TQ, LANES), F32)))
            dk_ref[0, atk, :] = dk
            dv_ref[0, atk, :] = dv
            return 0

        lax.fori_loop(0, nq, kv_tile, 0)

    s256 = pl.BlockSpec((1, t, 256), lambda b, h: (b, 0, h))
    s128 = pl.BlockSpec((1, t, LANES), lambda b, h: (b, 0, h))
    return pl.pallas_call(
        body, name="attn_bwd", grid=(bsz, HEADS),
        out_shape=[jax.ShapeDtypeStruct((bsz, t, 1024), F32), jax.ShapeDtypeStruct((bsz, t, 1024), F32),
                   jax.ShapeDtypeStruct((bsz, t, 512), F32)],
        in_specs=[s256, s256, s128, s128, pl.BlockSpec((1, 1, t, LANES), lambda b, h: (b, h, 0, 0)), s128],
        out_specs=[s256, s256, s128],
        scratch_shapes=[pltpu.VMEM((t, LANES), F32)],
        compiler_params=_cparams(("parallel", "parallel")),
    )(q, k, v, o, lse, do)


RW_HEADS = 8
TB = 32
CH = 16


def _lane_split(bsz):
    vs = LANES // (bsz * RW_HEADS)
    return vs, 64 // vs


def _to_k(x):
    bsz, t, _ = x.shape
    vs, _ = _lane_split(bsz)
    return jnp.transpose(x.reshape(bsz, t // vs, vs, RW_HEADS, 64), (1, 4, 2, 0, 3)).reshape(t // vs, 64, LANES)


def _from_k(y, bsz):
    vs, _ = _lane_split(bsz)
    tg = y.shape[0]
    return jnp.transpose(y.reshape(tg, 64, vs, bsz, RW_HEADS), (3, 0, 2, 4, 1)).reshape(bsz, tg * vs, RW)


def _to_v(x):
    bsz, t, _ = x.shape
    vs, vq = _lane_split(bsz)
    return jnp.transpose(x.reshape(bsz, t, RW_HEADS, vq, vs), (1, 3, 4, 0, 2)).reshape(t, vq, LANES)


def _from_v(y, bsz):
    t = y.shape[0]
    vs, vq = _lane_split(bsz)
    return jnp.transpose(y.reshape(t, vq, vs, bsz, RW_HEADS), (3, 0, 4, 1, 2)).reshape(bsz, t, RW)


def _ksum(a):
    return jnp.sum(a, axis=0, keepdims=True)


def _fold(a, group):
    sh = LANES // 2
    while sh >= group:
        a = a + pltpu.roll(a, sh, 1)
        sh //= 2
    return a


def _lane_group(shape, group):
    return lax.broadcasted_iota(jnp.int32, shape, 1) // group


def _spread(x, j, group):
    return _fold(jnp.where(_lane_group(x.shape, group) == j, x, 0.0), group)


def _spread_k(xs, vs):
    tg = xs[0].shape[0]
    n = len(xs)
    gb = 8
    group = LANES // vs

    def body(*refs):
        def one(g, _):
            for x_ref, o_ref in zip(refs[:n], refs[n:]):
                x = x_ref[g]
                for j in range(vs):
                    o_ref[g * vs + j] = _spread(x, j, group)
            return 0

        lax.fori_loop(0, gb, one, 0)

    return pl.pallas_call(
        body, name="wkv_spread", grid=(tg // gb,),
        out_shape=[jax.ShapeDtypeStruct((tg * vs, 64, LANES), F32)] * n,
        in_specs=[pl.BlockSpec((gb, 64, LANES), lambda i: (i, 0, 0))] * n,
        out_specs=[pl.BlockSpec((gb * vs, 64, LANES), lambda i: (i, 0, 0))] * n,
        compiler_params=_cparams(("parallel",)),
    )(*xs)


def _wkv_fwd(r, w, kp, al, be, v):
    t, vq = v.shape[0], v.shape[1]

    def body(r_ref, w_ref, kp_ref, al_ref, be_ref, v_ref, y_ref, ck_ref, st_ref):
        @pl.when(pl.program_id(0) == 0)
        def _():
            st_ref[...] = jnp.zeros(st_ref.shape, F32)

        def step(tl, _):
            @pl.when(tl % CH == 0)
            def _():
                ck_ref[tl // CH] = st_ref[...]

            rv, wv, kv, av, bv = r_ref[tl], w_ref[tl], kp_ref[tl], al_ref[tl], be_ref[tl]
            vals = v_ref[tl]
            rows = []
            for q in range(vq):
                s = st_ref[q]
                u = _ksum(s * av)
                s = s * wv + bv * u + kv * vals[q:q + 1]
                st_ref[q] = s
                rows.append(_ksum(s * rv))
            y_ref[tl] = jnp.concatenate(rows, axis=0)
            return 0

        lax.fori_loop(0, TB, step, 0)

    kspec = pl.BlockSpec((TB, 64, LANES), lambda i: (i, 0, 0))
    vspec = pl.BlockSpec((TB, vq, LANES), lambda i: (i, 0, 0))
    return pl.pallas_call(
        body, name="wkv_fwd", grid=(t // TB,),
        out_shape=[jax.ShapeDtypeStruct((t, vq, LANES), F32), jax.ShapeDtypeStruct((t // CH, vq, 64, LANES), F32)],
        in_specs=[kspec] * 5 + [vspec],
        out_specs=[vspec, pl.BlockSpec((TB // CH, vq, 64, LANES), lambda i: (i, 0, 0, 0))],
        scratch_shapes=[pltpu.VMEM((vq, 64, LANES), F32)],
        compiler_params=_cparams(("arbitrary",)),
    )(r, w, kp, al, be, v)


def _wkv_bwd(r, w, kp, al, be, v, dy, ck):
    t, vq = v.shape[0], v.shape[1]
    vs = 64 // vq
    group = LANES // vs
    n = t // CH
    ng = CH // vs

    def body(r_ref, w_ref, kp_ref, al_ref, be_ref, v_ref, dy_ref, ck_ref,
             dr_ref, dw_ref, dkp_ref, dal_ref, dbe_ref, dv_ref, ds_ref, sp_ref, u_ref):
        @pl.when(pl.program_id(0) == 0)
        def _():
            ds_ref[...] = jnp.zeros(ds_ref.shape, F32)

        sp_ref[0] = ck_ref[0]

        def recompute(tl, _):
            wv, kv, av, bv = w_ref[tl], kp_ref[tl], al_ref[tl], be_ref[tl]
            vals = v_ref[tl]
            rows = []
            for q in range(vq):
                s = sp_ref[tl, q]
                u = _ksum(s * av)
                rows.append(u)
                sp_ref[tl + 1, q] = s * wv + bv * u + kv * vals[q:q + 1]
            u_ref[tl] = jnp.concatenate(rows, axis=0)
            return 0

        lax.fori_loop(0, CH, recompute, 0)

        def reverse(i, _):
            g = ng - 1 - i
            grp = _lane_group((64, LANES), group)
            outs = None
            for j in reversed(range(vs)):
                tl = g * vs + j
                rv, wv, kv, av, bv = r_ref[tl], w_ref[tl], kp_ref[tl], al_ref[tl], be_ref[tl]
                vals, dys, us = v_ref[tl], dy_ref[tl], u_ref[tl]
                acc = None
                dvrows = []
                for q in range(vq):
                    s_prev = sp_ref[tl, q]
                    dyq = dys[q:q + 1]
                    ds = ds_ref[q] + rv * dyq
                    c = _ksum(ds * bv)
                    dvrows.append(_ksum(ds * kv))
                    terms = (sp_ref[tl + 1, q] * dyq, ds * s_prev, ds * vals[q:q + 1], s_prev * c, ds * us[q:q + 1])
                    acc = terms if acc is None else tuple(a + b for a, b in zip(acc, terms))
                    ds_ref[q] = ds * wv + av * c
                dv_ref[tl] = jnp.concatenate(dvrows, axis=0)
                summed = [_fold(a, group) for a in acc]
                outs = summed if outs is None else [jnp.where(grp == j, f, o) for f, o in zip(summed, outs)]
            for ref, o in zip((dr_ref, dw_ref, dkp_ref, dal_ref, dbe_ref), outs):
                ref[g] = o
            return 0

        lax.fori_loop(0, ng, reverse, 0)

    kspec = pl.BlockSpec((CH, 64, LANES), lambda i: (n - 1 - i, 0, 0))
    gspec = pl.BlockSpec((ng, 64, LANES), lambda i: (n - 1 - i, 0, 0))
    vspec = pl.BlockSpec((CH, vq, LANES), lambda i: (n - 1 - i, 0, 0))
    ksd = jax.ShapeDtypeStruct((t // vs, 64, LANES), F32)
    return pl.pallas_call(
        body, name="wkv_bwd", grid=(n,),
        out_shape=[ksd] * 5 + [jax.ShapeDtypeStruct((t, vq, LANES), F32)],
        in_specs=[kspec] * 5 + [vspec, vspec, pl.BlockSpec((1, vq, 64, LANES), lambda i: (n - 1 - i, 0, 0, 0))],
        out_specs=[gspec] * 5 + [vspec],
        scratch_shapes=[pltpu.VMEM((vq, 64, LANES), F32), pltpu.VMEM((CH + 1, vq, 64, LANES), F32),
                        pltpu.VMEM((CH, vq, LANES), F32)],
        compiler_params=_cparams(("arbitrary",)),
    )(r, w, kp, al, be, v, dy, ck)


def _post(x, tgt, pp, o, yw, r, kp, v, ln_g, ln_b, r_k, wo, wot, gpost, bo):
    bsz, t, _ = x.shape
    nt = t // TT

    def body(x_ref, tgt_ref, z_ref, o_ref, yw_ref, r_ref, kp_ref, v_ref, lng_ref, lnb_ref, rk_ref, wo_ref, wot_ref,
             gpost_ref, bo_ref,
             dh_ref, dz_ref, dym_ref, dyw_ref, dbon_ref, loss_ref, dwo_ref, dgpost_ref, dlng_ref, dlnb_ref, drk_ref):
        first = (pl.program_id(0) == 0) & (pl.program_id(1) == 0)

        @pl.when(first)
        def _():
            for ref in (loss_ref, dwo_ref, dgpost_ref, dlng_ref, dlnb_ref, drk_ref):
                ref[...] = jnp.zeros(ref.shape, F32)

        bo_m = bo_ref[...]
        seg = lambda a: _seg(a, bo_m)
        rowsum = lambda a: jnp.sum(a, axis=0, keepdims=True)
        ywv, rv, kpv, vv = yw_ref[0], r_ref[0], kp_ref[0], v_ref[0]
        ln_g, r_k = lng_ref[...], rk_ref[...]
        mean = seg(ywv) * (1.0 / 64)
        yc = ywv - mean
        rstd = lax.rsqrt(seg(yc * yc) * (1.0 / 64) + GN_EPS)
        yhat = yc * rstd
        sb = seg(rv * kpv * r_k)
        y_rw = yhat * ln_g + lnb_ref[...] + sb * vv
        z = z_ref[0]
        sig = _sigmoid(z)
        sz = z * sig
        ycat = jnp.concatenate([o_ref[0], y_rw], axis=1)
        ycg = (ycat * sz).astype(BF16)
        out = _dot(ycg, wo_ref[...])
        hn, nx, rstd_o = _rms(out, gpost_ref[...], D)
        err = x_ref[0] + hn - tgt_ref[0]
        loss_ref[...] += jnp.sum(err * err) * (0.5 / D)
        dh = err * (1.0 / D)
        dh_ref[0] = dh
        dout, dgp = _rms_bwd(dh, nx, rstd_o, gpost_ref[...], D)
        dgpost_ref[...] += dgp
        doutb = dout.astype(BF16)
        dwo_ref[...] += _dot_tn(ycg, doutb)
        dycg = _dot(doutb, wot_ref[...])
        dz_ref[0] = dycg * ycat * (sig * (1.0 + z * (1.0 - sig)))
        dycat = dycg * sz
        dym_ref[0] = dycat[:, 0:512]
        dy_rw = dycat[:, 512:1024]
        dlnb_ref[...] += rowsum(dy_rw)
        dlng_ref[...] += rowsum(dy_rw * yhat)
        dyhat = dy_rw * ln_g
        dyw_ref[0] = rstd * (dyhat - seg(dyhat) * (1.0 / 64) - yhat * (seg(dyhat * yhat) * (1.0 / 64)))
        dsb = seg(dy_rw * vv)
        drk_ref[...] += rowsum(dsb * rv * kpv)
        dbon_ref[0, :, 0:512] = dsb * kpv * r_k
        dbon_ref[0, :, 512:1024] = dsb * rv * r_k
        dbon_ref[0, :, 1024:1536] = dy_rw * sb

    tok = lambda c: pl.BlockSpec((1, TT, c), lambda b, i: (b, i, 0))
    full = lambda a: _full(a.shape)
    ins = (x, tgt, pp, o, yw, r, kp, v, ln_g, ln_b, r_k, wo, wot, gpost, bo)
    in_specs = [tok(D), tok(D), tok(1024)] + [tok(512)] * 5 + [full(a) for a in ins[8:]]
    sd = lambda c: jax.ShapeDtypeStruct((bsz, t, c), F32)
    vec = lambda c: jax.ShapeDtypeStruct((1, c), F32)
    out_shape = [sd(D), sd(1024), sd(512), sd(512), sd(1536), jax.ShapeDtypeStruct((8, LANES), F32),
                 jax.ShapeDtypeStruct((1024, 1024), F32), vec(D), vec(512), vec(512), vec(512)]
    out_specs = [tok(D), tok(1024), tok(512), tok(512), tok(1536), _full((8, LANES)), _full((1024, 1024)),
                 _full((1, D)), _full((1, 512)), _full((1, 512)), _full((1, 512))]
    return pl.pallas_call(
        body, name="post", grid=(bsz, nt), out_shape=out_shape, in_specs=in_specs, out_specs=out_specs,
        compiler_params=_cparams(("arbitrary", "arbitrary")),
    )(*ins)


def _pre_bwd_a(pp, pos, invf, cqkv_w, mu, w0, w2p, w2pt, a0, a2p, a2pt, k_k, k_a, bo,
               dq, dk, dva, dwkv, dbon):
    gq, wuqt, gkv, wukvt = cqkv_w
    bsz, t, _ = pp.shape
    nt = t // TT
    dr_w, dw_w, dkp_w, dv_w, dal_w, dbe_w = dwkv

    def body(pp_ref, pos_ref, invf_ref, gq_ref, wuqt_ref, gkv_ref, wukvt_ref, mu_ref, w0_ref, w2p_ref, w2pt_ref,
             a0_ref, a2p_ref, a2pt_ref, kk_ref, ka_ref, bo_ref, dq_ref, dk_ref, dva_ref,
             dr_ref, dw_ref, dkp_ref, dv_ref, dal_ref, dbe_ref, dbon_ref,
             da_ref, dwuq_ref, dwukv_ref, dw2p_ref, da2p_ref, dgq_ref, dgkv_ref, dmu_ref, dw0_ref, da0_ref,
             dkk_ref, dka_ref, carry):
        i = pl.program_id(1)
        first = (pl.program_id(0) == 0) & (i == 0)

        @pl.when(first)
        def _():
            for ref in (dwuq_ref, dwukv_ref, dw2p_ref, da2p_ref, dgq_ref, dgkv_ref, dmu_ref, dw0_ref, da0_ref,
                        dkk_ref, dka_ref):
                ref[...] = jnp.zeros(ref.shape, F32)

        bo_m = bo_ref[...]
        rowsum = lambda a: jnp.sum(a, axis=0, keepdims=True)
        prw = pp_ref[0, :, RW0:DP]

        @pl.when(i == 0)
        def _():
            carry[...] = jnp.zeros(carry.shape, F32)

        ps, sh = _shift_mix(prw, carry[7:8, :], mu_ref[...])
        carry[...] = prw[TT - 8:TT, :]
        k_k, k_a = kk_ref[...], ka_ref[...]
        g = _rw_gates(ps, w0_ref[...], w2p_ref[...], a0_ref[...], a2p_ref[...], k_k, k_a, bo_m)
        a, kk, k = g["a"], g["kk"], g["k"]
        dr = dr_ref[0] + dbon_ref[0, :, 0:512]
        dkp = dkp_ref[0] + dbon_ref[0, :, 512:1024]
        dv = dv_ref[0] + dbon_ref[0, :, 1024:1536]
        dbe = dbe_ref[0]
        dkk = dbe * a - dal_ref[0]
        da = dbe * kk + dkp * k * k_a
        dka_ref[...] += rowsum(dkp * k * (a - 1.0))
        dm = (dkk - kk * _seg(dkk * kk, bo_m)) / g["nrm"]
        dkk_ref[...] += rowsum(dm * k)
        dk_tot = dkp * (1.0 + (a - 1.0) * k_a) + dm * k_k
        dapre = da * a * (1.0 - a)
        da0_ref[...] += rowsum(dapre)
        dapb = dapre.astype(BF16)
        da2p_ref[...] += _dot_tn(g["misc"].astype(BF16), dapb)
        dwpre = dw_ref[0] * g["w"] * (-g["e"]) * _sigmoid(-g["wpre"])
        dw0_ref[...] += rowsum(dwpre)
        dwpb = dwpre.astype(BF16)
        th = g["th"]
        dw2p_ref[...] += _dot_tn(th.astype(BF16), dwpb)
        dmisc = _dot(dapb, a2pt_ref[...]) + _dot(dwpb, w2pt_ref[...]) * (1.0 - th * th)
        ang = pos_ref[0] * invf_ref[...]
        cs, sn = jnp.cos(ang), jnp.sin(ang)
        unrope = lambda gr: gr * cs - _rot(gr * sn)
        lane = lax.broadcasted_iota(jnp.int32, cs.shape, 1)
        dkr = dk_ref[0, :, 128:256]
        for h in range(1, HEADS):
            dkr = dkr + dk_ref[0, :, 256 * h + 128:256 * h + 256]
        dkr = jnp.where(lane < 64, unrope(dkr), 0.0)
        dmisc = dmisc + jnp.concatenate([dkr, jnp.zeros_like(dkr)], axis=1)
        dqp = jnp.concatenate(
            [blk for h in range(HEADS)
             for blk in (dq_ref[0, :, 256 * h:256 * h + 128], unrope(dq_ref[0, :, 256 * h + 128:256 * h + 256]))],
            axis=1).astype(BF16)
        dkvp = jnp.concatenate([dk_ref[0, :, 256 * h:256 * h + 128] for h in range(HEADS)] + [dva_ref[0]],
                               axis=1).astype(BF16)
        cqn, cq_nx, cq_rstd = _rms(pp_ref[0, :, CQ0:CQ0 + 256], gq_ref[...], 256)
        ckvn, ckv_nx, ckv_rstd = _rms(pp_ref[0, :, CKV0:CKV0 + 128], gkv_ref[...], 128)
        dwuq_ref[...] += _dot_tn(cqn.astype(BF16), dqp)
        dwukv_ref[...] += _dot_tn(ckvn.astype(BF16), dkvp)
        dcq, dgq = _rms_bwd(_dot(dqp, wuqt_ref[...]), cq_nx, cq_rstd, gq_ref[...], 256)
        dckv, dgkv = _rms_bwd(_dot(dkvp, wukvt_ref[...]), ckv_nx, ckv_rstd, gkv_ref[...], 128)
        dgq_ref[...] += dgq
        dgkv_ref[...] += dgkv
        dps = jnp.concatenate([dr, dk_tot, dv, dmisc], axis=1)
        dmu_ref[...] += rowsum(dps * (sh - prw))
        da_ref[0, :, 0:256] = dcq
        da_ref[0, :, 256:384] = dckv
        da_ref[0, :, 384:384 + NRW] = dps

    tok = lambda c: pl.BlockSpec((1, TT, c), lambda b, i: (b, i, 0))
    full = lambda a: _full(a.shape)
    ins = (pp, pos, invf, gq, wuqt, gkv, wukvt, mu, w0, w2p, w2pt, a0, a2p, a2pt, k_k, k_a, bo,
           dq, dk, dva, dr_w, dw_w, dkp_w, dv_w, dal_w, dbe_w, dbon)
    in_specs = ([tok(DP), tok(1)] + [full(a) for a in ins[2:17]] + [tok(1024), tok(1024), tok(512)]
                + [tok(512)] * 6 + [tok(1536)])
    shp = lambda *s: jax.ShapeDtypeStruct(s, F32)
    out_shape = [shp(bsz, t, 384 + NRW), shp(256, 1024), shp(128, 1024), shp(256, 512), shp(256, 512),
                 shp(1, 256), shp(1, 128), shp(1, NRW), shp(1, 512), shp(1, 512), shp(1, 512), shp(1, 512)]
    out_specs = [tok(384 + NRW)] + [_full(s.shape) for s in out_shape[1:]]
    return pl.pallas_call(
        body, name="pre_bwd_a", grid=(bsz, nt), out_shape=out_shape, in_specs=in_specs, out_specs=out_specs,
        scratch_shapes=[pltpu.VMEM((8, NRW), F32)],
        compiler_params=_cparams(("arbitrary", "arbitrary")),
    )(*ins)


def _pre_bwd_b(x, dh, dz, da, mu, wpt, gpre):
    bsz, t, _ = x.shape
    nt = t // TT
    nblk = t // 8

    def body(x_ref, dh_ref, dz_ref, da_ref, nxt_ref, mu_ref, wpt_ref, gpre_ref, gx_ref, dp_ref, dgpre_ref):
        i = pl.program_id(1)
        first = (pl.program_id(0) == 0) & (i == 0)

        @pl.when(first)
        def _():
            dgpre_ref[...] = jnp.zeros(dgpre_ref.shape, F32)

        mu_v = mu_ref[...]
        dps = da_ref[0, :, 384:384 + NRW]
        nxt = jnp.where(i < nt - 1, nxt_ref[0, 0:1, 384:384 + NRW], 0.0)
        row = lax.broadcasted_iota(jnp.int32, dps.shape, 0)
        up = jnp.where(row == TT - 1, nxt, pltpu.roll(dps, TT - 1, 0))
        dprw = dps * (1.0 - mu_v) + up * mu_v
        dp = jnp.concatenate([dz_ref[0], da_ref[0, :, 0:384], dprw], axis=1).astype(BF16)
        dp_ref[0] = dp
        du = _dot(dp, wpt_ref[...])
        _, nx, rstd = _rms(x_ref[0], gpre_ref[...], D)
        dx, dg = _rms_bwd(du, nx, rstd, gpre_ref[...], D)
        dgpre_ref[...] += dg
        gx_ref[0] = dh_ref[0] + dx

    tok = lambda c: pl.BlockSpec((1, TT, c), lambda b, i: (b, i, 0))
    nxt_spec = pl.BlockSpec((1, 8, 384 + NRW), lambda b, i: (b, jnp.minimum((i + 1) * (TT // 8), nblk - 1), 0))
    ins = (x, dh, dz, da, da, mu, wpt, gpre)
    return pl.pallas_call(
        body, name="pre_bwd_b", grid=(bsz, nt),
        out_shape=[jax.ShapeDtypeStruct((bsz, t, D), F32), jax.ShapeDtypeStruct((bsz, t, DP), BF16),
                   jax.ShapeDtypeStruct((1, D), F32)],
        in_specs=[tok(D), tok(D), tok(1024), tok(384 + NRW), nxt_spec, _full(mu.shape), _full(wpt.shape),
                  _full(gpre.shape)],
        out_specs=[tok(D), tok(DP), _full((1, D))],
        compiler_params=_cparams(("arbitrary", "arbitrary")),
    )(*ins)


def _tn_matmul(a, b, bn, name, bk=512):
    kdim, m = a.shape
    _, n = b.shape
    nk = kdim // bk

    def body(a_ref, b_ref, o_ref):
        @pl.when(pl.program_id(1) == 0)
        def _():
            o_ref[...] = jnp.zeros(o_ref.shape, F32)

        o_ref[...] += _dot_tn(a_ref[...], b_ref[...])

    return pl.pallas_call(
        body, name=name, grid=(n // bn, nk),
        out_shape=jax.ShapeDtypeStruct((m, n), F32),
        in_specs=[pl.BlockSpec((bk, m), lambda j, kk: (kk, 0)), pl.BlockSpec((bk, bn), lambda j, kk: (kk, j))],
        out_specs=pl.BlockSpec((m, bn), lambda j, kk: (0, j)),
        compiler_params=_cparams(("parallel", "arbitrary")),
    )(a, b)


SHARDED = ("w_in", "mla_w_uq", "mla_w_ukv", "rw_w2", "rw_a2", "w_out")
SMALL = ("norm_pre_g", "mla_q_norm_g", "mla_kv_norm_g", "rw_mu", "rw_w0", "rw_a0", "rw_k_k", "rw_k_a", "rw_r_k",
         "rw_ln_g", "rw_ln_b", "norm_post_g")
WEIGHTS = ("norm_pre_g", "w_in", "mla_q_norm_g", "mla_w_uq", "mla_kv_norm_g", "mla_w_ukv", "rw_mu", "rw_w0", "rw_w2",
           "rw_a0", "rw_a2", "rw_k_k", "rw_k_a", "rw_r_k", "rw_ln_g", "rw_ln_b", "w_out", "norm_post_g")


def _pack_small(d):
    flat = jnp.concatenate([d[n].reshape(1, -1) for n in SMALL], axis=1)
    return jnp.pad(flat, ((0, 0), (0, SMALL_ROWS * LANES - flat.shape[1]))).reshape(SMALL_ROWS, LANES)


def _unpack_small(packed, like):
    flat = packed.reshape(1, -1)
    out, at = {}, 0
    for n in SMALL:
        size = int(np.prod(like[n].shape))
        out[n] = flat[:, at:at + size].reshape(like[n].shape)
        at += size
    return out


def _pack_shard(d):
    return jnp.concatenate([d[n].reshape(-1, LANES) for n in SHARDED], axis=0)


def _unpack_shard(packed, like):
    out, at = {}, 0
    for n, rows in zip(SHARDED, PACK_ROWS):
        out[n] = packed[at:at + rows].reshape(like[n].shape)
        at += rows
    return out


def _constants():
    bo = np.kron(np.eye(2, dtype=np.float32), np.ones((64, 64), np.float32))
    inv = ROPE_THETA ** (-np.arange(0, 64, 2, dtype=np.float32) / 64)
    invf = np.concatenate([inv, inv, np.zeros(64, np.float32)]).astype(np.float32)[None, :]
    return jnp.asarray(bo), jnp.asarray(invf)


def kernel(x, positions, norm_pre_g, w_in, mla_q_norm_g, mla_w_uq, mla_kv_norm_g, mla_w_ukv, rw_mu, rw_w0, rw_w2, rw_a0, rw_a2, rw_k_k, rw_k_a, rw_r_k, rw_ln_g, rw_ln_b, w_out, norm_post_g, loss_target, m_norm_pre_g, m_w_in, m_mla_q_norm_g, m_mla_w_uq, m_mla_kv_norm_g, m_mla_w_ukv, m_rw_mu, m_rw_w0, m_rw_w2, m_rw_a0, m_rw_a2, m_rw_k_k, m_rw_k_a, m_rw_r_k, m_rw_ln_g, m_rw_ln_b, m_w_out, m_norm_post_g, v_norm_pre_g, v_w_in, v_mla_q_norm_g, v_mla_w_uq, v_mla_kv_norm_g, v_mla_w_ukv, v_rw_mu, v_rw_w0, v_rw_w2, v_rw_a0, v_rw_a2, v_rw_k_k, v_rw_k_a, v_rw_r_k, v_rw_ln_g, v_rw_ln_b, v_w_out, v_norm_post_g):
    wts = dict(norm_pre_g=norm_pre_g, w_in=w_in, mla_q_norm_g=mla_q_norm_g, mla_w_uq=mla_w_uq,
               mla_kv_norm_g=mla_kv_norm_g, mla_w_ukv=mla_w_ukv, rw_mu=rw_mu, rw_w0=rw_w0, rw_w2=rw_w2, rw_a0=rw_a0,
               rw_a2=rw_a2, rw_k_k=rw_k_k, rw_k_a=rw_k_a, rw_r_k=rw_r_k, rw_ln_g=rw_ln_g, rw_ln_b=rw_ln_b, w_out=w_out,
               norm_post_g=norm_post_g)
    mom_m = dict(norm_pre_g=m_norm_pre_g, w_in=m_w_in, mla_q_norm_g=m_mla_q_norm_g, mla_w_uq=m_mla_w_uq,
                 mla_kv_norm_g=m_mla_kv_norm_g, mla_w_ukv=m_mla_w_ukv, rw_mu=m_rw_mu, rw_w0=m_rw_w0, rw_w2=m_rw_w2,
                 rw_a0=m_rw_a0, rw_a2=m_rw_a2, rw_k_k=m_rw_k_k, rw_k_a=m_rw_k_a, rw_r_k=m_rw_r_k, rw_ln_g=m_rw_ln_g,
                 rw_ln_b=m_rw_ln_b, w_out=m_w_out, norm_post_g=m_norm_post_g)
    mom_v = dict(norm_pre_g=v_norm_pre_g, w_in=v_w_in, mla_q_norm_g=v_mla_q_norm_g, mla_w_uq=v_mla_w_uq,
                 mla_kv_norm_g=v_mla_kv_norm_g, mla_w_ukv=v_mla_w_ukv, rw_mu=v_rw_mu, rw_w0=v_rw_w0, rw_w2=v_rw_w2,
                 rw_a0=v_rw_a0, rw_a2=v_rw_a2, rw_k_k=v_rw_k_k, rw_k_a=v_rw_k_a, rw_r_k=v_rw_r_k, rw_ln_g=v_rw_ln_g,
                 rw_ln_b=v_rw_ln_b, w_out=v_w_out, norm_post_g=v_norm_post_g)
    bsz, t, _ = x.shape
    bo, invf = _constants()
    c_idx = lax.axis_index("c")
    shard_idx = 2 * lax.axis_index("x") + lax.axis_index("y")

    g_in, g_uq, g_ukv, g_w2, g_a2, g_out = _ag_weights([wts[n][0] for n in SHARDED])
    w_in_f = jnp.transpose(g_in, (1, 0, 2)).reshape(D, D_IN)
    wp = jnp.concatenate([w_in_f[:, 2112:3136], w_in_f[:, 0:384], w_in_f[:, 448:1984], w_in_f[:, 384:448],
                          w_in_f[:, 1984:2112], jnp.zeros((D, 64), BF16)], axis=1)
    wuq = jnp.pad(jnp.transpose(g_uq, (1, 0, 2)).reshape(256, HEADS, 192), ((0, 0), (0, 0), (0, 64))).reshape(256, 1024)
    wukv = jnp.transpose(jnp.transpose(g_ukv, (1, 0, 2)).reshape(128, HEADS, 2, 128), (0, 2, 1, 3)).reshape(128, 1024)
    w2 = jnp.transpose(g_w2, (1, 0, 2)).reshape(64, RW)
    a2 = jnp.transpose(g_a2, (1, 0, 2)).reshape(64, RW)
    w2p = jnp.pad(w2, ((64, 128), (0, 0)))
    a2p = jnp.pad(a2, ((128, 64), (0, 0)))
    wo = g_out.reshape(D, D)
    mu = jnp.concatenate([rw_mu[:, 0:1536], jnp.zeros((1, 64), F32), rw_mu[:, 1536:1664], jnp.zeros((1, 64), F32)],
                         axis=1)
    r_k = rw_r_k.reshape(1, RW)
    pos = positions.astype(F32)[:, :, None]

    (u, pp, q_att, k_att, v_att, r, w, kp, v, al, be) = _pre_fwd(
        x, pos, invf, norm_pre_g, wp, mla_q_norm_g, wuq, mla_kv_norm_g, wukv, mu, rw_w0, w2p, rw_a0, a2p, rw_k_k,
        rw_k_a, bo)
    o, lse = _attn_fwd(q_att, k_att, v_att)
    rw_k = _spread_k([_to_k(a) for a in (r, w, kp, al, be)], _lane_split(bsz)[0])
    v_v = _to_v(v)
    yw_v, ck = _wkv_fwd(*rw_k, v_v)
    yw = _from_v(yw_v, bsz)

    (dh, dz, dym, dyw, dbon, loss_acc, d_wo, d_gpost, d_lng, d_lnb, d_rk) = _post(
        x, loss_target, pp, o, yw, r, kp, v, rw_ln_g, rw_ln_b, r_k, wo, wo.T, norm_post_g, bo)
    loss = lax.psum(loss_acc[0, 0], ("x", "y", "c"))

    d_k = _wkv_bwd(*rw_k, v_v, _to_v(dyw), ck)
    dr_w, dw_w, dkp_w, dal_w, dbe_w = (_from_k(a, bsz) for a in d_k[:5])
    dwkv = (dr_w, dw_w, dkp_w, _from_v(d_k[5], bsz), dal_w, dbe_w)
    dq, dk, dva = _attn_bwd(q_att, k_att, v_att, o, lse, dym)

    (da, d_wuq, d_wukv, d_w2p, d_a2p, d_gq, d_gkv, d_mu, d_w0, d_a0, d_kk, d_ka) = _pre_bwd_a(
        pp, pos, invf, (mla_q_norm_g, wuq.T, mla_kv_norm_g, wukv.T), mu, rw_w0, w2p, w2p.T, rw_a0, a2p, a2p.T,
        rw_k_k, rw_k_a, bo, dq, dk, dva, dwkv, dbon)
    grad_x, dpb, d_gpre = _pre_bwd_b(x, dh, dz, da, mu, wp.T, norm_pre_g)
    d_wp = _tn_matmul(u.reshape(bsz * t, D), dpb.reshape(bsz * t, DP), 640, "dw_in")

    full_g = {
        "w_in": jnp.concatenate([d_wp[:, 1024:1408], d_wp[:, 2944:3008], d_wp[:, 1408:2944], d_wp[:, 3008:3136],
                                 d_wp[:, 0:1024]], axis=1),
        "mla_w_uq": d_wuq.reshape(256, HEADS, 256)[:, :, :192].reshape(256, 768),
        "mla_w_ukv": jnp.transpose(d_wukv.reshape(128, 2, HEADS, 128), (0, 2, 1, 3)).reshape(128, 1024),
        "rw_w2": d_w2p[64:128],
        "rw_a2": d_a2p[128:192],
        "w_out": d_wo,
    }
    small_g = {
        "norm_pre_g": d_gpre, "mla_q_norm_g": d_gq, "mla_kv_norm_g": d_gkv,
        "rw_mu": jnp.concatenate([d_mu[:, 0:1536], d_mu[:, 1600:1728]], axis=1),
        "rw_w0": d_w0, "rw_a0": d_a0, "rw_k_k": d_kk, "rw_k_a": d_ka, "rw_r_k": d_rk, "rw_ln_g": d_lng,
        "rw_ln_b": d_lnb, "norm_post_g": d_gpost,
    }

    def by_shard(name, g):
        if name == "w_out":
            return g.reshape(N_SHARD, -1, LANES)
        rows, cols = g.shape
        return jnp.transpose(g.reshape(rows, N_SHARD, cols // N_SHARD), (1, 0, 2)).reshape(N_SHARD, -1, LANES)

    packed = jnp.concatenate([by_shard(n, full_g[n]) for n in SHARDED], axis=1)
    halves = packed.reshape(N_SHARD, 2, HALF, LANES)
    keep = lax.dynamic_index_in_dim(halves, c_idx, 1, keepdims=False)
    give = lax.dynamic_index_in_dim(halves, 1 - c_idx, 1, keepdims=False)
    got = _rs_pair_exchange(give)
    pair_sum = _add_n([keep.reshape(-1, LANES), got.reshape(-1, LANES)], "rs_pair_sum").reshape(N_SHARD, HALF, LANES)
    arrived = _rs_chip_exchange(pair_sum)
    own = lax.dynamic_index_in_dim(pair_sum, shard_idx, 0, keepdims=False)
    reduced_half = _add_n([own, arrived[0], arrived[1], arrived[2]], "rs_chip_sum")
    g_shard = _rs_pair_gather(reduced_half).reshape(PACK_TOTAL, LANES)

    g_small = _small_allreduce(_pack_small(small_g))

    shard_like = {n: wts[n][0] for n in SHARDED}
    d_sh, nm_sh, nv_sh = _adamw(_pack_shard({n: wts[n][0] for n in SHARDED}), g_shard,
                                _pack_shard({n: mom_m[n][0] for n in SHARDED}),
                                _pack_shard({n: mom_v[n][0] for n in SHARDED}), "adamw_sharded", 568)
    d_sm, nm_sm, nv_sm = _adamw(_pack_small(wts), g_small, _pack_small(mom_m), _pack_small(mom_v), "adamw_small",
                                SMALL_ROWS)

    def unpack(sh, sm):
        out = {n: a[None] for n, a in _unpack_shard(sh, shard_like).items()}
        out.update(_unpack_small(sm, wts))
        return out

    grads, deltas, new_m, new_v = unpack(g_shard, g_small), unpack(d_sh, d_sm), unpack(nm_sh, nm_sm), unpack(nv_sh, nv_sm)
    return (loss, grad_x, *[grads[n] for n in WEIGHTS], *[deltas[n] for n in WEIGHTS],
            *[new_m[n] for n in WEIGHTS], *[new_v[n] for n in WEIGHTS])
```

```python
import functools

import numpy as np
import jax
import jax.numpy as jnp
from jax import lax
from jax.experimental import pallas as pl
from jax.experimental.pallas import tpu as pltpu

F32, BF16 = jnp.float32, jnp.bfloat16
HIGHEST = lax.Precision.HIGHEST
MESH = pl.DeviceIdType.MESH

D = 1024
HEADS = 4
RW = 512
NORM_EPS = 1e-6
GN_EPS = 64e-5
ROPE_THETA = 10000.0
SCALE = (128 + 64) ** -0.5
D_IN = 3136
LR, B1, B2, ADAM_EPS, WD, STEP = 0.001, 0.9, 0.999, 1e-08, 0.01, 10

Z0, CQ0, CKV0, RW0, DP = 0, 1024, 1280, 1408, 3200
NRW = DP - RW0

LANES = 128
SUBLANES = 8
VMEM_LIMIT = 56 * 1024 * 1024

TT = 256
TQ = 512

N_SHARD = 4
PACK_ROWS = (1024 * 784 // 128, 256 * 192 // 128, 128 * 256 // 128, 64, 64, 256 * 1024 // 128)
PACK_TOTAL = sum(PACK_ROWS)
HALF = PACK_TOTAL // 2
SMALL_ROWS = 64


def _cparams(sem=None):
    return pltpu.CompilerParams(dimension_semantics=sem, vmem_limit_bytes=VMEM_LIMIT)


def _full(shape):
    n = len(shape)
    return pl.BlockSpec(shape, lambda *_: (0,) * n)


def _dot(a, b):
    return jnp.dot(a, b, preferred_element_type=F32)


def _dot_nt(a, b):
    return lax.dot_general(a, b, (((1,), (1,)), ((), ())), preferred_element_type=F32)


def _dot_tn(a, b):
    return lax.dot_general(a, b, (((0,), (0,)), ((), ())), preferred_element_type=F32)


def _seg(x, bo):
    parts = [jnp.dot(x[:, LANES * i:LANES * (i + 1)], bo, precision=HIGHEST, preferred_element_type=F32)
             for i in range(x.shape[1] // LANES)]
    return parts[0] if len(parts) == 1 else jnp.concatenate(parts, axis=1)


def _rms(x, g, n):
    rstd = lax.rsqrt(jnp.sum(x * x, axis=-1, keepdims=True) * (1.0 / n) + NORM_EPS)
    nx = x * rstd
    return nx * g, nx, rstd


def _rms_bwd(dy, nx, rstd, g, n):
    dn = dy * g
    dx = rstd * (dn - nx * (jnp.sum(dn * nx, axis=-1, keepdims=True) * (1.0 / n)))
    return dx, jnp.sum(dy * nx, axis=0, keepdims=True)


def _rot(x):
    lane = lax.broadcasted_iota(jnp.int32, x.shape, 1)
    return jnp.where((lane % 64) < 32, -pltpu.roll(x, x.shape[1] - 32, 1), pltpu.roll(x, 32, 1))


def _sigmoid(x):
    return 1.0 / (1.0 + jnp.exp(-x))


def _softplus(x):
    return jnp.maximum(x, 0.0) + jnp.log(1.0 + jnp.exp(-jnp.abs(x)))


def _rw_gates(ps, w0, w2p, a0, a2p, k_k, k_a, bo):
    r, k, v, misc = ps[:, 0:512], ps[:, 512:1024], ps[:, 1024:1536], ps[:, 1536:NRW]
    th = jnp.tanh(misc)
    wpre = w0 + _dot(th.astype(BF16), w2p)
    e = jnp.exp(-_softplus(-wpre) - 0.5)
    w = jnp.exp(-e)
    a = _sigmoid(a0 + _dot(misc.astype(BF16), a2p))
    m = k * k_k
    nrm = jnp.maximum(jnp.sqrt(_seg(m * m, bo)), 1e-12)
    kk = m / nrm
    kp = k * (1.0 + (a - 1.0) * k_a)
    return dict(r=r, k=k, v=v, misc=misc, th=th, wpre=wpre, e=e, w=w, a=a, nrm=nrm, kk=kk, kp=kp)


def _shift_mix(prw, prev_row, mu):
    row = lax.broadcasted_iota(jnp.int32, prw.shape, 0)
    sh = jnp.where(row == 0, prev_row, pltpu.roll(prw, 1, 0))
    return prw + (sh - prw) * mu, sh


def _ag_weights(shards):
    n = len(shards)

    def body(*refs):
        ins, outs = refs[:n], refs[n:2 * n]
        send_sems, recv_sems = refs[2 * n], refs[2 * n + 1]
        x, y, c = lax.axis_index("x"), lax.axis_index("y"), lax.axis_index("c")
        mine = 2 * x + y
        for w in range(n):
            outs[w][mine] = ins[w][...].astype(BF16)
        flips = ((1, 0), (0, 1), (1, 1))

        def copy(w, k):
            fx, fy = flips[k]
            return pltpu.make_async_remote_copy(
                src_ref=outs[w].at[mine], dst_ref=outs[w].at[mine],
                send_sem=send_sems.at[w * 3 + k], recv_sem=recv_sems.at[w * 3 + k],
                device_id=(x ^ fx, y ^ fy, c), device_id_type=MESH)

        def arrival(w, k):
            fx, fy = flips[k]
            theirs = 2 * (x ^ fx) + (y ^ fy)
            return pltpu.make_async_remote_copy(
                src_ref=outs[w].at[theirs], dst_ref=outs[w].at[theirs],
                send_sem=send_sems.at[w * 3 + k], recv_sem=recv_sems.at[w * 3 + k],
                device_id=(x ^ fx, y ^ fy, c), device_id_type=MESH)

        for w in range(n):
            for k in range(3):
                copy(w, k).start()
        for w in range(n):
            for k in range(3):
                arrival(w, k).wait_recv()
        for w in range(n):
            for k in range(3):
                copy(w, k).wait_send()

    vm = pl.BlockSpec(memory_space=pltpu.VMEM)
    return pl.pallas_call(
        body, name="ag_weights",
        out_shape=[jax.ShapeDtypeStruct((N_SHARD,) + s.shape, BF16) for s in shards],
        in_specs=[vm] * n, out_specs=[vm] * n,
        scratch_shapes=[pltpu.SemaphoreType.DMA((3 * n,)), pltpu.SemaphoreType.DMA((3 * n,))],
        compiler_params=pltpu.CompilerParams(vmem_limit_bytes=VMEM_LIMIT),
    )(*shards)


def _rs_pair_exchange(send_half):
    def body(src_ref, dst_ref, send_sem, recv_sem):
        x, y, c = lax.axis_index("x"), lax.axis_index("y"), lax.axis_index("c")
        cp = pltpu.make_async_remote_copy(src_ref=src_ref, dst_ref=dst_ref, send_sem=send_sem, recv_sem=recv_sem,
                                          device_id=(x, y, 1 - c), device_id_type=MESH)
        cp.start()
        cp.wait()

    hbm = pl.BlockSpec(memory_space=pl.ANY)
    return pl.pallas_call(
        body, name="rs_pair_exchange",
        out_shape=jax.ShapeDtypeStruct(send_half.shape, send_half.dtype),
        in_specs=[hbm], out_specs=hbm,
        scratch_shapes=[pltpu.SemaphoreType.DMA, pltpu.SemaphoreType.DMA],
    )(send_half)


def _rs_chip_exchange(part):
    def body(src_ref, dst_ref, send_sems, recv_sems):
        x, y, c = lax.axis_index("x"), lax.axis_index("y"), lax.axis_index("c")
        flips = ((1, 0), (0, 1), (1, 1))
        cps = []
        for k, (fx, fy) in enumerate(flips):
            theirs = 2 * (x ^ fx) + (y ^ fy)
            cps.append(pltpu.make_async_remote_copy(
                src_ref=src_ref.at[theirs], dst_ref=dst_ref.at[k],
                send_sem=send_sems.at[k], recv_sem=recv_sems.at[k],
                device_id=(x ^ fx, y ^ fy, c), device_id_type=MESH))
        for cp in cps:
            cp.start()
        for cp in cps:
            cp.wait()

    hbm = pl.BlockSpec(memory_space=pl.ANY)
    return pl.pallas_call(
        body, name="rs_chip_exchange",
        out_shape=jax.ShapeDtypeStruct((3,) + part.shape[1:], part.dtype),
        in_specs=[hbm], out_specs=hbm,
        scratch_shapes=[pltpu.SemaphoreType.DMA((3,)), pltpu.SemaphoreType.DMA((3,))],
    )(part)


def _rs_pair_gather(half):
    def body(src_ref, dst_ref, send_sem, recv_sem, local_sem):
        x, y, c = lax.axis_index("x"), lax.axis_index("y"), lax.axis_index("c")
        own = pltpu.make_async_copy(src_ref, dst_ref.at[c], local_sem)
        own.start()
        cp = pltpu.make_async_remote_copy(src_ref=src_ref, dst_ref=dst_ref.at[c], send_sem=send_sem, recv_sem=recv_sem,
                                          device_id=(x, y, 1 - c), device_id_type=MESH)
        cp.start()
        arrival = pltpu.make_async_remote_copy(src_ref=src_ref, dst_ref=dst_ref.at[1 - c], send_sem=send_sem,
                                               recv_sem=recv_sem, device_id=(x, y, 1 - c), device_id_type=MESH)
        arrival.wait_recv()
        cp.wait_send()
        own.wait()

    hbm = pl.BlockSpec(memory_space=pl.ANY)
    return pl.pallas_call(
        body, name="rs_pair_gather",
        out_shape=jax.ShapeDtypeStruct((2,) + half.shape, half.dtype),
        in_specs=[hbm], out_specs=hbm,
        scratch_shapes=[pltpu.SemaphoreType.DMA, pltpu.SemaphoreType.DMA, pltpu.SemaphoreType.DMA],
    )(half)


def _small_allreduce(vec):
    def body(in_ref, out_ref, recv, send_sems, recv_sems):
        x, y, c = lax.axis_index("x"), lax.axis_index("y"), lax.axis_index("c")
        me = 4 * x + 2 * y + c
        cps = []
        for k in range(1, 8):
            fx, fy, fc = (k >> 2) & 1, (k >> 1) & 1, k & 1
            cps.append(pltpu.make_async_remote_copy(
                src_ref=in_ref, dst_ref=recv.at[k - 1],
                send_sem=send_sems.at[k - 1], recv_sem=recv_sems.at[k - 1],
                device_id=(x ^ fx, y ^ fy, c ^ fc), device_id_type=MESH))
        for cp in cps:
            cp.start()
        for cp in cps:
            cp.wait()
        acc = jnp.zeros(in_ref.shape, F32)
        for j in range(8):
            slot = jnp.maximum((me ^ j) - 1, 0)
            acc = acc + jnp.where(me == j, in_ref[...], recv[slot])
        out_ref[...] = acc

    vm = pl.BlockSpec(memory_space=pltpu.VMEM)
    return pl.pallas_call(
        body, name="small_allreduce",
        out_shape=jax.ShapeDtypeStruct(vec.shape, F32),
        in_specs=[vm], out_specs=vm,
        scratch_shapes=[pltpu.VMEM((7,) + vec.shape, F32), pltpu.SemaphoreType.DMA((7,)),
                        pltpu.SemaphoreType.DMA((7,))],
    )(vec)


def _add_n(arrs, name, rows=568):
    n = len(arrs)
    r = arrs[0].shape[0]

    def body(*refs):
        acc = refs[0][...]
        for k in range(1, n):
            acc = acc + refs[k][...]
        refs[n][...] = acc

    spec = pl.BlockSpec((rows, LANES), lambda i: (i, 0))
    return pl.pallas_call(
        body, name=name, grid=(r // rows,),
        out_shape=jax.ShapeDtypeStruct(arrs[0].shape, F32),
        in_specs=[spec] * n, out_specs=spec,
        compiler_params=_cparams(("parallel",)),
    )(*arrs)


def _adamw(w, g, m, v, name, rows):
    r = w.shape[0]

    def body(w_ref, g_ref, m_ref, v_ref, d_ref, nm_ref, nv_ref):
        gg = g_ref[...]
        nm = B1 * m_ref[...] + (1.0 - B1) * gg
        nv = B2 * v_ref[...] + (1.0 - B2) * (gg * gg)
        m_hat = nm / (1.0 - B1 ** STEP)
        v_hat = nv / (1.0 - B2 ** STEP)
        d_ref[...] = -LR * (m_hat / (jnp.sqrt(v_hat) + ADAM_EPS) + WD * w_ref[...])
        nm_ref[...] = nm
        nv_ref[...] = nv

    spec = pl.BlockSpec((rows, LANES), lambda i: (i, 0))
    sds = jax.ShapeDtypeStruct(w.shape, F32)
    return pl.pallas_call(
        body, name=name, grid=(r // rows,),
        out_shape=[sds, sds, sds],
        in_specs=[spec] * 4, out_specs=[spec] * 3,
        compiler_params=_cparams(("parallel",)),
    )(w, g, m, v)


def _pre_fwd(x, pos, invf, gpre, wp, gq, wuq, gkv, wukv, mu, w0, w2p, a0, a2p, k_k, k_a, bo):
    bsz, t, _ = x.shape
    nt = t // TT

    def body(x_ref, pos_ref, invf_ref, gpre_ref, wp_ref, gq_ref, wuq_ref, gkv_ref, wukv_ref, mu_ref, w0_ref,
             w2p_ref, a0_ref, a2p_ref, kk_ref, ka_ref, bo_ref,
             u_ref, pp_ref, q_ref, k_ref, v_ref, r_o, w_o, kp_o, vv_o, al_o, be_o, carry):
        i = pl.program_id(1)
        u, _, _ = _rms(x_ref[0], gpre_ref[...], D)
        ub = u.astype(BF16)
        u_ref[0] = ub
        p = _dot(ub, wp_ref[...])
        pp_ref[0] = p
        prw = p[:, RW0:DP]

        @pl.when(i == 0)
        def _():
            carry[...] = jnp.zeros(carry.shape, F32)

        ps, _ = _shift_mix(prw, carry[7:8, :], mu_ref[...])
        carry[...] = prw[TT - 8:TT, :]

        g = _rw_gates(ps, w0_ref[...], w2p_ref[...], a0_ref[...], a2p_ref[...], kk_ref[...], ka_ref[...],
                      bo_ref[...])
        r_o[0] = g["r"]
        w_o[0] = g["w"]
        kp_o[0] = g["kp"]
        vv_o[0] = g["v"]
        al_o[0] = -g["kk"]
        be_o[0] = g["kk"] * g["a"]

        cqn, _, _ = _rms(p[:, CQ0:CQ0 + 256], gq_ref[...], 256)
        q = _dot(cqn.astype(BF16), wuq_ref[...])
        ckvn, _, _ = _rms(p[:, CKV0:CKV0 + 128], gkv_ref[...], 128)
        kv = _dot(ckvn.astype(BF16), wukv_ref[...])
        ang = pos_ref[0] * invf_ref[...]
        cs, sn = jnp.cos(ang), jnp.sin(ang)
        lane = lax.broadcasted_iota(jnp.int32, cs.shape, 1)
        kr = ps[:, 1536:1536 + LANES]
        kr = jnp.where(lane < 64, kr * cs + _rot(kr) * sn, 0.0).astype(BF16)
        for h in range(HEADS):
            qr = q[:, 256 * h + 128:256 * h + 256]
            q_ref[0, :, 256 * h:256 * h + 128] = q[:, 256 * h:256 * h + 128].astype(BF16)
            q_ref[0, :, 256 * h + 128:256 * h + 256] = (qr * cs + _rot(qr) * sn).astype(BF16)
            k_ref[0, :, 256 * h:256 * h + 128] = kv[:, 128 * h:128 * h + 128].astype(BF16)
            k_ref[0, :, 256 * h + 128:256 * h + 256] = kr
        v_ref[0] = kv[:, 512:1024].astype(BF16)

    tok = lambda c: pl.BlockSpec((1, TT, c), lambda b, i: (b, i, 0))
    full = lambda a: _full(a.shape)
    ins = (x, pos, invf, gpre, wp, gq, wuq, gkv, wukv, mu, w0, w2p, a0, a2p, k_k, k_a, bo)
    in_specs = [tok(D), tok(1)] + [full(a) for a in ins[2:]]
    sd = lambda c, dt: jax.ShapeDtypeStruct((bsz, t, c), dt)
    out_shape = [sd(D, BF16), sd(DP, F32), sd(1024, BF16), sd(1024, BF16), sd(512, BF16)] + [sd(RW, F32)] * 6
    out_specs = [tok(D), tok(DP), tok(1024), tok(1024), tok(512)] + [tok(RW)] * 6
    return pl.pallas_call(
        body, name="pre_fwd", grid=(bsz, nt), out_shape=out_shape, in_specs=in_specs, out_specs=out_specs,
        scratch_shapes=[pltpu.VMEM((8, NRW), F32)],
        compiler_params=_cparams(("arbitrary", "arbitrary")),
    )(*ins)


def _attn_fwd(q, k, v):
    bsz, t, _ = q.shape
    nq = t // TQ

    def body(q_ref, k_ref, v_ref, o_ref, lse_ref):
        i = pl.program_id(2)
        qt = q_ref[0]
        row = lax.broadcasted_iota(jnp.int32, (TQ, TQ), 0) + i * TQ
        col0 = lax.broadcasted_iota(jnp.int32, (TQ, TQ), 1)

        def step(j, carry):
            m, l, acc = carry
            at = pl.ds(pl.multiple_of(j * TQ, TQ), TQ)
            s = _dot_nt(qt, k_ref[0, at, :]) * SCALE
            s = jnp.where(col0 + j * TQ <= row, s, -1e30)
            mn = jnp.maximum(m, jnp.max(s, axis=1, keepdims=True))
            p = jnp.exp(s - mn)
            al = jnp.exp(m - mn)
            l = al * l + jnp.sum(p, axis=1, keepdims=True)
            acc = al * acc + _dot(p.astype(BF16), v_ref[0, at, :])
            return mn, l, acc

        m, l, acc = lax.fori_loop(
            0, i + 1, step,
            (jnp.full((TQ, 1), -1e30, F32), jnp.zeros((TQ, 1), F32), jnp.zeros((TQ, LANES), F32)))
        o_ref[0] = acc / l
        lse_ref[0, 0] = jnp.broadcast_to(m + jnp.log(l), (TQ, LANES))

    return pl.pallas_call(
        body, name="attn_fwd", grid=(bsz, HEADS, nq),
        out_shape=[jax.ShapeDtypeStruct((bsz, t, 512), F32), jax.ShapeDtypeStruct((bsz, HEADS, t, LANES), F32)],
        in_specs=[pl.BlockSpec((1, TQ, 256), lambda b, h, i: (b, i, h)),
                  pl.BlockSpec((1, t, 256), lambda b, h, i: (b, 0, h)),
                  pl.BlockSpec((1, t, LANES), lambda b, h, i: (b, 0, h))],
        out_specs=[pl.BlockSpec((1, TQ, LANES), lambda b, h, i: (b, i, h)),
                   pl.BlockSpec((1, 1, TQ, LANES), lambda b, h, i: (b, h, i, 0))],
        compiler_params=_cparams(("parallel", "parallel", "arbitrary")),
    )(q, k, v)


def _attn_bwd(q, k, v, o, lse, do):
    bsz, t, _ = q.shape
    nq = t // TQ

    def body(q_ref, k_ref, v_ref, o_ref, lse_ref, do_ref, dq_ref, dk_ref, dv_ref, dl_ref):
        def prep(i, _):
            at = pl.ds(pl.multiple_of(i * TQ, TQ), TQ)
            dl_ref[at, :] = jnp.broadcast_to(jnp.sum(do_ref[0, at, :] * o_ref[0, at, :], axis=1, keepdims=True),
                                             (TQ, LANES))
            return 0

        lax.fori_loop(0, nq, prep, 0)
        dq_ref[0] = jnp.zeros((t, 256), F32)
        row0 = lax.broadcasted_iota(jnp.int32, (TQ, TQ), 0)
        col0 = lax.broadcasted_iota(jnp.int32, (TQ, TQ), 1)

        def kv_tile(j, _):
            atk = pl.ds(pl.multiple_of(j * TQ, TQ), TQ)
            kt = k_ref[0, atk, :]
            vt = v_ref[0, atk, :]

            def q_tile(i, carry):
                dk, dv = carry
                atq = pl.ds(pl.multiple_of(i * TQ, TQ), TQ)
                qt = q_ref[0, atq, :]
                dob = do_ref[0, atq, :].astype(BF16)
                s = _dot_nt(qt, kt) * SCALE
                s = jnp.where(col0 + j * TQ <= row0 + i * TQ, s, -1e30)
                p = jnp.exp(s - lse_ref[0, 0, atq, :][:, 0:1])
                dv = dv + _dot_tn(p.astype(BF16), dob)
                dp = _dot_nt(dob, vt)
                ds = (p * (dp - dl_ref[atq, :][:, 0:1]) * SCALE).astype(BF16)
                dk = dk + _dot_tn(ds, qt)
                dq_ref[0, atq, :] += _dot(ds, kt)
                return dk, dv

            dk, dv = lax.fori_loop(j, nq, q_tile, (jnp.zeros((TQ, 256), F32), jnp.zeros((TQ, LANES), F32)))
            dk_ref[0, atk, :] = dk
            dv_ref[0, atk, :] = dv
            return 0

        lax.fori_loop(0, nq, kv_tile, 0)

    s256 = pl.BlockSpec((1, t, 256), lambda b, h: (b, 0, h))
    s128 = pl.BlockSpec((1, t, LANES), lambda b, h: (b, 0, h))
    return pl.pallas_call(
        body, name="attn_bwd", grid=(bsz, HEADS),
        out_shape=[jax.ShapeDtypeStruct((bsz, t, 1024), F32), jax.ShapeDtypeStruct((bsz, t, 1024), F32),
                   jax.ShapeDtypeStruct((bsz, t, 512), F32)],
        in_specs=[s256, s256, s128, s128, pl.BlockSpec((1, 1, t, LANES), lambda b, h: (b, h, 0, 0)), s128],
        out_specs=[s256, s256, s128],
        scratch_shapes=[pltpu.VMEM((t, LANES), F32)],
        compiler_params=_cparams(("parallel", "parallel")),
    )(q, k, v, o, lse, do)


RW_HEADS = 8
CH = 16


def _lane_split(bsz):
    vs = LANES // (bsz * RW_HEADS)
    return vs, 64 // vs


def _to_k(x):
    bsz, t, _ = x.shape
    vs, _ = _lane_split(bsz)
    return jnp.transpose(x.reshape(bsz, t // vs, vs, RW_HEADS, 64), (1, 4, 2, 0, 3)).reshape(t // vs, 64, LANES)


def _from_k(y, bsz):
    vs, _ = _lane_split(bsz)
    tg = y.shape[0]
    return jnp.transpose(y.reshape(tg, 64, vs, bsz, RW_HEADS), (3, 0, 2, 4, 1)).reshape(bsz, tg * vs, RW)


def _to_v(x):
    bsz, t, _ = x.shape
    vs, vq = _lane_split(bsz)
    return jnp.transpose(x.reshape(bsz, t, RW_HEADS, vq, vs), (1, 3, 4, 0, 2)).reshape(t, vq, LANES)


def _from_v(y, bsz):
    t = y.shape[0]
    vs, vq = _lane_split(bsz)
    return jnp.transpose(y.reshape(t, vq, vs, bsz, RW_HEADS), (3, 0, 4, 1, 2)).reshape(bsz, t, RW)


def _ksum(a):
    return jnp.sum(a, axis=0, keepdims=True)


def _fold(a, group):
    sh = LANES // 2
    while sh >= group:
        a = a + pltpu.roll(a, sh, 1)
        sh //= 2
    return a


def _lane_group(shape, group):
    return lax.broadcasted_iota(jnp.int32, shape, 1) // group


def _spread(x, j, group):
    return _fold(jnp.where(_lane_group(x.shape, group) == j, x, 0.0), group)


def _spread_matrix(vs):
    group = LANES // vs
    p = np.zeros((LANES, vs * LANES), np.float32)
    for j in range(vs):
        for lane in range(LANES):
            p[j * group + lane % group, j * LANES + lane] = 1.0
    return jnp.asarray(p, BF16)


def _spread_k(xs, vs):
    tg = xs[0].shape[0]
    n = len(xs)
    gb = 8

    def body(*refs):
        pm = refs[n][...]
        for x_ref, o_ref in zip(refs[:n], refs[n + 1:]):
            x = x_ref[...].reshape(gb * 64, LANES)
            hi = x.astype(BF16)
            r1 = x - hi.astype(F32)
            mid = r1.astype(BF16)
            lo = (r1 - mid.astype(F32)).astype(BF16)
            res = _dot(hi, pm) + _dot(mid, pm) + _dot(lo, pm)
            for g in range(gb):
                for j in range(vs):
                    o_ref[g * vs + j] = res[g * 64:(g + 1) * 64, j * LANES:(j + 1) * LANES]

    pm = _spread_matrix(vs)
    return pl.pallas_call(
        body, name="wkv_spread", grid=(tg // gb,),
        out_shape=[jax.ShapeDtypeStruct((tg * vs, 64, LANES), F32)] * n,
        in_specs=[pl.BlockSpec((gb, 64, LANES), lambda i: (i, 0, 0))] * n + [_full(pm.shape)],
        out_specs=[pl.BlockSpec((gb * vs, 64, LANES), lambda i: (i, 0, 0))] * n,
        compiler_params=_cparams(("parallel",)),
    )(*xs, pm)


def _wkv_fwd(r, w, kp, al, be, v):
    t, vq = v.shape[0], v.shape[1]

    def body(r_ref, w_ref, kp_ref, al_ref, be_ref, v_ref, y_ref, a_ref, u_ref, st_ref):
        @pl.when(pl.program_id(0) == 0)
        def _():
            st_ref[...] = jnp.zeros(st_ref.shape, F32)

        def step(tl, _):
            rv, wv, kv, av, bv = r_ref[tl], w_ref[tl], kp_ref[tl], al_ref[tl], be_ref[tl]
            vals = v_ref[tl]
            yrows, urows = [], []
            for q in range(vq):
                s = st_ref[q]
                u = _ksum(s * av)
                s = s * wv + bv * u + kv * vals[q:q + 1]
                st_ref[q] = s
                a_ref[tl, q] = s
                urows.append(u)
                yrows.append(_ksum(s * rv))
            y_ref[tl] = jnp.concatenate(yrows, axis=0)
            u_ref[tl] = jnp.concatenate(urows, axis=0)
            return 0

        lax.fori_loop(0, CH, step, 0)

    kspec = pl.BlockSpec((CH, 64, LANES), lambda i: (i, 0, 0))
    vspec = pl.BlockSpec((CH, vq, LANES), lambda i: (i, 0, 0))
    vsd = jax.ShapeDtypeStruct((t, vq, LANES), F32)
    return pl.pallas_call(
        body, name="wkv_fwd", grid=(t // CH,),
        out_shape=[vsd, jax.ShapeDtypeStruct((t, vq, 64, LANES), F32), vsd],
        in_specs=[kspec] * 5 + [vspec],
        out_specs=[vspec, pl.BlockSpec((CH, vq, 64, LANES), lambda i: (i, 0, 0, 0)), vspec],
        scratch_shapes=[pltpu.VMEM((vq, 64, LANES), F32)],
        compiler_params=_cparams(("arbitrary",)),
    )(r, w, kp, al, be, v)


def _wkv_bwd(r, w, kp, al, be, v, dy, states, u):
    t, vq = v.shape[0], v.shape[1]
    vs = 64 // vq
    group = LANES // vs
    n = t // CH
    ng = CH // vs

    def body(r_ref, w_ref, kp_ref, al_ref, be_ref, v_ref, dy_ref, u_ref, a_ref, ap_ref,
             dr_ref, dw_ref, dkp_ref, dal_ref, dbe_ref, dv_ref, ds_ref):
        @pl.when(pl.program_id(0) == 0)
        def _():
            ds_ref[...] = jnp.zeros(ds_ref.shape, F32)

        earliest = pl.program_id(0) == n - 1

        def reverse(i, _):
            g = ng - 1 - i
            grp = _lane_group((64, LANES), group)
            outs = None
            for j in reversed(range(vs)):
                tl = g * vs + j
                rv, wv, kv, av, bv = r_ref[tl], w_ref[tl], kp_ref[tl], al_ref[tl], be_ref[tl]
                vals, dys, us = v_ref[tl], dy_ref[tl], u_ref[tl]
                acc = None
                dvrows = []
                for q in range(vq):
                    if j > 0:
                        s_prev = a_ref[tl - 1, q]
                    else:
                        before = jnp.where(earliest, 0.0, ap_ref[0, q])
                        s_prev = jnp.where(g == 0, before, a_ref[jnp.maximum(tl - 1, 0), q])
                    dyq = dys[q:q + 1]
                    ds = ds_ref[q] + rv * dyq
                    c = _ksum(ds * bv)
                    dvrows.append(_ksum(ds * kv))
                    terms = (a_ref[tl, q] * dyq, ds * s_prev, ds * vals[q:q + 1], s_prev * c, ds * us[q:q + 1])
                    acc = terms if acc is None else tuple(a + b for a, b in zip(acc, terms))
                    ds_ref[q] = ds * wv + av * c
                dv_ref[tl] = jnp.concatenate(dvrows, axis=0)
                summed = [_fold(a, group) for a in acc]
                outs = summed if outs is None else [jnp.where(grp == j, f, o) for f, o in zip(summed, outs)]
            for ref, o in zip((dr_ref, dw_ref, dkp_ref, dal_ref, dbe_ref), outs):
                ref[g] = o
            return 0

        lax.fori_loop(0, ng, reverse, 0)

    kspec = pl.BlockSpec((CH, 64, LANES), lambda i: (n - 1 - i, 0, 0))
    gspec = pl.BlockSpec((ng, 64, LANES), lambda i: (n - 1 - i, 0, 0))
    vspec = pl.BlockSpec((CH, vq, LANES), lambda i: (n - 1 - i, 0, 0))
    ksd = jax.ShapeDtypeStruct((t // vs, 64, LANES), F32)
    return pl.pallas_call(
        body, name="wkv_bwd", grid=(n,),
        out_shape=[ksd] * 5 + [jax.ShapeDtypeStruct((t, vq, LANES), F32)],
        in_specs=[kspec] * 5 + [vspec, vspec, vspec,
                                pl.BlockSpec((CH, vq, 64, LANES), lambda i: (n - 1 - i, 0, 0, 0)),
                                pl.BlockSpec((1, vq, 64, LANES), lambda i: (jnp.maximum((n - 1 - i) * CH - 1, 0), 0, 0, 0))],
        out_specs=[gspec] * 5 + [vspec],
        scratch_shapes=[pltpu.VMEM((vq, 64, LANES), F32)],
        compiler_params=_cparams(("arbitrary",)),
    )(r, w, kp, al, be, v, dy, u, states, states)


def _post(x, tgt, pp, o, yw, r, kp, v, ln_g, ln_b, r_k, wo, wot, gpost, bo):
    bsz, t, _ = x.shape
    nt = t // TT

    def body(x_ref, tgt_ref, z_ref, o_ref, yw_ref, r_ref, kp_ref, v_ref, lng_ref, lnb_ref, rk_ref, wo_ref, wot_ref,
             gpost_ref, bo_ref,
             dh_ref, dz_ref, dym_ref, dyw_ref, dbon_ref, loss_ref, dwo_ref, dgpost_ref, dlng_ref, dlnb_ref, drk_ref):
        first = (pl.program_id(0) == 0) & (pl.program_id(1) == 0)

        @pl.when(first)
        def _():
            for ref in (loss_ref, dwo_ref, dgpost_ref, dlng_ref, dlnb_ref, drk_ref):
                ref[...] = jnp.zeros(ref.shape, F32)

        bo_m = bo_ref[...]
        seg = lambda a: _seg(a, bo_m)
        rowsum = lambda a: jnp.sum(a, axis=0, keepdims=True)
        ywv, rv, kpv, vv = yw_ref[0], r_ref[0], kp_ref[0], v_ref[0]
        ln_g, r_k = lng_ref[...], rk_ref[...]
        mean = seg(ywv) * (1.0 / 64)
        yc = ywv - mean
        rstd = lax.rsqrt(seg(yc * yc) * (1.0 / 64) + GN_EPS)
        yhat = yc * rstd
        sb = seg(rv * kpv * r_k)
        y_rw = yhat * ln_g + lnb_ref[...] + sb * vv
        z = z_ref[0]
        sig = _sigmoid(z)
        sz = z * sig
        ycat = jnp.concatenate([o_ref[0], y_rw], axis=1)
        ycg = (ycat * sz).astype(BF16)
        out = _dot(ycg, wo_ref[...])
        hn, nx, rstd_o = _rms(out, gpost_ref[...], D)
        err = x_ref[0] + hn - tgt_ref[0]
        loss_ref[...] += jnp.sum(err * err) * (0.5 / D)
        dh = err * (1.0 / D)
        dh_ref[0] = dh
        dout, dgp = _rms_bwd(dh, nx, rstd_o, gpost_ref[...], D)
        dgpost_ref[...] += dgp
        doutb = dout.astype(BF16)
        dwo_ref[...] += _dot_tn(ycg, doutb)
        dycg = _dot(doutb, wot_ref[...])
        dz_ref[0] = dycg * ycat * (sig * (1.0 + z * (1.0 - sig)))
        dycat = dycg * sz
        dym_ref[0] = dycat[:, 0:512]
        dy_rw = dycat[:, 512:1024]
        dlnb_ref[...] += rowsum(dy_rw)
        dlng_ref[...] += rowsum(dy_rw * yhat)
        dyhat = dy_rw * ln_g
        dyw_ref[0] = rstd * (dyhat - seg(dyhat) * (1.0 / 64) - yhat * (seg(dyhat * yhat) * (1.0 / 64)))
        dsb = seg(dy_rw * vv)
        drk_ref[...] += rowsum(dsb * rv * kpv)
        dbon_ref[0, :, 0:512] = dsb * kpv * r_k
        dbon_ref[0, :, 512:1024] = dsb * rv * r_k
        dbon_ref[0, :, 1024:1536] = dy_rw * sb

    tok = lambda c: pl.BlockSpec((1, TT, c), lambda b, i: (b, i, 0))
    full = lambda a: _full(a.shape)
    ins = (x, tgt, pp, o, yw, r, kp, v, ln_g, ln_b, r_k, wo, wot, gpost, bo)
    in_specs = [tok(D), tok(D), tok(1024)] + [tok(512)] * 5 + [full(a) for a in ins[8:]]
    sd = lambda c: jax.ShapeDtypeStruct((bsz, t, c), F32)
    vec = lambda c: jax.ShapeDtypeStruct((1, c), F32)
    out_shape = [sd(D), sd(1024), sd(512), sd(512), sd(1536), jax.ShapeDtypeStruct((8, LANES), F32),
                 jax.ShapeDtypeStruct((1024, 1024), F32), vec(D), vec(512), vec(512), vec(512)]
    out_specs = [tok(D), tok(1024), tok(512), tok(512), tok(1536), _full((8, LANES)), _full((1024, 1024)),
                 _full((1, D)), _full((1, 512)), _full((1, 512)), _full((1, 512))]
    return pl.pallas_call(
        body, name="post", grid=(bsz, nt), out_shape=out_shape, in_specs=in_specs, out_specs=out_specs,
        compiler_params=_cparams(("arbitrary", "arbitrary")),
    )(*ins)


def _pre_bwd_a(pp, pos, invf, cqkv_w, mu, w0, w2p, w2pt, a0, a2p, a2pt, k_k, k_a, bo,
               dq, dk, dva, dwkv, dbon):
    gq, wuqt, gkv, wukvt = cqkv_w
    bsz, t, _ = pp.shape
    nt = t // TT
    dr_w, dw_w, dkp_w, dv_w, dal_w, dbe_w = dwkv

    def body(pp_ref, pos_ref, invf_ref, gq_ref, wuqt_ref, gkv_ref, wukvt_ref, mu_ref, w0_ref, w2p_ref, w2pt_ref,
             a0_ref, a2p_ref, a2pt_ref, kk_ref, ka_ref, bo_ref, dq_ref, dk_ref, dva_ref,
             dr_ref, dw_ref, dkp_ref, dv_ref, dal_ref, dbe_ref, dbon_ref,
             da_ref, dwuq_ref, dwukv_ref, dw2p_ref, da2p_ref, dgq_ref, dgkv_ref, dmu_ref, dw0_ref, da0_ref,
             dkk_ref, dka_ref, carry):
        i = pl.program_id(1)
        first = (pl.program_id(0) == 0) & (i == 0)

        @pl.when(first)
        def _():
            for ref in (dwuq_ref, dwukv_ref, dw2p_ref, da2p_ref, dgq_ref, dgkv_ref, dmu_ref, dw0_ref, da0_ref,
                        dkk_ref, dka_ref):
                ref[...] = jnp.zeros(ref.shape, F32)

        bo_m = bo_ref[...]
        rowsum = lambda a: jnp.sum(a, axis=0, keepdims=True)
        prw = pp_ref[0, :, RW0:DP]

        @pl.when(i == 0)
        def _():
            carry[...] = jnp.zeros(carry.shape, F32)

        ps, sh = _shift_mix(prw, carry[7:8, :], mu_ref[...])
        carry[...] = prw[TT - 8:TT, :]
        k_k, k_a = kk_ref[...], ka_ref[...]
        g = _rw_gates(ps, w0_ref[...], w2p_ref[...], a0_ref[...], a2p_ref[...], k_k, k_a, bo_m)
        a, kk, k = g["a"], g["kk"], g["k"]
        dr = dr_ref[0] + dbon_ref[0, :, 0:512]
        dkp = dkp_ref[0] + dbon_ref[0, :, 512:1024]
        dv = dv_ref[0] + dbon_ref[0, :, 1024:1536]
        dbe = dbe_ref[0]
        dkk = dbe * a - dal_ref[0]
        da = dbe * kk + dkp * k * k_a
        dka_ref[...] += rowsum(dkp * k * (a - 1.0))
        dm = (dkk - kk * _seg(dkk * kk, bo_m)) / g["nrm"]
        dkk_ref[...] += rowsum(dm * k)
        dk_tot = dkp * (1.0 + (a - 1.0) * k_a) + dm * k_k
        dapre = da * a * (1.0 - a)
        da0_ref[...] += rowsum(dapre)
        dapb = dapre.astype(BF16)
        da2p_ref[...] += _dot_tn(g["misc"].astype(BF16), dapb)
        dwpre = dw_ref[0] * g["w"] * (-g["e"]) * _sigmoid(-g["wpre"])
        dw0_ref[...] += rowsum(dwpre)
        dwpb = dwpre.astype(BF16)
        th = g["th"]
        dw2p_ref[...] += _dot_tn(th.astype(BF16), dwpb)
        dmisc = _dot(dapb, a2pt_ref[...]) + _dot(dwpb, w2pt_ref[...]) * (1.0 - th * th)
        ang = pos_ref[0] * invf_ref[...]
        cs, sn = jnp.cos(ang), jnp.sin(ang)
        unrope = lambda gr: gr * cs - _rot(gr * sn)
        lane = lax.broadcasted_iota(jnp.int32, cs.shape, 1)
        dkr = dk_ref[0, :, 128:256]
        for h in range(1, HEADS):
            dkr = dkr + dk_ref[0, :, 256 * h + 128:256 * h + 256]
        dkr = jnp.where(lane < 64, unrope(dkr), 0.0)
        dmisc = dmisc + jnp.concatenate([dkr, jnp.zeros_like(dkr)], axis=1)
        dqp = jnp.concatenate(
            [blk for h in range(HEADS)
             for blk in (dq_ref[0, :, 256 * h:256 * h + 128], unrope(dq_ref[0, :, 256 * h + 128:256 * h + 256]))],
            axis=1).astype(BF16)
        dkvp = jnp.concatenate([dk_ref[0, :, 256 * h:256 * h + 128] for h in range(HEADS)] + [dva_ref[0]],
                               axis=1).astype(BF16)
        cqn, cq_nx, cq_rstd = _rms(pp_ref[0, :, CQ0:CQ0 + 256], gq_ref[...], 256)
        ckvn, ckv_nx, ckv_rstd = _rms(pp_ref[0, :, CKV0:CKV0 + 128], gkv_ref[...], 128)
        dwuq_ref[...] += _dot_tn(cqn.astype(BF16), dqp)
        dwukv_ref[...] += _dot_tn(ckvn.astype(BF16), dkvp)
        dcq, dgq = _rms_bwd(_dot(dqp, wuqt_ref[...]), cq_nx, cq_rstd, gq_ref[...], 256)
        dckv, dgkv = _rms_bwd(_dot(dkvp, wukvt_ref[...]), ckv_nx, ckv_rstd, gkv_ref[...], 128)
        dgq_ref[...] += dgq
        dgkv_ref[...] += dgkv
        dps = jnp.concatenate([dr, dk_tot, dv, dmisc], axis=1)
        dmu_ref[...] += rowsum(dps * (sh - prw))
        da_ref[0, :, 0:256] = dcq
        da_ref[0, :, 256:384] = dckv
        da_ref[0, :, 384:384 + NRW] = dps

    tok = lambda c: pl.BlockSpec((1, TT, c), lambda b, i: (b, i, 0))
    full = lambda a: _full(a.shape)
    ins = (pp, pos, invf, gq, wuqt, gkv, wukvt, mu, w0, w2p, w2pt, a0, a2p, a2pt, k_k, k_a, bo,
           dq, dk, dva, dr_w, dw_w, dkp_w, dv_w, dal_w, dbe_w, dbon)
    in_specs = ([tok(DP), tok(1)] + [full(a) for a in ins[2:17]] + [tok(1024), tok(1024), tok(512)]
                + [tok(512)] * 6 + [tok(1536)])
    shp = lambda *s: jax.ShapeDtypeStruct(s, F32)
    out_shape = [shp(bsz, t, 384 + NRW), shp(256, 1024), shp(128, 1024), shp(256, 512), shp(256, 512),
                 shp(1, 256), shp(1, 128), shp(1, NRW), shp(1, 512), shp(1, 512), shp(1, 512), shp(1, 512)]
    out_specs = [tok(384 + NRW)] + [_full(s.shape) for s in out_shape[1:]]
    return pl.pallas_call(
        body, name="pre_bwd_a", grid=(bsz, nt), out_shape=out_shape, in_specs=in_specs, out_specs=out_specs,
        scratch_shapes=[pltpu.VMEM((8, NRW), F32)],
        compiler_params=_cparams(("arbitrary", "arbitrary")),
    )(*ins)


def _pre_bwd_b(x, dh, dz, da, mu, wpt, gpre):
    bsz, t, _ = x.shape
    nt = t // TT
    nblk = t // 8

    def body(x_ref, dh_ref, dz_ref, da_ref, nxt_ref, mu_ref, wpt_ref, gpre_ref, gx_ref, dp_ref, dgpre_ref):
        i = pl.program_id(1)
        first = (pl.program_id(0) == 0) & (i == 0)

        @pl.when(first)
        def _():
            dgpre_ref[...] = jnp.zeros(dgpre_ref.shape, F32)

        mu_v = mu_ref[...]
        dps = da_ref[0, :, 384:384 + NRW]
        nxt = jnp.where(i < nt - 1, nxt_ref[0, 0:1, 384:384 + NRW], 0.0)
        row = lax.broadcasted_iota(jnp.int32, dps.shape, 0)
        up = jnp.where(row == TT - 1, nxt, pltpu.roll(dps, TT - 1, 0))
        dprw = dps * (1.0 - mu_v) + up * mu_v
        dp = jnp.concatenate([dz_ref[0], da_ref[0, :, 0:384], dprw], axis=1).astype(BF16)
        dp_ref[0] = dp
        du = _dot(dp, wpt_ref[...])
        _, nx, rstd = _rms(x_ref[0], gpre_ref[...], D)
        dx, dg = _rms_bwd(du, nx, rstd, gpre_ref[...], D)
        dgpre_ref[...] += dg
        gx_ref[0] = dh_ref[0] + dx

    tok = lambda c: pl.BlockSpec((1, TT, c), lambda b, i: (b, i, 0))
    nxt_spec = pl.BlockSpec((1, 8, 384 + NRW), lambda b, i: (b, jnp.minimum((i + 1) * (TT // 8), nblk - 1), 0))
    ins = (x, dh, dz, da, da, mu, wpt, gpre)
    return pl.pallas_call(
        body, name="pre_bwd_b", grid=(bsz, nt),
        out_shape=[jax.ShapeDtypeStruct((bsz, t, D), F32), jax.ShapeDtypeStruct((bsz, t, DP), BF16),
                   jax.ShapeDtypeStruct((1, D), F32)],
        in_specs=[tok(D), tok(D), tok(1024), tok(384 + NRW), nxt_spec, _full(mu.shape), _full(wpt.shape),
                  _full(gpre.shape)],
        out_specs=[tok(D), tok(DP), _full((1, D))],
        compiler_params=_cparams(("arbitrary", "arbitrary")),
    )(*ins)


def _tn_matmul(a, b, bn, name, bk=512):
    kdim, m = a.shape
    _, n = b.shape
    nk = kdim // bk

    def body(a_ref, b_ref, o_ref):
        @pl.when(pl.program_id(1) == 0)
        def _():
            o_ref[...] = jnp.zeros(o_ref.shape, F32)

        o_ref[...] += _dot_tn(a_ref[...], b_ref[...])

    return pl.pallas_call(
        body, name=name, grid=(n // bn, nk),
        out_shape=jax.ShapeDtypeStruct((m, n), F32),
        in_specs=[pl.BlockSpec((bk, m), lambda j, kk: (kk, 0)), pl.BlockSpec((bk, bn), lambda j, kk: (kk, j))],
        out_specs=pl.BlockSpec((m, bn), lambda j, kk: (0, j)),
        compiler_params=_cparams(("parallel", "arbitrary")),
    )(a, b)


SHARDED = ("w_in", "mla_w_uq", "mla_w_ukv", "rw_w2", "rw_a2", "w_out")
SMALL = ("norm_pre_g", "mla_q_norm_g", "mla_kv_norm_g", "rw_mu", "rw_w0", "rw_a0", "rw_k_k", "rw_k_a", "rw_r_k",
         "rw_ln_g", "rw_ln_b", "norm_post_g")
WEIGHTS = ("norm_pre_g", "w_in", "mla_q_norm_g", "mla_w_uq", "mla_kv_norm_g", "mla_w_ukv", "rw_mu", "rw_w0", "rw_w2",
           "rw_a0", "rw_a2", "rw_k_k", "rw_k_a", "rw_r_k", "rw_ln_g", "rw_ln_b", "w_out", "norm_post_g")


def _pack_small(d):
    flat = jnp.concatenate([d[n].reshape(1, -1) for n in SMALL], axis=1)
    return jnp.pad(flat, ((0, 0), (0, SMALL_ROWS * LANES - flat.shape[1]))).reshape(SMALL_ROWS, LANES)


def _unpack_small(packed, like):
    flat = packed.reshape(1, -1)
    out, at = {}, 0
    for n in SMALL:
        size = int(np.prod(like[n].shape))
        out[n] = flat[:, at:at + size].reshape(like[n].shape)
        at += size
    return out


def _pack_shard(d):
    return jnp.concatenate([d[n].reshape(-1, LANES) for n in SHARDED], axis=0)


def _unpack_shard(packed, like):
    out, at = {}, 0
    for n, rows in zip(SHARDED, PACK_ROWS):
        out[n] = packed[at:at + rows].reshape(like[n].shape)
        at += rows
    return out


def _constants():
    bo = np.kron(np.eye(2, dtype=np.float32), np.ones((64, 64), np.float32))
    inv = ROPE_THETA ** (-np.arange(0, 64, 2, dtype=np.float32) / 64)
    invf = np.concatenate([inv, inv, np.zeros(64, np.float32)]).astype(np.float32)[None, :]
    return jnp.asarray(bo), jnp.asarray(invf)


def kernel(x, positions, norm_pre_g, w_in, mla_q_norm_g, mla_w_uq, mla_kv_norm_g, mla_w_ukv, rw_mu, rw_w0, rw_w2, rw_a0, rw_a2, rw_k_k, rw_k_a, rw_r_k, rw_ln_g, rw_ln_b, w_out, norm_post_g, loss_target, m_norm_pre_g, m_w_in, m_mla_q_norm_g, m_mla_w_uq, m_mla_kv_norm_g, m_mla_w_ukv, m_rw_mu, m_rw_w0, m_rw_w2, m_rw_a0, m_rw_a2, m_rw_k_k, m_rw_k_a, m_rw_r_k, m_rw_ln_g, m_rw_ln_b, m_w_out, m_norm_post_g, v_norm_pre_g, v_w_in, v_mla_q_norm_g, v_mla_w_uq, v_mla_kv_norm_g, v_mla_w_ukv, v_rw_mu, v_rw_w0, v_rw_w2, v_rw_a0, v_rw_a2, v_rw_k_k, v_rw_k_a, v_rw_r_k, v_rw_ln_g, v_rw_ln_b, v_w_out, v_norm_post_g):
    wts = dict(norm_pre_g=norm_pre_g, w_in=w_in, mla_q_norm_g=mla_q_norm_g, mla_w_uq=mla_w_uq,
               mla_kv_norm_g=mla_kv_norm_g, mla_w_ukv=mla_w_ukv, rw_mu=rw_mu, rw_w0=rw_w0, rw_w2=rw_w2, rw_a0=rw_a0,
               rw_a2=rw_a2, rw_k_k=rw_k_k, rw_k_a=rw_k_a, rw_r_k=rw_r_k, rw_ln_g=rw_ln_g, rw_ln_b=rw_ln_b, w_out=w_out,
               norm_post_g=norm_post_g)
    mom_m = dict(norm_pre_g=m_norm_pre_g, w_in=m_w_in, mla_q_norm_g=m_mla_q_norm_g, mla_w_uq=m_mla_w_uq,
                 mla_kv_norm_g=m_mla_kv_norm_g, mla_w_ukv=m_mla_w_ukv, rw_mu=m_rw_mu, rw_w0=m_rw_w0, rw_w2=m_rw_w2,
                 rw_a0=m_rw_a0, rw_a2=m_rw_a2, rw_k_k=m_rw_k_k, rw_k_a=m_rw_k_a, rw_r_k=m_rw_r_k, rw_ln_g=m_rw_ln_g,
                 rw_ln_b=m_rw_ln_b, w_out=m_w_out, norm_post_g=m_norm_post_g)
    mom_v = dict(norm_pre_g=v_norm_pre_g, w_in=v_w_in, mla_q_norm_g=v_mla_q_norm_g, mla_w_uq=v_mla_w_uq,
                 mla_kv_norm_g=v_mla_kv_norm_g, mla_w_ukv=v_mla_w_ukv, rw_mu=v_rw_mu, rw_w0=v_rw_w0, rw_w2=v_rw_w2,
                 rw_a0=v_rw_a0, rw_a2=v_rw_a2, rw_k_k=v_rw_k_k, rw_k_a=v_rw_k_a, rw_r_k=v_rw_r_k, rw_ln_g=v_rw_ln_g,
                 rw_ln_b=v_rw_ln_b, w_out=v_w_out, norm_post_g=v_norm_post_g)
    bsz, t, _ = x.shape
    bo, invf = _constants()
    c_idx = lax.axis_index("c")
    shard_idx = 2 * lax.axis_index("x") + lax.axis_index("y")

    g_in, g_uq, g_ukv, g_w2, g_a2, g_out = _ag_weights([wts[n][0] for n in SHARDED])
    w_in_f = jnp.transpose(g_in, (1, 0, 2)).reshape(D, D_IN)
    wp = jnp.concatenate([w_in_f[:, 2112:3136], w_in_f[:, 0:384], w_in_f[:, 448:1984], w_in_f[:, 384:448],
                          w_in_f[:, 1984:2112], jnp.zeros((D, 64), BF16)], axis=1)
    wuq = jnp.pad(jnp.transpose(g_uq, (1, 0, 2)).reshape(256, HEADS, 192), ((0, 0), (0, 0), (0, 64))).reshape(256, 1024)
    wukv = jnp.transpose(jnp.transpose(g_ukv, (1, 0, 2)).reshape(128, HEADS, 2, 128), (0, 2, 1, 3)).reshape(128, 1024)
    w2 = jnp.transpose(g_w2, (1, 0, 2)).reshape(64, RW)
    a2 = jnp.transpose(g_a2, (1, 0, 2)).reshape(64, RW)
    w2p = jnp.pad(w2, ((64, 128), (0, 0)))
    a2p = jnp.pad(a2, ((128, 64), (0, 0)))
    wo = g_out.reshape(D, D)
    mu = jnp.concatenate([rw_mu[:, 0:1536], jnp.zeros((1, 64), F32), rw_mu[:, 1536:1664], jnp.zeros((1, 64), F32)],
                         axis=1)
    r_k = rw_r_k.reshape(1, RW)
    pos = positions.astype(F32)[:, :, None]

    (u, pp, q_att, k_att, v_att, r, w, kp, v, al, be) = _pre_fwd(
        x, pos, invf, norm_pre_g, wp, mla_q_norm_g, wuq, mla_kv_norm_g, wukv, mu, rw_w0, w2p, rw_a0, a2p, rw_k_k,
        rw_k_a, bo)
    o, lse = _attn_fwd(q_att, k_att, v_att)
    rw_k = _spread_k([_to_k(a) for a in (r, w, kp, al, be)], _lane_split(bsz)[0])
    v_v = _to_v(v)
    yw_v, states, u_v = _wkv_fwd(*rw_k, v_v)
    yw = _from_v(yw_v, bsz)

    (dh, dz, dym, dyw, dbon, loss_acc, d_wo, d_gpost, d_lng, d_lnb, d_rk) = _post(
        x, loss_target, pp, o, yw, r, kp, v, rw_ln_g, rw_ln_b, r_k, wo, wo.T, norm_post_g, bo)
    loss = lax.psum(loss_acc[0, 0], ("x", "y", "c"))

    d_k = _wkv_bwd(*rw_k, v_v, _to_v(dyw), states, u_v)
    dr_w, dw_w, dkp_w, dal_w, dbe_w = (_from_k(a, bsz) for a in d_k[:5])
    dwkv = (dr_w, dw_w, dkp_w, _from_v(d_k[5], bsz), dal_w, dbe_w)
    dq, dk, dva = _attn_bwd(q_att, k_att, v_att, o, lse, dym)

    (da, d_wuq, d_wukv, d_w2p, d_a2p, d_gq, d_gkv, d_mu, d_w0, d_a0, d_kk, d_ka) = _pre_bwd_a(
        pp, pos, invf, (mla_q_norm_g, wuq.T, mla_kv_norm_g, wukv.T), mu, rw_w0, w2p, w2p.T, rw_a0, a2p, a2p.T,
        rw_k_k, rw_k_a, bo, dq, dk, dva, dwkv, dbon)
    grad_x, dpb, d_gpre = _pre_bwd_b(x, dh, dz, da, mu, wp.T, norm_pre_g)
    d_wp = _tn_matmul(u.reshape(bsz * t, D), dpb.reshape(bsz * t, DP), 640, "dw_in")

    full_g = {
        "w_in": jnp.concatenate([d_wp[:, 1024:1408], d_wp[:, 2944:3008], d_wp[:, 1408:2944], d_wp[:, 3008:3136],
                                 d_wp[:, 0:1024]], axis=1),
        "mla_w_uq": d_wuq.reshape(256, HEADS, 256)[:, :, :192].reshape(256, 768),
        "mla_w_ukv": jnp.transpose(d_wukv.reshape(128, 2, HEADS, 128), (0, 2, 1, 3)).reshape(128, 1024),
        "rw_w2": d_w2p[64:128],
        "rw_a2": d_a2p[128:192],
        "w_out": d_wo,
    }
    small_g = {
        "norm_pre_g": d_gpre, "mla_q_norm_g": d_gq, "mla_kv_norm_g": d_gkv,
        "rw_mu": jnp.concatenate([d_mu[:, 0:1536], d_mu[:, 1600:1728]], axis=1),
        "rw_w0": d_w0, "rw_a0": d_a0, "rw_k_k": d_kk, "rw_k_a": d_ka, "rw_r_k": d_rk, "rw_ln_g": d_lng,
        "rw_ln_b": d_lnb, "norm_post_g": d_gpost,
    }

    def by_shard(name, g):
        if name == "w_out":
            return g.reshape(N_SHARD, -1, LANES)
        rows, cols = g.shape
        return jnp.transpose(g.reshape(rows, N_SHARD, cols // N_SHARD), (1, 0, 2)).reshape(N_SHARD, -1, LANES)

    packed = jnp.concatenate([by_shard(n, full_g[n]) for n in SHARDED], axis=1)
    halves = packed.reshape(N_SHARD, 2, HALF, LANES)
    keep = lax.dynamic_index_in_dim(halves, c_idx, 1, keepdims=False)
    give = lax.dynamic_index_in_dim(halves, 1 - c_idx, 1, keepdims=False)
    got = _rs_pair_exchange(give)
    pair_sum = _add_n([keep.reshape(-1, LANES), got.reshape(-1, LANES)], "rs_pair_sum").reshape(N_SHARD, HALF, LANES)
    arrived = _rs_chip_exchange(pair_sum)
    own = lax.dynamic_index_in_dim(pair_sum, shard_idx, 0, keepdims=False)
    reduced_half = _add_n([own, arrived[0], arrived[1], arrived[2]], "rs_chip_sum")
    g_shard = _rs_pair_gather(reduced_half).reshape(PACK_TOTAL, LANES)

    g_small = _small_allreduce(_pack_small(small_g))

    shard_like = {n: wts[n][0] for n in SHARDED}
    d_sh, nm_sh, nv_sh = _adamw(_pack_shard({n: wts[n][0] for n in SHARDED}), g_shard,
                                _pack_shard({n: mom_m[n][0] for n in SHARDED}),
                                _pack_shard({n: mom_v[n][0] for n in SHARDED}), "adamw_sharded", 568)
    d_sm, nm_sm, nv_sm = _adamw(_pack_small(wts), g_small, _pack_small(mom_m), _pack_small(mom_v), "adamw_small",
                                SMALL_ROWS)

    def unpack(sh, sm):
        out = {n: a[None] for n, a in _unpack_shard(sh, shard_like).items()}
        out.update(_unpack_small(sm, wts))
        return out

    grads, deltas, new_m, new_v = unpack(g_shard, g_small), unpack(d_sh, d_sm), unpack(nm_sh, nm_sm), unpack(nv_sh, nv_sm)
    return (loss, grad_x, *[grads[n] for n in WEIGHTS], *[deltas[n] for n in WEIGHTS],
            *[new_m[n] for n in WEIGHTS], *[new_v[n] for n in WEIGHTS])
```

```python
import functools

import numpy as np
import jax
import jax.numpy as jnp
from jax import lax
from jax.experimental import pallas as pl
from jax.experimental.pallas import tpu as pltpu

F32, BF16 = jnp.float32, jnp.bfloat16
HIGHEST = lax.Precision.HIGHEST
MESH = pl.DeviceIdType.MESH

D = 1024
HEADS = 4
RW = 512
NORM_EPS = 1e-6
GN_EPS = 64e-5
ROPE_THETA = 10000.0
SCALE = (128 + 64) ** -0.5
D_IN = 3136
LR, B1, B2, ADAM_EPS, WD, STEP = 0.001, 0.9, 0.999, 1e-08, 0.01, 10

Z0, CQ0, CKV0, RW0, DP = 0, 1024, 1280, 1408, 3200
NRW = DP - RW0

LANES = 128
SUBLANES = 8
VMEM_LIMIT = 56 * 1024 * 1024

TT = 512
TT_VPU = 256
TQ = 512

N_SHARD = 4
PACK_ROWS = (1024 * 784 // 128, 256 * 192 // 128, 128 * 256 // 128, 64, 64, 256 * 1024 // 128)
PACK_TOTAL = sum(PACK_ROWS)
HALF = PACK_TOTAL // 2
SUM_ROWS = HALF // 4
SMALL_ROWS = 64


def _cparams(sem=None):
    return pltpu.CompilerParams(dimension_semantics=sem, vmem_limit_bytes=VMEM_LIMIT)


def _full(shape):
    n = len(shape)
    return pl.BlockSpec(shape, lambda *_: (0,) * n, pipeline_mode=pl.Buffered(1))


def _resident(shape):
    n = len(shape)
    return pl.BlockSpec(shape, lambda *_: (0,) * n)


def _dot(a, b):
    return jnp.dot(a, b, preferred_element_type=F32)


def _dot_nt(a, b):
    return lax.dot_general(a, b, (((1,), (1,)), ((), ())), preferred_element_type=F32)


def _dot_tn(a, b):
    return lax.dot_general(a, b, (((0,), (0,)), ((), ())), preferred_element_type=F32)


def _seg(x, bo):
    parts = [jnp.dot(x[:, LANES * i:LANES * (i + 1)], bo, precision=HIGHEST, preferred_element_type=F32)
             for i in range(x.shape[1] // LANES)]
    return parts[0] if len(parts) == 1 else jnp.concatenate(parts, axis=1)


def _rms(x, g, n):
    rstd = lax.rsqrt(jnp.sum(x * x, axis=-1, keepdims=True) * (1.0 / n) + NORM_EPS)
    nx = x * rstd
    return nx * g, nx, rstd


def _rms_bwd(dy, nx, rstd, g, n):
    dn = dy * g
    dx = rstd * (dn - nx * (jnp.sum(dn * nx, axis=-1, keepdims=True) * (1.0 / n)))
    return dx, jnp.sum(dy * nx, axis=0, keepdims=True)


def _rot(x):
    lane = lax.broadcasted_iota(jnp.int32, x.shape, 1)
    return jnp.where((lane % 64) < 32, -pltpu.roll(x, x.shape[1] - 32, 1), pltpu.roll(x, 32, 1))


def _sigmoid(x):
    return 1.0 / (1.0 + jnp.exp(-x))


def _softplus(x):
    return jnp.maximum(x, 0.0) + jnp.log(1.0 + jnp.exp(-jnp.abs(x)))


def _rw_gates(ps, w0, w2p, a0, a2p, k_k, k_a, bo):
    r, k, v, misc = ps[:, 0:512], ps[:, 512:1024], ps[:, 1024:1536], ps[:, 1536:NRW]
    th = jnp.tanh(misc)
    wpre = w0 + _dot(th.astype(BF16), w2p)
    e = jnp.exp(-_softplus(-wpre) - 0.5)
    w = jnp.exp(-e)
    a = _sigmoid(a0 + _dot(misc.astype(BF16), a2p))
    m = k * k_k
    nrm = jnp.maximum(jnp.sqrt(_seg(m * m, bo)), 1e-12)
    kk = m / nrm
    kp = k * (1.0 + (a - 1.0) * k_a)
    return dict(r=r, k=k, v=v, misc=misc, th=th, wpre=wpre, e=e, w=w, a=a, nrm=nrm, kk=kk, kp=kp)


def _shift_mix(prw, prev_row, mu):
    row = lax.broadcasted_iota(jnp.int32, prw.shape, 0)
    sh = jnp.where(row == 0, prev_row, pltpu.roll(prw, 1, 0))
    return prw + (sh - prw) * mu, sh


def _ag_weights(shards):
    n = len(shards)

    def body(*refs):
        ins, outs = refs[:n], refs[n:2 * n]
        ici_send, ici_recv, d2d_send, d2d_recv = refs[2 * n:2 * n + 4]
        x, y, c = lax.axis_index("x"), lax.axis_index("y"), lax.axis_index("c")
        mine = 2 * x + y
        for w in range(n):
            outs[w][mine] = ins[w][...].astype(BF16)
        flips = ((1, 0), (0, 1), (1, 1))

        def half(w, shard, cc):
            rows = outs[w].shape[1] // 2
            return outs[w].at[shard, pl.ds(pl.multiple_of(cc * rows, 16), rows)]

        def ici(w, k, shard):
            fx, fy = flips[k]
            return pltpu.make_async_remote_copy(
                src_ref=half(w, shard, c), dst_ref=half(w, shard, c),
                send_sem=ici_send.at[w * 3 + k], recv_sem=ici_recv.at[w * 3 + k],
                device_id=(x ^ fx, y ^ fy, c), device_id_type=MESH)

        def d2d(w, k, cc):
            fx, fy = flips[k]
            theirs = 2 * (x ^ fx) + (y ^ fy)
            return pltpu.make_async_remote_copy(
                src_ref=half(w, theirs, cc), dst_ref=half(w, theirs, cc),
                send_sem=d2d_send.at[w * 3 + k], recv_sem=d2d_recv.at[w * 3 + k],
                device_id=(x, y, 1 - c), device_id_type=MESH)

        for w in range(n):
            for k in range(3):
                ici(w, k, mine).start()
        for w in range(n):
            for k in range(3):
                fx, fy = flips[k]
                ici(w, k, 2 * (x ^ fx) + (y ^ fy)).wait_recv()
                d2d(w, k, c).start()
        for w in range(n):
            for k in range(3):
                d2d(w, k, 1 - c).wait_recv()
        for w in range(n):
            for k in range(3):
                ici(w, k, mine).wait_send()
                d2d(w, k, c).wait_send()

    vm = pl.BlockSpec(memory_space=pltpu.VMEM)
    return pl.pallas_call(
        body, name="ag_weights",
        out_shape=[jax.ShapeDtypeStruct((N_SHARD,) + s.shape, BF16) for s in shards],
        in_specs=[vm] * n, out_specs=[vm] * n,
        scratch_shapes=[pltpu.SemaphoreType.DMA((3 * n,))] * 4,
        compiler_params=pltpu.CompilerParams(vmem_limit_bytes=VMEM_LIMIT),
    )(*shards)


def _rs_pair_exchange(send_half):
    def body(src_ref, dst_ref, send_sem, recv_sem):
        x, y, c = lax.axis_index("x"), lax.axis_index("y"), lax.axis_index("c")
        cp = pltpu.make_async_remote_copy(src_ref=src_ref, dst_ref=dst_ref, send_sem=send_sem, recv_sem=recv_sem,
                                          device_id=(x, y, 1 - c), device_id_type=MESH)
        cp.start()
        cp.wait()

    hbm = pl.BlockSpec(memory_space=pl.ANY)
    return pl.pallas_call(
        body, name="rs_pair_exchange",
        out_shape=jax.ShapeDtypeStruct(send_half.shape, send_half.dtype),
        in_specs=[hbm], out_specs=hbm,
        scratch_shapes=[pltpu.SemaphoreType.DMA, pltpu.SemaphoreType.DMA],
    )(send_half)


def _rs_chip_exchange(part):
    def body(src_ref, dst_ref, send_sems, recv_sems):
        x, y, c = lax.axis_index("x"), lax.axis_index("y"), lax.axis_index("c")
        flips = ((1, 0), (0, 1), (1, 1))
        cps = []
        for k, (fx, fy) in enumerate(flips):
            theirs = 2 * (x ^ fx) + (y ^ fy)
            cps.append(pltpu.make_async_remote_copy(
                src_ref=src_ref.at[theirs], dst_ref=dst_ref.at[k],
                send_sem=send_sems.at[k], recv_sem=recv_sems.at[k],
                device_id=(x ^ fx, y ^ fy, c), device_id_type=MESH))
        for cp in cps:
            cp.start()
        for cp in cps:
            cp.wait()

    hbm = pl.BlockSpec(memory_space=pl.ANY)
    return pl.pallas_call(
        body, name="rs_chip_exchange",
        out_shape=jax.ShapeDtypeStruct((3,) + part.shape[1:], part.dtype),
        in_specs=[hbm], out_specs=hbm,
        scratch_shapes=[pltpu.SemaphoreType.DMA((3,)), pltpu.SemaphoreType.DMA((3,))],
    )(part)


def _rs_pair_gather(half):
    def body(src_ref, dst_ref, send_sem, recv_sem, local_sem):
        x, y, c = lax.axis_index("x"), lax.axis_index("y"), lax.axis_index("c")
        own = pltpu.make_async_copy(src_ref, dst_ref.at[c], local_sem)
        own.start()
        cp = pltpu.make_async_remote_copy(src_ref=src_ref, dst_ref=dst_ref.at[c], send_sem=send_sem, recv_sem=recv_sem,
                                          device_id=(x, y, 1 - c), device_id_type=MESH)
        cp.start()
        arrival = pltpu.make_async_remote_copy(src_ref=src_ref, dst_ref=dst_ref.at[1 - c], send_sem=send_sem,
                                               recv_sem=recv_sem, device_id=(x, y, 1 - c), device_id_type=MESH)
        arrival.wait_recv()
        cp.wait_send()
        own.wait()

    hbm = pl.BlockSpec(memory_space=pl.ANY)
    return pl.pallas_call(
        body, name="rs_pair_gather",
        out_shape=jax.ShapeDtypeStruct((2,) + half.shape, half.dtype),
        in_specs=[hbm], out_specs=hbm,
        scratch_shapes=[pltpu.SemaphoreType.DMA, pltpu.SemaphoreType.DMA, pltpu.SemaphoreType.DMA],
    )(half)


def _small_allreduce(vec):
    def body(in_ref, out_ref, recv, send_sems, recv_sems):
        x, y, c = lax.axis_index("x"), lax.axis_index("y"), lax.axis_index("c")
        me = 4 * x + 2 * y + c
        cps = []
        for k in range(1, 8):
            fx, fy, fc = (k >> 2) & 1, (k >> 1) & 1, k & 1
            cps.append(pltpu.make_async_remote_copy(
                src_ref=in_ref, dst_ref=recv.at[k - 1],
                send_sem=send_sems.at[k - 1], recv_sem=recv_sems.at[k - 1],
                device_id=(x ^ fx, y ^ fy, c ^ fc), device_id_type=MESH))
        for cp in cps:
            cp.start()
        for cp in cps:
            cp.wait()
        acc = jnp.zeros(in_ref.shape, F32)
        for j in range(8):
            slot = jnp.maximum((me ^ j) - 1, 0)
            acc = acc + jnp.where(me == j, in_ref[...], recv[slot])
        out_ref[...] = acc

    vm = pl.BlockSpec(memory_space=pltpu.VMEM)
    return pl.pallas_call(
        body, name="small_allreduce",
        out_shape=jax.ShapeDtypeStruct(vec.shape, F32),
        in_specs=[vm], out_specs=vm,
        scratch_shapes=[pltpu.VMEM((7,) + vec.shape, F32), pltpu.SemaphoreType.DMA((7,)),
                        pltpu.SemaphoreType.DMA((7,))],
    )(vec)


def _add_n(arrs, name, rows, also_bf16=False):
    n = len(arrs)
    r = arrs[0].shape[0]

    def body(*refs):
        acc = refs[0][...].astype(F32)
        for k in range(1, n):
            acc = acc + refs[k][...].astype(F32)
        refs[n][...] = acc
        if also_bf16:
            refs[n + 1][...] = acc.astype(BF16)

    spec = pl.BlockSpec((rows, LANES), lambda i: (i, 0))
    out_shape = [jax.ShapeDtypeStruct(arrs[0].shape, F32)]
    if also_bf16:
        out_shape.append(jax.ShapeDtypeStruct(arrs[0].shape, BF16))
    return pl.pallas_call(
        body, name=name, grid=(r // rows,),
        out_shape=out_shape,
        in_specs=[spec] * n, out_specs=[spec] * len(out_shape),
        compiler_params=_cparams(("parallel",)),
    )(*arrs)


def _adamw(w, g, m, v, name, rows):
    r = w.shape[0]

    def body(w_ref, g_ref, m_ref, v_ref, d_ref, nm_ref, nv_ref):
        gg = g_ref[...]
        nm = B1 * m_ref[...] + (1.0 - B1) * gg
        nv = B2 * v_ref[...] + (1.0 - B2) * (gg * gg)
        m_hat = nm / (1.0 - B1 ** STEP)
        v_hat = nv / (1.0 - B2 ** STEP)
        d_ref[...] = -LR * (m_hat / (jnp.sqrt(v_hat) + ADAM_EPS) + WD * w_ref[...])
        nm_ref[...] = nm
        nv_ref[...] = nv

    spec = pl.BlockSpec((rows, LANES), lambda i: (i, 0))
    sds = jax.ShapeDtypeStruct(w.shape, F32)
    return pl.pallas_call(
        body, name=name, grid=(r // rows,),
        out_shape=[sds, sds, sds],
        in_specs=[spec] * 4, out_specs=[spec] * 3,
        compiler_params=_cparams(("parallel",)),
    )(w, g, m, v)


def _pre_fwd(x, pos, invf, gpre, wp, gq, wuq, gkv, wukv, mu, w0, w2p, a0, a2p, k_k, k_a, bo):
    bsz, t, _ = x.shape
    nt = t // TT

    def body(x_ref, pos_ref, invf_ref, gpre_ref, wp_ref, gq_ref, wuq_ref, gkv_ref, wukv_ref, mu_ref, w0_ref,
             w2p_ref, a0_ref, a2p_ref, kk_ref, ka_ref, bo_ref,
             u_ref, pp_ref, q_ref, k_ref, v_ref, r_o, w_o, kp_o, vv_o, al_o, be_o, carry):
        i = pl.program_id(1)
        u, _, _ = _rms(x_ref[0], gpre_ref[...], D)
        ub = u.astype(BF16)
        u_ref[0] = ub
        p = _dot(ub, wp_ref[...])
        pp_ref[0] = p
        prw = p[:, RW0:DP]

        @pl.when(i == 0)
        def _():
            carry[...] = jnp.zeros(carry.shape, F32)

        ps, _ = _shift_mix(prw, carry[7:8, :], mu_ref[...])
        carry[...] = prw[TT - 8:TT, :]

        g = _rw_gates(ps, w0_ref[...], w2p_ref[...], a0_ref[...], a2p_ref[...], kk_ref[...], ka_ref[...],
                      bo_ref[...])
        r_o[0] = g["r"]
        w_o[0] = g["w"]
        kp_o[0] = g["kp"]
        vv_o[0] = g["v"]
        al_o[0] = -g["kk"]
        be_o[0] = g["kk"] * g["a"]

        cqn, _, _ = _rms(p[:, CQ0:CQ0 + 256], gq_ref[...], 256)
        q = _dot(cqn.astype(BF16), wuq_ref[...])
        ckvn, _, _ = _rms(p[:, CKV0:CKV0 + 128], gkv_ref[...], 128)
        kv = _dot(ckvn.astype(BF16), wukv_ref[...])
        ang = pos_ref[0] * invf_ref[...]
        cs, sn = jnp.cos(ang), jnp.sin(ang)
        lane = lax.broadcasted_iota(jnp.int32, cs.shape, 1)
        kr = ps[:, 1536:1536 + LANES]
        kr = jnp.where(lane < 64, kr * cs + _rot(kr) * sn, 0.0).astype(BF16)
        for h in range(HEADS):
            qr = q[:, 256 * h + 128:256 * h + 256]
            q_ref[0, :, 256 * h:256 * h + 128] = q[:, 256 * h:256 * h + 128].astype(BF16)
            q_ref[0, :, 256 * h + 128:256 * h + 256] = (qr * cs + _rot(qr) * sn).astype(BF16)
            k_ref[0, :, 256 * h:256 * h + 128] = kv[:, 128 * h:128 * h + 128].astype(BF16)
            k_ref[0, :, 256 * h + 128:256 * h + 256] = kr
        v_ref[0] = kv[:, 512:1024].astype(BF16)

    tok = lambda c: pl.BlockSpec((1, TT, c), lambda b, i: (b, i, 0))
    full = lambda a: _full(a.shape)
    ins = (x, pos, invf, gpre, wp, gq, wuq, gkv, wukv, mu, w0, w2p, a0, a2p, k_k, k_a, bo)
    in_specs = [tok(D), tok(1)] + [full(a) for a in ins[2:]]
    sd = lambda c, dt: jax.ShapeDtypeStruct((bsz, t, c), dt)
    out_shape = [sd(D, BF16), sd(DP, F32), sd(1024, BF16), sd(1024, BF16), sd(512, BF16)] + [sd(RW, F32)] * 6
    out_specs = [tok(D), tok(DP), tok(1024), tok(1024), tok(512)] + [tok(RW)] * 6
    return pl.pallas_call(
        body, name="pre_fwd", grid=(bsz, nt), out_shape=out_shape, in_specs=in_specs, out_specs=out_specs,
        scratch_shapes=[pltpu.VMEM((8, NRW), F32)],
        compiler_params=_cparams(("arbitrary", "arbitrary")),
    )(*ins)


def _attn_fwd(q, k, v):
    bsz, t, _ = q.shape
    nq = t // TQ

    def body(q_ref, k_ref, v_ref, o_ref, lse_ref):
        i = pl.program_id(2)
        qt = q_ref[0]
        row = lax.broadcasted_iota(jnp.int32, (TQ, TQ), 0) + i * TQ
        col0 = lax.broadcasted_iota(jnp.int32, (TQ, TQ), 1)

        def step(j, carry):
            m, l, acc = carry
            at = pl.ds(pl.multiple_of(j * TQ, TQ), TQ)
            s = _dot_nt(qt, k_ref[0, at, :]) * SCALE
            s = jnp.where(col0 + j * TQ <= row, s, -1e30)
            mn = jnp.maximum(m, jnp.max(s, axis=1, keepdims=True))
            p = jnp.exp(s - mn)
            al = jnp.exp(m - mn)
            l = al * l + jnp.sum(p, axis=1, keepdims=True)
            acc = al * acc + _dot(p.astype(BF16), v_ref[0, at, :])
            return mn, l, acc

        m, l, acc = lax.fori_loop(
            0, i + 1, step,
            (jnp.full((TQ, 1), -1e30, F32), jnp.zeros((TQ, 1), F32), jnp.zeros((TQ, LANES), F32)))
        o_ref[0] = acc / l
        lse_ref[0, 0] = jnp.broadcast_to(m + jnp.log(l), (TQ, LANES))

    return pl.pallas_call(
        body, name="attn_fwd", grid=(bsz, HEADS, nq),
        out_shape=[jax.ShapeDtypeStruct((bsz, t, 512), F32), jax.ShapeDtypeStruct((bsz, HEADS, t, LANES), F32)],
        in_specs=[pl.BlockSpec((1, TQ, 256), lambda b, h, i: (b, i, h)),
                  pl.BlockSpec((1, t, 256), lambda b, h, i: (b, 0, h)),
                  pl.BlockSpec((1, t, LANES), lambda b, h, i: (b, 0, h))],
        out_specs=[pl.BlockSpec((1, TQ, LANES), lambda b, h, i: (b, i, h)),
                   pl.BlockSpec((1, 1, TQ, LANES), lambda b, h, i: (b, h, i, 0))],
        compiler_params=_cparams(("parallel", "parallel", "arbitrary")),
    )(q, k, v)


def _attn_bwd(q, k, v, o, lse, do):
    bsz, t, _ = q.shape
    nq = t // TQ

    def body(q_ref, k_ref, v_ref, o_ref, lse_ref, do_ref, dq_ref, dk_ref, dv_ref, dl_ref):
        def prep(i, _):
            at = pl.ds(pl.multiple_of(i * TQ, TQ), TQ)
            dl_ref[at, :] = jnp.broadcast_to(jnp.sum(do_ref[0, at, :] * o_ref[0, at, :], axis=1, keepdims=True),
                                             (TQ, LANES))
            return 0

        lax.fori_loop(0, nq, prep, 0)
        dq_ref[0] = jnp.zeros((t, 256), F32)
        row0 = lax.broadcasted_iota(jnp.int32, (TQ, TQ), 0)
        col0 = lax.broadcasted_iota(jnp.int32, (TQ, TQ), 1)

        def kv_tile(j, _):
            atk = pl.ds(pl.multiple_of(j * TQ, TQ), TQ)
            kt = k_ref[0, atk, :]
            vt = v_ref[0, atk, :]

            def q_tile(i, carry):
                dk, dv = carry
                atq = pl.ds(pl.multiple_of(i * TQ, TQ), TQ)
                qt = q_ref[0, atq, :]
                dob = do_ref[0, atq, :].astype(BF16)
                s = _dot_nt(qt, kt) * SCALE
                s = jnp.where(col0 + j * TQ <= row0 + i * TQ, s, -1e30)
                p = jnp.exp(s - lse_ref[0, 0, atq, :][:, 0:1])
                dv = dv + _dot_tn(p.astype(BF16), dob)
                dp = _dot_nt(dob, vt)
                ds = (p * (dp - dl_ref[atq, :][:, 0:1]) * SCALE).astype(BF16)
                dk = dk + _dot_tn(ds, qt)
                dq_ref[0, atq, :] += _dot(ds, kt)
                return dk, dv

            dk, dv = lax.fori_loop(j, nq, q_tile, (jnp.zeros((TQ, 256), F32), jnp.zeros((TQ, LANES), F32)))
            dk_ref[0, atk, :] = dk
            dv_ref[0, atk, :] = dv
            return 0

        lax.fori_loop(0, nq, kv_tile, 0)

    s256 = pl.BlockSpec((1, t, 256), lambda b, h: (b, 0, h))
    s128 = pl.BlockSpec((1, t, LANES), lambda b, h: (b, 0, h))
    return pl.pallas_call(
        body, name="attn_bwd", grid=(bsz, HEADS),
        out_shape=[jax.ShapeDtypeStruct((bsz, t, 1024), F32), jax.ShapeDtypeStruct((bsz, t, 1024), F32),
                   jax.ShapeDtypeStruct((bsz, t, 512), F32)],
        in_specs=[s256, s256, s128, s128, pl.BlockSpec((1, 1, t, LANES), lambda b, h: (b, h, 0, 0)), s128],
        out_specs=[s256, s256, s128],
        scratch_shapes=[pltpu.VMEM((t, LANES), F32)],
        compiler_params=_cparams(("parallel", "parallel")),
    )(q, k, v, o, lse, do)


RW_HEADS = 8
CH = 16


def _lane_split(bsz):
    vs = LANES // (bsz * RW_HEADS)
    return vs, 64 // vs


def _to_k(x):
    bsz, t, _ = x.shape
    vs, _ = _lane_split(bsz)
    return jnp.transpose(x.reshape(bsz, t // vs, vs, RW_HEADS, 64), (1, 4, 2, 0, 3)).reshape(t // vs, 64, LANES)


def _from_k(y, bsz):
    vs, _ = _lane_split(bsz)
    tg = y.shape[0]
    return jnp.transpose(y.reshape(tg, 64, vs, bsz, RW_HEADS), (3, 0, 2, 4, 1)).reshape(bsz, tg * vs, RW)


def _to_v(x):
    bsz, t, _ = x.shape
    vs, vq = _lane_split(bsz)
    return jnp.transpose(x.reshape(bsz, t, RW_HEADS, vq, vs), (1, 3, 4, 0, 2)).reshape(t, vq, LANES)


def _from_v(y, bsz):
    t = y.shape[0]
    vs, vq = _lane_split(bsz)
    return jnp.transpose(y.reshape(t, vq, vs, bsz, RW_HEADS), (3, 0, 4, 1, 2)).reshape(bsz, t, RW)


def _ksum(a):
    return jnp.sum(a, axis=0, keepdims=True)


def _fold(a, group):
    sh = LANES // 2
    while sh >= group:
        a = a + pltpu.roll(a, sh, 1)
        sh //= 2
    return a


def _lane_group(shape, group):
    return lax.broadcasted_iota(jnp.int32, shape, 1) // group


def _spread(x, j, group):
    return _fold(jnp.where(_lane_group(x.shape, group) == j, x, 0.0), group)


def _spread_matrix(vs):
    group = LANES // vs
    p = np.zeros((LANES, vs * LANES), np.float32)
    for j in range(vs):
        for lane in range(LANES):
            p[j * group + lane % group, j * LANES + lane] = 1.0
    return jnp.asarray(p, BF16)


def _spread_k(xs, vs):
    tg = xs[0].shape[0]
    n = len(xs)
    gb = 8

    def body(*refs):
        pm = refs[n][...]
        for x_ref, o_ref in zip(refs[:n], refs[n + 1:]):
            x = x_ref[...].reshape(gb * 64, LANES)
            hi = x.astype(BF16)
            r1 = x - hi.astype(F32)
            mid = r1.astype(BF16)
            lo = (r1 - mid.astype(F32)).astype(BF16)
            res = _dot(hi, pm) + _dot(mid, pm) + _dot(lo, pm)
            for g in range(gb):
                for j in range(vs):
                    o_ref[g * vs + j] = res[g * 64:(g + 1) * 64, j * LANES:(j + 1) * LANES]

    pm = _spread_matrix(vs)
    return pl.pallas_call(
        body, name="wkv_spread", grid=(tg // gb,),
        out_shape=[jax.ShapeDtypeStruct((tg * vs, 64, LANES), F32)] * n,
        in_specs=[pl.BlockSpec((gb, 64, LANES), lambda i: (i, 0, 0))] * n + [_full(pm.shape)],
        out_specs=[pl.BlockSpec((gb * vs, 64, LANES), lambda i: (i, 0, 0))] * n,
        compiler_params=_cparams(("parallel",)),
    )(*xs, pm)


def _wkv_fwd(r, w, kp, al, be, v):
    t, vq = v.shape[0], v.shape[1]

    def body(r_ref, w_ref, kp_ref, al_ref, be_ref, v_ref, y_ref, a_ref, u_ref, st_ref):
        @pl.when(pl.program_id(0) == 0)
        def _():
            st_ref[...] = jnp.zeros(st_ref.shape, F32)

        def step(tl, _):
            rv, wv, kv, av, bv = r_ref[tl], w_ref[tl], kp_ref[tl], al_ref[tl], be_ref[tl]
            vals = v_ref[tl]
            yrows, urows = [], []
            for q in range(vq):
                s = st_ref[q]
                u = _ksum(s * av)
                s = s * wv + bv * u + kv * vals[q:q + 1]
                st_ref[q] = s
                a_ref[tl, q] = s
                urows.append(u)
                yrows.append(_ksum(s * rv))
            y_ref[tl] = jnp.concatenate(yrows, axis=0)
            u_ref[tl] = jnp.concatenate(urows, axis=0)
            return 0

        lax.fori_loop(0, CH, step, 0)

    kspec = pl.BlockSpec((CH, 64, LANES), lambda i: (i, 0, 0))
    vspec = pl.BlockSpec((CH, vq, LANES), lambda i: (i, 0, 0))
    vsd = jax.ShapeDtypeStruct((t, vq, LANES), F32)
    return pl.pallas_call(
        body, name="wkv_fwd", grid=(t // CH,),
        out_shape=[vsd, jax.ShapeDtypeStruct((t, vq, 64, LANES), F32), vsd],
        in_specs=[kspec] * 5 + [vspec],
        out_specs=[vspec, pl.BlockSpec((CH, vq, 64, LANES), lambda i: (i, 0, 0, 0)), vspec],
        scratch_shapes=[pltpu.VMEM((vq, 64, LANES), F32)],
        compiler_params=_cparams(("arbitrary",)),
    )(r, w, kp, al, be, v)


def _wkv_bwd(r, w, kp, al, be, v, dy, states, u):
    t, vq = v.shape[0], v.shape[1]
    vs = 64 // vq
    group = LANES // vs
    n = t // CH
    ng = CH // vs

    def body(r_ref, w_ref, kp_ref, al_ref, be_ref, v_ref, dy_ref, u_ref, a_ref, ap_ref,
             dr_ref, dw_ref, dkp_ref, dal_ref, dbe_ref, dv_ref, ds_ref):
        @pl.when(pl.program_id(0) == 0)
        def _():
            ds_ref[...] = jnp.zeros(ds_ref.shape, F32)

        earliest = pl.program_id(0) == n - 1

        def reverse(i, _):
            g = ng - 1 - i
            grp = _lane_group((64, LANES), group)
            outs = None
            for j in reversed(range(vs)):
                tl = g * vs + j
                rv, wv, kv, av, bv = r_ref[tl], w_ref[tl], kp_ref[tl], al_ref[tl], be_ref[tl]
                vals, dys, us = v_ref[tl], dy_ref[tl], u_ref[tl]
                acc = None
                dvrows = []
                for q in range(vq):
                    if j > 0:
                        s_prev = a_ref[tl - 1, q]
                    else:
                        before = jnp.where(earliest, 0.0, ap_ref[0, q])
                        s_prev = jnp.where(g == 0, before, a_ref[jnp.maximum(tl - 1, 0), q])
                    dyq = dys[q:q + 1]
                    ds = ds_ref[q] + rv * dyq
                    c = _ksum(ds * bv)
                    dvrows.append(_ksum(ds * kv))
                    terms = (a_ref[tl, q] * dyq, ds * s_prev, ds * vals[q:q + 1], s_prev * c, ds * us[q:q + 1])
                    acc = terms if acc is None else tuple(a + b for a, b in zip(acc, terms))
                    ds_ref[q] = ds * wv + av * c
                dv_ref[tl] = jnp.concatenate(dvrows, axis=0)
                summed = [_fold(a, group) for a in acc]
                outs = summed if outs is None else [jnp.where(grp == j, f, o) for f, o in zip(summed, outs)]
            for ref, o in zip((dr_ref, dw_ref, dkp_ref, dal_ref, dbe_ref), outs):
                ref[g] = o
            return 0

        lax.fori_loop(0, ng, reverse, 0)

    kspec = pl.BlockSpec((CH, 64, LANES), lambda i: (n - 1 - i, 0, 0))
    gspec = pl.BlockSpec((ng, 64, LANES), lambda i: (n - 1 - i, 0, 0))
    vspec = pl.BlockSpec((CH, vq, LANES), lambda i: (n - 1 - i, 0, 0))
    ksd = jax.ShapeDtypeStruct((t // vs, 64, LANES), F32)
    return pl.pallas_call(
        body, name="wkv_bwd", grid=(n,),
        out_shape=[ksd] * 5 + [jax.ShapeDtypeStruct((t, vq, LANES), F32)],
        in_specs=[kspec] * 5 + [vspec, vspec, vspec,
                                pl.BlockSpec((CH, vq, 64, LANES), lambda i: (n - 1 - i, 0, 0, 0)),
                                pl.BlockSpec((1, vq, 64, LANES), lambda i: (jnp.maximum((n - 1 - i) * CH - 1, 0), 0, 0, 0))],
        out_specs=[gspec] * 5 + [vspec],
        scratch_shapes=[pltpu.VMEM((vq, 64, LANES), F32)],
        compiler_params=_cparams(("arbitrary",)),
    )(r, w, kp, al, be, v, dy, u, states, states)


def _post(x, tgt, pp, o, yw, r, kp, v, ln_g, ln_b, r_k, wo, wot, gpost, bo):
    bsz, t, _ = x.shape
    tt = TT_VPU
    nt = t // tt

    def body(x_ref, tgt_ref, z_ref, o_ref, yw_ref, r_ref, kp_ref, v_ref, lng_ref, lnb_ref, rk_ref, wo_ref, wot_ref,
             gpost_ref, bo_ref,
             dh_ref, dz_ref, dym_ref, dyw_ref, dbon_ref, loss_ref, dwo_ref, dgpost_ref, dlng_ref, dlnb_ref, drk_ref):
        first = (pl.program_id(0) == 0) & (pl.program_id(1) == 0)

        @pl.when(first)
        def _():
            for ref in (loss_ref, dwo_ref, dgpost_ref, dlng_ref, dlnb_ref, drk_ref):
                ref[...] = jnp.zeros(ref.shape, F32)

        bo_m = bo_ref[...]
        seg = lambda a: _seg(a, bo_m)
        rowsum = lambda a: jnp.sum(a, axis=0, keepdims=True)
        ywv, rv, kpv, vv = yw_ref[0], r_ref[0], kp_ref[0], v_ref[0]
        ln_g, r_k = lng_ref[...], rk_ref[...]
        mean = seg(ywv) * (1.0 / 64)
        yc = ywv - mean
        rstd = lax.rsqrt(seg(yc * yc) * (1.0 / 64) + GN_EPS)
        yhat = yc * rstd
        sb = seg(rv * kpv * r_k)
        y_rw = yhat * ln_g + lnb_ref[...] + sb * vv
        z = z_ref[0]
        sig = _sigmoid(z)
        sz = z * sig
        ycat = jnp.concatenate([o_ref[0], y_rw], axis=1)
        ycg = (ycat * sz).astype(BF16)
        out = _dot(ycg, wo_ref[...])
        hn, nx, rstd_o = _rms(out, gpost_ref[...], D)
        err = x_ref[0] + hn - tgt_ref[0]
        loss_ref[...] += jnp.sum(err * err) * (0.5 / D)
        dh = err * (1.0 / D)
        dh_ref[0] = dh
        dout, dgp = _rms_bwd(dh, nx, rstd_o, gpost_ref[...], D)
        dgpost_ref[...] += dgp
        doutb = dout.astype(BF16)
        dwo_ref[...] += _dot_tn(ycg, doutb)
        dycg = _dot(doutb, wot_ref[...])
        dz_ref[0] = dycg * ycat * (sig * (1.0 + z * (1.0 - sig)))
        dycat = dycg * sz
        dym_ref[0] = dycat[:, 0:512]
        dy_rw = dycat[:, 512:1024]
        dlnb_ref[...] += rowsum(dy_rw)
        dlng_ref[...] += rowsum(dy_rw * yhat)
        dyhat = dy_rw * ln_g
        dyw_ref[0] = rstd * (dyhat - seg(dyhat) * (1.0 / 64) - yhat * (seg(dyhat * yhat) * (1.0 / 64)))
        dsb = seg(dy_rw * vv)
        drk_ref[...] += rowsum(dsb * rv * kpv)
        dbon_ref[0, :, 0:512] = dsb * kpv * r_k
        dbon_ref[0, :, 512:1024] = dsb * rv * r_k
        dbon_ref[0, :, 1024:1536] = dy_rw * sb

    tok = lambda c: pl.BlockSpec((1, tt, c), lambda b, i: (b, i, 0))
    full = lambda a: _full(a.shape)
    ins = (x, tgt, pp, o, yw, r, kp, v, ln_g, ln_b, r_k, wo, wot, gpost, bo)
    in_specs = [tok(D), tok(D), tok(1024)] + [tok(512)] * 5 + [full(a) for a in ins[8:]]
    sd = lambda c: jax.ShapeDtypeStruct((bsz, t, c), F32)
    vec = lambda c: jax.ShapeDtypeStruct((1, c), F32)
    out_shape = [sd(D), sd(1024), sd(512), sd(512), sd(1536), jax.ShapeDtypeStruct((8, LANES), F32),
                 jax.ShapeDtypeStruct((1024, 1024), F32), vec(D), vec(512), vec(512), vec(512)]
    out_specs = [tok(D), tok(1024), tok(512), tok(512), tok(1536), _resident((8, LANES)), _resident((1024, 1024)),
                 _resident((1, D)), _resident((1, 512)), _resident((1, 512)), _resident((1, 512))]
    return pl.pallas_call(
        body, name="post", grid=(bsz, nt), out_shape=out_shape, in_specs=in_specs, out_specs=out_specs,
        compiler_params=_cparams(("arbitrary", "arbitrary")),
    )(*ins)


def _pre_bwd_a(pp, pos, invf, cqkv_w, mu, w0, w2p, w2pt, a0, a2p, a2pt, k_k, k_a, bo,
               dq, dk, dva, dwkv, dbon):
    gq, wuqt, gkv, wukvt = cqkv_w
    bsz, t, _ = pp.shape
    tt = TT_VPU
    nt = t // tt
    dr_w, dw_w, dkp_w, dv_w, dal_w, dbe_w = dwkv

    def body(pp_ref, pos_ref, invf_ref, gq_ref, wuqt_ref, gkv_ref, wukvt_ref, mu_ref, w0_ref, w2p_ref, w2pt_ref,
             a0_ref, a2p_ref, a2pt_ref, kk_ref, ka_ref, bo_ref, dq_ref, dk_ref, dva_ref,
             dr_ref, dw_ref, dkp_ref, dv_ref, dal_ref, dbe_ref, dbon_ref,
             da_ref, dwuq_ref, dwukv_ref, dw2p_ref, da2p_ref, dgq_ref, dgkv_ref, dmu_ref, dw0_ref, da0_ref,
             dkk_ref, dka_ref, carry):
        i = pl.program_id(1)
        first = (pl.program_id(0) == 0) & (i == 0)

        @pl.when(first)
        def _():
            for ref in (dwuq_ref, dwukv_ref, dw2p_ref, da2p_ref, dgq_ref, dgkv_ref, dmu_ref, dw0_ref, da0_ref,
                        dkk_ref, dka_ref):
                ref[...] = jnp.zeros(ref.shape, F32)

        bo_m = bo_ref[...]
        rowsum = lambda a: jnp.sum(a, axis=0, keepdims=True)
        prw = pp_ref[0, :, RW0:DP]

        @pl.when(i == 0)
        def _():
            carry[...] = jnp.zeros(carry.shape, F32)

        ps, sh = _shift_mix(prw, carry[7:8, :], mu_ref[...])
        carry[...] = prw[tt - 8:tt, :]
        k_k, k_a = kk_ref[...], ka_ref[...]
        g = _rw_gates(ps, w0_ref[...], w2p_ref[...], a0_ref[...], a2p_ref[...], k_k, k_a, bo_m)
        a, kk, k = g["a"], g["kk"], g["k"]
        dr = dr_ref[0] + dbon_ref[0, :, 0:512]
        dkp = dkp_ref[0] + dbon_ref[0, :, 512:1024]
        dv = dv_ref[0] + dbon_ref[0, :, 1024:1536]
        dbe = dbe_ref[0]
        dkk = dbe * a - dal_ref[0]
        da = dbe * kk + dkp * k * k_a
        dka_ref[...] += rowsum(dkp * k * (a - 1.0))
        dm = (dkk - kk * _seg(dkk * kk, bo_m)) / g["nrm"]
        dkk_ref[...] += rowsum(dm * k)
        dk_tot = dkp * (1.0 + (a - 1.0) * k_a) + dm * k_k
        dapre = da * a * (1.0 - a)
        da0_ref[...] += rowsum(dapre)
        dapb = dapre.astype(BF16)
        da2p_ref[...] += _dot_tn(g["misc"].astype(BF16), dapb)
        dwpre = dw_ref[0] * g["w"] * (-g["e"]) * _sigmoid(-g["wpre"])
        dw0_ref[...] += rowsum(dwpre)
        dwpb = dwpre.astype(BF16)
        th = g["th"]
        dw2p_ref[...] += _dot_tn(th.astype(BF16), dwpb)
        dmisc = _dot(dapb, a2pt_ref[...]) + _dot(dwpb, w2pt_ref[...]) * (1.0 - th * th)
        ang = pos_ref[0] * invf_ref[...]
        cs, sn = jnp.cos(ang), jnp.sin(ang)
        unrope = lambda gr: gr * cs - _rot(gr * sn)
        lane = lax.broadcasted_iota(jnp.int32, cs.shape, 1)
        dkr = dk_ref[0, :, 128:256]
        for h in range(1, HEADS):
            dkr = dkr + dk_ref[0, :, 256 * h + 128:256 * h + 256]
        dkr = jnp.where(lane < 64, unrope(dkr), 0.0)
        dmisc = dmisc + jnp.concatenate([dkr, jnp.zeros_like(dkr)], axis=1)
        dqp = jnp.concatenate(
            [blk for h in range(HEADS)
             for blk in (dq_ref[0, :, 256 * h:256 * h + 128], unrope(dq_ref[0, :, 256 * h + 128:256 * h + 256]))],
            axis=1).astype(BF16)
        dkvp = jnp.concatenate([dk_ref[0, :, 256 * h:256 * h + 128] for h in range(HEADS)] + [dva_ref[0]],
                               axis=1).astype(BF16)
        cqn, cq_nx, cq_rstd = _rms(pp_ref[0, :, CQ0:CQ0 + 256], gq_ref[...], 256)
        ckvn, ckv_nx, ckv_rstd = _rms(pp_ref[0, :, CKV0:CKV0 + 128], gkv_ref[...], 128)
        dwuq_ref[...] += _dot_tn(cqn.astype(BF16), dqp)
        dwukv_ref[...] += _dot_tn(ckvn.astype(BF16), dkvp)
        dcq, dgq = _rms_bwd(_dot(dqp, wuqt_ref[...]), cq_nx, cq_rstd, gq_ref[...], 256)
        dckv, dgkv = _rms_bwd(_dot(dkvp, wukvt_ref[...]), ckv_nx, ckv_rstd, gkv_ref[...], 128)
        dgq_ref[...] += dgq
        dgkv_ref[...] += dgkv
        dps = jnp.concatenate([dr, dk_tot, dv, dmisc], axis=1)
        dmu_ref[...] += rowsum(dps * (sh - prw))
        da_ref[0, :, 0:256] = dcq
        da_ref[0, :, 256:384] = dckv
        da_ref[0, :, 384:384 + NRW] = dps

    tok = lambda c: pl.BlockSpec((1, tt, c), lambda b, i: (b, i, 0))
    full = lambda a: _full(a.shape)
    ins = (pp, pos, invf, gq, wuqt, gkv, wukvt, mu, w0, w2p, w2pt, a0, a2p, a2pt, k_k, k_a, bo,
           dq, dk, dva, dr_w, dw_w, dkp_w, dv_w, dal_w, dbe_w, dbon)
    in_specs = ([tok(DP), tok(1)] + [full(a) for a in ins[2:17]] + [tok(1024), tok(1024), tok(512)]
                + [tok(512)] * 6 + [tok(1536)])
    shp = lambda *s: jax.ShapeDtypeStruct(s, F32)
    out_shape = [shp(bsz, t, 384 + NRW), shp(256, 1024), shp(128, 1024), shp(256, 512), shp(256, 512),
                 shp(1, 256), shp(1, 128), shp(1, NRW), shp(1, 512), shp(1, 512), shp(1, 512), shp(1, 512)]
    out_specs = [tok(384 + NRW)] + [_resident(s.shape) for s in out_shape[1:]]
    return pl.pallas_call(
        body, name="pre_bwd_a", grid=(bsz, nt), out_shape=out_shape, in_specs=in_specs, out_specs=out_specs,
        scratch_shapes=[pltpu.VMEM((8, NRW), F32)],
        compiler_params=_cparams(("arbitrary", "arbitrary")),
    )(*ins)


def _pre_bwd_b(x, dh, dz, da, mu, wpt, gpre):
    bsz, t, _ = x.shape
    nt = t // TT
    nblk = t // 8

    def body(x_ref, dh_ref, dz_ref, da_ref, nxt_ref, mu_ref, wpt_ref, gpre_ref, gx_ref, dp_ref, dgpre_ref):
        i = pl.program_id(1)
        first = (pl.program_id(0) == 0) & (i == 0)

        @pl.when(first)
        def _():
            dgpre_ref[...] = jnp.zeros(dgpre_ref.shape, F32)

        mu_v = mu_ref[...]
        dps = da_ref[0, :, 384:384 + NRW]
        nxt = jnp.where(i < nt - 1, nxt_ref[0, 0:1, 384:384 + NRW], 0.0)
        row = lax.broadcasted_iota(jnp.int32, dps.shape, 0)
        up = jnp.where(row == TT - 1, nxt, pltpu.roll(dps, TT - 1, 0))
        dprw = dps * (1.0 - mu_v) + up * mu_v
        dp = jnp.concatenate([dz_ref[0], da_ref[0, :, 0:384], dprw], axis=1).astype(BF16)
        dp_ref[0] = dp
        du = _dot(dp, wpt_ref[...])
        _, nx, rstd = _rms(x_ref[0], gpre_ref[...], D)
        dx, dg = _rms_bwd(du, nx, rstd, gpre_ref[...], D)
        dgpre_ref[...] += dg
        gx_ref[0] = dh_ref[0] + dx

    tok = lambda c: pl.BlockSpec((1, TT, c), lambda b, i: (b, i, 0))
    nxt_spec = pl.BlockSpec((1, 8, 384 + NRW), lambda b, i: (b, jnp.minimum((i + 1) * (TT // 8), nblk - 1), 0))
    ins = (x, dh, dz, da, da, mu, wpt, gpre)
    return pl.pallas_call(
        body, name="pre_bwd_b", grid=(bsz, nt),
        out_shape=[jax.ShapeDtypeStruct((bsz, t, D), F32), jax.ShapeDtypeStruct((bsz, t, DP), BF16),
                   jax.ShapeDtypeStruct((1, D), F32)],
        in_specs=[tok(D), tok(D), tok(1024), tok(384 + NRW), nxt_spec, _full(mu.shape), _full(wpt.shape),
                  _full(gpre.shape)],
        out_specs=[tok(D), tok(DP), _resident((1, D))],
        compiler_params=_cparams(("arbitrary", "arbitrary")),
    )(*ins)


def _tn_matmul(a, b, bn, name, bk=512):
    kdim, m = a.shape
    _, n = b.shape
    nk = kdim // bk

    def body(a_ref, b_ref, o_ref):
        @pl.when(pl.program_id(1) == 0)
        def _():
            o_ref[...] = jnp.zeros(o_ref.shape, F32)

        o_ref[...] += _dot_tn(a_ref[...], b_ref[...])

    return pl.pallas_call(
        body, name=name, grid=(n // bn, nk),
        out_shape=jax.ShapeDtypeStruct((m, n), F32),
        in_specs=[pl.BlockSpec((bk, m), lambda j, kk: (kk, 0)), pl.BlockSpec((bk, bn), lambda j, kk: (kk, j))],
        out_specs=pl.BlockSpec((m, bn), lambda j, kk: (0, j)),
        compiler_params=_cparams(("parallel", "arbitrary")),
    )(a, b)


SHARDED = ("w_in", "mla_w_uq", "mla_w_ukv", "rw_w2", "rw_a2", "w_out")
SMALL = ("norm_pre_g", "mla_q_norm_g", "mla_kv_norm_g", "rw_mu", "rw_w0", "rw_a0", "rw_k_k", "rw_k_a", "rw_r_k",
         "rw_ln_g", "rw_ln_b", "norm_post_g")
WEIGHTS = ("norm_pre_g", "w_in", "mla_q_norm_g", "mla_w_uq", "mla_kv_norm_g", "mla_w_ukv", "rw_mu", "rw_w0", "rw_w2",
           "rw_a0", "rw_a2", "rw_k_k", "rw_k_a", "rw_r_k", "rw_ln_g", "rw_ln_b", "w_out", "norm_post_g")


def _pack_small(d):
    flat = jnp.concatenate([d[n].reshape(1, -1) for n in SMALL], axis=1)
    return jnp.pad(flat, ((0, 0), (0, SMALL_ROWS * LANES - flat.shape[1]))).reshape(SMALL_ROWS, LANES)


def _unpack_small(packed, like):
    flat = packed.reshape(1, -1)
    out, at = {}, 0
    for n in SMALL:
        size = int(np.prod(like[n].shape))
        out[n] = flat[:, at:at + size].reshape(like[n].shape)
        at += size
    return out


def _pack_shard(d):
    return jnp.concatenate([d[n].reshape(-1, LANES) for n in SHARDED], axis=0)


def _unpack_shard(packed, like):
    out, at = {}, 0
    for n, rows in zip(SHARDED, PACK_ROWS):
        out[n] = packed[at:at + rows].reshape(like[n].shape)
        at += rows
    return out


def _constants():
    bo = np.kron(np.eye(2, dtype=np.float32), np.ones((64, 64), np.float32))
    inv = ROPE_THETA ** (-np.arange(0, 64, 2, dtype=np.float32) / 64)
    invf = np.concatenate([inv, inv, np.zeros(64, np.float32)]).astype(np.float32)[None, :]
    return jnp.asarray(bo), jnp.asarray(invf)


def kernel(x, positions, norm_pre_g, w_in, mla_q_norm_g, mla_w_uq, mla_kv_norm_g, mla_w_ukv, rw_mu, rw_w0, rw_w2, rw_a0, rw_a2, rw_k_k, rw_k_a, rw_r_k, rw_ln_g, rw_ln_b, w_out, norm_post_g, loss_target, m_norm_pre_g, m_w_in, m_mla_q_norm_g, m_mla_w_uq, m_mla_kv_norm_g, m_mla_w_ukv, m_rw_mu, m_rw_w0, m_rw_w2, m_rw_a0, m_rw_a2, m_rw_k_k, m_rw_k_a, m_rw_r_k, m_rw_ln_g, m_rw_ln_b, m_w_out, m_norm_post_g, v_norm_pre_g, v_w_in, v_mla_q_norm_g, v_mla_w_uq, v_mla_kv_norm_g, v_mla_w_ukv, v_rw_mu, v_rw_w0, v_rw_w2, v_rw_a0, v_rw_a2, v_rw_k_k, v_rw_k_a, v_rw_r_k, v_rw_ln_g, v_rw_ln_b, v_w_out, v_norm_post_g):
    wts = dict(norm_pre_g=norm_pre_g, w_in=w_in, mla_q_norm_g=mla_q_norm_g, mla_w_uq=mla_w_uq,
               mla_kv_norm_g=mla_kv_norm_g, mla_w_ukv=mla_w_ukv, rw_mu=rw_mu, rw_w0=rw_w0, rw_w2=rw_w2, rw_a0=rw_a0,
               rw_a2=rw_a2, rw_k_k=rw_k_k, rw_k_a=rw_k_a, rw_r_k=rw_r_k, rw_ln_g=rw_ln_g, rw_ln_b=rw_ln_b, w_out=w_out,
               norm_post_g=norm_post_g)
    mom_m = dict(norm_pre_g=m_norm_pre_g, w_in=m_w_in, mla_q_norm_g=m_mla_q_norm_g, mla_w_uq=m_mla_w_uq,
                 mla_kv_norm_g=m_mla_kv_norm_g, mla_w_ukv=m_mla_w_ukv, rw_mu=m_rw_mu, rw_w0=m_rw_w0, rw_w2=m_rw_w2,
                 rw_a0=m_rw_a0, rw_a2=m_rw_a2, rw_k_k=m_rw_k_k, rw_k_a=m_rw_k_a, rw_r_k=m_rw_r_k, rw_ln_g=m_rw_ln_g,
                 rw_ln_b=m_rw_ln_b, w_out=m_w_out, norm_post_g=m_norm_post_g)
    mom_v = dict(norm_pre_g=v_norm_pre_g, w_in=v_w_in, mla_q_norm_g=v_mla_q_norm_g, mla_w_uq=v_mla_w_uq,
                 mla_kv_norm_g=v_mla_kv_norm_g, mla_w_ukv=v_mla_w_ukv, rw_mu=v_rw_mu, rw_w0=v_rw_w0, rw_w2=v_rw_w2,
                 rw_a0=v_rw_a0, rw_a2=v_rw_a2, rw_k_k=v_rw_k_k, rw_k_a=v_rw_k_a, rw_r_k=v_rw_r_k, rw_ln_g=v_rw_ln_g,
                 rw_ln_b=v_rw_ln_b, w_out=v_w_out, norm_post_g=v_norm_post_g)
    bsz, t, _ = x.shape
    bo, invf = _constants()
    c_idx = lax.axis_index("c")
    shard_idx = 2 * lax.axis_index("x") + lax.axis_index("y")

    g_in, g_uq, g_ukv, g_w2, g_a2, g_out = _ag_weights([wts[n][0] for n in SHARDED])
    w_in_f = jnp.transpose(g_in, (1, 0, 2)).reshape(D, D_IN)
    wp = jnp.concatenate([w_in_f[:, 2112:3136], w_in_f[:, 0:384], w_in_f[:, 448:1984], w_in_f[:, 384:448],
                          w_in_f[:, 1984:2112], jnp.zeros((D, 64), BF16)], axis=1)
    wuq = jnp.pad(jnp.transpose(g_uq, (1, 0, 2)).reshape(256, HEADS, 192), ((0, 0), (0, 0), (0, 64))).reshape(256, 1024)
    wukv = jnp.transpose(jnp.transpose(g_ukv, (1, 0, 2)).reshape(128, HEADS, 2, 128), (0, 2, 1, 3)).reshape(128, 1024)
    w2 = jnp.transpose(g_w2, (1, 0, 2)).reshape(64, RW)
    a2 = jnp.transpose(g_a2, (1, 0, 2)).reshape(64, RW)
    w2p = jnp.pad(w2, ((64, 128), (0, 0)))
    a2p = jnp.pad(a2, ((128, 64), (0, 0)))
    wo = g_out.reshape(D, D)
    mu = jnp.concatenate([rw_mu[:, 0:1536], jnp.zeros((1, 64), F32), rw_mu[:, 1536:1664], jnp.zeros((1, 64), F32)],
                         axis=1)
    r_k = rw_r_k.reshape(1, RW)
    pos = positions.astype(F32)[:, :, None]

    (u, pp, q_att, k_att, v_att, r, w, kp, v, al, be) = _pre_fwd(
        x, pos, invf, norm_pre_g, wp, mla_q_norm_g, wuq, mla_kv_norm_g, wukv, mu, rw_w0, w2p, rw_a0, a2p, rw_k_k,
        rw_k_a, bo)
    o, lse = _attn_fwd(q_att, k_att, v_att)
    rw_k = _spread_k([_to_k(a) for a in (r, w, kp, al, be)], _lane_split(bsz)[0])
    v_v = _to_v(v)
    yw_v, states, u_v = _wkv_fwd(*rw_k, v_v)
    yw = _from_v(yw_v, bsz)

    (dh, dz, dym, dyw, dbon, loss_acc, d_wo, d_gpost, d_lng, d_lnb, d_rk) = _post(
        x, loss_target, pp, o, yw, r, kp, v, rw_ln_g, rw_ln_b, r_k, wo, wo.T, norm_post_g, bo)
    loss = lax.psum(loss_acc[0, 0], ("x", "y", "c"))

    d_k = _wkv_bwd(*rw_k, v_v, _to_v(dyw), states, u_v)
    dr_w, dw_w, dkp_w, dal_w, dbe_w = (_from_k(a, bsz) for a in d_k[:5])
    dwkv = (dr_w, dw_w, dkp_w, _from_v(d_k[5], bsz), dal_w, dbe_w)
    dq, dk, dva = _attn_bwd(q_att, k_att, v_att, o, lse, dym)

    (da, d_wuq, d_wukv, d_w2p, d_a2p, d_gq, d_gkv, d_mu, d_w0, d_a0, d_kk, d_ka) = _pre_bwd_a(
        pp, pos, invf, (mla_q_norm_g, wuq.T, mla_kv_norm_g, wukv.T), mu, rw_w0, w2p, w2p.T, rw_a0, a2p, a2p.T,
        rw_k_k, rw_k_a, bo, dq, dk, dva, dwkv, dbon)
    grad_x, dpb, d_gpre = _pre_bwd_b(x, dh, dz, da, mu, wp.T, norm_pre_g)
    d_wp = _tn_matmul(u.reshape(bsz * t, D), dpb.reshape(bsz * t, DP), 640, "dw_in")

    full_g = {
        "w_in": jnp.concatenate([d_wp[:, 1024:1408], d_wp[:, 2944:3008], d_wp[:, 1408:2944], d_wp[:, 3008:3136],
                                 d_wp[:, 0:1024]], axis=1),
        "mla_w_uq": d_wuq.reshape(256, HEADS, 256)[:, :, :192].reshape(256, 768),
        "mla_w_ukv": jnp.transpose(d_wukv.reshape(128, 2, HEADS, 128), (0, 2, 1, 3)).reshape(128, 1024),
        "rw_w2": d_w2p[64:128],
        "rw_a2": d_a2p[128:192],
        "w_out": d_wo,
    }
    small_g = {
        "norm_pre_g": d_gpre, "mla_q_norm_g": d_gq, "mla_kv_norm_g": d_gkv,
        "rw_mu": jnp.concatenate([d_mu[:, 0:1536], d_mu[:, 1600:1728]], axis=1),
        "rw_w0": d_w0, "rw_a0": d_a0, "rw_k_k": d_kk, "rw_k_a": d_ka, "rw_r_k": d_rk, "rw_ln_g": d_lng,
        "rw_ln_b": d_lnb, "norm_post_g": d_gpost,
    }

    def by_shard(name, g):
        if name == "w_out":
            return g.reshape(N_SHARD, -1, LANES)
        rows, cols = g.shape
        return jnp.transpose(g.reshape(rows, N_SHARD, cols // N_SHARD), (1, 0, 2)).reshape(N_SHARD, -1, LANES)

    packed = jnp.concatenate([by_shard(n, full_g[n]) for n in SHARDED], axis=1)
    halves = packed.reshape(N_SHARD, 2, HALF, LANES)
    keep = lax.dynamic_index_in_dim(halves, c_idx, 1, keepdims=False)
    give = lax.dynamic_index_in_dim(halves, 1 - c_idx, 1, keepdims=False)
    got = _rs_pair_exchange(give)
    pair_sum, pair_sum_b = _add_n([keep.reshape(-1, LANES), got.reshape(-1, LANES)], "rs_pair_sum", SUM_ROWS,
                                  also_bf16=True)
    arrived = _rs_chip_exchange(pair_sum_b.reshape(N_SHARD, HALF, LANES))
    own = lax.dynamic_index_in_dim(pair_sum.reshape(N_SHARD, HALF, LANES), shard_idx, 0, keepdims=False)
    (reduced_half,) = _add_n([own, arrived[0], arrived[1], arrived[2]], "rs_chip_sum", SUM_ROWS)
    g_shard = _rs_pair_gather(reduced_half).reshape(PACK_TOTAL, LANES)

    g_small = _small_allreduce(_pack_small(small_g))

    shard_like = {n: wts[n][0] for n in SHARDED}
    d_sh, nm_sh, nv_sh = _adamw(_pack_shard({n: wts[n][0] for n in SHARDED}), g_shard,
                                _pack_shard({n: mom_m[n][0] for n in SHARDED}),
                                _pack_shard({n: mom_v[n][0] for n in SHARDED}), "adamw_sharded", 568)
    d_sm, nm_sm, nv_sm = _adamw(_pack_small(wts), g_small, _pack_small(mom_m), _pack_small(mom_v), "adamw_small",
                                SMALL_ROWS)

    def unpack(sh, sm):
        out = {n: a[None] for n, a in _unpack_shard(sh, shard_like).items()}
        out.update(_unpack_small(sm, wts))
        return out

    grads, deltas, new_m, new_v = unpack(g_shard, g_small), unpack(d_sh, d_sm), unpack(nm_sh, nm_sm), unpack(nv_sh, nv_sm)
    return (loss, grad_x, *[grads[n] for n in WEIGHTS], *[deltas[n] for n in WEIGHTS],
            *[new_m[n] for n in WEIGHTS], *[new_v[n] for n in WEIGHTS])
```

```python
import functools

import numpy as np
import jax
import jax.numpy as jnp
from jax import lax
from jax.experimental import pallas as pl
from jax.experimental.pallas import tpu as pltpu

F32, BF16 = jnp.float32, jnp.bfloat16
MESH = pl.DeviceIdType.MESH

D = 1024
HEADS = 4
RW = 512
NORM_EPS = 1e-6
GN_EPS = 64e-5
ROPE_THETA = 10000.0
SCALE = (128 + 64) ** -0.5
D_IN = 3136
LR, B1, B2, ADAM_EPS, WD, STEP = 0.001, 0.9, 0.999, 1e-08, 0.01, 10

Z0, CQ0, CKV0, RW0, DP = 0, 1024, 1280, 1408, 3200
NRW = DP - RW0

LANES = 128
SUBLANES = 8
VMEM_LIMIT = 56 * 1024 * 1024

TT = 512
TT_VPU = 256
TQ = 512

N_SHARD = 4
PACK_ROWS = (1024 * 784 // 128, 256 * 192 // 128, 128 * 256 // 128, 64, 64, 256 * 1024 // 128)
PACK_TOTAL = sum(PACK_ROWS)
HALF = PACK_TOTAL // 2
SUM_ROWS = HALF // 4
SMALL_ROWS = 64


def _cparams(sem=None):
    return pltpu.CompilerParams(dimension_semantics=sem, vmem_limit_bytes=VMEM_LIMIT)


def _full(shape):
    n = len(shape)
    return pl.BlockSpec(shape, lambda *_: (0,) * n, pipeline_mode=pl.Buffered(1))


def _resident(shape):
    n = len(shape)
    return pl.BlockSpec(shape, lambda *_: (0,) * n)


def _dot(a, b):
    return jnp.dot(a, b, preferred_element_type=F32)


def _dot_nt(a, b):
    return lax.dot_general(a, b, (((1,), (1,)), ((), ())), preferred_element_type=F32)


def _dot_tn(a, b):
    return lax.dot_general(a, b, (((0,), (0,)), ((), ())), preferred_element_type=F32)


def _split3(x):
    hi = x.astype(BF16)
    r1 = x - hi.astype(F32)
    mid = r1.astype(BF16)
    lo = (r1 - mid.astype(F32)).astype(BF16)
    return hi, mid, lo


def _seg(x, bo):
    rows, nblk = x.shape[0], x.shape[1] // LANES
    pieces = [p for i in range(nblk) for p in _split3(x[:, LANES * i:LANES * (i + 1)])]
    res = _dot(jnp.concatenate(pieces, axis=0), bo)
    parts = [res[(3 * i) * rows:(3 * i + 1) * rows] + res[(3 * i + 1) * rows:(3 * i + 2) * rows]
             + res[(3 * i + 2) * rows:(3 * i + 3) * rows] for i in range(nblk)]
    return parts[0] if nblk == 1 else jnp.concatenate(parts, axis=1)


def _rms(x, g, n):
    rstd = lax.rsqrt(jnp.sum(x * x, axis=-1, keepdims=True) * (1.0 / n) + NORM_EPS)
    nx = x * rstd
    return nx * g, nx, rstd


def _rms_bwd(dy, nx, rstd, g, n):
    dn = dy * g
    dx = rstd * (dn - nx * (jnp.sum(dn * nx, axis=-1, keepdims=True) * (1.0 / n)))
    return dx, jnp.sum(dy * nx, axis=0, keepdims=True)


def _rot(x):
    lane = lax.broadcasted_iota(jnp.int32, x.shape, 1)
    return jnp.where((lane % 64) < 32, -pltpu.roll(x, x.shape[1] - 32, 1), pltpu.roll(x, 32, 1))


def _sigmoid(x):
    return 1.0 / (1.0 + jnp.exp(-x))


def _softplus(x):
    return jnp.maximum(x, 0.0) + jnp.log(1.0 + jnp.exp(-jnp.abs(x)))


def _rw_gates(ps, w0, w2p, a0, a2p, k_k, k_a, bo):
    r, k, v, misc = ps[:, 0:512], ps[:, 512:1024], ps[:, 1024:1536], ps[:, 1536:NRW]
    th = jnp.tanh(misc)
    wpre = w0 + _dot(th.astype(BF16), w2p)
    e = jnp.exp(-_softplus(-wpre) - 0.5)
    w = jnp.exp(-e)
    a = _sigmoid(a0 + _dot(misc.astype(BF16), a2p))
    m = k * k_k
    nrm = jnp.maximum(jnp.sqrt(_seg(m * m, bo)), 1e-12)
    kk = m / nrm
    kp = k * (1.0 + (a - 1.0) * k_a)
    return dict(r=r, k=k, v=v, misc=misc, th=th, wpre=wpre, e=e, w=w, a=a, nrm=nrm, kk=kk, kp=kp)


def _shift_mix(prw, prev_row, mu):
    row = lax.broadcasted_iota(jnp.int32, prw.shape, 0)
    sh = jnp.where(row == 0, prev_row, pltpu.roll(prw, 1, 0))
    return prw + (sh - prw) * mu, sh


def _ag_weights(shards):
    n = len(shards)

    def body(*refs):
        ins, outs = refs[:n], refs[n:2 * n]
        ici_send, ici_recv, d2d_send, d2d_recv = refs[2 * n:2 * n + 4]
        x, y, c = lax.axis_index("x"), lax.axis_index("y"), lax.axis_index("c")
        mine = 2 * x + y
        for w in range(n):
            outs[w][mine] = ins[w][...].astype(BF16)
        flips = ((1, 0), (0, 1), (1, 1))

        def half(w, shard, cc):
            rows = outs[w].shape[1] // 2
            return outs[w].at[shard, pl.ds(pl.multiple_of(cc * rows, 16), rows)]

        def ici(w, k, shard):
            fx, fy = flips[k]
            return pltpu.make_async_remote_copy(
                src_ref=half(w, shard, c), dst_ref=half(w, shard, c),
                send_sem=ici_send.at[w * 3 + k], recv_sem=ici_recv.at[w * 3 + k],
                device_id=(x ^ fx, y ^ fy, c), device_id_type=MESH)

        def d2d(w, k, cc):
            fx, fy = flips[k]
            theirs = 2 * (x ^ fx) + (y ^ fy)
            return pltpu.make_async_remote_copy(
                src_ref=half(w, theirs, cc), dst_ref=half(w, theirs, cc),
                send_sem=d2d_send.at[w * 3 + k], recv_sem=d2d_recv.at[w * 3 + k],
                device_id=(x, y, 1 - c), device_id_type=MESH)

        for w in range(n):
            for k in range(3):
                ici(w, k, mine).start()
        for w in range(n):
            for k in range(3):
                fx, fy = flips[k]
                ici(w, k, 2 * (x ^ fx) + (y ^ fy)).wait_recv()
                d2d(w, k, c).start()
        for w in range(n):
            for k in range(3):
                d2d(w, k, 1 - c).wait_recv()
        for w in range(n):
            for k in range(3):
                ici(w, k, mine).wait_send()
                d2d(w, k, c).wait_send()

    vm = pl.BlockSpec(memory_space=pltpu.VMEM)
    return pl.pallas_call(
        body, name="ag_weights",
        out_shape=[jax.ShapeDtypeStruct((N_SHARD,) + s.shape, BF16) for s in shards],
        in_specs=[vm] * n, out_specs=[vm] * n,
        scratch_shapes=[pltpu.SemaphoreType.DMA((3 * n,))] * 4,
        compiler_params=pltpu.CompilerParams(vmem_limit_bytes=VMEM_LIMIT),
    )(*shards)


def _rs_pair_exchange(send_half):
    def body(src_ref, dst_ref, send_sem, recv_sem):
        x, y, c = lax.axis_index("x"), lax.axis_index("y"), lax.axis_index("c")
        cp = pltpu.make_async_remote_copy(src_ref=src_ref, dst_ref=dst_ref, send_sem=send_sem, recv_sem=recv_sem,
                                          device_id=(x, y, 1 - c), device_id_type=MESH)
        cp.start()
        cp.wait()

    hbm = pl.BlockSpec(memory_space=pl.ANY)
    return pl.pallas_call(
        body, name="rs_pair_exchange",
        out_shape=jax.ShapeDtypeStruct(send_half.shape, send_half.dtype),
        in_specs=[hbm], out_specs=hbm,
        scratch_shapes=[pltpu.SemaphoreType.DMA, pltpu.SemaphoreType.DMA],
    )(send_half)


def _rs_chip_exchange(part):
    def body(src_ref, dst_ref, send_sems, recv_sems):
        x, y, c = lax.axis_index("x"), lax.axis_index("y"), lax.axis_index("c")
        flips = ((1, 0), (0, 1), (1, 1))
        cps = []
        for k, (fx, fy) in enumerate(flips):
            theirs = 2 * (x ^ fx) + (y ^ fy)
            cps.append(pltpu.make_async_remote_copy(
                src_ref=src_ref.at[theirs], dst_ref=dst_ref.at[k],
                send_sem=send_sems.at[k], recv_sem=recv_sems.at[k],
                device_id=(x ^ fx, y ^ fy, c), device_id_type=MESH))
        for cp in cps:
            cp.start()
        for cp in cps:
            cp.wait()

    hbm = pl.BlockSpec(memory_space=pl.ANY)
    return pl.pallas_call(
        body, name="rs_chip_exchange",
        out_shape=jax.ShapeDtypeStruct((3,) + part.shape[1:], part.dtype),
        in_specs=[hbm], out_specs=hbm,
        scratch_shapes=[pltpu.SemaphoreType.DMA((3,)), pltpu.SemaphoreType.DMA((3,))],
    )(part)


def _rs_pair_gather(half):
    def body(src_ref, dst_ref, send_sem, recv_sem, local_sem):
        x, y, c = lax.axis_index("x"), lax.axis_index("y"), lax.axis_index("c")
        own = pltpu.make_async_copy(src_ref, dst_ref.at[c], local_sem)
        own.start()
        cp = pltpu.make_async_remote_copy(src_ref=src_ref, dst_ref=dst_ref.at[c], send_sem=send_sem, recv_sem=recv_sem,
                                          device_id=(x, y, 1 - c), device_id_type=MESH)
        cp.start()
        arrival = pltpu.make_async_remote_copy(src_ref=src_ref, dst_ref=dst_ref.at[1 - c], send_sem=send_sem,
                                               recv_sem=recv_sem, device_id=(x, y, 1 - c), device_id_type=MESH)
        arrival.wait_recv()
        cp.wait_send()
        own.wait()

    hbm = pl.BlockSpec(memory_space=pl.ANY)
    return pl.pallas_call(
        body, name="rs_pair_gather",
        out_shape=jax.ShapeDtypeStruct((2,) + half.shape, half.dtype),
        in_specs=[hbm], out_specs=hbm,
        scratch_shapes=[pltpu.SemaphoreType.DMA, pltpu.SemaphoreType.DMA, pltpu.SemaphoreType.DMA],
    )(half)


def _small_allreduce(vec):
    def body(in_ref, out_ref, recv, send_sems, recv_sems):
        x, y, c = lax.axis_index("x"), lax.axis_index("y"), lax.axis_index("c")
        me = 4 * x + 2 * y + c
        cps = []
        for k in range(1, 8):
            fx, fy, fc = (k >> 2) & 1, (k >> 1) & 1, k & 1
            cps.append(pltpu.make_async_remote_copy(
                src_ref=in_ref, dst_ref=recv.at[k - 1],
                send_sem=send_sems.at[k - 1], recv_sem=recv_sems.at[k - 1],
                device_id=(x ^ fx, y ^ fy, c ^ fc), device_id_type=MESH))
        for cp in cps:
            cp.start()
        for cp in cps:
            cp.wait()
        acc = jnp.zeros(in_ref.shape, F32)
        for j in range(8):
            slot = jnp.maximum((me ^ j) - 1, 0)
            acc = acc + jnp.where(me == j, in_ref[...], recv[slot])
        out_ref[...] = acc

    vm = pl.BlockSpec(memory_space=pltpu.VMEM)
    return pl.pallas_call(
        body, name="small_allreduce",
        out_shape=jax.ShapeDtypeStruct(vec.shape, F32),
        in_specs=[vm], out_specs=vm,
        scratch_shapes=[pltpu.VMEM((7,) + vec.shape, F32), pltpu.SemaphoreType.DMA((7,)),
                        pltpu.SemaphoreType.DMA((7,))],
    )(vec)


def _add_n(arrs, name, rows, also_bf16=False):
    n = len(arrs)
    r = arrs[0].shape[0]

    def body(*refs):
        acc = refs[0][...].astype(F32)
        for k in range(1, n):
            acc = acc + refs[k][...].astype(F32)
        refs[n][...] = acc
        if also_bf16:
            refs[n + 1][...] = acc.astype(BF16)

    spec = pl.BlockSpec((rows, LANES), lambda i: (i, 0))
    out_shape = [jax.ShapeDtypeStruct(arrs[0].shape, F32)]
    if also_bf16:
        out_shape.append(jax.ShapeDtypeStruct(arrs[0].shape, BF16))
    return pl.pallas_call(
        body, name=name, grid=(r // rows,),
        out_shape=out_shape,
        in_specs=[spec] * n, out_specs=[spec] * len(out_shape),
        compiler_params=_cparams(("parallel",)),
    )(*arrs)


def _adamw(w, g, m, v, name, rows):
    r = w.shape[0]

    def body(w_ref, g_ref, m_ref, v_ref, d_ref, nm_ref, nv_ref):
        gg = g_ref[...]
        nm = B1 * m_ref[...] + (1.0 - B1) * gg
        nv = B2 * v_ref[...] + (1.0 - B2) * (gg * gg)
        m_hat = nm / (1.0 - B1 ** STEP)
        v_hat = nv / (1.0 - B2 ** STEP)
        d_ref[...] = -LR * (m_hat / (jnp.sqrt(v_hat) + ADAM_EPS) + WD * w_ref[...])
        nm_ref[...] = nm
        nv_ref[...] = nv

    spec = pl.BlockSpec((rows, LANES), lambda i: (i, 0))
    sds = jax.ShapeDtypeStruct(w.shape, F32)
    return pl.pallas_call(
        body, name=name, grid=(r // rows,),
        out_shape=[sds, sds, sds],
        in_specs=[spec] * 4, out_specs=[spec] * 3,
        compiler_params=_cparams(("parallel",)),
    )(w, g, m, v)


def _pre_fwd(x, pos, invf, gpre, wp, gq, wuq, gkv, wukv, mu, w0, w2p, a0, a2p, k_k, k_a, bo):
    bsz, t, _ = x.shape
    nt = t // TT

    def body(x_ref, pos_ref, invf_ref, gpre_ref, wp_ref, gq_ref, wuq_ref, gkv_ref, wukv_ref, mu_ref, w0_ref,
             w2p_ref, a0_ref, a2p_ref, kk_ref, ka_ref, bo_ref,
             u_ref, pp_ref, q_ref, k_ref, v_ref, r_o, w_o, kp_o, vv_o, al_o, be_o, carry):
        i = pl.program_id(1)
        u, _, _ = _rms(x_ref[0], gpre_ref[...], D)
        ub = u.astype(BF16)
        u_ref[0] = ub
        p = _dot(ub, wp_ref[...])
        pp_ref[0] = p
        prw = p[:, RW0:DP]

        @pl.when(i == 0)
        def _():
            carry[...] = jnp.zeros(carry.shape, F32)

        ps, _ = _shift_mix(prw, carry[7:8, :], mu_ref[...])
        carry[...] = prw[TT - 8:TT, :]

        g = _rw_gates(ps, w0_ref[...], w2p_ref[...], a0_ref[...], a2p_ref[...], kk_ref[...], ka_ref[...],
                      bo_ref[...])
        r_o[0] = g["r"]
        w_o[0] = g["w"]
        kp_o[0] = g["kp"]
        vv_o[0] = g["v"]
        al_o[0] = -g["kk"]
        be_o[0] = g["kk"] * g["a"]

        cqn, _, _ = _rms(p[:, CQ0:CQ0 + 256], gq_ref[...], 256)
        q = _dot(cqn.astype(BF16), wuq_ref[...])
        ckvn, _, _ = _rms(p[:, CKV0:CKV0 + 128], gkv_ref[...], 128)
        kv = _dot(ckvn.astype(BF16), wukv_ref[...])
        ang = pos_ref[0] * invf_ref[...]
        cs, sn = jnp.cos(ang), jnp.sin(ang)
        lane = lax.broadcasted_iota(jnp.int32, cs.shape, 1)
        kr = ps[:, 1536:1536 + LANES]
        kr = jnp.where(lane < 64, kr * cs + _rot(kr) * sn, 0.0).astype(BF16)
        for h in range(HEADS):
            qr = q[:, 256 * h + 128:256 * h + 256]
            q_ref[0, :, 256 * h:256 * h + 128] = q[:, 256 * h:256 * h + 128].astype(BF16)
            q_ref[0, :, 256 * h + 128:256 * h + 256] = (qr * cs + _rot(qr) * sn).astype(BF16)
            k_ref[0, :, 256 * h:256 * h + 128] = kv[:, 128 * h:128 * h + 128].astype(BF16)
            k_ref[0, :, 256 * h + 128:256 * h + 256] = kr
        v_ref[0] = kv[:, 512:1024].astype(BF16)

    tok = lambda c: pl.BlockSpec((1, TT, c), lambda b, i: (b, i, 0))
    full = lambda a: _full(a.shape)
    ins = (x, pos, invf, gpre, wp, gq, wuq, gkv, wukv, mu, w0, w2p, a0, a2p, k_k, k_a, bo)
    in_specs = [tok(D), tok(1)] + [full(a) for a in ins[2:]]
    sd = lambda c, dt: jax.ShapeDtypeStruct((bsz, t, c), dt)
    out_shape = [sd(D, BF16), sd(DP, F32), sd(1024, BF16), sd(1024, BF16), sd(512, BF16)] + [sd(RW, F32)] * 6
    out_specs = [tok(D), tok(DP), tok(1024), tok(1024), tok(512)] + [tok(RW)] * 6
    return pl.pallas_call(
        body, name="pre_fwd", grid=(bsz, nt), out_shape=out_shape, in_specs=in_specs, out_specs=out_specs,
        scratch_shapes=[pltpu.VMEM((8, NRW), F32)],
        compiler_params=_cparams(("arbitrary", "arbitrary")),
    )(*ins)


def _attn_fwd(q, k, v):
    bsz, t, _ = q.shape
    nq = t // TQ

    def body(q_ref, k_ref, v_ref, o_ref, lse_ref):
        i = pl.program_id(2)
        qt = q_ref[0]
        row = lax.broadcasted_iota(jnp.int32, (TQ, TQ), 0) + i * TQ
        col0 = lax.broadcasted_iota(jnp.int32, (TQ, TQ), 1)

        def step(j, carry):
            m, l, acc = carry
            at = pl.ds(pl.multiple_of(j * TQ, TQ), TQ)
            s = _dot_nt(qt, k_ref[0, at, :]) * SCALE
            s = jnp.where(col0 + j * TQ <= row, s, -1e30)
            mn = jnp.maximum(m, jnp.max(s, axis=1, keepdims=True))
            p = jnp.exp(s - mn)
            al = jnp.exp(m - mn)
            l = al * l + jnp.sum(p, axis=1, keepdims=True)
            acc = al * acc + _dot(p.astype(BF16), v_ref[0, at, :])
            return mn, l, acc

        m, l, acc = lax.fori_loop(
            0, i + 1, step,
            (jnp.full((TQ, 1), -1e30, F32), jnp.zeros((TQ, 1), F32), jnp.zeros((TQ, LANES), F32)))
        o_ref[0] = acc / l
        lse_ref[0, 0] = jnp.broadcast_to(m + jnp.log(l), (TQ, LANES))

    return pl.pallas_call(
        body, name="attn_fwd", grid=(bsz, HEADS, nq),
        out_shape=[jax.ShapeDtypeStruct((bsz, t, 512), F32), jax.ShapeDtypeStruct((bsz, HEADS, t, LANES), F32)],
        in_specs=[pl.BlockSpec((1, TQ, 256), lambda b, h, i: (b, i, h)),
                  pl.BlockSpec((1, t, 256), lambda b, h, i: (b, 0, h)),
                  pl.BlockSpec((1, t, LANES), lambda b, h, i: (b, 0, h))],
        out_specs=[pl.BlockSpec((1, TQ, LANES), lambda b, h, i: (b, i, h)),
                   pl.BlockSpec((1, 1, TQ, LANES), lambda b, h, i: (b, h, i, 0))],
        compiler_params=_cparams(("parallel", "parallel", "arbitrary")),
    )(q, k, v)


def _attn_bwd(q, k, v, o, lse, do):
    bsz, t, _ = q.shape
    nq = t // TQ

    def body(q_ref, k_ref, v_ref, o_ref, lse_ref, do_ref, dq_ref, dk_ref, dv_ref, dl_ref):
        def prep(i, _):
            at = pl.ds(pl.multiple_of(i * TQ, TQ), TQ)
            dl_ref[at, :] = jnp.broadcast_to(jnp.sum(do_ref[0, at, :] * o_ref[0, at, :], axis=1, keepdims=True),
                                             (TQ, LANES))
            return 0

        lax.fori_loop(0, nq, prep, 0)
        dq_ref[0] = jnp.zeros((t, 256), F32)
        row0 = lax.broadcasted_iota(jnp.int32, (TQ, TQ), 0)
        col0 = lax.broadcasted_iota(jnp.int32, (TQ, TQ), 1)

        def kv_tile(j, _):
            atk = pl.ds(pl.multiple_of(j * TQ, TQ), TQ)
            kt = k_ref[0, atk, :]
            vt = v_ref[0, atk, :]

            def q_tile(i, carry):
                dk, dv = carry
                atq = pl.ds(pl.multiple_of(i * TQ, TQ), TQ)
                qt = q_ref[0, atq, :]
                dob = do_ref[0, atq, :].astype(BF16)
                s = _dot_nt(qt, kt) * SCALE
                s = jnp.where(col0 + j * TQ <= row0 + i * TQ, s, -1e30)
                p = jnp.exp(s - lse_ref[0, 0, atq, :][:, 0:1])
                dv = dv + _dot_tn(p.astype(BF16), dob)
                dp = _dot_nt(dob, vt)
                ds = (p * (dp - dl_ref[atq, :][:, 0:1]) * SCALE).astype(BF16)
                dk = dk + _dot_tn(ds, qt)
                dq_ref[0, atq, :] += _dot(ds, kt)
                return dk, dv

            dk, dv = lax.fori_loop(j, nq, q_tile, (jnp.zeros((TQ, 256), F32), jnp.zeros((TQ, LANES), F32)))
            dk_ref[0, atk, :] = dk
            dv_ref[0, atk, :] = dv
            return 0

        lax.fori_loop(0, nq, kv_tile, 0)

    s256 = pl.BlockSpec((1, t, 256), lambda b, h: (b, 0, h))
    s128 = pl.BlockSpec((1, t, LANES), lambda b, h: (b, 0, h))
    return pl.pallas_call(
        body, name="attn_bwd", grid=(bsz, HEADS),
        out_shape=[jax.ShapeDtypeStruct((bsz, t, 1024), F32), jax.ShapeDtypeStruct((bsz, t, 1024), F32),
                   jax.ShapeDtypeStruct((bsz, t, 512), F32)],
        in_specs=[s256, s256, s128, s128, pl.BlockSpec((1, 1, t, LANES), lambda b, h: (b, h, 0, 0)), s128],
        out_specs=[s256, s256, s128],
        scratch_shapes=[pltpu.VMEM((t, LANES), F32)],
        compiler_params=_cparams(("parallel", "parallel")),
    )(q, k, v, o, lse, do)


RW_HEADS = 8
CH = 16


def _lane_split(bsz):
    vs = LANES // (bsz * RW_HEADS)
    return vs, 64 // vs


def _from_k(y, bsz):
    vs, _ = _lane_split(bsz)
    tg = y.shape[0]
    return jnp.transpose(y.reshape(tg, 64, vs, bsz, RW_HEADS), (3, 0, 2, 4, 1)).reshape(bsz, tg * vs, RW)


def _to_v(x):
    bsz, t, _ = x.shape
    vs, vq = _lane_split(bsz)
    return jnp.transpose(x.reshape(bsz, t, RW_HEADS, vq, vs), (1, 3, 4, 0, 2)).reshape(t, vq, LANES)


def _from_v(y, bsz):
    t = y.shape[0]
    vs, vq = _lane_split(bsz)
    return jnp.transpose(y.reshape(t, vq, vs, bsz, RW_HEADS), (3, 0, 4, 1, 2)).reshape(bsz, t, RW)


def _ksum(a):
    return jnp.sum(a, axis=0, keepdims=True)


def _fold(a, group):
    sh = LANES // 2
    while sh >= group:
        a = a + pltpu.roll(a, sh, 1)
        sh //= 2
    return a


def _lane_group(shape, group):
    return lax.broadcasted_iota(jnp.int32, shape, 1) // group


SPREAD_STEPS = 8
SPREAD_BLOCK = 32


def _spread_matrix(bsz):
    group = bsz * RW_HEADS
    vs = LANES // group
    rows = (RW_HEADS // 2) * bsz * SPREAD_STEPS
    q = np.zeros((2, rows, SPREAD_STEPS * LANES), np.float32)
    for hpar in range(2):
        for hp in range(RW_HEADS // 2):
            for b in range(bsz):
                for st in range(SPREAD_STEPS):
                    row = (hp * bsz + b) * SPREAD_STEPS + st
                    for s in range(vs):
                        q[hpar, row, st * LANES + s * group + b * RW_HEADS + 2 * hp + hpar] = 1.0
    return jnp.asarray(np.concatenate([q[0], q[1]] * 3, axis=0), BF16)


def _spread_k(xs):
    bsz, t, _ = xs[0].shape
    assert (RW_HEADS // 2) * bsz * SPREAD_STEPS == LANES, "the transposed tile must be 128 lanes wide"
    n = len(xs)
    ngrp = SPREAD_BLOCK // SPREAD_STEPS

    def body(*refs):
        qm = refs[n][...]
        for x_ref, o_ref in zip(refs[:n], refs[n + 1:]):
            cols = [[] for _ in range(6)]
            for m in range(ngrp):
                at = slice(SPREAD_STEPS * m, SPREAD_STEPS * (m + 1))
                x8 = jnp.concatenate([x_ref[b, at, LANES * hp:LANES * (hp + 1)]
                                      for hp in range(RW_HEADS // 2) for b in range(bsz)], axis=0)
                for pi, piece in enumerate(_split3(x8.T)):
                    cols[2 * pi].append(piece[0:64])
                    cols[2 * pi + 1].append(piece[64:128])
            lhs = jnp.concatenate([jnp.concatenate(c, axis=0) for c in cols], axis=1)
            y = _dot(lhs, qm)
            for m in range(ngrp):
                for st in range(SPREAD_STEPS):
                    o_ref[SPREAD_STEPS * m + st] = y[64 * m:64 * (m + 1), LANES * st:LANES * (st + 1)]

    qm = _spread_matrix(bsz)
    return pl.pallas_call(
        body, name="wkv_spread", grid=(t // SPREAD_BLOCK,),
        out_shape=[jax.ShapeDtypeStruct((t, 64, LANES), F32)] * n,
        in_specs=[pl.BlockSpec((bsz, SPREAD_BLOCK, RW), lambda i: (0, i, 0))] * n + [_full(qm.shape)],
        out_specs=[pl.BlockSpec((SPREAD_BLOCK, 64, LANES), lambda i: (i, 0, 0))] * n,
        compiler_params=_cparams(("parallel",)),
    )(*xs, qm)


def _wkv_fwd(r, w, kp, al, be, v):
    t, vq = v.shape[0], v.shape[1]

    def body(r_ref, w_ref, kp_ref, al_ref, be_ref, v_ref, y_ref, a_ref, u_ref, st_ref):
        @pl.when(pl.program_id(0) == 0)
        def _():
            st_ref[...] = jnp.zeros(st_ref.shape, F32)

        def step(tl, _):
            rv, wv, kv, av, bv = r_ref[tl], w_ref[tl], kp_ref[tl], al_ref[tl], be_ref[tl]
            vals = v_ref[tl]
            yrows, urows = [], []
            for q in range(vq):
                s = st_ref[q]
                u = _ksum(s * av)
                s = s * wv + bv * u + kv * vals[q:q + 1]
                st_ref[q] = s
                a_ref[tl, q] = s
                urows.append(u)
                yrows.append(_ksum(s * rv))
            y_ref[tl] = jnp.concatenate(yrows, axis=0)
            u_ref[tl] = jnp.concatenate(urows, axis=0)
            return 0

        lax.fori_loop(0, CH, step, 0)

    kspec = pl.BlockSpec((CH, 64, LANES), lambda i: (i, 0, 0))
    vspec = pl.BlockSpec((CH, vq, LANES), lambda i: (i, 0, 0))
    vsd = jax.ShapeDtypeStruct((t, vq, LANES), F32)
    return pl.pallas_call(
        body, name="wkv_fwd", grid=(t // CH,),
        out_shape=[vsd, jax.ShapeDtypeStruct((t, vq, 64, LANES), F32), vsd],
        in_specs=[kspec] * 5 + [vspec],
        out_specs=[vspec, pl.BlockSpec((CH, vq, 64, LANES), lambda i: (i, 0, 0, 0)), vspec],
        scratch_shapes=[pltpu.VMEM((vq, 64, LANES), F32)],
        compiler_params=_cparams(("arbitrary",)),
    )(r, w, kp, al, be, v)


def _wkv_bwd(r, w, kp, al, be, v, dy, states, u):
    t, vq = v.shape[0], v.shape[1]
    vs = 64 // vq
    group = LANES // vs
    n = t // CH
    ng = CH // vs

    def body(r_ref, w_ref, kp_ref, al_ref, be_ref, v_ref, dy_ref, u_ref, a_ref, ap_ref,
             dr_ref, dw_ref, dkp_ref, dal_ref, dbe_ref, dv_ref, ds_ref):
        @pl.when(pl.program_id(0) == 0)
        def _():
            ds_ref[...] = jnp.zeros(ds_ref.shape, F32)

        earliest = pl.program_id(0) == n - 1

        def reverse(i, _):
            g = ng - 1 - i
            grp = _lane_group((64, LANES), group)
            outs = None
            for j in reversed(range(vs)):
                tl = g * vs + j
                rv, wv, kv, av, bv = r_ref[tl], w_ref[tl], kp_ref[tl], al_ref[tl], be_ref[tl]
                vals, dys, us = v_ref[tl], dy_ref[tl], u_ref[tl]
                acc = None
                dvrows = []
                for q in range(vq):
                    if j > 0:
                        s_prev = a_ref[tl - 1, q]
                    else:
                        before = jnp.where(earliest, 0.0, ap_ref[0, q])
                        s_prev = jnp.where(g == 0, before, a_ref[jnp.maximum(tl - 1, 0), q])
                    dyq = dys[q:q + 1]
                    ds = ds_ref[q] + rv * dyq
                    c = _ksum(ds * bv)
                    dvrows.append(_ksum(ds * kv))
                    terms = (a_ref[tl, q] * dyq, ds * s_prev, ds * vals[q:q + 1], s_prev * c, ds * us[q:q + 1])
                    acc = terms if acc is None else tuple(a + b for a, b in zip(acc, terms))
                    ds_ref[q] = ds * wv + av * c
                dv_ref[tl] = jnp.concatenate(dvrows, axis=0)
                summed = [_fold(a, group) for a in acc]
                outs = summed if outs is None else [jnp.where(grp == j, f, o) for f, o in zip(summed, outs)]
            for ref, o in zip((dr_ref, dw_ref, dkp_ref, dal_ref, dbe_ref), outs):
                ref[g] = o
            return 0

        lax.fori_loop(0, ng, reverse, 0)

    kspec = pl.BlockSpec((CH, 64, LANES), lambda i: (n - 1 - i, 0, 0))
    gspec = pl.BlockSpec((ng, 64, LANES), lambda i: (n - 1 - i, 0, 0))
    vspec = pl.BlockSpec((CH, vq, LANES), lambda i: (n - 1 - i, 0, 0))
    ksd = jax.ShapeDtypeStruct((t // vs, 64, LANES), F32)
    return pl.pallas_call(
        body, name="wkv_bwd", grid=(n,),
        out_shape=[ksd] * 5 + [jax.ShapeDtypeStruct((t, vq, LANES), F32)],
        in_specs=[kspec] * 5 + [vspec, vspec, vspec,
                                pl.BlockSpec((CH, vq, 64, LANES), lambda i: (n - 1 - i, 0, 0, 0)),
                                pl.BlockSpec((1, vq, 64, LANES), lambda i: (jnp.maximum((n - 1 - i) * CH - 1, 0), 0, 0, 0))],
        out_specs=[gspec] * 5 + [vspec],
        scratch_shapes=[pltpu.VMEM((vq, 64, LANES), F32)],
        compiler_params=_cparams(("arbitrary",)),
    )(r, w, kp, al, be, v, dy, u, states, states)


def _post(x, tgt, pp, o, yw, r, kp, v, ln_g, ln_b, r_k, wo, wot, gpost, bo):
    bsz, t, _ = x.shape
    tt = TT_VPU
    nt = t // tt

    def body(x_ref, tgt_ref, z_ref, o_ref, yw_ref, r_ref, kp_ref, v_ref, lng_ref, lnb_ref, rk_ref, wo_ref, wot_ref,
             gpost_ref, bo_ref,
             dh_ref, dz_ref, dym_ref, dyw_ref, dbon_ref, loss_ref, dwo_ref, dgpost_ref, dlng_ref, dlnb_ref, drk_ref):
        first = (pl.program_id(0) == 0) & (pl.program_id(1) == 0)

        @pl.when(first)
        def _():
            for ref in (loss_ref, dwo_ref, dgpost_ref, dlng_ref, dlnb_ref, drk_ref):
                ref[...] = jnp.zeros(ref.shape, F32)

        bo_m = bo_ref[...]
        seg = lambda a: _seg(a, bo_m)
        rowsum = lambda a: jnp.sum(a, axis=0, keepdims=True)
        ywv, rv, kpv, vv = yw_ref[0], r_ref[0], kp_ref[0], v_ref[0]
        ln_g, r_k = lng_ref[...], rk_ref[...]
        mean = seg(ywv) * (1.0 / 64)
        yc = ywv - mean
        rstd = lax.rsqrt(seg(yc * yc) * (1.0 / 64) + GN_EPS)
        yhat = yc * rstd
        sb = seg(rv * kpv * r_k)
        y_rw = yhat * ln_g + lnb_ref[...] + sb * vv
        z = z_ref[0]
        sig = _sigmoid(z)
        sz = z * sig
        ycat = jnp.concatenate([o_ref[0], y_rw], axis=1)
        ycg = (ycat * sz).astype(BF16)
        out = _dot(ycg, wo_ref[...])
        hn, nx, rstd_o = _rms(out, gpost_ref[...], D)
        err = x_ref[0] + hn - tgt_ref[0]
        loss_ref[...] += jnp.sum(err * err) * (0.5 / D)
        dh = err * (1.0 / D)
        dh_ref[0] = dh
        dout, dgp = _rms_bwd(dh, nx, rstd_o, gpost_ref[...], D)
        dgpost_ref[...] += dgp
        doutb = dout.astype(BF16)
        dwo_ref[...] += _dot_tn(ycg, doutb)
        dycg = _dot(doutb, wot_ref[...])
        dz_ref[0] = dycg * ycat * (sig * (1.0 + z * (1.0 - sig)))
        dycat = dycg * sz
        dym_ref[0] = dycat[:, 0:512]
        dy_rw = dycat[:, 512:1024]
        dlnb_ref[...] += rowsum(dy_rw)
        dlng_ref[...] += rowsum(dy_rw * yhat)
        dyhat = dy_rw * ln_g
        dyw_ref[0] = rstd * (dyhat - seg(dyhat) * (1.0 / 64) - yhat * (seg(dyhat * yhat) * (1.0 / 64)))
        dsb = seg(dy_rw * vv)
        drk_ref[...] += rowsum(dsb * rv * kpv)
        dbon_ref[0, :, 0:512] = dsb * kpv * r_k
        dbon_ref[0, :, 512:1024] = dsb * rv * r_k
        dbon_ref[0, :, 1024:1536] = dy_rw * sb

    tok = lambda c: pl.BlockSpec((1, tt, c), lambda b, i: (b, i, 0))
    full = lambda a: _full(a.shape)
    ins = (x, tgt, pp, o, yw, r, kp, v, ln_g, ln_b, r_k, wo, wot, gpost, bo)
    in_specs = [tok(D), tok(D), tok(1024)] + [tok(512)] * 5 + [full(a) for a in ins[8:]]
    sd = lambda c: jax.ShapeDtypeStruct((bsz, t, c), F32)
    vec = lambda c: jax.ShapeDtypeStruct((1, c), F32)
    out_shape = [sd(D), sd(1024), sd(512), sd(512), sd(1536), jax.ShapeDtypeStruct((8, LANES), F32),
                 jax.ShapeDtypeStruct((1024, 1024), F32), vec(D), vec(512), vec(512), vec(512)]
    out_specs = [tok(D), tok(1024), tok(512), tok(512), tok(1536), _resident((8, LANES)), _resident((1024, 1024)),
                 _resident((1, D)), _resident((1, 512)), _resident((1, 512)), _resident((1, 512))]
    return pl.pallas_call(
        body, name="post", grid=(bsz, nt), out_shape=out_shape, in_specs=in_specs, out_specs=out_specs,
        compiler_params=_cparams(("arbitrary", "arbitrary")),
    )(*ins)


def _pre_bwd_a(pp, pos, invf, cqkv_w, mu, w0, w2p, w2pt, a0, a2p, a2pt, k_k, k_a, bo,
               dq, dk, dva, dwkv, dbon):
    gq, wuqt, gkv, wukvt = cqkv_w
    bsz, t, _ = pp.shape
    tt = TT_VPU
    nt = t // tt
    dr_w, dw_w, dkp_w, dv_w, dal_w, dbe_w = dwkv

    def body(pp_ref, pos_ref, invf_ref, gq_ref, wuqt_ref, gkv_ref, wukvt_ref, mu_ref, w0_ref, w2p_ref, w2pt_ref,
             a0_ref, a2p_ref, a2pt_ref, kk_ref, ka_ref, bo_ref, dq_ref, dk_ref, dva_ref,
             dr_ref, dw_ref, dkp_ref, dv_ref, dal_ref, dbe_ref, dbon_ref,
             da_ref, dwuq_ref, dwukv_ref, dw2p_ref, da2p_ref, dgq_ref, dgkv_ref, dmu_ref, dw0_ref, da0_ref,
             dkk_ref, dka_ref, carry):
        i = pl.program_id(1)
        first = (pl.program_id(0) == 0) & (i == 0)

        @pl.when(first)
        def _():
            for ref in (dwuq_ref, dwukv_ref, dw2p_ref, da2p_ref, dgq_ref, dgkv_ref, dmu_ref, dw0_ref, da0_ref,
                        dkk_ref, dka_ref):
                ref[...] = jnp.zeros(ref.shape, F32)

        bo_m = bo_ref[...]
        rowsum = lambda a: jnp.sum(a, axis=0, keepdims=True)
        prw = pp_ref[0, :, RW0:DP]

        @pl.when(i == 0)
        def _():
            carry[...] = jnp.zeros(carry.shape, F32)

        ps, sh = _shift_mix(prw, carry[7:8, :], mu_ref[...])
        carry[...] = prw[tt - 8:tt, :]
        k_k, k_a = kk_ref[...], ka_ref[...]
        g = _rw_gates(ps, w0_ref[...], w2p_ref[...], a0_ref[...], a2p_ref[...], k_k, k_a, bo_m)
        a, kk, k = g["a"], g["kk"], g["k"]
        dr = dr_ref[0] + dbon_ref[0, :, 0:512]
        dkp = dkp_ref[0] + dbon_ref[0, :, 512:1024]
        dv = dv_ref[0] + dbon_ref[0, :, 1024:1536]
        dbe = dbe_ref[0]
        dkk = dbe * a - dal_ref[0]
        da = dbe * kk + dkp * k * k_a
        dka_ref[...] += rowsum(dkp * k * (a - 1.0))
        dm = (dkk - kk * _seg(dkk * kk, bo_m)) / g["nrm"]
        dkk_ref[...] += rowsum(dm * k)
        dk_tot = dkp * (1.0 + (a - 1.0) * k_a) + dm * k_k
        dapre = da * a * (1.0 - a)
        da0_ref[...] += rowsum(dapre)
        dapb = dapre.astype(BF16)
        da2p_ref[...] += _dot_tn(g["misc"].astype(BF16), dapb)
        dwpre = dw_ref[0] * g["w"] * (-g["e"]) * _sigmoid(-g["wpre"])
        dw0_ref[...] += rowsum(dwpre)
        dwpb = dwpre.astype(BF16)
        th = g["th"]
        dw2p_ref[...] += _dot_tn(th.astype(BF16), dwpb)
        dmisc = _dot(dapb, a2pt_ref[...]) + _dot(dwpb, w2pt_ref[...]) * (1.0 - th * th)
        ang = pos_ref[0] * invf_ref[...]
        cs, sn = jnp.cos(ang), jnp.sin(ang)
        unrope = lambda gr: gr * cs - _rot(gr * sn)
        lane = lax.broadcasted_iota(jnp.int32, cs.shape, 1)
        dkr = dk_ref[0, :, 128:256]
        for h in range(1, HEADS):
            dkr = dkr + dk_ref[0, :, 256 * h + 128:256 * h + 256]
        dkr = jnp.where(lane < 64, unrope(dkr), 0.0)
        dmisc = dmisc + jnp.concatenate([dkr, jnp.zeros_like(dkr)], axis=1)
        dqp = jnp.concatenate(
            [blk for h in range(HEADS)
             for blk in (dq_ref[0, :, 256 * h:256 * h + 128], unrope(dq_ref[0, :, 256 * h + 128:256 * h + 256]))],
            axis=1).astype(BF16)
        dkvp = jnp.concatenate([dk_ref[0, :, 256 * h:256 * h + 128] for h in range(HEADS)] + [dva_ref[0]],
                               axis=1).astype(BF16)
        cqn, cq_nx, cq_rstd = _rms(pp_ref[0, :, CQ0:CQ0 + 256], gq_ref[...], 256)
        ckvn, ckv_nx, ckv_rstd = _rms(pp_ref[0, :, CKV0:CKV0 + 128], gkv_ref[...], 128)
        dwuq_ref[...] += _dot_tn(cqn.astype(BF16), dqp)
        dwukv_ref[...] += _dot_tn(ckvn.astype(BF16), dkvp)
        dcq, dgq = _rms_bwd(_dot(dqp, wuqt_ref[...]), cq_nx, cq_rstd, gq_ref[...], 256)
        dckv, dgkv = _rms_bwd(_dot(dkvp, wukvt_ref[...]), ckv_nx, ckv_rstd, gkv_ref[...], 128)
        dgq_ref[...] += dgq
        dgkv_ref[...] += dgkv
        dps = jnp.concatenate([dr, dk_tot, dv, dmisc], axis=1)
        dmu_ref[...] += rowsum(dps * (sh - prw))
        da_ref[0, :, 0:256] = dcq
        da_ref[0, :, 256:384] = dckv
        da_ref[0, :, 384:384 + NRW] = dps

    tok = lambda c: pl.BlockSpec((1, tt, c), lambda b, i: (b, i, 0))
    full = lambda a: _full(a.shape)
    ins = (pp, pos, invf, gq, wuqt, gkv, wukvt, mu, w0, w2p, w2pt, a0, a2p, a2pt, k_k, k_a, bo,
           dq, dk, dva, dr_w, dw_w, dkp_w, dv_w, dal_w, dbe_w, dbon)
    in_specs = ([tok(DP), tok(1)] + [full(a) for a in ins[2:17]] + [tok(1024), tok(1024), tok(512)]
                + [tok(512)] * 6 + [tok(1536)])
    shp = lambda *s: jax.ShapeDtypeStruct(s, F32)
    out_shape = [shp(bsz, t, 384 + NRW), shp(256, 1024), shp(128, 1024), shp(256, 512), shp(256, 512),
                 shp(1, 256), shp(1, 128), shp(1, NRW), shp(1, 512), shp(1, 512), shp(1, 512), shp(1, 512)]
    out_specs = [tok(384 + NRW)] + [_resident(s.shape) for s in out_shape[1:]]
    return pl.pallas_call(
        body, name="pre_bwd_a", grid=(bsz, nt), out_shape=out_shape, in_specs=in_specs, out_specs=out_specs,
        scratch_shapes=[pltpu.VMEM((8, NRW), F32)],
        compiler_params=_cparams(("arbitrary", "arbitrary")),
    )(*ins)


def _pre_bwd_b(x, dh, dz, da, mu, wpt, gpre):
    bsz, t, _ = x.shape
    nt = t // TT
    nblk = t // 8

    def body(x_ref, dh_ref, dz_ref, da_ref, nxt_ref, mu_ref, wpt_ref, gpre_ref, gx_ref, dp_ref, dgpre_ref):
        i = pl.program_id(1)
        first = (pl.program_id(0) == 0) & (i == 0)

        @pl.when(first)
        def _():
            dgpre_ref[...] = jnp.zeros(dgpre_ref.shape, F32)

        mu_v = mu_ref[...]
        dps = da_ref[0, :, 384:384 + NRW]
        nxt = jnp.where(i < nt - 1, nxt_ref[0, 0:1, 384:384 + NRW], 0.0)
        row = lax.broadcasted_iota(jnp.int32, dps.shape, 0)
        up = jnp.where(row == TT - 1, nxt, pltpu.roll(dps, TT - 1, 0))
        dprw = dps * (1.0 - mu_v) + up * mu_v
        dp = jnp.concatenate([dz_ref[0], da_ref[0, :, 0:384], dprw], axis=1).astype(BF16)
        dp_ref[0] = dp
        du = _dot(dp, wpt_ref[...])
        _, nx, rstd = _rms(x_ref[0], gpre_ref[...], D)
        dx, dg = _rms_bwd(du, nx, rstd, gpre_ref[...], D)
        dgpre_ref[...] += dg
        gx_ref[0] = dh_ref[0] + dx

    tok = lambda c: pl.BlockSpec((1, TT, c), lambda b, i: (b, i, 0))
    nxt_spec = pl.BlockSpec((1, 8, 384 + NRW), lambda b, i: (b, jnp.minimum((i + 1) * (TT // 8), nblk - 1), 0))
    ins = (x, dh, dz, da, da, mu, wpt, gpre)
    return pl.pallas_call(
        body, name="pre_bwd_b", grid=(bsz, nt),
        out_shape=[jax.ShapeDtypeStruct((bsz, t, D), F32), jax.ShapeDtypeStruct((bsz, t, DP), BF16),
                   jax.ShapeDtypeStruct((1, D), F32)],
        in_specs=[tok(D), tok(D), tok(1024), tok(384 + NRW), nxt_spec, _full(mu.shape), _full(wpt.shape),
                  _full(gpre.shape)],
        out_specs=[tok(D), tok(DP), _resident((1, D))],
        compiler_params=_cparams(("arbitrary", "arbitrary")),
    )(*ins)


def _tn_matmul(a, b, bn, name, bk=512):
    kdim, m = a.shape
    _, n = b.shape
    nk = kdim // bk

    def body(a_ref, b_ref, o_ref):
        @pl.when(pl.program_id(1) == 0)
        def _():
            o_ref[...] = jnp.zeros(o_ref.shape, F32)

        o_ref[...] += _dot_tn(a_ref[...], b_ref[...])

    return pl.pallas_call(
        body, name=name, grid=(n // bn, nk),
        out_shape=jax.ShapeDtypeStruct((m, n), F32),
        in_specs=[pl.BlockSpec((bk, m), lambda j, kk: (kk, 0)), pl.BlockSpec((bk, bn), lambda j, kk: (kk, j))],
        out_specs=pl.BlockSpec((m, bn), lambda j, kk: (0, j)),
        compiler_params=_cparams(("parallel", "arbitrary")),
    )(a, b)


SHARDED = ("w_in", "mla_w_uq", "mla_w_ukv", "rw_w2", "rw_a2", "w_out")
SMALL = ("norm_pre_g", "mla_q_norm_g", "mla_kv_norm_g", "rw_mu", "rw_w0", "rw_a0", "rw_k_k", "rw_k_a", "rw_r_k",
         "rw_ln_g", "rw_ln_b", "norm_post_g")
WEIGHTS = ("norm_pre_g", "w_in", "mla_q_norm_g", "mla_w_uq", "mla_kv_norm_g", "mla_w_ukv", "rw_mu", "rw_w0", "rw_w2",
           "rw_a0", "rw_a2", "rw_k_k", "rw_k_a", "rw_r_k", "rw_ln_g", "rw_ln_b", "w_out", "norm_post_g")


def _pack_small(d):
    flat = jnp.concatenate([d[n].reshape(1, -1) for n in SMALL], axis=1)
    return jnp.pad(flat, ((0, 0), (0, SMALL_ROWS * LANES - flat.shape[1]))).reshape(SMALL_ROWS, LANES)


def _unpack_small(packed, like):
    flat = packed.reshape(1, -1)
    out, at = {}, 0
    for n in SMALL:
        size = int(np.prod(like[n].shape))
        out[n] = flat[:, at:at + size].reshape(like[n].shape)
        at += size
    return out


def _pack_shard(d):
    return jnp.concatenate([d[n].reshape(-1, LANES) for n in SHARDED], axis=0)


def _unpack_shard(packed, like):
    out, at = {}, 0
    for n, rows in zip(SHARDED, PACK_ROWS):
        out[n] = packed[at:at + rows].reshape(like[n].shape)
        at += rows
    return out


def _constants():
    bo = np.kron(np.eye(2, dtype=np.float32), np.ones((64, 64), np.float32))
    inv = ROPE_THETA ** (-np.arange(0, 64, 2, dtype=np.float32) / 64)
    invf = np.concatenate([inv, inv, np.zeros(64, np.float32)]).astype(np.float32)[None, :]
    return jnp.asarray(bo, BF16), jnp.asarray(invf)


def kernel(x, positions, norm_pre_g, w_in, mla_q_norm_g, mla_w_uq, mla_kv_norm_g, mla_w_ukv, rw_mu, rw_w0, rw_w2, rw_a0, rw_a2, rw_k_k, rw_k_a, rw_r_k, rw_ln_g, rw_ln_b, w_out, norm_post_g, loss_target, m_norm_pre_g, m_w_in, m_mla_q_norm_g, m_mla_w_uq, m_mla_kv_norm_g, m_mla_w_ukv, m_rw_mu, m_rw_w0, m_rw_w2, m_rw_a0, m_rw_a2, m_rw_k_k, m_rw_k_a, m_rw_r_k, m_rw_ln_g, m_rw_ln_b, m_w_out, m_norm_post_g, v_norm_pre_g, v_w_in, v_mla_q_norm_g, v_mla_w_uq, v_mla_kv_norm_g, v_mla_w_ukv, v_rw_mu, v_rw_w0, v_rw_w2, v_rw_a0, v_rw_a2, v_rw_k_k, v_rw_k_a, v_rw_r_k, v_rw_ln_g, v_rw_ln_b, v_w_out, v_norm_post_g):
    wts = dict(norm_pre_g=norm_pre_g, w_in=w_in, mla_q_norm_g=mla_q_norm_g, mla_w_uq=mla_w_uq,
               mla_kv_norm_g=mla_kv_norm_g, mla_w_ukv=mla_w_ukv, rw_mu=rw_mu, rw_w0=rw_w0, rw_w2=rw_w2, rw_a0=rw_a0,
               rw_a2=rw_a2, rw_k_k=rw_k_k, rw_k_a=rw_k_a, rw_r_k=rw_r_k, rw_ln_g=rw_ln_g, rw_ln_b=rw_ln_b, w_out=w_out,
               norm_post_g=norm_post_g)
    mom_m = dict(norm_pre_g=m_norm_pre_g, w_in=m_w_in, mla_q_norm_g=m_mla_q_norm_g, mla_w_uq=m_mla_w_uq,
                 mla_kv_norm_g=m_mla_kv_norm_g, mla_w_ukv=m_mla_w_ukv, rw_mu=m_rw_mu, rw_w0=m_rw_w0, rw_w2=m_rw_w2,
                 rw_a0=m_rw_a0, rw_a2=m_rw_a2, rw_k_k=m_rw_k_k, rw_k_a=m_rw_k_a, rw_r_k=m_rw_r_k, rw_ln_g=m_rw_ln_g,
                 rw_ln_b=m_rw_ln_b, w_out=m_w_out, norm_post_g=m_norm_post_g)
    mom_v = dict(norm_pre_g=v_norm_pre_g, w_in=v_w_in, mla_q_norm_g=v_mla_q_norm_g, mla_w_uq=v_mla_w_uq,
                 mla_kv_norm_g=v_mla_kv_norm_g, mla_w_ukv=v_mla_w_ukv, rw_mu=v_rw_mu, rw_w0=v_rw_w0, rw_w2=v_rw_w2,
                 rw_a0=v_rw_a0, rw_a2=v_rw_a2, rw_k_k=v_rw_k_k, rw_k_a=v_rw_k_a, rw_r_k=v_rw_r_k, rw_ln_g=v_rw_ln_g,
                 rw_ln_b=v_rw_ln_b, w_out=v_w_out, norm_post_g=v_norm_post_g)
    bsz, t, _ = x.shape
    bo, invf = _constants()
    c_idx = lax.axis_index("c")
    shard_idx = 2 * lax.axis_index("x") + lax.axis_index("y")

    g_in, g_uq, g_ukv, g_w2, g_a2, g_out = _ag_weights([wts[n][0] for n in SHARDED])
    w_in_f = jnp.transpose(g_in, (1, 0, 2)).reshape(D, D_IN)
    wp = jnp.concatenate([w_in_f[:, 2112:3136], w_in_f[:, 0:384], w_in_f[:, 448:1984], w_in_f[:, 384:448],
                          w_in_f[:, 1984:2112], jnp.zeros((D, 64), BF16)], axis=1)
    wuq = jnp.pad(jnp.transpose(g_uq, (1, 0, 2)).reshape(256, HEADS, 192), ((0, 0), (0, 0), (0, 64))).reshape(256, 1024)
    wukv = jnp.transpose(jnp.transpose(g_ukv, (1, 0, 2)).reshape(128, HEADS, 2, 128), (0, 2, 1, 3)).reshape(128, 1024)
    w2 = jnp.transpose(g_w2, (1, 0, 2)).reshape(64, RW)
    a2 = jnp.transpose(g_a2, (1, 0, 2)).reshape(64, RW)
    w2p = jnp.pad(w2, ((64, 128), (0, 0)))
    a2p = jnp.pad(a2, ((128, 64), (0, 0)))
    wo = g_out.reshape(D, D)
    mu = jnp.concatenate([rw_mu[:, 0:1536], jnp.zeros((1, 64), F32), rw_mu[:, 1536:1664], jnp.zeros((1, 64), F32)],
                         axis=1)
    r_k = rw_r_k.reshape(1, RW)
    pos = positions.astype(F32)[:, :, None]

    (u, pp, q_att, k_att, v_att, r, w, kp, v, al, be) = _pre_fwd(
        x, pos, invf, norm_pre_g, wp, mla_q_norm_g, wuq, mla_kv_norm_g, wukv, mu, rw_w0, w2p, rw_a0, a2p, rw_k_k,
        rw_k_a, bo)
    o, lse = _attn_fwd(q_att, k_att, v_att)
    rw_k = _spread_k([r, w, kp, al, be])
    v_v = _to_v(v)
    yw_v, states, u_v = _wkv_fwd(*rw_k, v_v)
    yw = _from_v(yw_v, bsz)

    (dh, dz, dym, dyw, dbon, loss_acc, d_wo, d_gpost, d_lng, d_lnb, d_rk) = _post(
        x, loss_target, pp, o, yw, r, kp, v, rw_ln_g, rw_ln_b, r_k, wo, wo.T, norm_post_g, bo)
    loss = lax.psum(loss_acc[0, 0], ("x", "y", "c"))

    d_k = _wkv_bwd(*rw_k, v_v, _to_v(dyw), states, u_v)
    dr_w, dw_w, dkp_w, dal_w, dbe_w = (_from_k(a, bsz) for a in d_k[:5])
    dwkv = (dr_w, dw_w, dkp_w, _from_v(d_k[5], bsz), dal_w, dbe_w)
    dq, dk, dva = _attn_bwd(q_att, k_att, v_att, o, lse, dym)

    (da, d_wuq, d_wukv, d_w2p, d_a2p, d_gq, d_gkv, d_mu, d_w0, d_a0, d_kk, d_ka) = _pre_bwd_a(
        pp, pos, invf, (mla_q_norm_g, wuq.T, mla_kv_norm_g, wukv.T), mu, rw_w0, w2p, w2p.T, rw_a0, a2p, a2p.T,
        rw_k_k, rw_k_a, bo, dq, dk, dva, dwkv, dbon)
    grad_x, dpb, d_gpre = _pre_bwd_b(x, dh, dz, da, mu, wp.T, norm_pre_g)
    d_wp = _tn_matmul(u.reshape(bsz * t, D), dpb.reshape(bsz * t, DP), 640, "dw_in")

    full_g = {
        "w_in": jnp.concatenate([d_wp[:, 1024:1408], d_wp[:, 2944:3008], d_wp[:, 1408:2944], d_wp[:, 3008:3136],
                                 d_wp[:, 0:1024]], axis=1),
        "mla_w_uq": d_wuq.reshape(256, HEADS, 256)[:, :, :192].reshape(256, 768),
        "mla_w_ukv": jnp.transpose(d_wukv.reshape(128, 2, HEADS, 128), (0, 2, 1, 3)).reshape(128, 1024),
        "rw_w2": d_w2p[64:128],
        "rw_a2": d_a2p[128:192],
        "w_out": d_wo,
    }
    small_g = {
        "norm_pre_g": d_gpre, "mla_q_norm_g": d_gq, "mla_kv_norm_g": d_gkv,
        "rw_mu": jnp.concatenate([d_mu[:, 0:1536], d_mu[:, 1600:1728]], axis=1),
        "rw_w0": d_w0, "rw_a0": d_a0, "rw_k_k": d_kk, "rw_k_a": d_ka, "rw_r_k": d_rk, "rw_ln_g": d_lng,
        "rw_ln_b": d_lnb, "norm_post_g": d_gpost,
    }

    def by_shard(name, g):
        if name == "w_out":
            return g.reshape(N_SHARD, -1, LANES)
        rows, cols = g.shape
        return jnp.transpose(g.reshape(rows, N_SHARD, cols // N_SHARD), (1, 0, 2)).reshape(N_SHARD, -1, LANES)

    packed = jnp.concatenate([by_shard(n, full_g[n]) for n in SHARDED], axis=1)
    halves = packed.reshape(N_SHARD, 2, HALF, LANES)
    keep = lax.dynamic_index_in_dim(halves, c_idx, 1, keepdims=False)
    give = lax.dynamic_index_in_dim(halves, 1 - c_idx, 1, keepdims=False)
    got = _rs_pair_exchange(give)
    pair_sum, pair_sum_b = _add_n([keep.reshape(-1, LANES), got.reshape(-1, LANES)], "rs_pair_sum", SUM_ROWS,
                                  also_bf16=True)
    arrived = _rs_chip_exchange(pair_sum_b.reshape(N_SHARD, HALF, LANES))
    own = lax.dynamic_index_in_dim(pair_sum.reshape(N_SHARD, HALF, LANES), shard_idx, 0, keepdims=False)
    (reduced_half,) = _add_n([own, arrived[0], arrived[1], arrived[2]], "rs_chip_sum", SUM_ROWS)
    g_shard = _rs_pair_gather(reduced_half).reshape(PACK_TOTAL, LANES)

    g_small = _small_allreduce(_pack_small(small_g))

    shard_like = {n: wts[n][0] for n in SHARDED}
    d_sh, nm_sh, nv_sh = _adamw(_pack_shard({n: wts[n][0] for n in SHARDED}), g_shard,
                                _pack_shard({n: mom_m[n][0] for n in SHARDED}),
                                _pack_shard({n: mom_v[n][0] for n in SHARDED}), "adamw_sharded", 568)
    d_sm, nm_sm, nv_sm = _adamw(_pack_small(wts), g_small, _pack_small(mom_m), _pack_small(mom_v), "adamw_small",
                                SMALL_ROWS)

    def unpack(sh, sm):
        out = {n: a[None] for n, a in _unpack_shard(sh, shard_like).items()}
        out.update(_unpack_small(sm, wts))
        return out

    grads, deltas, new_m, new_v = unpack(g_shard, g_small), unpack(d_sh, d_sm), unpack(nm_sh, nm_sm), unpack(nv_sh, nv_sm)
    return (loss, grad_x, *[grads[n] for n in WEIGHTS], *[deltas[n] for n in WEIGHTS],
            *[new_m[n] for n in WEIGHTS], *[new_v[n] for n in WEIGHTS])
```

```python
import functools

import numpy as np
import jax
import jax.numpy as jnp
from jax import lax
from jax.experimental import pallas as pl
from jax.experimental.pallas import tpu as pltpu

F32, BF16 = jnp.float32, jnp.bfloat16
MESH = pl.DeviceIdType.MESH

D = 1024
HEADS = 4
RW = 512
NORM_EPS = 1e-6
GN_EPS = 64e-5
ROPE_THETA = 10000.0
SCALE = (128 + 64) ** -0.5
D_IN = 3136
LR, B1, B2, ADAM_EPS, WD, STEP = 0.001, 0.9, 0.999, 1e-08, 0.01, 10

Z0, CQ0, CKV0, RW0, DP = 0, 1024, 1280, 1408, 3200
NRW = DP - RW0

LANES = 128
SUBLANES = 8
VMEM_LIMIT = 56 * 1024 * 1024

TT = 512
TT_VPU = 256
TQ = 512

N_SHARD = 4
PACK_ROWS = (1024 * 784 // 128, 256 * 192 // 128, 128 * 256 // 128, 64, 64, 256 * 1024 // 128)
PACK_TOTAL = sum(PACK_ROWS)
HALF = PACK_TOTAL // 2
SUM_ROWS = HALF // 4
SMALL_ROWS = 64


def _cparams(sem=None):
    return pltpu.CompilerParams(dimension_semantics=sem, vmem_limit_bytes=VMEM_LIMIT)


def _full(shape):
    n = len(shape)
    return pl.BlockSpec(shape, lambda *_: (0,) * n, pipeline_mode=pl.Buffered(1))


def _resident(shape):
    n = len(shape)
    return pl.BlockSpec(shape, lambda *_: (0,) * n)


def _dot(a, b):
    return jnp.dot(a, b, preferred_element_type=F32)


def _dot_nt(a, b):
    return lax.dot_general(a, b, (((1,), (1,)), ((), ())), preferred_element_type=F32)


def _dot_tn(a, b):
    return lax.dot_general(a, b, (((0,), (0,)), ((), ())), preferred_element_type=F32)


def _split3(x):
    hi = x.astype(BF16)
    r1 = x - hi.astype(F32)
    mid = r1.astype(BF16)
    lo = (r1 - mid.astype(F32)).astype(BF16)
    return hi, mid, lo


def _seg(x, bo):
    rows, nblk = x.shape[0], x.shape[1] // LANES
    pieces = [p for i in range(nblk) for p in _split3(x[:, LANES * i:LANES * (i + 1)])]
    res = _dot(jnp.concatenate(pieces, axis=0), bo)
    parts = [res[(3 * i) * rows:(3 * i + 1) * rows] + res[(3 * i + 1) * rows:(3 * i + 2) * rows]
             + res[(3 * i + 2) * rows:(3 * i + 3) * rows] for i in range(nblk)]
    return parts[0] if nblk == 1 else jnp.concatenate(parts, axis=1)


def _rms(x, g, n):
    rstd = lax.rsqrt(jnp.sum(x * x, axis=-1, keepdims=True) * (1.0 / n) + NORM_EPS)
    nx = x * rstd
    return nx * g, nx, rstd


def _rms_bwd(dy, nx, rstd, g, n):
    dn = dy * g
    dx = rstd * (dn - nx * (jnp.sum(dn * nx, axis=-1, keepdims=True) * (1.0 / n)))
    return dx, jnp.sum(dy * nx, axis=0, keepdims=True)


def _rot(x):
    lane = lax.broadcasted_iota(jnp.int32, x.shape, 1)
    return jnp.where((lane % 64) < 32, -pltpu.roll(x, x.shape[1] - 32, 1), pltpu.roll(x, 32, 1))


def _sigmoid(x):
    return 1.0 / (1.0 + jnp.exp(-x))


def _softplus(x):
    return jnp.maximum(x, 0.0) + jnp.log(1.0 + jnp.exp(-jnp.abs(x)))


def _rw_gates(ps, w0, w2p, a0, a2p, k_k, k_a, bo):
    r, k, v, misc = ps[:, 0:512], ps[:, 512:1024], ps[:, 1024:1536], ps[:, 1536:NRW]
    th = jnp.tanh(misc)
    wpre = w0 + _dot(th.astype(BF16), w2p)
    e = jnp.exp(-_softplus(-wpre) - 0.5)
    w = jnp.exp(-e)
    a = _sigmoid(a0 + _dot(misc.astype(BF16), a2p))
    m = k * k_k
    nrm = jnp.maximum(jnp.sqrt(_seg(m * m, bo)), 1e-12)
    kk = m / nrm
    kp = k * (1.0 + (a - 1.0) * k_a)
    return dict(r=r, k=k, v=v, misc=misc, th=th, wpre=wpre, e=e, w=w, a=a, nrm=nrm, kk=kk, kp=kp)


def _shift_mix(prw, prev_row, mu):
    row = lax.broadcasted_iota(jnp.int32, prw.shape, 0)
    sh = jnp.where(row == 0, prev_row, pltpu.roll(prw, 1, 0))
    return prw + (sh - prw) * mu, sh


def _ag_weights(shards):
    n = len(shards)

    def body(*refs):
        ins, outs = refs[:n], refs[n:2 * n]
        ici_send, ici_recv, d2d_send, d2d_recv = refs[2 * n:2 * n + 4]
        x, y, c = lax.axis_index("x"), lax.axis_index("y"), lax.axis_index("c")
        mine = 2 * x + y
        for w in range(n):
            outs[w][mine] = ins[w][...].astype(BF16)
        flips = ((1, 0), (0, 1), (1, 1))

        def half(w, shard, cc):
            rows = outs[w].shape[1] // 2
            return outs[w].at[shard, pl.ds(pl.multiple_of(cc * rows, 16), rows)]

        def ici(w, k, shard):
            fx, fy = flips[k]
            return pltpu.make_async_remote_copy(
                src_ref=half(w, shard, c), dst_ref=half(w, shard, c),
                send_sem=ici_send.at[w * 3 + k], recv_sem=ici_recv.at[w * 3 + k],
                device_id=(x ^ fx, y ^ fy, c), device_id_type=MESH)

        def d2d(w, k, cc):
            fx, fy = flips[k]
            theirs = 2 * (x ^ fx) + (y ^ fy)
            return pltpu.make_async_remote_copy(
                src_ref=half(w, theirs, cc), dst_ref=half(w, theirs, cc),
                send_sem=d2d_send.at[w * 3 + k], recv_sem=d2d_recv.at[w * 3 + k],
                device_id=(x, y, 1 - c), device_id_type=MESH)

        for w in range(n):
            for k in range(3):
                ici(w, k, mine).start()
        for w in range(n):
            for k in range(3):
                fx, fy = flips[k]
                ici(w, k, 2 * (x ^ fx) + (y ^ fy)).wait_recv()
                d2d(w, k, c).start()
        for w in range(n):
            for k in range(3):
                d2d(w, k, 1 - c).wait_recv()
        for w in range(n):
            for k in range(3):
                ici(w, k, mine).wait_send()
                d2d(w, k, c).wait_send()

    vm = pl.BlockSpec(memory_space=pltpu.VMEM)
    return pl.pallas_call(
        body, name="ag_weights",
        out_shape=[jax.ShapeDtypeStruct((N_SHARD,) + s.shape, BF16) for s in shards],
        in_specs=[vm] * n, out_specs=[vm] * n,
        scratch_shapes=[pltpu.SemaphoreType.DMA((3 * n,))] * 4,
        compiler_params=pltpu.CompilerParams(vmem_limit_bytes=VMEM_LIMIT),
    )(*shards)


def _rs_pair_exchange(send_half):
    def body(src_ref, dst_ref, send_sem, recv_sem):
        x, y, c = lax.axis_index("x"), lax.axis_index("y"), lax.axis_index("c")
        cp = pltpu.make_async_remote_copy(src_ref=src_ref, dst_ref=dst_ref, send_sem=send_sem, recv_sem=recv_sem,
                                          device_id=(x, y, 1 - c), device_id_type=MESH)
        cp.start()
        cp.wait()

    hbm = pl.BlockSpec(memory_space=pl.ANY)
    return pl.pallas_call(
        body, name="rs_pair_exchange",
        out_shape=jax.ShapeDtypeStruct(send_half.shape, send_half.dtype),
        in_specs=[hbm], out_specs=hbm,
        scratch_shapes=[pltpu.SemaphoreType.DMA, pltpu.SemaphoreType.DMA],
    )(send_half)


def _rs_chip_exchange(part):
    def body(src_ref, dst_ref, send_sems, recv_sems):
        x, y, c = lax.axis_index("x"), lax.axis_index("y"), lax.axis_index("c")
        flips = ((1, 0), (0, 1), (1, 1))
        cps = []
        for k, (fx, fy) in enumerate(flips):
            theirs = 2 * (x ^ fx) + (y ^ fy)
            cps.append(pltpu.make_async_remote_copy(
                src_ref=src_ref.at[theirs], dst_ref=dst_ref.at[k],
                send_sem=send_sems.at[k], recv_sem=recv_sems.at[k],
                device_id=(x ^ fx, y ^ fy, c), device_id_type=MESH))
        for cp in cps:
            cp.start()
        for cp in cps:
            cp.wait()

    hbm = pl.BlockSpec(memory_space=pl.ANY)
    return pl.pallas_call(
        body, name="rs_chip_exchange",
        out_shape=jax.ShapeDtypeStruct((3,) + part.shape[1:], part.dtype),
        in_specs=[hbm], out_specs=hbm,
        scratch_shapes=[pltpu.SemaphoreType.DMA((3,)), pltpu.SemaphoreType.DMA((3,))],
    )(part)


def _rs_pair_gather(half):
    def body(src_ref, dst_ref, send_sem, recv_sem, local_sem):
        x, y, c = lax.axis_index("x"), lax.axis_index("y"), lax.axis_index("c")
        own = pltpu.make_async_copy(src_ref, dst_ref.at[c], local_sem)
        own.start()
        cp = pltpu.make_async_remote_copy(src_ref=src_ref, dst_ref=dst_ref.at[c], send_sem=send_sem, recv_sem=recv_sem,
                                          device_id=(x, y, 1 - c), device_id_type=MESH)
        cp.start()
        arrival = pltpu.make_async_remote_copy(src_ref=src_ref, dst_ref=dst_ref.at[1 - c], send_sem=send_sem,
                                               recv_sem=recv_sem, device_id=(x, y, 1 - c), device_id_type=MESH)
        arrival.wait_recv()
        cp.wait_send()
        own.wait()

    hbm = pl.BlockSpec(memory_space=pl.ANY)
    return pl.pallas_call(
        body, name="rs_pair_gather",
        out_shape=jax.ShapeDtypeStruct((2,) + half.shape, half.dtype),
        in_specs=[hbm], out_specs=hbm,
        scratch_shapes=[pltpu.SemaphoreType.DMA, pltpu.SemaphoreType.DMA, pltpu.SemaphoreType.DMA],
    )(half)


def _small_allreduce(vec):
    def body(in_ref, out_ref, recv, send_sems, recv_sems):
        x, y, c = lax.axis_index("x"), lax.axis_index("y"), lax.axis_index("c")
        me = 4 * x + 2 * y + c
        cps = []
        for k in range(1, 8):
            fx, fy, fc = (k >> 2) & 1, (k >> 1) & 1, k & 1
            cps.append(pltpu.make_async_remote_copy(
                src_ref=in_ref, dst_ref=recv.at[k - 1],
                send_sem=send_sems.at[k - 1], recv_sem=recv_sems.at[k - 1],
                device_id=(x ^ fx, y ^ fy, c ^ fc), device_id_type=MESH))
        for cp in cps:
            cp.start()
        for cp in cps:
            cp.wait()
        acc = jnp.zeros(in_ref.shape, F32)
        for j in range(8):
            slot = jnp.maximum((me ^ j) - 1, 0)
            acc = acc + jnp.where(me == j, in_ref[...], recv[slot])
        out_ref[...] = acc

    vm = pl.BlockSpec(memory_space=pltpu.VMEM)
    return pl.pallas_call(
        body, name="small_allreduce",
        out_shape=jax.ShapeDtypeStruct(vec.shape, F32),
        in_specs=[vm], out_specs=vm,
        scratch_shapes=[pltpu.VMEM((7,) + vec.shape, F32), pltpu.SemaphoreType.DMA((7,)),
                        pltpu.SemaphoreType.DMA((7,))],
    )(vec)


def _add_n(arrs, name, rows, also_bf16=False):
    n = len(arrs)
    r = arrs[0].shape[0]

    def body(*refs):
        acc = refs[0][...].astype(F32)
        for k in range(1, n):
            acc = acc + refs[k][...].astype(F32)
        refs[n][...] = acc
        if also_bf16:
            refs[n + 1][...] = acc.astype(BF16)

    spec = pl.BlockSpec((rows, LANES), lambda i: (i, 0))
    out_shape = [jax.ShapeDtypeStruct(arrs[0].shape, F32)]
    if also_bf16:
        out_shape.append(jax.ShapeDtypeStruct(arrs[0].shape, BF16))
    return pl.pallas_call(
        body, name=name, grid=(r // rows,),
        out_shape=out_shape,
        in_specs=[spec] * n, out_specs=[spec] * len(out_shape),
        compiler_params=_cparams(("parallel",)),
    )(*arrs)


def _adamw(w, g, m, v, name, rows):
    r = w.shape[0]

    def body(w_ref, g_ref, m_ref, v_ref, d_ref, nm_ref, nv_ref):
        gg = g_ref[...]
        nm = B1 * m_ref[...] + (1.0 - B1) * gg
        nv = B2 * v_ref[...] + (1.0 - B2) * (gg * gg)
        m_hat = nm / (1.0 - B1 ** STEP)
        v_hat = nv / (1.0 - B2 ** STEP)
        d_ref[...] = -LR * (m_hat / (jnp.sqrt(v_hat) + ADAM_EPS) + WD * w_ref[...])
        nm_ref[...] = nm
        nv_ref[...] = nv

    spec = pl.BlockSpec((rows, LANES), lambda i: (i, 0))
    sds = jax.ShapeDtypeStruct(w.shape, F32)
    return pl.pallas_call(
        body, name=name, grid=(r // rows,),
        out_shape=[sds, sds, sds],
        in_specs=[spec] * 4, out_specs=[spec] * 3,
        compiler_params=_cparams(("parallel",)),
    )(w, g, m, v)


def _pre_fwd(x, pos, invf, gpre, wp, gq, wuq, gkv, wukv, mu, w0, w2p, a0, a2p, k_k, k_a, bo):
    bsz, t, _ = x.shape
    nt = t // TT

    def body(x_ref, pos_ref, invf_ref, gpre_ref, wp_ref, gq_ref, wuq_ref, gkv_ref, wukv_ref, mu_ref, w0_ref,
             w2p_ref, a0_ref, a2p_ref, kk_ref, ka_ref, bo_ref,
             u_ref, pp_ref, q_ref, k_ref, v_ref, r_o, w_o, kp_o, vv_o, al_o, be_o, carry):
        i = pl.program_id(1)
        u, _, _ = _rms(x_ref[0], gpre_ref[...], D)
        ub = u.astype(BF16)
        u_ref[0] = ub
        p = _dot(ub, wp_ref[...])
        pp_ref[0] = p
        prw = p[:, RW0:DP]

        @pl.when(i == 0)
        def _():
            carry[...] = jnp.zeros(carry.shape, F32)

        ps, _ = _shift_mix(prw, carry[7:8, :], mu_ref[...])
        carry[...] = prw[TT - 8:TT, :]

        g = _rw_gates(ps, w0_ref[...], w2p_ref[...], a0_ref[...], a2p_ref[...], kk_ref[...], ka_ref[...],
                      bo_ref[...])
        r_o[0] = g["r"]
        w_o[0] = g["w"]
        kp_o[0] = g["kp"]
        vv_o[0] = g["v"]
        al_o[0] = -g["kk"]
        be_o[0] = g["kk"] * g["a"]

        cqn, _, _ = _rms(p[:, CQ0:CQ0 + 256], gq_ref[...], 256)
        q = _dot(cqn.astype(BF16), wuq_ref[...])
        ckvn, _, _ = _rms(p[:, CKV0:CKV0 + 128], gkv_ref[...], 128)
        kv = _dot(ckvn.astype(BF16), wukv_ref[...])
        ang = pos_ref[0] * invf_ref[...]
        cs, sn = jnp.cos(ang), jnp.sin(ang)
        lane = lax.broadcasted_iota(jnp.int32, cs.shape, 1)
        kr = ps[:, 1536:1536 + LANES]
        kr = jnp.where(lane < 64, kr * cs + _rot(kr) * sn, 0.0).astype(BF16)
        for h in range(HEADS):
            qr = q[:, 256 * h + 128:256 * h + 256]
            q_ref[0, :, 256 * h:256 * h + 128] = q[:, 256 * h:256 * h + 128].astype(BF16)
            q_ref[0, :, 256 * h + 128:256 * h + 256] = (qr * cs + _rot(qr) * sn).astype(BF16)
            k_ref[0, :, 256 * h:256 * h + 128] = kv[:, 128 * h:128 * h + 128].astype(BF16)
            k_ref[0, :, 256 * h + 128:256 * h + 256] = kr
        v_ref[0] = kv[:, 512:1024].astype(BF16)

    tok = lambda c: pl.BlockSpec((1, TT, c), lambda b, i: (b, i, 0))
    full = lambda a: _full(a.shape)
    ins = (x, pos, invf, gpre, wp, gq, wuq, gkv, wukv, mu, w0, w2p, a0, a2p, k_k, k_a, bo)
    in_specs = [tok(D), tok(1)] + [full(a) for a in ins[2:]]
    sd = lambda c, dt: jax.ShapeDtypeStruct((bsz, t, c), dt)
    out_shape = [sd(D, BF16), sd(DP, F32), sd(1024, BF16), sd(1024, BF16), sd(512, BF16)] + [sd(RW, F32)] * 6
    out_specs = [tok(D), tok(DP), tok(1024), tok(1024), tok(512)] + [tok(RW)] * 6
    return pl.pallas_call(
        body, name="pre_fwd", grid=(bsz, nt), out_shape=out_shape, in_specs=in_specs, out_specs=out_specs,
        scratch_shapes=[pltpu.VMEM((8, NRW), F32)],
        compiler_params=_cparams(("arbitrary", "arbitrary")),
    )(*ins)


def _attn_fwd(q, k, v):
    bsz, t, _ = q.shape
    nq = t // TQ

    def body(q_ref, k_ref, v_ref, o_ref, lse_ref):
        i = pl.program_id(2)
        qt = q_ref[0]
        row = lax.broadcasted_iota(jnp.int32, (TQ, TQ), 0) + i * TQ
        col0 = lax.broadcasted_iota(jnp.int32, (TQ, TQ), 1)

        def step(j, carry):
            m, l, acc = carry
            at = pl.ds(pl.multiple_of(j * TQ, TQ), TQ)
            s = _dot_nt(qt, k_ref[0, at, :]) * SCALE
            s = jnp.where(col0 + j * TQ <= row, s, -1e30)
            mn = jnp.maximum(m, jnp.max(s, axis=1, keepdims=True))
            p = jnp.exp(s - mn)
            al = jnp.exp(m - mn)
            l = al * l + jnp.sum(p, axis=1, keepdims=True)
            acc = al * acc + _dot(p.astype(BF16), v_ref[0, at, :])
            return mn, l, acc

        m, l, acc = lax.fori_loop(
            0, i + 1, step,
            (jnp.full((TQ, 1), -1e30, F32), jnp.zeros((TQ, 1), F32), jnp.zeros((TQ, LANES), F32)))
        o_ref[0] = acc / l
        lse_ref[0, 0] = jnp.broadcast_to(m + jnp.log(l), (TQ, LANES))

    return pl.pallas_call(
        body, name="attn_fwd", grid=(bsz, HEADS, nq),
        out_shape=[jax.ShapeDtypeStruct((bsz, t, 512), F32), jax.ShapeDtypeStruct((bsz, HEADS, t, LANES), F32)],
        in_specs=[pl.BlockSpec((1, TQ, 256), lambda b, h, i: (b, i, h)),
                  pl.BlockSpec((1, t, 256), lambda b, h, i: (b, 0, h)),
                  pl.BlockSpec((1, t, LANES), lambda b, h, i: (b, 0, h))],
        out_specs=[pl.BlockSpec((1, TQ, LANES), lambda b, h, i: (b, i, h)),
                   pl.BlockSpec((1, 1, TQ, LANES), lambda b, h, i: (b, h, i, 0))],
        compiler_params=_cparams(("parallel", "parallel", "arbitrary")),
    )(q, k, v)


def _attn_bwd(q, k, v, o, lse, do):
    bsz, t, _ = q.shape
    nq = t // TQ

    def body(q_ref, k_ref, v_ref, o_ref, lse_ref, do_ref, dq_ref, dk_ref, dv_ref, dl_ref):
        def prep(i, _):
            at = pl.ds(pl.multiple_of(i * TQ, TQ), TQ)
            dl_ref[at, :] = jnp.broadcast_to(jnp.sum(do_ref[0, at, :] * o_ref[0, at, :], axis=1, keepdims=True),
                                             (TQ, LANES))
            return 0

        lax.fori_loop(0, nq, prep, 0)
        dq_ref[0] = jnp.zeros((t, 256), F32)
        row0 = lax.broadcasted_iota(jnp.int32, (TQ, TQ), 0)
        col0 = lax.broadcasted_iota(jnp.int32, (TQ, TQ), 1)

        def kv_tile(j, _):
            atk = pl.ds(pl.multiple_of(j * TQ, TQ), TQ)
            kt = k_ref[0, atk, :]
            vt = v_ref[0, atk, :]

            def q_tile(i, carry):
                dk, dv = carry
                atq = pl.ds(pl.multiple_of(i * TQ, TQ), TQ)
                qt = q_ref[0, atq, :]
                dob = do_ref[0, atq, :].astype(BF16)
                s = _dot_nt(qt, kt) * SCALE
                s = jnp.where(col0 + j * TQ <= row0 + i * TQ, s, -1e30)
                p = jnp.exp(s - lse_ref[0, 0, atq, :][:, 0:1])
                dv = dv + _dot_tn(p.astype(BF16), dob)
                dp = _dot_nt(dob, vt)
                ds = (p * (dp - dl_ref[atq, :][:, 0:1]) * SCALE).astype(BF16)
                dk = dk + _dot_tn(ds, qt)
                dq_ref[0, atq, :] += _dot(ds, kt)
                return dk, dv

            dk, dv = lax.fori_loop(j, nq, q_tile, (jnp.zeros((TQ, 256), F32), jnp.zeros((TQ, LANES), F32)))
            dk_ref[0, atk, :] = dk
            dv_ref[0, atk, :] = dv
            return 0

        lax.fori_loop(0, nq, kv_tile, 0)

    s256 = pl.BlockSpec((1, t, 256), lambda b, h: (b, 0, h))
    s128 = pl.BlockSpec((1, t, LANES), lambda b, h: (b, 0, h))
    return pl.pallas_call(
        body, name="attn_bwd", grid=(bsz, HEADS),
        out_shape=[jax.ShapeDtypeStruct((bsz, t, 1024), F32), jax.ShapeDtypeStruct((bsz, t, 1024), F32),
                   jax.ShapeDtypeStruct((bsz, t, 512), F32)],
        in_specs=[s256, s256, s128, s128, pl.BlockSpec((1, 1, t, LANES), lambda b, h: (b, h, 0, 0)), s128],
        out_specs=[s256, s256, s128],
        scratch_shapes=[pltpu.VMEM((t, LANES), F32)],
        compiler_params=_cparams(("parallel", "parallel")),
    )(q, k, v, o, lse, do)


RW_HEADS = 8
CH = 16


def _lane_split(bsz):
    vs = LANES // (bsz * RW_HEADS)
    return vs, 64 // vs


def _gather_matrix(bsz):
    group = bsz * RW_HEADS
    vs = LANES // group
    half = (RW_HEADS // 2) * bsz * SPREAD_STEPS
    p = np.zeros((SPREAD_STEPS // vs * LANES, 2 * half), np.float32)
    for g2 in range(SPREAD_STEPS // vs):
        for j in range(vs):
            for b in range(bsz):
                for h in range(RW_HEADS):
                    hp, hpar = h // 2, h % 2
                    p[g2 * LANES + j * group + b * RW_HEADS + h,
                      hpar * half + (hp * bsz + b) * SPREAD_STEPS + g2 * vs + j] = 1.0
    return jnp.asarray(np.concatenate([p] * 3, axis=0), BF16)


def _gather_k(ys, bsz):
    vs = LANES // (bsz * RW_HEADS)
    assert (RW_HEADS // 2) * bsz * SPREAD_STEPS == LANES, "the transposed tile must be 128 lanes wide"
    tg = ys[0].shape[0]
    n = len(ys)
    ngrp = SPREAD_BLOCK // SPREAD_STEPS
    per = SPREAD_STEPS // vs

    def body(*refs):
        pm = refs[n][...]
        for y_ref, o_ref in zip(refs[:n], refs[n + 1:]):
            lhs = jnp.concatenate(
                [jnp.concatenate(_split3(jnp.concatenate([y_ref[per * m + g2] for g2 in range(per)], axis=1)), axis=1)
                 for m in range(ngrp)], axis=0)
            a = _dot(lhs, pm)
            for m in range(ngrp):
                am = a[64 * m:64 * (m + 1)]
                bt = jnp.concatenate([am[:, 0:LANES], am[:, LANES:2 * LANES]], axis=0).T
                for hp in range(RW_HEADS // 2):
                    for b in range(bsz):
                        at = (hp * bsz + b) * SPREAD_STEPS
                        o_ref[b, SPREAD_STEPS * m:SPREAD_STEPS * (m + 1), LANES * hp:LANES * (hp + 1)] = \
                            bt[at:at + SPREAD_STEPS]

    pm = _gather_matrix(bsz)
    return pl.pallas_call(
        body, name="wkv_gather", grid=(tg * vs // SPREAD_BLOCK,),
        out_shape=[jax.ShapeDtypeStruct((bsz, tg * vs, RW), F32)] * n,
        in_specs=[pl.BlockSpec((SPREAD_BLOCK // vs, 64, LANES), lambda i: (i, 0, 0))] * n + [_full(pm.shape)],
        out_specs=[pl.BlockSpec((bsz, SPREAD_BLOCK, RW), lambda i: (0, i, 0))] * n,
        compiler_params=_cparams(("parallel",)),
    )(*ys, pm)


def _to_v(x):
    bsz, t, _ = x.shape
    vs, vq = _lane_split(bsz)
    return jnp.transpose(x.reshape(bsz, t, RW_HEADS, vq, vs), (1, 3, 4, 0, 2)).reshape(t, vq, LANES)


def _from_v(y, bsz):
    t = y.shape[0]
    vs, vq = _lane_split(bsz)
    return jnp.transpose(y.reshape(t, vq, vs, bsz, RW_HEADS), (3, 0, 4, 1, 2)).reshape(bsz, t, RW)


def _ksum(a):
    return jnp.sum(a, axis=0, keepdims=True)


def _fold(a, group):
    sh = LANES // 2
    while sh >= group:
        a = a + pltpu.roll(a, sh, 1)
        sh //= 2
    return a


def _lane_group(shape, group):
    return lax.broadcasted_iota(jnp.int32, shape, 1) // group


SPREAD_STEPS = 8
SPREAD_BLOCK = 32


def _spread_matrix(bsz):
    group = bsz * RW_HEADS
    vs = LANES // group
    rows = (RW_HEADS // 2) * bsz * SPREAD_STEPS
    q = np.zeros((2, rows, SPREAD_STEPS * LANES), np.float32)
    for hpar in range(2):
        for hp in range(RW_HEADS // 2):
            for b in range(bsz):
                for st in range(SPREAD_STEPS):
                    row = (hp * bsz + b) * SPREAD_STEPS + st
                    for s in range(vs):
                        q[hpar, row, st * LANES + s * group + b * RW_HEADS + 2 * hp + hpar] = 1.0
    return jnp.asarray(np.concatenate([q[0], q[1]] * 3, axis=0), BF16)


def _spread_k(xs):
    bsz, t, _ = xs[0].shape
    assert (RW_HEADS // 2) * bsz * SPREAD_STEPS == LANES, "the transposed tile must be 128 lanes wide"
    n = len(xs)
    ngrp = SPREAD_BLOCK // SPREAD_STEPS

    def body(*refs):
        qm = refs[n][...]
        for x_ref, o_ref in zip(refs[:n], refs[n + 1:]):
            cols = [[] for _ in range(6)]
            for m in range(ngrp):
                at = slice(SPREAD_STEPS * m, SPREAD_STEPS * (m + 1))
                x8 = jnp.concatenate([x_ref[b, at, LANES * hp:LANES * (hp + 1)]
                                      for hp in range(RW_HEADS // 2) for b in range(bsz)], axis=0)
                for pi, piece in enumerate(_split3(x8.T)):
                    cols[2 * pi].append(piece[0:64])
                    cols[2 * pi + 1].append(piece[64:128])
            lhs = jnp.concatenate([jnp.concatenate(c, axis=0) for c in cols], axis=1)
            y = _dot(lhs, qm)
            for m in range(ngrp):
                for st in range(SPREAD_STEPS):
                    o_ref[SPREAD_STEPS * m + st] = y[64 * m:64 * (m + 1), LANES * st:LANES * (st + 1)]

    qm = _spread_matrix(bsz)
    return pl.pallas_call(
        body, name="wkv_spread", grid=(t // SPREAD_BLOCK,),
        out_shape=[jax.ShapeDtypeStruct((t, 64, LANES), F32)] * n,
        in_specs=[pl.BlockSpec((bsz, SPREAD_BLOCK, RW), lambda i: (0, i, 0))] * n + [_full(qm.shape)],
        out_specs=[pl.BlockSpec((SPREAD_BLOCK, 64, LANES), lambda i: (i, 0, 0))] * n,
        compiler_params=_cparams(("parallel",)),
    )(*xs, qm)


def _wkv_fwd(r, w, kp, al, be, v):
    t, vq = v.shape[0], v.shape[1]

    def body(r_ref, w_ref, kp_ref, al_ref, be_ref, v_ref, y_ref, a_ref, u_ref, st_ref):
        @pl.when(pl.program_id(0) == 0)
        def _():
            st_ref[...] = jnp.zeros(st_ref.shape, F32)

        def step(tl, _):
            rv, wv, kv, av, bv = r_ref[tl], w_ref[tl], kp_ref[tl], al_ref[tl], be_ref[tl]
            vals = v_ref[tl]
            yrows, urows = [], []
            for q in range(vq):
                s = st_ref[q]
                u = _ksum(s * av)
                s = s * wv + bv * u + kv * vals[q:q + 1]
                st_ref[q] = s
                a_ref[tl, q] = s
                urows.append(u)
                yrows.append(_ksum(s * rv))
            y_ref[tl] = jnp.concatenate(yrows, axis=0)
            u_ref[tl] = jnp.concatenate(urows, axis=0)
            return 0

        lax.fori_loop(0, CH, step, 0)

    kspec = pl.BlockSpec((CH, 64, LANES), lambda i: (i, 0, 0))
    vspec = pl.BlockSpec((CH, vq, LANES), lambda i: (i, 0, 0))
    vsd = jax.ShapeDtypeStruct((t, vq, LANES), F32)
    return pl.pallas_call(
        body, name="wkv_fwd", grid=(t // CH,),
        out_shape=[vsd, jax.ShapeDtypeStruct((t, vq, 64, LANES), F32), vsd],
        in_specs=[kspec] * 5 + [vspec],
        out_specs=[vspec, pl.BlockSpec((CH, vq, 64, LANES), lambda i: (i, 0, 0, 0)), vspec],
        scratch_shapes=[pltpu.VMEM((vq, 64, LANES), F32)],
        compiler_params=_cparams(("arbitrary",)),
    )(r, w, kp, al, be, v)


def _wkv_bwd(r, w, kp, al, be, v, dy, states, u):
    t, vq = v.shape[0], v.shape[1]
    vs = 64 // vq
    group = LANES // vs
    n = t // CH
    ng = CH // vs

    def body(r_ref, w_ref, kp_ref, al_ref, be_ref, v_ref, dy_ref, u_ref, a_ref, ap_ref,
             dr_ref, dw_ref, dkp_ref, dal_ref, dbe_ref, dv_ref, ds_ref):
        @pl.when(pl.program_id(0) == 0)
        def _():
            ds_ref[...] = jnp.zeros(ds_ref.shape, F32)

        earliest = pl.program_id(0) == n - 1

        def reverse(i, _):
            g = ng - 1 - i
            grp = _lane_group((64, LANES), group)
            outs = None
            for j in reversed(range(vs)):
                tl = g * vs + j
                rv, wv, kv, av, bv = r_ref[tl], w_ref[tl], kp_ref[tl], al_ref[tl], be_ref[tl]
                vals, dys, us = v_ref[tl], dy_ref[tl], u_ref[tl]
                acc = None
                dvrows = []
                for q in range(vq):
                    if j > 0:
                        s_prev = a_ref[tl - 1, q]
                    else:
                        before = jnp.where(earliest, 0.0, ap_ref[0, q])
                        s_prev = jnp.where(g == 0, before, a_ref[jnp.maximum(tl - 1, 0), q])
                    dyq = dys[q:q + 1]
                    ds = ds_ref[q] + rv * dyq
                    c = _ksum(ds * bv)
                    dvrows.append(_ksum(ds * kv))
                    terms = (a_ref[tl, q] * dyq, ds * s_prev, ds * vals[q:q + 1], s_prev * c, ds * us[q:q + 1])
                    acc = terms if acc is None else tuple(a + b for a, b in zip(acc, terms))
                    ds_ref[q] = ds * wv + av * c
                dv_ref[tl] = jnp.concatenate(dvrows, axis=0)
                summed = [_fold(a, group) for a in acc]
                outs = summed if outs is None else [jnp.where(grp == j, f, o) for f, o in zip(summed, outs)]
            for ref, o in zip((dr_ref, dw_ref, dkp_ref, dal_ref, dbe_ref), outs):
                ref[g] = o
            return 0

        lax.fori_loop(0, ng, reverse, 0)

    kspec = pl.BlockSpec((CH, 64, LANES), lambda i: (n - 1 - i, 0, 0))
    gspec = pl.BlockSpec((ng, 64, LANES), lambda i: (n - 1 - i, 0, 0))
    vspec = pl.BlockSpec((CH, vq, LANES), lambda i: (n - 1 - i, 0, 0))
    ksd = jax.ShapeDtypeStruct((t // vs, 64, LANES), F32)
    return pl.pallas_call(
        body, name="wkv_bwd", grid=(n,),
        out_shape=[ksd] * 5 + [jax.ShapeDtypeStruct((t, vq, LANES), F32)],
        in_specs=[kspec] * 5 + [vspec, vspec, vspec,
                                pl.BlockSpec((CH, vq, 64, LANES), lambda i: (n - 1 - i, 0, 0, 0)),
                                pl.BlockSpec((1, vq, 64, LANES), lambda i: (jnp.maximum((n - 1 - i) * CH - 1, 0), 0, 0, 0))],
        out_specs=[gspec] * 5 + [vspec],
        scratch_shapes=[pltpu.VMEM((vq, 64, LANES), F32)],
        compiler_params=_cparams(("arbitrary",)),
    )(r, w, kp, al, be, v, dy, u, states, states)


def _post(x, tgt, pp, o, yw, r, kp, v, ln_g, ln_b, r_k, wo, wot, gpost, bo):
    bsz, t, _ = x.shape
    tt = TT_VPU
    nt = t // tt

    def body(x_ref, tgt_ref, z_ref, o_ref, yw_ref, r_ref, kp_ref, v_ref, lng_ref, lnb_ref, rk_ref, wo_ref, wot_ref,
             gpost_ref, bo_ref,
             dh_ref, dz_ref, dym_ref, dyw_ref, dbon_ref, loss_ref, dwo_ref, dgpost_ref, dlng_ref, dlnb_ref, drk_ref):
        first = (pl.program_id(0) == 0) & (pl.program_id(1) == 0)

        @pl.when(first)
        def _():
            for ref in (loss_ref, dwo_ref, dgpost_ref, dlng_ref, dlnb_ref, drk_ref):
                ref[...] = jnp.zeros(ref.shape, F32)

        bo_m = bo_ref[...]
        seg = lambda a: _seg(a, bo_m)
        rowsum = lambda a: jnp.sum(a, axis=0, keepdims=True)
        ywv, rv, kpv, vv = yw_ref[0], r_ref[0], kp_ref[0], v_ref[0]
        ln_g, r_k = lng_ref[...], rk_ref[...]
        mean = seg(ywv) * (1.0 / 64)
        yc = ywv - mean
        rstd = lax.rsqrt(seg(yc * yc) * (1.0 / 64) + GN_EPS)
        yhat = yc * rstd
        sb = seg(rv * kpv * r_k)
        y_rw = yhat * ln_g + lnb_ref[...] + sb * vv
        z = z_ref[0]
        sig = _sigmoid(z)
        sz = z * sig
        ycat = jnp.concatenate([o_ref[0], y_rw], axis=1)
        ycg = (ycat * sz).astype(BF16)
        out = _dot(ycg, wo_ref[...])
        hn, nx, rstd_o = _rms(out, gpost_ref[...], D)
        err = x_ref[0] + hn - tgt_ref[0]
        loss_ref[...] += jnp.sum(err * err) * (0.5 / D)
        dh = err * (1.0 / D)
        dh_ref[0] = dh
        dout, dgp = _rms_bwd(dh, nx, rstd_o, gpost_ref[...], D)
        dgpost_ref[...] += dgp
        doutb = dout.astype(BF16)
        dwo_ref[...] += _dot_tn(ycg, doutb)
        dycg = _dot(doutb, wot_ref[...])
        dz_ref[0] = dycg * ycat * (sig * (1.0 + z * (1.0 - sig)))
        dycat = dycg * sz
        dym_ref[0] = dycat[:, 0:512]
        dy_rw = dycat[:, 512:1024]
        dlnb_ref[...] += rowsum(dy_rw)
        dlng_ref[...] += rowsum(dy_rw * yhat)
        dyhat = dy_rw * ln_g
        dyw_ref[0] = rstd * (dyhat - seg(dyhat) * (1.0 / 64) - yhat * (seg(dyhat * yhat) * (1.0 / 64)))
        dsb = seg(dy_rw * vv)
        drk_ref[...] += rowsum(dsb * rv * kpv)
        dbon_ref[0, :, 0:512] = dsb * kpv * r_k
        dbon_ref[0, :, 512:1024] = dsb * rv * r_k
        dbon_ref[0, :, 1024:1536] = dy_rw * sb

    tok = lambda c: pl.BlockSpec((1, tt, c), lambda b, i: (b, i, 0))
    full = lambda a: _full(a.shape)
    ins = (x, tgt, pp, o, yw, r, kp, v, ln_g, ln_b, r_k, wo, wot, gpost, bo)
    in_specs = [tok(D), tok(D), tok(1024)] + [tok(512)] * 5 + [full(a) for a in ins[8:]]
    sd = lambda c: jax.ShapeDtypeStruct((bsz, t, c), F32)
    vec = lambda c: jax.ShapeDtypeStruct((1, c), F32)
    out_shape = [sd(D), sd(1024), sd(512), sd(512), sd(1536), jax.ShapeDtypeStruct((8, LANES), F32),
                 jax.ShapeDtypeStruct((1024, 1024), F32), vec(D), vec(512), vec(512), vec(512)]
    out_specs = [tok(D), tok(1024), tok(512), tok(512), tok(1536), _resident((8, LANES)), _resident((1024, 1024)),
                 _resident((1, D)), _resident((1, 512)), _resident((1, 512)), _resident((1, 512))]
    return pl.pallas_call(
        body, name="post", grid=(bsz, nt), out_shape=out_shape, in_specs=in_specs, out_specs=out_specs,
        compiler_params=_cparams(("arbitrary", "arbitrary")),
    )(*ins)


def _pre_bwd_a(pp, pos, invf, cqkv_w, mu, w0, w2p, w2pt, a0, a2p, a2pt, k_k, k_a, bo,
               dq, dk, dva, dwkv, dbon):
    gq, wuqt, gkv, wukvt = cqkv_w
    bsz, t, _ = pp.shape
    tt = TT_VPU
    nt = t // tt
    dr_w, dw_w, dkp_w, dv_w, dal_w, dbe_w = dwkv

    def body(pp_ref, pos_ref, invf_ref, gq_ref, wuqt_ref, gkv_ref, wukvt_ref, mu_ref, w0_ref, w2p_ref, w2pt_ref,
             a0_ref, a2p_ref, a2pt_ref, kk_ref, ka_ref, bo_ref, dq_ref, dk_ref, dva_ref,
             dr_ref, dw_ref, dkp_ref, dv_ref, dal_ref, dbe_ref, dbon_ref,
             da_ref, dwuq_ref, dwukv_ref, dw2p_ref, da2p_ref, dgq_ref, dgkv_ref, dmu_ref, dw0_ref, da0_ref,
             dkk_ref, dka_ref, carry):
        i = pl.program_id(1)
        first = (pl.program_id(0) == 0) & (i == 0)

        @pl.when(first)
        def _():
            for ref in (dwuq_ref, dwukv_ref, dw2p_ref, da2p_ref, dgq_ref, dgkv_ref, dmu_ref, dw0_ref, da0_ref,
                        dkk_ref, dka_ref):
                ref[...] = jnp.zeros(ref.shape, F32)

        bo_m = bo_ref[...]
        rowsum = lambda a: jnp.sum(a, axis=0, keepdims=True)
        prw = pp_ref[0, :, RW0:DP]

        @pl.when(i == 0)
        def _():
            carry[...] = jnp.zeros(carry.shape, F32)

        ps, sh = _shift_mix(prw, carry[7:8, :], mu_ref[...])
        carry[...] = prw[tt - 8:tt, :]
        k_k, k_a = kk_ref[...], ka_ref[...]
        g = _rw_gates(ps, w0_ref[...], w2p_ref[...], a0_ref[...], a2p_ref[...], k_k, k_a, bo_m)
        a, kk, k = g["a"], g["kk"], g["k"]
        dr = dr_ref[0] + dbon_ref[0, :, 0:512]
        dkp = dkp_ref[0] + dbon_ref[0, :, 512:1024]
        dv = dv_ref[0] + dbon_ref[0, :, 1024:1536]
        dbe = dbe_ref[0]
        dkk = dbe * a - dal_ref[0]
        da = dbe * kk + dkp * k * k_a
        dka_ref[...] += rowsum(dkp * k * (a - 1.0))
        dm = (dkk - kk * _seg(dkk * kk, bo_m)) / g["nrm"]
        dkk_ref[...] += rowsum(dm * k)
        dk_tot = dkp * (1.0 + (a - 1.0) * k_a) + dm * k_k
        dapre = da * a * (1.0 - a)
        da0_ref[...] += rowsum(dapre)
        dapb = dapre.astype(BF16)
        da2p_ref[...] += _dot_tn(g["misc"].astype(BF16), dapb)
        dwpre = dw_ref[0] * g["w"] * (-g["e"]) * _sigmoid(-g["wpre"])
        dw0_ref[...] += rowsum(dwpre)
        dwpb = dwpre.astype(BF16)
        th = g["th"]
        dw2p_ref[...] += _dot_tn(th.astype(BF16), dwpb)
        dmisc = _dot(dapb, a2pt_ref[...]) + _dot(dwpb, w2pt_ref[...]) * (1.0 - th * th)
        ang = pos_ref[0] * invf_ref[...]
        cs, sn = jnp.cos(ang), jnp.sin(ang)
        unrope = lambda gr: gr * cs - _rot(gr * sn)
        lane = lax.broadcasted_iota(jnp.int32, cs.shape, 1)
        dkr = dk_ref[0, :, 128:256]
        for h in range(1, HEADS):
            dkr = dkr + dk_ref[0, :, 256 * h + 128:256 * h + 256]
        dkr = jnp.where(lane < 64, unrope(dkr), 0.0)
        dmisc = dmisc + jnp.concatenate([dkr, jnp.zeros_like(dkr)], axis=1)
        dqp = jnp.concatenate(
            [blk for h in range(HEADS)
             for blk in (dq_ref[0, :, 256 * h:256 * h + 128], unrope(dq_ref[0, :, 256 * h + 128:256 * h + 256]))],
            axis=1).astype(BF16)
        dkvp = jnp.concatenate([dk_ref[0, :, 256 * h:256 * h + 128] for h in range(HEADS)] + [dva_ref[0]],
                               axis=1).astype(BF16)
        cqn, cq_nx, cq_rstd = _rms(pp_ref[0, :, CQ0:CQ0 + 256], gq_ref[...], 256)
        ckvn, ckv_nx, ckv_rstd = _rms(pp_ref[0, :, CKV0:CKV0 + 128], gkv_ref[...], 128)
        dwuq_ref[...] += _dot_tn(cqn.astype(BF16), dqp)
        dwukv_ref[...] += _dot_tn(ckvn.astype(BF16), dkvp)
        dcq, dgq = _rms_bwd(_dot(dqp, wuqt_ref[...]), cq_nx, cq_rstd, gq_ref[...], 256)
        dckv, dgkv = _rms_bwd(_dot(dkvp, wukvt_ref[...]), ckv_nx, ckv_rstd, gkv_ref[...], 128)
        dgq_ref[...] += dgq
        dgkv_ref[...] += dgkv
        dps = jnp.concatenate([dr, dk_tot, dv, dmisc], axis=1)
        dmu_ref[...] += rowsum(dps * (sh - prw))
        da_ref[0, :, 0:256] = dcq
        da_ref[0, :, 256:384] = dckv
        da_ref[0, :, 384:384 + NRW] = dps

    tok = lambda c: pl.BlockSpec((1, tt, c), lambda b, i: (b, i, 0))
    full = lambda a: _full(a.shape)
    ins = (pp, pos, invf, gq, wuqt, gkv, wukvt, mu, w0, w2p, w2pt, a0, a2p, a2pt, k_k, k_a, bo,
           dq, dk, dva, dr_w, dw_w, dkp_w, dv_w, dal_w, dbe_w, dbon)
    in_specs = ([tok(DP), tok(1)] + [full(a) for a in ins[2:17]] + [tok(1024), tok(1024), tok(512)]
                + [tok(512)] * 6 + [tok(1536)])
    shp = lambda *s: jax.ShapeDtypeStruct(s, F32)
    out_shape = [shp(bsz, t, 384 + NRW), shp(256, 1024), shp(128, 1024), shp(256, 512), shp(256, 512),
                 shp(1, 256), shp(1, 128), shp(1, NRW), shp(1, 512), shp(1, 512), shp(1, 512), shp(1, 512)]
    out_specs = [tok(384 + NRW)] + [_resident(s.shape) for s in out_shape[1:]]
    return pl.pallas_call(
        body, name="pre_bwd_a", grid=(bsz, nt), out_shape=out_shape, in_specs=in_specs, out_specs=out_specs,
        scratch_shapes=[pltpu.VMEM((8, NRW), F32)],
        compiler_params=_cparams(("arbitrary", "arbitrary")),
    )(*ins)


def _pre_bwd_b(x, dh, dz, da, mu, wpt, gpre):
    bsz, t, _ = x.shape
    nt = t // TT
    nblk = t // 8

    def body(x_ref, dh_ref, dz_ref, da_ref, nxt_ref, mu_ref, wpt_ref, gpre_ref, gx_ref, dp_ref, dgpre_ref):
        i = pl.program_id(1)
        first = (pl.program_id(0) == 0) & (i == 0)

        @pl.when(first)
        def _():
            dgpre_ref[...] = jnp.zeros(dgpre_ref.shape, F32)

        mu_v = mu_ref[...]
        dps = da_ref[0, :, 384:384 + NRW]
        nxt = jnp.where(i < nt - 1, nxt_ref[0, 0:1, 384:384 + NRW], 0.0)
        row = lax.broadcasted_iota(jnp.int32, dps.shape, 0)
        up = jnp.where(row == TT - 1, nxt, pltpu.roll(dps, TT - 1, 0))
        dprw = dps * (1.0 - mu_v) + up * mu_v
        dp = jnp.concatenate([dz_ref[0], da_ref[0, :, 0:384], dprw], axis=1).astype(BF16)
        dp_ref[0] = dp
        du = _dot(dp, wpt_ref[...])
        _, nx, rstd = _rms(x_ref[0], gpre_ref[...], D)
        dx, dg = _rms_bwd(du, nx, rstd, gpre_ref[...], D)
        dgpre_ref[...] += dg
        gx_ref[0] = dh_ref[0] + dx

    tok = lambda c: pl.BlockSpec((1, TT, c), lambda b, i: (b, i, 0))
    nxt_spec = pl.BlockSpec((1, 8, 384 + NRW), lambda b, i: (b, jnp.minimum((i + 1) * (TT // 8), nblk - 1), 0))
    ins = (x, dh, dz, da, da, mu, wpt, gpre)
    return pl.pallas_call(
        body, name="pre_bwd_b", grid=(bsz, nt),
        out_shape=[jax.ShapeDtypeStruct((bsz, t, D), F32), jax.ShapeDtypeStruct((bsz, t, DP), BF16),
                   jax.ShapeDtypeStruct((1, D), F32)],
        in_specs=[tok(D), tok(D), tok(1024), tok(384 + NRW), nxt_spec, _full(mu.shape), _full(wpt.shape),
                  _full(gpre.shape)],
        out_specs=[tok(D), tok(DP), _resident((1, D))],
        compiler_params=_cparams(("arbitrary", "arbitrary")),
    )(*ins)


def _tn_matmul(a, b, bn, name, bk=512):
    kdim, m = a.shape
    _, n = b.shape
    nk = kdim // bk

    def body(a_ref, b_ref, o_ref):
        @pl.when(pl.program_id(1) == 0)
        def _():
            o_ref[...] = jnp.zeros(o_ref.shape, F32)

        o_ref[...] += _dot_tn(a_ref[...], b_ref[...])

    return pl.pallas_call(
        body, name=name, grid=(n // bn, nk),
        out_shape=jax.ShapeDtypeStruct((m, n), F32),
        in_specs=[pl.BlockSpec((bk, m), lambda j, kk: (kk, 0)), pl.BlockSpec((bk, bn), lambda j, kk: (kk, j))],
        out_specs=pl.BlockSpec((m, bn), lambda j, kk: (0, j)),
        compiler_params=_cparams(("parallel", "arbitrary")),
    )(a, b)


SHARDED = ("w_in", "mla_w_uq", "mla_w_ukv", "rw_w2", "rw_a2", "w_out")
SMALL = ("norm_pre_g", "mla_q_norm_g", "mla_kv_norm_g", "rw_mu", "rw_w0", "rw_a0", "rw_k_k", "rw_k_a", "rw_r_k",
         "rw_ln_g", "rw_ln_b", "norm_post_g")
WEIGHTS = ("norm_pre_g", "w_in", "mla_q_norm_g", "mla_w_uq", "mla_kv_norm_g", "mla_w_ukv", "rw_mu", "rw_w0", "rw_w2",
           "rw_a0", "rw_a2", "rw_k_k", "rw_k_a", "rw_r_k", "rw_ln_g", "rw_ln_b", "w_out", "norm_post_g")


def _pack_small(d):
    flat = jnp.concatenate([d[n].reshape(1, -1) for n in SMALL], axis=1)
    return jnp.pad(flat, ((0, 0), (0, SMALL_ROWS * LANES - flat.shape[1]))).reshape(SMALL_ROWS, LANES)


def _unpack_small(packed, like):
    flat = packed.reshape(1, -1)
    out, at = {}, 0
    for n in SMALL:
        size = int(np.prod(like[n].shape))
        out[n] = flat[:, at:at + size].reshape(like[n].shape)
        at += size
    return out


def _pack_shard(d):
    return jnp.concatenate([d[n].reshape(-1, LANES) for n in SHARDED], axis=0)


def _unpack_shard(packed, like):
    out, at = {}, 0
    for n, rows in zip(SHARDED, PACK_ROWS):
        out[n] = packed[at:at + rows].reshape(like[n].shape)
        at += rows
    return out


def _constants():
    bo = np.kron(np.eye(2, dtype=np.float32), np.ones((64, 64), np.float32))
    inv = ROPE_THETA ** (-np.arange(0, 64, 2, dtype=np.float32) / 64)
    invf = np.concatenate([inv, inv, np.zeros(64, np.float32)]).astype(np.float32)[None, :]
    return jnp.asarray(bo, BF16), jnp.asarray(invf)


def kernel(x, positions, norm_pre_g, w_in, mla_q_norm_g, mla_w_uq, mla_kv_norm_g, mla_w_ukv, rw_mu, rw_w0, rw_w2, rw_a0, rw_a2, rw_k_k, rw_k_a, rw_r_k, rw_ln_g, rw_ln_b, w_out, norm_post_g, loss_target, m_norm_pre_g, m_w_in, m_mla_q_norm_g, m_mla_w_uq, m_mla_kv_norm_g, m_mla_w_ukv, m_rw_mu, m_rw_w0, m_rw_w2, m_rw_a0, m_rw_a2, m_rw_k_k, m_rw_k_a, m_rw_r_k, m_rw_ln_g, m_rw_ln_b, m_w_out, m_norm_post_g, v_norm_pre_g, v_w_in, v_mla_q_norm_g, v_mla_w_uq, v_mla_kv_norm_g, v_mla_w_ukv, v_rw_mu, v_rw_w0, v_rw_w2, v_rw_a0, v_rw_a2, v_rw_k_k, v_rw_k_a, v_rw_r_k, v_rw_ln_g, v_rw_ln_b, v_w_out, v_norm_post_g):
    wts = dict(norm_pre_g=norm_pre_g, w_in=w_in, mla_q_norm_g=mla_q_norm_g, mla_w_uq=mla_w_uq,
               mla_kv_norm_g=mla_kv_norm_g, mla_w_ukv=mla_w_ukv, rw_mu=rw_mu, rw_w0=rw_w0, rw_w2=rw_w2, rw_a0=rw_a0,
               rw_a2=rw_a2, rw_k_k=rw_k_k, rw_k_a=rw_k_a, rw_r_k=rw_r_k, rw_ln_g=rw_ln_g, rw_ln_b=rw_ln_b, w_out=w_out,
               norm_post_g=norm_post_g)
    mom_m = dict(norm_pre_g=m_norm_pre_g, w_in=m_w_in, mla_q_norm_g=m_mla_q_norm_g, mla_w_uq=m_mla_w_uq,
                 mla_kv_norm_g=m_mla_kv_norm_g, mla_w_ukv=m_mla_w_ukv, rw_mu=m_rw_mu, rw_w0=m_rw_w0, rw_w2=m_rw_w2,
                 rw_a0=m_rw_a0, rw_a2=m_rw_a2, rw_k_k=m_rw_k_k, rw_k_a=m_rw_k_a, rw_r_k=m_rw_r_k, rw_ln_g=m_rw_ln_g,
                 rw_ln_b=m_rw_ln_b, w_out=m_w_out, norm_post_g=m_norm_post_g)
    mom_v = dict(norm_pre_g=v_norm_pre_g, w_in=v_w_in, mla_q_norm_g=v_mla_q_norm_g, mla_w_uq=v_mla_w_uq,
                 mla_kv_norm_g=v_mla_kv_norm_g, mla_w_ukv=v_mla_w_ukv, rw_mu=v_rw_mu, rw_w0=v_rw_w0, rw_w2=v_rw_w2,
                 rw_a0=v_rw_a0, rw_a2=v_rw_a2, rw_k_k=v_rw_k_k, rw_k_a=v_rw_k_a, rw_r_k=v_rw_r_k, rw_ln_g=v_rw_ln_g,
                 rw_ln_b=v_rw_ln_b, w_out=v_w_out, norm_post_g=v_norm_post_g)
    bsz, t, _ = x.shape
    bo, invf = _constants()
    c_idx = lax.axis_index("c")
    shard_idx = 2 * lax.axis_index("x") + lax.axis_index("y")

    g_in, g_uq, g_ukv, g_w2, g_a2, g_out = _ag_weights([wts[n][0] for n in SHARDED])
    w_in_f = jnp.transpose(g_in, (1, 0, 2)).reshape(D, D_IN)
    wp = jnp.concatenate([w_in_f[:, 2112:3136], w_in_f[:, 0:384], w_in_f[:, 448:1984], w_in_f[:, 384:448],
                          w_in_f[:, 1984:2112], jnp.zeros((D, 64), BF16)], axis=1)
    wuq = jnp.pad(jnp.transpose(g_uq, (1, 0, 2)).reshape(256, HEADS, 192), ((0, 0), (0, 0), (0, 64))).reshape(256, 1024)
    wukv = jnp.transpose(jnp.transpose(g_ukv, (1, 0, 2)).reshape(128, HEADS, 2, 128), (0, 2, 1, 3)).reshape(128, 1024)
    w2 = jnp.transpose(g_w2, (1, 0, 2)).reshape(64, RW)
    a2 = jnp.transpose(g_a2, (1, 0, 2)).reshape(64, RW)
    w2p = jnp.pad(w2, ((64, 128), (0, 0)))
    a2p = jnp.pad(a2, ((128, 64), (0, 0)))
    wo = g_out.reshape(D, D)
    mu = jnp.concatenate([rw_mu[:, 0:1536], jnp.zeros((1, 64), F32), rw_mu[:, 1536:1664], jnp.zeros((1, 64), F32)],
                         axis=1)
    r_k = rw_r_k.reshape(1, RW)
    pos = positions.astype(F32)[:, :, None]

    (u, pp, q_att, k_att, v_att, r, w, kp, v, al, be) = _pre_fwd(
        x, pos, invf, norm_pre_g, wp, mla_q_norm_g, wuq, mla_kv_norm_g, wukv, mu, rw_w0, w2p, rw_a0, a2p, rw_k_k,
        rw_k_a, bo)
    o, lse = _attn_fwd(q_att, k_att, v_att)
    rw_k = _spread_k([r, w, kp, al, be])
    v_v = _to_v(v)
    yw_v, states, u_v = _wkv_fwd(*rw_k, v_v)
    yw = _from_v(yw_v, bsz)

    (dh, dz, dym, dyw, dbon, loss_acc, d_wo, d_gpost, d_lng, d_lnb, d_rk) = _post(
        x, loss_target, pp, o, yw, r, kp, v, rw_ln_g, rw_ln_b, r_k, wo, wo.T, norm_post_g, bo)
    loss = lax.psum(loss_acc[0, 0], ("x", "y", "c"))

    d_k = _wkv_bwd(*rw_k, v_v, _to_v(dyw), states, u_v)
    dr_w, dw_w, dkp_w, dal_w, dbe_w = _gather_k(d_k[:5], bsz)
    dwkv = (dr_w, dw_w, dkp_w, _from_v(d_k[5], bsz), dal_w, dbe_w)
    dq, dk, dva = _attn_bwd(q_att, k_att, v_att, o, lse, dym)

    (da, d_wuq, d_wukv, d_w2p, d_a2p, d_gq, d_gkv, d_mu, d_w0, d_a0, d_kk, d_ka) = _pre_bwd_a(
        pp, pos, invf, (mla_q_norm_g, wuq.T, mla_kv_norm_g, wukv.T), mu, rw_w0, w2p, w2p.T, rw_a0, a2p, a2p.T,
        rw_k_k, rw_k_a, bo, dq, dk, dva, dwkv, dbon)
    grad_x, dpb, d_gpre = _pre_bwd_b(x, dh, dz, da, mu, wp.T, norm_pre_g)
    d_wp = _tn_matmul(u.reshape(bsz * t, D), dpb.reshape(bsz * t, DP), 640, "dw_in")

    full_g = {
        "w_in": jnp.concatenate([d_wp[:, 1024:1408], d_wp[:, 2944:3008], d_wp[:, 1408:2944], d_wp[:, 3008:3136],
                                 d_wp[:, 0:1024]], axis=1),
        "mla_w_uq": d_wuq.reshape(256, HEADS, 256)[:, :, :192].reshape(256, 768),
        "mla_w_ukv": jnp.transpose(d_wukv.reshape(128, 2, HEADS, 128), (0, 2, 1, 3)).reshape(128, 1024),
        "rw_w2": d_w2p[64:128],
        "rw_a2": d_a2p[128:192],
        "w_out": d_wo,
    }
    small_g = {
        "norm_pre_g": d_gpre, "mla_q_norm_g": d_gq, "mla_kv_norm_g": d_gkv,
        "rw_mu": jnp.concatenate([d_mu[:, 0:1536], d_mu[:, 1600:1728]], axis=1),
        "rw_w0": d_w0, "rw_a0": d_a0, "rw_k_k": d_kk, "rw_k_a": d_ka, "rw_r_k": d_rk, "rw_ln_g": d_lng,
        "rw_ln_b": d_lnb, "norm_post_g": d_gpost,
    }

    def by_shard(name, g):
        if name == "w_out":
            return g.reshape(N_SHARD, -1, LANES)
        rows, cols = g.shape
        return jnp.transpose(g.reshape(rows, N_SHARD, cols // N_SHARD), (1, 0, 2)).reshape(N_SHARD, -1, LANES)

    packed = jnp.concatenate([by_shard(n, full_g[n]) for n in SHARDED], axis=1)
    halves = packed.reshape(N_SHARD, 2, HALF, LANES)
    keep = lax.dynamic_index_in_dim(halves, c_idx, 1, keepdims=False)
    give = lax.dynamic_index_in_dim(halves, 1 - c_idx, 1, keepdims=False)
    got = _rs_pair_exchange(give)
    pair_sum, pair_sum_b = _add_n([keep.reshape(-1, LANES), got.reshape(-1, LANES)], "rs_pair_sum", SUM_ROWS,
                                  also_bf16=True)
    arrived = _rs_chip_exchange(pair_sum_b.reshape(N_SHARD, HALF, LANES))
    own = lax.dynamic_index_in_dim(pair_sum.reshape(N_SHARD, HALF, LANES), shard_idx, 0, keepdims=False)
    (reduced_half,) = _add_n([own, arrived[0], arrived[1], arrived[2]], "rs_chip_sum", SUM_ROWS)
    g_shard = _rs_pair_gather(reduced_half).reshape(PACK_TOTAL, LANES)

    g_small = _small_allreduce(_pack_small(small_g))

    shard_like = {n: wts[n][0] for n in SHARDED}
    d_sh, nm_sh, nv_sh = _adamw(_pack_shard({n: wts[n][0] for n in SHARDED}), g_shard,
                                _pack_shard({n: mom_m[n][0] for n in SHARDED}),
                                _pack_shard({n: mom_v[n][0] for n in SHARDED}), "adamw_sharded", 568)
    d_sm, nm_sm, nv_sm = _adamw(_pack_small(wts), g_small, _pack_small(mom_m), _pack_small(mom_v), "adamw_small",
                                SMALL_ROWS)

    def unpack(sh, sm):
        out = {n: a[None] for n, a in _unpack_shard(sh, shard_like).items()}
        out.update(_unpack_small(sm, wts))
        return out

    grads, deltas, new_m, new_v = unpack(g_shard, g_small), unpack(d_sh, d_sm), unpack(nm_sh, nm_sm), unpack(nv_sh, nv_sm)
    return (loss, grad_x, *[grads[n] for n in WEIGHTS], *[deltas[n] for n in WEIGHTS],
            *[new_m[n] for n in WEIGHTS], *[new_v[n] for n in WEIGHTS])
```

```python
import functools

import numpy as np
import jax
import jax.numpy as jnp
from jax import lax
from jax.experimental import pallas as pl
from jax.experimental.pallas import tpu as pltpu

F32, BF16 = jnp.float32, jnp.bfloat16
MESH = pl.DeviceIdType.MESH

D = 1024
HEADS = 4
RW = 512
NORM_EPS = 1e-6
GN_EPS = 64e-5
ROPE_THETA = 10000.0
SCALE = (128 + 64) ** -0.5
D_IN = 3136
LR, B1, B2, ADAM_EPS, WD, STEP = 0.001, 0.9, 0.999, 1e-08, 0.01, 10

Z0, CQ0, CKV0, RW0, DP = 0, 1024, 1280, 1408, 3200
NRW = DP - RW0

LANES = 128
SUBLANES = 8
VMEM_LIMIT = 56 * 1024 * 1024

TT = 512
TT_VPU = 256
TQ = 512

N_SHARD = 4
PACK_ROWS = (1024 * 784 // 128, 256 * 192 // 128, 128 * 256 // 128, 64, 64, 256 * 1024 // 128)
PACK_TOTAL = sum(PACK_ROWS)
HALF = PACK_TOTAL // 2
SUM_ROWS = HALF // 4
SMALL_ROWS = 64


def _cparams(sem=None):
    return pltpu.CompilerParams(dimension_semantics=sem, vmem_limit_bytes=VMEM_LIMIT)


def _full(shape):
    n = len(shape)
    return pl.BlockSpec(shape, lambda *_: (0,) * n, pipeline_mode=pl.Buffered(1))


def _resident(shape):
    n = len(shape)
    return pl.BlockSpec(shape, lambda *_: (0,) * n)


def _dot(a, b):
    return jnp.dot(a, b, preferred_element_type=F32)


def _dot_nt(a, b):
    return lax.dot_general(a, b, (((1,), (1,)), ((), ())), preferred_element_type=F32)


def _dot_tn(a, b):
    return lax.dot_general(a, b, (((0,), (0,)), ((), ())), preferred_element_type=F32)


def _split3(x):
    hi = x.astype(BF16)
    r1 = x - hi.astype(F32)
    mid = r1.astype(BF16)
    lo = (r1 - mid.astype(F32)).astype(BF16)
    return hi, mid, lo


def _seg(x, bo):
    rows, nblk = x.shape[0], x.shape[1] // LANES
    pieces = [p for i in range(nblk) for p in _split3(x[:, LANES * i:LANES * (i + 1)])]
    res = _dot(jnp.concatenate(pieces, axis=0), bo)
    parts = [res[(3 * i) * rows:(3 * i + 1) * rows] + res[(3 * i + 1) * rows:(3 * i + 2) * rows]
             + res[(3 * i + 2) * rows:(3 * i + 3) * rows] for i in range(nblk)]
    return parts[0] if nblk == 1 else jnp.concatenate(parts, axis=1)


def _rms(x, g, n):
    rstd = lax.rsqrt(jnp.sum(x * x, axis=-1, keepdims=True) * (1.0 / n) + NORM_EPS)
    nx = x * rstd
    return nx * g, nx, rstd


def _rms_bwd(dy, nx, rstd, g, n):
    dn = dy * g
    dx = rstd * (dn - nx * (jnp.sum(dn * nx, axis=-1, keepdims=True) * (1.0 / n)))
    return dx, jnp.sum(dy * nx, axis=0, keepdims=True)


def _rot(x):
    lane = lax.broadcasted_iota(jnp.int32, x.shape, 1)
    return jnp.where((lane % 64) < 32, -pltpu.roll(x, x.shape[1] - 32, 1), pltpu.roll(x, 32, 1))


def _sigmoid(x):
    return 1.0 / (1.0 + jnp.exp(-x))


def _softplus(x):
    return jnp.maximum(x, 0.0) + jnp.log(1.0 + jnp.exp(-jnp.abs(x)))


def _rw_gates(ps, w0, w2p, a0, a2p, k_k, k_a, bo):
    r, k, v, misc = ps[:, 0:512], ps[:, 512:1024], ps[:, 1024:1536], ps[:, 1536:NRW]
    th = jnp.tanh(misc)
    wpre = w0 + _dot(th.astype(BF16), w2p)
    e = jnp.exp(-_softplus(-wpre) - 0.5)
    w = jnp.exp(-e)
    a = _sigmoid(a0 + _dot(misc.astype(BF16), a2p))
    m = k * k_k
    nrm = jnp.maximum(jnp.sqrt(_seg(m * m, bo)), 1e-12)
    kk = m / nrm
    kp = k * (1.0 + (a - 1.0) * k_a)
    return dict(r=r, k=k, v=v, misc=misc, th=th, wpre=wpre, e=e, w=w, a=a, nrm=nrm, kk=kk, kp=kp)


def _shift_mix(prw, prev_row, mu):
    row = lax.broadcasted_iota(jnp.int32, prw.shape, 0)
    sh = jnp.where(row == 0, prev_row, pltpu.roll(prw, 1, 0))
    return prw + (sh - prw) * mu, sh


def _ag_weights(shards):
    n = len(shards)

    def body(*refs):
        ins, outs = refs[:n], refs[n:2 * n]
        ici_send, ici_recv, d2d_send, d2d_recv = refs[2 * n:2 * n + 4]
        x, y, c = lax.axis_index("x"), lax.axis_index("y"), lax.axis_index("c")
        mine = 2 * x + y
        for w in range(n):
            outs[w][mine] = ins[w][...].astype(BF16)
        flips = ((1, 0), (0, 1), (1, 1))

        def half(w, shard, cc):
            rows = outs[w].shape[1] // 2
            return outs[w].at[shard, pl.ds(pl.multiple_of(cc * rows, 16), rows)]

        def ici(w, k, shard):
            fx, fy = flips[k]
            return pltpu.make_async_remote_copy(
                src_ref=half(w, shard, c), dst_ref=half(w, shard, c),
                send_sem=ici_send.at[w * 3 + k], recv_sem=ici_recv.at[w * 3 + k],
                device_id=(x ^ fx, y ^ fy, c), device_id_type=MESH)

        def d2d(w, k, cc):
            fx, fy = flips[k]
            theirs = 2 * (x ^ fx) + (y ^ fy)
            return pltpu.make_async_remote_copy(
                src_ref=half(w, theirs, cc), dst_ref=half(w, theirs, cc),
                send_sem=d2d_send.at[w * 3 + k], recv_sem=d2d_recv.at[w * 3 + k],
                device_id=(x, y, 1 - c), device_id_type=MESH)

        for w in range(n):
            for k in range(3):
                ici(w, k, mine).start()
        for w in range(n):
            for k in range(3):
                fx, fy = flips[k]
                ici(w, k, 2 * (x ^ fx) + (y ^ fy)).wait_recv()
                d2d(w, k, c).start()
        for w in range(n):
            for k in range(3):
                d2d(w, k, 1 - c).wait_recv()
        for w in range(n):
            for k in range(3):
                ici(w, k, mine).wait_send()
                d2d(w, k, c).wait_send()

    vm = pl.BlockSpec(memory_space=pltpu.VMEM)
    return pl.pallas_call(
        body, name="ag_weights",
        out_shape=[jax.ShapeDtypeStruct((N_SHARD,) + s.shape, BF16) for s in shards],
        in_specs=[vm] * n, out_specs=[vm] * n,
        scratch_shapes=[pltpu.SemaphoreType.DMA((3 * n,))] * 4,
        compiler_params=pltpu.CompilerParams(vmem_limit_bytes=VMEM_LIMIT),
    )(*shards)


def _rs_pair_exchange(send_half):
    def body(src_ref, dst_ref, send_sem, recv_sem):
        x, y, c = lax.axis_index("x"), lax.axis_index("y"), lax.axis_index("c")
        cp = pltpu.make_async_remote_copy(src_ref=src_ref, dst_ref=dst_ref, send_sem=send_sem, recv_sem=recv_sem,
                                          device_id=(x, y, 1 - c), device_id_type=MESH)
        cp.start()
        cp.wait()

    hbm = pl.BlockSpec(memory_space=pl.ANY)
    return pl.pallas_call(
        body, name="rs_pair_exchange",
        out_shape=jax.ShapeDtypeStruct(send_half.shape, send_half.dtype),
        in_specs=[hbm], out_specs=hbm,
        scratch_shapes=[pltpu.SemaphoreType.DMA, pltpu.SemaphoreType.DMA],
    )(send_half)


def _rs_chips(part_f32, part_bf16):
    def body(own_ref, src_ref, out_ref, recv, ici_send, ici_recv, d2d_send, d2d_recv):
        x, y, c = lax.axis_index("x"), lax.axis_index("y"), lax.axis_index("c")
        mine = 2 * x + y
        flips = ((1, 0), (0, 1), (1, 1))
        cps = []
        for k, (fx, fy) in enumerate(flips):
            theirs = 2 * (x ^ fx) + (y ^ fy)
            cps.append(pltpu.make_async_remote_copy(
                src_ref=src_ref.at[theirs], dst_ref=recv.at[k],
                send_sem=ici_send.at[k], recv_sem=ici_recv.at[k],
                device_id=(x ^ fx, y ^ fy, c), device_id_type=MESH))
        for cp in cps:
            cp.start()
        acc = own_ref[mine]
        for k, cp in enumerate(cps):
            cp.wait_recv()
            acc = acc + recv[k].astype(F32)
        out_ref[c] = acc
        to_sibling = pltpu.make_async_remote_copy(
            src_ref=out_ref.at[c], dst_ref=out_ref.at[c], send_sem=d2d_send, recv_sem=d2d_recv,
            device_id=(x, y, 1 - c), device_id_type=MESH)
        to_sibling.start()
        pltpu.make_async_remote_copy(
            src_ref=out_ref.at[1 - c], dst_ref=out_ref.at[1 - c], send_sem=d2d_send, recv_sem=d2d_recv,
            device_id=(x, y, 1 - c), device_id_type=MESH).wait_recv()
        to_sibling.wait_send()
        for cp in cps:
            cp.wait_send()

    vm = pl.BlockSpec(memory_space=pltpu.VMEM)
    return pl.pallas_call(
        body, name="rs_chips",
        out_shape=jax.ShapeDtypeStruct((2,) + part_f32.shape[1:], F32),
        in_specs=[vm, vm], out_specs=vm,
        scratch_shapes=[pltpu.VMEM((3,) + part_bf16.shape[1:], BF16), pltpu.SemaphoreType.DMA((3,)),
                        pltpu.SemaphoreType.DMA((3,)), pltpu.SemaphoreType.DMA, pltpu.SemaphoreType.DMA],
        compiler_params=pltpu.CompilerParams(vmem_limit_bytes=VMEM_LIMIT),
    )(part_f32, part_bf16)


def _small_allreduce(vec):
    def body(in_ref, out_ref, recv, send_sems, recv_sems):
        x, y, c = lax.axis_index("x"), lax.axis_index("y"), lax.axis_index("c")
        me = 4 * x + 2 * y + c
        cps = []
        for k in range(1, 8):
            fx, fy, fc = (k >> 2) & 1, (k >> 1) & 1, k & 1
            cps.append(pltpu.make_async_remote_copy(
                src_ref=in_ref, dst_ref=recv.at[k - 1],
                send_sem=send_sems.at[k - 1], recv_sem=recv_sems.at[k - 1],
                device_id=(x ^ fx, y ^ fy, c ^ fc), device_id_type=MESH))
        for cp in cps:
            cp.start()
        for cp in cps:
            cp.wait()
        acc = jnp.zeros(in_ref.shape, F32)
        for j in range(8):
            slot = jnp.maximum((me ^ j) - 1, 0)
            acc = acc + jnp.where(me == j, in_ref[...], recv[slot])
        out_ref[...] = acc

    vm = pl.BlockSpec(memory_space=pltpu.VMEM)
    return pl.pallas_call(
        body, name="small_allreduce",
        out_shape=jax.ShapeDtypeStruct(vec.shape, F32),
        in_specs=[vm], out_specs=vm,
        scratch_shapes=[pltpu.VMEM((7,) + vec.shape, F32), pltpu.SemaphoreType.DMA((7,)),
                        pltpu.SemaphoreType.DMA((7,))],
    )(vec)


def _add_n(arrs, name, rows, also_bf16=False):
    n = len(arrs)
    r = arrs[0].shape[0]

    def body(*refs):
        acc = refs[0][...].astype(F32)
        for k in range(1, n):
            acc = acc + refs[k][...].astype(F32)
        refs[n][...] = acc
        if also_bf16:
            refs[n + 1][...] = acc.astype(BF16)

    spec = pl.BlockSpec((rows, LANES), lambda i: (i, 0))
    out_shape = [jax.ShapeDtypeStruct(arrs[0].shape, F32)]
    if also_bf16:
        out_shape.append(jax.ShapeDtypeStruct(arrs[0].shape, BF16))
    return pl.pallas_call(
        body, name=name, grid=(r // rows,),
        out_shape=out_shape,
        in_specs=[spec] * n, out_specs=[spec] * len(out_shape),
        compiler_params=_cparams(("parallel",)),
    )(*arrs)


def _adamw(w, g, m, v, name, rows):
    r = w.shape[0]

    def body(w_ref, g_ref, m_ref, v_ref, d_ref, nm_ref, nv_ref):
        gg = g_ref[...]
        nm = B1 * m_ref[...] + (1.0 - B1) * gg
        nv = B2 * v_ref[...] + (1.0 - B2) * (gg * gg)
        m_hat = nm / (1.0 - B1 ** STEP)
        v_hat = nv / (1.0 - B2 ** STEP)
        d_ref[...] = -LR * (m_hat / (jnp.sqrt(v_hat) + ADAM_EPS) + WD * w_ref[...])
        nm_ref[...] = nm
        nv_ref[...] = nv

    spec = pl.BlockSpec((rows, LANES), lambda i: (i, 0))
    sds = jax.ShapeDtypeStruct(w.shape, F32)
    return pl.pallas_call(
        body, name=name, grid=(r // rows,),
        out_shape=[sds, sds, sds],
        in_specs=[spec] * 4, out_specs=[spec] * 3,
        compiler_params=_cparams(("parallel",)),
    )(w, g, m, v)


def _pre_fwd(x, pos, invf, gpre, wp, gq, wuq, gkv, wukv, mu, w0, w2p, a0, a2p, k_k, k_a, bo):
    bsz, t, _ = x.shape
    nt = t // TT

    def body(x_ref, pos_ref, invf_ref, gpre_ref, wp_ref, gq_ref, wuq_ref, gkv_ref, wukv_ref, mu_ref, w0_ref,
             w2p_ref, a0_ref, a2p_ref, kk_ref, ka_ref, bo_ref,
             u_ref, pp_ref, q_ref, k_ref, v_ref, r_o, w_o, kp_o, vv_o, al_o, be_o, carry):
        i = pl.program_id(1)
        u, _, _ = _rms(x_ref[0], gpre_ref[...], D)
        ub = u.astype(BF16)
        u_ref[0] = ub
        p = _dot(ub, wp_ref[...])
        pp_ref[0] = p
        prw = p[:, RW0:DP]

        @pl.when(i == 0)
        def _():
            carry[...] = jnp.zeros(carry.shape, F32)

        ps, _ = _shift_mix(prw, carry[7:8, :], mu_ref[...])
        carry[...] = prw[TT - 8:TT, :]

        g = _rw_gates(ps, w0_ref[...], w2p_ref[...], a0_ref[...], a2p_ref[...], kk_ref[...], ka_ref[...],
                      bo_ref[...])
        r_o[0] = g["r"]
        w_o[0] = g["w"]
        kp_o[0] = g["kp"]
        vv_o[0] = g["v"]
        al_o[0] = -g["kk"]
        be_o[0] = g["kk"] * g["a"]

        cqn, _, _ = _rms(p[:, CQ0:CQ0 + 256], gq_ref[...], 256)
        q = _dot(cqn.astype(BF16), wuq_ref[...])
        ckvn, _, _ = _rms(p[:, CKV0:CKV0 + 128], gkv_ref[...], 128)
        kv = _dot(ckvn.astype(BF16), wukv_ref[...])
        ang = pos_ref[0] * invf_ref[...]
        cs, sn = jnp.cos(ang), jnp.sin(ang)
        lane = lax.broadcasted_iota(jnp.int32, cs.shape, 1)
        kr = ps[:, 1536:1536 + LANES]
        kr = jnp.where(lane < 64, kr * cs + _rot(kr) * sn, 0.0).astype(BF16)
        for h in range(HEADS):
            qr = q[:, 256 * h + 128:256 * h + 256]
            q_ref[0, :, 256 * h:256 * h + 128] = q[:, 256 * h:256 * h + 128].astype(BF16)
            q_ref[0, :, 256 * h + 128:256 * h + 256] = (qr * cs + _rot(qr) * sn).astype(BF16)
            k_ref[0, :, 256 * h:256 * h + 128] = kv[:, 128 * h:128 * h + 128].astype(BF16)
            k_ref[0, :, 256 * h + 128:256 * h + 256] = kr
        v_ref[0] = kv[:, 512:1024].astype(BF16)

    tok = lambda c: pl.BlockSpec((1, TT, c), lambda b, i: (b, i, 0))
    full = lambda a: _full(a.shape)
    ins = (x, pos, invf, gpre, wp, gq, wuq, gkv, wukv, mu, w0, w2p, a0, a2p, k_k, k_a, bo)
    in_specs = [tok(D), tok(1)] + [full(a) for a in ins[2:]]
    sd = lambda c, dt: jax.ShapeDtypeStruct((bsz, t, c), dt)
    out_shape = [sd(D, BF16), sd(DP, F32), sd(1024, BF16), sd(1024, BF16), sd(512, BF16)] + [sd(RW, F32)] * 6
    out_specs = [tok(D), tok(DP), tok(1024), tok(1024), tok(512)] + [tok(RW)] * 6
    return pl.pallas_call(
        body, name="pre_fwd", grid=(bsz, nt), out_shape=out_shape, in_specs=in_specs, out_specs=out_specs,
        scratch_shapes=[pltpu.VMEM((8, NRW), F32)],
        compiler_params=_cparams(("arbitrary", "arbitrary")),
    )(*ins)


def _attn_fwd(q, k, v):
    bsz, t, _ = q.shape
    nq = t // TQ

    def body(q_ref, k_ref, v_ref, o_ref, lse_ref):
        i = pl.program_id(2)
        qt = q_ref[0]

        def step(j, carry, diagonal):
            m, l, acc = carry
            at = pl.ds(pl.multiple_of(j * TQ, TQ), TQ)
            s = _dot_nt(qt, k_ref[0, at, :]) * SCALE
            if diagonal:
                s = jnp.where(lax.broadcasted_iota(jnp.int32, (TQ, TQ), 1)
                              <= lax.broadcasted_iota(jnp.int32, (TQ, TQ), 0), s, -1e30)
            mn = jnp.maximum(m, jnp.max(s, axis=1, keepdims=True))
            p = jnp.exp(s - mn)
            al = jnp.exp(m - mn)
            l = al * l + jnp.sum(p, axis=1, keepdims=True)
            acc = al * acc + _dot(p.astype(BF16), v_ref[0, at, :])
            return mn, l, acc

        before = lax.fori_loop(
            0, i, lambda j, carry: step(j, carry, False),
            (jnp.full((TQ, 1), -1e30, F32), jnp.zeros((TQ, 1), F32), jnp.zeros((TQ, LANES), F32)))
        m, l, acc = step(i, before, True)
        o_ref[0] = acc / l
        lse_ref[0, 0] = jnp.broadcast_to(m + jnp.log(l), (TQ, LANES))

    return pl.pallas_call(
        body, name="attn_fwd", grid=(bsz, HEADS, nq),
        out_shape=[jax.ShapeDtypeStruct((bsz, t, 512), F32), jax.ShapeDtypeStruct((bsz, HEADS, t, LANES), F32)],
        in_specs=[pl.BlockSpec((1, TQ, 256), lambda b, h, i: (b, i, h)),
                  pl.BlockSpec((1, t, 256), lambda b, h, i: (b, 0, h)),
                  pl.BlockSpec((1, t, LANES), lambda b, h, i: (b, 0, h))],
        out_specs=[pl.BlockSpec((1, TQ, LANES), lambda b, h, i: (b, i, h)),
                   pl.BlockSpec((1, 1, TQ, LANES), lambda b, h, i: (b, h, i, 0))],
        compiler_params=_cparams(("parallel", "parallel", "arbitrary")),
    )(q, k, v)


def _attn_bwd(q, k, v, o, lse, do):
    bsz, t, _ = q.shape
    nq = t // TQ

    def body(q_ref, k_ref, v_ref, o_ref, lse_ref, do_ref, dq_ref, dk_ref, dv_ref, dl_ref):
        def prep(i, _):
            at = pl.ds(pl.multiple_of(i * TQ, TQ), TQ)
            dl_ref[at, :] = jnp.broadcast_to(jnp.sum(do_ref[0, at, :] * o_ref[0, at, :], axis=1, keepdims=True),
                                             (TQ, LANES))
            return 0

        lax.fori_loop(0, nq, prep, 0)
        dq_ref[0] = jnp.zeros((t, 256), F32)

        def kv_tile(j, _):
            atk = pl.ds(pl.multiple_of(j * TQ, TQ), TQ)
            kt = k_ref[0, atk, :]
            vt = v_ref[0, atk, :]

            def q_tile(i, carry, diagonal):
                dk, dv = carry
                atq = pl.ds(pl.multiple_of(i * TQ, TQ), TQ)
                qt = q_ref[0, atq, :]
                dob = do_ref[0, atq, :].astype(BF16)
                s = _dot_nt(qt, kt) * SCALE
                if diagonal:
                    s = jnp.where(lax.broadcasted_iota(jnp.int32, (TQ, TQ), 1)
                                  <= lax.broadcasted_iota(jnp.int32, (TQ, TQ), 0), s, -1e30)
                p = jnp.exp(s - lse_ref[0, 0, atq, :][:, 0:1])
                dv = dv + _dot_tn(p.astype(BF16), dob)
                dp = _dot_nt(dob, vt)
                ds = (p * (dp - dl_ref[atq, :][:, 0:1]) * SCALE).astype(BF16)
                dk = dk + _dot_tn(ds, qt)
                dq_ref[0, atq, :] += _dot(ds, kt)
                return dk, dv

            first = q_tile(j, (jnp.zeros((TQ, 256), F32), jnp.zeros((TQ, LANES), F32)), True)
            dk, dv = lax.fori_loop(j + 1, nq, lambda i, carry: q_tile(i, carry, False), first)
            dk_ref[0, atk, :] = dk
            dv_ref[0, atk, :] = dv
            return 0

        lax.fori_loop(0, nq, kv_tile, 0)

    s256 = pl.BlockSpec((1, t, 256), lambda b, h: (b, 0, h))
    s128 = pl.BlockSpec((1, t, LANES), lambda b, h: (b, 0, h))
    return pl.pallas_call(
        body, name="attn_bwd", grid=(bsz, HEADS),
        out_shape=[jax.ShapeDtypeStruct((bsz, t, 1024), F32), jax.ShapeDtypeStruct((bsz, t, 1024), F32),
                   jax.ShapeDtypeStruct((bsz, t, 512), F32)],
        in_specs=[s256, s256, s128, s128, pl.BlockSpec((1, 1, t, LANES), lambda b, h: (b, h, 0, 0)), s128],
        out_specs=[s256, s256, s128],
        scratch_shapes=[pltpu.VMEM((t, LANES), F32)],
        compiler_params=_cparams(("parallel", "parallel")),
    )(q, k, v, o, lse, do)


RW_HEADS = 8
CH = 16


def _lane_split(bsz):
    vs = LANES // (bsz * RW_HEADS)
    return vs, 64 // vs


def _gather_matrix(bsz):
    group = bsz * RW_HEADS
    vs = LANES // group
    half = (RW_HEADS // 2) * bsz * SPREAD_STEPS
    p = np.zeros((SPREAD_STEPS // vs * LANES, 2 * half), np.float32)
    for g2 in range(SPREAD_STEPS // vs):
        for j in range(vs):
            for b in range(bsz):
                for h in range(RW_HEADS):
                    hp, hpar = h // 2, h % 2
                    p[g2 * LANES + j * group + b * RW_HEADS + h,
                      hpar * half + (hp * bsz + b) * SPREAD_STEPS + g2 * vs + j] = 1.0
    return jnp.asarray(np.concatenate([p] * 3, axis=0), BF16)


def _gather_k(ys, bsz):
    vs = LANES // (bsz * RW_HEADS)
    assert (RW_HEADS // 2) * bsz * SPREAD_STEPS == LANES, "the transposed tile must be 128 lanes wide"
    tg = ys[0].shape[0]
    n = len(ys)
    ngrp = SPREAD_BLOCK // SPREAD_STEPS
    per = SPREAD_STEPS // vs

    def body(*refs):
        pm = refs[n][...]
        for y_ref, o_ref in zip(refs[:n], refs[n + 1:]):
            lhs = jnp.concatenate(
                [jnp.concatenate(_split3(jnp.concatenate([y_ref[per * m + g2] for g2 in range(per)], axis=1)), axis=1)
                 for m in range(ngrp)], axis=0)
            a = _dot(lhs, pm)
            for m in range(ngrp):
                am = a[64 * m:64 * (m + 1)]
                bt = jnp.concatenate([am[:, 0:LANES], am[:, LANES:2 * LANES]], axis=0).T
                for hp in range(RW_HEADS // 2):
                    for b in range(bsz):
                        at = (hp * bsz + b) * SPREAD_STEPS
                        o_ref[b, SPREAD_STEPS * m:SPREAD_STEPS * (m + 1), LANES * hp:LANES * (hp + 1)] = \
                            bt[at:at + SPREAD_STEPS]

    pm = _gather_matrix(bsz)
    return pl.pallas_call(
        body, name="wkv_gather", grid=(tg * vs // SPREAD_BLOCK,),
        out_shape=[jax.ShapeDtypeStruct((bsz, tg * vs, RW), F32)] * n,
        in_specs=[pl.BlockSpec((SPREAD_BLOCK // vs, 64, LANES), lambda i: (i, 0, 0))] * n + [_full(pm.shape)],
        out_specs=[pl.BlockSpec((bsz, SPREAD_BLOCK, RW), lambda i: (0, i, 0))] * n,
        compiler_params=_cparams(("parallel",)),
    )(*ys, pm)


def _to_v(x):
    bsz, t, _ = x.shape
    vs, vq = _lane_split(bsz)
    return jnp.transpose(x.reshape(bsz, t, RW_HEADS, vq, vs), (1, 3, 4, 0, 2)).reshape(t, vq, LANES)


def _from_v(y, bsz):
    t = y.shape[0]
    vs, vq = _lane_split(bsz)
    return jnp.transpose(y.reshape(t, vq, vs, bsz, RW_HEADS), (3, 0, 4, 1, 2)).reshape(bsz, t, RW)


def _ksum(a):
    return jnp.sum(a, axis=0, keepdims=True)


def _fold(a, group):
    sh = LANES // 2
    while sh >= group:
        a = a + pltpu.roll(a, sh, 1)
        sh //= 2
    return a


def _lane_group(shape, group):
    return lax.broadcasted_iota(jnp.int32, shape, 1) // group


SPREAD_STEPS = 8
SPREAD_BLOCK = 32


def _spread_matrix(bsz):
    group = bsz * RW_HEADS
    vs = LANES // group
    rows = (RW_HEADS // 2) * bsz * SPREAD_STEPS
    q = np.zeros((2, rows, SPREAD_STEPS * LANES), np.float32)
    for hpar in range(2):
        for hp in range(RW_HEADS // 2):
            for b in range(bsz):
                for st in range(SPREAD_STEPS):
                    row = (hp * bsz + b) * SPREAD_STEPS + st
                    for s in range(vs):
                        q[hpar, row, st * LANES + s * group + b * RW_HEADS + 2 * hp + hpar] = 1.0
    return jnp.asarray(np.concatenate([q[0], q[1]] * 3, axis=0), BF16)


def _spread_k(xs):
    bsz, t, _ = xs[0].shape
    assert (RW_HEADS // 2) * bsz * SPREAD_STEPS == LANES, "the transposed tile must be 128 lanes wide"
    n = len(xs)
    ngrp = SPREAD_BLOCK // SPREAD_STEPS

    def body(*refs):
        qm = refs[n][...]
        for x_ref, o_ref in zip(refs[:n], refs[n + 1:]):
            cols = [[] for _ in range(6)]
            for m in range(ngrp):
                at = slice(SPREAD_STEPS * m, SPREAD_STEPS * (m + 1))
                x8 = jnp.concatenate([x_ref[b, at, LANES * hp:LANES * (hp + 1)]
                                      for hp in range(RW_HEADS // 2) for b in range(bsz)], axis=0)
                for pi, piece in enumerate(_split3(x8.T)):
                    cols[2 * pi].append(piece[0:64])
                    cols[2 * pi + 1].append(piece[64:128])
            lhs = jnp.concatenate([jnp.concatenate(c, axis=0) for c in cols], axis=1)
            y = _dot(lhs, qm)
            for m in range(ngrp):
                for st in range(SPREAD_STEPS):
                    o_ref[SPREAD_STEPS * m + st] = y[64 * m:64 * (m + 1), LANES * st:LANES * (st + 1)]

    qm = _spread_matrix(bsz)
    return pl.pallas_call(
        body, name="wkv_spread", grid=(t // SPREAD_BLOCK,),
        out_shape=[jax.ShapeDtypeStruct((t, 64, LANES), F32)] * n,
        in_specs=[pl.BlockSpec((bsz, SPREAD_BLOCK, RW), lambda i: (0, i, 0))] * n + [_full(qm.shape)],
        out_specs=[pl.BlockSpec((SPREAD_BLOCK, 64, LANES), lambda i: (i, 0, 0))] * n,
        compiler_params=_cparams(("parallel",)),
    )(*xs, qm)


def _wkv_fwd(r, w, kp, al, be, v):
    t, vq = v.shape[0], v.shape[1]

    def body(r_ref, w_ref, kp_ref, al_ref, be_ref, v_ref, y_ref, a_ref, u_ref, st_ref):
        @pl.when(pl.program_id(0) == 0)
        def _():
            st_ref[...] = jnp.zeros(st_ref.shape, F32)

        def step(tl, _):
            rv, wv, kv, av, bv = r_ref[tl], w_ref[tl], kp_ref[tl], al_ref[tl], be_ref[tl]
            vals = v_ref[tl]
            yrows, urows = [], []
            for q in range(vq):
                s = st_ref[q]
                u = _ksum(s * av)
                s = s * wv + bv * u + kv * vals[q:q + 1]
                st_ref[q] = s
                a_ref[tl, q] = s
                urows.append(u)
                yrows.append(_ksum(s * rv))
            y_ref[tl] = jnp.concatenate(yrows, axis=0)
            u_ref[tl] = jnp.concatenate(urows, axis=0)
            return 0

        lax.fori_loop(0, CH, step, 0)

    kspec = pl.BlockSpec((CH, 64, LANES), lambda i: (i, 0, 0))
    vspec = pl.BlockSpec((CH, vq, LANES), lambda i: (i, 0, 0))
    vsd = jax.ShapeDtypeStruct((t, vq, LANES), F32)
    return pl.pallas_call(
        body, name="wkv_fwd", grid=(t // CH,),
        out_shape=[vsd, jax.ShapeDtypeStruct((t, vq, 64, LANES), F32), vsd],
        in_specs=[kspec] * 5 + [vspec],
        out_specs=[vspec, pl.BlockSpec((CH, vq, 64, LANES), lambda i: (i, 0, 0, 0)), vspec],
        scratch_shapes=[pltpu.VMEM((vq, 64, LANES), F32)],
        compiler_params=_cparams(("arbitrary",)),
    )(r, w, kp, al, be, v)


def _wkv_bwd(r, w, kp, al, be, v, dy, states, u):
    t, vq = v.shape[0], v.shape[1]
    vs = 64 // vq
    group = LANES // vs
    n = t // CH
    ng = CH // vs

    def body(r_ref, w_ref, kp_ref, al_ref, be_ref, v_ref, dy_ref, u_ref, a_ref, ap_ref,
             dr_ref, dw_ref, dkp_ref, dal_ref, dbe_ref, dv_ref, ds_ref):
        @pl.when(pl.program_id(0) == 0)
        def _():
            ds_ref[...] = jnp.zeros(ds_ref.shape, F32)

        earliest = pl.program_id(0) == n - 1

        def reverse(i, _):
            g = ng - 1 - i
            grp = _lane_group((64, LANES), group)
            outs = None
            for j in reversed(range(vs)):
                tl = g * vs + j
                rv, wv, kv, av, bv = r_ref[tl], w_ref[tl], kp_ref[tl], al_ref[tl], be_ref[tl]
                vals, dys, us = v_ref[tl], dy_ref[tl], u_ref[tl]
                acc = None
                dvrows = []
                for q in range(vq):
                    if j > 0:
                        s_prev = a_ref[tl - 1, q]
                    else:
                        before = jnp.where(earliest, 0.0, ap_ref[0, q])
                        s_prev = jnp.where(g == 0, before, a_ref[jnp.maximum(tl - 1, 0), q])
                    dyq = dys[q:q + 1]
                    ds = ds_ref[q] + rv * dyq
                    c = _ksum(ds * bv)
                    dvrows.append(_ksum(ds * kv))
                    terms = (a_ref[tl, q] * dyq, ds * s_prev, ds * vals[q:q + 1], s_prev * c, ds * us[q:q + 1])
                    acc = terms if acc is None else tuple(a + b for a, b in zip(acc, terms))
                    ds_ref[q] = ds * wv + av * c
                dv_ref[tl] = jnp.concatenate(dvrows, axis=0)
                summed = [_fold(a, group) for a in acc]
                outs = summed if outs is None else [jnp.where(grp == j, f, o) for f, o in zip(summed, outs)]
            for ref, o in zip((dr_ref, dw_ref, dkp_ref, dal_ref, dbe_ref), outs):
                ref[g] = o
            return 0

        lax.fori_loop(0, ng, reverse, 0)

    kspec = pl.BlockSpec((CH, 64, LANES), lambda i: (n - 1 - i, 0, 0))
    gspec = pl.BlockSpec((ng, 64, LANES), lambda i: (n - 1 - i, 0, 0))
    vspec = pl.BlockSpec((CH, vq, LANES), lambda i: (n - 1 - i, 0, 0))
    ksd = jax.ShapeDtypeStruct((t // vs, 64, LANES), F32)
    return pl.pallas_call(
        body, name="wkv_bwd", grid=(n,),
        out_shape=[ksd] * 5 + [jax.ShapeDtypeStruct((t, vq, LANES), F32)],
        in_specs=[kspec] * 5 + [vspec, vspec, vspec,
                                pl.BlockSpec((CH, vq, 64, LANES), lambda i: (n - 1 - i, 0, 0, 0)),
                                pl.BlockSpec((1, vq, 64, LANES), lambda i: (jnp.maximum((n - 1 - i) * CH - 1, 0), 0, 0, 0))],
        out_specs=[gspec] * 5 + [vspec],
        scratch_shapes=[pltpu.VMEM((vq, 64, LANES), F32)],
        compiler_params=_cparams(("arbitrary",)),
    )(r, w, kp, al, be, v, dy, u, states, states)


def _post(x, tgt, pp, o, yw, r, kp, v, ln_g, ln_b, r_k, wo, wot, gpost, bo):
    bsz, t, _ = x.shape
    tt = TT_VPU
    nt = t // tt

    def body(x_ref, tgt_ref, z_ref, o_ref, yw_ref, r_ref, kp_ref, v_ref, lng_ref, lnb_ref, rk_ref, wo_ref, wot_ref,
             gpost_ref, bo_ref,
             dh_ref, dz_ref, dym_ref, dyw_ref, dbon_ref, loss_ref, dwo_ref, dgpost_ref, dlng_ref, dlnb_ref, drk_ref):
        first = (pl.program_id(0) == 0) & (pl.program_id(1) == 0)

        @pl.when(first)
        def _():
            for ref in (loss_ref, dwo_ref, dgpost_ref, dlng_ref, dlnb_ref, drk_ref):
                ref[...] = jnp.zeros(ref.shape, F32)

        bo_m = bo_ref[...]
        seg = lambda a: _seg(a, bo_m)
        rowsum = lambda a: jnp.sum(a, axis=0, keepdims=True)
        ywv, rv, kpv, vv = yw_ref[0], r_ref[0], kp_ref[0], v_ref[0]
        ln_g, r_k = lng_ref[...], rk_ref[...]
        mean = seg(ywv) * (1.0 / 64)
        yc = ywv - mean
        rstd = lax.rsqrt(seg(yc * yc) * (1.0 / 64) + GN_EPS)
        yhat = yc * rstd
        sb = seg(rv * kpv * r_k)
        y_rw = yhat * ln_g + lnb_ref[...] + sb * vv
        z = z_ref[0]
        sig = _sigmoid(z)
        sz = z * sig
        ycat = jnp.concatenate([o_ref[0], y_rw], axis=1)
        ycg = (ycat * sz).astype(BF16)
        out = _dot(ycg, wo_ref[...])
        hn, nx, rstd_o = _rms(out, gpost_ref[...], D)
        err = x_ref[0] + hn - tgt_ref[0]
        loss_ref[...] += jnp.sum(err * err) * (0.5 / D)
        dh = err * (1.0 / D)
        dh_ref[0] = dh
        dout, dgp = _rms_bwd(dh, nx, rstd_o, gpost_ref[...], D)
        dgpost_ref[...] += dgp
        doutb = dout.astype(BF16)
        dwo_ref[...] += _dot_tn(ycg, doutb)
        dycg = _dot(doutb, wot_ref[...])
        dz_ref[0] = dycg * ycat * (sig * (1.0 + z * (1.0 - sig)))
        dycat = dycg * sz
        dym_ref[0] = dycat[:, 0:512]
        dy_rw = dycat[:, 512:1024]
        dlnb_ref[...] += rowsum(dy_rw)
        dlng_ref[...] += rowsum(dy_rw * yhat)
        dyhat = dy_rw * ln_g
        dyw_ref[0] = rstd * (dyhat - seg(dyhat) * (1.0 / 64) - yhat * (seg(dyhat * yhat) * (1.0 / 64)))
        dsb = seg(dy_rw * vv)
        drk_ref[...] += rowsum(dsb * rv * kpv)
        dbon_ref[0, :, 0:512] = dsb * kpv * r_k
        dbon_ref[0, :, 512:1024] = dsb * rv * r_k
        dbon_ref[0, :, 1024:1536] = dy_rw * sb

    tok = lambda c: pl.BlockSpec((1, tt, c), lambda b, i: (b, i, 0))
    full = lambda a: _full(a.shape)
    ins = (x, tgt, pp, o, yw, r, kp, v, ln_g, ln_b, r_k, wo, wot, gpost, bo)
    in_specs = [tok(D), tok(D), tok(1024)] + [tok(512)] * 5 + [full(a) for a in ins[8:]]
    sd = lambda c: jax.ShapeDtypeStruct((bsz, t, c), F32)
    vec = lambda c: jax.ShapeDtypeStruct((1, c), F32)
    out_shape = [sd(D), sd(1024), sd(512), sd(512), sd(1536), jax.ShapeDtypeStruct((8, LANES), F32),
                 jax.ShapeDtypeStruct((1024, 1024), F32), vec(D), vec(512), vec(512), vec(512)]
    out_specs = [tok(D), tok(1024), tok(512), tok(512), tok(1536), _resident((8, LANES)), _resident((1024, 1024)),
                 _resident((1, D)), _resident((1, 512)), _resident((1, 512)), _resident((1, 512))]
    return pl.pallas_call(
        body, name="post", grid=(bsz, nt), out_shape=out_shape, in_specs=in_specs, out_specs=out_specs,
        compiler_params=_cparams(("arbitrary", "arbitrary")),
    )(*ins)


def _pre_bwd_a(pp, pos, invf, cqkv_w, mu, w0, w2p, w2pt, a0, a2p, a2pt, k_k, k_a, bo,
               dq, dk, dva, dwkv, dbon):
    gq, wuqt, gkv, wukvt = cqkv_w
    bsz, t, _ = pp.shape
    tt = TT_VPU
    nt = t // tt
    dr_w, dw_w, dkp_w, dv_w, dal_w, dbe_w = dwkv

    def body(pp_ref, pos_ref, invf_ref, gq_ref, wuqt_ref, gkv_ref, wukvt_ref, mu_ref, w0_ref, w2p_ref, w2pt_ref,
             a0_ref, a2p_ref, a2pt_ref, kk_ref, ka_ref, bo_ref, dq_ref, dk_ref, dva_ref,
             dr_ref, dw_ref, dkp_ref, dv_ref, dal_ref, dbe_ref, dbon_ref,
             da_ref, dwuq_ref, dwukv_ref, dw2p_ref, da2p_ref, dgq_ref, dgkv_ref, dmu_ref, dw0_ref, da0_ref,
             dkk_ref, dka_ref, carry):
        i = pl.program_id(1)
        first = (pl.program_id(0) == 0) & (i == 0)

        @pl.when(first)
        def _():
            for ref in (dwuq_ref, dwukv_ref, dw2p_ref, da2p_ref, dgq_ref, dgkv_ref, dmu_ref, dw0_ref, da0_ref,
                        dkk_ref, dka_ref):
                ref[...] = jnp.zeros(ref.shape, F32)

        bo_m = bo_ref[...]
        rowsum = lambda a: jnp.sum(a, axis=0, keepdims=True)
        prw = pp_ref[0, :, RW0:DP]

        @pl.when(i == 0)
        def _():
            carry[...] = jnp.zeros(carry.shape, F32)

        ps, sh = _shift_mix(prw, carry[7:8, :], mu_ref[...])
        carry[...] = prw[tt - 8:tt, :]
        k_k, k_a = kk_ref[...], ka_ref[...]
        g = _rw_gates(ps, w0_ref[...], w2p_ref[...], a0_ref[...], a2p_ref[...], k_k, k_a, bo_m)
        a, kk, k = g["a"], g["kk"], g["k"]
        dr = dr_ref[0] + dbon_ref[0, :, 0:512]
        dkp = dkp_ref[0] + dbon_ref[0, :, 512:1024]
        dv = dv_ref[0] + dbon_ref[0, :, 1024:1536]
        dbe = dbe_ref[0]
        dkk = dbe * a - dal_ref[0]
        da = dbe * kk + dkp * k * k_a
        dka_ref[...] += rowsum(dkp * k * (a - 1.0))
        dm = (dkk - kk * _seg(dkk * kk, bo_m)) / g["nrm"]
        dkk_ref[...] += rowsum(dm * k)
        dk_tot = dkp * (1.0 + (a - 1.0) * k_a) + dm * k_k
        dapre = da * a * (1.0 - a)
        da0_ref[...] += rowsum(dapre)
        dapb = dapre.astype(BF16)
        da2p_ref[...] += _dot_tn(g["misc"].astype(BF16), dapb)
        dwpre = dw_ref[0] * g["w"] * (-g["e"]) * _sigmoid(-g["wpre"])
        dw0_ref[...] += rowsum(dwpre)
        dwpb = dwpre.astype(BF16)
        th = g["th"]
        dw2p_ref[...] += _dot_tn(th.astype(BF16), dwpb)
        dmisc = _dot(dapb, a2pt_ref[...]) + _dot(dwpb, w2pt_ref[...]) * (1.0 - th * th)
        ang = pos_ref[0] * invf_ref[...]
        cs, sn = jnp.cos(ang), jnp.sin(ang)
        unrope = lambda gr: gr * cs - _rot(gr * sn)
        lane = lax.broadcasted_iota(jnp.int32, cs.shape, 1)
        dkr = dk_ref[0, :, 128:256]
        for h in range(1, HEADS):
            dkr = dkr + dk_ref[0, :, 256 * h + 128:256 * h + 256]
        dkr = jnp.where(lane < 64, unrope(dkr), 0.0)
        dmisc = dmisc + jnp.concatenate([dkr, jnp.zeros_like(dkr)], axis=1)
        dqp = jnp.concatenate(
            [blk for h in range(HEADS)
             for blk in (dq_ref[0, :, 256 * h:256 * h + 128], unrope(dq_ref[0, :, 256 * h + 128:256 * h + 256]))],
            axis=1).astype(BF16)
        dkvp = jnp.concatenate([dk_ref[0, :, 256 * h:256 * h + 128] for h in range(HEADS)] + [dva_ref[0]],
                               axis=1).astype(BF16)
        cqn, cq_nx, cq_rstd = _rms(pp_ref[0, :, CQ0:CQ0 + 256], gq_ref[...], 256)
        ckvn, ckv_nx, ckv_rstd = _rms(pp_ref[0, :, CKV0:CKV0 + 128], gkv_ref[...], 128)
        dwuq_ref[...] += _dot_tn(cqn.astype(BF16), dqp)
        dwukv_ref[...] += _dot_tn(ckvn.astype(BF16), dkvp)
        dcq, dgq = _rms_bwd(_dot(dqp, wuqt_ref[...]), cq_nx, cq_rstd, gq_ref[...], 256)
        dckv, dgkv = _rms_bwd(_dot(dkvp, wukvt_ref[...]), ckv_nx, ckv_rstd, gkv_ref[...], 128)
        dgq_ref[...] += dgq
        dgkv_ref[...] += dgkv
        dps = jnp.concatenate([dr, dk_tot, dv, dmisc], axis=1)
        dmu_ref[...] += rowsum(dps * (sh - prw))
        da_ref[0, :, 0:256] = dcq
        da_ref[0, :, 256:384] = dckv
        da_ref[0, :, 384:384 + NRW] = dps

    tok = lambda c: pl.BlockSpec((1, tt, c), lambda b, i: (b, i, 0))
    full = lambda a: _full(a.shape)
    ins = (pp, pos, invf, gq, wuqt, gkv, wukvt, mu, w0, w2p, w2pt, a0, a2p, a2pt, k_k, k_a, bo,
           dq, dk, dva, dr_w, dw_w, dkp_w, dv_w, dal_w, dbe_w, dbon)
    in_specs = ([tok(DP), tok(1)] + [full(a) for a in ins[2:17]] + [tok(1024), tok(1024), tok(512)]
                + [tok(512)] * 6 + [tok(1536)])
    shp = lambda *s: jax.ShapeDtypeStruct(s, F32)
    out_shape = [shp(bsz, t, 384 + NRW), shp(256, 1024), shp(128, 1024), shp(256, 512), shp(256, 512),
                 shp(1, 256), shp(1, 128), shp(1, NRW), shp(1, 512), shp(1, 512), shp(1, 512), shp(1, 512)]
    out_specs = [tok(384 + NRW)] + [_resident(s.shape) for s in out_shape[1:]]
    return pl.pallas_call(
        body, name="pre_bwd_a", grid=(bsz, nt), out_shape=out_shape, in_specs=in_specs, out_specs=out_specs,
        scratch_shapes=[pltpu.VMEM((8, NRW), F32)],
        compiler_params=_cparams(("arbitrary", "arbitrary")),
    )(*ins)


def _pre_bwd_b(x, dh, dz, da, mu, wpt, gpre):
    bsz, t, _ = x.shape
    nt = t // TT
    nblk = t // 8

    def body(x_ref, dh_ref, dz_ref, da_ref, nxt_ref, mu_ref, wpt_ref, gpre_ref, gx_ref, dp_ref, dgpre_ref):
        i = pl.program_id(1)
        first = (pl.program_id(0) == 0) & (i == 0)

        @pl.when(first)
        def _():
            dgpre_ref[...] = jnp.zeros(dgpre_ref.shape, F32)

        mu_v = mu_ref[...]
        dps = da_ref[0, :, 384:384 + NRW]
        nxt = jnp.where(i < nt - 1, nxt_ref[0, 0:1, 384:384 + NRW], 0.0)
        row = lax.broadcasted_iota(jnp.int32, dps.shape, 0)
        up = jnp.where(row == TT - 1, nxt, pltpu.roll(dps, TT - 1, 0))
        dprw = dps * (1.0 - mu_v) + up * mu_v
        dp = jnp.concatenate([dz_ref[0], da_ref[0, :, 0:384], dprw], axis=1).astype(BF16)
        dp_ref[0] = dp
        du = _dot(dp, wpt_ref[...])
        _, nx, rstd = _rms(x_ref[0], gpre_ref[...], D)
        dx, dg = _rms_bwd(du, nx, rstd, gpre_ref[...], D)
        dgpre_ref[...] += dg
        gx_ref[0] = dh_ref[0] + dx

    tok = lambda c: pl.BlockSpec((1, TT, c), lambda b, i: (b, i, 0))
    nxt_spec = pl.BlockSpec((1, 8, 384 + NRW), lambda b, i: (b, jnp.minimum((i + 1) * (TT // 8), nblk - 1), 0))
    ins = (x, dh, dz, da, da, mu, wpt, gpre)
    return pl.pallas_call(
        body, name="pre_bwd_b", grid=(bsz, nt),
        out_shape=[jax.ShapeDtypeStruct((bsz, t, D), F32), jax.ShapeDtypeStruct((bsz, t, DP), BF16),
                   jax.ShapeDtypeStruct((1, D), F32)],
        in_specs=[tok(D), tok(D), tok(1024), tok(384 + NRW), nxt_spec, _full(mu.shape), _full(wpt.shape),
                  _full(gpre.shape)],
        out_specs=[tok(D), tok(DP), _resident((1, D))],
        compiler_params=_cparams(("arbitrary", "arbitrary")),
    )(*ins)


def _tn_matmul(a, b, bn, name, bk=512):
    kdim, m = a.shape
    _, n = b.shape
    nk = kdim // bk

    def body(a_ref, b_ref, o_ref):
        @pl.when(pl.program_id(1) == 0)
        def _():
            o_ref[...] = jnp.zeros(o_ref.shape, F32)

        o_ref[...] += _dot_tn(a_ref[...], b_ref[...])

    return pl.pallas_call(
        body, name=name, grid=(n // bn, nk),
        out_shape=jax.ShapeDtypeStruct((m, n), F32),
        in_specs=[pl.BlockSpec((bk, m), lambda j, kk: (kk, 0)), pl.BlockSpec((bk, bn), lambda j, kk: (kk, j))],
        out_specs=pl.BlockSpec((m, bn), lambda j, kk: (0, j)),
        compiler_params=_cparams(("parallel", "arbitrary")),
    )(a, b)


SHARDED = ("w_in", "mla_w_uq", "mla_w_ukv", "rw_w2", "rw_a2", "w_out")
SMALL = ("norm_pre_g", "mla_q_norm_g", "mla_kv_norm_g", "rw_mu", "rw_w0", "rw_a0", "rw_k_k", "rw_k_a", "rw_r_k",
         "rw_ln_g", "rw_ln_b", "norm_post_g")
WEIGHTS = ("norm_pre_g", "w_in", "mla_q_norm_g", "mla_w_uq", "mla_kv_norm_g", "mla_w_ukv", "rw_mu", "rw_w0", "rw_w2",
           "rw_a0", "rw_a2", "rw_k_k", "rw_k_a", "rw_r_k", "rw_ln_g", "rw_ln_b", "w_out", "norm_post_g")


def _pack_small(d):
    flat = jnp.concatenate([d[n].reshape(1, -1) for n in SMALL], axis=1)
    return jnp.pad(flat, ((0, 0), (0, SMALL_ROWS * LANES - flat.shape[1]))).reshape(SMALL_ROWS, LANES)


def _unpack_small(packed, like):
    flat = packed.reshape(1, -1)
    out, at = {}, 0
    for n in SMALL:
        size = int(np.prod(like[n].shape))
        out[n] = flat[:, at:at + size].reshape(like[n].shape)
        at += size
    return out


def _pack_shard(d):
    return jnp.concatenate([d[n].reshape(-1, LANES) for n in SHARDED], axis=0)


def _unpack_shard(packed, like):
    out, at = {}, 0
    for n, rows in zip(SHARDED, PACK_ROWS):
        out[n] = packed[at:at + rows].reshape(like[n].shape)
        at += rows
    return out


def _constants():
    bo = np.kron(np.eye(2, dtype=np.float32), np.ones((64, 64), np.float32))
    inv = ROPE_THETA ** (-np.arange(0, 64, 2, dtype=np.float32) / 64)
    invf = np.concatenate([inv, inv, np.zeros(64, np.float32)]).astype(np.float32)[None, :]
    return jnp.asarray(bo, BF16), jnp.asarray(invf)


def kernel(x, positions, norm_pre_g, w_in, mla_q_norm_g, mla_w_uq, mla_kv_norm_g, mla_w_ukv, rw_mu, rw_w0, rw_w2, rw_a0, rw_a2, rw_k_k, rw_k_a, rw_r_k, rw_ln_g, rw_ln_b, w_out, norm_post_g, loss_target, m_norm_pre_g, m_w_in, m_mla_q_norm_g, m_mla_w_uq, m_mla_kv_norm_g, m_mla_w_ukv, m_rw_mu, m_rw_w0, m_rw_w2, m_rw_a0, m_rw_a2, m_rw_k_k, m_rw_k_a, m_rw_r_k, m_rw_ln_g, m_rw_ln_b, m_w_out, m_norm_post_g, v_norm_pre_g, v_w_in, v_mla_q_norm_g, v_mla_w_uq, v_mla_kv_norm_g, v_mla_w_ukv, v_rw_mu, v_rw_w0, v_rw_w2, v_rw_a0, v_rw_a2, v_rw_k_k, v_rw_k_a, v_rw_r_k, v_rw_ln_g, v_rw_ln_b, v_w_out, v_norm_post_g):
    wts = dict(norm_pre_g=norm_pre_g, w_in=w_in, mla_q_norm_g=mla_q_norm_g, mla_w_uq=mla_w_uq,
               mla_kv_norm_g=mla_kv_norm_g, mla_w_ukv=mla_w_ukv, rw_mu=rw_mu, rw_w0=rw_w0, rw_w2=rw_w2, rw_a0=rw_a0,
               rw_a2=rw_a2, rw_k_k=rw_k_k, rw_k_a=rw_k_a, rw_r_k=rw_r_k, rw_ln_g=rw_ln_g, rw_ln_b=rw_ln_b, w_out=w_out,
               norm_post_g=norm_post_g)
    mom_m = dict(norm_pre_g=m_norm_pre_g, w_in=m_w_in, mla_q_norm_g=m_mla_q_norm_g, mla_w_uq=m_mla_w_uq,
                 mla_kv_norm_g=m_mla_kv_norm_g, mla_w_ukv=m_mla_w_ukv, rw_mu=m_rw_mu, rw_w0=m_rw_w0, rw_w2=m_rw_w2,
                 rw_a0=m_rw_a0, rw_a2=m_rw_a2, rw_k_k=m_rw_k_k, rw_k_a=m_rw_k_a, rw_r_k=m_rw_r_k, rw_ln_g=m_rw_ln_g,
                 rw_ln_b=m_rw_ln_b, w_out=m_w_out, norm_post_g=m_norm_post_g)
    mom_v = dict(norm_pre_g=v_norm_pre_g, w_in=v_w_in, mla_q_norm_g=v_mla_q_norm_g, mla_w_uq=v_mla_w_uq,
                 mla_kv_norm_g=v_mla_kv_norm_g, mla_w_ukv=v_mla_w_ukv, rw_mu=v_rw_mu, rw_w0=v_rw_w0, rw_w2=v_rw_w2,
                 rw_a0=v_rw_a0, rw_a2=v_rw_a2, rw_k_k=v_rw_k_k, rw_k_a=v_rw_k_a, rw_r_k=v_rw_r_k, rw_ln_g=v_rw_ln_g,
                 rw_ln_b=v_rw_ln_b, w_out=v_w_out, norm_post_g=v_norm_post_g)
    bsz, t, _ = x.shape
    bo, invf = _constants()
    c_idx = lax.axis_index("c")

    g_in, g_uq, g_ukv, g_w2, g_a2, g_out = _ag_weights([wts[n][0] for n in SHARDED])
    w_in_f = jnp.transpose(g_in, (1, 0, 2)).reshape(D, D_IN)
    wp = jnp.concatenate([w_in_f[:, 2112:3136], w_in_f[:, 0:384], w_in_f[:, 448:1984], w_in_f[:, 384:448],
                          w_in_f[:, 1984:2112], jnp.zeros((D, 64), BF16)], axis=1)
    wuq = jnp.pad(jnp.transpose(g_uq, (1, 0, 2)).reshape(256, HEADS, 192), ((0, 0), (0, 0), (0, 64))).reshape(256, 1024)
    wukv = jnp.transpose(jnp.transpose(g_ukv, (1, 0, 2)).reshape(128, HEADS, 2, 128), (0, 2, 1, 3)).reshape(128, 1024)
    w2 = jnp.transpose(g_w2, (1, 0, 2)).reshape(64, RW)
    a2 = jnp.transpose(g_a2, (1, 0, 2)).reshape(64, RW)
    w2p = jnp.pad(w2, ((64, 128), (0, 0)))
    a2p = jnp.pad(a2, ((128, 64), (0, 0)))
    wo = g_out.reshape(D, D)
    mu = jnp.concatenate([rw_mu[:, 0:1536], jnp.zeros((1, 64), F32), rw_mu[:, 1536:1664], jnp.zeros((1, 64), F32)],
                         axis=1)
    r_k = rw_r_k.reshape(1, RW)
    pos = positions.astype(F32)[:, :, None]

    (u, pp, q_att, k_att, v_att, r, w, kp, v, al, be) = _pre_fwd(
        x, pos, invf, norm_pre_g, wp, mla_q_norm_g, wuq, mla_kv_norm_g, wukv, mu, rw_w0, w2p, rw_a0, a2p, rw_k_k,
        rw_k_a, bo)
    o, lse = _attn_fwd(q_att, k_att, v_att)
    rw_k = _spread_k([r, w, kp, al, be])
    v_v = _to_v(v)
    yw_v, states, u_v = _wkv_fwd(*rw_k, v_v)
    yw = _from_v(yw_v, bsz)

    (dh, dz, dym, dyw, dbon, loss_acc, d_wo, d_gpost, d_lng, d_lnb, d_rk) = _post(
        x, loss_target, pp, o, yw, r, kp, v, rw_ln_g, rw_ln_b, r_k, wo, wo.T, norm_post_g, bo)
    loss = lax.psum(loss_acc[0, 0], ("x", "y", "c"))

    d_k = _wkv_bwd(*rw_k, v_v, _to_v(dyw), states, u_v)
    dr_w, dw_w, dkp_w, dal_w, dbe_w = _gather_k(d_k[:5], bsz)
    dwkv = (dr_w, dw_w, dkp_w, _from_v(d_k[5], bsz), dal_w, dbe_w)
    dq, dk, dva = _attn_bwd(q_att, k_att, v_att, o, lse, dym)

    (da, d_wuq, d_wukv, d_w2p, d_a2p, d_gq, d_gkv, d_mu, d_w0, d_a0, d_kk, d_ka) = _pre_bwd_a(
        pp, pos, invf, (mla_q_norm_g, wuq.T, mla_kv_norm_g, wukv.T), mu, rw_w0, w2p, w2p.T, rw_a0, a2p, a2p.T,
        rw_k_k, rw_k_a, bo, dq, dk, dva, dwkv, dbon)
    grad_x, dpb, d_gpre = _pre_bwd_b(x, dh, dz, da, mu, wp.T, norm_pre_g)
    d_wp = _tn_matmul(u.reshape(bsz * t, D), dpb.reshape(bsz * t, DP), DP, "dw_in")

    full_g = {
        "w_in": jnp.concatenate([d_wp[:, 1024:1408], d_wp[:, 2944:3008], d_wp[:, 1408:2944], d_wp[:, 3008:3136],
                                 d_wp[:, 0:1024]], axis=1),
        "mla_w_uq": d_wuq.reshape(256, HEADS, 256)[:, :, :192].reshape(256, 768),
        "mla_w_ukv": jnp.transpose(d_wukv.reshape(128, 2, HEADS, 128), (0, 2, 1, 3)).reshape(128, 1024),
        "rw_w2": d_w2p[64:128],
        "rw_a2": d_a2p[128:192],
        "w_out": d_wo,
    }
    small_g = {
        "norm_pre_g": d_gpre, "mla_q_norm_g": d_gq, "mla_kv_norm_g": d_gkv,
        "rw_mu": jnp.concatenate([d_mu[:, 0:1536], d_mu[:, 1600:1728]], axis=1),
        "rw_w0": d_w0, "rw_a0": d_a0, "rw_k_k": d_kk, "rw_k_a": d_ka, "rw_r_k": d_rk, "rw_ln_g": d_lng,
        "rw_ln_b": d_lnb, "norm_post_g": d_gpost,
    }

    def by_shard(name, g):
        if name == "w_out":
            return g.reshape(N_SHARD, -1, LANES)
        rows, cols = g.shape
        return jnp.transpose(g.reshape(rows, N_SHARD, cols // N_SHARD), (1, 0, 2)).reshape(N_SHARD, -1, LANES)

    packed = jnp.concatenate([by_shard(n, full_g[n]) for n in SHARDED], axis=1)
    halves = packed.reshape(N_SHARD, 2, HALF, LANES)
    keep = lax.dynamic_index_in_dim(halves, c_idx, 1, keepdims=False)
    give = lax.dynamic_index_in_dim(halves, 1 - c_idx, 1, keepdims=False)
    got = _rs_pair_exchange(give)
    pair_sum, pair_sum_b = _add_n([keep.reshape(-1, LANES), got.reshape(-1, LANES)], "rs_pair_sum", SUM_ROWS,
                                  also_bf16=True)
    g_shard = _rs_chips(pair_sum.reshape(N_SHARD, HALF, LANES),
                        pair_sum_b.reshape(N_SHARD, HALF, LANES)).reshape(PACK_TOTAL, LANES)

    g_small = _small_allreduce(_pack_small(small_g))

    shard_like = {n: wts[n][0] for n in SHARDED}
    d_sh, nm_sh, nv_sh = _adamw(_pack_shard({n: wts[n][0] for n in SHARDED}), g_shard,
                                _pack_shard({n: mom_m[n][0] for n in SHARDED}),
                                _pack_shard({n: mom_v[n][0] for n in SHARDED}), "adamw_sharded", 568)
    d_sm, nm_sm, nv_sm = _adamw(_pack_small(wts), g_small, _pack_small(mom_m), _pack_small(mom_v), "adamw_small",
                                SMALL_ROWS)

    def unpack(sh, sm):
        out = {n: a[None] for n, a in _unpack_shard(sh, shard_like).items()}
        out.update(_unpack_small(sm, wts))
        return out

    grads, deltas, new_m, new_v = unpack(g_shard, g_small), unpack(d_sh, d_sm), unpack(nm_sh, nm_sm), unpack(nv_sh, nv_sm)
    return (loss, grad_x, *[grads[n] for n in WEIGHTS], *[deltas[n] for n in WEIGHTS],
            *[new_m[n] for n in WEIGHTS], *[new_v[n] for n in WEIGHTS])
```

```python
import functools

import numpy as np
import jax
import jax.numpy as jnp
from jax import lax
from jax.experimental import pallas as pl
from jax.experimental.pallas import tpu as pltpu

F32, BF16 = jnp.float32, jnp.bfloat16
MESH = pl.DeviceIdType.MESH

D = 1024
HEADS = 4
RW = 512
NORM_EPS = 1e-6
GN_EPS = 64e-5
ROPE_THETA = 10000.0
SCALE = (128 + 64) ** -0.5
D_IN = 3136
LR, B1, B2, ADAM_EPS, WD, STEP = 0.001, 0.9, 0.999, 1e-08, 0.01, 10

Z0, CQ0, CKV0, RW0, DP = 0, 1024, 1280, 1408, 3200
NRW = DP - RW0

LANES = 128
SUBLANES = 8
VMEM_LIMIT = 56 * 1024 * 1024

TT = 512
TT_VPU = 256
TQ = 512

N_SHARD = 4
PACK_ROWS = (1024 * 784 // 128, 256 * 192 // 128, 128 * 256 // 128, 64, 64, 256 * 1024 // 128)
PACK_TOTAL = sum(PACK_ROWS)
HALF = PACK_TOTAL // 2
SUM_ROWS = HALF // 4
SMALL_ROWS = 64


def _cparams(sem=None):
    return pltpu.CompilerParams(dimension_semantics=sem, vmem_limit_bytes=VMEM_LIMIT)


def _full(shape):
    n = len(shape)
    return pl.BlockSpec(shape, lambda *_: (0,) * n, pipeline_mode=pl.Buffered(1))


def _resident(shape):
    n = len(shape)
    return pl.BlockSpec(shape, lambda *_: (0,) * n)


def _dot(a, b):
    return jnp.dot(a, b, preferred_element_type=F32)


def _dot_nt(a, b):
    return lax.dot_general(a, b, (((1,), (1,)), ((), ())), preferred_element_type=F32)


def _dot_tn(a, b):
    return lax.dot_general(a, b, (((0,), (0,)), ((), ())), preferred_element_type=F32)


def _split3(x):
    hi = x.astype(BF16)
    r1 = x - hi.astype(F32)
    mid = r1.astype(BF16)
    lo = (r1 - mid.astype(F32)).astype(BF16)
    return hi, mid, lo


def _seg(x, bo):
    rows, nblk = x.shape[0], x.shape[1] // LANES
    pieces = [p for i in range(nblk) for p in _split3(x[:, LANES * i:LANES * (i + 1)])]
    res = _dot(jnp.concatenate(pieces, axis=0), bo)
    parts = [res[(3 * i) * rows:(3 * i + 1) * rows] + res[(3 * i + 1) * rows:(3 * i + 2) * rows]
             + res[(3 * i + 2) * rows:(3 * i + 3) * rows] for i in range(nblk)]
    return parts[0] if nblk == 1 else jnp.concatenate(parts, axis=1)


def _rms(x, g, n):
    rstd = lax.rsqrt(jnp.sum(x * x, axis=-1, keepdims=True) * (1.0 / n) + NORM_EPS)
    nx = x * rstd
    return nx * g, nx, rstd


def _rms_bwd(dy, nx, rstd, g, n):
    dn = dy * g
    dx = rstd * (dn - nx * (jnp.sum(dn * nx, axis=-1, keepdims=True) * (1.0 / n)))
    return dx, jnp.sum(dy * nx, axis=0, keepdims=True)


def _rot(x):
    lane = lax.broadcasted_iota(jnp.int32, x.shape, 1)
    return jnp.where((lane % 64) < 32, -pltpu.roll(x, x.shape[1] - 32, 1), pltpu.roll(x, 32, 1))


def _sigmoid(x):
    return 1.0 / (1.0 + jnp.exp(-x))


def _softplus(x):
    return jnp.maximum(x, 0.0) + jnp.log(1.0 + jnp.exp(-jnp.abs(x)))


def _rw_gates(ps, w0, w2p, a0, a2p, k_k, k_a, bo):
    r, k, v, misc = ps[:, 0:512], ps[:, 512:1024], ps[:, 1024:1536], ps[:, 1536:NRW]
    th = jnp.tanh(misc)
    wpre = w0 + _dot(th.astype(BF16), w2p)
    e = jnp.exp(-_softplus(-wpre) - 0.5)
    w = jnp.exp(-e)
    a = _sigmoid(a0 + _dot(misc.astype(BF16), a2p))
    m = k * k_k
    nrm = jnp.maximum(jnp.sqrt(_seg(m * m, bo)), 1e-12)
    kk = m / nrm
    kp = k * (1.0 + (a - 1.0) * k_a)
    return dict(r=r, k=k, v=v, misc=misc, th=th, wpre=wpre, e=e, w=w, a=a, nrm=nrm, kk=kk, kp=kp)


def _shift_mix(prw, prev_row, mu):
    row = lax.broadcasted_iota(jnp.int32, prw.shape, 0)
    sh = jnp.where(row == 0, prev_row, pltpu.roll(prw, 1, 0))
    return prw + (sh - prw) * mu, sh


def _ag_weights(shards):
    n = len(shards)

    def body(*refs):
        ins, outs = refs[:n], refs[n:2 * n]
        ici_send, ici_recv, d2d_send, d2d_recv = refs[2 * n:2 * n + 4]
        x, y, c = lax.axis_index("x"), lax.axis_index("y"), lax.axis_index("c")
        mine = 2 * x + y
        for w in range(n):
            outs[w][mine] = ins[w][...].astype(BF16)
        flips = ((1, 0), (0, 1), (1, 1))

        def half(w, shard, cc):
            rows = outs[w].shape[1] // 2
            return outs[w].at[shard, pl.ds(pl.multiple_of(cc * rows, 16), rows)]

        def ici(w, k, shard):
            fx, fy = flips[k]
            return pltpu.make_async_remote_copy(
                src_ref=half(w, shard, c), dst_ref=half(w, shard, c),
                send_sem=ici_send.at[w * 3 + k], recv_sem=ici_recv.at[w * 3 + k],
                device_id=(x ^ fx, y ^ fy, c), device_id_type=MESH)

        def d2d(w, k, cc):
            fx, fy = flips[k]
            theirs = 2 * (x ^ fx) + (y ^ fy)
            return pltpu.make_async_remote_copy(
                src_ref=half(w, theirs, cc), dst_ref=half(w, theirs, cc),
                send_sem=d2d_send.at[w * 3 + k], recv_sem=d2d_recv.at[w * 3 + k],
                device_id=(x, y, 1 - c), device_id_type=MESH)

        for w in range(n):
            for k in range(3):
                ici(w, k, mine).start()
        for w in range(n):
            for k in range(3):
                fx, fy = flips[k]
                ici(w, k, 2 * (x ^ fx) + (y ^ fy)).wait_recv()
                d2d(w, k, c).start()
        for w in range(n):
            for k in range(3):
                d2d(w, k, 1 - c).wait_recv()
        for w in range(n):
            for k in range(3):
                ici(w, k, mine).wait_send()
                d2d(w, k, c).wait_send()

    vm = pl.BlockSpec(memory_space=pltpu.VMEM)
    return pl.pallas_call(
        body, name="ag_weights",
        out_shape=[jax.ShapeDtypeStruct((N_SHARD,) + s.shape, BF16) for s in shards],
        in_specs=[vm] * n, out_specs=[vm] * n,
        scratch_shapes=[pltpu.SemaphoreType.DMA((3 * n,))] * 4,
        compiler_params=pltpu.CompilerParams(vmem_limit_bytes=VMEM_LIMIT),
    )(*shards)


def _rs_pair_exchange(send_half):
    def body(src_ref, dst_ref, send_sem, recv_sem):
        x, y, c = lax.axis_index("x"), lax.axis_index("y"), lax.axis_index("c")
        cp = pltpu.make_async_remote_copy(src_ref=src_ref, dst_ref=dst_ref, send_sem=send_sem, recv_sem=recv_sem,
                                          device_id=(x, y, 1 - c), device_id_type=MESH)
        cp.start()
        cp.wait()

    hbm = pl.BlockSpec(memory_space=pl.ANY)
    return pl.pallas_call(
        body, name="rs_pair_exchange",
        out_shape=jax.ShapeDtypeStruct(send_half.shape, send_half.dtype),
        in_specs=[hbm], out_specs=hbm,
        scratch_shapes=[pltpu.SemaphoreType.DMA, pltpu.SemaphoreType.DMA],
    )(send_half)


def _rs_chips(part_f32, part_bf16):
    def body(own_ref, src_ref, out_ref, recv, ici_send, ici_recv, d2d_send, d2d_recv):
        x, y, c = lax.axis_index("x"), lax.axis_index("y"), lax.axis_index("c")
        mine = 2 * x + y
        flips = ((1, 0), (0, 1), (1, 1))
        cps = []
        for k, (fx, fy) in enumerate(flips):
            theirs = 2 * (x ^ fx) + (y ^ fy)
            cps.append(pltpu.make_async_remote_copy(
                src_ref=src_ref.at[theirs], dst_ref=recv.at[k],
                send_sem=ici_send.at[k], recv_sem=ici_recv.at[k],
                device_id=(x ^ fx, y ^ fy, c), device_id_type=MESH))
        for cp in cps:
            cp.start()
        acc = own_ref[mine]
        for k, cp in enumerate(cps):
            cp.wait_recv()
            acc = acc + recv[k].astype(F32)
        out_ref[c] = acc
        to_sibling = pltpu.make_async_remote_copy(
            src_ref=out_ref.at[c], dst_ref=out_ref.at[c], send_sem=d2d_send, recv_sem=d2d_recv,
            device_id=(x, y, 1 - c), device_id_type=MESH)
        to_sibling.start()
        pltpu.make_async_remote_copy(
            src_ref=out_ref.at[1 - c], dst_ref=out_ref.at[1 - c], send_sem=d2d_send, recv_sem=d2d_recv,
            device_id=(x, y, 1 - c), device_id_type=MESH).wait_recv()
        to_sibling.wait_send()
        for cp in cps:
            cp.wait_send()

    vm = pl.BlockSpec(memory_space=pltpu.VMEM)
    return pl.pallas_call(
        body, name="rs_chips",
        out_shape=jax.ShapeDtypeStruct((2,) + part_f32.shape[1:], F32),
        in_specs=[vm, vm], out_specs=vm,
        scratch_shapes=[pltpu.VMEM((3,) + part_bf16.shape[1:], BF16), pltpu.SemaphoreType.DMA((3,)),
                        pltpu.SemaphoreType.DMA((3,)), pltpu.SemaphoreType.DMA, pltpu.SemaphoreType.DMA],
        compiler_params=pltpu.CompilerParams(vmem_limit_bytes=VMEM_LIMIT),
    )(part_f32, part_bf16)


def _small_allreduce(vec):
    def body(in_ref, out_ref, recv, send_sems, recv_sems):
        x, y, c = lax.axis_index("x"), lax.axis_index("y"), lax.axis_index("c")
        me = 4 * x + 2 * y + c
        cps = []
        for k in range(1, 8):
            fx, fy, fc = (k >> 2) & 1, (k >> 1) & 1, k & 1
            cps.append(pltpu.make_async_remote_copy(
                src_ref=in_ref, dst_ref=recv.at[k - 1],
                send_sem=send_sems.at[k - 1], recv_sem=recv_sems.at[k - 1],
                device_id=(x ^ fx, y ^ fy, c ^ fc), device_id_type=MESH))
        for cp in cps:
            cp.start()
        for cp in cps:
            cp.wait()
        acc = jnp.zeros(in_ref.shape, F32)
        for j in range(8):
            slot = jnp.maximum((me ^ j) - 1, 0)
            acc = acc + jnp.where(me == j, in_ref[...], recv[slot])
        out_ref[...] = acc

    vm = pl.BlockSpec(memory_space=pltpu.VMEM)
    return pl.pallas_call(
        body, name="small_allreduce",
        out_shape=jax.ShapeDtypeStruct(vec.shape, F32),
        in_specs=[vm], out_specs=vm,
        scratch_shapes=[pltpu.VMEM((7,) + vec.shape, F32), pltpu.SemaphoreType.DMA((7,)),
                        pltpu.SemaphoreType.DMA((7,))],
    )(vec)


def _add_n(arrs, name, rows, also_bf16=False):
    n = len(arrs)
    r = arrs[0].shape[0]

    def body(*refs):
        acc = refs[0][...].astype(F32)
        for k in range(1, n):
            acc = acc + refs[k][...].astype(F32)
        refs[n][...] = acc
        if also_bf16:
            refs[n + 1][...] = acc.astype(BF16)

    spec = pl.BlockSpec((rows, LANES), lambda i: (i, 0))
    out_shape = [jax.ShapeDtypeStruct(arrs[0].shape, F32)]
    if also_bf16:
        out_shape.append(jax.ShapeDtypeStruct(arrs[0].shape, BF16))
    return pl.pallas_call(
        body, name=name, grid=(r // rows,),
        out_shape=out_shape,
        in_specs=[spec] * n, out_specs=[spec] * len(out_shape),
        compiler_params=_cparams(("parallel",)),
    )(*arrs)


ADAM_ROWS = 64


def _adamw(ws, gs, ms, vs, name):
    n = len(ws)

    def body(*refs):
        for i in range(n):
            w_ref, g_ref, m_ref, v_ref = (refs[k * n + i] for k in range(4))
            d_ref, nm_ref, nv_ref = (refs[(4 + k) * n + i] for k in range(3))
            rows = min(ADAM_ROWS, w_ref.shape[0])

            def chunk(r, _):
                at = pl.ds(pl.multiple_of(r * rows, SUBLANES), rows)
                gg = g_ref[at, :]
                nm = B1 * m_ref[at, :] + (1.0 - B1) * gg
                nv = B2 * v_ref[at, :] + (1.0 - B2) * (gg * gg)
                m_hat = nm / (1.0 - B1 ** STEP)
                v_hat = nv / (1.0 - B2 ** STEP)
                d_ref[at, :] = -LR * (m_hat / (jnp.sqrt(v_hat) + ADAM_EPS) + WD * w_ref[at, :])
                nm_ref[at, :] = nm
                nv_ref[at, :] = nv
                return 0

            lax.fori_loop(0, w_ref.shape[0] // rows, chunk, 0)

    vm = pl.BlockSpec(memory_space=pltpu.VMEM)
    sds = [jax.ShapeDtypeStruct(w.shape, F32) for w in ws]
    outs = pl.pallas_call(
        body, name=name, out_shape=sds * 3, in_specs=[vm] * (4 * n), out_specs=[vm] * (3 * n),
        compiler_params=pltpu.CompilerParams(vmem_limit_bytes=VMEM_LIMIT),
    )(*ws, *gs, *ms, *vs)
    return outs[:n], outs[n:2 * n], outs[2 * n:]


def _pre_fwd(x, pos, invf, gpre, wp, gq, wuq, gkv, wukv, mu, w0, w2p, a0, a2p, k_k, k_a, bo):
    bsz, t, _ = x.shape
    nt = t // TT

    def body(x_ref, pos_ref, invf_ref, gpre_ref, wp_ref, gq_ref, wuq_ref, gkv_ref, wukv_ref, mu_ref, w0_ref,
             w2p_ref, a0_ref, a2p_ref, kk_ref, ka_ref, bo_ref,
             u_ref, pp_ref, q_ref, k_ref, v_ref, r_o, w_o, kp_o, vv_o, al_o, be_o, carry):
        i = pl.program_id(1)
        u, _, _ = _rms(x_ref[0], gpre_ref[...], D)
        ub = u.astype(BF16)
        u_ref[0] = ub
        p = _dot(ub, wp_ref[...])
        pp_ref[0] = p
        prw = p[:, RW0:DP]

        @pl.when(i == 0)
        def _():
            carry[...] = jnp.zeros(carry.shape, F32)

        ps, _ = _shift_mix(prw, carry[7:8, :], mu_ref[...])
        carry[...] = prw[TT - 8:TT, :]

        g = _rw_gates(ps, w0_ref[...], w2p_ref[...], a0_ref[...], a2p_ref[...], kk_ref[...], ka_ref[...],
                      bo_ref[...])
        r_o[0] = g["r"]
        w_o[0] = g["w"]
        kp_o[0] = g["kp"]
        vv_o[0] = g["v"]
        al_o[0] = -g["kk"]
        be_o[0] = g["kk"] * g["a"]

        cqn, _, _ = _rms(p[:, CQ0:CQ0 + 256], gq_ref[...], 256)
        q = _dot(cqn.astype(BF16), wuq_ref[...])
        ckvn, _, _ = _rms(p[:, CKV0:CKV0 + 128], gkv_ref[...], 128)
        kv = _dot(ckvn.astype(BF16), wukv_ref[...])
        ang = pos_ref[0] * invf_ref[...]
        cs, sn = jnp.cos(ang), jnp.sin(ang)
        lane = lax.broadcasted_iota(jnp.int32, cs.shape, 1)
        kr = ps[:, 1536:1536 + LANES]
        kr = jnp.where(lane < 64, kr * cs + _rot(kr) * sn, 0.0).astype(BF16)
        for h in range(HEADS):
            qr = q[:, 256 * h + 128:256 * h + 256]
            q_ref[0, :, 256 * h:256 * h + 128] = q[:, 256 * h:256 * h + 128].astype(BF16)
            q_ref[0, :, 256 * h + 128:256 * h + 256] = (qr * cs + _rot(qr) * sn).astype(BF16)
            k_ref[0, :, 256 * h:256 * h + 128] = kv[:, 128 * h:128 * h + 128].astype(BF16)
            k_ref[0, :, 256 * h + 128:256 * h + 256] = kr
        v_ref[0] = kv[:, 512:1024].astype(BF16)

    tok = lambda c: pl.BlockSpec((1, TT, c), lambda b, i: (b, i, 0))
    full = lambda a: _full(a.shape)
    ins = (x, pos, invf, gpre, wp, gq, wuq, gkv, wukv, mu, w0, w2p, a0, a2p, k_k, k_a, bo)
    in_specs = [tok(D), tok(1)] + [full(a) for a in ins[2:]]
    sd = lambda c, dt: jax.ShapeDtypeStruct((bsz, t, c), dt)
    out_shape = [sd(D, BF16), sd(DP, F32), sd(1024, BF16), sd(1024, BF16), sd(512, BF16)] + [sd(RW, F32)] * 6
    out_specs = [tok(D), tok(DP), tok(1024), tok(1024), tok(512)] + [tok(RW)] * 6
    return pl.pallas_call(
        body, name="pre_fwd", grid=(bsz, nt), out_shape=out_shape, in_specs=in_specs, out_specs=out_specs,
        scratch_shapes=[pltpu.VMEM((8, NRW), F32)],
        compiler_params=_cparams(("arbitrary", "arbitrary")),
    )(*ins)


def _attn_fwd(q, k, v):
    bsz, t, _ = q.shape
    nq = t // TQ

    def body(q_ref, k_ref, v_ref, o_ref, lse_ref):
        i = pl.program_id(2)

        def step(j, carry, diagonal):
            at = pl.ds(pl.multiple_of(j * TQ, TQ), TQ)
            out = []
            for hh in range(2):
                m, l, acc = carry[hh]
                s = _dot_nt(q_ref[0, :, 256 * hh:256 * (hh + 1)], k_ref[0, at, 256 * hh:256 * (hh + 1)]) * SCALE
                if diagonal:
                    s = jnp.where(lax.broadcasted_iota(jnp.int32, (TQ, TQ), 1)
                                  <= lax.broadcasted_iota(jnp.int32, (TQ, TQ), 0), s, -1e30)
                mn = jnp.maximum(m, jnp.max(s, axis=1, keepdims=True))
                p = jnp.exp(s - mn)
                al = jnp.exp(m - mn)
                l = al * l + jnp.sum(p, axis=1, keepdims=True)
                acc = al * acc + _dot(p.astype(BF16), v_ref[0, at, LANES * hh:LANES * (hh + 1)])
                out.append((mn, l, acc))
            return tuple(out)

        start = (jnp.full((TQ, 1), -1e30, F32), jnp.zeros((TQ, 1), F32), jnp.zeros((TQ, LANES), F32))
        before = lax.fori_loop(0, i, lambda j, carry: step(j, carry, False), (start, start))
        for hh, (m, l, acc) in enumerate(step(i, before, True)):
            o_ref[0, :, LANES * hh:LANES * (hh + 1)] = acc / l
            lse_ref[0, hh] = jnp.broadcast_to(m + jnp.log(l), (TQ, LANES))

    return pl.pallas_call(
        body, name="attn_fwd", grid=(bsz, HEADS // 2, nq),
        out_shape=[jax.ShapeDtypeStruct((bsz, t, 512), F32), jax.ShapeDtypeStruct((bsz, HEADS, t, LANES), F32)],
        in_specs=[pl.BlockSpec((1, TQ, 512), lambda b, h, i: (b, i, h)),
                  pl.BlockSpec((1, t, 512), lambda b, h, i: (b, 0, h)),
                  pl.BlockSpec((1, t, 256), lambda b, h, i: (b, 0, h))],
        out_specs=[pl.BlockSpec((1, TQ, 256), lambda b, h, i: (b, i, h)),
                   pl.BlockSpec((1, 2, TQ, LANES), lambda b, h, i: (b, h, i, 0))],
        compiler_params=_cparams(("parallel", "parallel", "arbitrary")),
    )(q, k, v)


def _attn_bwd(q, k, v, o, lse, do):
    bsz, t, _ = q.shape
    nq = t // TQ

    def body(q_ref, k_ref, v_ref, o_ref, lse_ref, do_ref, dq_ref, dk_ref, dv_ref, dl_ref):
        def prep(i, _):
            at = pl.ds(pl.multiple_of(i * TQ, TQ), TQ)
            dl_ref[at, :] = jnp.broadcast_to(jnp.sum(do_ref[0, at, :] * o_ref[0, at, :], axis=1, keepdims=True),
                                             (TQ, LANES))
            return 0

        lax.fori_loop(0, nq, prep, 0)
        dq_ref[0] = jnp.zeros((t, 256), F32)

        def kv_tile(j, _):
            atk = pl.ds(pl.multiple_of(j * TQ, TQ), TQ)
            kt = k_ref[0, atk, :]
            vt = v_ref[0, atk, :]

            def q_tile(i, carry, diagonal):
                dk, dv = carry
                atq = pl.ds(pl.multiple_of(i * TQ, TQ), TQ)
                qt = q_ref[0, atq, :]
                dob = do_ref[0, atq, :].astype(BF16)
                s = _dot_nt(qt, kt) * SCALE
                if diagonal:
                    s = jnp.where(lax.broadcasted_iota(jnp.int32, (TQ, TQ), 1)
                                  <= lax.broadcasted_iota(jnp.int32, (TQ, TQ), 0), s, -1e30)
                p = jnp.exp(s - lse_ref[0, 0, atq, :][:, 0:1])
                dv = dv + _dot_tn(p.astype(BF16), dob)
                dp = _dot_nt(dob, vt)
                ds = (p * (dp - dl_ref[atq, :][:, 0:1]) * SCALE).astype(BF16)
                dk = dk + _dot_tn(ds, qt)
                dq_ref[0, atq, :] += _dot(ds, kt)
                return dk, dv

            first = q_tile(j, (jnp.zeros((TQ, 256), F32), jnp.zeros((TQ, LANES), F32)), True)
            dk, dv = lax.fori_loop(j + 1, nq, lambda i, carry: q_tile(i, carry, False), first)
            dk_ref[0, atk, :] = dk
            dv_ref[0, atk, :] = dv
            return 0

        lax.fori_loop(0, nq, kv_tile, 0)

    s256 = pl.BlockSpec((1, t, 256), lambda b, h: (b, 0, h))
    s128 = pl.BlockSpec((1, t, LANES), lambda b, h: (b, 0, h))
    return pl.pallas_call(
        body, name="attn_bwd", grid=(bsz, HEADS),
        out_shape=[jax.ShapeDtypeStruct((bsz, t, 1024), F32), jax.ShapeDtypeStruct((bsz, t, 1024), F32),
                   jax.ShapeDtypeStruct((bsz, t, 512), F32)],
        in_specs=[s256, s256, s128, s128, pl.BlockSpec((1, 1, t, LANES), lambda b, h: (b, h, 0, 0)), s128],
        out_specs=[s256, s256, s128],
        scratch_shapes=[pltpu.VMEM((t, LANES), F32)],
        compiler_params=_cparams(("parallel", "parallel")),
    )(q, k, v, o, lse, do)


RW_HEADS = 8
CH = 16


def _lane_split(bsz):
    vs = LANES // (bsz * RW_HEADS)
    return vs, 64 // vs


def _gather_matrix(bsz):
    group = bsz * RW_HEADS
    vs = LANES // group
    half = (RW_HEADS // 2) * bsz * SPREAD_STEPS
    p = np.zeros((SPREAD_STEPS // vs * LANES, 2 * half), np.float32)
    for g2 in range(SPREAD_STEPS // vs):
        for j in range(vs):
            for b in range(bsz):
                for h in range(RW_HEADS):
                    hp, hpar = h // 2, h % 2
                    p[g2 * LANES + j * group + b * RW_HEADS + h,
                      hpar * half + (hp * bsz + b) * SPREAD_STEPS + g2 * vs + j] = 1.0
    return jnp.asarray(np.concatenate([p] * 3, axis=0), BF16)


def _gather_k(ys, bsz):
    vs = LANES // (bsz * RW_HEADS)
    assert (RW_HEADS // 2) * bsz * SPREAD_STEPS == LANES, "the transposed tile must be 128 lanes wide"
    tg = ys[0].shape[0]
    n = len(ys)
    ngrp = SPREAD_BLOCK // SPREAD_STEPS
    per = SPREAD_STEPS // vs

    def body(*refs):
        pm = refs[n][...]
        for y_ref, o_ref in zip(refs[:n], refs[n + 1:]):
            lhs = jnp.concatenate(
                [jnp.concatenate(_split3(jnp.concatenate([y_ref[per * m + g2] for g2 in range(per)], axis=1)), axis=1)
                 for m in range(ngrp)], axis=0)
            a = _dot(lhs, pm)
            for m in range(ngrp):
                am = a[64 * m:64 * (m + 1)]
                bt = jnp.concatenate([am[:, 0:LANES], am[:, LANES:2 * LANES]], axis=0).T
                for hp in range(RW_HEADS // 2):
                    for b in range(bsz):
                        at = (hp * bsz + b) * SPREAD_STEPS
                        o_ref[b, SPREAD_STEPS * m:SPREAD_STEPS * (m + 1), LANES * hp:LANES * (hp + 1)] = \
                            bt[at:at + SPREAD_STEPS]

    pm = _gather_matrix(bsz)
    return pl.pallas_call(
        body, name="wkv_gather", grid=(tg * vs // SPREAD_BLOCK,),
        out_shape=[jax.ShapeDtypeStruct((bsz, tg * vs, RW), F32)] * n,
        in_specs=[pl.BlockSpec((SPREAD_BLOCK // vs, 64, LANES), lambda i: (i, 0, 0))] * n + [_full(pm.shape)],
        out_specs=[pl.BlockSpec((bsz, SPREAD_BLOCK, RW), lambda i: (0, i, 0))] * n,
        compiler_params=_cparams(("parallel",)),
    )(*ys, pm)


def _to_v(x):
    bsz, t, _ = x.shape
    vs, vq = _lane_split(bsz)
    return jnp.transpose(x.reshape(bsz, t, RW_HEADS, vq, vs), (1, 3, 4, 0, 2)).reshape(t, vq, LANES)


def _from_v(y, bsz):
    t = y.shape[0]
    vs, vq = _lane_split(bsz)
    return jnp.transpose(y.reshape(t, vq, vs, bsz, RW_HEADS), (3, 0, 4, 1, 2)).reshape(bsz, t, RW)


def _ksum(a):
    return jnp.sum(a, axis=0, keepdims=True)


def _fold(a, group):
    sh = LANES // 2
    while sh >= group:
        a = a + pltpu.roll(a, sh, 1)
        sh //= 2
    return a


def _lane_group(shape, group):
    return lax.broadcasted_iota(jnp.int32, shape, 1) // group


SPREAD_STEPS = 8
SPREAD_BLOCK = 32


def _spread_matrix(bsz):
    group = bsz * RW_HEADS
    vs = LANES // group
    rows = (RW_HEADS // 2) * bsz * SPREAD_STEPS
    q = np.zeros((2, rows, SPREAD_STEPS * LANES), np.float32)
    for hpar in range(2):
        for hp in range(RW_HEADS // 2):
            for b in range(bsz):
                for st in range(SPREAD_STEPS):
                    row = (hp * bsz + b) * SPREAD_STEPS + st
                    for s in range(vs):
                        q[hpar, row, st * LANES + s * group + b * RW_HEADS + 2 * hp + hpar] = 1.0
    return jnp.asarray(np.concatenate([q[0], q[1]] * 3, axis=0), BF16)


def _spread_k(xs):
    bsz, t, _ = xs[0].shape
    assert (RW_HEADS // 2) * bsz * SPREAD_STEPS == LANES, "the transposed tile must be 128 lanes wide"
    n = len(xs)
    ngrp = SPREAD_BLOCK // SPREAD_STEPS

    def body(*refs):
        qm = refs[n][...]
        for x_ref, o_ref in zip(refs[:n], refs[n + 1:]):
            cols = [[] for _ in range(6)]
            for m in range(ngrp):
                at = slice(SPREAD_STEPS * m, SPREAD_STEPS * (m + 1))
                x8 = jnp.concatenate([x_ref[b, at, LANES * hp:LANES * (hp + 1)]
                                      for hp in range(RW_HEADS // 2) for b in range(bsz)], axis=0)
                for pi, piece in enumerate(_split3(x8.T)):
                    cols[2 * pi].append(piece[0:64])
                    cols[2 * pi + 1].append(piece[64:128])
            lhs = jnp.concatenate([jnp.concatenate(c, axis=0) for c in cols], axis=1)
            y = _dot(lhs, qm)
            for m in range(ngrp):
                for st in range(SPREAD_STEPS):
                    o_ref[SPREAD_STEPS * m + st] = y[64 * m:64 * (m + 1), LANES * st:LANES * (st + 1)]

    qm = _spread_matrix(bsz)
    return pl.pallas_call(
        body, name="wkv_spread", grid=(t // SPREAD_BLOCK,),
        out_shape=[jax.ShapeDtypeStruct((t, 64, LANES), F32)] * n,
        in_specs=[pl.BlockSpec((bsz, SPREAD_BLOCK, RW), lambda i: (0, i, 0))] * n + [_full(qm.shape)],
        out_specs=[pl.BlockSpec((SPREAD_BLOCK, 64, LANES), lambda i: (i, 0, 0))] * n,
        compiler_params=_cparams(("parallel",)),
    )(*xs, qm)


def _wkv_fwd(r, w, kp, al, be, v):
    t, vq = v.shape[0], v.shape[1]

    def body(r_ref, w_ref, kp_ref, al_ref, be_ref, v_ref, y_ref, a_ref, u_ref, st_ref):
        @pl.when(pl.program_id(0) == 0)
        def _():
            st_ref[...] = jnp.zeros(st_ref.shape, F32)

        def step(tl, _):
            rv, wv, kv, av, bv = r_ref[tl], w_ref[tl], kp_ref[tl], al_ref[tl], be_ref[tl]
            vals = v_ref[tl]
            yrows, urows = [], []
            for q in range(vq):
                s = st_ref[q]
                u = _ksum(s * av)
                s = s * wv + bv * u + kv * vals[q:q + 1]
                st_ref[q] = s
                a_ref[tl, q] = s
                urows.append(u)
                yrows.append(_ksum(s * rv))
            y_ref[tl] = jnp.concatenate(yrows, axis=0)
            u_ref[tl] = jnp.concatenate(urows, axis=0)
            return 0

        lax.fori_loop(0, CH, step, 0)

    kspec = pl.BlockSpec((CH, 64, LANES), lambda i: (i, 0, 0))
    vspec = pl.BlockSpec((CH, vq, LANES), lambda i: (i, 0, 0))
    vsd = jax.ShapeDtypeStruct((t, vq, LANES), F32)
    return pl.pallas_call(
        body, name="wkv_fwd", grid=(t // CH,),
        out_shape=[vsd, jax.ShapeDtypeStruct((t, vq, 64, LANES), F32), vsd],
        in_specs=[kspec] * 5 + [vspec],
        out_specs=[vspec, pl.BlockSpec((CH, vq, 64, LANES), lambda i: (i, 0, 0, 0)), vspec],
        scratch_shapes=[pltpu.VMEM((vq, 64, LANES), F32)],
        compiler_params=_cparams(("arbitrary",)),
    )(r, w, kp, al, be, v)


def _wkv_bwd(r, w, kp, al, be, v, dy, states, u):
    t, vq = v.shape[0], v.shape[1]
    vs = 64 // vq
    group = LANES // vs
    n = t // CH
    ng = CH // vs

    def body(r_ref, w_ref, kp_ref, al_ref, be_ref, v_ref, dy_ref, u_ref, a_ref, ap_ref,
             dr_ref, dw_ref, dkp_ref, dal_ref, dbe_ref, dv_ref, ds_ref):
        @pl.when(pl.program_id(0) == 0)
        def _():
            ds_ref[...] = jnp.zeros(ds_ref.shape, F32)

        earliest = pl.program_id(0) == n - 1

        def reverse(i, _):
            g = ng - 1 - i
            grp = _lane_group((64, LANES), group)
            outs = None
            for j in reversed(range(vs)):
                tl = g * vs + j
                rv, wv, kv, av, bv = r_ref[tl], w_ref[tl], kp_ref[tl], al_ref[tl], be_ref[tl]
                vals, dys, us = v_ref[tl], dy_ref[tl], u_ref[tl]
                acc = None
                dvrows = []
                for q in range(vq):
                    if j > 0:
                        s_prev = a_ref[tl - 1, q]
                    else:
                        before = jnp.where(earliest, 0.0, ap_ref[0, q])
                        s_prev = jnp.where(g == 0, before, a_ref[jnp.maximum(tl - 1, 0), q])
                    dyq = dys[q:q + 1]
                    ds = ds_ref[q] + rv * dyq
                    c = _ksum(ds * bv)
                    dvrows.append(_ksum(ds * kv))
                    terms = (a_ref[tl, q] * dyq, ds * s_prev, ds * vals[q:q + 1], s_prev * c, ds * us[q:q + 1])
                    acc = terms if acc is None else tuple(a + b for a, b in zip(acc, terms))
                    ds_ref[q] = ds * wv + av * c
                dv_ref[tl] = jnp.concatenate(dvrows, axis=0)
                summed = [_fold(a, group) for a in acc]
                outs = summed if outs is None else [jnp.where(grp == j, f, o) for f, o in zip(summed, outs)]
            for ref, o in zip((dr_ref, dw_ref, dkp_ref, dal_ref, dbe_ref), outs):
                ref[g] = o
            return 0

        lax.fori_loop(0, ng, reverse, 0)

    kspec = pl.BlockSpec((CH, 64, LANES), lambda i: (n - 1 - i, 0, 0))
    gspec = pl.BlockSpec((ng, 64, LANES), lambda i: (n - 1 - i, 0, 0))
    vspec = pl.BlockSpec((CH, vq, LANES), lambda i: (n - 1 - i, 0, 0))
    ksd = jax.ShapeDtypeStruct((t // vs, 64, LANES), F32)
    return pl.pallas_call(
        body, name="wkv_bwd", grid=(n,),
        out_shape=[ksd] * 5 + [jax.ShapeDtypeStruct((t, vq, LANES), F32)],
        in_specs=[kspec] * 5 + [vspec, vspec, vspec,
                                pl.BlockSpec((CH, vq, 64, LANES), lambda i: (n - 1 - i, 0, 0, 0)),
                                pl.BlockSpec((1, vq, 64, LANES), lambda i: (jnp.maximum((n - 1 - i) * CH - 1, 0), 0, 0, 0))],
        out_specs=[gspec] * 5 + [vspec],
        scratch_shapes=[pltpu.VMEM((vq, 64, LANES), F32)],
        compiler_params=_cparams(("arbitrary",)),
    )(r, w, kp, al, be, v, dy, u, states, states)


def _post(x, tgt, pp, o, yw, r, kp, v, ln_g, ln_b, r_k, wo, wot, gpost, bo):
    bsz, t, _ = x.shape
    tt = TT_VPU
    nt = t // tt

    def body(x_ref, tgt_ref, z_ref, o_ref, yw_ref, r_ref, kp_ref, v_ref, lng_ref, lnb_ref, rk_ref, wo_ref, wot_ref,
             gpost_ref, bo_ref,
             dh_ref, dz_ref, dym_ref, dyw_ref, dbon_ref, loss_ref, dwo_ref, dgpost_ref, dlng_ref, dlnb_ref, drk_ref):
        first = (pl.program_id(0) == 0) & (pl.program_id(1) == 0)

        @pl.when(first)
        def _():
            for ref in (loss_ref, dwo_ref, dgpost_ref, dlng_ref, dlnb_ref, drk_ref):
                ref[...] = jnp.zeros(ref.shape, F32)

        bo_m = bo_ref[...]
        seg = lambda a: _seg(a, bo_m)
        rowsum = lambda a: jnp.sum(a, axis=0, keepdims=True)
        ywv, rv, kpv, vv = yw_ref[0], r_ref[0], kp_ref[0], v_ref[0]
        ln_g, r_k = lng_ref[...], rk_ref[...]
        mean = seg(ywv) * (1.0 / 64)
        yc = ywv - mean
        rstd = lax.rsqrt(seg(yc * yc) * (1.0 / 64) + GN_EPS)
        yhat = yc * rstd
        sb = seg(rv * kpv * r_k)
        y_rw = yhat * ln_g + lnb_ref[...] + sb * vv
        z = z_ref[0]
        sig = _sigmoid(z)
        sz = z * sig
        ycat = jnp.concatenate([o_ref[0], y_rw], axis=1)
        ycg = (ycat * sz).astype(BF16)
        out = _dot(ycg, wo_ref[...])
        hn, nx, rstd_o = _rms(out, gpost_ref[...], D)
        err = x_ref[0] + hn - tgt_ref[0]
        loss_ref[...] += jnp.sum(err * err) * (0.5 / D)
        dh = err * (1.0 / D)
        dh_ref[0] = dh
        dout, dgp = _rms_bwd(dh, nx, rstd_o, gpost_ref[...], D)
        dgpost_ref[...] += dgp
        doutb = dout.astype(BF16)
        dwo_ref[...] += _dot_tn(ycg, doutb)
        dycg = _dot(doutb, wot_ref[...])
        dz_ref[0] = dycg * ycat * (sig * (1.0 + z * (1.0 - sig)))
        dycat = dycg * sz
        dym_ref[0] = dycat[:, 0:512]
        dy_rw = dycat[:, 512:1024]
        dlnb_ref[...] += rowsum(dy_rw)
        dlng_ref[...] += rowsum(dy_rw * yhat)
        dyhat = dy_rw * ln_g
        dyw_ref[0] = rstd * (dyhat - seg(dyhat) * (1.0 / 64) - yhat * (seg(dyhat * yhat) * (1.0 / 64)))
        dsb = seg(dy_rw * vv)
        drk_ref[...] += rowsum(dsb * rv * kpv)
        dbon_ref[0, :, 0:512] = dsb * kpv * r_k
        dbon_ref[0, :, 512:1024] = dsb * rv * r_k
        dbon_ref[0, :, 1024:1536] = dy_rw * sb

    tok = lambda c: pl.BlockSpec((1, tt, c), lambda b, i: (b, i, 0))
    full = lambda a: _full(a.shape)
    ins = (x, tgt, pp, o, yw, r, kp, v, ln_g, ln_b, r_k, wo, wot, gpost, bo)
    in_specs = [tok(D), tok(D), tok(1024)] + [tok(512)] * 5 + [full(a) for a in ins[8:]]
    sd = lambda c: jax.ShapeDtypeStruct((bsz, t, c), F32)
    vec = lambda c: jax.ShapeDtypeStruct((1, c), F32)
    out_shape = [sd(D), sd(1024), sd(512), sd(512), sd(1536), jax.ShapeDtypeStruct((8, LANES), F32),
                 jax.ShapeDtypeStruct((1024, 1024), F32), vec(D), vec(512), vec(512), vec(512)]
    out_specs = [tok(D), tok(1024), tok(512), tok(512), tok(1536), _resident((8, LANES)), _resident((1024, 1024)),
                 _resident((1, D)), _resident((1, 512)), _resident((1, 512)), _resident((1, 512))]
    return pl.pallas_call(
        body, name="post", grid=(bsz, nt), out_shape=out_shape, in_specs=in_specs, out_specs=out_specs,
        compiler_params=_cparams(("arbitrary", "arbitrary")),
    )(*ins)


def _pre_bwd_a(pp, pos, invf, cqkv_w, mu, w0, w2p, w2pt, a0, a2p, a2pt, k_k, k_a, bo,
               dq, dk, dva, dwkv, dbon):
    gq, wuqt, gkv, wukvt = cqkv_w
    bsz, t, _ = pp.shape
    tt = TT_VPU
    nt = t // tt
    dr_w, dw_w, dkp_w, dv_w, dal_w, dbe_w = dwkv

    def body(pp_ref, pos_ref, invf_ref, gq_ref, wuqt_ref, gkv_ref, wukvt_ref, mu_ref, w0_ref, w2p_ref, w2pt_ref,
             a0_ref, a2p_ref, a2pt_ref, kk_ref, ka_ref, bo_ref, dq_ref, dk_ref, dva_ref,
             dr_ref, dw_ref, dkp_ref, dv_ref, dal_ref, dbe_ref, dbon_ref,
             da_ref, dwuq_ref, dwukv_ref, dw2p_ref, da2p_ref, dgq_ref, dgkv_ref, dmu_ref, dw0_ref, da0_ref,
             dkk_ref, dka_ref, carry):
        i = pl.program_id(1)
        first = (pl.program_id(0) == 0) & (i == 0)

        @pl.when(first)
        def _():
            for ref in (dwuq_ref, dwukv_ref, dw2p_ref, da2p_ref, dgq_ref, dgkv_ref, dmu_ref, dw0_ref, da0_ref,
                        dkk_ref, dka_ref):
                ref[...] = jnp.zeros(ref.shape, F32)

        bo_m = bo_ref[...]
        rowsum = lambda a: jnp.sum(a, axis=0, keepdims=True)
        prw = pp_ref[0, :, RW0:DP]

        @pl.when(i == 0)
        def _():
            carry[...] = jnp.zeros(carry.shape, F32)

        ps, sh = _shift_mix(prw, carry[7:8, :], mu_ref[...])
        carry[...] = prw[tt - 8:tt, :]
        k_k, k_a = kk_ref[...], ka_ref[...]
        g = _rw_gates(ps, w0_ref[...], w2p_ref[...], a0_ref[...], a2p_ref[...], k_k, k_a, bo_m)
        a, kk, k = g["a"], g["kk"], g["k"]
        dr = dr_ref[0] + dbon_ref[0, :, 0:512]
        dkp = dkp_ref[0] + dbon_ref[0, :, 512:1024]
        dv = dv_ref[0] + dbon_ref[0, :, 1024:1536]
        dbe = dbe_ref[0]
        dkk = dbe * a - dal_ref[0]
        da = dbe * kk + dkp * k * k_a
        dka_ref[...] += rowsum(dkp * k * (a - 1.0))
        dm = (dkk - kk * _seg(dkk * kk, bo_m)) / g["nrm"]
        dkk_ref[...] += rowsum(dm * k)
        dk_tot = dkp * (1.0 + (a - 1.0) * k_a) + dm * k_k
        dapre = da * a * (1.0 - a)
        da0_ref[...] += rowsum(dapre)
        dapb = dapre.astype(BF16)
        da2p_ref[...] += _dot_tn(g["misc"].astype(BF16), dapb)
        dwpre = dw_ref[0] * g["w"] * (-g["e"]) * _sigmoid(-g["wpre"])
        dw0_ref[...] += rowsum(dwpre)
        dwpb = dwpre.astype(BF16)
        th = g["th"]
        dw2p_ref[...] += _dot_tn(th.astype(BF16), dwpb)
        dmisc = _dot(dapb, a2pt_ref[...]) + _dot(dwpb, w2pt_ref[...]) * (1.0 - th * th)
        ang = pos_ref[0] * invf_ref[...]
        cs, sn = jnp.cos(ang), jnp.sin(ang)
        unrope = lambda gr: gr * cs - _rot(gr * sn)
        lane = lax.broadcasted_iota(jnp.int32, cs.shape, 1)
        dkr = dk_ref[0, :, 128:256]
        for h in range(1, HEADS):
            dkr = dkr + dk_ref[0, :, 256 * h + 128:256 * h + 256]
        dkr = jnp.where(lane < 64, unrope(dkr), 0.0)
        dmisc = dmisc + jnp.concatenate([dkr, jnp.zeros_like(dkr)], axis=1)
        dqp = jnp.concatenate(
            [blk for h in range(HEADS)
             for blk in (dq_ref[0, :, 256 * h:256 * h + 128], unrope(dq_ref[0, :, 256 * h + 128:256 * h + 256]))],
            axis=1).astype(BF16)
        dkvp = jnp.concatenate([dk_ref[0, :, 256 * h:256 * h + 128] for h in range(HEADS)] + [dva_ref[0]],
                               axis=1).astype(BF16)
        cqn, cq_nx, cq_rstd = _rms(pp_ref[0, :, CQ0:CQ0 + 256], gq_ref[...], 256)
        ckvn, ckv_nx, ckv_rstd = _rms(pp_ref[0, :, CKV0:CKV0 + 128], gkv_ref[...], 128)
        dwuq_ref[...] += _dot_tn(cqn.astype(BF16), dqp)
        dwukv_ref[...] += _dot_tn(ckvn.astype(BF16), dkvp)
        dcq, dgq = _rms_bwd(_dot(dqp, wuqt_ref[...]), cq_nx, cq_rstd, gq_ref[...], 256)
        dckv, dgkv = _rms_bwd(_dot(dkvp, wukvt_ref[...]), ckv_nx, ckv_rstd, gkv_ref[...], 128)
        dgq_ref[...] += dgq
        dgkv_ref[...] += dgkv
        dps = jnp.concatenate([dr, dk_tot, dv, dmisc], axis=1)
        dmu_ref[...] += rowsum(dps * (sh - prw))
        da_ref[0, :, 0:256] = dcq
        da_ref[0, :, 256:384] = dckv
        da_ref[0, :, 384:384 + NRW] = dps

    tok = lambda c: pl.BlockSpec((1, tt, c), lambda b, i: (b, i, 0))
    full = lambda a: _full(a.shape)
    ins = (pp, pos, invf, gq, wuqt, gkv, wukvt, mu, w0, w2p, w2pt, a0, a2p, a2pt, k_k, k_a, bo,
           dq, dk, dva, dr_w, dw_w, dkp_w, dv_w, dal_w, dbe_w, dbon)
    in_specs = ([tok(DP), tok(1)] + [full(a) for a in ins[2:17]] + [tok(1024), tok(1024), tok(512)]
                + [tok(512)] * 6 + [tok(1536)])
    shp = lambda *s: jax.ShapeDtypeStruct(s, F32)
    out_shape = [shp(bsz, t, 384 + NRW), shp(256, 1024), shp(128, 1024), shp(256, 512), shp(256, 512),
                 shp(1, 256), shp(1, 128), shp(1, NRW), shp(1, 512), shp(1, 512), shp(1, 512), shp(1, 512)]
    out_specs = [tok(384 + NRW)] + [_resident(s.shape) for s in out_shape[1:]]
    return pl.pallas_call(
        body, name="pre_bwd_a", grid=(bsz, nt), out_shape=out_shape, in_specs=in_specs, out_specs=out_specs,
        scratch_shapes=[pltpu.VMEM((8, NRW), F32)],
        compiler_params=_cparams(("arbitrary", "arbitrary")),
    )(*ins)


def _pre_bwd_b(x, dh, dz, da, mu, wpt, gpre):
    bsz, t, _ = x.shape
    nt = t // TT
    nblk = t // 8

    def body(x_ref, dh_ref, dz_ref, da_ref, nxt_ref, mu_ref, wpt_ref, gpre_ref, gx_ref, dp_ref, dgpre_ref):
        i = pl.program_id(1)
        first = (pl.program_id(0) == 0) & (i == 0)

        @pl.when(first)
        def _():
            dgpre_ref[...] = jnp.zeros(dgpre_ref.shape, F32)

        mu_v = mu_ref[...]
        dps = da_ref[0, :, 384:384 + NRW]
        nxt = jnp.where(i < nt - 1, nxt_ref[0, 0:1, 384:384 + NRW], 0.0)
        row = lax.broadcasted_iota(jnp.int32, dps.shape, 0)
        up = jnp.where(row == TT - 1, nxt, pltpu.roll(dps, TT - 1, 0))
        dprw = dps * (1.0 - mu_v) + up * mu_v
        dp = jnp.concatenate([dz_ref[0], da_ref[0, :, 0:384], dprw], axis=1).astype(BF16)
        dp_ref[0] = dp
        du = _dot(dp, wpt_ref[...])
        _, nx, rstd = _rms(x_ref[0], gpre_ref[...], D)
        dx, dg = _rms_bwd(du, nx, rstd, gpre_ref[...], D)
        dgpre_ref[...] += dg
        gx_ref[0] = dh_ref[0] + dx

    tok = lambda c: pl.BlockSpec((1, TT, c), lambda b, i: (b, i, 0))
    nxt_spec = pl.BlockSpec((1, 8, 384 + NRW), lambda b, i: (b, jnp.minimum((i + 1) * (TT // 8), nblk - 1), 0))
    ins = (x, dh, dz, da, da, mu, wpt, gpre)
    return pl.pallas_call(
        body, name="pre_bwd_b", grid=(bsz, nt),
        out_shape=[jax.ShapeDtypeStruct((bsz, t, D), F32), jax.ShapeDtypeStruct((bsz, t, DP), BF16),
                   jax.ShapeDtypeStruct((1, D), F32)],
        in_specs=[tok(D), tok(D), tok(1024), tok(384 + NRW), nxt_spec, _full(mu.shape), _full(wpt.shape),
                  _full(gpre.shape)],
        out_specs=[tok(D), tok(DP), _resident((1, D))],
        compiler_params=_cparams(("arbitrary", "arbitrary")),
    )(*ins)


def _tn_matmul(a, b, bn, name, bk=512):
    kdim, m = a.shape
    _, n = b.shape
    nk = kdim // bk

    def body(a_ref, b_ref, o_ref):
        @pl.when(pl.program_id(1) == 0)
        def _():
            o_ref[...] = jnp.zeros(o_ref.shape, F32)

        o_ref[...] += _dot_tn(a_ref[...], b_ref[...])

    return pl.pallas_call(
        body, name=name, grid=(n // bn, nk),
        out_shape=jax.ShapeDtypeStruct((m, n), F32),
        in_specs=[pl.BlockSpec((bk, m), lambda j, kk: (kk, 0)), pl.BlockSpec((bk, bn), lambda j, kk: (kk, j))],
        out_specs=pl.BlockSpec((m, bn), lambda j, kk: (0, j)),
        compiler_params=_cparams(("parallel", "arbitrary")),
    )(a, b)


SHARDED = ("w_in", "mla_w_uq", "mla_w_ukv", "rw_w2", "rw_a2", "w_out")
SMALL = ("norm_pre_g", "mla_q_norm_g", "mla_kv_norm_g", "rw_mu", "rw_w0", "rw_a0", "rw_k_k", "rw_k_a", "rw_r_k",
         "rw_ln_g", "rw_ln_b", "norm_post_g")
WEIGHTS = ("norm_pre_g", "w_in", "mla_q_norm_g", "mla_w_uq", "mla_kv_norm_g", "mla_w_ukv", "rw_mu", "rw_w0", "rw_w2",
           "rw_a0", "rw_a2", "rw_k_k", "rw_k_a", "rw_r_k", "rw_ln_g", "rw_ln_b", "w_out", "norm_post_g")


def _pack_small(d):
    flat = jnp.concatenate([d[n].reshape(1, -1) for n in SMALL], axis=1)
    return jnp.pad(flat, ((0, 0), (0, SMALL_ROWS * LANES - flat.shape[1]))).reshape(SMALL_ROWS, LANES)


def _unpack_small(packed, like):
    flat = packed.reshape(1, -1)
    out, at = {}, 0
    for n in SMALL:
        size = int(np.prod(like[n].shape))
        out[n] = flat[:, at:at + size].reshape(like[n].shape)
        at += size
    return out


def _unpack_shard(packed, like):
    out, at = {}, 0
    for n, rows in zip(SHARDED, PACK_ROWS):
        out[n] = packed[at:at + rows].reshape(like[n].shape)
        at += rows
    return out


def _constants():
    bo = np.kron(np.eye(2, dtype=np.float32), np.ones((64, 64), np.float32))
    inv = ROPE_THETA ** (-np.arange(0, 64, 2, dtype=np.float32) / 64)
    invf = np.concatenate([inv, inv, np.zeros(64, np.float32)]).astype(np.float32)[None, :]
    return jnp.asarray(bo, BF16), jnp.asarray(invf)


def kernel(x, positions, norm_pre_g, w_in, mla_q_norm_g, mla_w_uq, mla_kv_norm_g, mla_w_ukv, rw_mu, rw_w0, rw_w2, rw_a0, rw_a2, rw_k_k, rw_k_a, rw_r_k, rw_ln_g, rw_ln_b, w_out, norm_post_g, loss_target, m_norm_pre_g, m_w_in, m_mla_q_norm_g, m_mla_w_uq, m_mla_kv_norm_g, m_mla_w_ukv, m_rw_mu, m_rw_w0, m_rw_w2, m_rw_a0, m_rw_a2, m_rw_k_k, m_rw_k_a, m_rw_r_k, m_rw_ln_g, m_rw_ln_b, m_w_out, m_norm_post_g, v_norm_pre_g, v_w_in, v_mla_q_norm_g, v_mla_w_uq, v_mla_kv_norm_g, v_mla_w_ukv, v_rw_mu, v_rw_w0, v_rw_w2, v_rw_a0, v_rw_a2, v_rw_k_k, v_rw_k_a, v_rw_r_k, v_rw_ln_g, v_rw_ln_b, v_w_out, v_norm_post_g):
    wts = dict(norm_pre_g=norm_pre_g, w_in=w_in, mla_q_norm_g=mla_q_norm_g, mla_w_uq=mla_w_uq,
               mla_kv_norm_g=mla_kv_norm_g, mla_w_ukv=mla_w_ukv, rw_mu=rw_mu, rw_w0=rw_w0, rw_w2=rw_w2, rw_a0=rw_a0,
               rw_a2=rw_a2, rw_k_k=rw_k_k, rw_k_a=rw_k_a, rw_r_k=rw_r_k, rw_ln_g=rw_ln_g, rw_ln_b=rw_ln_b, w_out=w_out,
               norm_post_g=norm_post_g)
    mom_m = dict(norm_pre_g=m_norm_pre_g, w_in=m_w_in, mla_q_norm_g=m_mla_q_norm_g, mla_w_uq=m_mla_w_uq,
                 mla_kv_norm_g=m_mla_kv_norm_g, mla_w_ukv=m_mla_w_ukv, rw_mu=m_rw_mu, rw_w0=m_rw_w0, rw_w2=m_rw_w2,
                 rw_a0=m_rw_a0, rw_a2=m_rw_a2, rw_k_k=m_rw_k_k, rw_k_a=m_rw_k_a, rw_r_k=m_rw_r_k, rw_ln_g=m_rw_ln_g,
                 rw_ln_b=m_rw_ln_b, w_out=m_w_out, norm_post_g=m_norm_post_g)
    mom_v = dict(norm_pre_g=v_norm_pre_g, w_in=v_w_in, mla_q_norm_g=v_mla_q_norm_g, mla_w_uq=v_mla_w_uq,
                 mla_kv_norm_g=v_mla_kv_norm_g, mla_w_ukv=v_mla_w_ukv, rw_mu=v_rw_mu, rw_w0=v_rw_w0, rw_w2=v_rw_w2,
                 rw_a0=v_rw_a0, rw_a2=v_rw_a2, rw_k_k=v_rw_k_k, rw_k_a=v_rw_k_a, rw_r_k=v_rw_r_k, rw_ln_g=v_rw_ln_g,
                 rw_ln_b=v_rw_ln_b, w_out=v_w_out, norm_post_g=v_norm_post_g)
    bsz, t, _ = x.shape
    bo, invf = _constants()
    c_idx = lax.axis_index("c")

    g_in, g_uq, g_ukv, g_w2, g_a2, g_out = _ag_weights([wts[n][0] for n in SHARDED])
    w_in_f = jnp.transpose(g_in, (1, 0, 2)).reshape(D, D_IN)
    wp = jnp.concatenate([w_in_f[:, 2112:3136], w_in_f[:, 0:384], w_in_f[:, 448:1984], w_in_f[:, 384:448],
                          w_in_f[:, 1984:2112], jnp.zeros((D, 64), BF16)], axis=1)
    wuq = jnp.pad(jnp.transpose(g_uq, (1, 0, 2)).reshape(256, HEADS, 192), ((0, 0), (0, 0), (0, 64))).reshape(256, 1024)
    wukv = jnp.transpose(jnp.transpose(g_ukv, (1, 0, 2)).reshape(128, HEADS, 2, 128), (0, 2, 1, 3)).reshape(128, 1024)
    w2 = jnp.transpose(g_w2, (1, 0, 2)).reshape(64, RW)
    a2 = jnp.transpose(g_a2, (1, 0, 2)).reshape(64, RW)
    w2p = jnp.pad(w2, ((64, 128), (0, 0)))
    a2p = jnp.pad(a2, ((128, 64), (0, 0)))
    wo = g_out.reshape(D, D)
    mu = jnp.concatenate([rw_mu[:, 0:1536], jnp.zeros((1, 64), F32), rw_mu[:, 1536:1664], jnp.zeros((1, 64), F32)],
                         axis=1)
    r_k = rw_r_k.reshape(1, RW)
    pos = positions.astype(F32)[:, :, None]

    (u, pp, q_att, k_att, v_att, r, w, kp, v, al, be) = _pre_fwd(
        x, pos, invf, norm_pre_g, wp, mla_q_norm_g, wuq, mla_kv_norm_g, wukv, mu, rw_w0, w2p, rw_a0, a2p, rw_k_k,
        rw_k_a, bo)
    o, lse = _attn_fwd(q_att, k_att, v_att)
    rw_k = _spread_k([r, w, kp, al, be])
    v_v = _to_v(v)
    yw_v, states, u_v = _wkv_fwd(*rw_k, v_v)
    yw = _from_v(yw_v, bsz)

    (dh, dz, dym, dyw, dbon, loss_acc, d_wo, d_gpost, d_lng, d_lnb, d_rk) = _post(
        x, loss_target, pp, o, yw, r, kp, v, rw_ln_g, rw_ln_b, r_k, wo, wo.T, norm_post_g, bo)
    loss = lax.psum(loss_acc[0, 0], ("x", "y", "c"))

    d_k = _wkv_bwd(*rw_k, v_v, _to_v(dyw), states, u_v)
    dr_w, dw_w, dkp_w, dal_w, dbe_w = _gather_k(d_k[:5], bsz)
    dwkv = (dr_w, dw_w, dkp_w, _from_v(d_k[5], bsz), dal_w, dbe_w)
    dq, dk, dva = _attn_bwd(q_att, k_att, v_att, o, lse, dym)

    (da, d_wuq, d_wukv, d_w2p, d_a2p, d_gq, d_gkv, d_mu, d_w0, d_a0, d_kk, d_ka) = _pre_bwd_a(
        pp, pos, invf, (mla_q_norm_g, wuq.T, mla_kv_norm_g, wukv.T), mu, rw_w0, w2p, w2p.T, rw_a0, a2p, a2p.T,
        rw_k_k, rw_k_a, bo, dq, dk, dva, dwkv, dbon)
    grad_x, dpb, d_gpre = _pre_bwd_b(x, dh, dz, da, mu, wp.T, norm_pre_g)
    d_wp = _tn_matmul(u.reshape(bsz * t, D), dpb.reshape(bsz * t, DP), DP, "dw_in")

    full_g = {
        "w_in": jnp.concatenate([d_wp[:, 1024:1408], d_wp[:, 2944:3008], d_wp[:, 1408:2944], d_wp[:, 3008:3136],
                                 d_wp[:, 0:1024]], axis=1),
        "mla_w_uq": d_wuq.reshape(256, HEADS, 256)[:, :, :192].reshape(256, 768),
        "mla_w_ukv": jnp.transpose(d_wukv.reshape(128, 2, HEADS, 128), (0, 2, 1, 3)).reshape(128, 1024),
        "rw_w2": d_w2p[64:128],
        "rw_a2": d_a2p[128:192],
        "w_out": d_wo,
    }
    small_g = {
        "norm_pre_g": d_gpre, "mla_q_norm_g": d_gq, "mla_kv_norm_g": d_gkv,
        "rw_mu": jnp.concatenate([d_mu[:, 0:1536], d_mu[:, 1600:1728]], axis=1),
        "rw_w0": d_w0, "rw_a0": d_a0, "rw_k_k": d_kk, "rw_k_a": d_ka, "rw_r_k": d_rk, "rw_ln_g": d_lng,
        "rw_ln_b": d_lnb, "norm_post_g": d_gpost,
    }

    def by_shard(name, g):
        if name == "w_out":
            return g.reshape(N_SHARD, -1, LANES)
        rows, cols = g.shape
        return jnp.transpose(g.reshape(rows, N_SHARD, cols // N_SHARD), (1, 0, 2)).reshape(N_SHARD, -1, LANES)

    packed = jnp.concatenate([by_shard(n, full_g[n]) for n in SHARDED], axis=1)
    halves = packed.reshape(N_SHARD, 2, HALF, LANES)
    keep = lax.dynamic_index_in_dim(halves, c_idx, 1, keepdims=False)
    give = lax.dynamic_index_in_dim(halves, 1 - c_idx, 1, keepdims=False)
    got = _rs_pair_exchange(give)
    pair_sum, pair_sum_b = _add_n([keep.reshape(-1, LANES), got.reshape(-1, LANES)], "rs_pair_sum", SUM_ROWS,
                                  also_bf16=True)
    g_shard = _rs_chips(pair_sum.reshape(N_SHARD, HALF, LANES),
                        pair_sum_b.reshape(N_SHARD, HALF, LANES)).reshape(PACK_TOTAL, LANES)

    g_small = _small_allreduce(_pack_small(small_g))

    g_sharded = _unpack_shard(g_shard, {n: wts[n][0] for n in SHARDED})
    sh = _adamw([wts[n][0] for n in SHARDED], [g_sharded[n] for n in SHARDED], [mom_m[n][0] for n in SHARDED],
                [mom_v[n][0] for n in SHARDED], "adamw_sharded")
    sm = _adamw([_pack_small(wts)], [g_small], [_pack_small(mom_m)], [_pack_small(mom_v)], "adamw_small")

    def outputs(sharded, small):
        out = {n: a[None] for n, a in zip(SHARDED, sharded)}
        out.update(_unpack_small(small, wts))
        return out

    grads = outputs([g_sharded[n] for n in SHARDED], g_small)
    deltas, new_m, new_v = (outputs(sh[k], sm[k][0]) for k in range(3))
    return (loss, grad_x, *[grads[n] for n in WEIGHTS], *[deltas[n] for n in WEIGHTS],
            *[new_m[n] for n in WEIGHTS], *[new_v[n] for n in WEIGHTS])
```

```python
import functools

import numpy as np
import jax
import jax.numpy as jnp
from jax import lax
from jax.experimental import pallas as pl
from jax.experimental.pallas import tpu as pltpu

F32, BF16 = jnp.float32, jnp.bfloat16
MESH = pl.DeviceIdType.MESH

D = 1024
HEADS = 4
RW = 512
NORM_EPS = 1e-6
GN_EPS = 64e-5
ROPE_THETA = 10000.0
SCALE = (128 + 64) ** -0.5
D_IN = 3136
LR, B1, B2, ADAM_EPS, WD, STEP = 0.001, 0.9, 0.999, 1e-08, 0.01, 10

Z0, CQ0, CKV0, RW0, DP = 0, 1024, 1280, 1408, 3200
NRW = DP - RW0

LANES = 128
SUBLANES = 8
VMEM_LIMIT = 56 * 1024 * 1024

TT = 512
TT_VPU = 256
TQ = 512

N_SHARD = 4
PACK_ROWS = (1024 * 784 // 128, 256 * 192 // 128, 128 * 256 // 128, 64, 64, 256 * 1024 // 128)
PACK_TOTAL = sum(PACK_ROWS)
HALF = PACK_TOTAL // 2
SUM_ROWS = HALF // 4
SMALL_ROWS = 64
SMALL_USED = 60


def _cparams(sem=None):
    return pltpu.CompilerParams(dimension_semantics=sem, vmem_limit_bytes=VMEM_LIMIT)


def _full(shape):
    n = len(shape)
    return pl.BlockSpec(shape, lambda *_: (0,) * n, pipeline_mode=pl.Buffered(1))


def _resident(shape):
    n = len(shape)
    return pl.BlockSpec(shape, lambda *_: (0,) * n)


def _dot(a, b):
    return jnp.dot(a, b, preferred_element_type=F32)


def _dot_nt(a, b):
    return lax.dot_general(a, b, (((1,), (1,)), ((), ())), preferred_element_type=F32)


def _dot_tn(a, b):
    return lax.dot_general(a, b, (((0,), (0,)), ((), ())), preferred_element_type=F32)


def _split3(x):
    hi = x.astype(BF16)
    r1 = x - hi.astype(F32)
    mid = r1.astype(BF16)
    lo = (r1 - mid.astype(F32)).astype(BF16)
    return hi, mid, lo


def _seg(x, bo):
    rows, nblk = x.shape[0], x.shape[1] // LANES
    pieces = [p for i in range(nblk) for p in _split3(x[:, LANES * i:LANES * (i + 1)])]
    res = _dot(jnp.concatenate(pieces, axis=0), bo)
    parts = [res[(3 * i) * rows:(3 * i + 1) * rows] + res[(3 * i + 1) * rows:(3 * i + 2) * rows]
             + res[(3 * i + 2) * rows:(3 * i + 3) * rows] for i in range(nblk)]
    return parts[0] if nblk == 1 else jnp.concatenate(parts, axis=1)


def _rms(x, g, n):
    rstd = lax.rsqrt(jnp.sum(x * x, axis=-1, keepdims=True) * (1.0 / n) + NORM_EPS)
    nx = x * rstd
    return nx * g, nx, rstd


def _rms_bwd(dy, nx, rstd, g, n):
    dn = dy * g
    dx = rstd * (dn - nx * (jnp.sum(dn * nx, axis=-1, keepdims=True) * (1.0 / n)))
    return dx, jnp.sum(dy * nx, axis=0, keepdims=True)


def _rot(x):
    lane = lax.broadcasted_iota(jnp.int32, x.shape, 1)
    return jnp.where((lane % 64) < 32, -pltpu.roll(x, x.shape[1] - 32, 1), pltpu.roll(x, 32, 1))


def _sigmoid(x):
    return 1.0 / (1.0 + jnp.exp(-x))


def _softplus(x):
    return jnp.maximum(x, 0.0) + jnp.log(1.0 + jnp.exp(-jnp.abs(x)))


def _rw_gates(ps, w0, w2p, a0, a2p, k_k, k_a, bo):
    r, k, v, misc = ps[:, 0:512], ps[:, 512:1024], ps[:, 1024:1536], ps[:, 1536:NRW]
    th = jnp.tanh(misc)
    wpre = w0 + _dot(th.astype(BF16), w2p)
    e = jnp.exp(-_softplus(-wpre) - 0.5)
    w = jnp.exp(-e)
    a = _sigmoid(a0 + _dot(misc.astype(BF16), a2p))
    m = k * k_k
    nrm = jnp.maximum(jnp.sqrt(_seg(m * m, bo)), 1e-12)
    kk = m / nrm
    kp = k * (1.0 + (a - 1.0) * k_a)
    return dict(r=r, k=k, v=v, misc=misc, th=th, wpre=wpre, e=e, w=w, a=a, nrm=nrm, kk=kk, kp=kp)


def _shift_mix(prw, prev_row, mu):
    row = lax.broadcasted_iota(jnp.int32, prw.shape, 0)
    sh = jnp.where(row == 0, prev_row, pltpu.roll(prw, 1, 0))
    return prw + (sh - prw) * mu, sh


def _ag_weights(shards):
    n = len(shards)

    def body(*refs):
        ins, outs = refs[:n], refs[n:2 * n]
        ici_send, ici_recv, d2d_send, d2d_recv = refs[2 * n:2 * n + 4]
        x, y, c = lax.axis_index("x"), lax.axis_index("y"), lax.axis_index("c")
        mine = 2 * x + y
        for w in range(n):
            outs[w][mine] = ins[w][...].astype(BF16)
        flips = ((1, 0), (0, 1), (1, 1))

        def half(w, shard, cc):
            rows = outs[w].shape[1] // 2
            return outs[w].at[shard, pl.ds(pl.multiple_of(cc * rows, 16), rows)]

        def ici(w, k, shard):
            fx, fy = flips[k]
            return pltpu.make_async_remote_copy(
                src_ref=half(w, shard, c), dst_ref=half(w, shard, c),
                send_sem=ici_send.at[w * 3 + k], recv_sem=ici_recv.at[w * 3 + k],
                device_id=(x ^ fx, y ^ fy, c), device_id_type=MESH)

        def d2d(w, k, cc):
            fx, fy = flips[k]
            theirs = 2 * (x ^ fx) + (y ^ fy)
            return pltpu.make_async_remote_copy(
                src_ref=half(w, theirs, cc), dst_ref=half(w, theirs, cc),
                send_sem=d2d_send.at[w * 3 + k], recv_sem=d2d_recv.at[w * 3 + k],
                device_id=(x, y, 1 - c), device_id_type=MESH)

        for w in range(n):
            for k in range(3):
                ici(w, k, mine).start()
        for w in range(n):
            for k in range(3):
                fx, fy = flips[k]
                ici(w, k, 2 * (x ^ fx) + (y ^ fy)).wait_recv()
                d2d(w, k, c).start()
        for w in range(n):
            for k in range(3):
                d2d(w, k, 1 - c).wait_recv()
        for w in range(n):
            for k in range(3):
                ici(w, k, mine).wait_send()
                d2d(w, k, c).wait_send()

    vm = pl.BlockSpec(memory_space=pltpu.VMEM)
    return pl.pallas_call(
        body, name="ag_weights",
        out_shape=[jax.ShapeDtypeStruct((N_SHARD,) + s.shape, BF16) for s in shards],
        in_specs=[vm] * n, out_specs=[vm] * n,
        scratch_shapes=[pltpu.SemaphoreType.DMA((3 * n,))] * 4,
        compiler_params=pltpu.CompilerParams(vmem_limit_bytes=VMEM_LIMIT),
    )(*shards)


def _rs_pair_exchange(send_half):
    def body(src_ref, dst_ref, send_sem, recv_sem):
        x, y, c = lax.axis_index("x"), lax.axis_index("y"), lax.axis_index("c")
        cp = pltpu.make_async_remote_copy(src_ref=src_ref, dst_ref=dst_ref, send_sem=send_sem, recv_sem=recv_sem,
                                          device_id=(x, y, 1 - c), device_id_type=MESH)
        cp.start()
        cp.wait()

    hbm = pl.BlockSpec(memory_space=pl.ANY)
    return pl.pallas_call(
        body, name="rs_pair_exchange",
        out_shape=jax.ShapeDtypeStruct(send_half.shape, send_half.dtype),
        in_specs=[hbm], out_specs=hbm,
        scratch_shapes=[pltpu.SemaphoreType.DMA, pltpu.SemaphoreType.DMA],
    )(send_half)


def _rs_chips(part_f32, part_bf16):
    def body(own_ref, src_ref, out_ref, recv, ici_send, ici_recv, d2d_send, d2d_recv):
        x, y, c = lax.axis_index("x"), lax.axis_index("y"), lax.axis_index("c")
        mine = 2 * x + y
        flips = ((1, 0), (0, 1), (1, 1))
        cps = []
        for k, (fx, fy) in enumerate(flips):
            theirs = 2 * (x ^ fx) + (y ^ fy)
            cps.append(pltpu.make_async_remote_copy(
                src_ref=src_ref.at[theirs], dst_ref=recv.at[k],
                send_sem=ici_send.at[k], recv_sem=ici_recv.at[k],
                device_id=(x ^ fx, y ^ fy, c), device_id_type=MESH))
        for cp in cps:
            cp.start()
        acc = own_ref[mine]
        for k, cp in enumerate(cps):
            cp.wait_recv()
            acc = acc + recv[k].astype(F32)
        out_ref[c] = acc
        to_sibling = pltpu.make_async_remote_copy(
            src_ref=out_ref.at[c], dst_ref=out_ref.at[c], send_sem=d2d_send, recv_sem=d2d_recv,
            device_id=(x, y, 1 - c), device_id_type=MESH)
        to_sibling.start()
        pltpu.make_async_remote_copy(
            src_ref=out_ref.at[1 - c], dst_ref=out_ref.at[1 - c], send_sem=d2d_send, recv_sem=d2d_recv,
            device_id=(x, y, 1 - c), device_id_type=MESH).wait_recv()
        to_sibling.wait_send()
        for cp in cps:
            cp.wait_send()

    vm = pl.BlockSpec(memory_space=pltpu.VMEM)
    return pl.pallas_call(
        body, name="rs_chips",
        out_shape=jax.ShapeDtypeStruct((2,) + part_f32.shape[1:], F32),
        in_specs=[vm, vm], out_specs=vm,
        scratch_shapes=[pltpu.VMEM((3,) + part_bf16.shape[1:], BF16), pltpu.SemaphoreType.DMA((3,)),
                        pltpu.SemaphoreType.DMA((3,)), pltpu.SemaphoreType.DMA, pltpu.SemaphoreType.DMA],
        compiler_params=pltpu.CompilerParams(vmem_limit_bytes=VMEM_LIMIT),
    )(part_f32, part_bf16)


def _small_allreduce(vec):
    def body(in_ref, out_ref, recv, send_sems, recv_sems):
        x, y, c = lax.axis_index("x"), lax.axis_index("y"), lax.axis_index("c")
        me = 4 * x + 2 * y + c
        cps = []
        for k in range(1, 8):
            fx, fy, fc = (k >> 2) & 1, (k >> 1) & 1, k & 1
            cps.append(pltpu.make_async_remote_copy(
                src_ref=in_ref, dst_ref=recv.at[k - 1],
                send_sem=send_sems.at[k - 1], recv_sem=recv_sems.at[k - 1],
                device_id=(x ^ fx, y ^ fy, c ^ fc), device_id_type=MESH))
        for cp in cps:
            cp.start()
        for cp in cps:
            cp.wait()
        acc = jnp.zeros(in_ref.shape, F32)
        for j in range(8):
            slot = jnp.maximum((me ^ j) - 1, 0)
            acc = acc + jnp.where(me == j, in_ref[...], recv[slot])
        out_ref[...] = acc

    vm = pl.BlockSpec(memory_space=pltpu.VMEM)
    return pl.pallas_call(
        body, name="small_allreduce",
        out_shape=jax.ShapeDtypeStruct(vec.shape, F32),
        in_specs=[vm], out_specs=vm,
        scratch_shapes=[pltpu.VMEM((7,) + vec.shape, F32), pltpu.SemaphoreType.DMA((7,)),
                        pltpu.SemaphoreType.DMA((7,))],
    )(vec)


def _add_n(arrs, name, rows, also_bf16=False):
    n = len(arrs)
    r = arrs[0].shape[0]

    def body(*refs):
        acc = refs[0][...].astype(F32)
        for k in range(1, n):
            acc = acc + refs[k][...].astype(F32)
        refs[n][...] = acc
        if also_bf16:
            refs[n + 1][...] = acc.astype(BF16)

    spec = pl.BlockSpec((rows, LANES), lambda i: (i, 0))
    out_shape = [jax.ShapeDtypeStruct(arrs[0].shape, F32)]
    if also_bf16:
        out_shape.append(jax.ShapeDtypeStruct(arrs[0].shape, BF16))
    return pl.pallas_call(
        body, name=name, grid=(r // rows,),
        out_shape=out_shape,
        in_specs=[spec] * n, out_specs=[spec] * len(out_shape),
        compiler_params=_cparams(("parallel",)),
    )(*arrs)


ADAM_ROWS = 64


def _adamw(ws, gs, ms, vs, name):
    n = len(ws)

    def body(*refs):
        for i in range(n):
            w_ref, g_ref, m_ref, v_ref = (refs[k * n + i] for k in range(4))
            d_ref, nm_ref, nv_ref = (refs[(4 + k) * n + i] for k in range(3))
            rows = min(ADAM_ROWS, w_ref.shape[0])

            def chunk(r, _):
                at = pl.ds(pl.multiple_of(r * rows, SUBLANES), rows)
                gg = g_ref[at, :]
                nm = B1 * m_ref[at, :] + (1.0 - B1) * gg
                nv = B2 * v_ref[at, :] + (1.0 - B2) * (gg * gg)
                m_hat = nm / (1.0 - B1 ** STEP)
                v_hat = nv / (1.0 - B2 ** STEP)
                d_ref[at, :] = -LR * (m_hat / (jnp.sqrt(v_hat) + ADAM_EPS) + WD * w_ref[at, :])
                nm_ref[at, :] = nm
                nv_ref[at, :] = nv
                return 0

            lax.fori_loop(0, w_ref.shape[0] // rows, chunk, 0)

    vm = pl.BlockSpec(memory_space=pltpu.VMEM)
    sds = [jax.ShapeDtypeStruct(w.shape, F32) for w in ws]
    outs = pl.pallas_call(
        body, name=name, out_shape=sds * 3, in_specs=[vm] * (4 * n), out_specs=[vm] * (3 * n),
        compiler_params=pltpu.CompilerParams(vmem_limit_bytes=VMEM_LIMIT),
    )(*ws, *gs, *ms, *vs)
    return outs[:n], outs[n:2 * n], outs[2 * n:]


def _pre_fwd(x, pos, invf, gpre, wp, gq, wuq, gkv, wukv, mu, w0, w2p, a0, a2p, k_k, k_a, bo):
    bsz, t, _ = x.shape
    nt = t // TT

    def body(x_ref, pos_ref, invf_ref, gpre_ref, wp_ref, gq_ref, wuq_ref, gkv_ref, wukv_ref, mu_ref, w0_ref,
             w2p_ref, a0_ref, a2p_ref, kk_ref, ka_ref, bo_ref,
             u_ref, pp_ref, q_ref, k_ref, v_ref, r_o, w_o, kp_o, vv_o, al_o, be_o, carry):
        i = pl.program_id(1)
        u, _, _ = _rms(x_ref[0], gpre_ref[...], D)
        ub = u.astype(BF16)
        u_ref[0] = ub
        p = _dot(ub, wp_ref[...])
        pp_ref[0] = p
        prw = p[:, RW0:DP]

        @pl.when(i == 0)
        def _():
            carry[...] = jnp.zeros(carry.shape, F32)

        ps, _ = _shift_mix(prw, carry[7:8, :], mu_ref[...])
        carry[...] = prw[TT - 8:TT, :]

        g = _rw_gates(ps, w0_ref[...], w2p_ref[...], a0_ref[...], a2p_ref[...], kk_ref[...], ka_ref[...],
                      bo_ref[...])
        r_o[0] = g["r"]
        w_o[0] = g["w"]
        kp_o[0] = g["kp"]
        vv_o[0] = g["v"]
        al_o[0] = -g["kk"]
        be_o[0] = g["kk"] * g["a"]

        cqn, _, _ = _rms(p[:, CQ0:CQ0 + 256], gq_ref[...], 256)
        q = _dot(cqn.astype(BF16), wuq_ref[...])
        ckvn, _, _ = _rms(p[:, CKV0:CKV0 + 128], gkv_ref[...], 128)
        kv = _dot(ckvn.astype(BF16), wukv_ref[...])
        ang = pos_ref[0] * invf_ref[...]
        cs, sn = jnp.cos(ang), jnp.sin(ang)
        lane = lax.broadcasted_iota(jnp.int32, cs.shape, 1)
        kr = ps[:, 1536:1536 + LANES]
        kr = jnp.where(lane < 64, kr * cs + _rot(kr) * sn, 0.0).astype(BF16)
        for h in range(HEADS):
            qr = q[:, 256 * h + 128:256 * h + 256]
            q_ref[0, :, 256 * h:256 * h + 128] = q[:, 256 * h:256 * h + 128].astype(BF16)
            q_ref[0, :, 256 * h + 128:256 * h + 256] = (qr * cs + _rot(qr) * sn).astype(BF16)
            k_ref[0, :, 256 * h:256 * h + 128] = kv[:, 128 * h:128 * h + 128].astype(BF16)
            k_ref[0, :, 256 * h + 128:256 * h + 256] = kr
        v_ref[0] = kv[:, 512:1024].astype(BF16)

    tok = lambda c: pl.BlockSpec((1, TT, c), lambda b, i: (b, i, 0))
    full = lambda a: _full(a.shape)
    ins = (x, pos, invf, gpre, wp, gq, wuq, gkv, wukv, mu, w0, w2p, a0, a2p, k_k, k_a, bo)
    in_specs = [tok(D), tok(1)] + [full(a) for a in ins[2:]]
    sd = lambda c, dt: jax.ShapeDtypeStruct((bsz, t, c), dt)
    out_shape = [sd(D, BF16), sd(DP, F32), sd(1024, BF16), sd(1024, BF16), sd(512, BF16)] + [sd(RW, F32)] * 6
    out_specs = [tok(D), tok(DP), tok(1024), tok(1024), tok(512)] + [tok(RW)] * 6
    return pl.pallas_call(
        body, name="pre_fwd", grid=(bsz, nt), out_shape=out_shape, in_specs=in_specs, out_specs=out_specs,
        scratch_shapes=[pltpu.VMEM((8, NRW), F32)],
        compiler_params=_cparams(("arbitrary", "arbitrary")),
    )(*ins)


def _attn_fwd(q, k, v):
    bsz, t, _ = q.shape
    nq = t // TQ

    def body(q_ref, k_ref, v_ref, o_ref, lse_ref):
        i = pl.program_id(2)

        def step(j, carry, diagonal):
            at = pl.ds(pl.multiple_of(j * TQ, TQ), TQ)
            out = []
            for hh in range(2):
                m, l, acc = carry[hh]
                s = _dot_nt(q_ref[0, :, 256 * hh:256 * (hh + 1)], k_ref[0, at, 256 * hh:256 * (hh + 1)]) * SCALE
                if diagonal:
                    s = jnp.where(lax.broadcasted_iota(jnp.int32, (TQ, TQ), 1)
                                  <= lax.broadcasted_iota(jnp.int32, (TQ, TQ), 0), s, -1e30)
                mn = jnp.maximum(m, jnp.max(s, axis=1, keepdims=True))
                p = jnp.exp(s - mn)
                al = jnp.exp(m - mn)
                l = al * l + jnp.sum(p, axis=1, keepdims=True)
                acc = al * acc + _dot(p.astype(BF16), v_ref[0, at, LANES * hh:LANES * (hh + 1)])
                out.append((mn, l, acc))
            return tuple(out)

        start = (jnp.full((TQ, 1), -1e30, F32), jnp.zeros((TQ, 1), F32), jnp.zeros((TQ, LANES), F32))
        before = lax.fori_loop(0, i, lambda j, carry: step(j, carry, False), (start, start))
        for hh, (m, l, acc) in enumerate(step(i, before, True)):
            o_ref[0, :, LANES * hh:LANES * (hh + 1)] = acc / l
            lse_ref[0, hh] = jnp.broadcast_to(m + jnp.log(l), (TQ, LANES))

    return pl.pallas_call(
        body, name="attn_fwd", grid=(bsz, HEADS // 2, nq),
        out_shape=[jax.ShapeDtypeStruct((bsz, t, 512), F32), jax.ShapeDtypeStruct((bsz, HEADS, t, LANES), F32)],
        in_specs=[pl.BlockSpec((1, TQ, 512), lambda b, h, i: (b, i, h)),
                  pl.BlockSpec((1, t, 512), lambda b, h, i: (b, 0, h)),
                  pl.BlockSpec((1, t, 256), lambda b, h, i: (b, 0, h))],
        out_specs=[pl.BlockSpec((1, TQ, 256), lambda b, h, i: (b, i, h)),
                   pl.BlockSpec((1, 2, TQ, LANES), lambda b, h, i: (b, h, i, 0))],
        compiler_params=_cparams(("parallel", "parallel", "arbitrary")),
    )(q, k, v)


def _attn_bwd(q, k, v, o, lse, do):
    bsz, t, _ = q.shape
    nq = t // TQ

    def body(q_ref, k_ref, v_ref, o_ref, lse_ref, do_ref, dq_ref, dk_ref, dv_ref, dl_ref):
        j = pl.program_id(2)

        @pl.when(j == 0)
        def _():
            def prep(i, _):
                at = pl.ds(pl.multiple_of(i * TQ, TQ), TQ)
                for hh in range(2):
                    lanes = slice(LANES * hh, LANES * (hh + 1))
                    dl_ref[hh, at, :] = jnp.broadcast_to(
                        jnp.sum(do_ref[0, at, lanes] * o_ref[0, at, lanes], axis=1, keepdims=True), (TQ, LANES))
                return 0

            lax.fori_loop(0, nq, prep, 0)
            dq_ref[0] = jnp.zeros((t, 512), F32)

        def q_tile(i, carry, diagonal):
            atq = pl.ds(pl.multiple_of(i * TQ, TQ), TQ)
            out = []
            for hh in range(2):
                dk, dv = carry[hh]
                wide, narrow = slice(256 * hh, 256 * (hh + 1)), slice(LANES * hh, LANES * (hh + 1))
                qt, kt, vt = q_ref[0, atq, wide], k_ref[0, :, wide], v_ref[0, :, narrow]
                dob = do_ref[0, atq, narrow].astype(BF16)
                s = _dot_nt(qt, kt) * SCALE
                if diagonal:
                    s = jnp.where(lax.broadcasted_iota(jnp.int32, (TQ, TQ), 1)
                                  <= lax.broadcasted_iota(jnp.int32, (TQ, TQ), 0), s, -1e30)
                p = jnp.exp(s - lse_ref[0, hh, atq, :][:, 0:1])
                dv = dv + _dot_tn(p.astype(BF16), dob)
                dp = _dot_nt(dob, vt)
                ds = (p * (dp - dl_ref[hh, atq, :][:, 0:1]) * SCALE).astype(BF16)
                dk = dk + _dot_tn(ds, qt)
                dq_ref[0, atq, wide] += _dot(ds, kt)
                out.append((dk, dv))
            return tuple(out)

        zero = (jnp.zeros((TQ, 256), F32), jnp.zeros((TQ, LANES), F32))
        first = q_tile(j, (zero, zero), True)
        done = lax.fori_loop(j + 1, nq, lambda i, carry: q_tile(i, carry, False), first)
        for hh, (dk, dv) in enumerate(done):
            dk_ref[0, :, 256 * hh:256 * (hh + 1)] = dk
            dv_ref[0, :, LANES * hh:LANES * (hh + 1)] = dv

    whole = lambda c: pl.BlockSpec((1, t, c), lambda b, h, j: (b, 0, h))
    tile = lambda c: pl.BlockSpec((1, TQ, c), lambda b, h, j: (b, j, h))
    return pl.pallas_call(
        body, name="attn_bwd", grid=(bsz, HEADS // 2, nq),
        out_shape=[jax.ShapeDtypeStruct((bsz, t, 1024), F32), jax.ShapeDtypeStruct((bsz, t, 1024), F32),
                   jax.ShapeDtypeStruct((bsz, t, 512), F32)],
        in_specs=[whole(512), tile(512), tile(256), whole(256),
                  pl.BlockSpec((1, 2, t, LANES), lambda b, h, j: (b, h, 0, 0)), whole(256)],
        out_specs=[whole(512), tile(512), tile(256)],
        scratch_shapes=[pltpu.VMEM((2, t, LANES), F32)],
        compiler_params=_cparams(("parallel", "parallel", "arbitrary")),
    )(q, k, v, o, lse, do)


RW_HEADS = 8
CH = 16


def _lane_split(bsz):
    vs = LANES // (bsz * RW_HEADS)
    return vs, 64 // vs


def _gather_matrix(bsz):
    group = bsz * RW_HEADS
    vs = LANES // group
    half = (RW_HEADS // 2) * bsz * SPREAD_STEPS
    p = np.zeros((SPREAD_STEPS // vs * LANES, 2 * half), np.float32)
    for g2 in range(SPREAD_STEPS // vs):
        for j in range(vs):
            for b in range(bsz):
                for h in range(RW_HEADS):
                    hp, hpar = h // 2, h % 2
                    p[g2 * LANES + j * group + b * RW_HEADS + h,
                      hpar * half + (hp * bsz + b) * SPREAD_STEPS + g2 * vs + j] = 1.0
    return jnp.asarray(np.concatenate([p] * 3, axis=0), BF16)


def _gather_k(ys, bsz):
    vs = LANES // (bsz * RW_HEADS)
    assert (RW_HEADS // 2) * bsz * SPREAD_STEPS == LANES, "the transposed tile must be 128 lanes wide"
    tg = ys[0].shape[0]
    n = len(ys)
    ngrp = SPREAD_BLOCK // SPREAD_STEPS
    per = SPREAD_STEPS // vs

    def body(*refs):
        pm = refs[n][...]
        for y_ref, o_ref in zip(refs[:n], refs[n + 1:]):
            lhs = jnp.concatenate(
                [jnp.concatenate(_split3(jnp.concatenate([y_ref[per * m + g2] for g2 in range(per)], axis=1)), axis=1)
                 for m in range(ngrp)], axis=0)
            a = _dot(lhs, pm)
            for m in range(ngrp):
                am = a[64 * m:64 * (m + 1)]
                bt = jnp.concatenate([am[:, 0:LANES], am[:, LANES:2 * LANES]], axis=0).T
                for hp in range(RW_HEADS // 2):
                    for b in range(bsz):
                        at = (hp * bsz + b) * SPREAD_STEPS
                        o_ref[b, SPREAD_STEPS * m:SPREAD_STEPS * (m + 1), LANES * hp:LANES * (hp + 1)] = \
                            bt[at:at + SPREAD_STEPS]

    pm = _gather_matrix(bsz)
    return pl.pallas_call(
        body, name="wkv_gather", grid=(tg * vs // SPREAD_BLOCK,),
        out_shape=[jax.ShapeDtypeStruct((bsz, tg * vs, RW), F32)] * n,
        in_specs=[pl.BlockSpec((SPREAD_BLOCK // vs, 64, LANES), lambda i: (i, 0, 0))] * n + [_full(pm.shape)],
        out_specs=[pl.BlockSpec((bsz, SPREAD_BLOCK, RW), lambda i: (0, i, 0))] * n,
        compiler_params=_cparams(("parallel",)),
    )(*ys, pm)


def _to_v(x):
    bsz, t, _ = x.shape
    vs, vq = _lane_split(bsz)
    return jnp.transpose(x.reshape(bsz, t, RW_HEADS, vq, vs), (1, 3, 4, 0, 2)).reshape(t, vq, LANES)


def _from_v(y, bsz):
    t = y.shape[0]
    vs, vq = _lane_split(bsz)
    return jnp.transpose(y.reshape(t, vq, vs, bsz, RW_HEADS), (3, 0, 4, 1, 2)).reshape(bsz, t, RW)


def _ksum(a):
    return jnp.sum(a, axis=0, keepdims=True)


def _fold(a, group):
    sh = LANES // 2
    while sh >= group:
        a = a + pltpu.roll(a, sh, 1)
        sh //= 2
    return a


def _lane_group(shape, group):
    return lax.broadcasted_iota(jnp.int32, shape, 1) // group


SPREAD_STEPS = 8
SPREAD_BLOCK = 32


def _spread_matrix(bsz):
    group = bsz * RW_HEADS
    vs = LANES // group
    rows = (RW_HEADS // 2) * bsz * SPREAD_STEPS
    q = np.zeros((2, rows, SPREAD_STEPS * LANES), np.float32)
    for hpar in range(2):
        for hp in range(RW_HEADS // 2):
            for b in range(bsz):
                for st in range(SPREAD_STEPS):
                    row = (hp * bsz + b) * SPREAD_STEPS + st
                    for s in range(vs):
                        q[hpar, row, st * LANES + s * group + b * RW_HEADS + 2 * hp + hpar] = 1.0
    return jnp.asarray(np.concatenate([q[0], q[1]] * 3, axis=0), BF16)


def _spread_k(xs):
    bsz, t, _ = xs[0].shape
    assert (RW_HEADS // 2) * bsz * SPREAD_STEPS == LANES, "the transposed tile must be 128 lanes wide"
    n = len(xs)
    ngrp = SPREAD_BLOCK // SPREAD_STEPS

    def body(*refs):
        qm = refs[n][...]
        for x_ref, o_ref in zip(refs[:n], refs[n + 1:]):
            cols = [[] for _ in range(6)]
            for m in range(ngrp):
                at = slice(SPREAD_STEPS * m, SPREAD_STEPS * (m + 1))
                x8 = jnp.concatenate([x_ref[b, at, LANES * hp:LANES * (hp + 1)]
                                      for hp in range(RW_HEADS // 2) for b in range(bsz)], axis=0)
                for pi, piece in enumerate(_split3(x8.T)):
                    cols[2 * pi].append(piece[0:64])
                    cols[2 * pi + 1].append(piece[64:128])
            lhs = jnp.concatenate([jnp.concatenate(c, axis=0) for c in cols], axis=1)
            y = _dot(lhs, qm)
            for m in range(ngrp):
                for st in range(SPREAD_STEPS):
                    o_ref[SPREAD_STEPS * m + st] = y[64 * m:64 * (m + 1), LANES * st:LANES * (st + 1)]

    qm = _spread_matrix(bsz)
    return pl.pallas_call(
        body, name="wkv_spread", grid=(t // SPREAD_BLOCK,),
        out_shape=[jax.ShapeDtypeStruct((t, 64, LANES), F32)] * n,
        in_specs=[pl.BlockSpec((bsz, SPREAD_BLOCK, RW), lambda i: (0, i, 0))] * n + [_full(qm.shape)],
        out_specs=[pl.BlockSpec((SPREAD_BLOCK, 64, LANES), lambda i: (i, 0, 0))] * n,
        compiler_params=_cparams(("parallel",)),
    )(*xs, qm)


def _wkv_fwd(r, w, kp, al, be, v):
    t, vq = v.shape[0], v.shape[1]

    def body(r_ref, w_ref, kp_ref, al_ref, be_ref, v_ref, y_ref, a_ref, u_ref, st_ref):
        @pl.when(pl.program_id(0) == 0)
        def _():
            st_ref[...] = jnp.zeros(st_ref.shape, F32)

        def step(tl, _):
            rv, wv, kv, av, bv = r_ref[tl], w_ref[tl], kp_ref[tl], al_ref[tl], be_ref[tl]
            vals = v_ref[tl]
            yrows, urows = [], []
            for q in range(vq):
                s = st_ref[q]
                u = _ksum(s * av)
                s = s * wv + bv * u + kv * vals[q:q + 1]
                st_ref[q] = s
                a_ref[tl, q] = s
                urows.append(u)
                yrows.append(_ksum(s * rv))
            y_ref[tl] = jnp.concatenate(yrows, axis=0)
            u_ref[tl] = jnp.concatenate(urows, axis=0)
            return 0

        lax.fori_loop(0, CH, step, 0)

    kspec = pl.BlockSpec((CH, 64, LANES), lambda i: (i, 0, 0))
    vspec = pl.BlockSpec((CH, vq, LANES), lambda i: (i, 0, 0))
    vsd = jax.ShapeDtypeStruct((t, vq, LANES), F32)
    return pl.pallas_call(
        body, name="wkv_fwd", grid=(t // CH,),
        out_shape=[vsd, jax.ShapeDtypeStruct((t, vq, 64, LANES), F32), vsd],
        in_specs=[kspec] * 5 + [vspec],
        out_specs=[vspec, pl.BlockSpec((CH, vq, 64, LANES), lambda i: (i, 0, 0, 0)), vspec],
        scratch_shapes=[pltpu.VMEM((vq, 64, LANES), F32)],
        compiler_params=_cparams(("arbitrary",)),
    )(r, w, kp, al, be, v)


def _wkv_bwd(r, w, kp, al, be, v, dy, states, u):
    t, vq = v.shape[0], v.shape[1]
    vs = 64 // vq
    group = LANES // vs
    n = t // CH
    ng = CH // vs

    def body(r_ref, w_ref, kp_ref, al_ref, be_ref, v_ref, dy_ref, u_ref, a_ref, ap_ref,
             dr_ref, dw_ref, dkp_ref, dal_ref, dbe_ref, dv_ref, ds_ref):
        @pl.when(pl.program_id(0) == 0)
        def _():
            ds_ref[...] = jnp.zeros(ds_ref.shape, F32)

        earliest = pl.program_id(0) == n - 1

        def reverse(i, _):
            g = ng - 1 - i
            grp = _lane_group((64, LANES), group)
            outs = None
            for j in reversed(range(vs)):
                tl = g * vs + j
                rv, wv, kv, av, bv = r_ref[tl], w_ref[tl], kp_ref[tl], al_ref[tl], be_ref[tl]
                vals, dys, us = v_ref[tl], dy_ref[tl], u_ref[tl]
                acc = None
                dvrows = []
                for q in range(vq):
                    if j > 0:
                        s_prev = a_ref[tl - 1, q]
                    else:
                        before = jnp.where(earliest, 0.0, ap_ref[0, q])
                        s_prev = jnp.where(g == 0, before, a_ref[jnp.maximum(tl - 1, 0), q])
                    dyq = dys[q:q + 1]
                    ds = ds_ref[q] + rv * dyq
                    c = _ksum(ds * bv)
                    dvrows.append(_ksum(ds * kv))
                    terms = (a_ref[tl, q] * dyq, ds * s_prev, ds * vals[q:q + 1], s_prev * c, ds * us[q:q + 1])
                    acc = terms if acc is None else tuple(a + b for a, b in zip(acc, terms))
                    ds_ref[q] = ds * wv + av * c
                dv_ref[tl] = jnp.concatenate(dvrows, axis=0)
                summed = [_fold(a, group) for a in acc]
                outs = summed if outs is None else [jnp.where(grp == j, f, o) for f, o in zip(summed, outs)]
            for ref, o in zip((dr_ref, dw_ref, dkp_ref, dal_ref, dbe_ref), outs):
                ref[g] = o
            return 0

        lax.fori_loop(0, ng, reverse, 0)

    kspec = pl.BlockSpec((CH, 64, LANES), lambda i: (n - 1 - i, 0, 0))
    gspec = pl.BlockSpec((ng, 64, LANES), lambda i: (n - 1 - i, 0, 0))
    vspec = pl.BlockSpec((CH, vq, LANES), lambda i: (n - 1 - i, 0, 0))
    ksd = jax.ShapeDtypeStruct((t // vs, 64, LANES), F32)
    return pl.pallas_call(
        body, name="wkv_bwd", grid=(n,),
        out_shape=[ksd] * 5 + [jax.ShapeDtypeStruct((t, vq, LANES), F32)],
        in_specs=[kspec] * 5 + [vspec, vspec, vspec,
                                pl.BlockSpec((CH, vq, 64, LANES), lambda i: (n - 1 - i, 0, 0, 0)),
                                pl.BlockSpec((1, vq, 64, LANES), lambda i: (jnp.maximum((n - 1 - i) * CH - 1, 0), 0, 0, 0))],
        out_specs=[gspec] * 5 + [vspec],
        scratch_shapes=[pltpu.VMEM((vq, 64, LANES), F32)],
        compiler_params=_cparams(("arbitrary",)),
    )(r, w, kp, al, be, v, dy, u, states, states)


def _post(x, tgt, pp, o, yw, r, kp, v, ln_g, ln_b, r_k, wo, wot, gpost, bo):
    bsz, t, _ = x.shape
    tt = TT_VPU
    nt = t // tt

    def body(x_ref, tgt_ref, z_ref, o_ref, yw_ref, r_ref, kp_ref, v_ref, lng_ref, lnb_ref, rk_ref, wo_ref, wot_ref,
             gpost_ref, bo_ref,
             dh_ref, dz_ref, dym_ref, dyw_ref, dbon_ref, loss_ref, dwo_ref, dgpost_ref, dlng_ref, dlnb_ref, drk_ref):
        first = (pl.program_id(0) == 0) & (pl.program_id(1) == 0)

        @pl.when(first)
        def _():
            for ref in (loss_ref, dwo_ref, dgpost_ref, dlng_ref, dlnb_ref, drk_ref):
                ref[...] = jnp.zeros(ref.shape, F32)

        bo_m = bo_ref[...]
        seg = lambda a: _seg(a, bo_m)
        rowsum = lambda a: jnp.sum(a, axis=0, keepdims=True)
        ywv, rv, kpv, vv = yw_ref[0], r_ref[0], kp_ref[0], v_ref[0]
        ln_g, r_k = lng_ref[...], rk_ref[...]
        mean = seg(ywv) * (1.0 / 64)
        yc = ywv - mean
        rstd = lax.rsqrt(seg(yc * yc) * (1.0 / 64) + GN_EPS)
        yhat = yc * rstd
        sb = seg(rv * kpv * r_k)
        y_rw = yhat * ln_g + lnb_ref[...] + sb * vv
        z = z_ref[0]
        sig = _sigmoid(z)
        sz = z * sig
        ycat = jnp.concatenate([o_ref[0], y_rw], axis=1)
        ycg = (ycat * sz).astype(BF16)
        out = _dot(ycg, wo_ref[...])
        hn, nx, rstd_o = _rms(out, gpost_ref[...], D)
        err = x_ref[0] + hn - tgt_ref[0]
        loss_ref[...] += jnp.sum(err * err) * (0.5 / D)
        dh = err * (1.0 / D)
        dh_ref[0] = dh
        dout, dgp = _rms_bwd(dh, nx, rstd_o, gpost_ref[...], D)
        dgpost_ref[...] += dgp
        doutb = dout.astype(BF16)
        dwo_ref[...] += _dot_tn(ycg, doutb)
        dycg = _dot(doutb, wot_ref[...])
        dz_ref[0] = dycg * ycat * (sig * (1.0 + z * (1.0 - sig)))
        dycat = dycg * sz
        dym_ref[0] = dycat[:, 0:512]
        dy_rw = dycat[:, 512:1024]
        dlnb_ref[...] += rowsum(dy_rw)
        dlng_ref[...] += rowsum(dy_rw * yhat)
        dyhat = dy_rw * ln_g
        dyw_ref[0] = rstd * (dyhat - seg(dyhat) * (1.0 / 64) - yhat * (seg(dyhat * yhat) * (1.0 / 64)))
        dsb = seg(dy_rw * vv)
        drk_ref[...] += rowsum(dsb * rv * kpv)
        dbon_ref[0, :, 0:512] = dsb * kpv * r_k
        dbon_ref[0, :, 512:1024] = dsb * rv * r_k
        dbon_ref[0, :, 1024:1536] = dy_rw * sb

    tok = lambda c: pl.BlockSpec((1, tt, c), lambda b, i: (b, i, 0))
    full = lambda a: _full(a.shape)
    ins = (x, tgt, pp, o, yw, r, kp, v, ln_g, ln_b, r_k, wo, wot, gpost, bo)
    in_specs = [tok(D), tok(D), tok(1024)] + [tok(512)] * 5 + [full(a) for a in ins[8:]]
    sd = lambda c: jax.ShapeDtypeStruct((bsz, t, c), F32)
    vec = lambda c: jax.ShapeDtypeStruct((1, c), F32)
    out_shape = [sd(D), sd(1024), sd(512), sd(512), sd(1536), jax.ShapeDtypeStruct((8, LANES), F32),
                 jax.ShapeDtypeStruct((1024, 1024), F32), vec(D), vec(512), vec(512), vec(512)]
    out_specs = [tok(D), tok(1024), tok(512), tok(512), tok(1536), _resident((8, LANES)), _resident((1024, 1024)),
                 _resident((1, D)), _resident((1, 512)), _resident((1, 512)), _resident((1, 512))]
    return pl.pallas_call(
        body, name="post", grid=(bsz, nt), out_shape=out_shape, in_specs=in_specs, out_specs=out_specs,
        compiler_params=_cparams(("arbitrary", "arbitrary")),
    )(*ins)


def _pre_bwd_a(pp, pos, invf, cqkv_w, mu, w0, w2p, w2pt, a0, a2p, a2pt, k_k, k_a, bo,
               dq, dk, dva, dwkv, dbon):
    gq, wuqt, gkv, wukvt = cqkv_w
    bsz, t, _ = pp.shape
    tt = TT_VPU
    nt = t // tt
    dr_w, dw_w, dkp_w, dv_w, dal_w, dbe_w = dwkv

    def body(pp_ref, pos_ref, invf_ref, gq_ref, wuqt_ref, gkv_ref, wukvt_ref, mu_ref, w0_ref, w2p_ref, w2pt_ref,
             a0_ref, a2p_ref, a2pt_ref, kk_ref, ka_ref, bo_ref, dq_ref, dk_ref, dva_ref,
             dr_ref, dw_ref, dkp_ref, dv_ref, dal_ref, dbe_ref, dbon_ref,
             da_ref, dwuq_ref, dwukv_ref, dw2p_ref, da2p_ref, dgq_ref, dgkv_ref, dmu_ref, dw0_ref, da0_ref,
             dkk_ref, dka_ref, carry):
        i = pl.program_id(1)
        first = (pl.program_id(0) == 0) & (i == 0)

        @pl.when(first)
        def _():
            for ref in (dwuq_ref, dwukv_ref, dw2p_ref, da2p_ref, dgq_ref, dgkv_ref, dmu_ref, dw0_ref, da0_ref,
                        dkk_ref, dka_ref):
                ref[...] = jnp.zeros(ref.shape, F32)

        bo_m = bo_ref[...]
        rowsum = lambda a: jnp.sum(a, axis=0, keepdims=True)
        prw = pp_ref[0, :, RW0:DP]

        @pl.when(i == 0)
        def _():
            carry[...] = jnp.zeros(carry.shape, F32)

        ps, sh = _shift_mix(prw, carry[7:8, :], mu_ref[...])
        carry[...] = prw[tt - 8:tt, :]
        k_k, k_a = kk_ref[...], ka_ref[...]
        g = _rw_gates(ps, w0_ref[...], w2p_ref[...], a0_ref[...], a2p_ref[...], k_k, k_a, bo_m)
        a, kk, k = g["a"], g["kk"], g["k"]
        dr = dr_ref[0] + dbon_ref[0, :, 0:512]
        dkp = dkp_ref[0] + dbon_ref[0, :, 512:1024]
        dv = dv_ref[0] + dbon_ref[0, :, 1024:1536]
        dbe = dbe_ref[0]
        dkk = dbe * a - dal_ref[0]
        da = dbe * kk + dkp * k * k_a
        dka_ref[...] += rowsum(dkp * k * (a - 1.0))
        dm = (dkk - kk * _seg(dkk * kk, bo_m)) / g["nrm"]
        dkk_ref[...] += rowsum(dm * k)
        dk_tot = dkp * (1.0 + (a - 1.0) * k_a) + dm * k_k
        dapre = da * a * (1.0 - a)
        da0_ref[...] += rowsum(dapre)
        dapb = dapre.astype(BF16)
        da2p_ref[...] += _dot_tn(g["misc"].astype(BF16), dapb)
        dwpre = dw_ref[0] * g["w"] * (-g["e"]) * _sigmoid(-g["wpre"])
        dw0_ref[...] += rowsum(dwpre)
        dwpb = dwpre.astype(BF16)
        th = g["th"]
        dw2p_ref[...] += _dot_tn(th.astype(BF16), dwpb)
        dmisc = _dot(dapb, a2pt_ref[...]) + _dot(dwpb, w2pt_ref[...]) * (1.0 - th * th)
        ang = pos_ref[0] * invf_ref[...]
        cs, sn = jnp.cos(ang), jnp.sin(ang)
        unrope = lambda gr: gr * cs - _rot(gr * sn)
        lane = lax.broadcasted_iota(jnp.int32, cs.shape, 1)
        dkr = dk_ref[0, :, 128:256]
        for h in range(1, HEADS):
            dkr = dkr + dk_ref[0, :, 256 * h + 128:256 * h + 256]
        dkr = jnp.where(lane < 64, unrope(dkr), 0.0)
        dmisc = dmisc + jnp.concatenate([dkr, jnp.zeros_like(dkr)], axis=1)
        dqp = jnp.concatenate(
            [blk for h in range(HEADS)
             for blk in (dq_ref[0, :, 256 * h:256 * h + 128], unrope(dq_ref[0, :, 256 * h + 128:256 * h + 256]))],
            axis=1).astype(BF16)
        dkvp = jnp.concatenate([dk_ref[0, :, 256 * h:256 * h + 128] for h in range(HEADS)] + [dva_ref[0]],
                               axis=1).astype(BF16)
        cqn, cq_nx, cq_rstd = _rms(pp_ref[0, :, CQ0:CQ0 + 256], gq_ref[...], 256)
        ckvn, ckv_nx, ckv_rstd = _rms(pp_ref[0, :, CKV0:CKV0 + 128], gkv_ref[...], 128)
        dwuq_ref[...] += _dot_tn(cqn.astype(BF16), dqp)
        dwukv_ref[...] += _dot_tn(ckvn.astype(BF16), dkvp)
        dcq, dgq = _rms_bwd(_dot(dqp, wuqt_ref[...]), cq_nx, cq_rstd, gq_ref[...], 256)
        dckv, dgkv = _rms_bwd(_dot(dkvp, wukvt_ref[...]), ckv_nx, ckv_rstd, gkv_ref[...], 128)
        dgq_ref[...] += dgq
        dgkv_ref[...] += dgkv
        dps = jnp.concatenate([dr, dk_tot, dv, dmisc], axis=1)
        dmu_ref[...] += rowsum(dps * (sh - prw))
        da_ref[0, :, 0:256] = dcq
        da_ref[0, :, 256:384] = dckv
        da_ref[0, :, 384:384 + NRW] = dps

    tok = lambda c: pl.BlockSpec((1, tt, c), lambda b, i: (b, i, 0))
    full = lambda a: _full(a.shape)
    ins = (pp, pos, invf, gq, wuqt, gkv, wukvt, mu, w0, w2p, w2pt, a0, a2p, a2pt, k_k, k_a, bo,
           dq, dk, dva, dr_w, dw_w, dkp_w, dv_w, dal_w, dbe_w, dbon)
    in_specs = ([tok(DP), tok(1)] + [full(a) for a in ins[2:17]] + [tok(1024), tok(1024), tok(512)]
                + [tok(512)] * 6 + [tok(1536)])
    shp = lambda *s: jax.ShapeDtypeStruct(s, F32)
    out_shape = [shp(bsz, t, 384 + NRW), shp(256, 1024), shp(128, 1024), shp(256, 512), shp(256, 512),
                 shp(1, 256), shp(1, 128), shp(1, NRW), shp(1, 512), shp(1, 512), shp(1, 512), shp(1, 512)]
    out_specs = [tok(384 + NRW)] + [_resident(s.shape) for s in out_shape[1:]]
    return pl.pallas_call(
        body, name="pre_bwd_a", grid=(bsz, nt), out_shape=out_shape, in_specs=in_specs, out_specs=out_specs,
        scratch_shapes=[pltpu.VMEM((8, NRW), F32)],
        compiler_params=_cparams(("arbitrary", "arbitrary")),
    )(*ins)


def _pre_bwd_b(x, dh, dz, da, mu, wpt, gpre):
    bsz, t, _ = x.shape
    nt = t // TT
    nblk = t // 8

    def body(x_ref, dh_ref, dz_ref, da_ref, nxt_ref, mu_ref, wpt_ref, gpre_ref, gx_ref, dp_ref, dgpre_ref):
        i = pl.program_id(1)
        first = (pl.program_id(0) == 0) & (i == 0)

        @pl.when(first)
        def _():
            dgpre_ref[...] = jnp.zeros(dgpre_ref.shape, F32)

        mu_v = mu_ref[...]
        dps = da_ref[0, :, 384:384 + NRW]
        nxt = jnp.where(i < nt - 1, nxt_ref[0, 0:1, 384:384 + NRW], 0.0)
        row = lax.broadcasted_iota(jnp.int32, dps.shape, 0)
        up = jnp.where(row == TT - 1, nxt, pltpu.roll(dps, TT - 1, 0))
        dprw = dps * (1.0 - mu_v) + up * mu_v
        dp = jnp.concatenate([dz_ref[0], da_ref[0, :, 0:384], dprw], axis=1).astype(BF16)
        dp_ref[0] = dp
        du = _dot(dp, wpt_ref[...])
        _, nx, rstd = _rms(x_ref[0], gpre_ref[...], D)
        dx, dg = _rms_bwd(du, nx, rstd, gpre_ref[...], D)
        dgpre_ref[...] += dg
        gx_ref[0] = dh_ref[0] + dx

    tok = lambda c: pl.BlockSpec((1, TT, c), lambda b, i: (b, i, 0))
    nxt_spec = pl.BlockSpec((1, 8, 384 + NRW), lambda b, i: (b, jnp.minimum((i + 1) * (TT // 8), nblk - 1), 0))
    ins = (x, dh, dz, da, da, mu, wpt, gpre)
    return pl.pallas_call(
        body, name="pre_bwd_b", grid=(bsz, nt),
        out_shape=[jax.ShapeDtypeStruct((bsz, t, D), F32), jax.ShapeDtypeStruct((bsz, t, DP), BF16),
                   jax.ShapeDtypeStruct((1, D), F32)],
        in_specs=[tok(D), tok(D), tok(1024), tok(384 + NRW), nxt_spec, _full(mu.shape), _full(wpt.shape),
                  _full(gpre.shape)],
        out_specs=[tok(D), tok(DP), _resident((1, D))],
        compiler_params=_cparams(("arbitrary", "arbitrary")),
    )(*ins)


def _tn_matmul(a, b, bn, name, bk=512):
    kdim, m = a.shape
    _, n = b.shape
    nk = kdim // bk

    def body(a_ref, b_ref, o_ref):
        @pl.when(pl.program_id(1) == 0)
        def _():
            o_ref[...] = jnp.zeros(o_ref.shape, F32)

        o_ref[...] += _dot_tn(a_ref[...], b_ref[...])

    return pl.pallas_call(
        body, name=name, grid=(n // bn, nk),
        out_shape=jax.ShapeDtypeStruct((m, n), F32),
        in_specs=[pl.BlockSpec((bk, m), lambda j, kk: (kk, 0)), pl.BlockSpec((bk, bn), lambda j, kk: (kk, j))],
        out_specs=pl.BlockSpec((m, bn), lambda j, kk: (0, j)),
        compiler_params=_cparams(("parallel", "arbitrary")),
    )(a, b)


SHARDED = ("w_in", "mla_w_uq", "mla_w_ukv", "rw_w2", "rw_a2", "w_out")
SMALL = ("norm_pre_g", "mla_q_norm_g", "mla_kv_norm_g", "rw_mu", "rw_w0", "rw_a0", "rw_k_k", "rw_k_a", "rw_r_k",
         "rw_ln_g", "rw_ln_b", "norm_post_g")
WEIGHTS = ("norm_pre_g", "w_in", "mla_q_norm_g", "mla_w_uq", "mla_kv_norm_g", "mla_w_ukv", "rw_mu", "rw_w0", "rw_w2",
           "rw_a0", "rw_a2", "rw_k_k", "rw_k_a", "rw_r_k", "rw_ln_g", "rw_ln_b", "w_out", "norm_post_g")


def _pack_small(d):
    flat = jnp.concatenate([d[n].reshape(1, -1) for n in SMALL], axis=1)
    return jnp.pad(flat, ((0, 0), (0, SMALL_ROWS * LANES - flat.shape[1]))).reshape(SMALL_ROWS, LANES)


def _unpack_small(packed, like):
    flat = packed.reshape(1, -1)
    out, at = {}, 0
    for n in SMALL:
        size = int(np.prod(like[n].shape))
        out[n] = flat[:, at:at + size].reshape(like[n].shape)
        at += size
    return out


def _unpack_shard(packed, like):
    out, at = {}, 0
    for n, rows in zip(SHARDED, PACK_ROWS):
        out[n] = packed[at:at + rows].reshape(like[n].shape)
        at += rows
    return out


def _constants():
    bo = np.kron(np.eye(2, dtype=np.float32), np.ones((64, 64), np.float32))
    inv = ROPE_THETA ** (-np.arange(0, 64, 2, dtype=np.float32) / 64)
    invf = np.concatenate([inv, inv, np.zeros(64, np.float32)]).astype(np.float32)[None, :]
    return jnp.asarray(bo, BF16), jnp.asarray(invf)


def kernel(x, positions, norm_pre_g, w_in, mla_q_norm_g, mla_w_uq, mla_kv_norm_g, mla_w_ukv, rw_mu, rw_w0, rw_w2, rw_a0, rw_a2, rw_k_k, rw_k_a, rw_r_k, rw_ln_g, rw_ln_b, w_out, norm_post_g, loss_target, m_norm_pre_g, m_w_in, m_mla_q_norm_g, m_mla_w_uq, m_mla_kv_norm_g, m_mla_w_ukv, m_rw_mu, m_rw_w0, m_rw_w2, m_rw_a0, m_rw_a2, m_rw_k_k, m_rw_k_a, m_rw_r_k, m_rw_ln_g, m_rw_ln_b, m_w_out, m_norm_post_g, v_norm_pre_g, v_w_in, v_mla_q_norm_g, v_mla_w_uq, v_mla_kv_norm_g, v_mla_w_ukv, v_rw_mu, v_rw_w0, v_rw_w2, v_rw_a0, v_rw_a2, v_rw_k_k, v_rw_k_a, v_rw_r_k, v_rw_ln_g, v_rw_ln_b, v_w_out, v_norm_post_g):
    wts = dict(norm_pre_g=norm_pre_g, w_in=w_in, mla_q_norm_g=mla_q_norm_g, mla_w_uq=mla_w_uq,
               mla_kv_norm_g=mla_kv_norm_g, mla_w_ukv=mla_w_ukv, rw_mu=rw_mu, rw_w0=rw_w0, rw_w2=rw_w2, rw_a0=rw_a0,
               rw_a2=rw_a2, rw_k_k=rw_k_k, rw_k_a=rw_k_a, rw_r_k=rw_r_k, rw_ln_g=rw_ln_g, rw_ln_b=rw_ln_b, w_out=w_out,
               norm_post_g=norm_post_g)
    mom_m = dict(norm_pre_g=m_norm_pre_g, w_in=m_w_in, mla_q_norm_g=m_mla_q_norm_g, mla_w_uq=m_mla_w_uq,
                 mla_kv_norm_g=m_mla_kv_norm_g, mla_w_ukv=m_mla_w_ukv, rw_mu=m_rw_mu, rw_w0=m_rw_w0, rw_w2=m_rw_w2,
                 rw_a0=m_rw_a0, rw_a2=m_rw_a2, rw_k_k=m_rw_k_k, rw_k_a=m_rw_k_a, rw_r_k=m_rw_r_k, rw_ln_g=m_rw_ln_g,
                 rw_ln_b=m_rw_ln_b, w_out=m_w_out, norm_post_g=m_norm_post_g)
    mom_v = dict(norm_pre_g=v_norm_pre_g, w_in=v_w_in, mla_q_norm_g=v_mla_q_norm_g, mla_w_uq=v_mla_w_uq,
                 mla_kv_norm_g=v_mla_kv_norm_g, mla_w_ukv=v_mla_w_ukv, rw_mu=v_rw_mu, rw_w0=v_rw_w0, rw_w2=v_rw_w2,
                 rw_a0=v_rw_a0, rw_a2=v_rw_a2, rw_k_k=v_rw_k_k, rw_k_a=v_rw_k_a, rw_r_k=v_rw_r_k, rw_ln_g=v_rw_ln_g,
                 rw_ln_b=v_rw_ln_b, w_out=v_w_out, norm_post_g=v_norm_post_g)
    bsz, t, _ = x.shape
    bo, invf = _constants()
    c_idx = lax.axis_index("c")

    g_in, g_uq, g_ukv, g_w2, g_a2, g_out = _ag_weights([wts[n][0] for n in SHARDED])
    w_in_f = jnp.transpose(g_in, (1, 0, 2)).reshape(D, D_IN)
    wp = jnp.concatenate([w_in_f[:, 2112:3136], w_in_f[:, 0:384], w_in_f[:, 448:1984], w_in_f[:, 384:448],
                          w_in_f[:, 1984:2112], jnp.zeros((D, 64), BF16)], axis=1)
    wuq = jnp.pad(jnp.transpose(g_uq, (1, 0, 2)).reshape(256, HEADS, 192), ((0, 0), (0, 0), (0, 64))).reshape(256, 1024)
    wukv = jnp.transpose(jnp.transpose(g_ukv, (1, 0, 2)).reshape(128, HEADS, 2, 128), (0, 2, 1, 3)).reshape(128, 1024)
    w2 = jnp.transpose(g_w2, (1, 0, 2)).reshape(64, RW)
    a2 = jnp.transpose(g_a2, (1, 0, 2)).reshape(64, RW)
    w2p = jnp.pad(w2, ((64, 128), (0, 0)))
    a2p = jnp.pad(a2, ((128, 64), (0, 0)))
    wo = g_out.reshape(D, D)
    mu = jnp.concatenate([rw_mu[:, 0:1536], jnp.zeros((1, 64), F32), rw_mu[:, 1536:1664], jnp.zeros((1, 64), F32)],
                         axis=1)
    r_k = rw_r_k.reshape(1, RW)
    pos = positions.astype(F32)[:, :, None]

    (u, pp, q_att, k_att, v_att, r, w, kp, v, al, be) = _pre_fwd(
        x, pos, invf, norm_pre_g, wp, mla_q_norm_g, wuq, mla_kv_norm_g, wukv, mu, rw_w0, w2p, rw_a0, a2p, rw_k_k,
        rw_k_a, bo)
    o, lse = _attn_fwd(q_att, k_att, v_att)
    rw_k = _spread_k([r, w, kp, al, be])
    v_v = _to_v(v)
    yw_v, states, u_v = _wkv_fwd(*rw_k, v_v)
    yw = _from_v(yw_v, bsz)

    (dh, dz, dym, dyw, dbon, loss_acc, d_wo, d_gpost, d_lng, d_lnb, d_rk) = _post(
        x, loss_target, pp, o, yw, r, kp, v, rw_ln_g, rw_ln_b, r_k, wo, wo.T, norm_post_g, bo)

    d_k = _wkv_bwd(*rw_k, v_v, _to_v(dyw), states, u_v)
    dr_w, dw_w, dkp_w, dal_w, dbe_w = _gather_k(d_k[:5], bsz)
    dwkv = (dr_w, dw_w, dkp_w, _from_v(d_k[5], bsz), dal_w, dbe_w)
    dq, dk, dva = _attn_bwd(q_att, k_att, v_att, o, lse, dym)

    (da, d_wuq, d_wukv, d_w2p, d_a2p, d_gq, d_gkv, d_mu, d_w0, d_a0, d_kk, d_ka) = _pre_bwd_a(
        pp, pos, invf, (mla_q_norm_g, wuq.T, mla_kv_norm_g, wukv.T), mu, rw_w0, w2p, w2p.T, rw_a0, a2p, a2p.T,
        rw_k_k, rw_k_a, bo, dq, dk, dva, dwkv, dbon)
    grad_x, dpb, d_gpre = _pre_bwd_b(x, dh, dz, da, mu, wp.T, norm_pre_g)
    d_wp = _tn_matmul(u.reshape(bsz * t, D), dpb.reshape(bsz * t, DP), DP, "dw_in")

    full_g = {
        "w_in": jnp.concatenate([d_wp[:, 1024:1408], d_wp[:, 2944:3008], d_wp[:, 1408:2944], d_wp[:, 3008:3136],
                                 d_wp[:, 0:1024]], axis=1),
        "mla_w_uq": d_wuq.reshape(256, HEADS, 256)[:, :, :192].reshape(256, 768),
        "mla_w_ukv": jnp.transpose(d_wukv.reshape(128, 2, HEADS, 128), (0, 2, 1, 3)).reshape(128, 1024),
        "rw_w2": d_w2p[64:128],
        "rw_a2": d_a2p[128:192],
        "w_out": d_wo,
    }
    small_g = {
        "norm_pre_g": d_gpre, "mla_q_norm_g": d_gq, "mla_kv_norm_g": d_gkv,
        "rw_mu": jnp.concatenate([d_mu[:, 0:1536], d_mu[:, 1600:1728]], axis=1),
        "rw_w0": d_w0, "rw_a0": d_a0, "rw_k_k": d_kk, "rw_k_a": d_ka, "rw_r_k": d_rk, "rw_ln_g": d_lng,
        "rw_ln_b": d_lnb, "norm_post_g": d_gpost,
    }

    def by_shard(name, g):
        if name == "w_out":
            return g.reshape(N_SHARD, -1, LANES)
        rows, cols = g.shape
        return jnp.transpose(g.reshape(rows, N_SHARD, cols // N_SHARD), (1, 0, 2)).reshape(N_SHARD, -1, LANES)

    packed = jnp.concatenate([by_shard(n, full_g[n]) for n in SHARDED], axis=1)
    halves = packed.reshape(N_SHARD, 2, HALF, LANES)
    keep = lax.dynamic_index_in_dim(halves, c_idx, 1, keepdims=False)
    give = lax.dynamic_index_in_dim(halves, 1 - c_idx, 1, keepdims=False)
    got = _rs_pair_exchange(give)
    pair_sum, pair_sum_b = _add_n([keep.reshape(-1, LANES), got.reshape(-1, LANES)], "rs_pair_sum", SUM_ROWS,
                                  also_bf16=True)
    g_shard = _rs_chips(pair_sum.reshape(N_SHARD, HALF, LANES),
                        pair_sum_b.reshape(N_SHARD, HALF, LANES)).reshape(PACK_TOTAL, LANES)

    g_small = _small_allreduce(jnp.concatenate([_pack_small(small_g)[:SMALL_USED], loss_acc[0:SMALL_ROWS - SMALL_USED]]))
    loss = g_small[SMALL_USED, 0]

    g_sharded = _unpack_shard(g_shard, {n: wts[n][0] for n in SHARDED})
    sh = _adamw([wts[n][0] for n in SHARDED], [g_sharded[n] for n in SHARDED], [mom_m[n][0] for n in SHARDED],
                [mom_v[n][0] for n in SHARDED], "adamw_sharded")
    sm = _adamw([_pack_small(wts)], [g_small], [_pack_small(mom_m)], [_pack_small(mom_v)], "adamw_small")

    def outputs(sharded, small):
        out = {n: a[None] for n, a in zip(SHARDED, sharded)}
        out.update(_unpack_small(small, wts))
        return out

    grads = outputs([g_sharded[n] for n in SHARDED], g_small)
    deltas, new_m, new_v = (outputs(sh[k], sm[k][0]) for k in range(3))
    return (loss, grad_x, *[grads[n] for n in WEIGHTS], *[deltas[n] for n in WEIGHTS],
            *[new_m[n] for n in WEIGHTS], *[new_v[n] for n in WEIGHTS])
```

```python
import functools

import numpy as np
import jax
import jax.numpy as jnp
from jax import lax
from jax.experimental import pallas as pl
from jax.experimental.pallas import tpu as pltpu

F32, BF16 = jnp.float32, jnp.bfloat16
MESH = pl.DeviceIdType.MESH

D = 1024
HEADS = 4
RW = 512
NORM_EPS = 1e-6
GN_EPS = 64e-5
ROPE_THETA = 10000.0
SCALE = (128 + 64) ** -0.5
D_IN = 3136
LR, B1, B2, ADAM_EPS, WD, STEP = 0.001, 0.9, 0.999, 1e-08, 0.01, 10

Z0, CQ0, CKV0, RW0, DP = 0, 1024, 1280, 1408, 3200
NRW = DP - RW0

LANES = 128
SUBLANES = 8
VMEM_LIMIT = 56 * 1024 * 1024

TT = 512
TT_VPU = 256
TQ = 512

N_SHARD = 4
PACK_ROWS = (1024 * 784 // 128, 256 * 192 // 128, 128 * 256 // 128, 64, 64, 256 * 1024 // 128)
PACK_TOTAL = sum(PACK_ROWS)
HALF = PACK_TOTAL // 2
SMALL_ROWS = 64
SMALL_USED = 60


def _cparams(sem=None):
    return pltpu.CompilerParams(dimension_semantics=sem, vmem_limit_bytes=VMEM_LIMIT)


def _full(shape):
    n = len(shape)
    return pl.BlockSpec(shape, lambda *_: (0,) * n, pipeline_mode=pl.Buffered(1))


def _resident(shape):
    n = len(shape)
    return pl.BlockSpec(shape, lambda *_: (0,) * n)


def _dot(a, b):
    return jnp.dot(a, b, preferred_element_type=F32)


def _dot_nt(a, b):
    return lax.dot_general(a, b, (((1,), (1,)), ((), ())), preferred_element_type=F32)


def _dot_tn(a, b):
    return lax.dot_general(a, b, (((0,), (0,)), ((), ())), preferred_element_type=F32)


def _split3(x):
    hi = x.astype(BF16)
    r1 = x - hi.astype(F32)
    mid = r1.astype(BF16)
    lo = (r1 - mid.astype(F32)).astype(BF16)
    return hi, mid, lo


def _seg(x, bo):
    rows, nblk = x.shape[0], x.shape[1] // LANES
    pieces = [p for i in range(nblk) for p in _split3(x[:, LANES * i:LANES * (i + 1)])]
    res = _dot(jnp.concatenate(pieces, axis=0), bo)
    parts = [res[(3 * i) * rows:(3 * i + 1) * rows] + res[(3 * i + 1) * rows:(3 * i + 2) * rows]
             + res[(3 * i + 2) * rows:(3 * i + 3) * rows] for i in range(nblk)]
    return parts[0] if nblk == 1 else jnp.concatenate(parts, axis=1)


def _rms(x, g, n):
    rstd = lax.rsqrt(jnp.sum(x * x, axis=-1, keepdims=True) * (1.0 / n) + NORM_EPS)
    nx = x * rstd
    return nx * g, nx, rstd


def _rms_bwd(dy, nx, rstd, g, n):
    dn = dy * g
    dx = rstd * (dn - nx * (jnp.sum(dn * nx, axis=-1, keepdims=True) * (1.0 / n)))
    return dx, jnp.sum(dy * nx, axis=0, keepdims=True)


def _rot(x):
    lane = lax.broadcasted_iota(jnp.int32, x.shape, 1)
    return jnp.where((lane % 64) < 32, -pltpu.roll(x, x.shape[1] - 32, 1), pltpu.roll(x, 32, 1))


def _sigmoid(x):
    return 1.0 / (1.0 + jnp.exp(-x))


def _softplus(x):
    return jnp.maximum(x, 0.0) + jnp.log(1.0 + jnp.exp(-jnp.abs(x)))


def _rw_gates(ps, w0, w2p, a0, a2p, k_k, k_a, bo):
    r, k, v, misc = ps[:, 0:512], ps[:, 512:1024], ps[:, 1024:1536], ps[:, 1536:NRW]
    th = jnp.tanh(misc)
    wpre = w0 + _dot(th.astype(BF16), w2p)
    e = jnp.exp(-_softplus(-wpre) - 0.5)
    w = jnp.exp(-e)
    a = _sigmoid(a0 + _dot(misc.astype(BF16), a2p))
    m = k * k_k
    nrm = jnp.maximum(jnp.sqrt(_seg(m * m, bo)), 1e-12)
    kk = m / nrm
    kp = k * (1.0 + (a - 1.0) * k_a)
    return dict(r=r, k=k, v=v, misc=misc, th=th, wpre=wpre, e=e, w=w, a=a, nrm=nrm, kk=kk, kp=kp)


def _shift_mix(prw, prev_row, mu):
    row = lax.broadcasted_iota(jnp.int32, prw.shape, 0)
    sh = jnp.where(row == 0, prev_row, pltpu.roll(prw, 1, 0))
    return prw + (sh - prw) * mu, sh


def _ag_weights(shards):
    n = len(shards)

    def body(*refs):
        ins, outs = refs[:n], refs[n:2 * n]
        ici_send, ici_recv, d2d_send, d2d_recv = refs[2 * n:2 * n + 4]
        x, y, c = lax.axis_index("x"), lax.axis_index("y"), lax.axis_index("c")
        mine = 2 * x + y
        for w in range(n):
            outs[w][mine] = ins[w][...].astype(BF16)
        flips = ((1, 0), (0, 1), (1, 1))

        def half(w, shard, cc):
            rows = outs[w].shape[1] // 2
            return outs[w].at[shard, pl.ds(pl.multiple_of(cc * rows, 16), rows)]

        def ici(w, k, shard):
            fx, fy = flips[k]
            return pltpu.make_async_remote_copy(
                src_ref=half(w, shard, c), dst_ref=half(w, shard, c),
                send_sem=ici_send.at[w * 3 + k], recv_sem=ici_recv.at[w * 3 + k],
                device_id=(x ^ fx, y ^ fy, c), device_id_type=MESH)

        def d2d(w, k, cc):
            fx, fy = flips[k]
            theirs = 2 * (x ^ fx) + (y ^ fy)
            return pltpu.make_async_remote_copy(
                src_ref=half(w, theirs, cc), dst_ref=half(w, theirs, cc),
                send_sem=d2d_send.at[w * 3 + k], recv_sem=d2d_recv.at[w * 3 + k],
                device_id=(x, y, 1 - c), device_id_type=MESH)

        for w in range(n):
            for k in range(3):
                ici(w, k, mine).start()
        for w in range(n):
            for k in range(3):
                fx, fy = flips[k]
                ici(w, k, 2 * (x ^ fx) + (y ^ fy)).wait_recv()
                d2d(w, k, c).start()
        for w in range(n):
            for k in range(3):
                d2d(w, k, 1 - c).wait_recv()
        for w in range(n):
            for k in range(3):
                ici(w, k, mine).wait_send()
                d2d(w, k, c).wait_send()

    vm = pl.BlockSpec(memory_space=pltpu.VMEM)
    return pl.pallas_call(
        body, name="ag_weights",
        out_shape=[jax.ShapeDtypeStruct((N_SHARD,) + s.shape, BF16) for s in shards],
        in_specs=[vm] * n, out_specs=[vm] * n,
        scratch_shapes=[pltpu.SemaphoreType.DMA((3 * n,))] * 4,
        compiler_params=pltpu.CompilerParams(vmem_limit_bytes=VMEM_LIMIT),
    )(*shards)


def _rs_pairs(halves):
    def body(h_ref, sum_ref, sumb_ref, recv, send_sem, recv_sem):
        x, y, c = lax.axis_index("x"), lax.axis_index("y"), lax.axis_index("c")
        cps = [pltpu.make_async_remote_copy(src_ref=h_ref.at[s, 1 - c], dst_ref=recv.at[s], send_sem=send_sem.at[s],
                                            recv_sem=recv_sem.at[s], device_id=(x, y, 1 - c), device_id_type=MESH)
               for s in range(N_SHARD)]
        for cp in cps:
            cp.start()
        for s, cp in enumerate(cps):
            cp.wait_recv()
            acc = h_ref[s, c] + recv[s]
            sum_ref[s] = acc
            sumb_ref[s] = acc.astype(BF16)
        for cp in cps:
            cp.wait_send()

    vm = pl.BlockSpec(memory_space=pltpu.VMEM)
    shape = (N_SHARD,) + halves.shape[2:]
    return pl.pallas_call(
        body, name="rs_pairs",
        out_shape=[jax.ShapeDtypeStruct(shape, F32), jax.ShapeDtypeStruct(shape, BF16)],
        in_specs=[vm], out_specs=[vm, vm],
        scratch_shapes=[pltpu.VMEM(shape, F32), pltpu.SemaphoreType.DMA((N_SHARD,)),
                        pltpu.SemaphoreType.DMA((N_SHARD,))],
        compiler_params=pltpu.CompilerParams(vmem_limit_bytes=VMEM_LIMIT),
    )(halves)


def _rs_chips(part_f32, part_bf16):
    def body(own_ref, src_ref, out_ref, recv, ici_send, ici_recv, d2d_send, d2d_recv):
        x, y, c = lax.axis_index("x"), lax.axis_index("y"), lax.axis_index("c")
        mine = 2 * x + y
        flips = ((1, 0), (0, 1), (1, 1))
        cps = []
        for k, (fx, fy) in enumerate(flips):
            theirs = 2 * (x ^ fx) + (y ^ fy)
            cps.append(pltpu.make_async_remote_copy(
                src_ref=src_ref.at[theirs], dst_ref=recv.at[k],
                send_sem=ici_send.at[k], recv_sem=ici_recv.at[k],
                device_id=(x ^ fx, y ^ fy, c), device_id_type=MESH))
        for cp in cps:
            cp.start()
        acc = own_ref[mine]
        for k, cp in enumerate(cps):
            cp.wait_recv()
            acc = acc + recv[k].astype(F32)
        out_ref[c] = acc
        to_sibling = pltpu.make_async_remote_copy(
            src_ref=out_ref.at[c], dst_ref=out_ref.at[c], send_sem=d2d_send, recv_sem=d2d_recv,
            device_id=(x, y, 1 - c), device_id_type=MESH)
        to_sibling.start()
        pltpu.make_async_remote_copy(
            src_ref=out_ref.at[1 - c], dst_ref=out_ref.at[1 - c], send_sem=d2d_send, recv_sem=d2d_recv,
            device_id=(x, y, 1 - c), device_id_type=MESH).wait_recv()
        to_sibling.wait_send()
        for cp in cps:
            cp.wait_send()

    vm = pl.BlockSpec(memory_space=pltpu.VMEM)
    return pl.pallas_call(
        body, name="rs_chips",
        out_shape=jax.ShapeDtypeStruct((2,) + part_f32.shape[1:], F32),
        in_specs=[vm, vm], out_specs=vm,
        scratch_shapes=[pltpu.VMEM((3,) + part_bf16.shape[1:], BF16), pltpu.SemaphoreType.DMA((3,)),
                        pltpu.SemaphoreType.DMA((3,)), pltpu.SemaphoreType.DMA, pltpu.SemaphoreType.DMA],
        compiler_params=pltpu.CompilerParams(vmem_limit_bytes=VMEM_LIMIT),
    )(part_f32, part_bf16)


def _small_allreduce(vec):
    def body(in_ref, out_ref, recv, send_sems, recv_sems):
        x, y, c = lax.axis_index("x"), lax.axis_index("y"), lax.axis_index("c")
        me = 4 * x + 2 * y + c
        cps = []
        for k in range(1, 8):
            fx, fy, fc = (k >> 2) & 1, (k >> 1) & 1, k & 1
            cps.append(pltpu.make_async_remote_copy(
                src_ref=in_ref, dst_ref=recv.at[k - 1],
                send_sem=send_sems.at[k - 1], recv_sem=recv_sems.at[k - 1],
                device_id=(x ^ fx, y ^ fy, c ^ fc), device_id_type=MESH))
        for cp in cps:
            cp.start()
        for cp in cps:
            cp.wait()
        acc = jnp.zeros(in_ref.shape, F32)
        for j in range(8):
            slot = jnp.maximum((me ^ j) - 1, 0)
            acc = acc + jnp.where(me == j, in_ref[...], recv[slot])
        out_ref[...] = acc

    vm = pl.BlockSpec(memory_space=pltpu.VMEM)
    return pl.pallas_call(
        body, name="small_allreduce",
        out_shape=jax.ShapeDtypeStruct(vec.shape, F32),
        in_specs=[vm], out_specs=vm,
        scratch_shapes=[pltpu.VMEM((7,) + vec.shape, F32), pltpu.SemaphoreType.DMA((7,)),
                        pltpu.SemaphoreType.DMA((7,))],
    )(vec)


ADAM_ROWS = 64


def _adamw(ws, gs, ms, vs, name):
    n = len(ws)

    def body(*refs):
        for i in range(n):
            w_ref, g_ref, m_ref, v_ref = (refs[k * n + i] for k in range(4))
            d_ref, nm_ref, nv_ref = (refs[(4 + k) * n + i] for k in range(3))
            rows = min(ADAM_ROWS, w_ref.shape[0])

            def chunk(r, _):
                at = pl.ds(pl.multiple_of(r * rows, SUBLANES), rows)
                gg = g_ref[at, :]
                nm = B1 * m_ref[at, :] + (1.0 - B1) * gg
                nv = B2 * v_ref[at, :] + (1.0 - B2) * (gg * gg)
                m_hat = nm / (1.0 - B1 ** STEP)
                v_hat = nv / (1.0 - B2 ** STEP)
                d_ref[at, :] = -LR * (m_hat / (jnp.sqrt(v_hat) + ADAM_EPS) + WD * w_ref[at, :])
                nm_ref[at, :] = nm
                nv_ref[at, :] = nv
                return 0

            lax.fori_loop(0, w_ref.shape[0] // rows, chunk, 0)

    vm = pl.BlockSpec(memory_space=pltpu.VMEM)
    sds = [jax.ShapeDtypeStruct(w.shape, F32) for w in ws]
    outs = pl.pallas_call(
        body, name=name, out_shape=sds * 3, in_specs=[vm] * (4 * n), out_specs=[vm] * (3 * n),
        compiler_params=pltpu.CompilerParams(vmem_limit_bytes=VMEM_LIMIT),
    )(*ws, *gs, *ms, *vs)
    return outs[:n], outs[n:2 * n], outs[2 * n:]


def _pre_fwd(x, pos, invf, gpre, wp, gq, wuq, gkv, wukv, mu, w0, w2p, a0, a2p, k_k, k_a, bo):
    bsz, t, _ = x.shape
    nt = t // TT

    def body(x_ref, pos_ref, invf_ref, gpre_ref, wp_ref, gq_ref, wuq_ref, gkv_ref, wukv_ref, mu_ref, w0_ref,
             w2p_ref, a0_ref, a2p_ref, kk_ref, ka_ref, bo_ref,
             u_ref, pp_ref, q_ref, k_ref, v_ref, r_o, w_o, kp_o, vv_o, al_o, be_o, carry):
        i = pl.program_id(1)
        u, _, _ = _rms(x_ref[0], gpre_ref[...], D)
        ub = u.astype(BF16)
        u_ref[0] = ub
        p = _dot(ub, wp_ref[...])
        pp_ref[0] = p
        prw = p[:, RW0:DP]

        @pl.when(i == 0)
        def _():
            carry[...] = jnp.zeros(carry.shape, F32)

        ps, _ = _shift_mix(prw, carry[7:8, :], mu_ref[...])
        carry[...] = prw[TT - 8:TT, :]

        g = _rw_gates(ps, w0_ref[...], w2p_ref[...], a0_ref[...], a2p_ref[...], kk_ref[...], ka_ref[...],
                      bo_ref[...])
        r_o[0] = g["r"]
        w_o[0] = g["w"]
        kp_o[0] = g["kp"]
        vv_o[0] = g["v"]
        al_o[0] = -g["kk"]
        be_o[0] = g["kk"] * g["a"]

        cqn, _, _ = _rms(p[:, CQ0:CQ0 + 256], gq_ref[...], 256)
        q = _dot(cqn.astype(BF16), wuq_ref[...])
        ckvn, _, _ = _rms(p[:, CKV0:CKV0 + 128], gkv_ref[...], 128)
        kv = _dot(ckvn.astype(BF16), wukv_ref[...])
        ang = pos_ref[0] * invf_ref[...]
        cs, sn = jnp.cos(ang), jnp.sin(ang)
        lane = lax.broadcasted_iota(jnp.int32, cs.shape, 1)
        kr = ps[:, 1536:1536 + LANES]
        kr = jnp.where(lane < 64, kr * cs + _rot(kr) * sn, 0.0).astype(BF16)
        for h in range(HEADS):
            qr = q[:, 256 * h + 128:256 * h + 256]
            q_ref[0, :, 256 * h:256 * h + 128] = q[:, 256 * h:256 * h + 128].astype(BF16)
            q_ref[0, :, 256 * h + 128:256 * h + 256] = (qr * cs + _rot(qr) * sn).astype(BF16)
            k_ref[0, :, 256 * h:256 * h + 128] = kv[:, 128 * h:128 * h + 128].astype(BF16)
            k_ref[0, :, 256 * h + 128:256 * h + 256] = kr
        v_ref[0] = kv[:, 512:1024].astype(BF16)

    tok = lambda c: pl.BlockSpec((1, TT, c), lambda b, i: (b, i, 0))
    full = lambda a: _full(a.shape)
    ins = (x, pos, invf, gpre, wp, gq, wuq, gkv, wukv, mu, w0, w2p, a0, a2p, k_k, k_a, bo)
    in_specs = [tok(D), tok(1)] + [full(a) for a in ins[2:]]
    sd = lambda c, dt: jax.ShapeDtypeStruct((bsz, t, c), dt)
    out_shape = [sd(D, BF16), sd(DP, F32), sd(1024, BF16), sd(1024, BF16), sd(512, BF16)] + [sd(RW, F32)] * 6
    out_specs = [tok(D), tok(DP), tok(1024), tok(1024), tok(512)] + [tok(RW)] * 6
    return pl.pallas_call(
        body, name="pre_fwd", grid=(bsz, nt), out_shape=out_shape, in_specs=in_specs, out_specs=out_specs,
        scratch_shapes=[pltpu.VMEM((8, NRW), F32)],
        compiler_params=_cparams(("arbitrary", "arbitrary")),
    )(*ins)


def _attn_fwd(q, k, v):
    bsz, t, _ = q.shape
    nq = t // TQ

    def body(q_ref, k_ref, v_ref, o_ref, lse_ref):
        i = pl.program_id(2)

        def step(j, carry, diagonal):
            at = pl.ds(pl.multiple_of(j * TQ, TQ), TQ)
            out = []
            for hh in range(2):
                m, l, acc = carry[hh]
                s = _dot_nt(q_ref[0, :, 256 * hh:256 * (hh + 1)], k_ref[0, at, 256 * hh:256 * (hh + 1)]) * SCALE
                if diagonal:
                    s = jnp.where(lax.broadcasted_iota(jnp.int32, (TQ, TQ), 1)
                                  <= lax.broadcasted_iota(jnp.int32, (TQ, TQ), 0), s, -1e30)
                mn = jnp.maximum(m, jnp.max(s, axis=1, keepdims=True))
                p = jnp.exp(s - mn)
                al = jnp.exp(m - mn)
                l = al * l + jnp.sum(p, axis=1, keepdims=True)
                acc = al * acc + _dot(p.astype(BF16), v_ref[0, at, LANES * hh:LANES * (hh + 1)])
                out.append((mn, l, acc))
            return tuple(out)

        start = (jnp.full((TQ, 1), -1e30, F32), jnp.zeros((TQ, 1), F32), jnp.zeros((TQ, LANES), F32))
        before = lax.fori_loop(0, i, lambda j, carry: step(j, carry, False), (start, start))
        for hh, (m, l, acc) in enumerate(step(i, before, True)):
            o_ref[0, :, LANES * hh:LANES * (hh + 1)] = acc / l
            lse_ref[0, hh] = jnp.broadcast_to(m + jnp.log(l), (TQ, LANES))

    return pl.pallas_call(
        body, name="attn_fwd", grid=(bsz, HEADS // 2, nq),
        out_shape=[jax.ShapeDtypeStruct((bsz, t, 512), F32), jax.ShapeDtypeStruct((bsz, HEADS, t, LANES), F32)],
        in_specs=[pl.BlockSpec((1, TQ, 512), lambda b, h, i: (b, i, h)),
                  pl.BlockSpec((1, t, 512), lambda b, h, i: (b, 0, h)),
                  pl.BlockSpec((1, t, 256), lambda b, h, i: (b, 0, h))],
        out_specs=[pl.BlockSpec((1, TQ, 256), lambda b, h, i: (b, i, h)),
                   pl.BlockSpec((1, 2, TQ, LANES), lambda b, h, i: (b, h, i, 0))],
        compiler_params=_cparams(("parallel", "parallel", "arbitrary")),
    )(q, k, v)


def _attn_bwd(q, k, v, o, lse, do):
    bsz, t, _ = q.shape
    nq = t // TQ

    def body(q_ref, k_ref, v_ref, o_ref, lse_ref, do_ref, dq_ref, dk_ref, dv_ref, dl_ref):
        j = pl.program_id(2)

        @pl.when(j == 0)
        def _():
            def prep(i, _):
                at = pl.ds(pl.multiple_of(i * TQ, TQ), TQ)
                for hh in range(2):
                    lanes = slice(LANES * hh, LANES * (hh + 1))
                    dl_ref[hh, at, :] = jnp.broadcast_to(
                        jnp.sum(do_ref[0, at, lanes] * o_ref[0, at, lanes], axis=1, keepdims=True), (TQ, LANES))
                return 0

            lax.fori_loop(0, nq, prep, 0)
            dq_ref[0] = jnp.zeros((t, 512), F32)

        def q_tile(i, carry, diagonal):
            atq = pl.ds(pl.multiple_of(i * TQ, TQ), TQ)
            out = []
            for hh in range(2):
                dk, dv = carry[hh]
                wide, narrow = slice(256 * hh, 256 * (hh + 1)), slice(LANES * hh, LANES * (hh + 1))
                qt, kt, vt = q_ref[0, atq, wide], k_ref[0, :, wide], v_ref[0, :, narrow]
                dob = do_ref[0, atq, narrow].astype(BF16)
                s = _dot_nt(qt, kt) * SCALE
                if diagonal:
                    s = jnp.where(lax.broadcasted_iota(jnp.int32, (TQ, TQ), 1)
                                  <= lax.broadcasted_iota(jnp.int32, (TQ, TQ), 0), s, -1e30)
                p = jnp.exp(s - lse_ref[0, hh, atq, :][:, 0:1])
                dv = dv + _dot_tn(p.astype(BF16), dob)
                dp = _dot_nt(dob, vt)
                ds = (p * (dp - dl_ref[hh, atq, :][:, 0:1]) * SCALE).astype(BF16)
                dk = dk + _dot_tn(ds, qt)
                dq_ref[0, atq, wide] += _dot(ds, kt)
                out.append((dk, dv))
            return tuple(out)

        zero = (jnp.zeros((TQ, 256), F32), jnp.zeros((TQ, LANES), F32))
        first = q_tile(j, (zero, zero), True)
        done = lax.fori_loop(j + 1, nq, lambda i, carry: q_tile(i, carry, False), first)
        for hh, (dk, dv) in enumerate(done):
            dk_ref[0, :, 256 * hh:256 * (hh + 1)] = dk
            dv_ref[0, :, LANES * hh:LANES * (hh + 1)] = dv

    whole = lambda c: pl.BlockSpec((1, t, c), lambda b, h, j: (b, 0, h))
    tile = lambda c: pl.BlockSpec((1, TQ, c), lambda b, h, j: (b, j, h))
    return pl.pallas_call(
        body, name="attn_bwd", grid=(bsz, HEADS // 2, nq),
        out_shape=[jax.ShapeDtypeStruct((bsz, t, 1024), F32), jax.ShapeDtypeStruct((bsz, t, 1024), F32),
                   jax.ShapeDtypeStruct((bsz, t, 512), F32)],
        in_specs=[whole(512), tile(512), tile(256), whole(256),
                  pl.BlockSpec((1, 2, t, LANES), lambda b, h, j: (b, h, 0, 0)), whole(256)],
        out_specs=[whole(512), tile(512), tile(256)],
        scratch_shapes=[pltpu.VMEM((2, t, LANES), F32)],
        compiler_params=_cparams(("parallel", "parallel", "arbitrary")),
    )(q, k, v, o, lse, do)


RW_HEADS = 8
CH = 32


def _lane_split(bsz):
    vs = LANES // (bsz * RW_HEADS)
    return vs, 64 // vs


def _gather_matrix(bsz):
    group = bsz * RW_HEADS
    vs = LANES // group
    half = (RW_HEADS // 2) * bsz * SPREAD_STEPS
    p = np.zeros((SPREAD_STEPS // vs * LANES, 2 * half), np.float32)
    for g2 in range(SPREAD_STEPS // vs):
        for j in range(vs):
            for b in range(bsz):
                for h in range(RW_HEADS):
                    hp, hpar = h // 2, h % 2
                    p[g2 * LANES + j * group + b * RW_HEADS + h,
                      hpar * half + (hp * bsz + b) * SPREAD_STEPS + g2 * vs + j] = 1.0
    return jnp.asarray(np.concatenate([p] * 3, axis=0), BF16)


def _gather_k(ys, bsz):
    vs = LANES // (bsz * RW_HEADS)
    assert (RW_HEADS // 2) * bsz * SPREAD_STEPS == LANES, "the transposed tile must be 128 lanes wide"
    tg = ys[0].shape[0]
    n = len(ys)
    ngrp = SPREAD_BLOCK // SPREAD_STEPS
    per = SPREAD_STEPS // vs

    def body(*refs):
        pm = refs[n][...]
        for y_ref, o_ref in zip(refs[:n], refs[n + 1:]):
            lhs = jnp.concatenate(
                [jnp.concatenate(_split3(jnp.concatenate([y_ref[per * m + g2] for g2 in range(per)], axis=1)), axis=1)
                 for m in range(ngrp)], axis=0)
            a = _dot(lhs, pm)
            for m in range(ngrp):
                am = a[64 * m:64 * (m + 1)]
                bt = jnp.concatenate([am[:, 0:LANES], am[:, LANES:2 * LANES]], axis=0).T
                for hp in range(RW_HEADS // 2):
                    for b in range(bsz):
                        at = (hp * bsz + b) * SPREAD_STEPS
                        o_ref[b, SPREAD_STEPS * m:SPREAD_STEPS * (m + 1), LANES * hp:LANES * (hp + 1)] = \
                            bt[at:at + SPREAD_STEPS]

    pm = _gather_matrix(bsz)
    return pl.pallas_call(
        body, name="wkv_gather", grid=(tg * vs // SPREAD_BLOCK,),
        out_shape=[jax.ShapeDtypeStruct((bsz, tg * vs, RW), F32)] * n,
        in_specs=[pl.BlockSpec((SPREAD_BLOCK // vs, 64, LANES), lambda i: (i, 0, 0))] * n + [_full(pm.shape)],
        out_specs=[pl.BlockSpec((bsz, SPREAD_BLOCK, RW), lambda i: (0, i, 0))] * n,
        compiler_params=_cparams(("parallel",)),
    )(*ys, pm)


def _to_v(x):
    bsz, t, _ = x.shape
    vs, vq = _lane_split(bsz)
    return jnp.transpose(x.reshape(bsz, t, RW_HEADS, vq, vs), (1, 3, 4, 0, 2)).reshape(t, vq, LANES)


def _from_v(y, bsz):
    t = y.shape[0]
    vs, vq = _lane_split(bsz)
    return jnp.transpose(y.reshape(t, vq, vs, bsz, RW_HEADS), (3, 0, 4, 1, 2)).reshape(bsz, t, RW)


def _ksum(a):
    return jnp.sum(a, axis=0, keepdims=True)


def _fold(a, group):
    sh = LANES // 2
    while sh >= group:
        a = a + pltpu.roll(a, sh, 1)
        sh //= 2
    return a


def _lane_group(shape, group):
    return lax.broadcasted_iota(jnp.int32, shape, 1) // group


SPREAD_STEPS = 8
SPREAD_BLOCK = 32


def _spread_matrix(bsz):
    group = bsz * RW_HEADS
    vs = LANES // group
    rows = (RW_HEADS // 2) * bsz * SPREAD_STEPS
    q = np.zeros((2, rows, SPREAD_STEPS * LANES), np.float32)
    for hpar in range(2):
        for hp in range(RW_HEADS // 2):
            for b in range(bsz):
                for st in range(SPREAD_STEPS):
                    row = (hp * bsz + b) * SPREAD_STEPS + st
                    for s in range(vs):
                        q[hpar, row, st * LANES + s * group + b * RW_HEADS + 2 * hp + hpar] = 1.0
    return jnp.asarray(np.concatenate([q[0], q[1]] * 3, axis=0), BF16)


def _spread_k(xs):
    bsz, t, _ = xs[0].shape
    assert (RW_HEADS // 2) * bsz * SPREAD_STEPS == LANES, "the transposed tile must be 128 lanes wide"
    n = len(xs)
    ngrp = SPREAD_BLOCK // SPREAD_STEPS

    def body(*refs):
        qm = refs[n][...]
        for x_ref, o_ref in zip(refs[:n], refs[n + 1:]):
            cols = [[] for _ in range(6)]
            for m in range(ngrp):
                at = slice(SPREAD_STEPS * m, SPREAD_STEPS * (m + 1))
                x8 = jnp.concatenate([x_ref[b, at, LANES * hp:LANES * (hp + 1)]
                                      for hp in range(RW_HEADS // 2) for b in range(bsz)], axis=0)
                for pi, piece in enumerate(_split3(x8.T)):
                    cols[2 * pi].append(piece[0:64])
                    cols[2 * pi + 1].append(piece[64:128])
            lhs = jnp.concatenate([jnp.concatenate(c, axis=0) for c in cols], axis=1)
            y = _dot(lhs, qm)
            for m in range(ngrp):
                for st in range(SPREAD_STEPS):
                    o_ref[SPREAD_STEPS * m + st] = y[64 * m:64 * (m + 1), LANES * st:LANES * (st + 1)]

    qm = _spread_matrix(bsz)
    return pl.pallas_call(
        body, name="wkv_spread", grid=(t // SPREAD_BLOCK,),
        out_shape=[jax.ShapeDtypeStruct((t, 64, LANES), F32)] * n,
        in_specs=[pl.BlockSpec((bsz, SPREAD_BLOCK, RW), lambda i: (0, i, 0))] * n + [_full(qm.shape)],
        out_specs=[pl.BlockSpec((SPREAD_BLOCK, 64, LANES), lambda i: (i, 0, 0))] * n,
        compiler_params=_cparams(("parallel",)),
    )(*xs, qm)


def _wkv_fwd(r, w, kp, al, be, v):
    t, vq = v.shape[0], v.shape[1]

    def body(r_ref, w_ref, kp_ref, al_ref, be_ref, v_ref, y_ref, a_ref, u_ref, st_ref):
        @pl.when(pl.program_id(0) == 0)
        def _():
            st_ref[...] = jnp.zeros(st_ref.shape, F32)

        def step(tl, _):
            rv, wv, kv, av, bv = r_ref[tl], w_ref[tl], kp_ref[tl], al_ref[tl], be_ref[tl]
            vals = v_ref[tl]
            yrows, urows = [], []
            for q in range(vq):
                s = st_ref[q]
                u = _ksum(s * av)
                s = s * wv + bv * u + kv * vals[q:q + 1]
                st_ref[q] = s
                a_ref[tl, q] = s
                urows.append(u)
                yrows.append(_ksum(s * rv))
            y_ref[tl] = jnp.concatenate(yrows, axis=0)
            u_ref[tl] = jnp.concatenate(urows, axis=0)
            return 0

        lax.fori_loop(0, CH, step, 0)

    kspec = pl.BlockSpec((CH, 64, LANES), lambda i: (i, 0, 0))
    vspec = pl.BlockSpec((CH, vq, LANES), lambda i: (i, 0, 0))
    vsd = jax.ShapeDtypeStruct((t, vq, LANES), F32)
    return pl.pallas_call(
        body, name="wkv_fwd", grid=(t // CH,),
        out_shape=[vsd, jax.ShapeDtypeStruct((t, vq, 64, LANES), F32), vsd],
        in_specs=[kspec] * 5 + [vspec],
        out_specs=[vspec, pl.BlockSpec((CH, vq, 64, LANES), lambda i: (i, 0, 0, 0)), vspec],
        scratch_shapes=[pltpu.VMEM((vq, 64, LANES), F32)],
        compiler_params=_cparams(("arbitrary",)),
    )(r, w, kp, al, be, v)


def _wkv_bwd(r, w, kp, al, be, v, dy, states, u):
    t, vq = v.shape[0], v.shape[1]
    vs = 64 // vq
    group = LANES // vs
    n = t // CH
    ng = CH // vs

    def body(r_ref, w_ref, kp_ref, al_ref, be_ref, v_ref, dy_ref, u_ref, a_ref, ap_ref,
             dr_ref, dw_ref, dkp_ref, dal_ref, dbe_ref, dv_ref, ds_ref):
        @pl.when(pl.program_id(0) == 0)
        def _():
            ds_ref[...] = jnp.zeros(ds_ref.shape, F32)

        earliest = pl.program_id(0) == n - 1

        def reverse(i, _):
            g = ng - 1 - i
            grp = _lane_group((64, LANES), group)
            outs = None
            for j in reversed(range(vs)):
                tl = g * vs + j
                rv, wv, kv, av, bv = r_ref[tl], w_ref[tl], kp_ref[tl], al_ref[tl], be_ref[tl]
                vals, dys, us = v_ref[tl], dy_ref[tl], u_ref[tl]
                acc = None
                dvrows = []
                for q in range(vq):
                    if j > 0:
                        s_prev = a_ref[tl - 1, q]
                    else:
                        before = jnp.where(earliest, 0.0, ap_ref[0, q])
                        s_prev = jnp.where(g == 0, before, a_ref[jnp.maximum(tl - 1, 0), q])
                    dyq = dys[q:q + 1]
                    ds = ds_ref[q] + rv * dyq
                    c = _ksum(ds * bv)
                    dvrows.append(_ksum(ds * kv))
                    terms = (a_ref[tl, q] * dyq, ds * s_prev, ds * vals[q:q + 1], s_prev * c, ds * us[q:q + 1])
                    acc = terms if acc is None else tuple(a + b for a, b in zip(acc, terms))
                    ds_ref[q] = ds * wv + av * c
                dv_ref[tl] = jnp.concatenate(dvrows, axis=0)
                summed = [_fold(a, group) for a in acc]
                outs = summed if outs is None else [jnp.where(grp == j, f, o) for f, o in zip(summed, outs)]
            for ref, o in zip((dr_ref, dw_ref, dkp_ref, dal_ref, dbe_ref), outs):
                ref[g] = o
            return 0

        lax.fori_loop(0, ng, reverse, 0)

    kspec = pl.BlockSpec((CH, 64, LANES), lambda i: (n - 1 - i, 0, 0))
    gspec = pl.BlockSpec((ng, 64, LANES), lambda i: (n - 1 - i, 0, 0))
    vspec = pl.BlockSpec((CH, vq, LANES), lambda i: (n - 1 - i, 0, 0))
    ksd = jax.ShapeDtypeStruct((t // vs, 64, LANES), F32)
    return pl.pallas_call(
        body, name="wkv_bwd", grid=(n,),
        out_shape=[ksd] * 5 + [jax.ShapeDtypeStruct((t, vq, LANES), F32)],
        in_specs=[kspec] * 5 + [vspec, vspec, vspec,
                                pl.BlockSpec((CH, vq, 64, LANES), lambda i: (n - 1 - i, 0, 0, 0)),
                                pl.BlockSpec((1, vq, 64, LANES), lambda i: (jnp.maximum((n - 1 - i) * CH - 1, 0), 0, 0, 0))],
        out_specs=[gspec] * 5 + [vspec],
        scratch_shapes=[pltpu.VMEM((vq, 64, LANES), F32)],
        compiler_params=_cparams(("arbitrary",)),
    )(r, w, kp, al, be, v, dy, u, states, states)


def _post(x, tgt, pp, o, yw, r, kp, v, ln_g, ln_b, r_k, wo, wot, gpost, bo):
    bsz, t, _ = x.shape
    tt = TT_VPU
    nt = t // tt

    def body(x_ref, tgt_ref, z_ref, o_ref, yw_ref, r_ref, kp_ref, v_ref, lng_ref, lnb_ref, rk_ref, wo_ref, wot_ref,
             gpost_ref, bo_ref,
             dh_ref, dz_ref, dym_ref, dyw_ref, dbon_ref, loss_ref, dwo_ref, dgpost_ref, dlng_ref, dlnb_ref, drk_ref):
        first = (pl.program_id(0) == 0) & (pl.program_id(1) == 0)

        @pl.when(first)
        def _():
            for ref in (loss_ref, dwo_ref, dgpost_ref, dlng_ref, dlnb_ref, drk_ref):
                ref[...] = jnp.zeros(ref.shape, F32)

        bo_m = bo_ref[...]
        seg = lambda a: _seg(a, bo_m)
        rowsum = lambda a: jnp.sum(a, axis=0, keepdims=True)
        ywv, rv, kpv, vv = yw_ref[0], r_ref[0], kp_ref[0], v_ref[0]
        ln_g, r_k = lng_ref[...], rk_ref[...]
        mean = seg(ywv) * (1.0 / 64)
        yc = ywv - mean
        rstd = lax.rsqrt(seg(yc * yc) * (1.0 / 64) + GN_EPS)
        yhat = yc * rstd
        sb = seg(rv * kpv * r_k)
        y_rw = yhat * ln_g + lnb_ref[...] + sb * vv
        z = z_ref[0]
        sig = _sigmoid(z)
        sz = z * sig
        ycat = jnp.concatenate([o_ref[0], y_rw], axis=1)
        ycg = (ycat * sz).astype(BF16)
        out = _dot(ycg, wo_ref[...])
        hn, nx, rstd_o = _rms(out, gpost_ref[...], D)
        err = x_ref[0] + hn - tgt_ref[0]
        loss_ref[...] += jnp.sum(err * err) * (0.5 / D)
        dh = err * (1.0 / D)
        dh_ref[0] = dh
        dout, dgp = _rms_bwd(dh, nx, rstd_o, gpost_ref[...], D)
        dgpost_ref[...] += dgp
        doutb = dout.astype(BF16)
        dwo_ref[...] += _dot_tn(ycg, doutb)
        dycg = _dot(doutb, wot_ref[...])
        dz_ref[0] = dycg * ycat * (sig * (1.0 + z * (1.0 - sig)))
        dycat = dycg * sz
        dym_ref[0] = dycat[:, 0:512]
        dy_rw = dycat[:, 512:1024]
        dlnb_ref[...] += rowsum(dy_rw)
        dlng_ref[...] += rowsum(dy_rw * yhat)
        dyhat = dy_rw * ln_g
        dyw_ref[0] = rstd * (dyhat - seg(dyhat) * (1.0 / 64) - yhat * (seg(dyhat * yhat) * (1.0 / 64)))
        dsb = seg(dy_rw * vv)
        drk_ref[...] += rowsum(dsb * rv * kpv)
        dbon_ref[0, :, 0:512] = dsb * kpv * r_k
        dbon_ref[0, :, 512:1024] = dsb * rv * r_k
        dbon_ref[0, :, 1024:1536] = dy_rw * sb

    tok = lambda c: pl.BlockSpec((1, tt, c), lambda b, i: (b, i, 0))
    full = lambda a: _full(a.shape)
    ins = (x, tgt, pp, o, yw, r, kp, v, ln_g, ln_b, r_k, wo, wot, gpost, bo)
    in_specs = [tok(D), tok(D), tok(1024)] + [tok(512)] * 5 + [full(a) for a in ins[8:]]
    sd = lambda c: jax.ShapeDtypeStruct((bsz, t, c), F32)
    vec = lambda c: jax.ShapeDtypeStruct((1, c), F32)
    out_shape = [sd(D), sd(1024), sd(512), sd(512), sd(1536), jax.ShapeDtypeStruct((8, LANES), F32),
                 jax.ShapeDtypeStruct((1024, 1024), F32), vec(D), vec(512), vec(512), vec(512)]
    out_specs = [tok(D), tok(1024), tok(512), tok(512), tok(1536), _resident((8, LANES)), _resident((1024, 1024)),
                 _resident((1, D)), _resident((1, 512)), _resident((1, 512)), _resident((1, 512))]
    return pl.pallas_call(
        body, name="post", grid=(bsz, nt), out_shape=out_shape, in_specs=in_specs, out_specs=out_specs,
        compiler_params=_cparams(("arbitrary", "arbitrary")),
    )(*ins)


def _pre_bwd_a(pp, pos, invf, cqkv_w, mu, w0, w2p, w2pt, a0, a2p, a2pt, k_k, k_a, bo,
               dq, dk, dva, dwkv, dbon):
    gq, wuqt, gkv, wukvt = cqkv_w
    bsz, t, _ = pp.shape
    tt = TT_VPU
    nt = t // tt
    dr_w, dw_w, dkp_w, dv_w, dal_w, dbe_w = dwkv

    def body(pp_ref, pos_ref, invf_ref, gq_ref, wuqt_ref, gkv_ref, wukvt_ref, mu_ref, w0_ref, w2p_ref, w2pt_ref,
             a0_ref, a2p_ref, a2pt_ref, kk_ref, ka_ref, bo_ref, dq_ref, dk_ref, dva_ref,
             dr_ref, dw_ref, dkp_ref, dv_ref, dal_ref, dbe_ref, dbon_ref,
             da_ref, dwuq_ref, dwukv_ref, dw2p_ref, da2p_ref, dgq_ref, dgkv_ref, dmu_ref, dw0_ref, da0_ref,
             dkk_ref, dka_ref, carry):
        i = pl.program_id(1)
        first = (pl.program_id(0) == 0) & (i == 0)

        @pl.when(first)
        def _():
            for ref in (dwuq_ref, dwukv_ref, dw2p_ref, da2p_ref, dgq_ref, dgkv_ref, dmu_ref, dw0_ref, da0_ref,
                        dkk_ref, dka_ref):
                ref[...] = jnp.zeros(ref.shape, F32)

        bo_m = bo_ref[...]
        rowsum = lambda a: jnp.sum(a, axis=0, keepdims=True)
        prw = pp_ref[0, :, RW0:DP]

        @pl.when(i == 0)
        def _():
            carry[...] = jnp.zeros(carry.shape, F32)

        ps, sh = _shift_mix(prw, carry[7:8, :], mu_ref[...])
        carry[...] = prw[tt - 8:tt, :]
        k_k, k_a = kk_ref[...], ka_ref[...]
        g = _rw_gates(ps, w0_ref[...], w2p_ref[...], a0_ref[...], a2p_ref[...], k_k, k_a, bo_m)
        a, kk, k = g["a"], g["kk"], g["k"]
        dr = dr_ref[0] + dbon_ref[0, :, 0:512]
        dkp = dkp_ref[0] + dbon_ref[0, :, 512:1024]
        dv = dv_ref[0] + dbon_ref[0, :, 1024:1536]
        dbe = dbe_ref[0]
        dkk = dbe * a - dal_ref[0]
        da = dbe * kk + dkp * k * k_a
        dka_ref[...] += rowsum(dkp * k * (a - 1.0))
        dm = (dkk - kk * _seg(dkk * kk, bo_m)) / g["nrm"]
        dkk_ref[...] += rowsum(dm * k)
        dk_tot = dkp * (1.0 + (a - 1.0) * k_a) + dm * k_k
        dapre = da * a * (1.0 - a)
        da0_ref[...] += rowsum(dapre)
        dapb = dapre.astype(BF16)
        da2p_ref[...] += _dot_tn(g["misc"].astype(BF16), dapb)
        dwpre = dw_ref[0] * g["w"] * (-g["e"]) * _sigmoid(-g["wpre"])
        dw0_ref[...] += rowsum(dwpre)
        dwpb = dwpre.astype(BF16)
        th = g["th"]
        dw2p_ref[...] += _dot_tn(th.astype(BF16), dwpb)
        dmisc = _dot(dapb, a2pt_ref[...]) + _dot(dwpb, w2pt_ref[...]) * (1.0 - th * th)
        ang = pos_ref[0] * invf_ref[...]
        cs, sn = jnp.cos(ang), jnp.sin(ang)
        unrope = lambda gr: gr * cs - _rot(gr * sn)
        lane = lax.broadcasted_iota(jnp.int32, cs.shape, 1)
        dkr = dk_ref[0, :, 128:256]
        for h in range(1, HEADS):
            dkr = dkr + dk_ref[0, :, 256 * h + 128:256 * h + 256]
        dkr = jnp.where(lane < 64, unrope(dkr), 0.0)
        dmisc = dmisc + jnp.concatenate([dkr, jnp.zeros_like(dkr)], axis=1)
        dqp = jnp.concatenate(
            [blk for h in range(HEADS)
             for blk in (dq_ref[0, :, 256 * h:256 * h + 128], unrope(dq_ref[0, :, 256 * h + 128:256 * h + 256]))],
            axis=1).astype(BF16)
        dkvp = jnp.concatenate([dk_ref[0, :, 256 * h:256 * h + 128] for h in range(HEADS)] + [dva_ref[0]],
                               axis=1).astype(BF16)
        cqn, cq_nx, cq_rstd = _rms(pp_ref[0, :, CQ0:CQ0 + 256], gq_ref[...], 256)
        ckvn, ckv_nx, ckv_rstd = _rms(pp_ref[0, :, CKV0:CKV0 + 128], gkv_ref[...], 128)
        dwuq_ref[...] += _dot_tn(cqn.astype(BF16), dqp)
        dwukv_ref[...] += _dot_tn(ckvn.astype(BF16), dkvp)
        dcq, dgq = _rms_bwd(_dot(dqp, wuqt_ref[...]), cq_nx, cq_rstd, gq_ref[...], 256)
        dckv, dgkv = _rms_bwd(_dot(dkvp, wukvt_ref[...]), ckv_nx, ckv_rstd, gkv_ref[...], 128)
        dgq_ref[...] += dgq
        dgkv_ref[...] += dgkv
        dps = jnp.concatenate([dr, dk_tot, dv, dmisc], axis=1)
        dmu_ref[...] += rowsum(dps * (sh - prw))
        da_ref[0, :, 0:256] = dcq
        da_ref[0, :, 256:384] = dckv
        da_ref[0, :, 384:384 + NRW] = dps

    tok = lambda c: pl.BlockSpec((1, tt, c), lambda b, i: (b, i, 0))
    full = lambda a: _full(a.shape)
    ins = (pp, pos, invf, gq, wuqt, gkv, wukvt, mu, w0, w2p, w2pt, a0, a2p, a2pt, k_k, k_a, bo,
           dq, dk, dva, dr_w, dw_w, dkp_w, dv_w, dal_w, dbe_w, dbon)
    in_specs = ([tok(DP), tok(1)] + [full(a) for a in ins[2:17]] + [tok(1024), tok(1024), tok(512)]
                + [tok(512)] * 6 + [tok(1536)])
    shp = lambda *s: jax.ShapeDtypeStruct(s, F32)
    out_shape = [shp(bsz, t, 384 + NRW), shp(256, 1024), shp(128, 1024), shp(256, 512), shp(256, 512),
                 shp(1, 256), shp(1, 128), shp(1, NRW), shp(1, 512), shp(1, 512), shp(1, 512), shp(1, 512)]
    out_specs = [tok(384 + NRW)] + [_resident(s.shape) for s in out_shape[1:]]
    return pl.pallas_call(
        body, name="pre_bwd_a", grid=(bsz, nt), out_shape=out_shape, in_specs=in_specs, out_specs=out_specs,
        scratch_shapes=[pltpu.VMEM((8, NRW), F32)],
        compiler_params=_cparams(("arbitrary", "arbitrary")),
    )(*ins)


def _pre_bwd_b(x, dh, dz, da, mu, wpt, gpre):
    bsz, t, _ = x.shape
    nt = t // TT
    nblk = t // 8

    def body(x_ref, dh_ref, dz_ref, da_ref, nxt_ref, mu_ref, wpt_ref, gpre_ref, gx_ref, dp_ref, dgpre_ref):
        i = pl.program_id(1)
        first = (pl.program_id(0) == 0) & (i == 0)

        @pl.when(first)
        def _():
            dgpre_ref[...] = jnp.zeros(dgpre_ref.shape, F32)

        mu_v = mu_ref[...]
        dps = da_ref[0, :, 384:384 + NRW]
        nxt = jnp.where(i < nt - 1, nxt_ref[0, 0:1, 384:384 + NRW], 0.0)
        row = lax.broadcasted_iota(jnp.int32, dps.shape, 0)
        up = jnp.where(row == TT - 1, nxt, pltpu.roll(dps, TT - 1, 0))
        dprw = dps * (1.0 - mu_v) + up * mu_v
        dp = jnp.concatenate([dz_ref[0], da_ref[0, :, 0:384], dprw], axis=1).astype(BF16)
        dp_ref[0] = dp
        du = _dot(dp, wpt_ref[...])
        _, nx, rstd = _rms(x_ref[0], gpre_ref[...], D)
        dx, dg = _rms_bwd(du, nx, rstd, gpre_ref[...], D)
        dgpre_ref[...] += dg
        gx_ref[0] = dh_ref[0] + dx

    tok = lambda c: pl.BlockSpec((1, TT, c), lambda b, i: (b, i, 0))
    nxt_spec = pl.BlockSpec((1, 8, 384 + NRW), lambda b, i: (b, jnp.minimum((i + 1) * (TT // 8), nblk - 1), 0))
    ins = (x, dh, dz, da, da, mu, wpt, gpre)
    return pl.pallas_call(
        body, name="pre_bwd_b", grid=(bsz, nt),
        out_shape=[jax.ShapeDtypeStruct((bsz, t, D), F32), jax.ShapeDtypeStruct((bsz, t, DP), BF16),
                   jax.ShapeDtypeStruct((1, D), F32)],
        in_specs=[tok(D), tok(D), tok(1024), tok(384 + NRW), nxt_spec, _full(mu.shape), _full(wpt.shape),
                  _full(gpre.shape)],
        out_specs=[tok(D), tok(DP), _resident((1, D))],
        compiler_params=_cparams(("arbitrary", "arbitrary")),
    )(*ins)


def _tn_matmul(a, b, bn, name, bk=512):
    kdim, m = a.shape
    _, n = b.shape
    nk = kdim // bk

    def body(a_ref, b_ref, o_ref):
        @pl.when(pl.program_id(1) == 0)
        def _():
            o_ref[...] = jnp.zeros(o_ref.shape, F32)

        o_ref[...] += _dot_tn(a_ref[...], b_ref[...])

    return pl.pallas_call(
        body, name=name, grid=(n // bn, nk),
        out_shape=jax.ShapeDtypeStruct((m, n), F32),
        in_specs=[pl.BlockSpec((bk, m), lambda j, kk: (kk, 0)), pl.BlockSpec((bk, bn), lambda j, kk: (kk, j))],
        out_specs=pl.BlockSpec((m, bn), lambda j, kk: (0, j)),
        compiler_params=_cparams(("parallel", "arbitrary")),
    )(a, b)


SHARDED = ("w_in", "mla_w_uq", "mla_w_ukv", "rw_w2", "rw_a2", "w_out")
SMALL = ("norm_pre_g", "mla_q_norm_g", "mla_kv_norm_g", "rw_mu", "rw_w0", "rw_a0", "rw_k_k", "rw_k_a", "rw_r_k",
         "rw_ln_g", "rw_ln_b", "norm_post_g")
WEIGHTS = ("norm_pre_g", "w_in", "mla_q_norm_g", "mla_w_uq", "mla_kv_norm_g", "mla_w_ukv", "rw_mu", "rw_w0", "rw_w2",
           "rw_a0", "rw_a2", "rw_k_k", "rw_k_a", "rw_r_k", "rw_ln_g", "rw_ln_b", "w_out", "norm_post_g")


def _pack_small(d):
    flat = jnp.concatenate([d[n].reshape(1, -1) for n in SMALL], axis=1)
    return jnp.pad(flat, ((0, 0), (0, SMALL_ROWS * LANES - flat.shape[1]))).reshape(SMALL_ROWS, LANES)


def _unpack_small(packed, like):
    flat = packed.reshape(1, -1)
    out, at = {}, 0
    for n in SMALL:
        size = int(np.prod(like[n].shape))
        out[n] = flat[:, at:at + size].reshape(like[n].shape)
        at += size
    return out


def _unpack_shard(packed, like):
    out, at = {}, 0
    for n, rows in zip(SHARDED, PACK_ROWS):
        out[n] = packed[at:at + rows].reshape(like[n].shape)
        at += rows
    return out


def _constants():
    bo = np.kron(np.eye(2, dtype=np.float32), np.ones((64, 64), np.float32))
    inv = ROPE_THETA ** (-np.arange(0, 64, 2, dtype=np.float32) / 64)
    invf = np.concatenate([inv, inv, np.zeros(64, np.float32)]).astype(np.float32)[None, :]
    return jnp.asarray(bo, BF16), jnp.asarray(invf)


def kernel(x, positions, norm_pre_g, w_in, mla_q_norm_g, mla_w_uq, mla_kv_norm_g, mla_w_ukv, rw_mu, rw_w0, rw_w2, rw_a0, rw_a2, rw_k_k, rw_k_a, rw_r_k, rw_ln_g, rw_ln_b, w_out, norm_post_g, loss_target, m_norm_pre_g, m_w_in, m_mla_q_norm_g, m_mla_w_uq, m_mla_kv_norm_g, m_mla_w_ukv, m_rw_mu, m_rw_w0, m_rw_w2, m_rw_a0, m_rw_a2, m_rw_k_k, m_rw_k_a, m_rw_r_k, m_rw_ln_g, m_rw_ln_b, m_w_out, m_norm_post_g, v_norm_pre_g, v_w_in, v_mla_q_norm_g, v_mla_w_uq, v_mla_kv_norm_g, v_mla_w_ukv, v_rw_mu, v_rw_w0, v_rw_w2, v_rw_a0, v_rw_a2, v_rw_k_k, v_rw_k_a, v_rw_r_k, v_rw_ln_g, v_rw_ln_b, v_w_out, v_norm_post_g):
    wts = dict(norm_pre_g=norm_pre_g, w_in=w_in, mla_q_norm_g=mla_q_norm_g, mla_w_uq=mla_w_uq,
               mla_kv_norm_g=mla_kv_norm_g, mla_w_ukv=mla_w_ukv, rw_mu=rw_mu, rw_w0=rw_w0, rw_w2=rw_w2, rw_a0=rw_a0,
               rw_a2=rw_a2, rw_k_k=rw_k_k, rw_k_a=rw_k_a, rw_r_k=rw_r_k, rw_ln_g=rw_ln_g, rw_ln_b=rw_ln_b, w_out=w_out,
               norm_post_g=norm_post_g)
    mom_m = dict(norm_pre_g=m_norm_pre_g, w_in=m_w_in, mla_q_norm_g=m_mla_q_norm_g, mla_w_uq=m_mla_w_uq,
                 mla_kv_norm_g=m_mla_kv_norm_g, mla_w_ukv=m_mla_w_ukv, rw_mu=m_rw_mu, rw_w0=m_rw_w0, rw_w2=m_rw_w2,
                 rw_a0=m_rw_a0, rw_a2=m_rw_a2, rw_k_k=m_rw_k_k, rw_k_a=m_rw_k_a, rw_r_k=m_rw_r_k, rw_ln_g=m_rw_ln_g,
                 rw_ln_b=m_rw_ln_b, w_out=m_w_out, norm_post_g=m_norm_post_g)
    mom_v = dict(norm_pre_g=v_norm_pre_g, w_in=v_w_in, mla_q_norm_g=v_mla_q_norm_g, mla_w_uq=v_mla_w_uq,
                 mla_kv_norm_g=v_mla_kv_norm_g, mla_w_ukv=v_mla_w_ukv, rw_mu=v_rw_mu, rw_w0=v_rw_w0, rw_w2=v_rw_w2,
                 rw_a0=v_rw_a0, rw_a2=v_rw_a2, rw_k_k=v_rw_k_k, rw_k_a=v_rw_k_a, rw_r_k=v_rw_r_k, rw_ln_g=v_rw_ln_g,
                 rw_ln_b=v_rw_ln_b, w_out=v_w_out, norm_post_g=v_norm_post_g)
    bsz, t, _ = x.shape
    bo, invf = _constants()

    g_in, g_uq, g_ukv, g_w2, g_a2, g_out = _ag_weights([wts[n][0] for n in SHARDED])
    w_in_f = jnp.transpose(g_in, (1, 0, 2)).reshape(D, D_IN)
    wp = jnp.concatenate([w_in_f[:, 2112:3136], w_in_f[:, 0:384], w_in_f[:, 448:1984], w_in_f[:, 384:448],
                          w_in_f[:, 1984:2112], jnp.zeros((D, 64), BF16)], axis=1)
    wuq = jnp.pad(jnp.transpose(g_uq, (1, 0, 2)).reshape(256, HEADS, 192), ((0, 0), (0, 0), (0, 64))).reshape(256, 1024)
    wukv = jnp.transpose(jnp.transpose(g_ukv, (1, 0, 2)).reshape(128, HEADS, 2, 128), (0, 2, 1, 3)).reshape(128, 1024)
    w2 = jnp.transpose(g_w2, (1, 0, 2)).reshape(64, RW)
    a2 = jnp.transpose(g_a2, (1, 0, 2)).reshape(64, RW)
    w2p = jnp.pad(w2, ((64, 128), (0, 0)))
    a2p = jnp.pad(a2, ((128, 64), (0, 0)))
    wo = g_out.reshape(D, D)
    mu = jnp.concatenate([rw_mu[:, 0:1536], jnp.zeros((1, 64), F32), rw_mu[:, 1536:1664], jnp.zeros((1, 64), F32)],
                         axis=1)
    r_k = rw_r_k.reshape(1, RW)
    pos = positions.astype(F32)[:, :, None]

    (u, pp, q_att, k_att, v_att, r, w, kp, v, al, be) = _pre_fwd(
        x, pos, invf, norm_pre_g, wp, mla_q_norm_g, wuq, mla_kv_norm_g, wukv, mu, rw_w0, w2p, rw_a0, a2p, rw_k_k,
        rw_k_a, bo)
    o, lse = _attn_fwd(q_att, k_att, v_att)
    rw_k = _spread_k([r, w, kp, al, be])
    v_v = _to_v(v)
    yw_v, states, u_v = _wkv_fwd(*rw_k, v_v)
    yw = _from_v(yw_v, bsz)

    (dh, dz, dym, dyw, dbon, loss_acc, d_wo, d_gpost, d_lng, d_lnb, d_rk) = _post(
        x, loss_target, pp, o, yw, r, kp, v, rw_ln_g, rw_ln_b, r_k, wo, wo.T, norm_post_g, bo)

    d_k = _wkv_bwd(*rw_k, v_v, _to_v(dyw), states, u_v)
    dr_w, dw_w, dkp_w, dal_w, dbe_w = _gather_k(d_k[:5], bsz)
    dwkv = (dr_w, dw_w, dkp_w, _from_v(d_k[5], bsz), dal_w, dbe_w)
    dq, dk, dva = _attn_bwd(q_att, k_att, v_att, o, lse, dym)

    (da, d_wuq, d_wukv, d_w2p, d_a2p, d_gq, d_gkv, d_mu, d_w0, d_a0, d_kk, d_ka) = _pre_bwd_a(
        pp, pos, invf, (mla_q_norm_g, wuq.T, mla_kv_norm_g, wukv.T), mu, rw_w0, w2p, w2p.T, rw_a0, a2p, a2p.T,
        rw_k_k, rw_k_a, bo, dq, dk, dva, dwkv, dbon)
    grad_x, dpb, d_gpre = _pre_bwd_b(x, dh, dz, da, mu, wp.T, norm_pre_g)
    d_wp = _tn_matmul(u.reshape(bsz * t, D), dpb.reshape(bsz * t, DP), DP, "dw_in")

    full_g = {
        "w_in": jnp.concatenate([d_wp[:, 1024:1408], d_wp[:, 2944:3008], d_wp[:, 1408:2944], d_wp[:, 3008:3136],
                                 d_wp[:, 0:1024]], axis=1),
        "mla_w_uq": d_wuq.reshape(256, HEADS, 256)[:, :, :192].reshape(256, 768),
        "mla_w_ukv": jnp.transpose(d_wukv.reshape(128, 2, HEADS, 128), (0, 2, 1, 3)).reshape(128, 1024),
        "rw_w2": d_w2p[64:128],
        "rw_a2": d_a2p[128:192],
        "w_out": d_wo,
    }
    small_g = {
        "norm_pre_g": d_gpre, "mla_q_norm_g": d_gq, "mla_kv_norm_g": d_gkv,
        "rw_mu": jnp.concatenate([d_mu[:, 0:1536], d_mu[:, 1600:1728]], axis=1),
        "rw_w0": d_w0, "rw_a0": d_a0, "rw_k_k": d_kk, "rw_k_a": d_ka, "rw_r_k": d_rk, "rw_ln_g": d_lng,
        "rw_ln_b": d_lnb, "norm_post_g": d_gpost,
    }

    def by_shard(name, g):
        if name == "w_out":
            return g.reshape(N_SHARD, -1, LANES)
        rows, cols = g.shape
        return jnp.transpose(g.reshape(rows, N_SHARD, cols // N_SHARD), (1, 0, 2)).reshape(N_SHARD, -1, LANES)

    packed = jnp.concatenate([by_shard(n, full_g[n]) for n in SHARDED], axis=1)
    pair_sum, pair_sum_b = _rs_pairs(packed.reshape(N_SHARD, 2, HALF, LANES))
    g_shard = _rs_chips(pair_sum, pair_sum_b).reshape(PACK_TOTAL, LANES)

    g_small = _small_allreduce(jnp.concatenate([_pack_small(small_g)[:SMALL_USED], loss_acc[0:SMALL_ROWS - SMALL_USED]]))
    loss = g_small[SMALL_USED, 0]

    g_sharded = _unpack_shard(g_shard, {n: wts[n][0] for n in SHARDED})
    sh = _adamw([wts[n][0] for n in SHARDED], [g_sharded[n] for n in SHARDED], [mom_m[n][0] for n in SHARDED],
                [mom_v[n][0] for n in SHARDED], "adamw_sharded")
    sm = _adamw([_pack_small(wts)], [g_small], [_pack_small(mom_m)], [_pack_small(mom_v)], "adamw_small")

    def outputs(sharded, small):
        out = {n: a[None] for n, a in zip(SHARDED, sharded)}
        out.update(_unpack_small(small, wts))
        return out

    grads = outputs([g_sharded[n] for n in SHARDED], g_small)
    deltas, new_m, new_v = (outputs(sh[k], sm[k][0]) for k in range(3))
    return (loss, grad_x, *[grads[n] for n in WEIGHTS], *[deltas[n] for n in WEIGHTS],
            *[new_m[n] for n in WEIGHTS], *[new_v[n] for n in WEIGHTS])
```

```python
import functools

import numpy as np
import jax
import jax.numpy as jnp
from jax import lax
from jax.experimental import pallas as pl
from jax.experimental.pallas import tpu as pltpu

F32, BF16 = jnp.float32, jnp.bfloat16
MESH = pl.DeviceIdType.MESH

D = 1024
HEADS = 4
RW = 512
NORM_EPS = 1e-6
GN_EPS = 64e-5
ROPE_THETA = 10000.0
SCALE = (128 + 64) ** -0.5
D_IN = 3136
LR, B1, B2, ADAM_EPS, WD, STEP = 0.001, 0.9, 0.999, 1e-08, 0.01, 10

Z0, CQ0, CKV0, RW0, DP = 0, 1024, 1280, 1408, 3200
NRW = DP - RW0

LANES = 128
SUBLANES = 8
VMEM_LIMIT = 56 * 1024 * 1024

TT = 512
TT_VPU = 256
TQ = 512

N_SHARD = 4
PACK_ROWS = (1024 * 784 // 128, 256 * 192 // 128, 128 * 256 // 128, 64, 64, 256 * 1024 // 128)
PACK_TOTAL = sum(PACK_ROWS)
HALF = PACK_TOTAL // 2
SMALL_ROWS = 64
SMALL_USED = 60


def _cparams(sem=None):
    return pltpu.CompilerParams(dimension_semantics=sem, vmem_limit_bytes=VMEM_LIMIT)


def _full(shape):
    n = len(shape)
    return pl.BlockSpec(shape, lambda *_: (0,) * n, pipeline_mode=pl.Buffered(1))


def _resident(shape):
    n = len(shape)
    return pl.BlockSpec(shape, lambda *_: (0,) * n)


def _dot(a, b):
    return jnp.dot(a, b, preferred_element_type=F32)


def _dot_nt(a, b):
    return lax.dot_general(a, b, (((1,), (1,)), ((), ())), preferred_element_type=F32)


def _dot_tn(a, b):
    return lax.dot_general(a, b, (((0,), (0,)), ((), ())), preferred_element_type=F32)


def _split3(x):
    hi = x.astype(BF16)
    r1 = x - hi.astype(F32)
    mid = r1.astype(BF16)
    lo = (r1 - mid.astype(F32)).astype(BF16)
    return hi, mid, lo


def _seg(x, bo):
    rows, nblk = x.shape[0], x.shape[1] // LANES
    pieces = [p for i in range(nblk) for p in _split3(x[:, LANES * i:LANES * (i + 1)])]
    res = _dot(jnp.concatenate(pieces, axis=0), bo)
    parts = [res[(3 * i) * rows:(3 * i + 1) * rows] + res[(3 * i + 1) * rows:(3 * i + 2) * rows]
             + res[(3 * i + 2) * rows:(3 * i + 3) * rows] for i in range(nblk)]
    return parts[0] if nblk == 1 else jnp.concatenate(parts, axis=1)


def _rms(x, g, n):
    rstd = lax.rsqrt(jnp.sum(x * x, axis=-1, keepdims=True) * (1.0 / n) + NORM_EPS)
    nx = x * rstd
    return nx * g, nx, rstd


def _rms_bwd(dy, nx, rstd, g, n):
    dn = dy * g
    dx = rstd * (dn - nx * (jnp.sum(dn * nx, axis=-1, keepdims=True) * (1.0 / n)))
    return dx, jnp.sum(dy * nx, axis=0, keepdims=True)


def _rot(x):
    lane = lax.broadcasted_iota(jnp.int32, x.shape, 1)
    return jnp.where((lane % 64) < 32, -pltpu.roll(x, x.shape[1] - 32, 1), pltpu.roll(x, 32, 1))


def _sigmoid(x):
    return 1.0 / (1.0 + jnp.exp(-x))


def _softplus(x):
    return jnp.maximum(x, 0.0) + jnp.log(1.0 + jnp.exp(-jnp.abs(x)))


def _rw_gates(ps, w0, w2p, a0, a2p, k_k, k_a, bo):
    r, k, v, misc = ps[:, 0:512], ps[:, 512:1024], ps[:, 1024:1536], ps[:, 1536:NRW]
    th = jnp.tanh(misc)
    wpre = w0 + _dot(th.astype(BF16), w2p)
    e = jnp.exp(-_softplus(-wpre) - 0.5)
    w = jnp.exp(-e)
    a = _sigmoid(a0 + _dot(misc.astype(BF16), a2p))
    m = k * k_k
    nrm = jnp.maximum(jnp.sqrt(_seg(m * m, bo)), 1e-12)
    kk = m / nrm
    kp = k * (1.0 + (a - 1.0) * k_a)
    return dict(r=r, k=k, v=v, misc=misc, th=th, wpre=wpre, e=e, w=w, a=a, nrm=nrm, kk=kk, kp=kp)


def _shift_mix(prw, prev_row, mu):
    row = lax.broadcasted_iota(jnp.int32, prw.shape, 0)
    sh = jnp.where(row == 0, prev_row, pltpu.roll(prw, 1, 0))
    return prw + (sh - prw) * mu, sh


def _ag_weights(shards):
    n = len(shards)

    def body(*refs):
        ins, outs = refs[:n], refs[n:2 * n]
        ici_send, ici_recv, d2d_send, d2d_recv = refs[2 * n:2 * n + 4]
        x, y, c = lax.axis_index("x"), lax.axis_index("y"), lax.axis_index("c")
        mine = 2 * x + y
        for w in range(n):
            outs[w][mine] = ins[w][...].astype(BF16)
        flips = ((1, 0), (0, 1), (1, 1))

        def half(w, shard, cc):
            rows = outs[w].shape[1] // 2
            return outs[w].at[shard, pl.ds(pl.multiple_of(cc * rows, 16), rows)]

        def ici(w, k, shard):
            fx, fy = flips[k]
            return pltpu.make_async_remote_copy(
                src_ref=half(w, shard, c), dst_ref=half(w, shard, c),
                send_sem=ici_send.at[w * 3 + k], recv_sem=ici_recv.at[w * 3 + k],
                device_id=(x ^ fx, y ^ fy, c), device_id_type=MESH)

        def d2d(w, k, cc):
            fx, fy = flips[k]
            theirs = 2 * (x ^ fx) + (y ^ fy)
            return pltpu.make_async_remote_copy(
                src_ref=half(w, theirs, cc), dst_ref=half(w, theirs, cc),
                send_sem=d2d_send.at[w * 3 + k], recv_sem=d2d_recv.at[w * 3 + k],
                device_id=(x, y, 1 - c), device_id_type=MESH)

        for w in range(n):
            for k in range(3):
                ici(w, k, mine).start()
        for w in range(n):
            for k in range(3):
                fx, fy = flips[k]
                ici(w, k, 2 * (x ^ fx) + (y ^ fy)).wait_recv()
                d2d(w, k, c).start()
        for w in range(n):
            for k in range(3):
                d2d(w, k, 1 - c).wait_recv()
        for w in range(n):
            for k in range(3):
                ici(w, k, mine).wait_send()
                d2d(w, k, c).wait_send()

    vm = pl.BlockSpec(memory_space=pltpu.VMEM)
    return pl.pallas_call(
        body, name="ag_weights",
        out_shape=[jax.ShapeDtypeStruct((N_SHARD,) + s.shape, BF16) for s in shards],
        in_specs=[vm] * n, out_specs=[vm] * n,
        scratch_shapes=[pltpu.SemaphoreType.DMA((3 * n,))] * 4,
        compiler_params=pltpu.CompilerParams(vmem_limit_bytes=VMEM_LIMIT),
    )(*shards)


def _rs_pairs(halves):
    def body(h_ref, sum_ref, sumb_ref, recv, send_sem, recv_sem):
        x, y, c = lax.axis_index("x"), lax.axis_index("y"), lax.axis_index("c")
        cps = [pltpu.make_async_remote_copy(src_ref=h_ref.at[s, 1 - c], dst_ref=recv.at[s], send_sem=send_sem.at[s],
                                            recv_sem=recv_sem.at[s], device_id=(x, y, 1 - c), device_id_type=MESH)
               for s in range(N_SHARD)]
        for cp in cps:
            cp.start()
        for s, cp in enumerate(cps):
            cp.wait_recv()
            acc = h_ref[s, c] + recv[s]
            sum_ref[s] = acc
            sumb_ref[s] = acc.astype(BF16)
        for cp in cps:
            cp.wait_send()

    vm = pl.BlockSpec(memory_space=pltpu.VMEM)
    shape = (N_SHARD,) + halves.shape[2:]
    return pl.pallas_call(
        body, name="rs_pairs",
        out_shape=[jax.ShapeDtypeStruct(shape, F32), jax.ShapeDtypeStruct(shape, BF16)],
        in_specs=[vm], out_specs=[vm, vm],
        scratch_shapes=[pltpu.VMEM(shape, F32), pltpu.SemaphoreType.DMA((N_SHARD,)),
                        pltpu.SemaphoreType.DMA((N_SHARD,))],
        compiler_params=pltpu.CompilerParams(vmem_limit_bytes=VMEM_LIMIT),
    )(halves)


def _rs_chips(part_f32, part_bf16):
    def body(own_ref, src_ref, out_ref, recv, ici_send, ici_recv, d2d_send, d2d_recv):
        x, y, c = lax.axis_index("x"), lax.axis_index("y"), lax.axis_index("c")
        mine = 2 * x + y
        flips = ((1, 0), (0, 1), (1, 1))
        cps = []
        for k, (fx, fy) in enumerate(flips):
            theirs = 2 * (x ^ fx) + (y ^ fy)
            cps.append(pltpu.make_async_remote_copy(
                src_ref=src_ref.at[theirs], dst_ref=recv.at[k],
                send_sem=ici_send.at[k], recv_sem=ici_recv.at[k],
                device_id=(x ^ fx, y ^ fy, c), device_id_type=MESH))
        for cp in cps:
            cp.start()
        acc = own_ref[mine]
        for k, cp in enumerate(cps):
            cp.wait_recv()
            acc = acc + recv[k].astype(F32)
        out_ref[c] = acc
        to_sibling = pltpu.make_async_remote_copy(
            src_ref=out_ref.at[c], dst_ref=out_ref.at[c], send_sem=d2d_send, recv_sem=d2d_recv,
            device_id=(x, y, 1 - c), device_id_type=MESH)
        to_sibling.start()
        pltpu.make_async_remote_copy(
            src_ref=out_ref.at[1 - c], dst_ref=out_ref.at[1 - c], send_sem=d2d_send, recv_sem=d2d_recv,
            device_id=(x, y, 1 - c), device_id_type=MESH).wait_recv()
        to_sibling.wait_send()
        for cp in cps:
            cp.wait_send()

    vm = pl.BlockSpec(memory_space=pltpu.VMEM)
    return pl.pallas_call(
        body, name="rs_chips",
        out_shape=jax.ShapeDtypeStruct((2,) + part_f32.shape[1:], F32),
        in_specs=[vm, vm], out_specs=vm,
        scratch_shapes=[pltpu.VMEM((3,) + part_bf16.shape[1:], BF16), pltpu.SemaphoreType.DMA((3,)),
                        pltpu.SemaphoreType.DMA((3,)), pltpu.SemaphoreType.DMA, pltpu.SemaphoreType.DMA],
        compiler_params=pltpu.CompilerParams(vmem_limit_bytes=VMEM_LIMIT),
    )(part_f32, part_bf16)


def _small_allreduce(vec):
    def body(in_ref, out_ref, recv, send_sems, recv_sems):
        x, y, c = lax.axis_index("x"), lax.axis_index("y"), lax.axis_index("c")
        me = 4 * x + 2 * y + c
        cps = []
        for k in range(1, 8):
            fx, fy, fc = (k >> 2) & 1, (k >> 1) & 1, k & 1
            cps.append(pltpu.make_async_remote_copy(
                src_ref=in_ref, dst_ref=recv.at[k - 1],
                send_sem=send_sems.at[k - 1], recv_sem=recv_sems.at[k - 1],
                device_id=(x ^ fx, y ^ fy, c ^ fc), device_id_type=MESH))
        for cp in cps:
            cp.start()
        for cp in cps:
            cp.wait()
        acc = jnp.zeros(in_ref.shape, F32)
        for j in range(8):
            slot = jnp.maximum((me ^ j) - 1, 0)
            acc = acc + jnp.where(me == j, in_ref[...], recv[slot])
        out_ref[...] = acc

    vm = pl.BlockSpec(memory_space=pltpu.VMEM)
    return pl.pallas_call(
        body, name="small_allreduce",
        out_shape=jax.ShapeDtypeStruct(vec.shape, F32),
        in_specs=[vm], out_specs=vm,
        scratch_shapes=[pltpu.VMEM((7,) + vec.shape, F32), pltpu.SemaphoreType.DMA((7,)),
                        pltpu.SemaphoreType.DMA((7,))],
    )(vec)


ADAM_ROWS = 64


def _adamw(ws, gs, ms, vs, name):
    n = len(ws)

    def body(*refs):
        for i in range(n):
            w_ref, g_ref, m_ref, v_ref = (refs[k * n + i] for k in range(4))
            d_ref, nm_ref, nv_ref = (refs[(4 + k) * n + i] for k in range(3))
            rows = min(ADAM_ROWS, w_ref.shape[0])

            def chunk(r, _):
                at = pl.ds(pl.multiple_of(r * rows, SUBLANES), rows)
                gg = g_ref[at, :]
                nm = B1 * m_ref[at, :] + (1.0 - B1) * gg
                nv = B2 * v_ref[at, :] + (1.0 - B2) * (gg * gg)
                m_hat = nm / (1.0 - B1 ** STEP)
                v_hat = nv / (1.0 - B2 ** STEP)
                d_ref[at, :] = -LR * (m_hat / (jnp.sqrt(v_hat) + ADAM_EPS) + WD * w_ref[at, :])
                nm_ref[at, :] = nm
                nv_ref[at, :] = nv
                return 0

            lax.fori_loop(0, w_ref.shape[0] // rows, chunk, 0)

    vm = pl.BlockSpec(memory_space=pltpu.VMEM)
    sds = [jax.ShapeDtypeStruct(w.shape, F32) for w in ws]
    outs = pl.pallas_call(
        body, name=name, out_shape=sds * 3, in_specs=[vm] * (4 * n), out_specs=[vm] * (3 * n),
        compiler_params=pltpu.CompilerParams(vmem_limit_bytes=VMEM_LIMIT),
    )(*ws, *gs, *ms, *vs)
    return outs[:n], outs[n:2 * n], outs[2 * n:]


def _pre_fwd(x, pos, invf, gpre, wp, gq, wuq, gkv, wukv, mu, w0, w2p, a0, a2p, k_k, k_a, bo):
    bsz, t, _ = x.shape
    nt = t // TT

    def body(x_ref, pos_ref, invf_ref, gpre_ref, wp_ref, gq_ref, wuq_ref, gkv_ref, wukv_ref, mu_ref, w0_ref,
             w2p_ref, a0_ref, a2p_ref, kk_ref, ka_ref, bo_ref,
             u_ref, pp_ref, q_ref, k_ref, v_ref, r_o, w_o, kp_o, vv_o, al_o, be_o, carry):
        i = pl.program_id(1)
        u, _, _ = _rms(x_ref[0], gpre_ref[...], D)
        ub = u.astype(BF16)
        u_ref[0] = ub
        p = _dot(ub, wp_ref[...])
        pp_ref[0] = p
        prw = p[:, RW0:DP]

        @pl.when(i == 0)
        def _():
            carry[...] = jnp.zeros(carry.shape, F32)

        ps, _ = _shift_mix(prw, carry[7:8, :], mu_ref[...])
        carry[...] = prw[TT - 8:TT, :]

        g = _rw_gates(ps, w0_ref[...], w2p_ref[...], a0_ref[...], a2p_ref[...], kk_ref[...], ka_ref[...],
                      bo_ref[...])
        r_o[0] = g["r"]
        w_o[0] = g["w"]
        kp_o[0] = g["kp"]
        vv_o[0] = g["v"]
        al_o[0] = -g["kk"]
        be_o[0] = g["kk"] * g["a"]

        cqn, _, _ = _rms(p[:, CQ0:CQ0 + 256], gq_ref[...], 256)
        q = _dot(cqn.astype(BF16), wuq_ref[...])
        ckvn, _, _ = _rms(p[:, CKV0:CKV0 + 128], gkv_ref[...], 128)
        kv = _dot(ckvn.astype(BF16), wukv_ref[...])
        ang = pos_ref[0] * invf_ref[...]
        cs, sn = jnp.cos(ang), jnp.sin(ang)
        lane = lax.broadcasted_iota(jnp.int32, cs.shape, 1)
        kr = ps[:, 1536:1536 + LANES]
        kr = jnp.where(lane < 64, kr * cs + _rot(kr) * sn, 0.0).astype(BF16)
        for h in range(HEADS):
            qr = q[:, 256 * h + 128:256 * h + 256]
            q_ref[0, :, 256 * h:256 * h + 128] = q[:, 256 * h:256 * h + 128].astype(BF16)
            q_ref[0, :, 256 * h + 128:256 * h + 256] = (qr * cs + _rot(qr) * sn).astype(BF16)
            k_ref[0, :, 256 * h:256 * h + 128] = kv[:, 128 * h:128 * h + 128].astype(BF16)
            k_ref[0, :, 256 * h + 128:256 * h + 256] = kr
        v_ref[0] = kv[:, 512:1024].astype(BF16)

    tok = lambda c: pl.BlockSpec((1, TT, c), lambda b, i: (b, i, 0))
    full = lambda a: _full(a.shape)
    ins = (x, pos, invf, gpre, wp, gq, wuq, gkv, wukv, mu, w0, w2p, a0, a2p, k_k, k_a, bo)
    in_specs = [tok(D), tok(1)] + [full(a) for a in ins[2:]]
    sd = lambda c, dt: jax.ShapeDtypeStruct((bsz, t, c), dt)
    out_shape = [sd(D, BF16), sd(DP, F32), sd(1024, BF16), sd(1024, BF16), sd(512, BF16)] + [sd(RW, F32)] * 6
    out_specs = [tok(D), tok(DP), tok(1024), tok(1024), tok(512)] + [tok(RW)] * 6
    return pl.pallas_call(
        body, name="pre_fwd", grid=(bsz, nt), out_shape=out_shape, in_specs=in_specs, out_specs=out_specs,
        scratch_shapes=[pltpu.VMEM((8, NRW), F32)],
        compiler_params=_cparams(("arbitrary", "arbitrary")),
    )(*ins)


def _attn_fwd(q, k, v):
    bsz, t, _ = q.shape
    nq = t // TQ

    def body(q_ref, k_ref, v_ref, o_ref, lse_ref):
        i = pl.program_id(2)

        def step(j, carry, diagonal):
            at = pl.ds(pl.multiple_of(j * TQ, TQ), TQ)
            out = []
            for hh in range(2):
                m, l, acc = carry[hh]
                s = _dot_nt(q_ref[0, :, 256 * hh:256 * (hh + 1)], k_ref[0, at, 256 * hh:256 * (hh + 1)]) * SCALE
                if diagonal:
                    s = jnp.where(lax.broadcasted_iota(jnp.int32, (TQ, TQ), 1)
                                  <= lax.broadcasted_iota(jnp.int32, (TQ, TQ), 0), s, -1e30)
                mn = jnp.maximum(m, jnp.max(s, axis=1, keepdims=True))
                p = jnp.exp(s - mn)
                al = jnp.exp(m - mn)
                l = al * l + jnp.sum(p, axis=1, keepdims=True)
                acc = al * acc + _dot(p.astype(BF16), v_ref[0, at, LANES * hh:LANES * (hh + 1)])
                out.append((mn, l, acc))
            return tuple(out)

        start = (jnp.full((TQ, 1), -1e30, F32), jnp.zeros((TQ, 1), F32), jnp.zeros((TQ, LANES), F32))
        before = lax.fori_loop(0, i, lambda j, carry: step(j, carry, False), (start, start))
        for hh, (m, l, acc) in enumerate(step(i, before, True)):
            o_ref[0, :, LANES * hh:LANES * (hh + 1)] = acc / l
            lse_ref[0, hh] = jnp.broadcast_to(m + jnp.log(l), (TQ, LANES))

    return pl.pallas_call(
        body, name="attn_fwd", grid=(bsz, HEADS // 2, nq),
        out_shape=[jax.ShapeDtypeStruct((bsz, t, 512), F32), jax.ShapeDtypeStruct((bsz, HEADS, t, LANES), F32)],
        in_specs=[pl.BlockSpec((1, TQ, 512), lambda b, h, i: (b, i, h)),
                  pl.BlockSpec((1, t, 512), lambda b, h, i: (b, 0, h)),
                  pl.BlockSpec((1, t, 256), lambda b, h, i: (b, 0, h))],
        out_specs=[pl.BlockSpec((1, TQ, 256), lambda b, h, i: (b, i, h)),
                   pl.BlockSpec((1, 2, TQ, LANES), lambda b, h, i: (b, h, i, 0))],
        compiler_params=_cparams(("parallel", "parallel", "arbitrary")),
    )(q, k, v)


def _attn_bwd(q, k, v, o, lse, do):
    bsz, t, _ = q.shape
    nq = t // TQ

    def body(q_ref, k_ref, v_ref, o_ref, lse_ref, do_ref, dq_ref, dk_ref, dv_ref, dl_ref):
        j = pl.program_id(2)

        @pl.when(j == 0)
        def _():
            def prep(i, _):
                at = pl.ds(pl.multiple_of(i * TQ, TQ), TQ)
                for hh in range(2):
                    lanes = slice(LANES * hh, LANES * (hh + 1))
                    dl_ref[hh, at, :] = jnp.broadcast_to(
                        jnp.sum(do_ref[0, at, lanes] * o_ref[0, at, lanes], axis=1, keepdims=True), (TQ, LANES))
                return 0

            lax.fori_loop(0, nq, prep, 0)
            dq_ref[0] = jnp.zeros((t, 512), F32)

        def q_tile(i, carry, diagonal):
            atq = pl.ds(pl.multiple_of(i * TQ, TQ), TQ)
            out = []
            for hh in range(2):
                dk, dv = carry[hh]
                wide, narrow = slice(256 * hh, 256 * (hh + 1)), slice(LANES * hh, LANES * (hh + 1))
                qt, kt, vt = q_ref[0, atq, wide], k_ref[0, :, wide], v_ref[0, :, narrow]
                dob = do_ref[0, atq, narrow].astype(BF16)
                s = _dot_nt(qt, kt) * SCALE
                if diagonal:
                    s = jnp.where(lax.broadcasted_iota(jnp.int32, (TQ, TQ), 1)
                                  <= lax.broadcasted_iota(jnp.int32, (TQ, TQ), 0), s, -1e30)
                p = jnp.exp(s - lse_ref[0, hh, atq, :][:, 0:1])
                dv = dv + _dot_tn(p.astype(BF16), dob)
                dp = _dot_nt(dob, vt)
                ds = (p * (dp - dl_ref[hh, atq, :][:, 0:1]) * SCALE).astype(BF16)
                dk = dk + _dot_tn(ds, qt)
                dq_ref[0, atq, wide] += _dot(ds, kt)
                out.append((dk, dv))
            return tuple(out)

        zero = (jnp.zeros((TQ, 256), F32), jnp.zeros((TQ, LANES), F32))
        first = q_tile(j, (zero, zero), True)
        done = lax.fori_loop(j + 1, nq, lambda i, carry: q_tile(i, carry, False), first)
        for hh, (dk, dv) in enumerate(done):
            dk_ref[0, :, 256 * hh:256 * (hh + 1)] = dk
            dv_ref[0, :, LANES * hh:LANES * (hh + 1)] = dv

    whole = lambda c: pl.BlockSpec((1, t, c), lambda b, h, j: (b, 0, h))
    tile = lambda c: pl.BlockSpec((1, TQ, c), lambda b, h, j: (b, j, h))
    return pl.pallas_call(
        body, name="attn_bwd", grid=(bsz, HEADS // 2, nq),
        out_shape=[jax.ShapeDtypeStruct((bsz, t, 1024), F32), jax.ShapeDtypeStruct((bsz, t, 1024), F32),
                   jax.ShapeDtypeStruct((bsz, t, 512), F32)],
        in_specs=[whole(512), tile(512), tile(256), whole(256),
                  pl.BlockSpec((1, 2, t, LANES), lambda b, h, j: (b, h, 0, 0)), whole(256)],
        out_specs=[whole(512), tile(512), tile(256)],
        scratch_shapes=[pltpu.VMEM((2, t, LANES), F32)],
        compiler_params=_cparams(("parallel", "parallel", "arbitrary")),
    )(q, k, v, o, lse, do)


RW_HEADS = 8
CH = 32


def _lane_split(bsz):
    vs = LANES // (bsz * RW_HEADS)
    return vs, 64 // vs


def _gather_matrix(bsz):
    group = bsz * RW_HEADS
    vs = LANES // group
    half = (RW_HEADS // 2) * bsz * SPREAD_STEPS
    p = np.zeros((SPREAD_STEPS // vs * LANES, 2 * half), np.float32)
    for g2 in range(SPREAD_STEPS // vs):
        for j in range(vs):
            for b in range(bsz):
                for h in range(RW_HEADS):
                    hp, hpar = h // 2, h % 2
                    p[g2 * LANES + j * group + b * RW_HEADS + h,
                      hpar * half + (hp * bsz + b) * SPREAD_STEPS + g2 * vs + j] = 1.0
    return jnp.asarray(np.concatenate([p] * 3, axis=0), BF16)


def _gather_k(ys, bsz):
    vs = LANES // (bsz * RW_HEADS)
    assert (RW_HEADS // 2) * bsz * SPREAD_STEPS == LANES, "the transposed tile must be 128 lanes wide"
    tg = ys[0].shape[0]
    n = len(ys)
    ngrp = SPREAD_BLOCK // SPREAD_STEPS
    per = SPREAD_STEPS // vs

    def body(*refs):
        pm = refs[n][...]
        for y_ref, o_ref in zip(refs[:n], refs[n + 1:]):
            lhs = jnp.concatenate(
                [jnp.concatenate(_split3(jnp.concatenate([y_ref[per * m + g2] for g2 in range(per)], axis=1)), axis=1)
                 for m in range(ngrp)], axis=0)
            a = _dot(lhs, pm)
            for m in range(ngrp):
                am = a[64 * m:64 * (m + 1)]
                bt = jnp.concatenate([am[:, 0:LANES], am[:, LANES:2 * LANES]], axis=0).T
                for hp in range(RW_HEADS // 2):
                    for b in range(bsz):
                        at = (hp * bsz + b) * SPREAD_STEPS
                        o_ref[b, SPREAD_STEPS * m:SPREAD_STEPS * (m + 1), LANES * hp:LANES * (hp + 1)] = \
                            bt[at:at + SPREAD_STEPS]

    pm = _gather_matrix(bsz)
    return pl.pallas_call(
        body, name="wkv_gather", grid=(tg * vs // SPREAD_BLOCK,),
        out_shape=[jax.ShapeDtypeStruct((bsz, tg * vs, RW), F32)] * n,
        in_specs=[pl.BlockSpec((SPREAD_BLOCK // vs, 64, LANES), lambda i: (i, 0, 0))] * n + [_full(pm.shape)],
        out_specs=[pl.BlockSpec((bsz, SPREAD_BLOCK, RW), lambda i: (0, i, 0))] * n,
        compiler_params=_cparams(("parallel",)),
    )(*ys, pm)


def _to_v(x):
    bsz, t, _ = x.shape
    vs, vq = _lane_split(bsz)
    return jnp.transpose(x.reshape(bsz, t, RW_HEADS, vq, vs), (1, 3, 4, 0, 2)).reshape(t, vq, LANES)


def _from_v(y, bsz):
    t = y.shape[0]
    vs, vq = _lane_split(bsz)
    return jnp.transpose(y.reshape(t, vq, vs, bsz, RW_HEADS), (3, 0, 4, 1, 2)).reshape(bsz, t, RW)


def _ksum(a):
    return jnp.sum(a, axis=0, keepdims=True)


def _fold(a, group):
    sh = LANES // 2
    while sh >= group:
        a = a + pltpu.roll(a, sh, 1)
        sh //= 2
    return a


def _lane_group(shape, group):
    return lax.broadcasted_iota(jnp.int32, shape, 1) // group


SPREAD_STEPS = 8
SPREAD_BLOCK = 64


def _spread_matrix(bsz):
    group = bsz * RW_HEADS
    vs = LANES // group
    rows = (RW_HEADS // 2) * bsz * SPREAD_STEPS
    q = np.zeros((2, rows, SPREAD_STEPS * LANES), np.float32)
    for hpar in range(2):
        for hp in range(RW_HEADS // 2):
            for b in range(bsz):
                for st in range(SPREAD_STEPS):
                    row = (hp * bsz + b) * SPREAD_STEPS + st
                    for s in range(vs):
                        q[hpar, row, st * LANES + s * group + b * RW_HEADS + 2 * hp + hpar] = 1.0
    return jnp.asarray(np.concatenate([q[0], q[1]] * 3, axis=0), BF16)


def _spread_k(xs):
    bsz, t, _ = xs[0].shape
    assert (RW_HEADS // 2) * bsz * SPREAD_STEPS == LANES, "the transposed tile must be 128 lanes wide"
    n = len(xs)
    ngrp = SPREAD_BLOCK // SPREAD_STEPS

    def body(*refs):
        qm = refs[n][...]
        for x_ref, o_ref in zip(refs[:n], refs[n + 1:]):
            cols = [[] for _ in range(6)]
            for m in range(ngrp):
                at = slice(SPREAD_STEPS * m, SPREAD_STEPS * (m + 1))
                x8 = jnp.concatenate([x_ref[b, at, LANES * hp:LANES * (hp + 1)]
                                      for hp in range(RW_HEADS // 2) for b in range(bsz)], axis=0)
                for pi, piece in enumerate(_split3(x8.T)):
                    cols[2 * pi].append(piece[0:64])
                    cols[2 * pi + 1].append(piece[64:128])
            lhs = jnp.concatenate([jnp.concatenate(c, axis=0) for c in cols], axis=1)
            y = _dot(lhs, qm)
            for m in range(ngrp):
                for st in range(SPREAD_STEPS):
                    o_ref[SPREAD_STEPS * m + st] = y[64 * m:64 * (m + 1), LANES * st:LANES * (st + 1)]

    qm = _spread_matrix(bsz)
    return pl.pallas_call(
        body, name="wkv_spread", grid=(t // SPREAD_BLOCK,),
        out_shape=[jax.ShapeDtypeStruct((t, 64, LANES), F32)] * n,
        in_specs=[pl.BlockSpec((bsz, SPREAD_BLOCK, RW), lambda i: (0, i, 0))] * n + [_full(qm.shape)],
        out_specs=[pl.BlockSpec((SPREAD_BLOCK, 64, LANES), lambda i: (i, 0, 0))] * n,
        compiler_params=_cparams(("parallel",)),
    )(*xs, qm)


def _wkv_fwd(r, w, kp, al, be, v):
    t, vq = v.shape[0], v.shape[1]

    def body(r_ref, w_ref, kp_ref, al_ref, be_ref, v_ref, y_ref, a_ref, u_ref, st_ref):
        @pl.when(pl.program_id(0) == 0)
        def _():
            st_ref[...] = jnp.zeros(st_ref.shape, F32)

        def step(tl, _):
            rv, wv, kv, av, bv = r_ref[tl], w_ref[tl], kp_ref[tl], al_ref[tl], be_ref[tl]
            vals = v_ref[tl]
            yrows, urows = [], []
            for q in range(vq):
                s = st_ref[q]
                u = _ksum(s * av)
                s = s * wv + bv * u + kv * vals[q:q + 1]
                st_ref[q] = s
                a_ref[tl, q] = s
                urows.append(u)
                yrows.append(_ksum(s * rv))
            y_ref[tl] = jnp.concatenate(yrows, axis=0)
            u_ref[tl] = jnp.concatenate(urows, axis=0)
            return 0

        lax.fori_loop(0, CH, step, 0)

    kspec = pl.BlockSpec((CH, 64, LANES), lambda i: (i, 0, 0))
    vspec = pl.BlockSpec((CH, vq, LANES), lambda i: (i, 0, 0))
    vsd = jax.ShapeDtypeStruct((t, vq, LANES), F32)
    return pl.pallas_call(
        body, name="wkv_fwd", grid=(t // CH,),
        out_shape=[vsd, jax.ShapeDtypeStruct((t, vq, 64, LANES), F32), vsd],
        in_specs=[kspec] * 5 + [vspec],
        out_specs=[vspec, pl.BlockSpec((CH, vq, 64, LANES), lambda i: (i, 0, 0, 0)), vspec],
        scratch_shapes=[pltpu.VMEM((vq, 64, LANES), F32)],
        compiler_params=_cparams(("arbitrary",)),
    )(r, w, kp, al, be, v)


def _wkv_bwd(r, w, kp, al, be, v, dy, states, u):
    t, vq = v.shape[0], v.shape[1]
    vs = 64 // vq
    group = LANES // vs
    n = t // CH
    ng = CH // vs

    def body(r_ref, w_ref, kp_ref, al_ref, be_ref, v_ref, dy_ref, u_ref, a_ref, ap_ref,
             dr_ref, dw_ref, dkp_ref, dal_ref, dbe_ref, dv_ref, ds_ref):
        @pl.when(pl.program_id(0) == 0)
        def _():
            ds_ref[...] = jnp.zeros(ds_ref.shape, F32)

        earliest = pl.program_id(0) == n - 1

        def reverse(i, _):
            g = ng - 1 - i
            grp = _lane_group((64, LANES), group)
            outs = None
            for j in reversed(range(vs)):
                tl = g * vs + j
                rv, wv, kv, av, bv = r_ref[tl], w_ref[tl], kp_ref[tl], al_ref[tl], be_ref[tl]
                vals, dys, us = v_ref[tl], dy_ref[tl], u_ref[tl]
                acc = None
                dvrows = []
                for q in range(vq):
                    if j > 0:
                        s_prev = a_ref[tl - 1, q]
                    else:
                        before = jnp.where(earliest, 0.0, ap_ref[0, q])
                        s_prev = jnp.where(g == 0, before, a_ref[jnp.maximum(tl - 1, 0), q])
                    dyq = dys[q:q + 1]
                    ds = ds_ref[q] + rv * dyq
                    c = _ksum(ds * bv)
                    dvrows.append(_ksum(ds * kv))
                    terms = (a_ref[tl, q] * dyq, ds * s_prev, ds * vals[q:q + 1], s_prev * c, ds * us[q:q + 1])
                    acc = terms if acc is None else tuple(a + b for a, b in zip(acc, terms))
                    ds_ref[q] = ds * wv + av * c
                dv_ref[tl] = jnp.concatenate(dvrows, axis=0)
                summed = [_fold(a, group) for a in acc]
                outs = summed if outs is None else [jnp.where(grp == j, f, o) for f, o in zip(summed, outs)]
            for ref, o in zip((dr_ref, dw_ref, dkp_ref, dal_ref, dbe_ref), outs):
                ref[g] = o
            return 0

        lax.fori_loop(0, ng, reverse, 0)

    kspec = pl.BlockSpec((CH, 64, LANES), lambda i: (n - 1 - i, 0, 0))
    gspec = pl.BlockSpec((ng, 64, LANES), lambda i: (n - 1 - i, 0, 0))
    vspec = pl.BlockSpec((CH, vq, LANES), lambda i: (n - 1 - i, 0, 0))
    ksd = jax.ShapeDtypeStruct((t // vs, 64, LANES), F32)
    return pl.pallas_call(
        body, name="wkv_bwd", grid=(n,),
        out_shape=[ksd] * 5 + [jax.ShapeDtypeStruct((t, vq, LANES), F32)],
        in_specs=[kspec] * 5 + [vspec, vspec, vspec,
                                pl.BlockSpec((CH, vq, 64, LANES), lambda i: (n - 1 - i, 0, 0, 0)),
                                pl.BlockSpec((1, vq, 64, LANES), lambda i: (jnp.maximum((n - 1 - i) * CH - 1, 0), 0, 0, 0))],
        out_specs=[gspec] * 5 + [vspec],
        scratch_shapes=[pltpu.VMEM((vq, 64, LANES), F32)],
        compiler_params=_cparams(("arbitrary",)),
    )(r, w, kp, al, be, v, dy, u, states, states)


def _post(x, tgt, pp, o, yw, r, kp, v, ln_g, ln_b, r_k, wo, wot, gpost, bo):
    bsz, t, _ = x.shape
    tt = TT_VPU
    nt = t // tt

    def body(x_ref, tgt_ref, z_ref, o_ref, yw_ref, r_ref, kp_ref, v_ref, lng_ref, lnb_ref, rk_ref, wo_ref, wot_ref,
             gpost_ref, bo_ref,
             dh_ref, dz_ref, dym_ref, dyw_ref, dbon_ref, loss_ref, dwo_ref, dgpost_ref, dlng_ref, dlnb_ref, drk_ref):
        first = (pl.program_id(0) == 0) & (pl.program_id(1) == 0)

        @pl.when(first)
        def _():
            for ref in (loss_ref, dwo_ref, dgpost_ref, dlng_ref, dlnb_ref, drk_ref):
                ref[...] = jnp.zeros(ref.shape, F32)

        bo_m = bo_ref[...]
        seg = lambda a: _seg(a, bo_m)
        rowsum = lambda a: jnp.sum(a, axis=0, keepdims=True)
        ywv, rv, kpv, vv = yw_ref[0], r_ref[0], kp_ref[0], v_ref[0]
        ln_g, r_k = lng_ref[...], rk_ref[...]
        mean = seg(ywv) * (1.0 / 64)
        yc = ywv - mean
        rstd = lax.rsqrt(seg(yc * yc) * (1.0 / 64) + GN_EPS)
        yhat = yc * rstd
        sb = seg(rv * kpv * r_k)
        y_rw = yhat * ln_g + lnb_ref[...] + sb * vv
        z = z_ref[0]
        sig = _sigmoid(z)
        sz = z * sig
        ycat = jnp.concatenate([o_ref[0], y_rw], axis=1)
        ycg = (ycat * sz).astype(BF16)
        out = _dot(ycg, wo_ref[...])
        hn, nx, rstd_o = _rms(out, gpost_ref[...], D)
        err = x_ref[0] + hn - tgt_ref[0]
        loss_ref[...] += jnp.sum(err * err) * (0.5 / D)
        dh = err * (1.0 / D)
        dh_ref[0] = dh
        dout, dgp = _rms_bwd(dh, nx, rstd_o, gpost_ref[...], D)
        dgpost_ref[...] += dgp
        doutb = dout.astype(BF16)
        dwo_ref[...] += _dot_tn(ycg, doutb)
        dycg = _dot(doutb, wot_ref[...])
        dz_ref[0] = dycg * ycat * (sig * (1.0 + z * (1.0 - sig)))
        dycat = dycg * sz
        dym_ref[0] = dycat[:, 0:512]
        dy_rw = dycat[:, 512:1024]
        dlnb_ref[...] += rowsum(dy_rw)
        dlng_ref[...] += rowsum(dy_rw * yhat)
        dyhat = dy_rw * ln_g
        dyw_ref[0] = rstd * (dyhat - seg(dyhat) * (1.0 / 64) - yhat * (seg(dyhat * yhat) * (1.0 / 64)))
        dsb = seg(dy_rw * vv)
        drk_ref[...] += rowsum(dsb * rv * kpv)
        dbon_ref[0, :, 0:512] = dsb * kpv * r_k
        dbon_ref[0, :, 512:1024] = dsb * rv * r_k
        dbon_ref[0, :, 1024:1536] = dy_rw * sb

    tok = lambda c: pl.BlockSpec((1, tt, c), lambda b, i: (b, i, 0))
    full = lambda a: _full(a.shape)
    ins = (x, tgt, pp, o, yw, r, kp, v, ln_g, ln_b, r_k, wo, wot, gpost, bo)
    in_specs = [tok(D), tok(D), tok(1024)] + [tok(512)] * 5 + [full(a) for a in ins[8:]]
    sd = lambda c: jax.ShapeDtypeStruct((bsz, t, c), F32)
    vec = lambda c: jax.ShapeDtypeStruct((1, c), F32)
    out_shape = [sd(D), sd(1024), sd(512), sd(512), sd(1536), jax.ShapeDtypeStruct((8, LANES), F32),
                 jax.ShapeDtypeStruct((1024, 1024), F32), vec(D), vec(512), vec(512), vec(512)]
    out_specs = [tok(D), tok(1024), tok(512), tok(512), tok(1536), _resident((8, LANES)), _resident((1024, 1024)),
                 _resident((1, D)), _resident((1, 512)), _resident((1, 512)), _resident((1, 512))]
    return pl.pallas_call(
        body, name="post", grid=(bsz, nt), out_shape=out_shape, in_specs=in_specs, out_specs=out_specs,
        compiler_params=_cparams(("arbitrary", "arbitrary")),
    )(*ins)


def _pre_bwd_a(pp, pos, invf, cqkv_w, mu, w0, w2p, w2pt, a0, a2p, a2pt, k_k, k_a, bo,
               dq, dk, dva, dwkv, dbon):
    gq, wuqt, gkv, wukvt = cqkv_w
    bsz, t, _ = pp.shape
    tt = TT_VPU
    nt = t // tt
    dr_w, dw_w, dkp_w, dv_w, dal_w, dbe_w = dwkv

    def body(pp_ref, pos_ref, invf_ref, gq_ref, wuqt_ref, gkv_ref, wukvt_ref, mu_ref, w0_ref, w2p_ref, w2pt_ref,
             a0_ref, a2p_ref, a2pt_ref, kk_ref, ka_ref, bo_ref, dq_ref, dk_ref, dva_ref,
             dr_ref, dw_ref, dkp_ref, dv_ref, dal_ref, dbe_ref, dbon_ref,
             da_ref, dwuq_ref, dwukv_ref, dw2p_ref, da2p_ref, dgq_ref, dgkv_ref, dmu_ref, dw0_ref, da0_ref,
             dkk_ref, dka_ref, carry):
        i = pl.program_id(1)
        first = (pl.program_id(0) == 0) & (i == 0)

        @pl.when(first)
        def _():
            for ref in (dwuq_ref, dwukv_ref, dw2p_ref, da2p_ref, dgq_ref, dgkv_ref, dmu_ref, dw0_ref, da0_ref,
                        dkk_ref, dka_ref):
                ref[...] = jnp.zeros(ref.shape, F32)

        bo_m = bo_ref[...]
        rowsum = lambda a: jnp.sum(a, axis=0, keepdims=True)
        prw = pp_ref[0, :, RW0:DP]

        @pl.when(i == 0)
        def _():
            carry[...] = jnp.zeros(carry.shape, F32)

        ps, sh = _shift_mix(prw, carry[7:8, :], mu_ref[...])
        carry[...] = prw[tt - 8:tt, :]
        k_k, k_a = kk_ref[...], ka_ref[...]
        g = _rw_gates(ps, w0_ref[...], w2p_ref[...], a0_ref[...], a2p_ref[...], k_k, k_a, bo_m)
        a, kk, k = g["a"], g["kk"], g["k"]
        dr = dr_ref[0] + dbon_ref[0, :, 0:512]
        dkp = dkp_ref[0] + dbon_ref[0, :, 512:1024]
        dv = dv_ref[0] + dbon_ref[0, :, 1024:1536]
        dbe = dbe_ref[0]
        dkk = dbe * a - dal_ref[0]
        da = dbe * kk + dkp * k * k_a
        dka_ref[...] += rowsum(dkp * k * (a - 1.0))
        dm = (dkk - kk * _seg(dkk * kk, bo_m)) / g["nrm"]
        dkk_ref[...] += rowsum(dm * k)
        dk_tot = dkp * (1.0 + (a - 1.0) * k_a) + dm * k_k
        dapre = da * a * (1.0 - a)
        da0_ref[...] += rowsum(dapre)
        dapb = dapre.astype(BF16)
        da2p_ref[...] += _dot_tn(g["misc"].astype(BF16), dapb)
        dwpre = dw_ref[0] * g["w"] * (-g["e"]) * _sigmoid(-g["wpre"])
        dw0_ref[...] += rowsum(dwpre)
        dwpb = dwpre.astype(BF16)
        th = g["th"]
        dw2p_ref[...] += _dot_tn(th.astype(BF16), dwpb)
        dmisc = _dot(dapb, a2pt_ref[...]) + _dot(dwpb, w2pt_ref[...]) * (1.0 - th * th)
        ang = pos_ref[0] * invf_ref[...]
        cs, sn = jnp.cos(ang), jnp.sin(ang)
        unrope = lambda gr: gr * cs - _rot(gr * sn)
        lane = lax.broadcasted_iota(jnp.int32, cs.shape, 1)
        dkr = dk_ref[0, :, 128:256]
        for h in range(1, HEADS):
            dkr = dkr + dk_ref[0, :, 256 * h + 128:256 * h + 256]
        dkr = jnp.where(lane < 64, unrope(dkr), 0.0)
        dmisc = dmisc + jnp.concatenate([dkr, jnp.zeros_like(dkr)], axis=1)
        dqp = jnp.concatenate(
            [blk for h in range(HEADS)
             for blk in (dq_ref[0, :, 256 * h:256 * h + 128], unrope(dq_ref[0, :, 256 * h + 128:256 * h + 256]))],
            axis=1).astype(BF16)
        dkvp = jnp.concatenate([dk_ref[0, :, 256 * h:256 * h + 128] for h in range(HEADS)] + [dva_ref[0]],
                               axis=1).astype(BF16)
        cqn, cq_nx, cq_rstd = _rms(pp_ref[0, :, CQ0:CQ0 + 256], gq_ref[...], 256)
        ckvn, ckv_nx, ckv_rstd = _rms(pp_ref[0, :, CKV0:CKV0 + 128], gkv_ref[...], 128)
        dwuq_ref[...] += _dot_tn(cqn.astype(BF16), dqp)
        dwukv_ref[...] += _dot_tn(ckvn.astype(BF16), dkvp)
        dcq, dgq = _rms_bwd(_dot(dqp, wuqt_ref[...]), cq_nx, cq_rstd, gq_ref[...], 256)
        dckv, dgkv = _rms_bwd(_dot(dkvp, wukvt_ref[...]), ckv_nx, ckv_rstd, gkv_ref[...], 128)
        dgq_ref[...] += dgq
        dgkv_ref[...] += dgkv
        dps = jnp.concatenate([dr, dk_tot, dv, dmisc], axis=1)
        dmu_ref[...] += rowsum(dps * (sh - prw))
        da_ref[0, :, 0:256] = dcq
        da_ref[0, :, 256:384] = dckv
        da_ref[0, :, 384:384 + NRW] = dps

    tok = lambda c: pl.BlockSpec((1, tt, c), lambda b, i: (b, i, 0))
    full = lambda a: _full(a.shape)
    ins = (pp, pos, invf, gq, wuqt, gkv, wukvt, mu, w0, w2p, w2pt, a0, a2p, a2pt, k_k, k_a, bo,
           dq, dk, dva, dr_w, dw_w, dkp_w, dv_w, dal_w, dbe_w, dbon)
    in_specs = ([tok(DP), tok(1)] + [full(a) for a in ins[2:17]] + [tok(1024), tok(1024), tok(512)]
                + [tok(512)] * 6 + [tok(1536)])
    shp = lambda *s: jax.ShapeDtypeStruct(s, F32)
    out_shape = [shp(bsz, t, 384 + NRW), shp(256, 1024), shp(128, 1024), shp(256, 512), shp(256, 512),
                 shp(1, 256), shp(1, 128), shp(1, NRW), shp(1, 512), shp(1, 512), shp(1, 512), shp(1, 512)]
    out_specs = [tok(384 + NRW)] + [_resident(s.shape) for s in out_shape[1:]]
    return pl.pallas_call(
        body, name="pre_bwd_a", grid=(bsz, nt), out_shape=out_shape, in_specs=in_specs, out_specs=out_specs,
        scratch_shapes=[pltpu.VMEM((8, NRW), F32)],
        compiler_params=_cparams(("arbitrary", "arbitrary")),
    )(*ins)


def _pre_bwd_b(x, dh, dz, da, mu, wpt, gpre):
    bsz, t, _ = x.shape
    nt = t // TT
    nblk = t // 8

    def body(x_ref, dh_ref, dz_ref, da_ref, nxt_ref, mu_ref, wpt_ref, gpre_ref, gx_ref, dp_ref, dgpre_ref):
        i = pl.program_id(1)
        first = (pl.program_id(0) == 0) & (i == 0)

        @pl.when(first)
        def _():
            dgpre_ref[...] = jnp.zeros(dgpre_ref.shape, F32)

        mu_v = mu_ref[...]
        dps = da_ref[0, :, 384:384 + NRW]
        nxt = jnp.where(i < nt - 1, nxt_ref[0, 0:1, 384:384 + NRW], 0.0)
        row = lax.broadcasted_iota(jnp.int32, dps.shape, 0)
        up = jnp.where(row == TT - 1, nxt, pltpu.roll(dps, TT - 1, 0))
        dprw = dps * (1.0 - mu_v) + up * mu_v
        dp = jnp.concatenate([dz_ref[0], da_ref[0, :, 0:384], dprw], axis=1).astype(BF16)
        dp_ref[0] = dp
        du = _dot(dp, wpt_ref[...])
        _, nx, rstd = _rms(x_ref[0], gpre_ref[...], D)
        dx, dg = _rms_bwd(du, nx, rstd, gpre_ref[...], D)
        dgpre_ref[...] += dg
        gx_ref[0] = dh_ref[0] + dx

    tok = lambda c: pl.BlockSpec((1, TT, c), lambda b, i: (b, i, 0))
    nxt_spec = pl.BlockSpec((1, 8, 384 + NRW), lambda b, i: (b, jnp.minimum((i + 1) * (TT // 8), nblk - 1), 0))
    ins = (x, dh, dz, da, da, mu, wpt, gpre)
    return pl.pallas_call(
        body, name="pre_bwd_b", grid=(bsz, nt),
        out_shape=[jax.ShapeDtypeStruct((bsz, t, D), F32), jax.ShapeDtypeStruct((bsz, t, DP), BF16),
                   jax.ShapeDtypeStruct((1, D), F32)],
        in_specs=[tok(D), tok(D), tok(1024), tok(384 + NRW), nxt_spec, _full(mu.shape), _full(wpt.shape),
                  _full(gpre.shape)],
        out_specs=[tok(D), tok(DP), _resident((1, D))],
        compiler_params=_cparams(("arbitrary", "arbitrary")),
    )(*ins)


def _tn_matmul(a, b, bn, name, bk=512):
    kdim, m = a.shape
    _, n = b.shape
    nk = kdim // bk

    def body(a_ref, b_ref, o_ref):
        @pl.when(pl.program_id(1) == 0)
        def _():
            o_ref[...] = jnp.zeros(o_ref.shape, F32)

        o_ref[...] += _dot_tn(a_ref[...], b_ref[...])

    return pl.pallas_call(
        body, name=name, grid=(n // bn, nk),
        out_shape=jax.ShapeDtypeStruct((m, n), F32),
        in_specs=[pl.BlockSpec((bk, m), lambda j, kk: (kk, 0)), pl.BlockSpec((bk, bn), lambda j, kk: (kk, j))],
        out_specs=pl.BlockSpec((m, bn), lambda j, kk: (0, j)),
        compiler_params=_cparams(("parallel", "arbitrary")),
    )(a, b)


SHARDED = ("w_in", "mla_w_uq", "mla_w_ukv", "rw_w2", "rw_a2", "w_out")
SMALL = ("norm_pre_g", "mla_q_norm_g", "mla_kv_norm_g", "rw_mu", "rw_w0", "rw_a0", "rw_k_k", "rw_k_a", "rw_r_k",
         "rw_ln_g", "rw_ln_b", "norm_post_g")
WEIGHTS = ("norm_pre_g", "w_in", "mla_q_norm_g", "mla_w_uq", "mla_kv_norm_g", "mla_w_ukv", "rw_mu", "rw_w0", "rw_w2",
           "rw_a0", "rw_a2", "rw_k_k", "rw_k_a", "rw_r_k", "rw_ln_g", "rw_ln_b", "w_out", "norm_post_g")


def _pack_small(d):
    flat = jnp.concatenate([d[n].reshape(1, -1) for n in SMALL], axis=1)
    return jnp.pad(flat, ((0, 0), (0, SMALL_ROWS * LANES - flat.shape[1]))).reshape(SMALL_ROWS, LANES)


def _unpack_small(packed, like):
    flat = packed.reshape(1, -1)
    out, at = {}, 0
    for n in SMALL:
        size = int(np.prod(like[n].shape))
        out[n] = flat[:, at:at + size].reshape(like[n].shape)
        at += size
    return out


def _unpack_shard(packed, like):
    out, at = {}, 0
    for n, rows in zip(SHARDED, PACK_ROWS):
        out[n] = packed[at:at + rows].reshape(like[n].shape)
        at += rows
    return out


def _constants():
    bo = np.kron(np.eye(2, dtype=np.float32), np.ones((64, 64), np.float32))
    inv = ROPE_THETA ** (-np.arange(0, 64, 2, dtype=np.float32) / 64)
    invf = np.concatenate([inv, inv, np.zeros(64, np.float32)]).astype(np.float32)[None, :]
    return jnp.asarray(bo, BF16), jnp.asarray(invf)


def kernel(x, positions, norm_pre_g, w_in, mla_q_norm_g, mla_w_uq, mla_kv_norm_g, mla_w_ukv, rw_mu, rw_w0, rw_w2, rw_a0, rw_a2, rw_k_k, rw_k_a, rw_r_k, rw_ln_g, rw_ln_b, w_out, norm_post_g, loss_target, m_norm_pre_g, m_w_in, m_mla_q_norm_g, m_mla_w_uq, m_mla_kv_norm_g, m_mla_w_ukv, m_rw_mu, m_rw_w0, m_rw_w2, m_rw_a0, m_rw_a2, m_rw_k_k, m_rw_k_a, m_rw_r_k, m_rw_ln_g, m_rw_ln_b, m_w_out, m_norm_post_g, v_norm_pre_g, v_w_in, v_mla_q_norm_g, v_mla_w_uq, v_mla_kv_norm_g, v_mla_w_ukv, v_rw_mu, v_rw_w0, v_rw_w2, v_rw_a0, v_rw_a2, v_rw_k_k, v_rw_k_a, v_rw_r_k, v_rw_ln_g, v_rw_ln_b, v_w_out, v_norm_post_g):
    wts = dict(norm_pre_g=norm_pre_g, w_in=w_in, mla_q_norm_g=mla_q_norm_g, mla_w_uq=mla_w_uq,
               mla_kv_norm_g=mla_kv_norm_g, mla_w_ukv=mla_w_ukv, rw_mu=rw_mu, rw_w0=rw_w0, rw_w2=rw_w2, rw_a0=rw_a0,
               rw_a2=rw_a2, rw_k_k=rw_k_k, rw_k_a=rw_k_a, rw_r_k=rw_r_k, rw_ln_g=rw_ln_g, rw_ln_b=rw_ln_b, w_out=w_out,
               norm_post_g=norm_post_g)
    mom_m = dict(norm_pre_g=m_norm_pre_g, w_in=m_w_in, mla_q_norm_g=m_mla_q_norm_g, mla_w_uq=m_mla_w_uq,
                 mla_kv_norm_g=m_mla_kv_norm_g, mla_w_ukv=m_mla_w_ukv, rw_mu=m_rw_mu, rw_w0=m_rw_w0, rw_w2=m_rw_w2,
                 rw_a0=m_rw_a0, rw_a2=m_rw_a2, rw_k_k=m_rw_k_k, rw_k_a=m_rw_k_a, rw_r_k=m_rw_r_k, rw_ln_g=m_rw_ln_g,
                 rw_ln_b=m_rw_ln_b, w_out=m_w_out, norm_post_g=m_norm_post_g)
    mom_v = dict(norm_pre_g=v_norm_pre_g, w_in=v_w_in, mla_q_norm_g=v_mla_q_norm_g, mla_w_uq=v_mla_w_uq,
                 mla_kv_norm_g=v_mla_kv_norm_g, mla_w_ukv=v_mla_w_ukv, rw_mu=v_rw_mu, rw_w0=v_rw_w0, rw_w2=v_rw_w2,
                 rw_a0=v_rw_a0, rw_a2=v_rw_a2, rw_k_k=v_rw_k_k, rw_k_a=v_rw_k_a, rw_r_k=v_rw_r_k, rw_ln_g=v_rw_ln_g,
                 rw_ln_b=v_rw_ln_b, w_out=v_w_out, norm_post_g=v_norm_post_g)
    bsz, t, _ = x.shape
    bo, invf = _constants()

    g_in, g_uq, g_ukv, g_w2, g_a2, g_out = _ag_weights([wts[n][0] for n in SHARDED])
    w_in_f = jnp.transpose(g_in, (1, 0, 2)).reshape(D, D_IN)
    wp = jnp.concatenate([w_in_f[:, 2112:3136], w_in_f[:, 0:384], w_in_f[:, 448:1984], w_in_f[:, 384:448],
                          w_in_f[:, 1984:2112], jnp.zeros((D, 64), BF16)], axis=1)
    wuq = jnp.pad(jnp.transpose(g_uq, (1, 0, 2)).reshape(256, HEADS, 192), ((0, 0), (0, 0), (0, 64))).reshape(256, 1024)
    wukv = jnp.transpose(jnp.transpose(g_ukv, (1, 0, 2)).reshape(128, HEADS, 2, 128), (0, 2, 1, 3)).reshape(128, 1024)
    w2 = jnp.transpose(g_w2, (1, 0, 2)).reshape(64, RW)
    a2 = jnp.transpose(g_a2, (1, 0, 2)).reshape(64, RW)
    w2p = jnp.pad(w2, ((64, 128), (0, 0)))
    a2p = jnp.pad(a2, ((128, 64), (0, 0)))
    wo = g_out.reshape(D, D)
    mu = jnp.concatenate([rw_mu[:, 0:1536], jnp.zeros((1, 64), F32), rw_mu[:, 1536:1664], jnp.zeros((1, 64), F32)],
                         axis=1)
    r_k = rw_r_k.reshape(1, RW)
    pos = positions.astype(F32)[:, :, None]

    (u, pp, q_att, k_att, v_att, r, w, kp, v, al, be) = _pre_fwd(
        x, pos, invf, norm_pre_g, wp, mla_q_norm_g, wuq, mla_kv_norm_g, wukv, mu, rw_w0, w2p, rw_a0, a2p, rw_k_k,
        rw_k_a, bo)
    o, lse = _attn_fwd(q_att, k_att, v_att)
    rw_k = _spread_k([r, w, kp, al, be])
    v_v = _to_v(v)
    yw_v, states, u_v = _wkv_fwd(*rw_k, v_v)
    yw = _from_v(yw_v, bsz)

    (dh, dz, dym, dyw, dbon, loss_acc, d_wo, d_gpost, d_lng, d_lnb, d_rk) = _post(
        x, loss_target, pp, o, yw, r, kp, v, rw_ln_g, rw_ln_b, r_k, wo, wo.T, norm_post_g, bo)

    d_k = _wkv_bwd(*rw_k, v_v, _to_v(dyw), states, u_v)
    dr_w, dw_w, dkp_w, dal_w, dbe_w = _gather_k(d_k[:5], bsz)
    dwkv = (dr_w, dw_w, dkp_w, _from_v(d_k[5], bsz), dal_w, dbe_w)
    dq, dk, dva = _attn_bwd(q_att, k_att, v_att, o, lse, dym)

    (da, d_wuq, d_wukv, d_w2p, d_a2p, d_gq, d_gkv, d_mu, d_w0, d_a0, d_kk, d_ka) = _pre_bwd_a(
        pp, pos, invf, (mla_q_norm_g, wuq.T, mla_kv_norm_g, wukv.T), mu, rw_w0, w2p, w2p.T, rw_a0, a2p, a2p.T,
        rw_k_k, rw_k_a, bo, dq, dk, dva, dwkv, dbon)
    grad_x, dpb, d_gpre = _pre_bwd_b(x, dh, dz, da, mu, wp.T, norm_pre_g)
    d_wp = _tn_matmul(u.reshape(bsz * t, D), dpb.reshape(bsz * t, DP), DP, "dw_in")

    full_g = {
        "w_in": jnp.concatenate([d_wp[:, 1024:1408], d_wp[:, 2944:3008], d_wp[:, 1408:2944], d_wp[:, 3008:3136],
                                 d_wp[:, 0:1024]], axis=1),
        "mla_w_uq": d_wuq.reshape(256, HEADS, 256)[:, :, :192].reshape(256, 768),
        "mla_w_ukv": jnp.transpose(d_wukv.reshape(128, 2, HEADS, 128), (0, 2, 1, 3)).reshape(128, 1024),
        "rw_w2": d_w2p[64:128],
        "rw_a2": d_a2p[128:192],
        "w_out": d_wo,
    }
    small_g = {
        "norm_pre_g": d_gpre, "mla_q_norm_g": d_gq, "mla_kv_norm_g": d_gkv,
        "rw_mu": jnp.concatenate([d_mu[:, 0:1536], d_mu[:, 1600:1728]], axis=1),
        "rw_w0": d_w0, "rw_a0": d_a0, "rw_k_k": d_kk, "rw_k_a": d_ka, "rw_r_k": d_rk, "rw_ln_g": d_lng,
        "rw_ln_b": d_lnb, "norm_post_g": d_gpost,
    }

    def by_shard(name, g):
        if name == "w_out":
            return g.reshape(N_SHARD, -1, LANES)
        rows, cols = g.shape
        return jnp.transpose(g.reshape(rows, N_SHARD, cols // N_SHARD), (1, 0, 2)).reshape(N_SHARD, -1, LANES)

    packed = jnp.concatenate([by_shard(n, full_g[n]) for n in SHARDED], axis=1)
    pair_sum, pair_sum_b = _rs_pairs(packed.reshape(N_SHARD, 2, HALF, LANES))
    g_shard = _rs_chips(pair_sum, pair_sum_b).reshape(PACK_TOTAL, LANES)

    g_small = _small_allreduce(jnp.concatenate([_pack_small(small_g)[:SMALL_USED], loss_acc[0:SMALL_ROWS - SMALL_USED]]))
    loss = g_small[SMALL_USED, 0]

    g_sharded = _unpack_shard(g_shard, {n: wts[n][0] for n in SHARDED})
    sh = _adamw([wts[n][0] for n in SHARDED], [g_sharded[n] for n in SHARDED], [mom_m[n][0] for n in SHARDED],
                [mom_v[n][0] for n in SHARDED], "adamw_sharded")
    sm = _adamw([_pack_small(wts)], [g_small], [_pack_small(mom_m)], [_pack_small(mom_v)], "adamw_small")

    def outputs(sharded, small):
        out = {n: a[None] for n, a in zip(SHARDED, sharded)}
        out.update(_unpack_small(small, wts))
        return out

    grads = outputs([g_sharded[n] for n in SHARDED], g_small)
    deltas, new_m, new_v = (outputs(sh[k], sm[k][0]) for k in range(3))
    return (loss, grad_x, *[grads[n] for n in WEIGHTS], *[deltas[n] for n in WEIGHTS],
            *[new_m[n] for n in WEIGHTS], *[new_v[n] for n in WEIGHTS])
```

```python
import functools

import numpy as np
import jax
import jax.numpy as jnp
from jax import lax
from jax.experimental import pallas as pl
from jax.experimental.pallas import tpu as pltpu

F32, BF16 = jnp.float32, jnp.bfloat16
MESH = pl.DeviceIdType.MESH

D = 1024
HEADS = 4
RW = 512
NORM_EPS = 1e-6
GN_EPS = 64e-5
ROPE_THETA = 10000.0
SCALE = (128 + 64) ** -0.5
D_IN = 3136
LR, B1, B2, ADAM_EPS, WD, STEP = 0.001, 0.9, 0.999, 1e-08, 0.01, 10

Z0, CQ0, CKV0, RW0, DP = 0, 1024, 1280, 1408, 3200
NRW = DP - RW0

LANES = 128
SUBLANES = 8
VMEM_LIMIT = 56 * 1024 * 1024

TT = 512
TT_VPU = 256
TQ = 512

N_SHARD = 4
PACK_ROWS = (1024 * 784 // 128, 256 * 192 // 128, 128 * 256 // 128, 64, 64, 256 * 1024 // 128)
PACK_TOTAL = sum(PACK_ROWS)
HALF = PACK_TOTAL // 2
SMALL_ROWS = 64
SMALL_USED = 60


def _cparams(sem=None):
    return pltpu.CompilerParams(dimension_semantics=sem, vmem_limit_bytes=VMEM_LIMIT)


def _full(shape):
    n = len(shape)
    return pl.BlockSpec(shape, lambda *_: (0,) * n, pipeline_mode=pl.Buffered(1))


def _resident(shape):
    n = len(shape)
    return pl.BlockSpec(shape, lambda *_: (0,) * n)


def _dot(a, b):
    return jnp.dot(a, b, preferred_element_type=F32)


def _dot_nt(a, b):
    return lax.dot_general(a, b, (((1,), (1,)), ((), ())), preferred_element_type=F32)


def _dot_tn(a, b):
    return lax.dot_general(a, b, (((0,), (0,)), ((), ())), preferred_element_type=F32)


def _split3(x):
    hi = x.astype(BF16)
    r1 = x - hi.astype(F32)
    mid = r1.astype(BF16)
    lo = (r1 - mid.astype(F32)).astype(BF16)
    return hi, mid, lo


def _seg(x, bo):
    rows, nblk = x.shape[0], x.shape[1] // LANES
    pieces = [p for i in range(nblk) for p in _split3(x[:, LANES * i:LANES * (i + 1)])]
    res = _dot(jnp.concatenate(pieces, axis=0), bo)
    parts = [res[(3 * i) * rows:(3 * i + 1) * rows] + res[(3 * i + 1) * rows:(3 * i + 2) * rows]
             + res[(3 * i + 2) * rows:(3 * i + 3) * rows] for i in range(nblk)]
    return parts[0] if nblk == 1 else jnp.concatenate(parts, axis=1)


def _rms(x, g, n):
    rstd = lax.rsqrt(jnp.sum(x * x, axis=-1, keepdims=True) * (1.0 / n) + NORM_EPS)
    nx = x * rstd
    return nx * g, nx, rstd


def _rms_bwd(dy, nx, rstd, g, n):
    dn = dy * g
    dx = rstd * (dn - nx * (jnp.sum(dn * nx, axis=-1, keepdims=True) * (1.0 / n)))
    return dx, jnp.sum(dy * nx, axis=0, keepdims=True)


def _rot(x):
    lane = lax.broadcasted_iota(jnp.int32, x.shape, 1)
    return jnp.where((lane % 64) < 32, -pltpu.roll(x, x.shape[1] - 32, 1), pltpu.roll(x, 32, 1))


def _sigmoid(x):
    return 1.0 / (1.0 + jnp.exp(-x))


def _softplus(x):
    return jnp.maximum(x, 0.0) + jnp.log(1.0 + jnp.exp(-jnp.abs(x)))


def _rw_gates(ps, w0, w2p, a0, a2p, k_k, k_a, bo):
    r, k, v, misc = ps[:, 0:512], ps[:, 512:1024], ps[:, 1024:1536], ps[:, 1536:NRW]
    th = jnp.tanh(misc)
    wpre = w0 + _dot(th.astype(BF16), w2p)
    e = jnp.exp(-_softplus(-wpre) - 0.5)
    w = jnp.exp(-e)
    a = _sigmoid(a0 + _dot(misc.astype(BF16), a2p))
    m = k * k_k
    nrm = jnp.maximum(jnp.sqrt(_seg(m * m, bo)), 1e-12)
    kk = m / nrm
    kp = k * (1.0 + (a - 1.0) * k_a)
    return dict(r=r, k=k, v=v, misc=misc, th=th, wpre=wpre, e=e, w=w, a=a, nrm=nrm, kk=kk, kp=kp)


def _shift_mix(prw, prev_row, mu):
    row = lax.broadcasted_iota(jnp.int32, prw.shape, 0)
    sh = jnp.where(row == 0, prev_row, pltpu.roll(prw, 1, 0))
    return prw + (sh - prw) * mu, sh


def _ag_weights(shards):
    n = len(shards)

    def body(*refs):
        ins, outs = refs[:n], refs[n:2 * n]
        ici_send, ici_recv, d2d_send, d2d_recv = refs[2 * n:2 * n + 4]
        x, y, c = lax.axis_index("x"), lax.axis_index("y"), lax.axis_index("c")
        mine = 2 * x + y
        for w in range(n):
            outs[w][mine] = ins[w][...].astype(BF16)
        flips = ((1, 0), (0, 1), (1, 1))

        def half(w, shard, cc):
            rows = outs[w].shape[1] // 2
            return outs[w].at[shard, pl.ds(pl.multiple_of(cc * rows, 16), rows)]

        def ici(w, k, shard):
            fx, fy = flips[k]
            return pltpu.make_async_remote_copy(
                src_ref=half(w, shard, c), dst_ref=half(w, shard, c),
                send_sem=ici_send.at[w * 3 + k], recv_sem=ici_recv.at[w * 3 + k],
                device_id=(x ^ fx, y ^ fy, c), device_id_type=MESH)

        def d2d(w, k, cc):
            fx, fy = flips[k]
            theirs = 2 * (x ^ fx) + (y ^ fy)
            return pltpu.make_async_remote_copy(
                src_ref=half(w, theirs, cc), dst_ref=half(w, theirs, cc),
                send_sem=d2d_send.at[w * 3 + k], recv_sem=d2d_recv.at[w * 3 + k],
                device_id=(x, y, 1 - c), device_id_type=MESH)

        for w in range(n):
            for k in range(3):
                ici(w, k, mine).start()
        for w in range(n):
            for k in range(3):
                fx, fy = flips[k]
                ici(w, k, 2 * (x ^ fx) + (y ^ fy)).wait_recv()
                d2d(w, k, c).start()
        for w in range(n):
            for k in range(3):
                d2d(w, k, 1 - c).wait_recv()
        for w in range(n):
            for k in range(3):
                ici(w, k, mine).wait_send()
                d2d(w, k, c).wait_send()

    vm = pl.BlockSpec(memory_space=pltpu.VMEM)
    return pl.pallas_call(
        body, name="ag_weights",
        out_shape=[jax.ShapeDtypeStruct((N_SHARD,) + s.shape, BF16) for s in shards],
        in_specs=[vm] * n, out_specs=[vm] * n,
        scratch_shapes=[pltpu.SemaphoreType.DMA((3 * n,))] * 4,
        compiler_params=pltpu.CompilerParams(vmem_limit_bytes=VMEM_LIMIT),
    )(*shards)


def _rs_pairs(halves):
    def body(h_ref, sum_ref, sumb_ref, recv, send_sem, recv_sem):
        x, y, c = lax.axis_index("x"), lax.axis_index("y"), lax.axis_index("c")
        cps = [pltpu.make_async_remote_copy(src_ref=h_ref.at[s, 1 - c], dst_ref=recv.at[s], send_sem=send_sem.at[s],
                                            recv_sem=recv_sem.at[s], device_id=(x, y, 1 - c), device_id_type=MESH)
               for s in range(N_SHARD)]
        for cp in cps:
            cp.start()
        for s, cp in enumerate(cps):
            cp.wait_recv()
            acc = h_ref[s, c] + recv[s]
            sum_ref[s] = acc
            sumb_ref[s] = acc.astype(BF16)
        for cp in cps:
            cp.wait_send()

    vm = pl.BlockSpec(memory_space=pltpu.VMEM)
    shape = (N_SHARD,) + halves.shape[2:]
    return pl.pallas_call(
        body, name="rs_pairs",
        out_shape=[jax.ShapeDtypeStruct(shape, F32), jax.ShapeDtypeStruct(shape, BF16)],
        in_specs=[vm], out_specs=[vm, vm],
        scratch_shapes=[pltpu.VMEM(shape, F32), pltpu.SemaphoreType.DMA((N_SHARD,)),
                        pltpu.SemaphoreType.DMA((N_SHARD,))],
        compiler_params=pltpu.CompilerParams(vmem_limit_bytes=VMEM_LIMIT),
    )(halves)


def _rs_chips(part_f32, part_bf16):
    def body(own_ref, src_ref, out_ref, recv, ici_send, ici_recv, d2d_send, d2d_recv):
        x, y, c = lax.axis_index("x"), lax.axis_index("y"), lax.axis_index("c")
        mine = 2 * x + y
        flips = ((1, 0), (0, 1), (1, 1))
        cps = []
        for k, (fx, fy) in enumerate(flips):
            theirs = 2 * (x ^ fx) + (y ^ fy)
            cps.append(pltpu.make_async_remote_copy(
                src_ref=src_ref.at[theirs], dst_ref=recv.at[k],
                send_sem=ici_send.at[k], recv_sem=ici_recv.at[k],
                device_id=(x ^ fx, y ^ fy, c), device_id_type=MESH))
        for cp in cps:
            cp.start()
        acc = own_ref[mine]
        for k, cp in enumerate(cps):
            cp.wait_recv()
            acc = acc + recv[k].astype(F32)
        out_ref[c] = acc
        to_sibling = pltpu.make_async_remote_copy(
            src_ref=out_ref.at[c], dst_ref=out_ref.at[c], send_sem=d2d_send, recv_sem=d2d_recv,
            device_id=(x, y, 1 - c), device_id_type=MESH)
        to_sibling.start()
        pltpu.make_async_remote_copy(
            src_ref=out_ref.at[1 - c], dst_ref=out_ref.at[1 - c], send_sem=d2d_send, recv_sem=d2d_recv,
            device_id=(x, y, 1 - c), device_id_type=MESH).wait_recv()
        to_sibling.wait_send()
        for cp in cps:
            cp.wait_send()

    vm = pl.BlockSpec(memory_space=pltpu.VMEM)
    return pl.pallas_call(
        body, name="rs_chips",
        out_shape=jax.ShapeDtypeStruct((2,) + part_f32.shape[1:], F32),
        in_specs=[vm, vm], out_specs=vm,
        scratch_shapes=[pltpu.VMEM((3,) + part_bf16.shape[1:], BF16), pltpu.SemaphoreType.DMA((3,)),
                        pltpu.SemaphoreType.DMA((3,)), pltpu.SemaphoreType.DMA, pltpu.SemaphoreType.DMA],
        compiler_params=pltpu.CompilerParams(vmem_limit_bytes=VMEM_LIMIT),
    )(part_f32, part_bf16)


def _small_allreduce(vec):
    def body(in_ref, out_ref, recv, send_sems, recv_sems):
        x, y, c = lax.axis_index("x"), lax.axis_index("y"), lax.axis_index("c")
        me = 4 * x + 2 * y + c
        cps = []
        for k in range(1, 8):
            fx, fy, fc = (k >> 2) & 1, (k >> 1) & 1, k & 1
            cps.append(pltpu.make_async_remote_copy(
                src_ref=in_ref, dst_ref=recv.at[k - 1],
                send_sem=send_sems.at[k - 1], recv_sem=recv_sems.at[k - 1],
                device_id=(x ^ fx, y ^ fy, c ^ fc), device_id_type=MESH))
        for cp in cps:
            cp.start()
        for cp in cps:
            cp.wait()
        acc = jnp.zeros(in_ref.shape, F32)
        for j in range(8):
            slot = jnp.maximum((me ^ j) - 1, 0)
            acc = acc + jnp.where(me == j, in_ref[...], recv[slot])
        out_ref[...] = acc

    vm = pl.BlockSpec(memory_space=pltpu.VMEM)
    return pl.pallas_call(
        body, name="small_allreduce",
        out_shape=jax.ShapeDtypeStruct(vec.shape, F32),
        in_specs=[vm], out_specs=vm,
        scratch_shapes=[pltpu.VMEM((7,) + vec.shape, F32), pltpu.SemaphoreType.DMA((7,)),
                        pltpu.SemaphoreType.DMA((7,))],
    )(vec)


ADAM_ROWS = 64


def _adamw(ws, gs, ms, vs, name):
    n = len(ws)

    def body(*refs):
        for i in range(n):
            w_ref, g_ref, m_ref, v_ref = (refs[k * n + i] for k in range(4))
            d_ref, nm_ref, nv_ref = (refs[(4 + k) * n + i] for k in range(3))
            rows = min(ADAM_ROWS, w_ref.shape[0])

            def chunk(r, _):
                at = pl.ds(pl.multiple_of(r * rows, SUBLANES), rows)
                gg = g_ref[at, :]
                nm = B1 * m_ref[at, :] + (1.0 - B1) * gg
                nv = B2 * v_ref[at, :] + (1.0 - B2) * (gg * gg)
                m_hat = nm / (1.0 - B1 ** STEP)
                v_hat = nv / (1.0 - B2 ** STEP)
                d_ref[at, :] = -LR * (m_hat / (jnp.sqrt(v_hat) + ADAM_EPS) + WD * w_ref[at, :])
                nm_ref[at, :] = nm
                nv_ref[at, :] = nv
                return 0

            lax.fori_loop(0, w_ref.shape[0] // rows, chunk, 0)

    vm = pl.BlockSpec(memory_space=pltpu.VMEM)
    sds = [jax.ShapeDtypeStruct(w.shape, F32) for w in ws]
    outs = pl.pallas_call(
        body, name=name, out_shape=sds * 3, in_specs=[vm] * (4 * n), out_specs=[vm] * (3 * n),
        compiler_params=pltpu.CompilerParams(vmem_limit_bytes=VMEM_LIMIT),
    )(*ws, *gs, *ms, *vs)
    return outs[:n], outs[n:2 * n], outs[2 * n:]


def _pre_fwd(x, pos, invf, gpre, wp, gq, wuq, gkv, wukv, mu, w0, w2p, a0, a2p, k_k, k_a, bo):
    bsz, t, _ = x.shape
    nt = t // TT

    def body(x_ref, pos_ref, invf_ref, gpre_ref, wp_ref, gq_ref, wuq_ref, gkv_ref, wukv_ref, mu_ref, w0_ref,
             w2p_ref, a0_ref, a2p_ref, kk_ref, ka_ref, bo_ref,
             u_ref, pp_ref, q_ref, k_ref, v_ref, r_o, w_o, kp_o, vv_o, al_o, be_o, carry):
        i = pl.program_id(1)
        u, _, _ = _rms(x_ref[0], gpre_ref[...], D)
        ub = u.astype(BF16)
        u_ref[0] = ub
        p = _dot(ub, wp_ref[...])
        pp_ref[0] = p
        prw = p[:, RW0:DP]

        @pl.when(i == 0)
        def _():
            carry[...] = jnp.zeros(carry.shape, F32)

        ps, _ = _shift_mix(prw, carry[7:8, :], mu_ref[...])
        carry[...] = prw[TT - 8:TT, :]

        g = _rw_gates(ps, w0_ref[...], w2p_ref[...], a0_ref[...], a2p_ref[...], kk_ref[...], ka_ref[...],
                      bo_ref[...])
        r_o[0] = g["r"]
        w_o[0] = g["w"]
        kp_o[0] = g["kp"]
        vv_o[0] = g["v"]
        al_o[0] = -g["kk"]
        be_o[0] = g["kk"] * g["a"]

        cqn, _, _ = _rms(p[:, CQ0:CQ0 + 256], gq_ref[...], 256)
        q = _dot(cqn.astype(BF16), wuq_ref[...])
        ckvn, _, _ = _rms(p[:, CKV0:CKV0 + 128], gkv_ref[...], 128)
        kv = _dot(ckvn.astype(BF16), wukv_ref[...])
        ang = pos_ref[0] * invf_ref[...]
        cs, sn = jnp.cos(ang), jnp.sin(ang)
        lane = lax.broadcasted_iota(jnp.int32, cs.shape, 1)
        kr = ps[:, 1536:1536 + LANES]
        kr = jnp.where(lane < 64, kr * cs + _rot(kr) * sn, 0.0).astype(BF16)
        for h in range(HEADS):
            qr = q[:, 256 * h + 128:256 * h + 256]
            q_ref[0, :, 256 * h:256 * h + 128] = q[:, 256 * h:256 * h + 128].astype(BF16)
            q_ref[0, :, 256 * h + 128:256 * h + 256] = (qr * cs + _rot(qr) * sn).astype(BF16)
            k_ref[0, :, 256 * h:256 * h + 128] = kv[:, 128 * h:128 * h + 128].astype(BF16)
            k_ref[0, :, 256 * h + 128:256 * h + 256] = kr
        v_ref[0] = kv[:, 512:1024].astype(BF16)

    tok = lambda c: pl.BlockSpec((1, TT, c), lambda b, i: (b, i, 0))
    full = lambda a: _full(a.shape)
    ins = (x, pos, invf, gpre, wp, gq, wuq, gkv, wukv, mu, w0, w2p, a0, a2p, k_k, k_a, bo)
    in_specs = [tok(D), tok(1)] + [full(a) for a in ins[2:]]
    sd = lambda c, dt: jax.ShapeDtypeStruct((bsz, t, c), dt)
    out_shape = [sd(D, BF16), sd(DP, F32), sd(1024, BF16), sd(1024, BF16), sd(512, BF16)] + [sd(RW, F32)] * 6
    out_specs = [tok(D), tok(DP), tok(1024), tok(1024), tok(512)] + [tok(RW)] * 6
    return pl.pallas_call(
        body, name="pre_fwd", grid=(bsz, nt), out_shape=out_shape, in_specs=in_specs, out_specs=out_specs,
        scratch_shapes=[pltpu.VMEM((8, NRW), F32)],
        compiler_params=_cparams(("arbitrary", "arbitrary")),
    )(*ins)


def _attn_fwd(q, k, v):
    bsz, t, _ = q.shape
    nq = t // TQ

    def body(q_ref, k_ref, v_ref, o_ref, lse_ref):
        i = pl.program_id(2)

        def step(j, carry, diagonal):
            at = pl.ds(pl.multiple_of(j * TQ, TQ), TQ)
            out = []
            for hh in range(2):
                m, l, acc = carry[hh]
                s = _dot_nt(q_ref[0, :, 256 * hh:256 * (hh + 1)], k_ref[0, at, 256 * hh:256 * (hh + 1)]) * SCALE
                if diagonal:
                    s = jnp.where(lax.broadcasted_iota(jnp.int32, (TQ, TQ), 1)
                                  <= lax.broadcasted_iota(jnp.int32, (TQ, TQ), 0), s, -1e30)
                mn = jnp.maximum(m, jnp.max(s, axis=1, keepdims=True))
                p = jnp.exp(s - mn)
                al = jnp.exp(m - mn)
                l = al * l + jnp.sum(p, axis=1, keepdims=True)
                acc = al * acc + _dot(p.astype(BF16), v_ref[0, at, LANES * hh:LANES * (hh + 1)])
                out.append((mn, l, acc))
            return tuple(out)

        start = (jnp.full((TQ, 1), -1e30, F32), jnp.zeros((TQ, 1), F32), jnp.zeros((TQ, LANES), F32))
        before = lax.fori_loop(0, i, lambda j, carry: step(j, carry, False), (start, start))
        for hh, (m, l, acc) in enumerate(step(i, before, True)):
            o_ref[0, :, LANES * hh:LANES * (hh + 1)] = acc / l
            lse_ref[0, hh] = jnp.broadcast_to(m + jnp.log(l), (TQ, LANES))

    return pl.pallas_call(
        body, name="attn_fwd", grid=(bsz, HEADS // 2, nq),
        out_shape=[jax.ShapeDtypeStruct((bsz, t, 512), F32), jax.ShapeDtypeStruct((bsz, HEADS, t, LANES), F32)],
        in_specs=[pl.BlockSpec((1, TQ, 512), lambda b, h, i: (b, i, h)),
                  pl.BlockSpec((1, t, 512), lambda b, h, i: (b, 0, h)),
                  pl.BlockSpec((1, t, 256), lambda b, h, i: (b, 0, h))],
        out_specs=[pl.BlockSpec((1, TQ, 256), lambda b, h, i: (b, i, h)),
                   pl.BlockSpec((1, 2, TQ, LANES), lambda b, h, i: (b, h, i, 0))],
        compiler_params=_cparams(("parallel", "parallel", "arbitrary")),
    )(q, k, v)


def _attn_bwd(q, k, v, o, lse, do):
    bsz, t, _ = q.shape
    nq = t // TQ

    def body(q_ref, k_ref, v_ref, o_ref, lse_ref, do_ref, dq_ref, dk_ref, dv_ref, dl_ref):
        j = pl.program_id(2)

        @pl.when(j == 0)
        def _():
            def prep(i, _):
                at = pl.ds(pl.multiple_of(i * TQ, TQ), TQ)
                for hh in range(2):
                    lanes = slice(LANES * hh, LANES * (hh + 1))
                    dl_ref[hh, at, :] = jnp.broadcast_to(
                        jnp.sum(do_ref[0, at, lanes] * o_ref[0, at, lanes], axis=1, keepdims=True), (TQ, LANES))
                return 0

            lax.fori_loop(0, nq, prep, 0)
            dq_ref[0] = jnp.zeros((t, 512), F32)

        def q_tile(i, carry, diagonal):
            atq = pl.ds(pl.multiple_of(i * TQ, TQ), TQ)
            out = []
            for hh in range(2):
                dk, dv = carry[hh]
                wide, narrow = slice(256 * hh, 256 * (hh + 1)), slice(LANES * hh, LANES * (hh + 1))
                qt, kt, vt = q_ref[0, atq, wide], k_ref[0, :, wide], v_ref[0, :, narrow]
                dob = do_ref[0, atq, narrow].astype(BF16)
                s = _dot_nt(qt, kt) * SCALE
                if diagonal:
                    s = jnp.where(lax.broadcasted_iota(jnp.int32, (TQ, TQ), 1)
                                  <= lax.broadcasted_iota(jnp.int32, (TQ, TQ), 0), s, -1e30)
                p = jnp.exp(s - lse_ref[0, hh, atq, :][:, 0:1])
                dv = dv + _dot_tn(p.astype(BF16), dob)
                dp = _dot_nt(dob, vt)
                ds = (p * (dp - dl_ref[hh, atq, :][:, 0:1]) * SCALE).astype(BF16)
                dk = dk + _dot_tn(ds, qt)
                dq_ref[0, atq, wide] += _dot(ds, kt)
                out.append((dk, dv))
            return tuple(out)

        zero = (jnp.zeros((TQ, 256), F32), jnp.zeros((TQ, LANES), F32))
        first = q_tile(j, (zero, zero), True)
        done = lax.fori_loop(j + 1, nq, lambda i, carry: q_tile(i, carry, False), first)
        for hh, (dk, dv) in enumerate(done):
            dk_ref[0, :, 256 * hh:256 * (hh + 1)] = dk
            dv_ref[0, :, LANES * hh:LANES * (hh + 1)] = dv

    whole = lambda c: pl.BlockSpec((1, t, c), lambda b, h, j: (b, 0, h))
    tile = lambda c: pl.BlockSpec((1, TQ, c), lambda b, h, j: (b, j, h))
    return pl.pallas_call(
        body, name="attn_bwd", grid=(bsz, HEADS // 2, nq),
        out_shape=[jax.ShapeDtypeStruct((bsz, t, 1024), F32), jax.ShapeDtypeStruct((bsz, t, 1024), F32),
                   jax.ShapeDtypeStruct((bsz, t, 512), F32)],
        in_specs=[whole(512), tile(512), tile(256), whole(256),
                  pl.BlockSpec((1, 2, t, LANES), lambda b, h, j: (b, h, 0, 0)), whole(256)],
        out_specs=[whole(512), tile(512), tile(256)],
        scratch_shapes=[pltpu.VMEM((2, t, LANES), F32)],
        compiler_params=_cparams(("parallel", "parallel", "arbitrary")),
    )(q, k, v, o, lse, do)


RW_HEADS = 8
CH = 32


def _lane_split(bsz):
    vs = LANES // (bsz * RW_HEADS)
    return vs, 64 // vs


def _gather_matrix(bsz):
    group = bsz * RW_HEADS
    vs = LANES // group
    half = (RW_HEADS // 2) * bsz * SPREAD_STEPS
    p = np.zeros((SPREAD_STEPS // vs * LANES, 2 * half), np.float32)
    for g2 in range(SPREAD_STEPS // vs):
        for j in range(vs):
            for b in range(bsz):
                for h in range(RW_HEADS):
                    hp, hpar = h // 2, h % 2
                    p[g2 * LANES + j * group + b * RW_HEADS + h,
                      hpar * half + (hp * bsz + b) * SPREAD_STEPS + g2 * vs + j] = 1.0
    return jnp.asarray(np.concatenate([p] * 3, axis=0), BF16)


def _gather_k(ys, bsz):
    vs = LANES // (bsz * RW_HEADS)
    assert (RW_HEADS // 2) * bsz * SPREAD_STEPS == LANES, "the transposed tile must be 128 lanes wide"
    tg = ys[0].shape[0]
    n = len(ys)
    ngrp = GATHER_BLOCK // SPREAD_STEPS
    per = SPREAD_STEPS // vs

    def body(*refs):
        pm = refs[n][...]
        for y_ref, o_ref in zip(refs[:n], refs[n + 1:]):
            lhs = jnp.concatenate(
                [jnp.concatenate(_split3(jnp.concatenate([y_ref[per * m + g2] for g2 in range(per)], axis=1)), axis=1)
                 for m in range(ngrp)], axis=0)
            a = _dot(lhs, pm)
            for m in range(ngrp):
                am = a[64 * m:64 * (m + 1)]
                bt = jnp.concatenate([am[:, 0:LANES], am[:, LANES:2 * LANES]], axis=0).T
                for hp in range(RW_HEADS // 2):
                    for b in range(bsz):
                        at = (hp * bsz + b) * SPREAD_STEPS
                        o_ref[b, SPREAD_STEPS * m:SPREAD_STEPS * (m + 1), LANES * hp:LANES * (hp + 1)] = \
                            bt[at:at + SPREAD_STEPS]

    pm = _gather_matrix(bsz)
    return pl.pallas_call(
        body, name="wkv_gather", grid=(tg * vs // GATHER_BLOCK,),
        out_shape=[jax.ShapeDtypeStruct((bsz, tg * vs, RW), F32)] * n,
        in_specs=[pl.BlockSpec((GATHER_BLOCK // vs, 64, LANES), lambda i: (i, 0, 0))] * n + [_full(pm.shape)],
        out_specs=[pl.BlockSpec((bsz, GATHER_BLOCK, RW), lambda i: (0, i, 0))] * n,
        compiler_params=_cparams(("parallel",)),
    )(*ys, pm)


def _to_v(x):
    bsz, t, _ = x.shape
    vs, vq = _lane_split(bsz)
    return jnp.transpose(x.reshape(bsz, t, RW_HEADS, vq, vs), (1, 3, 4, 0, 2)).reshape(t, vq, LANES)


def _from_v(y, bsz):
    t = y.shape[0]
    vs, vq = _lane_split(bsz)
    return jnp.transpose(y.reshape(t, vq, vs, bsz, RW_HEADS), (3, 0, 4, 1, 2)).reshape(bsz, t, RW)


def _ksum(a):
    return jnp.sum(a, axis=0, keepdims=True)


def _fold(a, group):
    sh = LANES // 2
    while sh >= group:
        a = a + pltpu.roll(a, sh, 1)
        sh //= 2
    return a


def _lane_group(shape, group):
    return lax.broadcasted_iota(jnp.int32, shape, 1) // group


SPREAD_STEPS = 8
SPREAD_BLOCK = 64
GATHER_BLOCK = 128


def _spread_matrix(bsz):
    group = bsz * RW_HEADS
    vs = LANES // group
    rows = (RW_HEADS // 2) * bsz * SPREAD_STEPS
    q = np.zeros((2, rows, SPREAD_STEPS * LANES), np.float32)
    for hpar in range(2):
        for hp in range(RW_HEADS // 2):
            for b in range(bsz):
                for st in range(SPREAD_STEPS):
                    row = (hp * bsz + b) * SPREAD_STEPS + st
                    for s in range(vs):
                        q[hpar, row, st * LANES + s * group + b * RW_HEADS + 2 * hp + hpar] = 1.0
    return jnp.asarray(np.concatenate([q[0], q[1]] * 3, axis=0), BF16)


def _spread_k(xs):
    bsz, t, _ = xs[0].shape
    assert (RW_HEADS // 2) * bsz * SPREAD_STEPS == LANES, "the transposed tile must be 128 lanes wide"
    n = len(xs)
    ngrp = SPREAD_BLOCK // SPREAD_STEPS

    def body(*refs):
        qm = refs[n][...]
        for x_ref, o_ref in zip(refs[:n], refs[n + 1:]):
            cols = [[] for _ in range(6)]
            for m in range(ngrp):
                at = slice(SPREAD_STEPS * m, SPREAD_STEPS * (m + 1))
                x8 = jnp.concatenate([x_ref[b, at, LANES * hp:LANES * (hp + 1)]
                                      for hp in range(RW_HEADS // 2) for b in range(bsz)], axis=0)
                for pi, piece in enumerate(_split3(x8.T)):
                    cols[2 * pi].append(piece[0:64])
                    cols[2 * pi + 1].append(piece[64:128])
            lhs = jnp.concatenate([jnp.concatenate(c, axis=0) for c in cols], axis=1)
            y = _dot(lhs, qm)
            for m in range(ngrp):
                for st in range(SPREAD_STEPS):
                    o_ref[SPREAD_STEPS * m + st] = y[64 * m:64 * (m + 1), LANES * st:LANES * (st + 1)]

    qm = _spread_matrix(bsz)
    return pl.pallas_call(
        body, name="wkv_spread", grid=(t // SPREAD_BLOCK,),
        out_shape=[jax.ShapeDtypeStruct((t, 64, LANES), F32)] * n,
        in_specs=[pl.BlockSpec((bsz, SPREAD_BLOCK, RW), lambda i: (0, i, 0))] * n + [_full(qm.shape)],
        out_specs=[pl.BlockSpec((SPREAD_BLOCK, 64, LANES), lambda i: (i, 0, 0))] * n,
        compiler_params=_cparams(("parallel",)),
    )(*xs, qm)


def _wkv_fwd(r, w, kp, al, be, v):
    t, vq = v.shape[0], v.shape[1]

    def body(r_ref, w_ref, kp_ref, al_ref, be_ref, v_ref, y_ref, a_ref, u_ref, st_ref):
        @pl.when(pl.program_id(0) == 0)
        def _():
            st_ref[...] = jnp.zeros(st_ref.shape, F32)

        def step(tl, _):
            rv, wv, kv, av, bv = r_ref[tl], w_ref[tl], kp_ref[tl], al_ref[tl], be_ref[tl]
            vals = v_ref[tl]
            yrows, urows = [], []
            for q in range(vq):
                s = st_ref[q]
                u = _ksum(s * av)
                s = s * wv + bv * u + kv * vals[q:q + 1]
                st_ref[q] = s
                a_ref[tl, q] = s
                urows.append(u)
                yrows.append(_ksum(s * rv))
            y_ref[tl] = jnp.concatenate(yrows, axis=0)
            u_ref[tl] = jnp.concatenate(urows, axis=0)
            return 0

        lax.fori_loop(0, CH, step, 0)

    kspec = pl.BlockSpec((CH, 64, LANES), lambda i: (i, 0, 0))
    vspec = pl.BlockSpec((CH, vq, LANES), lambda i: (i, 0, 0))
    vsd = jax.ShapeDtypeStruct((t, vq, LANES), F32)
    return pl.pallas_call(
        body, name="wkv_fwd", grid=(t // CH,),
        out_shape=[vsd, jax.ShapeDtypeStruct((t, vq, 64, LANES), F32), vsd],
        in_specs=[kspec] * 5 + [vspec],
        out_specs=[vspec, pl.BlockSpec((CH, vq, 64, LANES), lambda i: (i, 0, 0, 0)), vspec],
        scratch_shapes=[pltpu.VMEM((vq, 64, LANES), F32)],
        compiler_params=_cparams(("arbitrary",)),
    )(r, w, kp, al, be, v)


def _wkv_bwd(r, w, kp, al, be, v, dy, states, u):
    t, vq = v.shape[0], v.shape[1]
    vs = 64 // vq
    group = LANES // vs
    n = t // CH
    ng = CH // vs

    def body(r_ref, w_ref, kp_ref, al_ref, be_ref, v_ref, dy_ref, u_ref, a_ref, ap_ref,
             dr_ref, dw_ref, dkp_ref, dal_ref, dbe_ref, dv_ref, ds_ref):
        @pl.when(pl.program_id(0) == 0)
        def _():
            ds_ref[...] = jnp.zeros(ds_ref.shape, F32)

        earliest = pl.program_id(0) == n - 1

        def reverse(i, _):
            g = ng - 1 - i
            grp = _lane_group((64, LANES), group)
            outs = None
            for j in reversed(range(vs)):
                tl = g * vs + j
                rv, wv, kv, av, bv = r_ref[tl], w_ref[tl], kp_ref[tl], al_ref[tl], be_ref[tl]
                vals, dys, us = v_ref[tl], dy_ref[tl], u_ref[tl]
                acc = None
                dvrows = []
                for q in range(vq):
                    if j > 0:
                        s_prev = a_ref[tl - 1, q]
                    else:
                        before = jnp.where(earliest, 0.0, ap_ref[0, q])
                        s_prev = jnp.where(g == 0, before, a_ref[jnp.maximum(tl - 1, 0), q])
                    dyq = dys[q:q + 1]
                    ds = ds_ref[q] + rv * dyq
                    c = _ksum(ds * bv)
                    dvrows.append(_ksum(ds * kv))
                    terms = (a_ref[tl, q] * dyq, ds * s_prev, ds * vals[q:q + 1], s_prev * c, ds * us[q:q + 1])
                    acc = terms if acc is None else tuple(a + b for a, b in zip(acc, terms))
                    ds_ref[q] = ds * wv + av * c
                dv_ref[tl] = jnp.concatenate(dvrows, axis=0)
                summed = [_fold(a, group) for a in acc]
                outs = summed if outs is None else [jnp.where(grp == j, f, o) for f, o in zip(summed, outs)]
            for ref, o in zip((dr_ref, dw_ref, dkp_ref, dal_ref, dbe_ref), outs):
                ref[g] = o
            return 0

        lax.fori_loop(0, ng, reverse, 0)

    kspec = pl.BlockSpec((CH, 64, LANES), lambda i: (n - 1 - i, 0, 0))
    gspec = pl.BlockSpec((ng, 64, LANES), lambda i: (n - 1 - i, 0, 0))
    vspec = pl.BlockSpec((CH, vq, LANES), lambda i: (n - 1 - i, 0, 0))
    ksd = jax.ShapeDtypeStruct((t // vs, 64, LANES), F32)
    return pl.pallas_call(
        body, name="wkv_bwd", grid=(n,),
        out_shape=[ksd] * 5 + [jax.ShapeDtypeStruct((t, vq, LANES), F32)],
        in_specs=[kspec] * 5 + [vspec, vspec, vspec,
                                pl.BlockSpec((CH, vq, 64, LANES), lambda i: (n - 1 - i, 0, 0, 0)),
                                pl.BlockSpec((1, vq, 64, LANES), lambda i: (jnp.maximum((n - 1 - i) * CH - 1, 0), 0, 0, 0))],
        out_specs=[gspec] * 5 + [vspec],
        scratch_shapes=[pltpu.VMEM((vq, 64, LANES), F32)],
        compiler_params=_cparams(("arbitrary",)),
    )(r, w, kp, al, be, v, dy, u, states, states)


def _post(x, tgt, pp, o, yw, r, kp, v, ln_g, ln_b, r_k, wo, wot, gpost, bo):
    bsz, t, _ = x.shape
    tt = TT_VPU
    nt = t // tt

    def body(x_ref, tgt_ref, z_ref, o_ref, yw_ref, r_ref, kp_ref, v_ref, lng_ref, lnb_ref, rk_ref, wo_ref, wot_ref,
             gpost_ref, bo_ref,
             dh_ref, dz_ref, dym_ref, dyw_ref, dbon_ref, loss_ref, dwo_ref, dgpost_ref, dlng_ref, dlnb_ref, drk_ref):
        first = (pl.program_id(0) == 0) & (pl.program_id(1) == 0)

        @pl.when(first)
        def _():
            for ref in (loss_ref, dwo_ref, dgpost_ref, dlng_ref, dlnb_ref, drk_ref):
                ref[...] = jnp.zeros(ref.shape, F32)

        bo_m = bo_ref[...]
        seg = lambda a: _seg(a, bo_m)
        rowsum = lambda a: jnp.sum(a, axis=0, keepdims=True)
        ywv, rv, kpv, vv = yw_ref[0], r_ref[0], kp_ref[0], v_ref[0]
        ln_g, r_k = lng_ref[...], rk_ref[...]
        mean = seg(ywv) * (1.0 / 64)
        yc = ywv - mean
        rstd = lax.rsqrt(seg(yc * yc) * (1.0 / 64) + GN_EPS)
        yhat = yc * rstd
        sb = seg(rv * kpv * r_k)
        y_rw = yhat * ln_g + lnb_ref[...] + sb * vv
        z = z_ref[0]
        sig = _sigmoid(z)
        sz = z * sig
        ycat = jnp.concatenate([o_ref[0], y_rw], axis=1)
        ycg = (ycat * sz).astype(BF16)
        out = _dot(ycg, wo_ref[...])
        hn, nx, rstd_o = _rms(out, gpost_ref[...], D)
        err = x_ref[0] + hn - tgt_ref[0]
        loss_ref[...] += jnp.sum(err * err) * (0.5 / D)
        dh = err * (1.0 / D)
        dh_ref[0] = dh
        dout, dgp = _rms_bwd(dh, nx, rstd_o, gpost_ref[...], D)
        dgpost_ref[...] += dgp
        doutb = dout.astype(BF16)
        dwo_ref[...] += _dot_tn(ycg, doutb)
        dycg = _dot(doutb, wot_ref[...])
        dz_ref[0] = dycg * ycat * (sig * (1.0 + z * (1.0 - sig)))
        dycat = dycg * sz
        dym_ref[0] = dycat[:, 0:512]
        dy_rw = dycat[:, 512:1024]
        dlnb_ref[...] += rowsum(dy_rw)
        dlng_ref[...] += rowsum(dy_rw * yhat)
        dyhat = dy_rw * ln_g
        dyw_ref[0] = rstd * (dyhat - seg(dyhat) * (1.0 / 64) - yhat * (seg(dyhat * yhat) * (1.0 / 64)))
        dsb = seg(dy_rw * vv)
        drk_ref[...] += rowsum(dsb * rv * kpv)
        dbon_ref[0, :, 0:512] = dsb * kpv * r_k
        dbon_ref[0, :, 512:1024] = dsb * rv * r_k
        dbon_ref[0, :, 1024:1536] = dy_rw * sb

    tok = lambda c: pl.BlockSpec((1, tt, c), lambda b, i: (b, i, 0))
    full = lambda a: _full(a.shape)
    ins = (x, tgt, pp, o, yw, r, kp, v, ln_g, ln_b, r_k, wo, wot, gpost, bo)
    in_specs = [tok(D), tok(D), tok(1024)] + [tok(512)] * 5 + [full(a) for a in ins[8:]]
    sd = lambda c: jax.ShapeDtypeStruct((bsz, t, c), F32)
    vec = lambda c: jax.ShapeDtypeStruct((1, c), F32)
    out_shape = [sd(D), sd(1024), sd(512), sd(512), sd(1536), jax.ShapeDtypeStruct((8, LANES), F32),
                 jax.ShapeDtypeStruct((1024, 1024), F32), vec(D), vec(512), vec(512), vec(512)]
    out_specs = [tok(D), tok(1024), tok(512), tok(512), tok(1536), _resident((8, LANES)), _resident((1024, 1024)),
                 _resident((1, D)), _resident((1, 512)), _resident((1, 512)), _resident((1, 512))]
    return pl.pallas_call(
        body, name="post", grid=(bsz, nt), out_shape=out_shape, in_specs=in_specs, out_specs=out_specs,
        compiler_params=_cparams(("arbitrary", "arbitrary")),
    )(*ins)


def _pre_bwd_a(pp, pos, invf, cqkv_w, mu, w0, w2p, w2pt, a0, a2p, a2pt, k_k, k_a, bo,
               dq, dk, dva, dwkv, dbon):
    gq, wuqt, gkv, wukvt = cqkv_w
    bsz, t, _ = pp.shape
    tt = TT_VPU
    nt = t // tt
    dr_w, dw_w, dkp_w, dv_w, dal_w, dbe_w = dwkv

    def body(pp_ref, pos_ref, invf_ref, gq_ref, wuqt_ref, gkv_ref, wukvt_ref, mu_ref, w0_ref, w2p_ref, w2pt_ref,
             a0_ref, a2p_ref, a2pt_ref, kk_ref, ka_ref, bo_ref, dq_ref, dk_ref, dva_ref,
             dr_ref, dw_ref, dkp_ref, dv_ref, dal_ref, dbe_ref, dbon_ref,
             da_ref, dwuq_ref, dwukv_ref, dw2p_ref, da2p_ref, dgq_ref, dgkv_ref, dmu_ref, dw0_ref, da0_ref,
             dkk_ref, dka_ref, carry):
        i = pl.program_id(1)
        first = (pl.program_id(0) == 0) & (i == 0)

        @pl.when(first)
        def _():
            for ref in (dwuq_ref, dwukv_ref, dw2p_ref, da2p_ref, dgq_ref, dgkv_ref, dmu_ref, dw0_ref, da0_ref,
                        dkk_ref, dka_ref):
                ref[...] = jnp.zeros(ref.shape, F32)

        bo_m = bo_ref[...]
        rowsum = lambda a: jnp.sum(a, axis=0, keepdims=True)
        prw = pp_ref[0, :, RW0:DP]

        @pl.when(i == 0)
        def _():
            carry[...] = jnp.zeros(carry.shape, F32)

        ps, sh = _shift_mix(prw, carry[7:8, :], mu_ref[...])
        carry[...] = prw[tt - 8:tt, :]
        k_k, k_a = kk_ref[...], ka_ref[...]
        g = _rw_gates(ps, w0_ref[...], w2p_ref[...], a0_ref[...], a2p_ref[...], k_k, k_a, bo_m)
        a, kk, k = g["a"], g["kk"], g["k"]
        dr = dr_ref[0] + dbon_ref[0, :, 0:512]
        dkp = dkp_ref[0] + dbon_ref[0, :, 512:1024]
        dv = dv_ref[0] + dbon_ref[0, :, 1024:1536]
        dbe = dbe_ref[0]
        dkk = dbe * a - dal_ref[0]
        da = dbe * kk + dkp * k * k_a
        dka_ref[...] += rowsum(dkp * k * (a - 1.0))
        dm = (dkk - kk * _seg(dkk * kk, bo_m)) / g["nrm"]
        dkk_ref[...] += rowsum(dm * k)
        dk_tot = dkp * (1.0 + (a - 1.0) * k_a) + dm * k_k
        dapre = da * a * (1.0 - a)
        da0_ref[...] += rowsum(dapre)
        dapb = dapre.astype(BF16)
        da2p_ref[...] += _dot_tn(g["misc"].astype(BF16), dapb)
        dwpre = dw_ref[0] * g["w"] * (-g["e"]) * _sigmoid(-g["wpre"])
        dw0_ref[...] += rowsum(dwpre)
        dwpb = dwpre.astype(BF16)
        th = g["th"]
        dw2p_ref[...] += _dot_tn(th.astype(BF16), dwpb)
        dmisc = _dot(dapb, a2pt_ref[...]) + _dot(dwpb, w2pt_ref[...]) * (1.0 - th * th)
        ang = pos_ref[0] * invf_ref[...]
        cs, sn = jnp.cos(ang), jnp.sin(ang)
        unrope = lambda gr: gr * cs - _rot(gr * sn)
        lane = lax.broadcasted_iota(jnp.int32, cs.shape, 1)
        dkr = dk_ref[0, :, 128:256]
        for h in range(1, HEADS):
            dkr = dkr + dk_ref[0, :, 256 * h + 128:256 * h + 256]
        dkr = jnp.where(lane < 64, unrope(dkr), 0.0)
        dmisc = dmisc + jnp.concatenate([dkr, jnp.zeros_like(dkr)], axis=1)
        dqp = jnp.concatenate(
            [blk for h in range(HEADS)
             for blk in (dq_ref[0, :, 256 * h:256 * h + 128], unrope(dq_ref[0, :, 256 * h + 128:256 * h + 256]))],
            axis=1).astype(BF16)
        dkvp = jnp.concatenate([dk_ref[0, :, 256 * h:256 * h + 128] for h in range(HEADS)] + [dva_ref[0]],
                               axis=1).astype(BF16)
        cqn, cq_nx, cq_rstd = _rms(pp_ref[0, :, CQ0:CQ0 + 256], gq_ref[...], 256)
        ckvn, ckv_nx, ckv_rstd = _rms(pp_ref[0, :, CKV0:CKV0 + 128], gkv_ref[...], 128)
        dwuq_ref[...] += _dot_tn(cqn.astype(BF16), dqp)
        dwukv_ref[...] += _dot_tn(ckvn.astype(BF16), dkvp)
        dcq, dgq = _rms_bwd(_dot(dqp, wuqt_ref[...]), cq_nx, cq_rstd, gq_ref[...], 256)
        dckv, dgkv = _rms_bwd(_dot(dkvp, wukvt_ref[...]), ckv_nx, ckv_rstd, gkv_ref[...], 128)
        dgq_ref[...] += dgq
        dgkv_ref[...] += dgkv
        dps = jnp.concatenate([dr, dk_tot, dv, dmisc], axis=1)
        dmu_ref[...] += rowsum(dps * (sh - prw))
        da_ref[0, :, 0:256] = dcq
        da_ref[0, :, 256:384] = dckv
        da_ref[0, :, 384:384 + NRW] = dps

    tok = lambda c: pl.BlockSpec((1, tt, c), lambda b, i: (b, i, 0))
    full = lambda a: _full(a.shape)
    ins = (pp, pos, invf, gq, wuqt, gkv, wukvt, mu, w0, w2p, w2pt, a0, a2p, a2pt, k_k, k_a, bo,
           dq, dk, dva, dr_w, dw_w, dkp_w, dv_w, dal_w, dbe_w, dbon)
    in_specs = ([tok(DP), tok(1)] + [full(a) for a in ins[2:17]] + [tok(1024), tok(1024), tok(512)]
                + [tok(512)] * 6 + [tok(1536)])
    shp = lambda *s: jax.ShapeDtypeStruct(s, F32)
    out_shape = [shp(bsz, t, 384 + NRW), shp(256, 1024), shp(128, 1024), shp(256, 512), shp(256, 512),
                 shp(1, 256), shp(1, 128), shp(1, NRW), shp(1, 512), shp(1, 512), shp(1, 512), shp(1, 512)]
    out_specs = [tok(384 + NRW)] + [_resident(s.shape) for s in out_shape[1:]]
    return pl.pallas_call(
        body, name="pre_bwd_a", grid=(bsz, nt), out_shape=out_shape, in_specs=in_specs, out_specs=out_specs,
        scratch_shapes=[pltpu.VMEM((8, NRW), F32)],
        compiler_params=_cparams(("arbitrary", "arbitrary")),
    )(*ins)


def _pre_bwd_b(x, dh, dz, da, mu, wpt, gpre):
    bsz, t, _ = x.shape
    nt = t // TT
    nblk = t // 8

    def body(x_ref, dh_ref, dz_ref, da_ref, nxt_ref, mu_ref, wpt_ref, gpre_ref, gx_ref, dp_ref, dgpre_ref):
        i = pl.program_id(1)
        first = (pl.program_id(0) == 0) & (i == 0)

        @pl.when(first)
        def _():
            dgpre_ref[...] = jnp.zeros(dgpre_ref.shape, F32)

        mu_v = mu_ref[...]
        dps = da_ref[0, :, 384:384 + NRW]
        nxt = jnp.where(i < nt - 1, nxt_ref[0, 0:1, 384:384 + NRW], 0.0)
        row = lax.broadcasted_iota(jnp.int32, dps.shape, 0)
        up = jnp.where(row == TT - 1, nxt, pltpu.roll(dps, TT - 1, 0))
        dprw = dps * (1.0 - mu_v) + up * mu_v
        dp = jnp.concatenate([dz_ref[0], da_ref[0, :, 0:384], dprw], axis=1).astype(BF16)
        dp_ref[0] = dp
        du = _dot(dp, wpt_ref[...])
        _, nx, rstd = _rms(x_ref[0], gpre_ref[...], D)
        dx, dg = _rms_bwd(du, nx, rstd, gpre_ref[...], D)
        dgpre_ref[...] += dg
        gx_ref[0] = dh_ref[0] + dx

    tok = lambda c: pl.BlockSpec((1, TT, c), lambda b, i: (b, i, 0))
    nxt_spec = pl.BlockSpec((1, 8, 384 + NRW), lambda b, i: (b, jnp.minimum((i + 1) * (TT // 8), nblk - 1), 0))
    ins = (x, dh, dz, da, da, mu, wpt, gpre)
    return pl.pallas_call(
        body, name="pre_bwd_b", grid=(bsz, nt),
        out_shape=[jax.ShapeDtypeStruct((bsz, t, D), F32), jax.ShapeDtypeStruct((bsz, t, DP), BF16),
                   jax.ShapeDtypeStruct((1, D), F32)],
        in_specs=[tok(D), tok(D), tok(1024), tok(384 + NRW), nxt_spec, _full(mu.shape), _full(wpt.shape),
                  _full(gpre.shape)],
        out_specs=[tok(D), tok(DP), _resident((1, D))],
        compiler_params=_cparams(("arbitrary", "arbitrary")),
    )(*ins)


def _tn_matmul(a, b, bn, name, bk=512):
    kdim, m = a.shape
    _, n = b.shape
    nk = kdim // bk

    def body(a_ref, b_ref, o_ref):
        @pl.when(pl.program_id(1) == 0)
        def _():
            o_ref[...] = jnp.zeros(o_ref.shape, F32)

        o_ref[...] += _dot_tn(a_ref[...], b_ref[...])

    return pl.pallas_call(
        body, name=name, grid=(n // bn, nk),
        out_shape=jax.ShapeDtypeStruct((m, n), F32),
        in_specs=[pl.BlockSpec((bk, m), lambda j, kk: (kk, 0)), pl.BlockSpec((bk, bn), lambda j, kk: (kk, j))],
        out_specs=pl.BlockSpec((m, bn), lambda j, kk: (0, j)),
        compiler_params=_cparams(("parallel", "arbitrary")),
    )(a, b)


SHARDED = ("w_in", "mla_w_uq", "mla_w_ukv", "rw_w2", "rw_a2", "w_out")
SMALL = ("norm_pre_g", "mla_q_norm_g", "mla_kv_norm_g", "rw_mu", "rw_w0", "rw_a0", "rw_k_k", "rw_k_a", "rw_r_k",
         "rw_ln_g", "rw_ln_b", "norm_post_g")
WEIGHTS = ("norm_pre_g", "w_in", "mla_q_norm_g", "mla_w_uq", "mla_kv_norm_g", "mla_w_ukv", "rw_mu", "rw_w0", "rw_w2",
           "rw_a0", "rw_a2", "rw_k_k", "rw_k_a", "rw_r_k", "rw_ln_g", "rw_ln_b", "w_out", "norm_post_g")


def _pack_small(d):
    flat = jnp.concatenate([d[n].reshape(1, -1) for n in SMALL], axis=1)
    return jnp.pad(flat, ((0, 0), (0, SMALL_ROWS * LANES - flat.shape[1]))).reshape(SMALL_ROWS, LANES)


def _unpack_small(packed, like):
    flat = packed.reshape(1, -1)
    out, at = {}, 0
    for n in SMALL:
        size = int(np.prod(like[n].shape))
        out[n] = flat[:, at:at + size].reshape(like[n].shape)
        at += size
    return out


def _unpack_shard(packed, like):
    out, at = {}, 0
    for n, rows in zip(SHARDED, PACK_ROWS):
        out[n] = packed[at:at + rows].reshape(like[n].shape)
        at += rows
    return out


def _constants():
    bo = np.kron(np.eye(2, dtype=np.float32), np.ones((64, 64), np.float32))
    inv = ROPE_THETA ** (-np.arange(0, 64, 2, dtype=np.float32) / 64)
    invf = np.concatenate([inv, inv, np.zeros(64, np.float32)]).astype(np.float32)[None, :]
    return jnp.asarray(bo, BF16), jnp.asarray(invf)


def kernel(x, positions, norm_pre_g, w_in, mla_q_norm_g, mla_w_uq, mla_kv_norm_g, mla_w_ukv, rw_mu, rw_w0, rw_w2, rw_a0, rw_a2, rw_k_k, rw_k_a, rw_r_k, rw_ln_g, rw_ln_b, w_out, norm_post_g, loss_target, m_norm_pre_g, m_w_in, m_mla_q_norm_g, m_mla_w_uq, m_mla_kv_norm_g, m_mla_w_ukv, m_rw_mu, m_rw_w0, m_rw_w2, m_rw_a0, m_rw_a2, m_rw_k_k, m_rw_k_a, m_rw_r_k, m_rw_ln_g, m_rw_ln_b, m_w_out, m_norm_post_g, v_norm_pre_g, v_w_in, v_mla_q_norm_g, v_mla_w_uq, v_mla_kv_norm_g, v_mla_w_ukv, v_rw_mu, v_rw_w0, v_rw_w2, v_rw_a0, v_rw_a2, v_rw_k_k, v_rw_k_a, v_rw_r_k, v_rw_ln_g, v_rw_ln_b, v_w_out, v_norm_post_g):
    wts = dict(norm_pre_g=norm_pre_g, w_in=w_in, mla_q_norm_g=mla_q_norm_g, mla_w_uq=mla_w_uq,
               mla_kv_norm_g=mla_kv_norm_g, mla_w_ukv=mla_w_ukv, rw_mu=rw_mu, rw_w0=rw_w0, rw_w2=rw_w2, rw_a0=rw_a0,
               rw_a2=rw_a2, rw_k_k=rw_k_k, rw_k_a=rw_k_a, rw_r_k=rw_r_k, rw_ln_g=rw_ln_g, rw_ln_b=rw_ln_b, w_out=w_out,
               norm_post_g=norm_post_g)
    mom_m = dict(norm_pre_g=m_norm_pre_g, w_in=m_w_in, mla_q_norm_g=m_mla_q_norm_g, mla_w_uq=m_mla_w_uq,
                 mla_kv_norm_g=m_mla_kv_norm_g, mla_w_ukv=m_mla_w_ukv, rw_mu=m_rw_mu, rw_w0=m_rw_w0, rw_w2=m_rw_w2,
                 rw_a0=m_rw_a0, rw_a2=m_rw_a2, rw_k_k=m_rw_k_k, rw_k_a=m_rw_k_a, rw_r_k=m_rw_r_k, rw_ln_g=m_rw_ln_g,
                 rw_ln_b=m_rw_ln_b, w_out=m_w_out, norm_post_g=m_norm_post_g)
    mom_v = dict(norm_pre_g=v_norm_pre_g, w_in=v_w_in, mla_q_norm_g=v_mla_q_norm_g, mla_w_uq=v_mla_w_uq,
                 mla_kv_norm_g=v_mla_kv_norm_g, mla_w_ukv=v_mla_w_ukv, rw_mu=v_rw_mu, rw_w0=v_rw_w0, rw_w2=v_rw_w2,
                 rw_a0=v_rw_a0, rw_a2=v_rw_a2, rw_k_k=v_rw_k_k, rw_k_a=v_rw_k_a, rw_r_k=v_rw_r_k, rw_ln_g=v_rw_ln_g,
                 rw_ln_b=v_rw_ln_b, w_out=v_w_out, norm_post_g=v_norm_post_g)
    bsz, t, _ = x.shape
    bo, invf = _constants()

    g_in, g_uq, g_ukv, g_w2, g_a2, g_out = _ag_weights([wts[n][0] for n in SHARDED])
    w_in_f = jnp.transpose(g_in, (1, 0, 2)).reshape(D, D_IN)
    wp = jnp.concatenate([w_in_f[:, 2112:3136], w_in_f[:, 0:384], w_in_f[:, 448:1984], w_in_f[:, 384:448],
                          w_in_f[:, 1984:2112], jnp.zeros((D, 64), BF16)], axis=1)
    wuq = jnp.pad(jnp.transpose(g_uq, (1, 0, 2)).reshape(256, HEADS, 192), ((0, 0), (0, 0), (0, 64))).reshape(256, 1024)
    wukv = jnp.transpose(jnp.transpose(g_ukv, (1, 0, 2)).reshape(128, HEADS, 2, 128), (0, 2, 1, 3)).reshape(128, 1024)
    w2 = jnp.transpose(g_w2, (1, 0, 2)).reshape(64, RW)
    a2 = jnp.transpose(g_a2, (1, 0, 2)).reshape(64, RW)
    w2p = jnp.pad(w2, ((64, 128), (0, 0)))
    a2p = jnp.pad(a2, ((128, 64), (0, 0)))
    wo = g_out.reshape(D, D)
    mu = jnp.concatenate([rw_mu[:, 0:1536], jnp.zeros((1, 64), F32), rw_mu[:, 1536:1664], jnp.zeros((1, 64), F32)],
                         axis=1)
    r_k = rw_r_k.reshape(1, RW)
    pos = positions.astype(F32)[:, :, None]

    (u, pp, q_att, k_att, v_att, r, w, kp, v, al, be) = _pre_fwd(
        x, pos, invf, norm_pre_g, wp, mla_q_norm_g, wuq, mla_kv_norm_g, wukv, mu, rw_w0, w2p, rw_a0, a2p, rw_k_k,
        rw_k_a, bo)
    o, lse = _attn_fwd(q_att, k_att, v_att)
    rw_k = _spread_k([r, w, kp, al, be])
    v_v = _to_v(v)
    yw_v, states, u_v = _wkv_fwd(*rw_k, v_v)
    yw = _from_v(yw_v, bsz)

    (dh, dz, dym, dyw, dbon, loss_acc, d_wo, d_gpost, d_lng, d_lnb, d_rk) = _post(
        x, loss_target, pp, o, yw, r, kp, v, rw_ln_g, rw_ln_b, r_k, wo, wo.T, norm_post_g, bo)

    d_k = _wkv_bwd(*rw_k, v_v, _to_v(dyw), states, u_v)
    dr_w, dw_w, dkp_w, dal_w, dbe_w = _gather_k(d_k[:5], bsz)
    dwkv = (dr_w, dw_w, dkp_w, _from_v(d_k[5], bsz), dal_w, dbe_w)
    dq, dk, dva = _attn_bwd(q_att, k_att, v_att, o, lse, dym)

    (da, d_wuq, d_wukv, d_w2p, d_a2p, d_gq, d_gkv, d_mu, d_w0, d_a0, d_kk, d_ka) = _pre_bwd_a(
        pp, pos, invf, (mla_q_norm_g, wuq.T, mla_kv_norm_g, wukv.T), mu, rw_w0, w2p, w2p.T, rw_a0, a2p, a2p.T,
        rw_k_k, rw_k_a, bo, dq, dk, dva, dwkv, dbon)
    grad_x, dpb, d_gpre = _pre_bwd_b(x, dh, dz, da, mu, wp.T, norm_pre_g)
    d_wp = _tn_matmul(u.reshape(bsz * t, D), dpb.reshape(bsz * t, DP), DP, "dw_in", bk=1024)

    full_g = {
        "w_in": jnp.concatenate([d_wp[:, 1024:1408], d_wp[:, 2944:3008], d_wp[:, 1408:2944], d_wp[:, 3008:3136],
                                 d_wp[:, 0:1024]], axis=1),
        "mla_w_uq": d_wuq.reshape(256, HEADS, 256)[:, :, :192].reshape(256, 768),
        "mla_w_ukv": jnp.transpose(d_wukv.reshape(128, 2, HEADS, 128), (0, 2, 1, 3)).reshape(128, 1024),
        "rw_w2": d_w2p[64:128],
        "rw_a2": d_a2p[128:192],
        "w_out": d_wo,
    }
    small_g = {
        "norm_pre_g": d_gpre, "mla_q_norm_g": d_gq, "mla_kv_norm_g": d_gkv,
        "rw_mu": jnp.concatenate([d_mu[:, 0:1536], d_mu[:, 1600:1728]], axis=1),
        "rw_w0": d_w0, "rw_a0": d_a0, "rw_k_k": d_kk, "rw_k_a": d_ka, "rw_r_k": d_rk, "rw_ln_g": d_lng,
        "rw_ln_b": d_lnb, "norm_post_g": d_gpost,
    }

    def by_shard(name, g):
        if name == "w_out":
            return g.reshape(N_SHARD, -1, LANES)
        rows, cols = g.shape
        return jnp.transpose(g.reshape(rows, N_SHARD, cols // N_SHARD), (1, 0, 2)).reshape(N_SHARD, -1, LANES)

    packed = jnp.concatenate([by_shard(n, full_g[n]) for n in SHARDED], axis=1)
    pair_sum, pair_sum_b = _rs_pairs(packed.reshape(N_SHARD, 2, HALF, LANES))
    g_shard = _rs_chips(pair_sum, pair_sum_b).reshape(PACK_TOTAL, LANES)

    g_small = _small_allreduce(jnp.concatenate([_pack_small(small_g)[:SMALL_USED], loss_acc[0:SMALL_ROWS - SMALL_USED]]))
    loss = g_small[SMALL_USED, 0]

    g_sharded = _unpack_shard(g_shard, {n: wts[n][0] for n in SHARDED})
    sh = _adamw([wts[n][0] for n in SHARDED], [g_sharded[n] for n in SHARDED], [mom_m[n][0] for n in SHARDED],
                [mom_v[n][0] for n in SHARDED], "adamw_sharded")
    sm = _adamw([_pack_small(wts)], [g_small], [_pack_small(mom_m)], [_pack_small(mom_v)], "adamw_small")

    def outputs(sharded, small):
        out = {n: a[None] for n, a in zip(SHARDED, sharded)}
        out.update(_unpack_small(small, wts))
        return out

    grads = outputs([g_sharded[n] for n in SHARDED], g_small)
    deltas, new_m, new_v = (outputs(sh[k], sm[k][0]) for k in range(3))
    return (loss, grad_x, *[grads[n] for n in WEIGHTS], *[deltas[n] for n in WEIGHTS],
            *[new_m[n] for n in WEIGHTS], *[new_v[n] for n in WEIGHTS])
```

```python
import numpy as np
import jax
import jax.numpy as jnp
from jax import lax
from jax.experimental import pallas as pl
from jax.experimental.pallas import tpu as pltpu

F32, BF16 = jnp.float32, jnp.bfloat16
MESH = pl.DeviceIdType.MESH

D = 1024
HEADS = 4
RW = 512
NORM_EPS = 1e-6
GN_EPS = 64e-5
ROPE_THETA = 10000.0
SCALE = (128 + 64) ** -0.5
D_IN = 3136
LR, B1, B2, ADAM_EPS, WD, STEP = 0.001, 0.9, 0.999, 1e-08, 0.01, 10

Z0, CQ0, CKV0, RW0, DP = 0, 1024, 1280, 1408, 3200
NRW = DP - RW0

LANES = 128
SUBLANES = 8
VMEM_LIMIT = 56 * 1024 * 1024

TT = 512
TT_VPU = 256
TQ = 512

N_SHARD = 4
PACK_ROWS = (1024 * 784 // 128, 256 * 192 // 128, 128 * 256 // 128, 64, 64, 256 * 1024 // 128)
PACK_TOTAL = sum(PACK_ROWS)
HALF = PACK_TOTAL // 2
SMALL_ROWS = 64
SMALL_USED = 60


def _cparams(sem=None):
    return pltpu.CompilerParams(dimension_semantics=sem, vmem_limit_bytes=VMEM_LIMIT)


def _full(shape):
    n = len(shape)
    return pl.BlockSpec(shape, lambda *_: (0,) * n, pipeline_mode=pl.Buffered(1))


def _resident(shape):
    n = len(shape)
    return pl.BlockSpec(shape, lambda *_: (0,) * n)


def _dot(a, b):
    return jnp.dot(a, b, preferred_element_type=F32)


def _dot_nt(a, b):
    return lax.dot_general(a, b, (((1,), (1,)), ((), ())), preferred_element_type=F32)


def _dot_tn(a, b):
    return lax.dot_general(a, b, (((0,), (0,)), ((), ())), preferred_element_type=F32)


def _split3(x):
    hi = x.astype(BF16)
    r1 = x - hi.astype(F32)
    mid = r1.astype(BF16)
    lo = (r1 - mid.astype(F32)).astype(BF16)
    return hi, mid, lo


def _seg(x, bo):
    rows, nblk = x.shape[0], x.shape[1] // LANES
    pieces = [p for i in range(nblk) for p in _split3(x[:, LANES * i:LANES * (i + 1)])]
    res = _dot(jnp.concatenate(pieces, axis=0), bo)
    parts = [res[(3 * i) * rows:(3 * i + 1) * rows] + res[(3 * i + 1) * rows:(3 * i + 2) * rows]
             + res[(3 * i + 2) * rows:(3 * i + 3) * rows] for i in range(nblk)]
    return parts[0] if nblk == 1 else jnp.concatenate(parts, axis=1)


def _rms(x, g, n):
    rstd = lax.rsqrt(jnp.sum(x * x, axis=-1, keepdims=True) * (1.0 / n) + NORM_EPS)
    nx = x * rstd
    return nx * g, nx, rstd


def _rms_bwd(dy, nx, rstd, g, n):
    dn = dy * g
    dx = rstd * (dn - nx * (jnp.sum(dn * nx, axis=-1, keepdims=True) * (1.0 / n)))
    return dx, jnp.sum(dy * nx, axis=0, keepdims=True)


def _rot(x):
    lane = lax.broadcasted_iota(jnp.int32, x.shape, 1)
    return jnp.where((lane % 64) < 32, -pltpu.roll(x, x.shape[1] - 32, 1), pltpu.roll(x, 32, 1))


def _sigmoid(x):
    return 1.0 / (1.0 + jnp.exp(-x))


def _softplus(x):
    return jnp.maximum(x, 0.0) + jnp.log(1.0 + jnp.exp(-jnp.abs(x)))


def _rw_gates(ps, w0, w2p, a0, a2p, k_k, k_a, bo):
    r, k, v, misc = ps[:, 0:512], ps[:, 512:1024], ps[:, 1024:1536], ps[:, 1536:NRW]
    th = jnp.tanh(misc)
    wpre = w0 + _dot(th.astype(BF16), w2p)
    e = jnp.exp(-_softplus(-wpre) - 0.5)
    w = jnp.exp(-e)
    a = _sigmoid(a0 + _dot(misc.astype(BF16), a2p))
    m = k * k_k
    nrm = jnp.maximum(jnp.sqrt(_seg(m * m, bo)), 1e-12)
    kk = m / nrm
    kp = k * (1.0 + (a - 1.0) * k_a)
    return dict(r=r, k=k, v=v, misc=misc, th=th, wpre=wpre, e=e, w=w, a=a, nrm=nrm, kk=kk, kp=kp)


def _shift_mix(prw, prev_row, mu):
    row = lax.broadcasted_iota(jnp.int32, prw.shape, 0)
    sh = jnp.where(row == 0, prev_row, pltpu.roll(prw, 1, 0))
    return prw + (sh - prw) * mu, sh


def _ag_weights(shards):
    n = len(shards)

    def body(*refs):
        ins, outs = refs[:n], refs[n:2 * n]
        ici_send, ici_recv, d2d_send, d2d_recv = refs[2 * n:2 * n + 4]
        x, y, c = lax.axis_index("x"), lax.axis_index("y"), lax.axis_index("c")
        mine = 2 * x + y
        for w in range(n):
            outs[w][mine] = ins[w][...].astype(BF16)
        flips = ((1, 0), (0, 1), (1, 1))

        def half(w, shard, cc):
            rows = outs[w].shape[1] // 2
            return outs[w].at[shard, pl.ds(pl.multiple_of(cc * rows, 16), rows)]

        def ici(w, k, shard):
            fx, fy = flips[k]
            return pltpu.make_async_remote_copy(
                src_ref=half(w, shard, c), dst_ref=half(w, shard, c),
                send_sem=ici_send.at[w * 3 + k], recv_sem=ici_recv.at[w * 3 + k],
                device_id=(x ^ fx, y ^ fy, c), device_id_type=MESH)

        def d2d(w, k, cc):
            fx, fy = flips[k]
            theirs = 2 * (x ^ fx) + (y ^ fy)
            return pltpu.make_async_remote_copy(
                src_ref=half(w, theirs, cc), dst_ref=half(w, theirs, cc),
                send_sem=d2d_send.at[w * 3 + k], recv_sem=d2d_recv.at[w * 3 + k],
                device_id=(x, y, 1 - c), device_id_type=MESH)

        for w in range(n):
            for k in range(3):
                ici(w, k, mine).start()
        for w in range(n):
            for k in range(3):
                fx, fy = flips[k]
                ici(w, k, 2 * (x ^ fx) + (y ^ fy)).wait_recv()
                d2d(w, k, c).start()
        for w in range(n):
            for k in range(3):
                d2d(w, k, 1 - c).wait_recv()
        for w in range(n):
            for k in range(3):
                ici(w, k, mine).wait_send()
                d2d(w, k, c).wait_send()

    vm = pl.BlockSpec(memory_space=pltpu.VMEM)
    return pl.pallas_call(
        body, name="ag_weights",
        out_shape=[jax.ShapeDtypeStruct((N_SHARD,) + s.shape, BF16) for s in shards],
        in_specs=[vm] * n, out_specs=[vm] * n,
        scratch_shapes=[pltpu.SemaphoreType.DMA((3 * n,))] * 4,
        compiler_params=pltpu.CompilerParams(vmem_limit_bytes=VMEM_LIMIT),
    )(*shards)


def _rs_pairs(halves):
    def body(h_ref, sum_ref, sumb_ref, recv, send_sem, recv_sem):
        x, y, c = lax.axis_index("x"), lax.axis_index("y"), lax.axis_index("c")
        cps = [pltpu.make_async_remote_copy(src_ref=h_ref.at[s, 1 - c], dst_ref=recv.at[s], send_sem=send_sem.at[s],
                                            recv_sem=recv_sem.at[s], device_id=(x, y, 1 - c), device_id_type=MESH)
               for s in range(N_SHARD)]
        for cp in cps:
            cp.start()
        for s, cp in enumerate(cps):
            cp.wait_recv()
            acc = h_ref[s, c] + recv[s]
            sum_ref[s] = acc
            sumb_ref[s] = acc.astype(BF16)
        for cp in cps:
            cp.wait_send()

    vm = pl.BlockSpec(memory_space=pltpu.VMEM)
    shape = (N_SHARD,) + halves.shape[2:]
    return pl.pallas_call(
        body, name="rs_pairs",
        out_shape=[jax.ShapeDtypeStruct(shape, F32), jax.ShapeDtypeStruct(shape, BF16)],
        in_specs=[vm], out_specs=[vm, vm],
        scratch_shapes=[pltpu.VMEM(shape, F32), pltpu.SemaphoreType.DMA((N_SHARD,)),
                        pltpu.SemaphoreType.DMA((N_SHARD,))],
        compiler_params=pltpu.CompilerParams(vmem_limit_bytes=VMEM_LIMIT),
    )(halves)


def _rs_chips(part_f32, part_bf16):
    def body(own_ref, src_ref, out_ref, recv, ici_send, ici_recv, d2d_send, d2d_recv):
        x, y, c = lax.axis_index("x"), lax.axis_index("y"), lax.axis_index("c")
        mine = 2 * x + y
        flips = ((1, 0), (0, 1), (1, 1))
        cps = []
        for k, (fx, fy) in enumerate(flips):
            theirs = 2 * (x ^ fx) + (y ^ fy)
            cps.append(pltpu.make_async_remote_copy(
                src_ref=src_ref.at[theirs], dst_ref=recv.at[k],
                send_sem=ici_send.at[k], recv_sem=ici_recv.at[k],
                device_id=(x ^ fx, y ^ fy, c), device_id_type=MESH))
        for cp in cps:
            cp.start()
        acc = own_ref[mine]
        for k, cp in enumerate(cps):
            cp.wait_recv()
            acc = acc + recv[k].astype(F32)
        out_ref[c] = acc
        to_sibling = pltpu.make_async_remote_copy(
            src_ref=out_ref.at[c], dst_ref=out_ref.at[c], send_sem=d2d_send, recv_sem=d2d_recv,
            device_id=(x, y, 1 - c), device_id_type=MESH)
        to_sibling.start()
        pltpu.make_async_remote_copy(
            src_ref=out_ref.at[1 - c], dst_ref=out_ref.at[1 - c], send_sem=d2d_send, recv_sem=d2d_recv,
            device_id=(x, y, 1 - c), device_id_type=MESH).wait_recv()
        to_sibling.wait_send()
        for cp in cps:
            cp.wait_send()

    vm = pl.BlockSpec(memory_space=pltpu.VMEM)
    return pl.pallas_call(
        body, name="rs_chips",
        out_shape=jax.ShapeDtypeStruct((2,) + part_f32.shape[1:], F32),
        in_specs=[vm, vm], out_specs=vm,
        scratch_shapes=[pltpu.VMEM((3,) + part_bf16.shape[1:], BF16), pltpu.SemaphoreType.DMA((3,)),
                        pltpu.SemaphoreType.DMA((3,)), pltpu.SemaphoreType.DMA, pltpu.SemaphoreType.DMA],
        compiler_params=pltpu.CompilerParams(vmem_limit_bytes=VMEM_LIMIT),
    )(part_f32, part_bf16)


def _small_allreduce(vec):
    def body(in_ref, out_ref, recv, send_sems, recv_sems):
        x, y, c = lax.axis_index("x"), lax.axis_index("y"), lax.axis_index("c")
        me = 4 * x + 2 * y + c
        cps = []
        for k in range(1, 8):
            fx, fy, fc = (k >> 2) & 1, (k >> 1) & 1, k & 1
            cps.append(pltpu.make_async_remote_copy(
                src_ref=in_ref, dst_ref=recv.at[k - 1],
                send_sem=send_sems.at[k - 1], recv_sem=recv_sems.at[k - 1],
                device_id=(x ^ fx, y ^ fy, c ^ fc), device_id_type=MESH))
        for cp in cps:
            cp.start()
        for cp in cps:
            cp.wait()
        acc = jnp.zeros(in_ref.shape, F32)
        for j in range(8):
            slot = jnp.maximum((me ^ j) - 1, 0)
            acc = acc + jnp.where(me == j, in_ref[...], recv[slot])
        out_ref[...] = acc

    vm = pl.BlockSpec(memory_space=pltpu.VMEM)
    return pl.pallas_call(
        body, name="small_allreduce",
        out_shape=jax.ShapeDtypeStruct(vec.shape, F32),
        in_specs=[vm], out_specs=vm,
        scratch_shapes=[pltpu.VMEM((7,) + vec.shape, F32), pltpu.SemaphoreType.DMA((7,)),
                        pltpu.SemaphoreType.DMA((7,))],
    )(vec)


ADAM_ROWS = 64


def _adamw(ws, gs, ms, vs, name):
    n = len(ws)

    def body(*refs):
        for i in range(n):
            w_ref, g_ref, m_ref, v_ref = (refs[k * n + i] for k in range(4))
            d_ref, nm_ref, nv_ref = (refs[(4 + k) * n + i] for k in range(3))
            rows = min(ADAM_ROWS, w_ref.shape[0])

            def chunk(r, _):
                at = pl.ds(pl.multiple_of(r * rows, SUBLANES), rows)
                gg = g_ref[at, :]
                nm = B1 * m_ref[at, :] + (1.0 - B1) * gg
                nv = B2 * v_ref[at, :] + (1.0 - B2) * (gg * gg)
                m_hat = nm / (1.0 - B1 ** STEP)
                v_hat = nv / (1.0 - B2 ** STEP)
                d_ref[at, :] = -LR * (m_hat / (jnp.sqrt(v_hat) + ADAM_EPS) + WD * w_ref[at, :])
                nm_ref[at, :] = nm
                nv_ref[at, :] = nv
                return 0

            lax.fori_loop(0, w_ref.shape[0] // rows, chunk, 0)

    vm = pl.BlockSpec(memory_space=pltpu.VMEM)
    sds = [jax.ShapeDtypeStruct(w.shape, F32) for w in ws]
    outs = pl.pallas_call(
        body, name=name, out_shape=sds * 3, in_specs=[vm] * (4 * n), out_specs=[vm] * (3 * n),
        compiler_params=pltpu.CompilerParams(vmem_limit_bytes=VMEM_LIMIT),
    )(*ws, *gs, *ms, *vs)
    return outs[:n], outs[n:2 * n], outs[2 * n:]


def _pre_fwd(x, pos, invf, gpre, wp, gq, wuq, gkv, wukv, mu, w0, w2p, a0, a2p, k_k, k_a, bo):
    bsz, t, _ = x.shape
    nt = t // TT

    def body(x_ref, pos_ref, invf_ref, gpre_ref, wp_ref, gq_ref, wuq_ref, gkv_ref, wukv_ref, mu_ref, w0_ref,
             w2p_ref, a0_ref, a2p_ref, kk_ref, ka_ref, bo_ref,
             u_ref, pp_ref, q_ref, k_ref, v_ref, r_o, w_o, kp_o, vv_o, al_o, be_o, carry):
        i = pl.program_id(1)
        u, _, _ = _rms(x_ref[0], gpre_ref[...], D)
        ub = u.astype(BF16)
        u_ref[0] = ub
        p = _dot(ub, wp_ref[...])
        pp_ref[0] = p
        prw = p[:, RW0:DP]

        @pl.when(i == 0)
        def _():
            carry[...] = jnp.zeros(carry.shape, F32)

        ps, _ = _shift_mix(prw, carry[7:8, :], mu_ref[...])
        carry[...] = prw[TT - 8:TT, :]

        g = _rw_gates(ps, w0_ref[...], w2p_ref[...], a0_ref[...], a2p_ref[...], kk_ref[...], ka_ref[...],
                      bo_ref[...])
        r_o[0] = g["r"]
        w_o[0] = g["w"]
        kp_o[0] = g["kp"]
        vv_o[0] = g["v"]
        al_o[0] = -g["kk"]
        be_o[0] = g["kk"] * g["a"]

        cqn, _, _ = _rms(p[:, CQ0:CQ0 + 256], gq_ref[...], 256)
        q = _dot(cqn.astype(BF16), wuq_ref[...])
        ckvn, _, _ = _rms(p[:, CKV0:CKV0 + 128], gkv_ref[...], 128)
        kv = _dot(ckvn.astype(BF16), wukv_ref[...])
        ang = pos_ref[0] * invf_ref[...]
        cs, sn = jnp.cos(ang), jnp.sin(ang)
        lane = lax.broadcasted_iota(jnp.int32, cs.shape, 1)
        kr = ps[:, 1536:1536 + LANES]
        kr = jnp.where(lane < 64, kr * cs + _rot(kr) * sn, 0.0).astype(BF16)
        for h in range(HEADS):
            qr = q[:, 256 * h + 128:256 * h + 256]
            q_ref[0, :, 256 * h:256 * h + 128] = q[:, 256 * h:256 * h + 128].astype(BF16)
            q_ref[0, :, 256 * h + 128:256 * h + 256] = (qr * cs + _rot(qr) * sn).astype(BF16)
            k_ref[0, :, 256 * h:256 * h + 128] = kv[:, 128 * h:128 * h + 128].astype(BF16)
            k_ref[0, :, 256 * h + 128:256 * h + 256] = kr
        v_ref[0] = kv[:, 512:1024].astype(BF16)

    tok = lambda c: pl.BlockSpec((1, TT, c), lambda b, i: (b, i, 0))
    full = lambda a: _full(a.shape)
    ins = (x, pos, invf, gpre, wp, gq, wuq, gkv, wukv, mu, w0, w2p, a0, a2p, k_k, k_a, bo)
    in_specs = [tok(D), tok(1)] + [full(a) for a in ins[2:]]
    sd = lambda c, dt: jax.ShapeDtypeStruct((bsz, t, c), dt)
    out_shape = [sd(D, BF16), sd(DP, F32), sd(1024, BF16), sd(1024, BF16), sd(512, BF16)] + [sd(RW, F32)] * 6
    out_specs = [tok(D), tok(DP), tok(1024), tok(1024), tok(512)] + [tok(RW)] * 6
    return pl.pallas_call(
        body, name="pre_fwd", grid=(bsz, nt), out_shape=out_shape, in_specs=in_specs, out_specs=out_specs,
        scratch_shapes=[pltpu.VMEM((8, NRW), F32)],
        compiler_params=_cparams(("arbitrary", "arbitrary")),
    )(*ins)


def _attn_fwd(q, k, v):
    bsz, t, _ = q.shape
    nq = t // TQ

    hps = HEADS

    def body(q_ref, k_ref, v_ref, o_ref, lse_ref):
        i = pl.program_id(2)

        def step(j, carry, diagonal):
            at = pl.ds(pl.multiple_of(j * TQ, TQ), TQ)
            out = []
            for hh in range(hps):
                m, l, acc = carry[hh]
                s = _dot_nt(q_ref[0, :, 256 * hh:256 * (hh + 1)], k_ref[0, at, 256 * hh:256 * (hh + 1)]) * SCALE
                if diagonal:
                    s = jnp.where(lax.broadcasted_iota(jnp.int32, (TQ, TQ), 1)
                                  <= lax.broadcasted_iota(jnp.int32, (TQ, TQ), 0), s, -1e30)
                mn = jnp.maximum(m, jnp.max(s, axis=1, keepdims=True))
                p = jnp.exp(s - mn)
                al = jnp.exp(m - mn)
                l = al * l + jnp.sum(p, axis=1, keepdims=True)
                acc = al * acc + _dot(p.astype(BF16), v_ref[0, at, LANES * hh:LANES * (hh + 1)])
                out.append((mn, l, acc))
            return tuple(out)

        start = (jnp.full((TQ, 1), -1e30, F32), jnp.zeros((TQ, 1), F32), jnp.zeros((TQ, LANES), F32))
        before = lax.fori_loop(0, i, lambda j, carry: step(j, carry, False), (start,) * hps)
        for hh, (m, l, acc) in enumerate(step(i, before, True)):
            o_ref[0, :, LANES * hh:LANES * (hh + 1)] = acc / l
            lse_ref[0, hh] = jnp.broadcast_to(m + jnp.log(l), (TQ, LANES))

    return pl.pallas_call(
        body, name="attn_fwd", grid=(bsz, HEADS // hps, nq),
        out_shape=[jax.ShapeDtypeStruct((bsz, t, 512), F32), jax.ShapeDtypeStruct((bsz, HEADS, t, LANES), F32)],
        in_specs=[pl.BlockSpec((1, TQ, 256 * hps), lambda b, h, i: (b, i, h)),
                  pl.BlockSpec((1, t, 256 * hps), lambda b, h, i: (b, 0, h)),
                  pl.BlockSpec((1, t, LANES * hps), lambda b, h, i: (b, 0, h))],
        out_specs=[pl.BlockSpec((1, TQ, LANES * hps), lambda b, h, i: (b, i, h)),
                   pl.BlockSpec((1, hps, TQ, LANES), lambda b, h, i: (b, h, i, 0))],
        compiler_params=_cparams(("parallel", "parallel", "arbitrary")),
    )(q, k, v)


def _attn_bwd(q, k, v, o, lse, do):
    bsz, t, _ = q.shape
    nq = t // TQ

    def body(q_ref, k_ref, v_ref, o_ref, lse_ref, do_ref, dq_ref, dk_ref, dv_ref, dl_ref):
        j = pl.program_id(2)

        @pl.when(j == 0)
        def _():
            def prep(i, _):
                at = pl.ds(pl.multiple_of(i * TQ, TQ), TQ)
                for hh in range(2):
                    lanes = slice(LANES * hh, LANES * (hh + 1))
                    dl_ref[hh, at, :] = jnp.broadcast_to(
                        jnp.sum(do_ref[0, at, lanes] * o_ref[0, at, lanes], axis=1, keepdims=True), (TQ, LANES))
                return 0

            lax.fori_loop(0, nq, prep, 0)
            dq_ref[0] = jnp.zeros((t, 512), F32)

        def q_tile(i, carry, diagonal):
            atq = pl.ds(pl.multiple_of(i * TQ, TQ), TQ)
            out = []
            for hh in range(2):
                dk, dv = carry[hh]
                wide, narrow = slice(256 * hh, 256 * (hh + 1)), slice(LANES * hh, LANES * (hh + 1))
                qt, kt, vt = q_ref[0, atq, wide], k_ref[0, :, wide], v_ref[0, :, narrow]
                dob = do_ref[0, atq, narrow].astype(BF16)
                s = _dot_nt(qt, kt) * SCALE
                if diagonal:
                    s = jnp.where(lax.broadcasted_iota(jnp.int32, (TQ, TQ), 1)
                                  <= lax.broadcasted_iota(jnp.int32, (TQ, TQ), 0), s, -1e30)
                p = jnp.exp(s - lse_ref[0, hh, atq, :][:, 0:1])
                dv = dv + _dot_tn(p.astype(BF16), dob)
                dp = _dot_nt(dob, vt)
                ds = (p * (dp - dl_ref[hh, atq, :][:, 0:1]) * SCALE).astype(BF16)
                dk = dk + _dot_tn(ds, qt)
                dq_ref[0, atq, wide] += _dot(ds, kt)
                out.append((dk, dv))
            return tuple(out)

        zero = (jnp.zeros((TQ, 256), F32), jnp.zeros((TQ, LANES), F32))
        first = q_tile(j, (zero, zero), True)
        done = lax.fori_loop(j + 1, nq, lambda i, carry: q_tile(i, carry, False), first)
        for hh, (dk, dv) in enumerate(done):
            dk_ref[0, :, 256 * hh:256 * (hh + 1)] = dk
            dv_ref[0, :, LANES * hh:LANES * (hh + 1)] = dv

    whole = lambda c: pl.BlockSpec((1, t, c), lambda b, h, j: (b, 0, h))
    tile = lambda c: pl.BlockSpec((1, TQ, c), lambda b, h, j: (b, j, h))
    return pl.pallas_call(
        body, name="attn_bwd", grid=(bsz, HEADS // 2, nq),
        out_shape=[jax.ShapeDtypeStruct((bsz, t, 1024), F32), jax.ShapeDtypeStruct((bsz, t, 1024), F32),
                   jax.ShapeDtypeStruct((bsz, t, 512), F32)],
        in_specs=[whole(512), tile(512), tile(256), whole(256),
                  pl.BlockSpec((1, 2, t, LANES), lambda b, h, j: (b, h, 0, 0)), whole(256)],
        out_specs=[whole(512), tile(512), tile(256)],
        scratch_shapes=[pltpu.VMEM((2, t, LANES), F32)],
        compiler_params=_cparams(("parallel", "parallel", "arbitrary")),
    )(q, k, v, o, lse, do)


RW_HEADS = 8
CH = 32


def _lane_split(bsz):
    vs = LANES // (bsz * RW_HEADS)
    return vs, 64 // vs


def _gather_matrix(bsz):
    group = bsz * RW_HEADS
    vs = LANES // group
    half = (RW_HEADS // 2) * bsz * SPREAD_STEPS
    p = np.zeros((SPREAD_STEPS // vs * LANES, 2 * half), np.float32)
    for g2 in range(SPREAD_STEPS // vs):
        for j in range(vs):
            for b in range(bsz):
                for h in range(RW_HEADS):
                    hp, hpar = h // 2, h % 2
                    p[g2 * LANES + j * group + b * RW_HEADS + h,
                      hpar * half + (hp * bsz + b) * SPREAD_STEPS + g2 * vs + j] = 1.0
    return jnp.asarray(np.concatenate([p] * 3, axis=0), BF16)


def _gather_k(ys, bsz):
    vs = LANES // (bsz * RW_HEADS)
    assert (RW_HEADS // 2) * bsz * SPREAD_STEPS == LANES, "the transposed tile must be 128 lanes wide"
    tg = ys[0].shape[0]
    n = len(ys)
    ngrp = GATHER_BLOCK // SPREAD_STEPS
    per = SPREAD_STEPS // vs

    def body(*refs):
        pm = refs[n][...]
        for y_ref, o_ref in zip(refs[:n], refs[n + 1:]):
            lhs = jnp.concatenate(
                [jnp.concatenate(_split3(jnp.concatenate([y_ref[per * m + g2] for g2 in range(per)], axis=1)), axis=1)
                 for m in range(ngrp)], axis=0)
            a = _dot(lhs, pm)
            for m in range(ngrp):
                am = a[64 * m:64 * (m + 1)]
                bt = jnp.concatenate([am[:, 0:LANES], am[:, LANES:2 * LANES]], axis=0).T
                for hp in range(RW_HEADS // 2):
                    for b in range(bsz):
                        at = (hp * bsz + b) * SPREAD_STEPS
                        o_ref[b, SPREAD_STEPS * m:SPREAD_STEPS * (m + 1), LANES * hp:LANES * (hp + 1)] = \
                            bt[at:at + SPREAD_STEPS]

    pm = _gather_matrix(bsz)
    return pl.pallas_call(
        body, name="wkv_gather", grid=(tg * vs // GATHER_BLOCK,),
        out_shape=[jax.ShapeDtypeStruct((bsz, tg * vs, RW), F32)] * n,
        in_specs=[pl.BlockSpec((GATHER_BLOCK // vs, 64, LANES), lambda i: (i, 0, 0))] * n + [_full(pm.shape)],
        out_specs=[pl.BlockSpec((bsz, GATHER_BLOCK, RW), lambda i: (0, i, 0))] * n,
        compiler_params=_cparams(("parallel",)),
    )(*ys, pm)


def _to_v(x):
    bsz, t, _ = x.shape
    vs, vq = _lane_split(bsz)
    return jnp.transpose(x.reshape(bsz, t, RW_HEADS, vq, vs), (1, 3, 4, 0, 2)).reshape(t, vq, LANES)


def _from_v(y, bsz):
    t = y.shape[0]
    vs, vq = _lane_split(bsz)
    return jnp.transpose(y.reshape(t, vq, vs, bsz, RW_HEADS), (3, 0, 4, 1, 2)).reshape(bsz, t, RW)


def _ksum(a):
    return jnp.sum(a, axis=0, keepdims=True)


def _fold(a, group):
    sh = LANES // 2
    while sh >= group:
        a = a + pltpu.roll(a, sh, 1)
        sh //= 2
    return a


def _lane_group(shape, group):
    return lax.broadcasted_iota(jnp.int32, shape, 1) // group


SPREAD_STEPS = 8
SPREAD_BLOCK = 64
GATHER_BLOCK = 128


def _spread_matrix(bsz):
    group = bsz * RW_HEADS
    vs = LANES // group
    rows = (RW_HEADS // 2) * bsz * SPREAD_STEPS
    q = np.zeros((2, rows, SPREAD_STEPS * LANES), np.float32)
    for hpar in range(2):
        for hp in range(RW_HEADS // 2):
            for b in range(bsz):
                for st in range(SPREAD_STEPS):
                    row = (hp * bsz + b) * SPREAD_STEPS + st
                    for s in range(vs):
                        q[hpar, row, st * LANES + s * group + b * RW_HEADS + 2 * hp + hpar] = 1.0
    return jnp.asarray(np.concatenate([q[0], q[1]] * 3, axis=0), BF16)


def _spread_k(xs):
    bsz, t, _ = xs[0].shape
    assert (RW_HEADS // 2) * bsz * SPREAD_STEPS == LANES, "the transposed tile must be 128 lanes wide"
    n = len(xs)
    ngrp = SPREAD_BLOCK // SPREAD_STEPS

    def body(*refs):
        qm = refs[n][...]
        for x_ref, o_ref in zip(refs[:n], refs[n + 1:]):
            cols = [[] for _ in range(6)]
            for m in range(ngrp):
                at = slice(SPREAD_STEPS * m, SPREAD_STEPS * (m + 1))
                x8 = jnp.concatenate([x_ref[b, at, LANES * hp:LANES * (hp + 1)]
                                      for hp in range(RW_HEADS // 2) for b in range(bsz)], axis=0)
                for pi, piece in enumerate(_split3(x8.T)):
                    cols[2 * pi].append(piece[0:64])
                    cols[2 * pi + 1].append(piece[64:128])
            lhs = jnp.concatenate([jnp.concatenate(c, axis=0) for c in cols], axis=1)
            y = _dot(lhs, qm)
            for m in range(ngrp):
                for st in range(SPREAD_STEPS):
                    o_ref[SPREAD_STEPS * m + st] = y[64 * m:64 * (m + 1), LANES * st:LANES * (st + 1)]

    qm = _spread_matrix(bsz)
    return pl.pallas_call(
        body, name="wkv_spread", grid=(t // SPREAD_BLOCK,),
        out_shape=[jax.ShapeDtypeStruct((t, 64, LANES), F32)] * n,
        in_specs=[pl.BlockSpec((bsz, SPREAD_BLOCK, RW), lambda i: (0, i, 0))] * n + [_full(qm.shape)],
        out_specs=[pl.BlockSpec((SPREAD_BLOCK, 64, LANES), lambda i: (i, 0, 0))] * n,
        compiler_params=_cparams(("parallel",)),
    )(*xs, qm)


def _wkv_fwd(r, w, kp, al, be, v):
    t, vq = v.shape[0], v.shape[1]

    def body(r_ref, w_ref, kp_ref, al_ref, be_ref, v_ref, y_ref, a_ref, u_ref, st_ref):
        @pl.when(pl.program_id(0) == 0)
        def _():
            st_ref[...] = jnp.zeros(st_ref.shape, F32)

        def step(tl, _):
            rv, wv, kv, av, bv = r_ref[tl], w_ref[tl], kp_ref[tl], al_ref[tl], be_ref[tl]
            vals = v_ref[tl]
            yrows, urows = [], []
            for q in range(vq):
                s = st_ref[q]
                u = _ksum(s * av)
                s = s * wv + bv * u + kv * vals[q:q + 1]
                st_ref[q] = s
                a_ref[tl, q] = s
                urows.append(u)
                yrows.append(_ksum(s * rv))
            y_ref[tl] = jnp.concatenate(yrows, axis=0)
            u_ref[tl] = jnp.concatenate(urows, axis=0)
            return 0

        lax.fori_loop(0, CH, step, 0)

    kspec = pl.BlockSpec((CH, 64, LANES), lambda i: (i, 0, 0))
    vspec = pl.BlockSpec((CH, vq, LANES), lambda i: (i, 0, 0))
    vsd = jax.ShapeDtypeStruct((t, vq, LANES), F32)
    return pl.pallas_call(
        body, name="wkv_fwd", grid=(t // CH,),
        out_shape=[vsd, jax.ShapeDtypeStruct((t, vq, 64, LANES), F32), vsd],
        in_specs=[kspec] * 5 + [vspec],
        out_specs=[vspec, pl.BlockSpec((CH, vq, 64, LANES), lambda i: (i, 0, 0, 0)), vspec],
        scratch_shapes=[pltpu.VMEM((vq, 64, LANES), F32)],
        compiler_params=_cparams(("arbitrary",)),
    )(r, w, kp, al, be, v)


def _wkv_bwd(r, w, kp, al, be, v, dy, states, u):
    t, vq = v.shape[0], v.shape[1]
    vs = 64 // vq
    group = LANES // vs
    n = t // CH
    ng = CH // vs

    def body(r_ref, w_ref, kp_ref, al_ref, be_ref, v_ref, dy_ref, u_ref, a_ref, ap_ref,
             dr_ref, dw_ref, dkp_ref, dal_ref, dbe_ref, dv_ref, ds_ref):
        @pl.when(pl.program_id(0) == 0)
        def _():
            ds_ref[...] = jnp.zeros(ds_ref.shape, F32)

        earliest = pl.program_id(0) == n - 1

        def reverse(i, _):
            g = ng - 1 - i
            grp = _lane_group((64, LANES), group)
            outs = None
            for j in reversed(range(vs)):
                tl = g * vs + j
                rv, wv, kv, av, bv = r_ref[tl], w_ref[tl], kp_ref[tl], al_ref[tl], be_ref[tl]
                vals, dys, us = v_ref[tl], dy_ref[tl], u_ref[tl]
                acc = None
                dvrows = []
                for q in range(vq):
                    if j > 0:
                        s_prev = a_ref[tl - 1, q]
                    else:
                        before = jnp.where(earliest, 0.0, ap_ref[0, q])
                        s_prev = jnp.where(g == 0, before, a_ref[jnp.maximum(tl - 1, 0), q])
                    dyq = dys[q:q + 1]
                    ds = ds_ref[q] + rv * dyq
                    c = _ksum(ds * bv)
                    dvrows.append(_ksum(ds * kv))
                    terms = (a_ref[tl, q] * dyq, ds * s_prev, ds * vals[q:q + 1], s_prev * c, ds * us[q:q + 1])
                    acc = terms if acc is None else tuple(a + b for a, b in zip(acc, terms))
                    ds_ref[q] = ds * wv + av * c
                dv_ref[tl] = jnp.concatenate(dvrows, axis=0)
                summed = [_fold(a, group) for a in acc]
                outs = summed if outs is None else [jnp.where(grp == j, f, o) for f, o in zip(summed, outs)]
            for ref, o in zip((dr_ref, dw_ref, dkp_ref, dal_ref, dbe_ref), outs):
                ref[g] = o
            return 0

        lax.fori_loop(0, ng, reverse, 0)

    kspec = pl.BlockSpec((CH, 64, LANES), lambda i: (n - 1 - i, 0, 0))
    gspec = pl.BlockSpec((ng, 64, LANES), lambda i: (n - 1 - i, 0, 0))
    vspec = pl.BlockSpec((CH, vq, LANES), lambda i: (n - 1 - i, 0, 0))
    ksd = jax.ShapeDtypeStruct((t // vs, 64, LANES), F32)
    return pl.pallas_call(
        body, name="wkv_bwd", grid=(n,),
        out_shape=[ksd] * 5 + [jax.ShapeDtypeStruct((t, vq, LANES), F32)],
        in_specs=[kspec] * 5 + [vspec, vspec, vspec,
                                pl.BlockSpec((CH, vq, 64, LANES), lambda i: (n - 1 - i, 0, 0, 0)),
                                pl.BlockSpec((1, vq, 64, LANES), lambda i: (jnp.maximum((n - 1 - i) * CH - 1, 0), 0, 0, 0))],
        out_specs=[gspec] * 5 + [vspec],
        scratch_shapes=[pltpu.VMEM((vq, 64, LANES), F32)],
        compiler_params=_cparams(("arbitrary",)),
    )(r, w, kp, al, be, v, dy, u, states, states)


def _post(x, tgt, pp, o, yw, r, kp, v, ln_g, ln_b, r_k, wo, wot, gpost, bo):
    bsz, t, _ = x.shape
    tt = TT_VPU
    nt = t // tt

    def body(x_ref, tgt_ref, z_ref, o_ref, yw_ref, r_ref, kp_ref, v_ref, lng_ref, lnb_ref, rk_ref, wo_ref, wot_ref,
             gpost_ref, bo_ref,
             dh_ref, dz_ref, dym_ref, dyw_ref, dbon_ref, loss_ref, dwo_ref, dgpost_ref, dlng_ref, dlnb_ref, drk_ref):
        first = (pl.program_id(0) == 0) & (pl.program_id(1) == 0)

        @pl.when(first)
        def _():
            for ref in (loss_ref, dwo_ref, dgpost_ref, dlng_ref, dlnb_ref, drk_ref):
                ref[...] = jnp.zeros(ref.shape, F32)

        bo_m = bo_ref[...]
        seg = lambda a: _seg(a, bo_m)
        rowsum = lambda a: jnp.sum(a, axis=0, keepdims=True)
        ywv, rv, kpv, vv = yw_ref[0], r_ref[0], kp_ref[0], v_ref[0]
        ln_g, r_k = lng_ref[...], rk_ref[...]
        mean = seg(ywv) * (1.0 / 64)
        yc = ywv - mean
        rstd = lax.rsqrt(seg(yc * yc) * (1.0 / 64) + GN_EPS)
        yhat = yc * rstd
        sb = seg(rv * kpv * r_k)
        y_rw = yhat * ln_g + lnb_ref[...] + sb * vv
        z = z_ref[0]
        sig = _sigmoid(z)
        sz = z * sig
        ycat = jnp.concatenate([o_ref[0], y_rw], axis=1)
        ycg = (ycat * sz).astype(BF16)
        out = _dot(ycg, wo_ref[...])
        hn, nx, rstd_o = _rms(out, gpost_ref[...], D)
        err = x_ref[0] + hn - tgt_ref[0]
        loss_ref[...] += jnp.sum(err * err) * (0.5 / D)
        dh = err * (1.0 / D)
        dh_ref[0] = dh
        dout, dgp = _rms_bwd(dh, nx, rstd_o, gpost_ref[...], D)
        dgpost_ref[...] += dgp
        doutb = dout.astype(BF16)
        dwo_ref[...] += _dot_tn(ycg, doutb)
        dycg = _dot(doutb, wot_ref[...])
        dz_ref[0] = dycg * ycat * (sig * (1.0 + z * (1.0 - sig)))
        dycat = dycg * sz
        dym_ref[0] = dycat[:, 0:512]
        dy_rw = dycat[:, 512:1024]
        dlnb_ref[...] += rowsum(dy_rw)
        dlng_ref[...] += rowsum(dy_rw * yhat)
        dyhat = dy_rw * ln_g
        dyw_ref[0] = rstd * (dyhat - seg(dyhat) * (1.0 / 64) - yhat * (seg(dyhat * yhat) * (1.0 / 64)))
        dsb = seg(dy_rw * vv)
        drk_ref[...] += rowsum(dsb * rv * kpv)
        dbon_ref[0, :, 0:512] = dsb * kpv * r_k
        dbon_ref[0, :, 512:1024] = dsb * rv * r_k
        dbon_ref[0, :, 1024:1536] = dy_rw * sb

    tok = lambda c: pl.BlockSpec((1, tt, c), lambda b, i: (b, i, 0))
    full = lambda a: _full(a.shape)
    ins = (x, tgt, pp, o, yw, r, kp, v, ln_g, ln_b, r_k, wo, wot, gpost, bo)
    in_specs = [tok(D), tok(D), tok(1024)] + [tok(512)] * 5 + [full(a) for a in ins[8:]]
    sd = lambda c: jax.ShapeDtypeStruct((bsz, t, c), F32)
    vec = lambda c: jax.ShapeDtypeStruct((1, c), F32)
    out_shape = [sd(D), sd(1024), sd(512), sd(512), sd(1536), jax.ShapeDtypeStruct((8, LANES), F32),
                 jax.ShapeDtypeStruct((1024, 1024), F32), vec(D), vec(512), vec(512), vec(512)]
    out_specs = [tok(D), tok(1024), tok(512), tok(512), tok(1536), _resident((8, LANES)), _resident((1024, 1024)),
                 _resident((1, D)), _resident((1, 512)), _resident((1, 512)), _resident((1, 512))]
    return pl.pallas_call(
        body, name="post", grid=(bsz, nt), out_shape=out_shape, in_specs=in_specs, out_specs=out_specs,
        compiler_params=_cparams(("arbitrary", "arbitrary")),
    )(*ins)


def _pre_bwd_a(pp, pos, invf, cqkv_w, mu, w0, w2p, w2pt, a0, a2p, a2pt, k_k, k_a, bo,
               dq, dk, dva, dwkv, dbon):
    gq, wuqt, gkv, wukvt = cqkv_w
    bsz, t, _ = pp.shape
    tt = TT_VPU
    nt = t // tt
    dr_w, dw_w, dkp_w, dv_w, dal_w, dbe_w = dwkv

    def body(pp_ref, pos_ref, invf_ref, gq_ref, wuqt_ref, gkv_ref, wukvt_ref, mu_ref, w0_ref, w2p_ref, w2pt_ref,
             a0_ref, a2p_ref, a2pt_ref, kk_ref, ka_ref, bo_ref, dq_ref, dk_ref, dva_ref,
             dr_ref, dw_ref, dkp_ref, dv_ref, dal_ref, dbe_ref, dbon_ref,
             da_ref, dwuq_ref, dwukv_ref, dw2p_ref, da2p_ref, dgq_ref, dgkv_ref, dmu_ref, dw0_ref, da0_ref,
             dkk_ref, dka_ref, carry):
        i = pl.program_id(1)
        first = (pl.program_id(0) == 0) & (i == 0)

        @pl.when(first)
        def _():
            for ref in (dwuq_ref, dwukv_ref, dw2p_ref, da2p_ref, dgq_ref, dgkv_ref, dmu_ref, dw0_ref, da0_ref,
                        dkk_ref, dka_ref):
                ref[...] = jnp.zeros(ref.shape, F32)

        bo_m = bo_ref[...]
        rowsum = lambda a: jnp.sum(a, axis=0, keepdims=True)
        prw = pp_ref[0, :, RW0:DP]

        @pl.when(i == 0)
        def _():
            carry[...] = jnp.zeros(carry.shape, F32)

        ps, sh = _shift_mix(prw, carry[7:8, :], mu_ref[...])
        carry[...] = prw[tt - 8:tt, :]
        k_k, k_a = kk_ref[...], ka_ref[...]
        g = _rw_gates(ps, w0_ref[...], w2p_ref[...], a0_ref[...], a2p_ref[...], k_k, k_a, bo_m)
        a, kk, k = g["a"], g["kk"], g["k"]
        dr = dr_ref[0] + dbon_ref[0, :, 0:512]
        dkp = dkp_ref[0] + dbon_ref[0, :, 512:1024]
        dv = dv_ref[0] + dbon_ref[0, :, 1024:1536]
        dbe = dbe_ref[0]
        dkk = dbe * a - dal_ref[0]
        da = dbe * kk + dkp * k * k_a
        dka_ref[...] += rowsum(dkp * k * (a - 1.0))
        dm = (dkk - kk * _seg(dkk * kk, bo_m)) / g["nrm"]
        dkk_ref[...] += rowsum(dm * k)
        dk_tot = dkp * (1.0 + (a - 1.0) * k_a) + dm * k_k
        dapre = da * a * (1.0 - a)
        da0_ref[...] += rowsum(dapre)
        dapb = dapre.astype(BF16)
        da2p_ref[...] += _dot_tn(g["misc"].astype(BF16), dapb)
        dwpre = dw_ref[0] * g["w"] * (-g["e"]) * _sigmoid(-g["wpre"])
        dw0_ref[...] += rowsum(dwpre)
        dwpb = dwpre.astype(BF16)
        th = g["th"]
        dw2p_ref[...] += _dot_tn(th.astype(BF16), dwpb)
        dmisc = _dot(dapb, a2pt_ref[...]) + _dot(dwpb, w2pt_ref[...]) * (1.0 - th * th)
        ang = pos_ref[0] * invf_ref[...]
        cs, sn = jnp.cos(ang), jnp.sin(ang)
        unrope = lambda gr: gr * cs - _rot(gr * sn)
        lane = lax.broadcasted_iota(jnp.int32, cs.shape, 1)
        dkr = dk_ref[0, :, 128:256]
        for h in range(1, HEADS):
            dkr = dkr + dk_ref[0, :, 256 * h + 128:256 * h + 256]
        dkr = jnp.where(lane < 64, unrope(dkr), 0.0)
        dmisc = dmisc + jnp.concatenate([dkr, jnp.zeros_like(dkr)], axis=1)
        dqp = jnp.concatenate(
            [blk for h in range(HEADS)
             for blk in (dq_ref[0, :, 256 * h:256 * h + 128], unrope(dq_ref[0, :, 256 * h + 128:256 * h + 256]))],
            axis=1).astype(BF16)
        dkvp = jnp.concatenate([dk_ref[0, :, 256 * h:256 * h + 128] for h in range(HEADS)] + [dva_ref[0]],
                               axis=1).astype(BF16)
        cqn, cq_nx, cq_rstd = _rms(pp_ref[0, :, CQ0:CQ0 + 256], gq_ref[...], 256)
        ckvn, ckv_nx, ckv_rstd = _rms(pp_ref[0, :, CKV0:CKV0 + 128], gkv_ref[...], 128)
        dwuq_ref[...] += _dot_tn(cqn.astype(BF16), dqp)
        dwukv_ref[...] += _dot_tn(ckvn.astype(BF16), dkvp)
        dcq, dgq = _rms_bwd(_dot(dqp, wuqt_ref[...]), cq_nx, cq_rstd, gq_ref[...], 256)
        dckv, dgkv = _rms_bwd(_dot(dkvp, wukvt_ref[...]), ckv_nx, ckv_rstd, gkv_ref[...], 128)
        dgq_ref[...] += dgq
        dgkv_ref[...] += dgkv
        dps = jnp.concatenate([dr, dk_tot, dv, dmisc], axis=1)
        dmu_ref[...] += rowsum(dps * (sh - prw))
        da_ref[0, :, 0:256] = dcq
        da_ref[0, :, 256:384] = dckv
        da_ref[0, :, 384:384 + NRW] = dps

    tok = lambda c: pl.BlockSpec((1, tt, c), lambda b, i: (b, i, 0))
    full = lambda a: _full(a.shape)
    ins = (pp, pos, invf, gq, wuqt, gkv, wukvt, mu, w0, w2p, w2pt, a0, a2p, a2pt, k_k, k_a, bo,
           dq, dk, dva, dr_w, dw_w, dkp_w, dv_w, dal_w, dbe_w, dbon)
    in_specs = ([tok(DP), tok(1)] + [full(a) for a in ins[2:17]] + [tok(1024), tok(1024), tok(512)]
                + [tok(512)] * 6 + [tok(1536)])
    shp = lambda *s: jax.ShapeDtypeStruct(s, F32)
    out_shape = [shp(bsz, t, 384 + NRW), shp(256, 1024), shp(128, 1024), shp(256, 512), shp(256, 512),
                 shp(1, 256), shp(1, 128), shp(1, NRW), shp(1, 512), shp(1, 512), shp(1, 512), shp(1, 512)]
    out_specs = [tok(384 + NRW)] + [_resident(s.shape) for s in out_shape[1:]]
    return pl.pallas_call(
        body, name="pre_bwd_a", grid=(bsz, nt), out_shape=out_shape, in_specs=in_specs, out_specs=out_specs,
        scratch_shapes=[pltpu.VMEM((8, NRW), F32)],
        compiler_params=_cparams(("arbitrary", "arbitrary")),
    )(*ins)


def _pre_bwd_b(x, dh, dz, da, mu, wpt, gpre):
    bsz, t, _ = x.shape
    nt = t // TT
    nblk = t // 8

    def body(x_ref, dh_ref, dz_ref, da_ref, nxt_ref, mu_ref, wpt_ref, gpre_ref, gx_ref, dp_ref, dgpre_ref):
        i = pl.program_id(1)
        first = (pl.program_id(0) == 0) & (i == 0)

        @pl.when(first)
        def _():
            dgpre_ref[...] = jnp.zeros(dgpre_ref.shape, F32)

        mu_v = mu_ref[...]
        dps = da_ref[0, :, 384:384 + NRW]
        nxt = jnp.where(i < nt - 1, nxt_ref[0, 0:1, 384:384 + NRW], 0.0)
        row = lax.broadcasted_iota(jnp.int32, dps.shape, 0)
        up = jnp.where(row == TT - 1, nxt, pltpu.roll(dps, TT - 1, 0))
        dprw = dps * (1.0 - mu_v) + up * mu_v
        dp = jnp.concatenate([dz_ref[0], da_ref[0, :, 0:384], dprw], axis=1).astype(BF16)
        dp_ref[0] = dp
        du = _dot(dp, wpt_ref[...])
        _, nx, rstd = _rms(x_ref[0], gpre_ref[...], D)
        dx, dg = _rms_bwd(du, nx, rstd, gpre_ref[...], D)
        dgpre_ref[...] += dg
        gx_ref[0] = dh_ref[0] + dx

    tok = lambda c: pl.BlockSpec((1, TT, c), lambda b, i: (b, i, 0))
    nxt_spec = pl.BlockSpec((1, 8, 384 + NRW), lambda b, i: (b, jnp.minimum((i + 1) * (TT // 8), nblk - 1), 0))
    ins = (x, dh, dz, da, da, mu, wpt, gpre)
    return pl.pallas_call(
        body, name="pre_bwd_b", grid=(bsz, nt),
        out_shape=[jax.ShapeDtypeStruct((bsz, t, D), F32), jax.ShapeDtypeStruct((bsz, t, DP), BF16),
                   jax.ShapeDtypeStruct((1, D), F32)],
        in_specs=[tok(D), tok(D), tok(1024), tok(384 + NRW), nxt_spec, _full(mu.shape), _full(wpt.shape),
                  _full(gpre.shape)],
        out_specs=[tok(D), tok(DP), _resident((1, D))],
        compiler_params=_cparams(("arbitrary", "arbitrary")),
    )(*ins)


def _tn_matmul(a, b, bn, name, bk=512):
    kdim, m = a.shape
    _, n = b.shape
    nk = kdim // bk

    def body(a_ref, b_ref, o_ref):
        @pl.when(pl.program_id(1) == 0)
        def _():
            o_ref[...] = jnp.zeros(o_ref.shape, F32)

        o_ref[...] += _dot_tn(a_ref[...], b_ref[...])

    return pl.pallas_call(
        body, name=name, grid=(n // bn, nk),
        out_shape=jax.ShapeDtypeStruct((m, n), F32),
        in_specs=[pl.BlockSpec((bk, m), lambda j, kk: (kk, 0)), pl.BlockSpec((bk, bn), lambda j, kk: (kk, j))],
        out_specs=pl.BlockSpec((m, bn), lambda j, kk: (0, j)),
        compiler_params=_cparams(("parallel", "arbitrary")),
    )(a, b)


SHARDED = ("w_in", "mla_w_uq", "mla_w_ukv", "rw_w2", "rw_a2", "w_out")
SMALL = ("norm_pre_g", "mla_q_norm_g", "mla_kv_norm_g", "rw_mu", "rw_w0", "rw_a0", "rw_k_k", "rw_k_a", "rw_r_k",
         "rw_ln_g", "rw_ln_b", "norm_post_g")
WEIGHTS = ("norm_pre_g", "w_in", "mla_q_norm_g", "mla_w_uq", "mla_kv_norm_g", "mla_w_ukv", "rw_mu", "rw_w0", "rw_w2",
           "rw_a0", "rw_a2", "rw_k_k", "rw_k_a", "rw_r_k", "rw_ln_g", "rw_ln_b", "w_out", "norm_post_g")


def _pack_small(d):
    flat = jnp.concatenate([d[n].reshape(1, -1) for n in SMALL], axis=1)
    return jnp.pad(flat, ((0, 0), (0, SMALL_ROWS * LANES - flat.shape[1]))).reshape(SMALL_ROWS, LANES)


def _unpack_small(packed, like):
    flat = packed.reshape(1, -1)
    out, at = {}, 0
    for n in SMALL:
        size = int(np.prod(like[n].shape))
        out[n] = flat[:, at:at + size].reshape(like[n].shape)
        at += size
    return out


def _unpack_shard(packed, like):
    out, at = {}, 0
    for n, rows in zip(SHARDED, PACK_ROWS):
        out[n] = packed[at:at + rows].reshape(like[n].shape)
        at += rows
    return out


def _constants():
    bo = np.kron(np.eye(2, dtype=np.float32), np.ones((64, 64), np.float32))
    inv = ROPE_THETA ** (-np.arange(0, 64, 2, dtype=np.float32) / 64)
    invf = np.concatenate([inv, inv, np.zeros(64, np.float32)]).astype(np.float32)[None, :]
    return jnp.asarray(bo, BF16), jnp.asarray(invf)


def kernel(x, positions, norm_pre_g, w_in, mla_q_norm_g, mla_w_uq, mla_kv_norm_g, mla_w_ukv, rw_mu, rw_w0, rw_w2, rw_a0, rw_a2, rw_k_k, rw_k_a, rw_r_k, rw_ln_g, rw_ln_b, w_out, norm_post_g, loss_target, m_norm_pre_g, m_w_in, m_mla_q_norm_g, m_mla_w_uq, m_mla_kv_norm_g, m_mla_w_ukv, m_rw_mu, m_rw_w0, m_rw_w2, m_rw_a0, m_rw_a2, m_rw_k_k, m_rw_k_a, m_rw_r_k, m_rw_ln_g, m_rw_ln_b, m_w_out, m_norm_post_g, v_norm_pre_g, v_w_in, v_mla_q_norm_g, v_mla_w_uq, v_mla_kv_norm_g, v_mla_w_ukv, v_rw_mu, v_rw_w0, v_rw_w2, v_rw_a0, v_rw_a2, v_rw_k_k, v_rw_k_a, v_rw_r_k, v_rw_ln_g, v_rw_ln_b, v_w_out, v_norm_post_g):
    wts = dict(norm_pre_g=norm_pre_g, w_in=w_in, mla_q_norm_g=mla_q_norm_g, mla_w_uq=mla_w_uq,
               mla_kv_norm_g=mla_kv_norm_g, mla_w_ukv=mla_w_ukv, rw_mu=rw_mu, rw_w0=rw_w0, rw_w2=rw_w2, rw_a0=rw_a0,
               rw_a2=rw_a2, rw_k_k=rw_k_k, rw_k_a=rw_k_a, rw_r_k=rw_r_k, rw_ln_g=rw_ln_g, rw_ln_b=rw_ln_b, w_out=w_out,
               norm_post_g=norm_post_g)
    mom_m = dict(norm_pre_g=m_norm_pre_g, w_in=m_w_in, mla_q_norm_g=m_mla_q_norm_g, mla_w_uq=m_mla_w_uq,
                 mla_kv_norm_g=m_mla_kv_norm_g, mla_w_ukv=m_mla_w_ukv, rw_mu=m_rw_mu, rw_w0=m_rw_w0, rw_w2=m_rw_w2,
                 rw_a0=m_rw_a0, rw_a2=m_rw_a2, rw_k_k=m_rw_k_k, rw_k_a=m_rw_k_a, rw_r_k=m_rw_r_k, rw_ln_g=m_rw_ln_g,
                 rw_ln_b=m_rw_ln_b, w_out=m_w_out, norm_post_g=m_norm_post_g)
    mom_v = dict(norm_pre_g=v_norm_pre_g, w_in=v_w_in, mla_q_norm_g=v_mla_q_norm_g, mla_w_uq=v_mla_w_uq,
                 mla_kv_norm_g=v_mla_kv_norm_g, mla_w_ukv=v_mla_w_ukv, rw_mu=v_rw_mu, rw_w0=v_rw_w0, rw_w2=v_rw_w2,
                 rw_a0=v_rw_a0, rw_a2=v_rw_a2, rw_k_k=v_rw_k_k, rw_k_a=v_rw_k_a, rw_r_k=v_rw_r_k, rw_ln_g=v_rw_ln_g,
                 rw_ln_b=v_rw_ln_b, w_out=v_w_out, norm_post_g=v_norm_post_g)
    bsz, t, _ = x.shape
    bo, invf = _constants()

    g_in, g_uq, g_ukv, g_w2, g_a2, g_out = _ag_weights([wts[n][0] for n in SHARDED])
    w_in_f = jnp.transpose(g_in, (1, 0, 2)).reshape(D, D_IN)
    wp = jnp.concatenate([w_in_f[:, 2112:3136], w_in_f[:, 0:384], w_in_f[:, 448:1984], w_in_f[:, 384:448],
                          w_in_f[:, 1984:2112], jnp.zeros((D, 64), BF16)], axis=1)
    wuq = jnp.pad(jnp.transpose(g_uq, (1, 0, 2)).reshape(256, HEADS, 192), ((0, 0), (0, 0), (0, 64))).reshape(256, 1024)
    wukv = jnp.transpose(jnp.transpose(g_ukv, (1, 0, 2)).reshape(128, HEADS, 2, 128), (0, 2, 1, 3)).reshape(128, 1024)
    w2 = jnp.transpose(g_w2, (1, 0, 2)).reshape(64, RW)
    a2 = jnp.transpose(g_a2, (1, 0, 2)).reshape(64, RW)
    w2p = jnp.pad(w2, ((64, 128), (0, 0)))
    a2p = jnp.pad(a2, ((128, 64), (0, 0)))
    wo = g_out.reshape(D, D)
    mu = jnp.concatenate([rw_mu[:, 0:1536], jnp.zeros((1, 64), F32), rw_mu[:, 1536:1664], jnp.zeros((1, 64), F32)],
                         axis=1)
    r_k = rw_r_k.reshape(1, RW)
    pos = positions.astype(F32)[:, :, None]

    (u, pp, q_att, k_att, v_att, r, w, kp, v, al, be) = _pre_fwd(
        x, pos, invf, norm_pre_g, wp, mla_q_norm_g, wuq, mla_kv_norm_g, wukv, mu, rw_w0, w2p, rw_a0, a2p, rw_k_k,
        rw_k_a, bo)
    o, lse = _attn_fwd(q_att, k_att, v_att)
    rw_k = _spread_k([r, w, kp, al, be])
    v_v = _to_v(v)
    yw_v, states, u_v = _wkv_fwd(*rw_k, v_v)
    yw = _from_v(yw_v, bsz)

    (dh, dz, dym, dyw, dbon, loss_acc, d_wo, d_gpost, d_lng, d_lnb, d_rk) = _post(
        x, loss_target, pp, o, yw, r, kp, v, rw_ln_g, rw_ln_b, r_k, wo, wo.T, norm_post_g, bo)

    d_k = _wkv_bwd(*rw_k, v_v, _to_v(dyw), states, u_v)
    dr_w, dw_w, dkp_w, dal_w, dbe_w = _gather_k(d_k[:5], bsz)
    dwkv = (dr_w, dw_w, dkp_w, _from_v(d_k[5], bsz), dal_w, dbe_w)
    dq, dk, dva = _attn_bwd(q_att, k_att, v_att, o, lse, dym)

    (da, d_wuq, d_wukv, d_w2p, d_a2p, d_gq, d_gkv, d_mu, d_w0, d_a0, d_kk, d_ka) = _pre_bwd_a(
        pp, pos, invf, (mla_q_norm_g, wuq.T, mla_kv_norm_g, wukv.T), mu, rw_w0, w2p, w2p.T, rw_a0, a2p, a2p.T,
        rw_k_k, rw_k_a, bo, dq, dk, dva, dwkv, dbon)
    grad_x, dpb, d_gpre = _pre_bwd_b(x, dh, dz, da, mu, wp.T, norm_pre_g)
    d_wp = _tn_matmul(u.reshape(bsz * t, D), dpb.reshape(bsz * t, DP), DP, "dw_in", bk=1024)

    full_g = {
        "w_in": jnp.concatenate([d_wp[:, 1024:1408], d_wp[:, 2944:3008], d_wp[:, 1408:2944], d_wp[:, 3008:3136],
                                 d_wp[:, 0:1024]], axis=1),
        "mla_w_uq": d_wuq.reshape(256, HEADS, 256)[:, :, :192].reshape(256, 768),
        "mla_w_ukv": jnp.transpose(d_wukv.reshape(128, 2, HEADS, 128), (0, 2, 1, 3)).reshape(128, 1024),
        "rw_w2": d_w2p[64:128],
        "rw_a2": d_a2p[128:192],
        "w_out": d_wo,
    }
    small_g = {
        "norm_pre_g": d_gpre, "mla_q_norm_g": d_gq, "mla_kv_norm_g": d_gkv,
        "rw_mu": jnp.concatenate([d_mu[:, 0:1536], d_mu[:, 1600:1728]], axis=1),
        "rw_w0": d_w0, "rw_a0": d_a0, "rw_k_k": d_kk, "rw_k_a": d_ka, "rw_r_k": d_rk, "rw_ln_g": d_lng,
        "rw_ln_b": d_lnb, "norm_post_g": d_gpost,
    }

    def by_shard(name, g):
        if name == "w_out":
            return g.reshape(N_SHARD, -1, LANES)
        rows, cols = g.shape
        return jnp.transpose(g.reshape(rows, N_SHARD, cols // N_SHARD), (1, 0, 2)).reshape(N_SHARD, -1, LANES)

    packed = jnp.concatenate([by_shard(n, full_g[n]) for n in SHARDED], axis=1)
    pair_sum, pair_sum_b = _rs_pairs(packed.reshape(N_SHARD, 2, HALF, LANES))
    g_shard = _rs_chips(pair_sum, pair_sum_b).reshape(PACK_TOTAL, LANES)

    g_small = _small_allreduce(jnp.concatenate([_pack_small(small_g)[:SMALL_USED], loss_acc[0:SMALL_ROWS - SMALL_USED]]))
    loss = g_small[SMALL_USED, 0]

    g_sharded = _unpack_shard(g_shard, {n: wts[n][0] for n in SHARDED})
    sh = _adamw([wts[n][0] for n in SHARDED], [g_sharded[n] for n in SHARDED], [mom_m[n][0] for n in SHARDED],
                [mom_v[n][0] for n in SHARDED], "adamw_sharded")
    sm = _adamw([_pack_small(wts)], [g_small], [_pack_small(mom_m)], [_pack_small(mom_v)], "adamw_small")

    def outputs(sharded, small):
        out = {n: a[None] for n, a in zip(SHARDED, sharded)}
        out.update(_unpack_small(small, wts))
        return out

    grads = outputs([g_sharded[n] for n in SHARDED], g_small)
    deltas, new_m, new_v = (outputs(sh[k], sm[k][0]) for k in range(3))
    return (loss, grad_x, *[grads[n] for n in WEIGHTS], *[deltas[n] for n in WEIGHTS],
            *[new_m[n] for n in WEIGHTS], *[new_v[n] for n in WEIGHTS])
```

```python
import numpy as np
import jax
import jax.numpy as jnp
from jax import lax
from jax.experimental import pallas as pl
from jax.experimental.pallas import tpu as pltpu

F32, BF16 = jnp.float32, jnp.bfloat16
MESH = pl.DeviceIdType.MESH

D = 1024
HEADS = 4
RW = 512
NORM_EPS = 1e-6
GN_EPS = 64e-5
ROPE_THETA = 10000.0
SCALE = (128 + 64) ** -0.5
D_IN = 3136
LR, B1, B2, ADAM_EPS, WD, STEP = 0.001, 0.9, 0.999, 1e-08, 0.01, 10

Z0, CQ0, CKV0, RW0, DP = 0, 1024, 1280, 1408, 3200
NRW = DP - RW0

LANES = 128
SUBLANES = 8
VMEM_LIMIT = 56 * 1024 * 1024

TT = 512
TT_VPU = 256
TQ = 512

N_SHARD = 4
PACK_ROWS = (1024 * 784 // 128, 256 * 192 // 128, 128 * 256 // 128, 64, 64, 256 * 1024 // 128)
PACK_TOTAL = sum(PACK_ROWS)
HALF = PACK_TOTAL // 2
SMALL_ROWS = 64
SMALL_USED = 60


def _cparams(sem=None):
    return pltpu.CompilerParams(dimension_semantics=sem, vmem_limit_bytes=VMEM_LIMIT)


def _full(shape):
    n = len(shape)
    return pl.BlockSpec(shape, lambda *_: (0,) * n, pipeline_mode=pl.Buffered(1))


def _resident(shape):
    n = len(shape)
    return pl.BlockSpec(shape, lambda *_: (0,) * n)


def _dot(a, b):
    return jnp.dot(a, b, preferred_element_type=F32)


def _dot_nt(a, b):
    return lax.dot_general(a, b, (((1,), (1,)), ((), ())), preferred_element_type=F32)


def _dot_tn(a, b):
    return lax.dot_general(a, b, (((0,), (0,)), ((), ())), preferred_element_type=F32)


def _split3(x):
    hi = x.astype(BF16)
    r1 = x - hi.astype(F32)
    mid = r1.astype(BF16)
    lo = (r1 - mid.astype(F32)).astype(BF16)
    return hi, mid, lo


def _seg(x, bo):
    rows, nblk = x.shape[0], x.shape[1] // LANES
    pieces = [p for i in range(nblk) for p in _split3(x[:, LANES * i:LANES * (i + 1)])]
    res = _dot(jnp.concatenate(pieces, axis=0), bo)
    parts = [res[(3 * i) * rows:(3 * i + 1) * rows] + res[(3 * i + 1) * rows:(3 * i + 2) * rows]
             + res[(3 * i + 2) * rows:(3 * i + 3) * rows] for i in range(nblk)]
    return parts[0] if nblk == 1 else jnp.concatenate(parts, axis=1)


def _rms(x, g, n):
    rstd = lax.rsqrt(jnp.sum(x * x, axis=-1, keepdims=True) * (1.0 / n) + NORM_EPS)
    nx = x * rstd
    return nx * g, nx, rstd


def _rms_bwd(dy, nx, rstd, g, n):
    dn = dy * g
    dx = rstd * (dn - nx * (jnp.sum(dn * nx, axis=-1, keepdims=True) * (1.0 / n)))
    return dx, jnp.sum(dy * nx, axis=0, keepdims=True)


def _rot(x):
    lane = lax.broadcasted_iota(jnp.int32, x.shape, 1)
    return jnp.where((lane % 64) < 32, -pltpu.roll(x, x.shape[1] - 32, 1), pltpu.roll(x, 32, 1))


def _sigmoid(x):
    return 1.0 / (1.0 + jnp.exp(-x))


def _softplus(x):
    return jnp.maximum(x, 0.0) + jnp.log(1.0 + jnp.exp(-jnp.abs(x)))


def _rw_gates(ps, w0, w2p, a0, a2p, k_k, k_a, bo):
    r, k, v, misc = ps[:, 0:512], ps[:, 512:1024], ps[:, 1024:1536], ps[:, 1536:NRW]
    th = jnp.tanh(misc)
    wpre = w0 + _dot(th.astype(BF16), w2p)
    e = jnp.exp(-_softplus(-wpre) - 0.5)
    w = jnp.exp(-e)
    a = _sigmoid(a0 + _dot(misc.astype(BF16), a2p))
    m = k * k_k
    nrm = jnp.maximum(jnp.sqrt(_seg(m * m, bo)), 1e-12)
    kk = m / nrm
    kp = k * (1.0 + (a - 1.0) * k_a)
    return dict(r=r, k=k, v=v, misc=misc, th=th, wpre=wpre, e=e, w=w, a=a, nrm=nrm, kk=kk, kp=kp)


def _shift_mix(prw, prev_row, mu):
    row = lax.broadcasted_iota(jnp.int32, prw.shape, 0)
    sh = jnp.where(row == 0, prev_row, pltpu.roll(prw, 1, 0))
    return prw + (sh - prw) * mu, sh


def _ag_weights(shards):
    n = len(shards)

    def body(*refs):
        ins, outs = refs[:n], refs[n:2 * n]
        ici_send, ici_recv, d2d_send, d2d_recv = refs[2 * n:2 * n + 4]
        x, y, c = lax.axis_index("x"), lax.axis_index("y"), lax.axis_index("c")
        mine = 2 * x + y
        for w in range(n):
            outs[w][mine] = ins[w][...].astype(BF16)
        flips = ((1, 0), (0, 1), (1, 1))

        def half(w, shard, cc):
            rows = outs[w].shape[1] // 2
            return outs[w].at[shard, pl.ds(pl.multiple_of(cc * rows, 16), rows)]

        def ici(w, k, shard):
            fx, fy = flips[k]
            return pltpu.make_async_remote_copy(
                src_ref=half(w, shard, c), dst_ref=half(w, shard, c),
                send_sem=ici_send.at[w * 3 + k], recv_sem=ici_recv.at[w * 3 + k],
                device_id=(x ^ fx, y ^ fy, c), device_id_type=MESH)

        def d2d(w, k, cc):
            fx, fy = flips[k]
            theirs = 2 * (x ^ fx) + (y ^ fy)
            return pltpu.make_async_remote_copy(
                src_ref=half(w, theirs, cc), dst_ref=half(w, theirs, cc),
                send_sem=d2d_send.at[w * 3 + k], recv_sem=d2d_recv.at[w * 3 + k],
                device_id=(x, y, 1 - c), device_id_type=MESH)

        for w in range(n):
            for k in range(3):
                ici(w, k, mine).start()
        for w in range(n):
            for k in range(3):
                fx, fy = flips[k]
                ici(w, k, 2 * (x ^ fx) + (y ^ fy)).wait_recv()
                d2d(w, k, c).start()
        for w in range(n):
            for k in range(3):
                d2d(w, k, 1 - c).wait_recv()
        for w in range(n):
            for k in range(3):
                ici(w, k, mine).wait_send()
                d2d(w, k, c).wait_send()

    vm = pl.BlockSpec(memory_space=pltpu.VMEM)
    return pl.pallas_call(
        body, name="ag_weights",
        out_shape=[jax.ShapeDtypeStruct((N_SHARD,) + s.shape, BF16) for s in shards],
        in_specs=[vm] * n, out_specs=[vm] * n,
        scratch_shapes=[pltpu.SemaphoreType.DMA((3 * n,))] * 4,
        compiler_params=pltpu.CompilerParams(vmem_limit_bytes=VMEM_LIMIT),
    )(*shards)


def _rs_pairs(halves):
    def body(h_ref, sum_ref, sumb_ref, recv, send_sem, recv_sem):
        x, y, c = lax.axis_index("x"), lax.axis_index("y"), lax.axis_index("c")
        cps = [pltpu.make_async_remote_copy(src_ref=h_ref.at[s, 1 - c], dst_ref=recv.at[s], send_sem=send_sem.at[s],
                                            recv_sem=recv_sem.at[s], device_id=(x, y, 1 - c), device_id_type=MESH)
               for s in range(N_SHARD)]
        for cp in cps:
            cp.start()
        for s, cp in enumerate(cps):
            cp.wait_recv()
            acc = h_ref[s, c] + recv[s]
            sum_ref[s] = acc
            sumb_ref[s] = acc.astype(BF16)
        for cp in cps:
            cp.wait_send()

    vm = pl.BlockSpec(memory_space=pltpu.VMEM)
    shape = (N_SHARD,) + halves.shape[2:]
    return pl.pallas_call(
        body, name="rs_pairs",
        out_shape=[jax.ShapeDtypeStruct(shape, F32), jax.ShapeDtypeStruct(shape, BF16)],
        in_specs=[vm], out_specs=[vm, vm],
        scratch_shapes=[pltpu.VMEM(shape, F32), pltpu.SemaphoreType.DMA((N_SHARD,)),
                        pltpu.SemaphoreType.DMA((N_SHARD,))],
        compiler_params=pltpu.CompilerParams(vmem_limit_bytes=VMEM_LIMIT),
    )(halves)


def _rs_chips(part_f32, part_bf16):
    def body(own_ref, src_ref, out_ref, recv, ici_send, ici_recv, d2d_send, d2d_recv):
        x, y, c = lax.axis_index("x"), lax.axis_index("y"), lax.axis_index("c")
        mine = 2 * x + y
        flips = ((1, 0), (0, 1), (1, 1))
        cps = []
        for k, (fx, fy) in enumerate(flips):
            theirs = 2 * (x ^ fx) + (y ^ fy)
            cps.append(pltpu.make_async_remote_copy(
                src_ref=src_ref.at[theirs], dst_ref=recv.at[k],
                send_sem=ici_send.at[k], recv_sem=ici_recv.at[k],
                device_id=(x ^ fx, y ^ fy, c), device_id_type=MESH))
        for cp in cps:
            cp.start()
        acc = own_ref[mine]
        for k, cp in enumerate(cps):
            cp.wait_recv()
            acc = acc + recv[k].astype(F32)
        out_ref[c] = acc
        to_sibling = pltpu.make_async_remote_copy(
            src_ref=out_ref.at[c], dst_ref=out_ref.at[c], send_sem=d2d_send, recv_sem=d2d_recv,
            device_id=(x, y, 1 - c), device_id_type=MESH)
        to_sibling.start()
        pltpu.make_async_remote_copy(
            src_ref=out_ref.at[1 - c], dst_ref=out_ref.at[1 - c], send_sem=d2d_send, recv_sem=d2d_recv,
            device_id=(x, y, 1 - c), device_id_type=MESH).wait_recv()
        to_sibling.wait_send()
        for cp in cps:
            cp.wait_send()

    vm = pl.BlockSpec(memory_space=pltpu.VMEM)
    return pl.pallas_call(
        body, name="rs_chips",
        out_shape=jax.ShapeDtypeStruct((2,) + part_f32.shape[1:], F32),
        in_specs=[vm, vm], out_specs=vm,
        scratch_shapes=[pltpu.VMEM((3,) + part_bf16.shape[1:], BF16), pltpu.SemaphoreType.DMA((3,)),
                        pltpu.SemaphoreType.DMA((3,)), pltpu.SemaphoreType.DMA, pltpu.SemaphoreType.DMA],
        compiler_params=pltpu.CompilerParams(vmem_limit_bytes=VMEM_LIMIT),
    )(part_f32, part_bf16)


def _small_allreduce(vec):
    def body(in_ref, out_ref, recv, send_sems, recv_sems):
        x, y, c = lax.axis_index("x"), lax.axis_index("y"), lax.axis_index("c")
        me = 4 * x + 2 * y + c
        cps = []
        for k in range(1, 8):
            fx, fy, fc = (k >> 2) & 1, (k >> 1) & 1, k & 1
            cps.append(pltpu.make_async_remote_copy(
                src_ref=in_ref, dst_ref=recv.at[k - 1],
                send_sem=send_sems.at[k - 1], recv_sem=recv_sems.at[k - 1],
                device_id=(x ^ fx, y ^ fy, c ^ fc), device_id_type=MESH))
        for cp in cps:
            cp.start()
        for cp in cps:
            cp.wait()
        acc = jnp.zeros(in_ref.shape, F32)
        for j in range(8):
            slot = jnp.maximum((me ^ j) - 1, 0)
            acc = acc + jnp.where(me == j, in_ref[...], recv[slot])
        out_ref[...] = acc

    vm = pl.BlockSpec(memory_space=pltpu.VMEM)
    return pl.pallas_call(
        body, name="small_allreduce",
        out_shape=jax.ShapeDtypeStruct(vec.shape, F32),
        in_specs=[vm], out_specs=vm,
        scratch_shapes=[pltpu.VMEM((7,) + vec.shape, F32), pltpu.SemaphoreType.DMA((7,)),
                        pltpu.SemaphoreType.DMA((7,))],
    )(vec)


ADAM_ROWS = 64


def _adamw(ws, gs, ms, vs, name):
    n = len(ws)

    def body(*refs):
        for i in range(n):
            w_ref, g_ref, m_ref, v_ref = (refs[k * n + i] for k in range(4))
            d_ref, nm_ref, nv_ref = (refs[(4 + k) * n + i] for k in range(3))
            rows = min(ADAM_ROWS, w_ref.shape[0])

            def chunk(r, _):
                at = pl.ds(pl.multiple_of(r * rows, SUBLANES), rows)
                gg = g_ref[at, :]
                nm = B1 * m_ref[at, :] + (1.0 - B1) * gg
                nv = B2 * v_ref[at, :] + (1.0 - B2) * (gg * gg)
                m_hat = nm / (1.0 - B1 ** STEP)
                v_hat = nv / (1.0 - B2 ** STEP)
                d_ref[at, :] = -LR * (m_hat / (jnp.sqrt(v_hat) + ADAM_EPS) + WD * w_ref[at, :])
                nm_ref[at, :] = nm
                nv_ref[at, :] = nv
                return 0

            lax.fori_loop(0, w_ref.shape[0] // rows, chunk, 0)

    vm = pl.BlockSpec(memory_space=pltpu.VMEM)
    sds = [jax.ShapeDtypeStruct(w.shape, F32) for w in ws]
    outs = pl.pallas_call(
        body, name=name, out_shape=sds * 3, in_specs=[vm] * (4 * n), out_specs=[vm] * (3 * n),
        compiler_params=pltpu.CompilerParams(vmem_limit_bytes=VMEM_LIMIT),
    )(*ws, *gs, *ms, *vs)
    return outs[:n], outs[n:2 * n], outs[2 * n:]


def _pre_fwd(x, pos, invf, gpre, wp, gq, wuq, gkv, wukv, mu, w0, w2p, a0, a2p, k_k, k_a, bo):
    bsz, t, _ = x.shape
    nt = t // TT

    def body(x_ref, pos_ref, invf_ref, gpre_ref, wp_ref, gq_ref, wuq_ref, gkv_ref, wukv_ref, mu_ref, w0_ref,
             w2p_ref, a0_ref, a2p_ref, kk_ref, ka_ref, bo_ref,
             u_ref, pp_ref, q_ref, k_ref, v_ref, r_o, w_o, kp_o, vv_o, al_o, be_o, carry):
        i = pl.program_id(1)
        u, _, _ = _rms(x_ref[0], gpre_ref[...], D)
        ub = u.astype(BF16)
        u_ref[0] = ub
        p = _dot(ub, wp_ref[...])
        pp_ref[0] = p
        prw = p[:, RW0:DP]

        @pl.when(i == 0)
        def _():
            carry[...] = jnp.zeros(carry.shape, F32)

        ps, _ = _shift_mix(prw, carry[7:8, :], mu_ref[...])
        carry[...] = prw[TT - 8:TT, :]

        g = _rw_gates(ps, w0_ref[...], w2p_ref[...], a0_ref[...], a2p_ref[...], kk_ref[...], ka_ref[...],
                      bo_ref[...])
        r_o[0] = g["r"]
        w_o[0] = g["w"]
        kp_o[0] = g["kp"]
        vv_o[0] = g["v"]
        al_o[0] = -g["kk"]
        be_o[0] = g["kk"] * g["a"]

        cqn, _, _ = _rms(p[:, CQ0:CQ0 + 256], gq_ref[...], 256)
        q = _dot(cqn.astype(BF16), wuq_ref[...])
        ckvn, _, _ = _rms(p[:, CKV0:CKV0 + 128], gkv_ref[...], 128)
        kv = _dot(ckvn.astype(BF16), wukv_ref[...])
        ang = pos_ref[0] * invf_ref[...]
        cs, sn = jnp.cos(ang), jnp.sin(ang)
        lane = lax.broadcasted_iota(jnp.int32, cs.shape, 1)
        kr = ps[:, 1536:1536 + LANES]
        kr = jnp.where(lane < 64, kr * cs + _rot(kr) * sn, 0.0).astype(BF16)
        for h in range(HEADS):
            qr = q[:, 256 * h + 128:256 * h + 256]
            q_ref[0, :, 256 * h:256 * h + 128] = q[:, 256 * h:256 * h + 128].astype(BF16)
            q_ref[0, :, 256 * h + 128:256 * h + 256] = (qr * cs + _rot(qr) * sn).astype(BF16)
            k_ref[0, :, 256 * h:256 * h + 128] = kv[:, 128 * h:128 * h + 128].astype(BF16)
            k_ref[0, :, 256 * h + 128:256 * h + 256] = kr
        v_ref[0] = kv[:, 512:1024].astype(BF16)

    tok = lambda c: pl.BlockSpec((1, TT, c), lambda b, i: (b, i, 0))
    full = lambda a: _full(a.shape)
    ins = (x, pos, invf, gpre, wp, gq, wuq, gkv, wukv, mu, w0, w2p, a0, a2p, k_k, k_a, bo)
    in_specs = [tok(D), tok(1)] + [full(a) for a in ins[2:]]
    sd = lambda c, dt: jax.ShapeDtypeStruct((bsz, t, c), dt)
    out_shape = [sd(D, BF16), sd(DP, F32), sd(1024, BF16), sd(1024, BF16), sd(512, BF16)] + [sd(RW, F32)] * 6
    out_specs = [tok(D), tok(DP), tok(1024), tok(1024), tok(512)] + [tok(RW)] * 6
    return pl.pallas_call(
        body, name="pre_fwd", grid=(bsz, nt), out_shape=out_shape, in_specs=in_specs, out_specs=out_specs,
        scratch_shapes=[pltpu.VMEM((8, NRW), F32)],
        compiler_params=_cparams(("arbitrary", "arbitrary")),
    )(*ins)


def _attn_fwd(q, k, v):
    bsz, t, _ = q.shape
    nq = t // TQ

    hps = HEADS

    def body(q_ref, k_ref, v_ref, o_ref, lse_ref):
        i = pl.program_id(2)

        def step(j, carry, diagonal):
            at = pl.ds(pl.multiple_of(j * TQ, TQ), TQ)
            out = []
            for hh in range(hps):
                m, l, acc = carry[hh]
                s = _dot_nt(q_ref[0, :, 256 * hh:256 * (hh + 1)], k_ref[0, at, 256 * hh:256 * (hh + 1)]) * SCALE
                if diagonal:
                    s = jnp.where(lax.broadcasted_iota(jnp.int32, (TQ, TQ), 1)
                                  <= lax.broadcasted_iota(jnp.int32, (TQ, TQ), 0), s, -1e30)
                mn = jnp.maximum(m, jnp.max(s, axis=1, keepdims=True))
                p = jnp.exp(s - mn)
                al = jnp.exp(m - mn)
                l = al * l + jnp.sum(p, axis=1, keepdims=True)
                acc = al * acc + _dot(p.astype(BF16), v_ref[0, at, LANES * hh:LANES * (hh + 1)])
                out.append((mn, l, acc))
            return tuple(out)

        start = (jnp.full((TQ, 1), -1e30, F32), jnp.zeros((TQ, 1), F32), jnp.zeros((TQ, LANES), F32))
        before = lax.fori_loop(0, i, lambda j, carry: step(j, carry, False), (start,) * hps)
        for hh, (m, l, acc) in enumerate(step(i, before, True)):
            o_ref[0, :, LANES * hh:LANES * (hh + 1)] = acc / l
            lse_ref[0, hh] = jnp.broadcast_to(m + jnp.log(l), (TQ, LANES))

    return pl.pallas_call(
        body, name="attn_fwd", grid=(bsz, HEADS // hps, nq),
        out_shape=[jax.ShapeDtypeStruct((bsz, t, 512), F32), jax.ShapeDtypeStruct((bsz, HEADS, t, LANES), F32)],
        in_specs=[pl.BlockSpec((1, TQ, 256 * hps), lambda b, h, i: (b, i, h)),
                  pl.BlockSpec((1, t, 256 * hps), lambda b, h, i: (b, 0, h)),
                  pl.BlockSpec((1, t, LANES * hps), lambda b, h, i: (b, 0, h))],
        out_specs=[pl.BlockSpec((1, TQ, LANES * hps), lambda b, h, i: (b, i, h)),
                   pl.BlockSpec((1, hps, TQ, LANES), lambda b, h, i: (b, h, i, 0))],
        compiler_params=_cparams(("parallel", "parallel", "arbitrary")),
    )(q, k, v)


def _attn_bwd(q, k, v, o, lse, do):
    bsz, t, _ = q.shape
    nq = t // TQ

    def body(q_ref, k_ref, v_ref, o_ref, lse_ref, do_ref, dq_ref, dk_ref, dv_ref, dl_ref):
        j = pl.program_id(2)

        @pl.when(j == 0)
        def _():
            def prep(i, _):
                at = pl.ds(pl.multiple_of(i * TQ, TQ), TQ)
                for hh in range(2):
                    lanes = slice(LANES * hh, LANES * (hh + 1))
                    dl_ref[hh, at, :] = jnp.broadcast_to(
                        jnp.sum(do_ref[0, at, lanes] * o_ref[0, at, lanes], axis=1, keepdims=True), (TQ, LANES))
                return 0

            lax.fori_loop(0, nq, prep, 0)
            dq_ref[0] = jnp.zeros((t, 512), F32)

        def q_tile(i, carry, diagonal):
            atq = pl.ds(pl.multiple_of(i * TQ, TQ), TQ)
            out = []
            for hh in range(2):
                dk, dv = carry[hh]
                wide, narrow = slice(256 * hh, 256 * (hh + 1)), slice(LANES * hh, LANES * (hh + 1))
                qt, kt, vt = q_ref[0, atq, wide], k_ref[0, :, wide], v_ref[0, :, narrow]
                dob = do_ref[0, atq, narrow].astype(BF16)
                s = _dot_nt(qt, kt) * SCALE
                if diagonal:
                    s = jnp.where(lax.broadcasted_iota(jnp.int32, (TQ, TQ), 1)
                                  <= lax.broadcasted_iota(jnp.int32, (TQ, TQ), 0), s, -1e30)
                p = jnp.exp(s - lse_ref[0, hh, atq, :][:, 0:1])
                dv = dv + _dot_tn(p.astype(BF16), dob)
                dp = _dot_nt(dob, vt)
                ds = (p * (dp - dl_ref[hh, atq, :][:, 0:1]) * SCALE).astype(BF16)
                dk = dk + _dot_tn(ds, qt)
                dq_ref[0, atq, wide] += _dot(ds, kt)
                out.append((dk, dv))
            return tuple(out)

        zero = (jnp.zeros((TQ, 256), F32), jnp.zeros((TQ, LANES), F32))
        first = q_tile(j, (zero, zero), True)
        done = lax.fori_loop(j + 1, nq, lambda i, carry: q_tile(i, carry, False), first)
        for hh, (dk, dv) in enumerate(done):
            dk_ref[0, :, 256 * hh:256 * (hh + 1)] = dk
            dv_ref[0, :, LANES * hh:LANES * (hh + 1)] = dv

    whole = lambda c: pl.BlockSpec((1, t, c), lambda b, h, j: (b, 0, h))
    tile = lambda c: pl.BlockSpec((1, TQ, c), lambda b, h, j: (b, j, h))
    return pl.pallas_call(
        body, name="attn_bwd", grid=(bsz, HEADS // 2, nq),
        out_shape=[jax.ShapeDtypeStruct((bsz, t, 1024), F32), jax.ShapeDtypeStruct((bsz, t, 1024), F32),
                   jax.ShapeDtypeStruct((bsz, t, 512), F32)],
        in_specs=[whole(512), tile(512), tile(256), whole(256),
                  pl.BlockSpec((1, 2, t, LANES), lambda b, h, j: (b, h, 0, 0)), whole(256)],
        out_specs=[whole(512), tile(512), tile(256)],
        scratch_shapes=[pltpu.VMEM((2, t, LANES), F32)],
        compiler_params=_cparams(("parallel", "parallel", "arbitrary")),
    )(q, k, v, o, lse, do)


RW_HEADS = 8
CH = 32


def _lane_split(bsz):
    vs = LANES // (bsz * RW_HEADS)
    return vs, 64 // vs


def _gather_matrix(bsz):
    group = bsz * RW_HEADS
    vs = LANES // group
    half = (RW_HEADS // 2) * bsz * SPREAD_STEPS
    p = np.zeros((SPREAD_STEPS // vs * LANES, 2 * half), np.float32)
    for g2 in range(SPREAD_STEPS // vs):
        for j in range(vs):
            for b in range(bsz):
                for h in range(RW_HEADS):
                    hp, hpar = h // 2, h % 2
                    p[g2 * LANES + j * group + b * RW_HEADS + h,
                      hpar * half + (hp * bsz + b) * SPREAD_STEPS + g2 * vs + j] = 1.0
    return jnp.asarray(np.concatenate([p] * 3, axis=0), BF16)


def _gather_k(ys, bsz):
    vs = LANES // (bsz * RW_HEADS)
    assert (RW_HEADS // 2) * bsz * SPREAD_STEPS == LANES, "the transposed tile must be 128 lanes wide"
    tg = ys[0].shape[0]
    n = len(ys)
    ngrp = GATHER_BLOCK // SPREAD_STEPS
    per = SPREAD_STEPS // vs

    def body(*refs):
        pm = refs[n][...]
        for y_ref, o_ref in zip(refs[:n], refs[n + 1:]):
            lhs = jnp.concatenate(
                [jnp.concatenate(_split3(jnp.concatenate([y_ref[per * m + g2] for g2 in range(per)], axis=1)), axis=1)
                 for m in range(ngrp)], axis=0)
            a = _dot(lhs, pm)
            for m in range(ngrp):
                am = a[64 * m:64 * (m + 1)]
                bt = jnp.concatenate([am[:, 0:LANES], am[:, LANES:2 * LANES]], axis=0).T
                for hp in range(RW_HEADS // 2):
                    for b in range(bsz):
                        at = (hp * bsz + b) * SPREAD_STEPS
                        o_ref[b, SPREAD_STEPS * m:SPREAD_STEPS * (m + 1), LANES * hp:LANES * (hp + 1)] = \
                            bt[at:at + SPREAD_STEPS]

    pm = _gather_matrix(bsz)
    return pl.pallas_call(
        body, name="wkv_gather", grid=(tg * vs // GATHER_BLOCK,),
        out_shape=[jax.ShapeDtypeStruct((bsz, tg * vs, RW), F32)] * n,
        in_specs=[pl.BlockSpec((GATHER_BLOCK // vs, 64, LANES), lambda i: (i, 0, 0))] * n + [_full(pm.shape)],
        out_specs=[pl.BlockSpec((bsz, GATHER_BLOCK, RW), lambda i: (0, i, 0))] * n,
        compiler_params=_cparams(("parallel",)),
    )(*ys, pm)


def _to_v(x):
    bsz, t, _ = x.shape
    vs, vq = _lane_split(bsz)
    return jnp.transpose(x.reshape(bsz, t, RW_HEADS, vs, vq), (1, 4, 3, 0, 2)).reshape(t, vq, LANES)


def _from_v(y, bsz):
    t = y.shape[0]
    vs, vq = _lane_split(bsz)
    return jnp.transpose(y.reshape(t, vq, vs, bsz, RW_HEADS), (3, 0, 4, 2, 1)).reshape(bsz, t, RW)


def _ksum(a):
    return jnp.sum(a, axis=0, keepdims=True)


def _fold(a, group):
    sh = LANES // 2
    while sh >= group:
        a = a + pltpu.roll(a, sh, 1)
        sh //= 2
    return a


def _lane_group(shape, group):
    return lax.broadcasted_iota(jnp.int32, shape, 1) // group


SPREAD_STEPS = 8
SPREAD_BLOCK = 64
GATHER_BLOCK = 128


def _spread_matrix(bsz):
    group = bsz * RW_HEADS
    vs = LANES // group
    rows = (RW_HEADS // 2) * bsz * SPREAD_STEPS
    q = np.zeros((2, rows, SPREAD_STEPS * LANES), np.float32)
    for hpar in range(2):
        for hp in range(RW_HEADS // 2):
            for b in range(bsz):
                for st in range(SPREAD_STEPS):
                    row = (hp * bsz + b) * SPREAD_STEPS + st
                    for s in range(vs):
                        q[hpar, row, st * LANES + s * group + b * RW_HEADS + 2 * hp + hpar] = 1.0
    return jnp.asarray(np.concatenate([q[0], q[1]] * 3, axis=0), BF16)


def _spread_k(xs):
    bsz, t, _ = xs[0].shape
    assert (RW_HEADS // 2) * bsz * SPREAD_STEPS == LANES, "the transposed tile must be 128 lanes wide"
    n = len(xs)
    ngrp = SPREAD_BLOCK // SPREAD_STEPS

    def body(*refs):
        qm = refs[n][...]
        for x_ref, o_ref in zip(refs[:n], refs[n + 1:]):
            cols = [[] for _ in range(6)]
            for m in range(ngrp):
                at = slice(SPREAD_STEPS * m, SPREAD_STEPS * (m + 1))
                x8 = jnp.concatenate([x_ref[b, at, LANES * hp:LANES * (hp + 1)]
                                      for hp in range(RW_HEADS // 2) for b in range(bsz)], axis=0)
                for pi, piece in enumerate(_split3(x8.T)):
                    cols[2 * pi].append(piece[0:64])
                    cols[2 * pi + 1].append(piece[64:128])
            lhs = jnp.concatenate([jnp.concatenate(c, axis=0) for c in cols], axis=1)
            y = _dot(lhs, qm)
            for m in range(ngrp):
                for st in range(SPREAD_STEPS):
                    o_ref[SPREAD_STEPS * m + st] = y[64 * m:64 * (m + 1), LANES * st:LANES * (st + 1)]

    qm = _spread_matrix(bsz)
    return pl.pallas_call(
        body, name="wkv_spread", grid=(t // SPREAD_BLOCK,),
        out_shape=[jax.ShapeDtypeStruct((t, 64, LANES), F32)] * n,
        in_specs=[pl.BlockSpec((bsz, SPREAD_BLOCK, RW), lambda i: (0, i, 0))] * n + [_full(qm.shape)],
        out_specs=[pl.BlockSpec((SPREAD_BLOCK, 64, LANES), lambda i: (i, 0, 0))] * n,
        compiler_params=_cparams(("parallel",)),
    )(*xs, qm)


def _wkv_fwd(r, w, kp, al, be, v):
    t, vq = v.shape[0], v.shape[1]

    def body(r_ref, w_ref, kp_ref, al_ref, be_ref, v_ref, y_ref, a_ref, u_ref, st_ref):
        @pl.when(pl.program_id(0) == 0)
        def _():
            st_ref[...] = jnp.zeros(st_ref.shape, F32)

        def step(tl, _):
            rv, wv, kv, av, bv = r_ref[tl], w_ref[tl], kp_ref[tl], al_ref[tl], be_ref[tl]
            vals = v_ref[tl]
            yrows, urows = [], []
            for q in range(vq):
                s = st_ref[q]
                u = _ksum(s * av)
                s = s * wv + bv * u + kv * vals[q:q + 1]
                st_ref[q] = s
                a_ref[tl, q] = s
                urows.append(u)
                yrows.append(_ksum(s * rv))
            y_ref[tl] = jnp.concatenate(yrows, axis=0)
            u_ref[tl] = jnp.concatenate(urows, axis=0)
            return 0

        lax.fori_loop(0, CH, step, 0)

    kspec = pl.BlockSpec((CH, 64, LANES), lambda i: (i, 0, 0))
    vspec = pl.BlockSpec((CH, vq, LANES), lambda i: (i, 0, 0))
    vsd = jax.ShapeDtypeStruct((t, vq, LANES), F32)
    return pl.pallas_call(
        body, name="wkv_fwd", grid=(t // CH,),
        out_shape=[vsd, jax.ShapeDtypeStruct((t, vq, 64, LANES), F32), vsd],
        in_specs=[kspec] * 5 + [vspec],
        out_specs=[vspec, pl.BlockSpec((CH, vq, 64, LANES), lambda i: (i, 0, 0, 0)), vspec],
        scratch_shapes=[pltpu.VMEM((vq, 64, LANES), F32)],
        compiler_params=_cparams(("arbitrary",)),
    )(r, w, kp, al, be, v)


def _wkv_bwd(r, w, kp, al, be, v, dy, states, u):
    t, vq = v.shape[0], v.shape[1]
    vs = 64 // vq
    group = LANES // vs
    n = t // CH
    ng = CH // vs

    def body(r_ref, w_ref, kp_ref, al_ref, be_ref, v_ref, dy_ref, u_ref, a_ref, ap_ref,
             dr_ref, dw_ref, dkp_ref, dal_ref, dbe_ref, dv_ref, ds_ref):
        @pl.when(pl.program_id(0) == 0)
        def _():
            ds_ref[...] = jnp.zeros(ds_ref.shape, F32)

        earliest = pl.program_id(0) == n - 1

        def reverse(i, _):
            g = ng - 1 - i
            grp = _lane_group((64, LANES), group)
            outs = None
            for j in reversed(range(vs)):
                tl = g * vs + j
                rv, wv, kv, av, bv = r_ref[tl], w_ref[tl], kp_ref[tl], al_ref[tl], be_ref[tl]
                vals, dys, us = v_ref[tl], dy_ref[tl], u_ref[tl]
                acc = None
                dvrows = []
                for q in range(vq):
                    if j > 0:
                        s_prev = a_ref[tl - 1, q]
                    else:
                        before = jnp.where(earliest, 0.0, ap_ref[0, q])
                        s_prev = jnp.where(g == 0, before, a_ref[jnp.maximum(tl - 1, 0), q])
                    dyq = dys[q:q + 1]
                    ds = ds_ref[q] + rv * dyq
                    c = _ksum(ds * bv)
                    dvrows.append(_ksum(ds * kv))
                    terms = (a_ref[tl, q] * dyq, ds * s_prev, ds * vals[q:q + 1], s_prev * c, ds * us[q:q + 1])
                    acc = terms if acc is None else tuple(a + b for a, b in zip(acc, terms))
                    ds_ref[q] = ds * wv + av * c
                dv_ref[tl] = jnp.concatenate(dvrows, axis=0)
                summed = [_fold(a, group) for a in acc]
                outs = summed if outs is None else [jnp.where(grp == j, f, o) for f, o in zip(summed, outs)]
            for ref, o in zip((dr_ref, dw_ref, dkp_ref, dal_ref, dbe_ref), outs):
                ref[g] = o
            return 0

        lax.fori_loop(0, ng, reverse, 0)

    kspec = pl.BlockSpec((CH, 64, LANES), lambda i: (n - 1 - i, 0, 0))
    gspec = pl.BlockSpec((ng, 64, LANES), lambda i: (n - 1 - i, 0, 0))
    vspec = pl.BlockSpec((CH, vq, LANES), lambda i: (n - 1 - i, 0, 0))
    ksd = jax.ShapeDtypeStruct((t // vs, 64, LANES), F32)
    return pl.pallas_call(
        body, name="wkv_bwd", grid=(n,),
        out_shape=[ksd] * 5 + [jax.ShapeDtypeStruct((t, vq, LANES), F32)],
        in_specs=[kspec] * 5 + [vspec, vspec, vspec,
                                pl.BlockSpec((CH, vq, 64, LANES), lambda i: (n - 1 - i, 0, 0, 0)),
                                pl.BlockSpec((1, vq, 64, LANES), lambda i: (jnp.maximum((n - 1 - i) * CH - 1, 0), 0, 0, 0))],
        out_specs=[gspec] * 5 + [vspec],
        scratch_shapes=[pltpu.VMEM((vq, 64, LANES), F32)],
        compiler_params=_cparams(("arbitrary",)),
    )(r, w, kp, al, be, v, dy, u, states, states)


def _post(x, tgt, pp, o, yw, r, kp, v, ln_g, ln_b, r_k, wo, wot, gpost, bo):
    bsz, t, _ = x.shape
    tt = TT_VPU
    nt = t // tt

    def body(x_ref, tgt_ref, z_ref, o_ref, yw_ref, r_ref, kp_ref, v_ref, lng_ref, lnb_ref, rk_ref, wo_ref, wot_ref,
             gpost_ref, bo_ref,
             dh_ref, dz_ref, dym_ref, dyw_ref, dbon_ref, loss_ref, dwo_ref, dgpost_ref, dlng_ref, dlnb_ref, drk_ref):
        first = (pl.program_id(0) == 0) & (pl.program_id(1) == 0)

        @pl.when(first)
        def _():
            for ref in (loss_ref, dwo_ref, dgpost_ref, dlng_ref, dlnb_ref, drk_ref):
                ref[...] = jnp.zeros(ref.shape, F32)

        bo_m = bo_ref[...]
        seg = lambda a: _seg(a, bo_m)
        rowsum = lambda a: jnp.sum(a, axis=0, keepdims=True)
        ywv, rv, kpv, vv = yw_ref[0], r_ref[0], kp_ref[0], v_ref[0]
        ln_g, r_k = lng_ref[...], rk_ref[...]
        mean = seg(ywv) * (1.0 / 64)
        yc = ywv - mean
        rstd = lax.rsqrt(seg(yc * yc) * (1.0 / 64) + GN_EPS)
        yhat = yc * rstd
        sb = seg(rv * kpv * r_k)
        y_rw = yhat * ln_g + lnb_ref[...] + sb * vv
        z = z_ref[0]
        sig = _sigmoid(z)
        sz = z * sig
        ycat = jnp.concatenate([o_ref[0], y_rw], axis=1)
        ycg = (ycat * sz).astype(BF16)
        out = _dot(ycg, wo_ref[...])
        hn, nx, rstd_o = _rms(out, gpost_ref[...], D)
        err = x_ref[0] + hn - tgt_ref[0]
        loss_ref[...] += jnp.sum(err * err) * (0.5 / D)
        dh = err * (1.0 / D)
        dh_ref[0] = dh
        dout, dgp = _rms_bwd(dh, nx, rstd_o, gpost_ref[...], D)
        dgpost_ref[...] += dgp
        doutb = dout.astype(BF16)
        dwo_ref[...] += _dot_tn(ycg, doutb)
        dycg = _dot(doutb, wot_ref[...])
        dz_ref[0] = dycg * ycat * (sig * (1.0 + z * (1.0 - sig)))
        dycat = dycg * sz
        dym_ref[0] = dycat[:, 0:512]
        dy_rw = dycat[:, 512:1024]
        dlnb_ref[...] += rowsum(dy_rw)
        dlng_ref[...] += rowsum(dy_rw * yhat)
        dyhat = dy_rw * ln_g
        dyw_ref[0] = rstd * (dyhat - seg(dyhat) * (1.0 / 64) - yhat * (seg(dyhat * yhat) * (1.0 / 64)))
        dsb = seg(dy_rw * vv)
        drk_ref[...] += rowsum(dsb * rv * kpv)
        dbon_ref[0, :, 0:512] = dsb * kpv * r_k
        dbon_ref[0, :, 512:1024] = dsb * rv * r_k
        dbon_ref[0, :, 1024:1536] = dy_rw * sb

    tok = lambda c: pl.BlockSpec((1, tt, c), lambda b, i: (b, i, 0))
    full = lambda a: _full(a.shape)
    ins = (x, tgt, pp, o, yw, r, kp, v, ln_g, ln_b, r_k, wo, wot, gpost, bo)
    in_specs = [tok(D), tok(D), tok(1024)] + [tok(512)] * 5 + [full(a) for a in ins[8:]]
    sd = lambda c: jax.ShapeDtypeStruct((bsz, t, c), F32)
    vec = lambda c: jax.ShapeDtypeStruct((1, c), F32)
    out_shape = [sd(D), sd(1024), sd(512), sd(512), sd(1536), jax.ShapeDtypeStruct((8, LANES), F32),
                 jax.ShapeDtypeStruct((1024, 1024), F32), vec(D), vec(512), vec(512), vec(512)]
    out_specs = [tok(D), tok(1024), tok(512), tok(512), tok(1536), _resident((8, LANES)), _resident((1024, 1024)),
                 _resident((1, D)), _resident((1, 512)), _resident((1, 512)), _resident((1, 512))]
    return pl.pallas_call(
        body, name="post", grid=(bsz, nt), out_shape=out_shape, in_specs=in_specs, out_specs=out_specs,
        compiler_params=_cparams(("arbitrary", "arbitrary")),
    )(*ins)


def _pre_bwd_a(pp, pos, invf, cqkv_w, mu, w0, w2p, w2pt, a0, a2p, a2pt, k_k, k_a, bo,
               dq, dk, dva, dwkv, dbon):
    gq, wuqt, gkv, wukvt = cqkv_w
    bsz, t, _ = pp.shape
    tt = TT_VPU
    nt = t // tt
    dr_w, dw_w, dkp_w, dv_w, dal_w, dbe_w = dwkv

    def body(pp_ref, pos_ref, invf_ref, gq_ref, wuqt_ref, gkv_ref, wukvt_ref, mu_ref, w0_ref, w2p_ref, w2pt_ref,
             a0_ref, a2p_ref, a2pt_ref, kk_ref, ka_ref, bo_ref, dq_ref, dk_ref, dva_ref,
             dr_ref, dw_ref, dkp_ref, dv_ref, dal_ref, dbe_ref, dbon_ref,
             da_ref, dwuq_ref, dwukv_ref, dw2p_ref, da2p_ref, dgq_ref, dgkv_ref, dmu_ref, dw0_ref, da0_ref,
             dkk_ref, dka_ref, carry):
        i = pl.program_id(1)
        first = (pl.program_id(0) == 0) & (i == 0)

        @pl.when(first)
        def _():
            for ref in (dwuq_ref, dwukv_ref, dw2p_ref, da2p_ref, dgq_ref, dgkv_ref, dmu_ref, dw0_ref, da0_ref,
                        dkk_ref, dka_ref):
                ref[...] = jnp.zeros(ref.shape, F32)

        bo_m = bo_ref[...]
        rowsum = lambda a: jnp.sum(a, axis=0, keepdims=True)
        prw = pp_ref[0, :, RW0:DP]

        @pl.when(i == 0)
        def _():
            carry[...] = jnp.zeros(carry.shape, F32)

        ps, sh = _shift_mix(prw, carry[7:8, :], mu_ref[...])
        carry[...] = prw[tt - 8:tt, :]
        k_k, k_a = kk_ref[...], ka_ref[...]
        g = _rw_gates(ps, w0_ref[...], w2p_ref[...], a0_ref[...], a2p_ref[...], k_k, k_a, bo_m)
        a, kk, k = g["a"], g["kk"], g["k"]
        dr = dr_ref[0] + dbon_ref[0, :, 0:512]
        dkp = dkp_ref[0] + dbon_ref[0, :, 512:1024]
        dv = dv_ref[0] + dbon_ref[0, :, 1024:1536]
        dbe = dbe_ref[0]
        dkk = dbe * a - dal_ref[0]
        da = dbe * kk + dkp * k * k_a
        dka_ref[...] += rowsum(dkp * k * (a - 1.0))
        dm = (dkk - kk * _seg(dkk * kk, bo_m)) / g["nrm"]
        dkk_ref[...] += rowsum(dm * k)
        dk_tot = dkp * (1.0 + (a - 1.0) * k_a) + dm * k_k
        dapre = da * a * (1.0 - a)
        da0_ref[...] += rowsum(dapre)
        dapb = dapre.astype(BF16)
        da2p_ref[...] += _dot_tn(g["misc"].astype(BF16), dapb)
        dwpre = dw_ref[0] * g["w"] * (-g["e"]) * _sigmoid(-g["wpre"])
        dw0_ref[...] += rowsum(dwpre)
        dwpb = dwpre.astype(BF16)
        th = g["th"]
        dw2p_ref[...] += _dot_tn(th.astype(BF16), dwpb)
        dmisc = _dot(dapb, a2pt_ref[...]) + _dot(dwpb, w2pt_ref[...]) * (1.0 - th * th)
        ang = pos_ref[0] * invf_ref[...]
        cs, sn = jnp.cos(ang), jnp.sin(ang)
        unrope = lambda gr: gr * cs - _rot(gr * sn)
        lane = lax.broadcasted_iota(jnp.int32, cs.shape, 1)
        dkr = dk_ref[0, :, 128:256]
        for h in range(1, HEADS):
            dkr = dkr + dk_ref[0, :, 256 * h + 128:256 * h + 256]
        dkr = jnp.where(lane < 64, unrope(dkr), 0.0)
        dmisc = dmisc + jnp.concatenate([dkr, jnp.zeros_like(dkr)], axis=1)
        dqp = jnp.concatenate(
            [blk for h in range(HEADS)
             for blk in (dq_ref[0, :, 256 * h:256 * h + 128], unrope(dq_ref[0, :, 256 * h + 128:256 * h + 256]))],
            axis=1).astype(BF16)
        dkvp = jnp.concatenate([dk_ref[0, :, 256 * h:256 * h + 128] for h in range(HEADS)] + [dva_ref[0]],
                               axis=1).astype(BF16)
        cqn, cq_nx, cq_rstd = _rms(pp_ref[0, :, CQ0:CQ0 + 256], gq_ref[...], 256)
        ckvn, ckv_nx, ckv_rstd = _rms(pp_ref[0, :, CKV0:CKV0 + 128], gkv_ref[...], 128)
        dwuq_ref[...] += _dot_tn(cqn.astype(BF16), dqp)
        dwukv_ref[...] += _dot_tn(ckvn.astype(BF16), dkvp)
        dcq, dgq = _rms_bwd(_dot(dqp, wuqt_ref[...]), cq_nx, cq_rstd, gq_ref[...], 256)
        dckv, dgkv = _rms_bwd(_dot(dkvp, wukvt_ref[...]), ckv_nx, ckv_rstd, gkv_ref[...], 128)
        dgq_ref[...] += dgq
        dgkv_ref[...] += dgkv
        dps = jnp.concatenate([dr, dk_tot, dv, dmisc], axis=1)
        dmu_ref[...] += rowsum(dps * (sh - prw))
        da_ref[0, :, 0:256] = dcq
        da_ref[0, :, 256:384] = dckv
        da_ref[0, :, 384:384 + NRW] = dps

    tok = lambda c: pl.BlockSpec((1, tt, c), lambda b, i: (b, i, 0))
    full = lambda a: _full(a.shape)
    ins = (pp, pos, invf, gq, wuqt, gkv, wukvt, mu, w0, w2p, w2pt, a0, a2p, a2pt, k_k, k_a, bo,
           dq, dk, dva, dr_w, dw_w, dkp_w, dv_w, dal_w, dbe_w, dbon)
    in_specs = ([tok(DP), tok(1)] + [full(a) for a in ins[2:17]] + [tok(1024), tok(1024), tok(512)]
                + [tok(512)] * 6 + [tok(1536)])
    shp = lambda *s: jax.ShapeDtypeStruct(s, F32)
    out_shape = [shp(bsz, t, 384 + NRW), shp(256, 1024), shp(128, 1024), shp(256, 512), shp(256, 512),
                 shp(1, 256), shp(1, 128), shp(1, NRW), shp(1, 512), shp(1, 512), shp(1, 512), shp(1, 512)]
    out_specs = [tok(384 + NRW)] + [_resident(s.shape) for s in out_shape[1:]]
    return pl.pallas_call(
        body, name="pre_bwd_a", grid=(bsz, nt), out_shape=out_shape, in_specs=in_specs, out_specs=out_specs,
        scratch_shapes=[pltpu.VMEM((8, NRW), F32)],
        compiler_params=_cparams(("arbitrary", "arbitrary")),
    )(*ins)


def _pre_bwd_b(x, dh, dz, da, mu, wpt, gpre):
    bsz, t, _ = x.shape
    nt = t // TT
    nblk = t // 8

    def body(x_ref, dh_ref, dz_ref, da_ref, nxt_ref, mu_ref, wpt_ref, gpre_ref, gx_ref, dp_ref, dgpre_ref):
        i = pl.program_id(1)
        first = (pl.program_id(0) == 0) & (i == 0)

        @pl.when(first)
        def _():
            dgpre_ref[...] = jnp.zeros(dgpre_ref.shape, F32)

        mu_v = mu_ref[...]
        dps = da_ref[0, :, 384:384 + NRW]
        nxt = jnp.where(i < nt - 1, nxt_ref[0, 0:1, 384:384 + NRW], 0.0)
        row = lax.broadcasted_iota(jnp.int32, dps.shape, 0)
        up = jnp.where(row == TT - 1, nxt, pltpu.roll(dps, TT - 1, 0))
        dprw = dps * (1.0 - mu_v) + up * mu_v
        dp = jnp.concatenate([dz_ref[0], da_ref[0, :, 0:384], dprw], axis=1).astype(BF16)
        dp_ref[0] = dp
        du = _dot(dp, wpt_ref[...])
        _, nx, rstd = _rms(x_ref[0], gpre_ref[...], D)
        dx, dg = _rms_bwd(du, nx, rstd, gpre_ref[...], D)
        dgpre_ref[...] += dg
        gx_ref[0] = dh_ref[0] + dx

    tok = lambda c: pl.BlockSpec((1, TT, c), lambda b, i: (b, i, 0))
    nxt_spec = pl.BlockSpec((1, 8, 384 + NRW), lambda b, i: (b, jnp.minimum((i + 1) * (TT // 8), nblk - 1), 0))
    ins = (x, dh, dz, da, da, mu, wpt, gpre)
    return pl.pallas_call(
        body, name="pre_bwd_b", grid=(bsz, nt),
        out_shape=[jax.ShapeDtypeStruct((bsz, t, D), F32), jax.ShapeDtypeStruct((bsz, t, DP), BF16),
                   jax.ShapeDtypeStruct((1, D), F32)],
        in_specs=[tok(D), tok(D), tok(1024), tok(384 + NRW), nxt_spec, _full(mu.shape), _full(wpt.shape),
                  _full(gpre.shape)],
        out_specs=[tok(D), tok(DP), _resident((1, D))],
        compiler_params=_cparams(("arbitrary", "arbitrary")),
    )(*ins)


def _tn_matmul(a, b, bn, name, bk=512):
    kdim, m = a.shape
    _, n = b.shape
    nk = kdim // bk

    def body(a_ref, b_ref, o_ref):
        @pl.when(pl.program_id(1) == 0)
        def _():
            o_ref[...] = jnp.zeros(o_ref.shape, F32)

        o_ref[...] += _dot_tn(a_ref[...], b_ref[...])

    return pl.pallas_call(
        body, name=name, grid=(n // bn, nk),
        out_shape=jax.ShapeDtypeStruct((m, n), F32),
        in_specs=[pl.BlockSpec((bk, m), lambda j, kk: (kk, 0)), pl.BlockSpec((bk, bn), lambda j, kk: (kk, j))],
        out_specs=pl.BlockSpec((m, bn), lambda j, kk: (0, j)),
        compiler_params=_cparams(("parallel", "arbitrary")),
    )(a, b)


SHARDED = ("w_in", "mla_w_uq", "mla_w_ukv", "rw_w2", "rw_a2", "w_out")
SMALL = ("norm_pre_g", "mla_q_norm_g", "mla_kv_norm_g", "rw_mu", "rw_w0", "rw_a0", "rw_k_k", "rw_k_a", "rw_r_k",
         "rw_ln_g", "rw_ln_b", "norm_post_g")
WEIGHTS = ("norm_pre_g", "w_in", "mla_q_norm_g", "mla_w_uq", "mla_kv_norm_g", "mla_w_ukv", "rw_mu", "rw_w0", "rw_w2",
           "rw_a0", "rw_a2", "rw_k_k", "rw_k_a", "rw_r_k", "rw_ln_g", "rw_ln_b", "w_out", "norm_post_g")


def _pack_small(d):
    flat = jnp.concatenate([d[n].reshape(1, -1) for n in SMALL], axis=1)
    return jnp.pad(flat, ((0, 0), (0, SMALL_ROWS * LANES - flat.shape[1]))).reshape(SMALL_ROWS, LANES)


def _unpack_small(packed, like):
    flat = packed.reshape(1, -1)
    out, at = {}, 0
    for n in SMALL:
        size = int(np.prod(like[n].shape))
        out[n] = flat[:, at:at + size].reshape(like[n].shape)
        at += size
    return out


def _unpack_shard(packed, like):
    out, at = {}, 0
    for n, rows in zip(SHARDED, PACK_ROWS):
        out[n] = packed[at:at + rows].reshape(like[n].shape)
        at += rows
    return out


def _constants():
    bo = np.kron(np.eye(2, dtype=np.float32), np.ones((64, 64), np.float32))
    inv = ROPE_THETA ** (-np.arange(0, 64, 2, dtype=np.float32) / 64)
    invf = np.concatenate([inv, inv, np.zeros(64, np.float32)]).astype(np.float32)[None, :]
    return jnp.asarray(bo, BF16), jnp.asarray(invf)


def kernel(x, positions, norm_pre_g, w_in, mla_q_norm_g, mla_w_uq, mla_kv_norm_g, mla_w_ukv, rw_mu, rw_w0, rw_w2, rw_a0, rw_a2, rw_k_k, rw_k_a, rw_r_k, rw_ln_g, rw_ln_b, w_out, norm_post_g, loss_target, m_norm_pre_g, m_w_in, m_mla_q_norm_g, m_mla_w_uq, m_mla_kv_norm_g, m_mla_w_ukv, m_rw_mu, m_rw_w0, m_rw_w2, m_rw_a0, m_rw_a2, m_rw_k_k, m_rw_k_a, m_rw_r_k, m_rw_ln_g, m_rw_ln_b, m_w_out, m_norm_post_g, v_norm_pre_g, v_w_in, v_mla_q_norm_g, v_mla_w_uq, v_mla_kv_norm_g, v_mla_w_ukv, v_rw_mu, v_rw_w0, v_rw_w2, v_rw_a0, v_rw_a2, v_rw_k_k, v_rw_k_a, v_rw_r_k, v_rw_ln_g, v_rw_ln_b, v_w_out, v_norm_post_g):
    wts = dict(norm_pre_g=norm_pre_g, w_in=w_in, mla_q_norm_g=mla_q_norm_g, mla_w_uq=mla_w_uq,
               mla_kv_norm_g=mla_kv_norm_g, mla_w_ukv=mla_w_ukv, rw_mu=rw_mu, rw_w0=rw_w0, rw_w2=rw_w2, rw_a0=rw_a0,
               rw_a2=rw_a2, rw_k_k=rw_k_k, rw_k_a=rw_k_a, rw_r_k=rw_r_k, rw_ln_g=rw_ln_g, rw_ln_b=rw_ln_b, w_out=w_out,
               norm_post_g=norm_post_g)
    mom_m = dict(norm_pre_g=m_norm_pre_g, w_in=m_w_in, mla_q_norm_g=m_mla_q_norm_g, mla_w_uq=m_mla_w_uq,
                 mla_kv_norm_g=m_mla_kv_norm_g, mla_w_ukv=m_mla_w_ukv, rw_mu=m_rw_mu, rw_w0=m_rw_w0, rw_w2=m_rw_w2,
                 rw_a0=m_rw_a0, rw_a2=m_rw_a2, rw_k_k=m_rw_k_k, rw_k_a=m_rw_k_a, rw_r_k=m_rw_r_k, rw_ln_g=m_rw_ln_g,
                 rw_ln_b=m_rw_ln_b, w_out=m_w_out, norm_post_g=m_norm_post_g)
    mom_v = dict(norm_pre_g=v_norm_pre_g, w_in=v_w_in, mla_q_norm_g=v_mla_q_norm_g, mla_w_uq=v_mla_w_uq,
                 mla_kv_norm_g=v_mla_kv_norm_g, mla_w_ukv=v_mla_w_ukv, rw_mu=v_rw_mu, rw_w0=v_rw_w0, rw_w2=v_rw_w2,
                 rw_a0=v_rw_a0, rw_a2=v_rw_a2, rw_k_k=v_rw_k_k, rw_k_a=v_rw_k_a, rw_r_k=v_rw_r_k, rw_ln_g=v_rw_ln_g,
                 rw_ln_b=v_rw_ln_b, w_out=v_w_out, norm_post_g=v_norm_post_g)
    bsz, t, _ = x.shape
    bo, invf = _constants()

    g_in, g_uq, g_ukv, g_w2, g_a2, g_out = _ag_weights([wts[n][0] for n in SHARDED])
    w_in_f = jnp.transpose(g_in, (1, 0, 2)).reshape(D, D_IN)
    wp = jnp.concatenate([w_in_f[:, 2112:3136], w_in_f[:, 0:384], w_in_f[:, 448:1984], w_in_f[:, 384:448],
                          w_in_f[:, 1984:2112], jnp.zeros((D, 64), BF16)], axis=1)
    wuq = jnp.pad(jnp.transpose(g_uq, (1, 0, 2)).reshape(256, HEADS, 192), ((0, 0), (0, 0), (0, 64))).reshape(256, 1024)
    wukv = jnp.transpose(jnp.transpose(g_ukv, (1, 0, 2)).reshape(128, HEADS, 2, 128), (0, 2, 1, 3)).reshape(128, 1024)
    w2 = jnp.transpose(g_w2, (1, 0, 2)).reshape(64, RW)
    a2 = jnp.transpose(g_a2, (1, 0, 2)).reshape(64, RW)
    w2p = jnp.pad(w2, ((64, 128), (0, 0)))
    a2p = jnp.pad(a2, ((128, 64), (0, 0)))
    wo = g_out.reshape(D, D)
    mu = jnp.concatenate([rw_mu[:, 0:1536], jnp.zeros((1, 64), F32), rw_mu[:, 1536:1664], jnp.zeros((1, 64), F32)],
                         axis=1)
    r_k = rw_r_k.reshape(1, RW)
    pos = positions.astype(F32)[:, :, None]

    (u, pp, q_att, k_att, v_att, r, w, kp, v, al, be) = _pre_fwd(
        x, pos, invf, norm_pre_g, wp, mla_q_norm_g, wuq, mla_kv_norm_g, wukv, mu, rw_w0, w2p, rw_a0, a2p, rw_k_k,
        rw_k_a, bo)
    o, lse = _attn_fwd(q_att, k_att, v_att)
    rw_k = _spread_k([r, w, kp, al, be])
    v_v = _to_v(v)
    yw_v, states, u_v = _wkv_fwd(*rw_k, v_v)
    yw = _from_v(yw_v, bsz)

    (dh, dz, dym, dyw, dbon, loss_acc, d_wo, d_gpost, d_lng, d_lnb, d_rk) = _post(
        x, loss_target, pp, o, yw, r, kp, v, rw_ln_g, rw_ln_b, r_k, wo, wo.T, norm_post_g, bo)

    d_k = _wkv_bwd(*rw_k, v_v, _to_v(dyw), states, u_v)
    dr_w, dw_w, dkp_w, dal_w, dbe_w = _gather_k(d_k[:5], bsz)
    dwkv = (dr_w, dw_w, dkp_w, _from_v(d_k[5], bsz), dal_w, dbe_w)
    dq, dk, dva = _attn_bwd(q_att, k_att, v_att, o, lse, dym)

    (da, d_wuq, d_wukv, d_w2p, d_a2p, d_gq, d_gkv, d_mu, d_w0, d_a0, d_kk, d_ka) = _pre_bwd_a(
        pp, pos, invf, (mla_q_norm_g, wuq.T, mla_kv_norm_g, wukv.T), mu, rw_w0, w2p, w2p.T, rw_a0, a2p, a2p.T,
        rw_k_k, rw_k_a, bo, dq, dk, dva, dwkv, dbon)
    grad_x, dpb, d_gpre = _pre_bwd_b(x, dh, dz, da, mu, wp.T, norm_pre_g)
    d_wp = _tn_matmul(u.reshape(bsz * t, D), dpb.reshape(bsz * t, DP), DP, "dw_in", bk=1024)

    full_g = {
        "w_in": jnp.concatenate([d_wp[:, 1024:1408], d_wp[:, 2944:3008], d_wp[:, 1408:2944], d_wp[:, 3008:3136],
                                 d_wp[:, 0:1024]], axis=1),
        "mla_w_uq": d_wuq.reshape(256, HEADS, 256)[:, :, :192].reshape(256, 768),
        "mla_w_ukv": jnp.transpose(d_wukv.reshape(128, 2, HEADS, 128), (0, 2, 1, 3)).reshape(128, 1024),
        "rw_w2": d_w2p[64:128],
        "rw_a2": d_a2p[128:192],
        "w_out": d_wo,
    }
    small_g = {
        "norm_pre_g": d_gpre, "mla_q_norm_g": d_gq, "mla_kv_norm_g": d_gkv,
        "rw_mu": jnp.concatenate([d_mu[:, 0:1536], d_mu[:, 1600:1728]], axis=1),
        "rw_w0": d_w0, "rw_a0": d_a0, "rw_k_k": d_kk, "rw_k_a": d_ka, "rw_r_k": d_rk, "rw_ln_g": d_lng,
        "rw_ln_b": d_lnb, "norm_post_g": d_gpost,
    }

    def by_shard(name, g):
        if name == "w_out":
            return g.reshape(N_SHARD, -1, LANES)
        rows, cols = g.shape
        return jnp.transpose(g.reshape(rows, N_SHARD, cols // N_SHARD), (1, 0, 2)).reshape(N_SHARD, -1, LANES)

    packed = jnp.concatenate([by_shard(n, full_g[n]) for n in SHARDED], axis=1)
    pair_sum, pair_sum_b = _rs_pairs(packed.reshape(N_SHARD, 2, HALF, LANES))
    g_shard = _rs_chips(pair_sum, pair_sum_b).reshape(PACK_TOTAL, LANES)

    g_small = _small_allreduce(jnp.concatenate([_pack_small(small_g)[:SMALL_USED], loss_acc[0:SMALL_ROWS - SMALL_USED]]))
    loss = g_small[SMALL_USED, 0]

    g_sharded = _unpack_shard(g_shard, {n: wts[n][0] for n in SHARDED})
    sh = _adamw([wts[n][0] for n in SHARDED], [g_sharded[n] for n in SHARDED], [mom_m[n][0] for n in SHARDED],
                [mom_v[n][0] for n in SHARDED], "adamw_sharded")
    sm = _adamw([_pack_small(wts)], [g_small], [_pack_small(mom_m)], [_pack_small(mom_v)], "adamw_small")

    def outputs(sharded, small):
        out = {n: a[None] for n, a in zip(SHARDED, sharded)}
        out.update(_unpack_small(small, wts))
        return out

    grads = outputs([g_sharded[n] for n in SHARDED], g_small)
    deltas, new_m, new_v = (outputs(sh[k], sm[k][0]) for k in range(3))
    return (loss, grad_x, *[grads[n] for n in WEIGHTS], *[deltas[n] for n in WEIGHTS],
            *[new_m[n] for n in WEIGHTS], *[new_v[n] for n in WEIGHTS])
```

```python
import numpy as np
import jax
import jax.numpy as jnp
from jax import lax
from jax.experimental import pallas as pl
from jax.experimental.pallas import tpu as pltpu

F32, BF16 = jnp.float32, jnp.bfloat16
MESH = pl.DeviceIdType.MESH

D = 1024
HEADS = 4
RW = 512
NORM_EPS = 1e-6
GN_EPS = 64e-5
ROPE_THETA = 10000.0
SCALE = (128 + 64) ** -0.5
D_IN = 3136
LR, B1, B2, ADAM_EPS, WD, STEP = 0.001, 0.9, 0.999, 1e-08, 0.01, 10

Z0, CQ0, CKV0, RW0, DP = 0, 1024, 1280, 1408, 3200
NRW = DP - RW0

LANES = 128
SUBLANES = 8
VMEM_LIMIT = 56 * 1024 * 1024

TT = 512
TT_VPU = 256
TQ = 512

N_SHARD = 4
PACK_ROWS = (1024 * 784 // 128, 256 * 192 // 128, 128 * 256 // 128, 64, 64, 256 * 1024 // 128)
PACK_TOTAL = sum(PACK_ROWS)
HALF = PACK_TOTAL // 2
SMALL_ROWS = 64
SMALL_USED = 60


def _cparams(sem=None):
    return pltpu.CompilerParams(dimension_semantics=sem, vmem_limit_bytes=VMEM_LIMIT)


def _full(shape):
    n = len(shape)
    return pl.BlockSpec(shape, lambda *_: (0,) * n, pipeline_mode=pl.Buffered(1))


def _resident(shape):
    n = len(shape)
    return pl.BlockSpec(shape, lambda *_: (0,) * n)


def _dot(a, b):
    return jnp.dot(a, b, preferred_element_type=F32)


def _dot_nt(a, b):
    return lax.dot_general(a, b, (((1,), (1,)), ((), ())), preferred_element_type=F32)


def _dot_tn(a, b):
    return lax.dot_general(a, b, (((0,), (0,)), ((), ())), preferred_element_type=F32)


def _split3(x):
    hi = x.astype(BF16)
    r1 = x - hi.astype(F32)
    mid = r1.astype(BF16)
    lo = (r1 - mid.astype(F32)).astype(BF16)
    return hi, mid, lo


def _seg(x, bo):
    rows, nblk = x.shape[0], x.shape[1] // LANES
    pieces = [p for i in range(nblk) for p in _split3(x[:, LANES * i:LANES * (i + 1)])]
    res = _dot(jnp.concatenate(pieces, axis=0), bo)
    parts = [res[(3 * i) * rows:(3 * i + 1) * rows] + res[(3 * i + 1) * rows:(3 * i + 2) * rows]
             + res[(3 * i + 2) * rows:(3 * i + 3) * rows] for i in range(nblk)]
    return parts[0] if nblk == 1 else jnp.concatenate(parts, axis=1)


def _rms(x, g, n):
    rstd = lax.rsqrt(jnp.sum(x * x, axis=-1, keepdims=True) * (1.0 / n) + NORM_EPS)
    nx = x * rstd
    return nx * g, nx, rstd


def _rms_bwd(dy, nx, rstd, g, n):
    dn = dy * g
    dx = rstd * (dn - nx * (jnp.sum(dn * nx, axis=-1, keepdims=True) * (1.0 / n)))
    return dx, jnp.sum(dy * nx, axis=0, keepdims=True)


def _rot(x):
    lane = lax.broadcasted_iota(jnp.int32, x.shape, 1)
    return jnp.where((lane % 64) < 32, -pltpu.roll(x, x.shape[1] - 32, 1), pltpu.roll(x, 32, 1))


def _sigmoid(x):
    return 1.0 / (1.0 + jnp.exp(-x))


def _softplus(x):
    return jnp.maximum(x, 0.0) + jnp.log(1.0 + jnp.exp(-jnp.abs(x)))


def _rw_gates(ps, w0, w2p, a0, a2p, k_k, k_a, bo):
    r, k, v, misc = ps[:, 0:512], ps[:, 512:1024], ps[:, 1024:1536], ps[:, 1536:NRW]
    th = jnp.tanh(misc)
    wpre = w0 + _dot(th.astype(BF16), w2p)
    e = jnp.exp(-_softplus(-wpre) - 0.5)
    w = jnp.exp(-e)
    a = _sigmoid(a0 + _dot(misc.astype(BF16), a2p))
    m = k * k_k
    nrm = jnp.maximum(jnp.sqrt(_seg(m * m, bo)), 1e-12)
    kk = m / nrm
    kp = k * (1.0 + (a - 1.0) * k_a)
    return dict(r=r, k=k, v=v, misc=misc, th=th, wpre=wpre, e=e, w=w, a=a, nrm=nrm, kk=kk, kp=kp)


def _shift_mix(prw, prev_row, mu):
    row = lax.broadcasted_iota(jnp.int32, prw.shape, 0)
    sh = jnp.where(row == 0, prev_row, pltpu.roll(prw, 1, 0))
    return prw + (sh - prw) * mu, sh


def _ag_weights(shards):
    n = len(shards)

    def body(*refs):
        ins, outs = refs[:n], refs[n:2 * n]
        ici_send, ici_recv, d2d_send, d2d_recv = refs[2 * n:2 * n + 4]
        x, y, c = lax.axis_index("x"), lax.axis_index("y"), lax.axis_index("c")
        mine = 2 * x + y
        for w in range(n):
            outs[w][mine] = ins[w][...].astype(BF16)
        flips = ((1, 0), (0, 1), (1, 1))

        def half(w, shard, cc):
            rows = outs[w].shape[1] // 2
            return outs[w].at[shard, pl.ds(pl.multiple_of(cc * rows, 16), rows)]

        def ici(w, k, shard):
            fx, fy = flips[k]
            return pltpu.make_async_remote_copy(
                src_ref=half(w, shard, c), dst_ref=half(w, shard, c),
                send_sem=ici_send.at[w * 3 + k], recv_sem=ici_recv.at[w * 3 + k],
                device_id=(x ^ fx, y ^ fy, c), device_id_type=MESH)

        def d2d(w, k, cc):
            fx, fy = flips[k]
            theirs = 2 * (x ^ fx) + (y ^ fy)
            return pltpu.make_async_remote_copy(
                src_ref=half(w, theirs, cc), dst_ref=half(w, theirs, cc),
                send_sem=d2d_send.at[w * 3 + k], recv_sem=d2d_recv.at[w * 3 + k],
                device_id=(x, y, 1 - c), device_id_type=MESH)

        for w in range(n):
            for k in range(3):
                ici(w, k, mine).start()
        for w in range(n):
            for k in range(3):
                fx, fy = flips[k]
                ici(w, k, 2 * (x ^ fx) + (y ^ fy)).wait_recv()
                d2d(w, k, c).start()
        for w in range(n):
            for k in range(3):
                d2d(w, k, 1 - c).wait_recv()
        for w in range(n):
            for k in range(3):
                ici(w, k, mine).wait_send()
                d2d(w, k, c).wait_send()

    vm = pl.BlockSpec(memory_space=pltpu.VMEM)
    return pl.pallas_call(
        body, name="ag_weights",
        out_shape=[jax.ShapeDtypeStruct((N_SHARD,) + s.shape, BF16) for s in shards],
        in_specs=[vm] * n, out_specs=[vm] * n,
        scratch_shapes=[pltpu.SemaphoreType.DMA((3 * n,))] * 4,
        compiler_params=pltpu.CompilerParams(vmem_limit_bytes=VMEM_LIMIT),
    )(*shards)


def _rs_pairs(halves):
    def body(h_ref, sum_ref, sumb_ref, recv, send_sem, recv_sem):
        x, y, c = lax.axis_index("x"), lax.axis_index("y"), lax.axis_index("c")
        cps = [pltpu.make_async_remote_copy(src_ref=h_ref.at[s, 1 - c], dst_ref=recv.at[s], send_sem=send_sem.at[s],
                                            recv_sem=recv_sem.at[s], device_id=(x, y, 1 - c), device_id_type=MESH)
               for s in range(N_SHARD)]
        for cp in cps:
            cp.start()
        for s, cp in enumerate(cps):
            cp.wait_recv()
            acc = h_ref[s, c] + recv[s]
            sum_ref[s] = acc
            sumb_ref[s] = acc.astype(BF16)
        for cp in cps:
            cp.wait_send()

    vm = pl.BlockSpec(memory_space=pltpu.VMEM)
    shape = (N_SHARD,) + halves.shape[2:]
    return pl.pallas_call(
        body, name="rs_pairs",
        out_shape=[jax.ShapeDtypeStruct(shape, F32), jax.ShapeDtypeStruct(shape, BF16)],
        in_specs=[vm], out_specs=[vm, vm],
        scratch_shapes=[pltpu.VMEM(shape, F32), pltpu.SemaphoreType.DMA((N_SHARD,)),
                        pltpu.SemaphoreType.DMA((N_SHARD,))],
        compiler_params=pltpu.CompilerParams(vmem_limit_bytes=VMEM_LIMIT),
    )(halves)


def _rs_chips(part_f32, part_bf16):
    def body(own_ref, src_ref, out_ref, recv, ici_send, ici_recv, d2d_send, d2d_recv):
        x, y, c = lax.axis_index("x"), lax.axis_index("y"), lax.axis_index("c")
        mine = 2 * x + y
        flips = ((1, 0), (0, 1), (1, 1))
        cps = []
        for k, (fx, fy) in enumerate(flips):
            theirs = 2 * (x ^ fx) + (y ^ fy)
            cps.append(pltpu.make_async_remote_copy(
                src_ref=src_ref.at[theirs], dst_ref=recv.at[k],
                send_sem=ici_send.at[k], recv_sem=ici_recv.at[k],
                device_id=(x ^ fx, y ^ fy, c), device_id_type=MESH))
        for cp in cps:
            cp.start()
        acc = own_ref[mine]
        for k, cp in enumerate(cps):
            cp.wait_recv()
            acc = acc + recv[k].astype(F32)
        out_ref[c] = acc
        to_sibling = pltpu.make_async_remote_copy(
            src_ref=out_ref.at[c], dst_ref=out_ref.at[c], send_sem=d2d_send, recv_sem=d2d_recv,
            device_id=(x, y, 1 - c), device_id_type=MESH)
        to_sibling.start()
        pltpu.make_async_remote_copy(
            src_ref=out_ref.at[1 - c], dst_ref=out_ref.at[1 - c], send_sem=d2d_send, recv_sem=d2d_recv,
            device_id=(x, y, 1 - c), device_id_type=MESH).wait_recv()
        to_sibling.wait_send()
        for cp in cps:
            cp.wait_send()

    vm = pl.BlockSpec(memory_space=pltpu.VMEM)
    return pl.pallas_call(
        body, name="rs_chips",
        out_shape=jax.ShapeDtypeStruct((2,) + part_f32.shape[1:], F32),
        in_specs=[vm, vm], out_specs=vm,
        scratch_shapes=[pltpu.VMEM((3,) + part_bf16.shape[1:], BF16), pltpu.SemaphoreType.DMA((3,)),
                        pltpu.SemaphoreType.DMA((3,)), pltpu.SemaphoreType.DMA, pltpu.SemaphoreType.DMA],
        compiler_params=pltpu.CompilerParams(vmem_limit_bytes=VMEM_LIMIT),
    )(part_f32, part_bf16)


def _small_rows(vecs):
    out, at = [], 0
    for vec in vecs:
        rows = vec.shape[1] // LANES
        out.append((rows, at))
        at += rows
    assert at == SMALL_USED
    return out


def _small_allreduce(vecs, loss_acc):
    n = len(vecs)
    layout = _small_rows(vecs)

    def body(*refs):
        loss_ref, out_ref, stage, recv, send_sems, recv_sems = refs[n:]
        for vec_ref, (rows, at) in zip(refs[:n], layout):
            for j in range(rows):
                stage[at + j:at + j + 1, :] = vec_ref[0:1, LANES * j:LANES * (j + 1)]
        stage[SMALL_USED:SMALL_ROWS, :] = loss_ref[0:SMALL_ROWS - SMALL_USED, :]
        x, y, c = lax.axis_index("x"), lax.axis_index("y"), lax.axis_index("c")
        me = 4 * x + 2 * y + c
        cps = []
        for k in range(1, 8):
            fx, fy, fc = (k >> 2) & 1, (k >> 1) & 1, k & 1
            cps.append(pltpu.make_async_remote_copy(
                src_ref=stage, dst_ref=recv.at[k - 1],
                send_sem=send_sems.at[k - 1], recv_sem=recv_sems.at[k - 1],
                device_id=(x ^ fx, y ^ fy, c ^ fc), device_id_type=MESH))
        for cp in cps:
            cp.start()
        for cp in cps:
            cp.wait()
        acc = jnp.zeros(stage.shape, F32)
        for j in range(8):
            slot = jnp.maximum((me ^ j) - 1, 0)
            acc = acc + jnp.where(me == j, stage[...], recv[slot])
        out_ref[...] = acc

    vm = pl.BlockSpec(memory_space=pltpu.VMEM)
    shape = (SMALL_ROWS, LANES)
    return pl.pallas_call(
        body, name="small_allreduce",
        out_shape=jax.ShapeDtypeStruct(shape, F32),
        in_specs=[vm] * (n + 1), out_specs=vm,
        scratch_shapes=[pltpu.VMEM(shape, F32), pltpu.VMEM((7,) + shape, F32), pltpu.SemaphoreType.DMA((7,)),
                        pltpu.SemaphoreType.DMA((7,))],
    )(*vecs, loss_acc)


def _adamw_small(ws, g_packed, ms, vs):
    n = len(ws)
    layout = _small_rows(ws)

    def body(*refs):
        g_ref = refs[3 * n]
        outs = refs[3 * n + 1:]
        for i, (rows, at) in enumerate(layout):
            w_ref, m_ref, v_ref = refs[i], refs[n + i], refs[2 * n + i]
            go_ref, d_ref, nm_ref, nv_ref = (outs[k * n + i] for k in range(4))
            for j in range(rows):
                lanes = slice(LANES * j, LANES * (j + 1))
                gg = g_ref[at + j:at + j + 1, :]
                nm = B1 * m_ref[0:1, lanes] + (1.0 - B1) * gg
                nv = B2 * v_ref[0:1, lanes] + (1.0 - B2) * (gg * gg)
                m_hat = nm / (1.0 - B1 ** STEP)
                v_hat = nv / (1.0 - B2 ** STEP)
                go_ref[0:1, lanes] = gg
                d_ref[0:1, lanes] = -LR * (m_hat / (jnp.sqrt(v_hat) + ADAM_EPS) + WD * w_ref[0:1, lanes])
                nm_ref[0:1, lanes] = nm
                nv_ref[0:1, lanes] = nv

    vm = pl.BlockSpec(memory_space=pltpu.VMEM)
    sds = [jax.ShapeDtypeStruct(w.shape, F32) for w in ws]
    outs = pl.pallas_call(
        body, name="adamw_small", out_shape=sds * 4, in_specs=[vm] * (3 * n + 1), out_specs=[vm] * (4 * n),
    )(*ws, *ms, *vs, g_packed)
    return outs[:n], outs[n:2 * n], outs[2 * n:3 * n], outs[3 * n:]


ADAM_ROWS = 64


def _adamw(ws, gs, ms, vs, name):
    n = len(ws)

    def body(*refs):
        for i in range(n):
            w_ref, g_ref, m_ref, v_ref = (refs[k * n + i] for k in range(4))
            d_ref, nm_ref, nv_ref = (refs[(4 + k) * n + i] for k in range(3))
            rows = min(ADAM_ROWS, w_ref.shape[0])

            def chunk(r, _):
                at = pl.ds(pl.multiple_of(r * rows, SUBLANES), rows)
                gg = g_ref[at, :]
                nm = B1 * m_ref[at, :] + (1.0 - B1) * gg
                nv = B2 * v_ref[at, :] + (1.0 - B2) * (gg * gg)
                m_hat = nm / (1.0 - B1 ** STEP)
                v_hat = nv / (1.0 - B2 ** STEP)
                d_ref[at, :] = -LR * (m_hat / (jnp.sqrt(v_hat) + ADAM_EPS) + WD * w_ref[at, :])
                nm_ref[at, :] = nm
                nv_ref[at, :] = nv
                return 0

            lax.fori_loop(0, w_ref.shape[0] // rows, chunk, 0)

    vm = pl.BlockSpec(memory_space=pltpu.VMEM)
    sds = [jax.ShapeDtypeStruct(w.shape, F32) for w in ws]
    outs = pl.pallas_call(
        body, name=name, out_shape=sds * 3, in_specs=[vm] * (4 * n), out_specs=[vm] * (3 * n),
        compiler_params=pltpu.CompilerParams(vmem_limit_bytes=VMEM_LIMIT),
    )(*ws, *gs, *ms, *vs)
    return outs[:n], outs[n:2 * n], outs[2 * n:]


def _pre_fwd(x, pos, invf, gpre, wp, gq, wuq, gkv, wukv, mu, w0, w2p, a0, a2p, k_k, k_a, bo):
    bsz, t, _ = x.shape
    nt = t // TT

    def body(x_ref, pos_ref, invf_ref, gpre_ref, wp_ref, gq_ref, wuq_ref, gkv_ref, wukv_ref, mu_ref, w0_ref,
             w2p_ref, a0_ref, a2p_ref, kk_ref, ka_ref, bo_ref,
             u_ref, pp_ref, q_ref, k_ref, v_ref, r_o, w_o, kp_o, vv_o, al_o, be_o, carry):
        i = pl.program_id(1)
        u, _, _ = _rms(x_ref[0], gpre_ref[...], D)
        ub = u.astype(BF16)
        u_ref[0] = ub
        p = _dot(ub, wp_ref[...])
        pp_ref[0] = p
        prw = p[:, RW0:DP]

        @pl.when(i == 0)
        def _():
            carry[...] = jnp.zeros(carry.shape, F32)

        ps, _ = _shift_mix(prw, carry[7:8, :], mu_ref[...])
        carry[...] = prw[TT - 8:TT, :]

        g = _rw_gates(ps, w0_ref[...], w2p_ref[...], a0_ref[...], a2p_ref[...], kk_ref[...], ka_ref[...],
                      bo_ref[...])
        r_o[0] = g["r"]
        w_o[0] = g["w"]
        kp_o[0] = g["kp"]
        vv_o[0] = g["v"]
        al_o[0] = -g["kk"]
        be_o[0] = g["kk"] * g["a"]

        cqn, _, _ = _rms(p[:, CQ0:CQ0 + 256], gq_ref[...], 256)
        q = _dot(cqn.astype(BF16), wuq_ref[...])
        ckvn, _, _ = _rms(p[:, CKV0:CKV0 + 128], gkv_ref[...], 128)
        kv = _dot(ckvn.astype(BF16), wukv_ref[...])
        ang = pos_ref[0] * invf_ref[...]
        cs, sn = jnp.cos(ang), jnp.sin(ang)
        lane = lax.broadcasted_iota(jnp.int32, cs.shape, 1)
        kr = ps[:, 1536:1536 + LANES]
        kr = jnp.where(lane < 64, kr * cs + _rot(kr) * sn, 0.0).astype(BF16)
        for h in range(HEADS):
            qr = q[:, 256 * h + 128:256 * h + 256]
            q_ref[0, :, 256 * h:256 * h + 128] = q[:, 256 * h:256 * h + 128].astype(BF16)
            q_ref[0, :, 256 * h + 128:256 * h + 256] = (qr * cs + _rot(qr) * sn).astype(BF16)
            k_ref[0, :, 256 * h:256 * h + 128] = kv[:, 128 * h:128 * h + 128].astype(BF16)
            k_ref[0, :, 256 * h + 128:256 * h + 256] = kr
        v_ref[0] = kv[:, 512:1024].astype(BF16)

    tok = lambda c: pl.BlockSpec((1, TT, c), lambda b, i: (b, i, 0))
    full = lambda a: _full(a.shape)
    ins = (x, pos, invf, gpre, wp, gq, wuq, gkv, wukv, mu, w0, w2p, a0, a2p, k_k, k_a, bo)
    in_specs = [tok(D), tok(1)] + [full(a) for a in ins[2:]]
    sd = lambda c, dt: jax.ShapeDtypeStruct((bsz, t, c), dt)
    out_shape = [sd(D, BF16), sd(DP, F32), sd(1024, BF16), sd(1024, BF16), sd(512, BF16)] + [sd(RW, F32)] * 6
    out_specs = [tok(D), tok(DP), tok(1024), tok(1024), tok(512)] + [tok(RW)] * 6
    return pl.pallas_call(
        body, name="pre_fwd", grid=(bsz, nt), out_shape=out_shape, in_specs=in_specs, out_specs=out_specs,
        scratch_shapes=[pltpu.VMEM((8, NRW), F32)],
        compiler_params=_cparams(("arbitrary", "arbitrary")),
    )(*ins)


def _attn_fwd(q, k, v):
    bsz, t, _ = q.shape
    nq = t // TQ

    hps = HEADS

    def body(q_ref, k_ref, v_ref, o_ref, lse_ref):
        i = pl.program_id(2)

        def step(j, carry, diagonal):
            at = pl.ds(pl.multiple_of(j * TQ, TQ), TQ)
            out = []
            for hh in range(hps):
                m, l, acc = carry[hh]
                s = _dot_nt(q_ref[0, :, 256 * hh:256 * (hh + 1)], k_ref[0, at, 256 * hh:256 * (hh + 1)]) * SCALE
                if diagonal:
                    s = jnp.where(lax.broadcasted_iota(jnp.int32, (TQ, TQ), 1)
                                  <= lax.broadcasted_iota(jnp.int32, (TQ, TQ), 0), s, -1e30)
                mn = jnp.maximum(m, jnp.max(s, axis=1, keepdims=True))
                p = jnp.exp(s - mn)
                al = jnp.exp(m - mn)
                l = al * l + jnp.sum(p, axis=1, keepdims=True)
                acc = al * acc + _dot(p.astype(BF16), v_ref[0, at, LANES * hh:LANES * (hh + 1)])
                out.append((mn, l, acc))
            return tuple(out)

        start = (jnp.full((TQ, 1), -1e30, F32), jnp.zeros((TQ, 1), F32), jnp.zeros((TQ, LANES), F32))
        before = lax.fori_loop(0, i, lambda j, carry: step(j, carry, False), (start,) * hps)
        for hh, (m, l, acc) in enumerate(step(i, before, True)):
            o_ref[0, :, LANES * hh:LANES * (hh + 1)] = acc / l
            lse_ref[0, hh] = jnp.broadcast_to(m + jnp.log(l), (TQ, LANES))

    return pl.pallas_call(
        body, name="attn_fwd", grid=(bsz, HEADS // hps, nq),
        out_shape=[jax.ShapeDtypeStruct((bsz, t, 512), F32), jax.ShapeDtypeStruct((bsz, HEADS, t, LANES), F32)],
        in_specs=[pl.BlockSpec((1, TQ, 256 * hps), lambda b, h, i: (b, i, h)),
                  pl.BlockSpec((1, t, 256 * hps), lambda b, h, i: (b, 0, h)),
                  pl.BlockSpec((1, t, LANES * hps), lambda b, h, i: (b, 0, h))],
        out_specs=[pl.BlockSpec((1, TQ, LANES * hps), lambda b, h, i: (b, i, h)),
                   pl.BlockSpec((1, hps, TQ, LANES), lambda b, h, i: (b, h, i, 0))],
        compiler_params=_cparams(("parallel", "parallel", "arbitrary")),
    )(q, k, v)


def _attn_bwd(q, k, v, o, lse, do):
    bsz, t, _ = q.shape
    nq = t // TQ

    def body(q_ref, k_ref, v_ref, o_ref, lse_ref, do_ref, dq_ref, dk_ref, dv_ref, dl_ref):
        j = pl.program_id(2)

        @pl.when(j == 0)
        def _():
            def prep(i, _):
                at = pl.ds(pl.multiple_of(i * TQ, TQ), TQ)
                for hh in range(2):
                    lanes = slice(LANES * hh, LANES * (hh + 1))
                    dl_ref[hh, at, :] = jnp.broadcast_to(
                        jnp.sum(do_ref[0, at, lanes] * o_ref[0, at, lanes], axis=1, keepdims=True), (TQ, LANES))
                return 0

            lax.fori_loop(0, nq, prep, 0)
            dq_ref[0] = jnp.zeros((t, 512), F32)

        def q_tile(i, carry, diagonal):
            atq = pl.ds(pl.multiple_of(i * TQ, TQ), TQ)
            out = []
            for hh in range(2):
                dk, dv = carry[hh]
                wide, narrow = slice(256 * hh, 256 * (hh + 1)), slice(LANES * hh, LANES * (hh + 1))
                qt, kt, vt = q_ref[0, atq, wide], k_ref[0, :, wide], v_ref[0, :, narrow]
                dob = do_ref[0, atq, narrow].astype(BF16)
                s = _dot_nt(qt, kt) * SCALE
                if diagonal:
                    s = jnp.where(lax.broadcasted_iota(jnp.int32, (TQ, TQ), 1)
                                  <= lax.broadcasted_iota(jnp.int32, (TQ, TQ), 0), s, -1e30)
                p = jnp.exp(s - lse_ref[0, hh, atq, :][:, 0:1])
                dv = dv + _dot_tn(p.astype(BF16), dob)
                dp = _dot_nt(dob, vt)
                ds = (p * (dp - dl_ref[hh, atq, :][:, 0:1]) * SCALE).astype(BF16)
                dk = dk + _dot_tn(ds, qt)
                dq_ref[0, atq, wide] += _dot(ds, kt)
                out.append((dk, dv))
            return tuple(out)

        zero = (jnp.zeros((TQ, 256), F32), jnp.zeros((TQ, LANES), F32))
        first = q_tile(j, (zero, zero), True)
        done = lax.fori_loop(j + 1, nq, lambda i, carry: q_tile(i, carry, False), first)
        for hh, (dk, dv) in enumerate(done):
            dk_ref[0, :, 256 * hh:256 * (hh + 1)] = dk
            dv_ref[0, :, LANES * hh:LANES * (hh + 1)] = dv

    whole = lambda c: pl.BlockSpec((1, t, c), lambda b, h, j: (b, 0, h))
    tile = lambda c: pl.BlockSpec((1, TQ, c), lambda b, h, j: (b, j, h))
    return pl.pallas_call(
        body, name="attn_bwd", grid=(bsz, HEADS // 2, nq),
        out_shape=[jax.ShapeDtypeStruct((bsz, t, 1024), F32), jax.ShapeDtypeStruct((bsz, t, 1024), F32),
                   jax.ShapeDtypeStruct((bsz, t, 512), F32)],
        in_specs=[whole(512), tile(512), tile(256), whole(256),
                  pl.BlockSpec((1, 2, t, LANES), lambda b, h, j: (b, h, 0, 0)), whole(256)],
        out_specs=[whole(512), tile(512), tile(256)],
        scratch_shapes=[pltpu.VMEM((2, t, LANES), F32)],
        compiler_params=_cparams(("parallel", "parallel", "arbitrary")),
    )(q, k, v, o, lse, do)


RW_HEADS = 8
CH = 32


def _lane_split(bsz):
    vs = LANES // (bsz * RW_HEADS)
    return vs, 64 // vs


def _gather_matrix(bsz):
    group = bsz * RW_HEADS
    vs = LANES // group
    half = (RW_HEADS // 2) * bsz * SPREAD_STEPS
    p = np.zeros((SPREAD_STEPS // vs * LANES, 2 * half), np.float32)
    for g2 in range(SPREAD_STEPS // vs):
        for j in range(vs):
            for b in range(bsz):
                for h in range(RW_HEADS):
                    hp, hpar = h // 2, h % 2
                    p[g2 * LANES + j * group + b * RW_HEADS + h,
                      hpar * half + (hp * bsz + b) * SPREAD_STEPS + g2 * vs + j] = 1.0
    return jnp.asarray(np.concatenate([p] * 3, axis=0), BF16)


def _gather_k(ys, bsz):
    vs = LANES // (bsz * RW_HEADS)
    assert (RW_HEADS // 2) * bsz * SPREAD_STEPS == LANES, "the transposed tile must be 128 lanes wide"
    tg = ys[0].shape[0]
    n = len(ys)
    ngrp = GATHER_BLOCK // SPREAD_STEPS
    per = SPREAD_STEPS // vs

    def body(*refs):
        pm = refs[n][...]
        for y_ref, o_ref in zip(refs[:n], refs[n + 1:]):
            lhs = jnp.concatenate(
                [jnp.concatenate(_split3(jnp.concatenate([y_ref[per * m + g2] for g2 in range(per)], axis=1)), axis=1)
                 for m in range(ngrp)], axis=0)
            a = _dot(lhs, pm)
            for m in range(ngrp):
                am = a[64 * m:64 * (m + 1)]
                bt = jnp.concatenate([am[:, 0:LANES], am[:, LANES:2 * LANES]], axis=0).T
                for hp in range(RW_HEADS // 2):
                    for b in range(bsz):
                        at = (hp * bsz + b) * SPREAD_STEPS
                        o_ref[b, SPREAD_STEPS * m:SPREAD_STEPS * (m + 1), LANES * hp:LANES * (hp + 1)] = \
                            bt[at:at + SPREAD_STEPS]

    pm = _gather_matrix(bsz)
    return pl.pallas_call(
        body, name="wkv_gather", grid=(tg * vs // GATHER_BLOCK,),
        out_shape=[jax.ShapeDtypeStruct((bsz, tg * vs, RW), F32)] * n,
        in_specs=[pl.BlockSpec((GATHER_BLOCK // vs, 64, LANES), lambda i: (i, 0, 0))] * n + [_full(pm.shape)],
        out_specs=[pl.BlockSpec((bsz, GATHER_BLOCK, RW), lambda i: (0, i, 0))] * n,
        compiler_params=_cparams(("parallel",)),
    )(*ys, pm)


def _to_v(x):
    bsz, t, _ = x.shape
    vs, vq = _lane_split(bsz)
    return jnp.transpose(x.reshape(bsz, t, RW_HEADS, vs, vq), (1, 4, 3, 0, 2)).reshape(t, vq, LANES)


def _from_v(y, bsz):
    t = y.shape[0]
    vs, vq = _lane_split(bsz)
    return jnp.transpose(y.reshape(t, vq, vs, bsz, RW_HEADS), (3, 0, 4, 2, 1)).reshape(bsz, t, RW)


def _ksum(a):
    return jnp.sum(a, axis=0, keepdims=True)


def _fold(a, group):
    sh = LANES // 2
    while sh >= group:
        a = a + pltpu.roll(a, sh, 1)
        sh //= 2
    return a


def _lane_group(shape, group):
    return lax.broadcasted_iota(jnp.int32, shape, 1) // group


SPREAD_STEPS = 8
SPREAD_BLOCK = 64
GATHER_BLOCK = 128


def _spread_matrix(bsz):
    group = bsz * RW_HEADS
    vs = LANES // group
    rows = (RW_HEADS // 2) * bsz * SPREAD_STEPS
    q = np.zeros((2, rows, SPREAD_STEPS * LANES), np.float32)
    for hpar in range(2):
        for hp in range(RW_HEADS // 2):
            for b in range(bsz):
                for st in range(SPREAD_STEPS):
                    row = (hp * bsz + b) * SPREAD_STEPS + st
                    for s in range(vs):
                        q[hpar, row, st * LANES + s * group + b * RW_HEADS + 2 * hp + hpar] = 1.0
    return jnp.asarray(np.concatenate([q[0], q[1]] * 3, axis=0), BF16)


def _spread_k(xs):
    bsz, t, _ = xs[0].shape
    assert (RW_HEADS // 2) * bsz * SPREAD_STEPS == LANES, "the transposed tile must be 128 lanes wide"
    n = len(xs)
    ngrp = SPREAD_BLOCK // SPREAD_STEPS

    def body(*refs):
        qm = refs[n][...]
        for x_ref, o_ref in zip(refs[:n], refs[n + 1:]):
            cols = [[] for _ in range(6)]
            for m in range(ngrp):
                at = slice(SPREAD_STEPS * m, SPREAD_STEPS * (m + 1))
                x8 = jnp.concatenate([x_ref[b, at, LANES * hp:LANES * (hp + 1)]
                                      for hp in range(RW_HEADS // 2) for b in range(bsz)], axis=0)
                for pi, piece in enumerate(_split3(x8.T)):
                    cols[2 * pi].append(piece[0:64])
                    cols[2 * pi + 1].append(piece[64:128])
            lhs = jnp.concatenate([jnp.concatenate(c, axis=0) for c in cols], axis=1)
            y = _dot(lhs, qm)
            for m in range(ngrp):
                for st in range(SPREAD_STEPS):
                    o_ref[SPREAD_STEPS * m + st] = y[64 * m:64 * (m + 1), LANES * st:LANES * (st + 1)]

    qm = _spread_matrix(bsz)
    return pl.pallas_call(
        body, name="wkv_spread", grid=(t // SPREAD_BLOCK,),
        out_shape=[jax.ShapeDtypeStruct((t, 64, LANES), F32)] * n,
        in_specs=[pl.BlockSpec((bsz, SPREAD_BLOCK, RW), lambda i: (0, i, 0))] * n + [_full(qm.shape)],
        out_specs=[pl.BlockSpec((SPREAD_BLOCK, 64, LANES), lambda i: (i, 0, 0))] * n,
        compiler_params=_cparams(("parallel",)),
    )(*xs, qm)


def _wkv_fwd(r, w, kp, al, be, v):
    t, vq = v.shape[0], v.shape[1]

    def body(r_ref, w_ref, kp_ref, al_ref, be_ref, v_ref, y_ref, a_ref, u_ref, st_ref):
        @pl.when(pl.program_id(0) == 0)
        def _():
            st_ref[...] = jnp.zeros(st_ref.shape, F32)

        def step(tl, _):
            rv, wv, kv, av, bv = r_ref[tl], w_ref[tl], kp_ref[tl], al_ref[tl], be_ref[tl]
            vals = v_ref[tl]
            yrows, urows = [], []
            for q in range(vq):
                s = st_ref[q]
                u = _ksum(s * av)
                s = s * wv + bv * u + kv * vals[q:q + 1]
                st_ref[q] = s
                a_ref[tl, q] = s
                urows.append(u)
                yrows.append(_ksum(s * rv))
            y_ref[tl] = jnp.concatenate(yrows, axis=0)
            u_ref[tl] = jnp.concatenate(urows, axis=0)
            return 0

        lax.fori_loop(0, CH, step, 0)

    kspec = pl.BlockSpec((CH, 64, LANES), lambda i: (i, 0, 0))
    vspec = pl.BlockSpec((CH, vq, LANES), lambda i: (i, 0, 0))
    vsd = jax.ShapeDtypeStruct((t, vq, LANES), F32)
    return pl.pallas_call(
        body, name="wkv_fwd", grid=(t // CH,),
        out_shape=[vsd, jax.ShapeDtypeStruct((t, vq, 64, LANES), F32), vsd],
        in_specs=[kspec] * 5 + [vspec],
        out_specs=[vspec, pl.BlockSpec((CH, vq, 64, LANES), lambda i: (i, 0, 0, 0)), vspec],
        scratch_shapes=[pltpu.VMEM((vq, 64, LANES), F32)],
        compiler_params=_cparams(("arbitrary",)),
    )(r, w, kp, al, be, v)


def _wkv_bwd(r, w, kp, al, be, v, dy, states, u):
    t, vq = v.shape[0], v.shape[1]
    vs = 64 // vq
    group = LANES // vs
    n = t // CH
    ng = CH // vs

    def body(r_ref, w_ref, kp_ref, al_ref, be_ref, v_ref, dy_ref, u_ref, a_ref, ap_ref,
             dr_ref, dw_ref, dkp_ref, dal_ref, dbe_ref, dv_ref, ds_ref):
        @pl.when(pl.program_id(0) == 0)
        def _():
            ds_ref[...] = jnp.zeros(ds_ref.shape, F32)

        earliest = pl.program_id(0) == n - 1

        def reverse(i, _):
            g = ng - 1 - i
            grp = _lane_group((64, LANES), group)
            outs = None
            for j in reversed(range(vs)):
                tl = g * vs + j
                rv, wv, kv, av, bv = r_ref[tl], w_ref[tl], kp_ref[tl], al_ref[tl], be_ref[tl]
                vals, dys, us = v_ref[tl], dy_ref[tl], u_ref[tl]
                acc = None
                dvrows = []
                for q in range(vq):
                    if j > 0:
                        s_prev = a_ref[tl - 1, q]
                    else:
                        before = jnp.where(earliest, 0.0, ap_ref[0, q])
                        s_prev = jnp.where(g == 0, before, a_ref[jnp.maximum(tl - 1, 0), q])
                    dyq = dys[q:q + 1]
                    ds = ds_ref[q] + rv * dyq
                    c = _ksum(ds * bv)
                    dvrows.append(_ksum(ds * kv))
                    terms = (a_ref[tl, q] * dyq, ds * s_prev, ds * vals[q:q + 1], s_prev * c, ds * us[q:q + 1])
                    acc = terms if acc is None else tuple(a + b for a, b in zip(acc, terms))
                    ds_ref[q] = ds * wv + av * c
                dv_ref[tl] = jnp.concatenate(dvrows, axis=0)
                summed = [_fold(a, group) for a in acc]
                outs = summed if outs is None else [jnp.where(grp == j, f, o) for f, o in zip(summed, outs)]
            for ref, o in zip((dr_ref, dw_ref, dkp_ref, dal_ref, dbe_ref), outs):
                ref[g] = o
            return 0

        lax.fori_loop(0, ng, reverse, 0)

    kspec = pl.BlockSpec((CH, 64, LANES), lambda i: (n - 1 - i, 0, 0))
    gspec = pl.BlockSpec((ng, 64, LANES), lambda i: (n - 1 - i, 0, 0))
    vspec = pl.BlockSpec((CH, vq, LANES), lambda i: (n - 1 - i, 0, 0))
    ksd = jax.ShapeDtypeStruct((t // vs, 64, LANES), F32)
    return pl.pallas_call(
        body, name="wkv_bwd", grid=(n,),
        out_shape=[ksd] * 5 + [jax.ShapeDtypeStruct((t, vq, LANES), F32)],
        in_specs=[kspec] * 5 + [vspec, vspec, vspec,
                                pl.BlockSpec((CH, vq, 64, LANES), lambda i: (n - 1 - i, 0, 0, 0)),
                                pl.BlockSpec((1, vq, 64, LANES), lambda i: (jnp.maximum((n - 1 - i) * CH - 1, 0), 0, 0, 0))],
        out_specs=[gspec] * 5 + [vspec],
        scratch_shapes=[pltpu.VMEM((vq, 64, LANES), F32)],
        compiler_params=_cparams(("arbitrary",)),
    )(r, w, kp, al, be, v, dy, u, states, states)


def _post(x, tgt, pp, o, yw, r, kp, v, ln_g, ln_b, r_k, wo, wot, gpost, bo):
    bsz, t, _ = x.shape
    tt = TT_VPU
    nt = t // tt

    def body(x_ref, tgt_ref, z_ref, o_ref, yw_ref, r_ref, kp_ref, v_ref, lng_ref, lnb_ref, rk_ref, wo_ref, wot_ref,
             gpost_ref, bo_ref,
             dh_ref, dz_ref, dym_ref, dyw_ref, dbon_ref, loss_ref, dwo_ref, dgpost_ref, dlng_ref, dlnb_ref, drk_ref):
        first = (pl.program_id(0) == 0) & (pl.program_id(1) == 0)

        @pl.when(first)
        def _():
            for ref in (loss_ref, dwo_ref, dgpost_ref, dlng_ref, dlnb_ref, drk_ref):
                ref[...] = jnp.zeros(ref.shape, F32)

        bo_m = bo_ref[...]
        seg = lambda a: _seg(a, bo_m)
        rowsum = lambda a: jnp.sum(a, axis=0, keepdims=True)
        ywv, rv, kpv, vv = yw_ref[0], r_ref[0], kp_ref[0], v_ref[0]
        ln_g, r_k = lng_ref[...], rk_ref[...]
        mean = seg(ywv) * (1.0 / 64)
        yc = ywv - mean
        rstd = lax.rsqrt(seg(yc * yc) * (1.0 / 64) + GN_EPS)
        yhat = yc * rstd
        sb = seg(rv * kpv * r_k)
        y_rw = yhat * ln_g + lnb_ref[...] + sb * vv
        z = z_ref[0]
        sig = _sigmoid(z)
        sz = z * sig
        ycat = jnp.concatenate([o_ref[0], y_rw], axis=1)
        ycg = (ycat * sz).astype(BF16)
        out = _dot(ycg, wo_ref[...])
        hn, nx, rstd_o = _rms(out, gpost_ref[...], D)
        err = x_ref[0] + hn - tgt_ref[0]
        loss_ref[...] += jnp.sum(err * err) * (0.5 / D)
        dh = err * (1.0 / D)
        dh_ref[0] = dh
        dout, dgp = _rms_bwd(dh, nx, rstd_o, gpost_ref[...], D)
        dgpost_ref[...] += dgp
        doutb = dout.astype(BF16)
        dwo_ref[...] += _dot_tn(ycg, doutb)
        dycg = _dot(doutb, wot_ref[...])
        dz_ref[0] = dycg * ycat * (sig * (1.0 + z * (1.0 - sig)))
        dycat = dycg * sz
        dym_ref[0] = dycat[:, 0:512]
        dy_rw = dycat[:, 512:1024]
        dlnb_ref[...] += rowsum(dy_rw)
        dlng_ref[...] += rowsum(dy_rw * yhat)
        dyhat = dy_rw * ln_g
        dyw_ref[0] = rstd * (dyhat - seg(dyhat) * (1.0 / 64) - yhat * (seg(dyhat * yhat) * (1.0 / 64)))
        dsb = seg(dy_rw * vv)
        drk_ref[...] += rowsum(dsb * rv * kpv)
        dbon_ref[0, :, 0:512] = dsb * kpv * r_k
        dbon_ref[0, :, 512:1024] = dsb * rv * r_k
        dbon_ref[0, :, 1024:1536] = dy_rw * sb

    tok = lambda c: pl.BlockSpec((1, tt, c), lambda b, i: (b, i, 0))
    full = lambda a: _full(a.shape)
    ins = (x, tgt, pp, o, yw, r, kp, v, ln_g, ln_b, r_k, wo, wot, gpost, bo)
    in_specs = [tok(D), tok(D), tok(1024)] + [tok(512)] * 5 + [full(a) for a in ins[8:]]
    sd = lambda c: jax.ShapeDtypeStruct((bsz, t, c), F32)
    vec = lambda c: jax.ShapeDtypeStruct((1, c), F32)
    out_shape = [sd(D), sd(1024), sd(512), sd(512), sd(1536), jax.ShapeDtypeStruct((8, LANES), F32),
                 jax.ShapeDtypeStruct((1024, 1024), F32), vec(D), vec(512), vec(512), vec(512)]
    out_specs = [tok(D), tok(1024), tok(512), tok(512), tok(1536), _resident((8, LANES)), _resident((1024, 1024)),
                 _resident((1, D)), _resident((1, 512)), _resident((1, 512)), _resident((1, 512))]
    return pl.pallas_call(
        body, name="post", grid=(bsz, nt), out_shape=out_shape, in_specs=in_specs, out_specs=out_specs,
        compiler_params=_cparams(("arbitrary", "arbitrary")),
    )(*ins)


def _pre_bwd_a(pp, pos, invf, cqkv_w, mu, w0, w2p, w2pt, a0, a2p, a2pt, k_k, k_a, bo,
               dq, dk, dva, dwkv, dbon):
    gq, wuqt, gkv, wukvt = cqkv_w
    bsz, t, _ = pp.shape
    tt = TT_VPU
    nt = t // tt
    dr_w, dw_w, dkp_w, dv_w, dal_w, dbe_w = dwkv

    def body(pp_ref, pos_ref, invf_ref, gq_ref, wuqt_ref, gkv_ref, wukvt_ref, mu_ref, w0_ref, w2p_ref, w2pt_ref,
             a0_ref, a2p_ref, a2pt_ref, kk_ref, ka_ref, bo_ref, dq_ref, dk_ref, dva_ref,
             dr_ref, dw_ref, dkp_ref, dv_ref, dal_ref, dbe_ref, dbon_ref,
             da_ref, dwuq_ref, dwukv_ref, dw2p_ref, da2p_ref, dgq_ref, dgkv_ref, dmu_ref, dw0_ref, da0_ref,
             dkk_ref, dka_ref, carry):
        i = pl.program_id(1)
        first = (pl.program_id(0) == 0) & (i == 0)

        @pl.when(first)
        def _():
            for ref in (dwuq_ref, dwukv_ref, dw2p_ref, da2p_ref, dgq_ref, dgkv_ref, dmu_ref, dw0_ref, da0_ref,
                        dkk_ref, dka_ref):
                ref[...] = jnp.zeros(ref.shape, F32)

        bo_m = bo_ref[...]
        rowsum = lambda a: jnp.sum(a, axis=0, keepdims=True)
        prw = pp_ref[0, :, RW0:DP]

        @pl.when(i == 0)
        def _():
            carry[...] = jnp.zeros(carry.shape, F32)

        ps, sh = _shift_mix(prw, carry[7:8, :], mu_ref[...])
        carry[...] = prw[tt - 8:tt, :]
        k_k, k_a = kk_ref[...], ka_ref[...]
        g = _rw_gates(ps, w0_ref[...], w2p_ref[...], a0_ref[...], a2p_ref[...], k_k, k_a, bo_m)
        a, kk, k = g["a"], g["kk"], g["k"]
        dr = dr_ref[0] + dbon_ref[0, :, 0:512]
        dkp = dkp_ref[0] + dbon_ref[0, :, 512:1024]
        dv = dv_ref[0] + dbon_ref[0, :, 1024:1536]
        dbe = dbe_ref[0]
        dkk = dbe * a - dal_ref[0]
        da = dbe * kk + dkp * k * k_a
        dka_ref[...] += rowsum(dkp * k * (a - 1.0))
        dm = (dkk - kk * _seg(dkk * kk, bo_m)) / g["nrm"]
        dkk_ref[...] += rowsum(dm * k)
        dk_tot = dkp * (1.0 + (a - 1.0) * k_a) + dm * k_k
        dapre = da * a * (1.0 - a)
        da0_ref[...] += rowsum(dapre)
        dapb = dapre.astype(BF16)
        da2p_ref[...] += _dot_tn(g["misc"].astype(BF16), dapb)
        dwpre = dw_ref[0] * g["w"] * (-g["e"]) * _sigmoid(-g["wpre"])
        dw0_ref[...] += rowsum(dwpre)
        dwpb = dwpre.astype(BF16)
        th = g["th"]
        dw2p_ref[...] += _dot_tn(th.astype(BF16), dwpb)
        dmisc = _dot(dapb, a2pt_ref[...]) + _dot(dwpb, w2pt_ref[...]) * (1.0 - th * th)
        ang = pos_ref[0] * invf_ref[...]
        cs, sn = jnp.cos(ang), jnp.sin(ang)
        unrope = lambda gr: gr * cs - _rot(gr * sn)
        lane = lax.broadcasted_iota(jnp.int32, cs.shape, 1)
        dkr = dk_ref[0, :, 128:256]
        for h in range(1, HEADS):
            dkr = dkr + dk_ref[0, :, 256 * h + 128:256 * h + 256]
        dkr = jnp.where(lane < 64, unrope(dkr), 0.0)
        dmisc = dmisc + jnp.concatenate([dkr, jnp.zeros_like(dkr)], axis=1)
        dqp = jnp.concatenate(
            [blk for h in range(HEADS)
             for blk in (dq_ref[0, :, 256 * h:256 * h + 128], unrope(dq_ref[0, :, 256 * h + 128:256 * h + 256]))],
            axis=1).astype(BF16)
        dkvp = jnp.concatenate([dk_ref[0, :, 256 * h:256 * h + 128] for h in range(HEADS)] + [dva_ref[0]],
                               axis=1).astype(BF16)
        cqn, cq_nx, cq_rstd = _rms(pp_ref[0, :, CQ0:CQ0 + 256], gq_ref[...], 256)
        ckvn, ckv_nx, ckv_rstd = _rms(pp_ref[0, :, CKV0:CKV0 + 128], gkv_ref[...], 128)
        dwuq_ref[...] += _dot_tn(cqn.astype(BF16), dqp)
        dwukv_ref[...] += _dot_tn(ckvn.astype(BF16), dkvp)
        dcq, dgq = _rms_bwd(_dot(dqp, wuqt_ref[...]), cq_nx, cq_rstd, gq_ref[...], 256)
        dckv, dgkv = _rms_bwd(_dot(dkvp, wukvt_ref[...]), ckv_nx, ckv_rstd, gkv_ref[...], 128)
        dgq_ref[...] += dgq
        dgkv_ref[...] += dgkv
        dps = jnp.concatenate([dr, dk_tot, dv, dmisc], axis=1)
        dmu_ref[...] += rowsum(dps * (sh - prw))
        da_ref[0, :, 0:256] = dcq
        da_ref[0, :, 256:384] = dckv
        da_ref[0, :, 384:384 + NRW] = dps

    tok = lambda c: pl.BlockSpec((1, tt, c), lambda b, i: (b, i, 0))
    full = lambda a: _full(a.shape)
    ins = (pp, pos, invf, gq, wuqt, gkv, wukvt, mu, w0, w2p, w2pt, a0, a2p, a2pt, k_k, k_a, bo,
           dq, dk, dva, dr_w, dw_w, dkp_w, dv_w, dal_w, dbe_w, dbon)
    in_specs = ([tok(DP), tok(1)] + [full(a) for a in ins[2:17]] + [tok(1024), tok(1024), tok(512)]
                + [tok(512)] * 6 + [tok(1536)])
    shp = lambda *s: jax.ShapeDtypeStruct(s, F32)
    out_shape = [shp(bsz, t, 384 + NRW), shp(256, 1024), shp(128, 1024), shp(256, 512), shp(256, 512),
                 shp(1, 256), shp(1, 128), shp(1, NRW), shp(1, 512), shp(1, 512), shp(1, 512), shp(1, 512)]
    out_specs = [tok(384 + NRW)] + [_resident(s.shape) for s in out_shape[1:]]
    return pl.pallas_call(
        body, name="pre_bwd_a", grid=(bsz, nt), out_shape=out_shape, in_specs=in_specs, out_specs=out_specs,
        scratch_shapes=[pltpu.VMEM((8, NRW), F32)],
        compiler_params=_cparams(("arbitrary", "arbitrary")),
    )(*ins)


def _pre_bwd_b(x, dh, dz, da, mu, wpt, gpre):
    bsz, t, _ = x.shape
    nt = t // TT
    nblk = t // 8

    def body(x_ref, dh_ref, dz_ref, da_ref, nxt_ref, mu_ref, wpt_ref, gpre_ref, gx_ref, dp_ref, dgpre_ref):
        i = pl.program_id(1)
        first = (pl.program_id(0) == 0) & (i == 0)

        @pl.when(first)
        def _():
            dgpre_ref[...] = jnp.zeros(dgpre_ref.shape, F32)

        mu_v = mu_ref[...]
        dps = da_ref[0, :, 384:384 + NRW]
        nxt = jnp.where(i < nt - 1, nxt_ref[0, 0:1, 384:384 + NRW], 0.0)
        row = lax.broadcasted_iota(jnp.int32, dps.shape, 0)
        up = jnp.where(row == TT - 1, nxt, pltpu.roll(dps, TT - 1, 0))
        dprw = dps * (1.0 - mu_v) + up * mu_v
        dp = jnp.concatenate([dz_ref[0], da_ref[0, :, 0:384], dprw], axis=1).astype(BF16)
        dp_ref[0] = dp
        du = _dot(dp, wpt_ref[...])
        _, nx, rstd = _rms(x_ref[0], gpre_ref[...], D)
        dx, dg = _rms_bwd(du, nx, rstd, gpre_ref[...], D)
        dgpre_ref[...] += dg
        gx_ref[0] = dh_ref[0] + dx

    tok = lambda c: pl.BlockSpec((1, TT, c), lambda b, i: (b, i, 0))
    nxt_spec = pl.BlockSpec((1, 8, 384 + NRW), lambda b, i: (b, jnp.minimum((i + 1) * (TT // 8), nblk - 1), 0))
    ins = (x, dh, dz, da, da, mu, wpt, gpre)
    return pl.pallas_call(
        body, name="pre_bwd_b", grid=(bsz, nt),
        out_shape=[jax.ShapeDtypeStruct((bsz, t, D), F32), jax.ShapeDtypeStruct((bsz, t, DP), BF16),
                   jax.ShapeDtypeStruct((1, D), F32)],
        in_specs=[tok(D), tok(D), tok(1024), tok(384 + NRW), nxt_spec, _full(mu.shape), _full(wpt.shape),
                  _full(gpre.shape)],
        out_specs=[tok(D), tok(DP), _resident((1, D))],
        compiler_params=_cparams(("arbitrary", "arbitrary")),
    )(*ins)


def _tn_matmul(a, b, bn, name, bk=512):
    kdim, m = a.shape
    _, n = b.shape
    nk = kdim // bk

    def body(a_ref, b_ref, o_ref):
        @pl.when(pl.program_id(1) == 0)
        def _():
            o_ref[...] = jnp.zeros(o_ref.shape, F32)

        o_ref[...] += _dot_tn(a_ref[...], b_ref[...])

    return pl.pallas_call(
        body, name=name, grid=(n // bn, nk),
        out_shape=jax.ShapeDtypeStruct((m, n), F32),
        in_specs=[pl.BlockSpec((bk, m), lambda j, kk: (kk, 0)), pl.BlockSpec((bk, bn), lambda j, kk: (kk, j))],
        out_specs=pl.BlockSpec((m, bn), lambda j, kk: (0, j)),
        compiler_params=_cparams(("parallel", "arbitrary")),
    )(a, b)


SHARDED = ("w_in", "mla_w_uq", "mla_w_ukv", "rw_w2", "rw_a2", "w_out")
SMALL = ("norm_pre_g", "mla_q_norm_g", "mla_kv_norm_g", "rw_mu", "rw_w0", "rw_a0", "rw_k_k", "rw_k_a", "rw_r_k",
         "rw_ln_g", "rw_ln_b", "norm_post_g")
WEIGHTS = ("norm_pre_g", "w_in", "mla_q_norm_g", "mla_w_uq", "mla_kv_norm_g", "mla_w_ukv", "rw_mu", "rw_w0", "rw_w2",
           "rw_a0", "rw_a2", "rw_k_k", "rw_k_a", "rw_r_k", "rw_ln_g", "rw_ln_b", "w_out", "norm_post_g")


def _unpack_shard(packed, like):
    out, at = {}, 0
    for n, rows in zip(SHARDED, PACK_ROWS):
        out[n] = packed[at:at + rows].reshape(like[n].shape)
        at += rows
    return out


def _constants():
    bo = np.kron(np.eye(2, dtype=np.float32), np.ones((64, 64), np.float32))
    inv = ROPE_THETA ** (-np.arange(0, 64, 2, dtype=np.float32) / 64)
    invf = np.concatenate([inv, inv, np.zeros(64, np.float32)]).astype(np.float32)[None, :]
    return jnp.asarray(bo, BF16), jnp.asarray(invf)


def kernel(x, positions, norm_pre_g, w_in, mla_q_norm_g, mla_w_uq, mla_kv_norm_g, mla_w_ukv, rw_mu, rw_w0, rw_w2, rw_a0, rw_a2, rw_k_k, rw_k_a, rw_r_k, rw_ln_g, rw_ln_b, w_out, norm_post_g, loss_target, m_norm_pre_g, m_w_in, m_mla_q_norm_g, m_mla_w_uq, m_mla_kv_norm_g, m_mla_w_ukv, m_rw_mu, m_rw_w0, m_rw_w2, m_rw_a0, m_rw_a2, m_rw_k_k, m_rw_k_a, m_rw_r_k, m_rw_ln_g, m_rw_ln_b, m_w_out, m_norm_post_g, v_norm_pre_g, v_w_in, v_mla_q_norm_g, v_mla_w_uq, v_mla_kv_norm_g, v_mla_w_ukv, v_rw_mu, v_rw_w0, v_rw_w2, v_rw_a0, v_rw_a2, v_rw_k_k, v_rw_k_a, v_rw_r_k, v_rw_ln_g, v_rw_ln_b, v_w_out, v_norm_post_g):
    wts = dict(norm_pre_g=norm_pre_g, w_in=w_in, mla_q_norm_g=mla_q_norm_g, mla_w_uq=mla_w_uq,
               mla_kv_norm_g=mla_kv_norm_g, mla_w_ukv=mla_w_ukv, rw_mu=rw_mu, rw_w0=rw_w0, rw_w2=rw_w2, rw_a0=rw_a0,
               rw_a2=rw_a2, rw_k_k=rw_k_k, rw_k_a=rw_k_a, rw_r_k=rw_r_k, rw_ln_g=rw_ln_g, rw_ln_b=rw_ln_b, w_out=w_out,
               norm_post_g=norm_post_g)
    mom_m = dict(norm_pre_g=m_norm_pre_g, w_in=m_w_in, mla_q_norm_g=m_mla_q_norm_g, mla_w_uq=m_mla_w_uq,
                 mla_kv_norm_g=m_mla_kv_norm_g, mla_w_ukv=m_mla_w_ukv, rw_mu=m_rw_mu, rw_w0=m_rw_w0, rw_w2=m_rw_w2,
                 rw_a0=m_rw_a0, rw_a2=m_rw_a2, rw_k_k=m_rw_k_k, rw_k_a=m_rw_k_a, rw_r_k=m_rw_r_k, rw_ln_g=m_rw_ln_g,
                 rw_ln_b=m_rw_ln_b, w_out=m_w_out, norm_post_g=m_norm_post_g)
    mom_v = dict(norm_pre_g=v_norm_pre_g, w_in=v_w_in, mla_q_norm_g=v_mla_q_norm_g, mla_w_uq=v_mla_w_uq,
                 mla_kv_norm_g=v_mla_kv_norm_g, mla_w_ukv=v_mla_w_ukv, rw_mu=v_rw_mu, rw_w0=v_rw_w0, rw_w2=v_rw_w2,
                 rw_a0=v_rw_a0, rw_a2=v_rw_a2, rw_k_k=v_rw_k_k, rw_k_a=v_rw_k_a, rw_r_k=v_rw_r_k, rw_ln_g=v_rw_ln_g,
                 rw_ln_b=v_rw_ln_b, w_out=v_w_out, norm_post_g=v_norm_post_g)
    bsz, t, _ = x.shape
    bo, invf = _constants()

    g_in, g_uq, g_ukv, g_w2, g_a2, g_out = _ag_weights([wts[n][0] for n in SHARDED])
    w_in_f = jnp.transpose(g_in, (1, 0, 2)).reshape(D, D_IN)
    wp = jnp.concatenate([w_in_f[:, 2112:3136], w_in_f[:, 0:384], w_in_f[:, 448:1984], w_in_f[:, 384:448],
                          w_in_f[:, 1984:2112], jnp.zeros((D, 64), BF16)], axis=1)
    wuq = jnp.pad(jnp.transpose(g_uq, (1, 0, 2)).reshape(256, HEADS, 192), ((0, 0), (0, 0), (0, 64))).reshape(256, 1024)
    wukv = jnp.transpose(jnp.transpose(g_ukv, (1, 0, 2)).reshape(128, HEADS, 2, 128), (0, 2, 1, 3)).reshape(128, 1024)
    w2 = jnp.transpose(g_w2, (1, 0, 2)).reshape(64, RW)
    a2 = jnp.transpose(g_a2, (1, 0, 2)).reshape(64, RW)
    w2p = jnp.pad(w2, ((64, 128), (0, 0)))
    a2p = jnp.pad(a2, ((128, 64), (0, 0)))
    wo = g_out.reshape(D, D)
    mu = jnp.concatenate([rw_mu[:, 0:1536], jnp.zeros((1, 64), F32), rw_mu[:, 1536:1664], jnp.zeros((1, 64), F32)],
                         axis=1)
    r_k = rw_r_k.reshape(1, RW)
    pos = positions.astype(F32)[:, :, None]

    (u, pp, q_att, k_att, v_att, r, w, kp, v, al, be) = _pre_fwd(
        x, pos, invf, norm_pre_g, wp, mla_q_norm_g, wuq, mla_kv_norm_g, wukv, mu, rw_w0, w2p, rw_a0, a2p, rw_k_k,
        rw_k_a, bo)
    o, lse = _attn_fwd(q_att, k_att, v_att)
    rw_k = _spread_k([r, w, kp, al, be])
    v_v = _to_v(v)
    yw_v, states, u_v = _wkv_fwd(*rw_k, v_v)
    yw = _from_v(yw_v, bsz)

    (dh, dz, dym, dyw, dbon, loss_acc, d_wo, d_gpost, d_lng, d_lnb, d_rk) = _post(
        x, loss_target, pp, o, yw, r, kp, v, rw_ln_g, rw_ln_b, r_k, wo, wo.T, norm_post_g, bo)

    d_k = _wkv_bwd(*rw_k, v_v, _to_v(dyw), states, u_v)
    dr_w, dw_w, dkp_w, dal_w, dbe_w = _gather_k(d_k[:5], bsz)
    dwkv = (dr_w, dw_w, dkp_w, _from_v(d_k[5], bsz), dal_w, dbe_w)
    dq, dk, dva = _attn_bwd(q_att, k_att, v_att, o, lse, dym)

    (da, d_wuq, d_wukv, d_w2p, d_a2p, d_gq, d_gkv, d_mu, d_w0, d_a0, d_kk, d_ka) = _pre_bwd_a(
        pp, pos, invf, (mla_q_norm_g, wuq.T, mla_kv_norm_g, wukv.T), mu, rw_w0, w2p, w2p.T, rw_a0, a2p, a2p.T,
        rw_k_k, rw_k_a, bo, dq, dk, dva, dwkv, dbon)
    grad_x, dpb, d_gpre = _pre_bwd_b(x, dh, dz, da, mu, wp.T, norm_pre_g)
    d_wp = _tn_matmul(u.reshape(bsz * t, D), dpb.reshape(bsz * t, DP), DP, "dw_in", bk=1024)

    full_g = {
        "w_in": jnp.concatenate([d_wp[:, 1024:1408], d_wp[:, 2944:3008], d_wp[:, 1408:2944], d_wp[:, 3008:3136],
                                 d_wp[:, 0:1024]], axis=1),
        "mla_w_uq": d_wuq.reshape(256, HEADS, 256)[:, :, :192].reshape(256, 768),
        "mla_w_ukv": jnp.transpose(d_wukv.reshape(128, 2, HEADS, 128), (0, 2, 1, 3)).reshape(128, 1024),
        "rw_w2": d_w2p[64:128],
        "rw_a2": d_a2p[128:192],
        "w_out": d_wo,
    }
    small_g = {
        "norm_pre_g": d_gpre, "mla_q_norm_g": d_gq, "mla_kv_norm_g": d_gkv,
        "rw_mu": jnp.concatenate([d_mu[:, 0:1536], d_mu[:, 1600:1728]], axis=1),
        "rw_w0": d_w0, "rw_a0": d_a0, "rw_k_k": d_kk, "rw_k_a": d_ka, "rw_r_k": d_rk, "rw_ln_g": d_lng,
        "rw_ln_b": d_lnb, "norm_post_g": d_gpost,
    }

    def by_shard(name, g):
        if name == "w_out":
            return g.reshape(N_SHARD, -1, LANES)
        rows, cols = g.shape
        return jnp.transpose(g.reshape(rows, N_SHARD, cols // N_SHARD), (1, 0, 2)).reshape(N_SHARD, -1, LANES)

    packed = jnp.concatenate([by_shard(n, full_g[n]) for n in SHARDED], axis=1)
    pair_sum, pair_sum_b = _rs_pairs(packed.reshape(N_SHARD, 2, HALF, LANES))
    g_shard = _rs_chips(pair_sum, pair_sum_b).reshape(PACK_TOTAL, LANES)

    flat = lambda a: a.reshape(1, -1)
    g_small = _small_allreduce([flat(small_g[n]) for n in SMALL], loss_acc)
    loss = g_small[SMALL_USED, 0]

    g_sharded = _unpack_shard(g_shard, {n: wts[n][0] for n in SHARDED})
    sh = _adamw([wts[n][0] for n in SHARDED], [g_sharded[n] for n in SHARDED], [mom_m[n][0] for n in SHARDED],
                [mom_v[n][0] for n in SHARDED], "adamw_sharded")
    sm = _adamw_small([flat(wts[n]) for n in SMALL], g_small, [flat(mom_m[n]) for n in SMALL],
                      [flat(mom_v[n]) for n in SMALL])

    def outputs(sharded, small):
        out = {n: a[None] for n, a in zip(SHARDED, sharded)}
        out.update({n: a.reshape(wts[n].shape) for n, a in zip(SMALL, small)})
        return out

    grads = outputs([g_sharded[n] for n in SHARDED], sm[0])
    deltas, new_m, new_v = (outputs(sh[k], sm[k + 1]) for k in range(3))
    return (loss, grad_x, *[grads[n] for n in WEIGHTS], *[deltas[n] for n in WEIGHTS],
            *[new_m[n] for n in WEIGHTS], *[new_v[n] for n in WEIGHTS])
```

```python
import numpy as np
import jax
import jax.numpy as jnp
from jax import lax
from jax.experimental import pallas as pl
from jax.experimental.pallas import tpu as pltpu

F32, BF16 = jnp.float32, jnp.bfloat16
MESH = pl.DeviceIdType.MESH

D = 1024
HEADS = 4
RW = 512
NORM_EPS = 1e-6
GN_EPS = 64e-5
ROPE_THETA = 10000.0
SCALE = (128 + 64) ** -0.5
D_IN = 3136
LR, B1, B2, ADAM_EPS, WD, STEP = 0.001, 0.9, 0.999, 1e-08, 0.01, 10

Z0, CQ0, CKV0, RW0, DP = 0, 1024, 1280, 1408, 3200
NRW = DP - RW0

LANES = 128
SUBLANES = 8
VMEM_LIMIT = 56 * 1024 * 1024

TT = 512
TT_VPU = 256
TQ = 512

N_SHARD = 4
PACK_ROWS = (256 * 192 // 128, 128 * 256 // 128, 64, 64, 256 * 1024 // 128)
PACK_REST = sum(PACK_ROWS)
SMALL_ROWS = 64
SMALL_USED = 60


def _cparams(sem=None):
    return pltpu.CompilerParams(dimension_semantics=sem, vmem_limit_bytes=VMEM_LIMIT)


def _full(shape):
    n = len(shape)
    return pl.BlockSpec(shape, lambda *_: (0,) * n, pipeline_mode=pl.Buffered(1))


def _resident(shape):
    n = len(shape)
    return pl.BlockSpec(shape, lambda *_: (0,) * n)


def _dot(a, b):
    return jnp.dot(a, b, preferred_element_type=F32)


def _dot_nt(a, b):
    return lax.dot_general(a, b, (((1,), (1,)), ((), ())), preferred_element_type=F32)


def _dot_tn(a, b):
    return lax.dot_general(a, b, (((0,), (0,)), ((), ())), preferred_element_type=F32)


def _split3(x):
    hi = x.astype(BF16)
    r1 = x - hi.astype(F32)
    mid = r1.astype(BF16)
    lo = (r1 - mid.astype(F32)).astype(BF16)
    return hi, mid, lo


def _seg(x, bo):
    rows, nblk = x.shape[0], x.shape[1] // LANES
    pieces = [p for i in range(nblk) for p in _split3(x[:, LANES * i:LANES * (i + 1)])]
    res = _dot(jnp.concatenate(pieces, axis=0), bo)
    parts = [res[(3 * i) * rows:(3 * i + 1) * rows] + res[(3 * i + 1) * rows:(3 * i + 2) * rows]
             + res[(3 * i + 2) * rows:(3 * i + 3) * rows] for i in range(nblk)]
    return parts[0] if nblk == 1 else jnp.concatenate(parts, axis=1)


def _rms(x, g, n):
    rstd = lax.rsqrt(jnp.sum(x * x, axis=-1, keepdims=True) * (1.0 / n) + NORM_EPS)
    nx = x * rstd
    return nx * g, nx, rstd


def _rms_bwd(dy, nx, rstd, g, n):
    dn = dy * g
    dx = rstd * (dn - nx * (jnp.sum(dn * nx, axis=-1, keepdims=True) * (1.0 / n)))
    return dx, jnp.sum(dy * nx, axis=0, keepdims=True)


def _rot(x):
    lane = lax.broadcasted_iota(jnp.int32, x.shape, 1)
    return jnp.where((lane % 64) < 32, -pltpu.roll(x, x.shape[1] - 32, 1), pltpu.roll(x, 32, 1))


def _sigmoid(x):
    return 1.0 / (1.0 + jnp.exp(-x))


def _softplus(x):
    return jnp.maximum(x, 0.0) + jnp.log(1.0 + jnp.exp(-jnp.abs(x)))


def _rw_gates(ps, w0, w2p, a0, a2p, k_k, k_a, bo):
    r, k, v, misc = ps[:, 0:512], ps[:, 512:1024], ps[:, 1024:1536], ps[:, 1536:NRW]
    th = jnp.tanh(misc)
    wpre = w0 + _dot(th.astype(BF16), w2p)
    e = jnp.exp(-_softplus(-wpre) - 0.5)
    w = jnp.exp(-e)
    a = _sigmoid(a0 + _dot(misc.astype(BF16), a2p))
    m = k * k_k
    nrm = jnp.maximum(jnp.sqrt(_seg(m * m, bo)), 1e-12)
    kk = m / nrm
    kp = k * (1.0 + (a - 1.0) * k_a)
    return dict(r=r, k=k, v=v, misc=misc, th=th, wpre=wpre, e=e, w=w, a=a, nrm=nrm, kk=kk, kp=kp)


def _shift_mix(prw, prev_row, mu):
    row = lax.broadcasted_iota(jnp.int32, prw.shape, 0)
    sh = jnp.where(row == 0, prev_row, pltpu.roll(prw, 1, 0))
    return prw + (sh - prw) * mu, sh


def _ag_weights(shards):
    n = len(shards)

    def body(*refs):
        ins, outs = refs[:n], refs[n:2 * n]
        ici_send, ici_recv, d2d_send, d2d_recv = refs[2 * n:2 * n + 4]
        x, y, c = lax.axis_index("x"), lax.axis_index("y"), lax.axis_index("c")
        mine = 2 * x + y
        for w in range(n):
            outs[w][mine] = ins[w][...].astype(BF16)
        flips = ((1, 0), (0, 1), (1, 1))

        def half(w, shard, cc):
            rows = outs[w].shape[1] // 2
            return outs[w].at[shard, pl.ds(pl.multiple_of(cc * rows, 16), rows)]

        def ici(w, k, shard):
            fx, fy = flips[k]
            return pltpu.make_async_remote_copy(
                src_ref=half(w, shard, c), dst_ref=half(w, shard, c),
                send_sem=ici_send.at[w * 3 + k], recv_sem=ici_recv.at[w * 3 + k],
                device_id=(x ^ fx, y ^ fy, c), device_id_type=MESH)

        def d2d(w, k, cc):
            fx, fy = flips[k]
            theirs = 2 * (x ^ fx) + (y ^ fy)
            return pltpu.make_async_remote_copy(
                src_ref=half(w, theirs, cc), dst_ref=half(w, theirs, cc),
                send_sem=d2d_send.at[w * 3 + k], recv_sem=d2d_recv.at[w * 3 + k],
                device_id=(x, y, 1 - c), device_id_type=MESH)

        for w in range(n):
            for k in range(3):
                ici(w, k, mine).start()
        for w in range(n):
            for k in range(3):
                fx, fy = flips[k]
                ici(w, k, 2 * (x ^ fx) + (y ^ fy)).wait_recv()
                d2d(w, k, c).start()
        for w in range(n):
            for k in range(3):
                d2d(w, k, 1 - c).wait_recv()
        for w in range(n):
            for k in range(3):
                ici(w, k, mine).wait_send()
                d2d(w, k, c).wait_send()

    vm = pl.BlockSpec(memory_space=pltpu.VMEM)
    return pl.pallas_call(
        body, name="ag_weights",
        out_shape=[jax.ShapeDtypeStruct((N_SHARD,) + s.shape, BF16) for s in shards],
        in_specs=[vm] * n, out_specs=[vm] * n,
        scratch_shapes=[pltpu.SemaphoreType.DMA((3 * n,))] * 4,
        compiler_params=pltpu.CompilerParams(vmem_limit_bytes=VMEM_LIMIT),
    )(*shards)


def _rs_pairs(halves):
    n = len(halves)

    def body(*refs):
        h_refs, sum_refs, sumb_refs, recvs = (refs[k * n:(k + 1) * n] for k in range(4))
        send_sem, recv_sem = refs[4 * n:]
        x, y, c = lax.axis_index("x"), lax.axis_index("y"), lax.axis_index("c")
        cps = [pltpu.make_async_remote_copy(src_ref=h_refs[i].at[s, 1 - c], dst_ref=recvs[i].at[s],
                                            send_sem=send_sem.at[i * N_SHARD + s], recv_sem=recv_sem.at[i * N_SHARD + s],
                                            device_id=(x, y, 1 - c), device_id_type=MESH)
               for i in range(n) for s in range(N_SHARD)]
        for cp in cps:
            cp.start()
        for i in range(n):
            for s in range(N_SHARD):
                cps[i * N_SHARD + s].wait_recv()
                acc = h_refs[i][s, c] + recvs[i][s]
                sum_refs[i][s] = acc
                sumb_refs[i][s] = acc.astype(BF16)
        for cp in cps:
            cp.wait_send()

    vm = pl.BlockSpec(memory_space=pltpu.VMEM)
    shapes = [(N_SHARD,) + h.shape[2:] for h in halves]
    outs = pl.pallas_call(
        body, name="rs_pairs",
        out_shape=[jax.ShapeDtypeStruct(sh, F32) for sh in shapes] + [jax.ShapeDtypeStruct(sh, BF16) for sh in shapes],
        in_specs=[vm] * n, out_specs=[vm] * (2 * n),
        scratch_shapes=[pltpu.VMEM(sh, F32) for sh in shapes] + [pltpu.SemaphoreType.DMA((n * N_SHARD,)),
                                                                 pltpu.SemaphoreType.DMA((n * N_SHARD,))],
        compiler_params=pltpu.CompilerParams(vmem_limit_bytes=VMEM_LIMIT),
    )(*halves)
    return outs[:n], outs[n:]


def _rs_chips(part_f32, part_bf16):
    n = len(part_f32)

    def body(*refs):
        own_refs, src_refs, out_refs, recvs = (refs[k * n:(k + 1) * n] for k in range(4))
        ici_send, ici_recv, d2d_send, d2d_recv = refs[4 * n:]
        x, y, c = lax.axis_index("x"), lax.axis_index("y"), lax.axis_index("c")
        mine = 2 * x + y
        flips = ((1, 0), (0, 1), (1, 1))
        cps = []
        for i in range(n):
            for k, (fx, fy) in enumerate(flips):
                theirs = 2 * (x ^ fx) + (y ^ fy)
                cps.append(pltpu.make_async_remote_copy(
                    src_ref=src_refs[i].at[theirs], dst_ref=recvs[i].at[k],
                    send_sem=ici_send.at[3 * i + k], recv_sem=ici_recv.at[3 * i + k],
                    device_id=(x ^ fx, y ^ fy, c), device_id_type=MESH))
        for cp in cps:
            cp.start()
        handed = []
        for i in range(n):
            acc = own_refs[i][mine]
            for k in range(3):
                cps[3 * i + k].wait_recv()
                acc = acc + recvs[i][k].astype(F32)
            out_refs[i][c] = acc
            to_sibling = pltpu.make_async_remote_copy(
                src_ref=out_refs[i].at[c], dst_ref=out_refs[i].at[c], send_sem=d2d_send.at[i], recv_sem=d2d_recv.at[i],
                device_id=(x, y, 1 - c), device_id_type=MESH)
            to_sibling.start()
            handed.append(to_sibling)
        for i in range(n):
            pltpu.make_async_remote_copy(
                src_ref=out_refs[i].at[1 - c], dst_ref=out_refs[i].at[1 - c], send_sem=d2d_send.at[i],
                recv_sem=d2d_recv.at[i], device_id=(x, y, 1 - c), device_id_type=MESH).wait_recv()
        for cp in handed + cps:
            cp.wait_send()

    vm = pl.BlockSpec(memory_space=pltpu.VMEM)
    return pl.pallas_call(
        body, name="rs_chips",
        out_shape=[jax.ShapeDtypeStruct((2,) + p.shape[1:], F32) for p in part_f32],
        in_specs=[vm] * (2 * n), out_specs=[vm] * n,
        scratch_shapes=[pltpu.VMEM((3,) + p.shape[1:], BF16) for p in part_bf16]
        + [pltpu.SemaphoreType.DMA((3 * n,)), pltpu.SemaphoreType.DMA((3 * n,)), pltpu.SemaphoreType.DMA((n,)),
           pltpu.SemaphoreType.DMA((n,))],
        compiler_params=pltpu.CompilerParams(vmem_limit_bytes=VMEM_LIMIT),
    )(*part_f32, *part_bf16)


def _small_rows(vecs):
    out, at = [], 0
    for vec in vecs:
        rows = vec.shape[1] // LANES
        out.append((rows, at))
        at += rows
    assert at == SMALL_USED
    return out


def _small_allreduce(vecs, loss_acc):
    n = len(vecs)
    layout = _small_rows(vecs)

    def body(*refs):
        loss_ref, out_ref, stage, recv, send_sems, recv_sems = refs[n:]
        for vec_ref, (rows, at) in zip(refs[:n], layout):
            for j in range(rows):
                stage[at + j:at + j + 1, :] = vec_ref[0:1, LANES * j:LANES * (j + 1)]
        stage[SMALL_USED:SMALL_ROWS, :] = loss_ref[0:SMALL_ROWS - SMALL_USED, :]
        x, y, c = lax.axis_index("x"), lax.axis_index("y"), lax.axis_index("c")
        me = 4 * x + 2 * y + c
        cps = []
        for k in range(1, 8):
            fx, fy, fc = (k >> 2) & 1, (k >> 1) & 1, k & 1
            cps.append(pltpu.make_async_remote_copy(
                src_ref=stage, dst_ref=recv.at[k - 1],
                send_sem=send_sems.at[k - 1], recv_sem=recv_sems.at[k - 1],
                device_id=(x ^ fx, y ^ fy, c ^ fc), device_id_type=MESH))
        for cp in cps:
            cp.start()
        for cp in cps:
            cp.wait()
        acc = jnp.zeros(stage.shape, F32)
        for j in range(8):
            slot = jnp.maximum((me ^ j) - 1, 0)
            acc = acc + jnp.where(me == j, stage[...], recv[slot])
        out_ref[...] = acc

    vm = pl.BlockSpec(memory_space=pltpu.VMEM)
    shape = (SMALL_ROWS, LANES)
    return pl.pallas_call(
        body, name="small_allreduce",
        out_shape=jax.ShapeDtypeStruct(shape, F32),
        in_specs=[vm] * (n + 1), out_specs=vm,
        scratch_shapes=[pltpu.VMEM(shape, F32), pltpu.VMEM((7,) + shape, F32), pltpu.SemaphoreType.DMA((7,)),
                        pltpu.SemaphoreType.DMA((7,))],
    )(*vecs, loss_acc)


def _adamw_small(ws, g_packed, ms, vs):
    n = len(ws)
    layout = _small_rows(ws)

    def body(*refs):
        g_ref = refs[3 * n]
        outs = refs[3 * n + 1:]
        for i, (rows, at) in enumerate(layout):
            w_ref, m_ref, v_ref = refs[i], refs[n + i], refs[2 * n + i]
            go_ref, d_ref, nm_ref, nv_ref = (outs[k * n + i] for k in range(4))
            for j in range(rows):
                lanes = slice(LANES * j, LANES * (j + 1))
                gg = g_ref[at + j:at + j + 1, :]
                nm = B1 * m_ref[0:1, lanes] + (1.0 - B1) * gg
                nv = B2 * v_ref[0:1, lanes] + (1.0 - B2) * (gg * gg)
                m_hat = nm / (1.0 - B1 ** STEP)
                v_hat = nv / (1.0 - B2 ** STEP)
                go_ref[0:1, lanes] = gg
                d_ref[0:1, lanes] = -LR * (m_hat / (jnp.sqrt(v_hat) + ADAM_EPS) + WD * w_ref[0:1, lanes])
                nm_ref[0:1, lanes] = nm
                nv_ref[0:1, lanes] = nv

    vm = pl.BlockSpec(memory_space=pltpu.VMEM)
    sds = [jax.ShapeDtypeStruct(w.shape, F32) for w in ws]
    outs = pl.pallas_call(
        body, name="adamw_small", out_shape=sds * 4, in_specs=[vm] * (3 * n + 1), out_specs=[vm] * (4 * n),
    )(*ws, *ms, *vs, g_packed)
    return outs[:n], outs[n:2 * n], outs[2 * n:3 * n], outs[3 * n:]


ADAM_ROWS = 64


def _adamw(ws, gs, ms, vs, name):
    n = len(ws)

    def body(*refs):
        for i in range(n):
            w_ref, g_ref, m_ref, v_ref = (refs[k * n + i] for k in range(4))
            d_ref, nm_ref, nv_ref = (refs[(4 + k) * n + i] for k in range(3))
            rows = min(ADAM_ROWS, w_ref.shape[0])

            def chunk(r, _):
                at = pl.ds(pl.multiple_of(r * rows, SUBLANES), rows)
                gg = g_ref[at, :]
                nm = B1 * m_ref[at, :] + (1.0 - B1) * gg
                nv = B2 * v_ref[at, :] + (1.0 - B2) * (gg * gg)
                m_hat = nm / (1.0 - B1 ** STEP)
                v_hat = nv / (1.0 - B2 ** STEP)
                d_ref[at, :] = -LR * (m_hat / (jnp.sqrt(v_hat) + ADAM_EPS) + WD * w_ref[at, :])
                nm_ref[at, :] = nm
                nv_ref[at, :] = nv
                return 0

            lax.fori_loop(0, w_ref.shape[0] // rows, chunk, 0)

    vm = pl.BlockSpec(memory_space=pltpu.VMEM)
    sds = [jax.ShapeDtypeStruct(w.shape, F32) for w in ws]
    outs = pl.pallas_call(
        body, name=name, out_shape=sds * 3, in_specs=[vm] * (4 * n), out_specs=[vm] * (3 * n),
        compiler_params=pltpu.CompilerParams(vmem_limit_bytes=VMEM_LIMIT),
    )(*ws, *gs, *ms, *vs)
    return outs[:n], outs[n:2 * n], outs[2 * n:]


def _pre_fwd(x, pos, invf, gpre, wp, gq, wuq, gkv, wukv, mu, w0, w2p, a0, a2p, k_k, k_a, bo):
    bsz, t, _ = x.shape
    nt = t // TT

    def body(x_ref, pos_ref, invf_ref, gpre_ref, wp_ref, gq_ref, wuq_ref, gkv_ref, wukv_ref, mu_ref, w0_ref,
             w2p_ref, a0_ref, a2p_ref, kk_ref, ka_ref, bo_ref,
             u_ref, pp_ref, q_ref, k_ref, v_ref, r_o, w_o, kp_o, vv_o, al_o, be_o, carry):
        i = pl.program_id(1)
        u, _, _ = _rms(x_ref[0], gpre_ref[...], D)
        ub = u.astype(BF16)
        u_ref[0] = ub
        p = _dot(ub, wp_ref[...])
        pp_ref[0] = p
        prw = p[:, RW0:DP]

        @pl.when(i == 0)
        def _():
            carry[...] = jnp.zeros(carry.shape, F32)

        ps, _ = _shift_mix(prw, carry[7:8, :], mu_ref[...])
        carry[...] = prw[TT - 8:TT, :]

        g = _rw_gates(ps, w0_ref[...], w2p_ref[...], a0_ref[...], a2p_ref[...], kk_ref[...], ka_ref[...],
                      bo_ref[...])
        r_o[0] = g["r"]
        w_o[0] = g["w"]
        kp_o[0] = g["kp"]
        vv_o[0] = g["v"]
        al_o[0] = -g["kk"]
        be_o[0] = g["kk"] * g["a"]

        cqn, _, _ = _rms(p[:, CQ0:CQ0 + 256], gq_ref[...], 256)
        q = _dot(cqn.astype(BF16), wuq_ref[...])
        ckvn, _, _ = _rms(p[:, CKV0:CKV0 + 128], gkv_ref[...], 128)
        kv = _dot(ckvn.astype(BF16), wukv_ref[...])
        ang = pos_ref[0] * invf_ref[...]
        cs, sn = jnp.cos(ang), jnp.sin(ang)
        lane = lax.broadcasted_iota(jnp.int32, cs.shape, 1)
        kr = ps[:, 1536:1536 + LANES]
        kr = jnp.where(lane < 64, kr * cs + _rot(kr) * sn, 0.0).astype(BF16)
        for h in range(HEADS):
            qr = q[:, 256 * h + 128:256 * h + 256]
            q_ref[0, :, 256 * h:256 * h + 128] = q[:, 256 * h:256 * h + 128].astype(BF16)
            q_ref[0, :, 256 * h + 128:256 * h + 256] = (qr * cs + _rot(qr) * sn).astype(BF16)
            k_ref[0, :, 256 * h:256 * h + 128] = kv[:, 128 * h:128 * h + 128].astype(BF16)
            k_ref[0, :, 256 * h + 128:256 * h + 256] = kr
        v_ref[0] = kv[:, 512:1024].astype(BF16)

    tok = lambda c: pl.BlockSpec((1, TT, c), lambda b, i: (b, i, 0))
    full = lambda a: _full(a.shape)
    ins = (x, pos, invf, gpre, wp, gq, wuq, gkv, wukv, mu, w0, w2p, a0, a2p, k_k, k_a, bo)
    in_specs = [tok(D), tok(1)] + [full(a) for a in ins[2:]]
    sd = lambda c, dt: jax.ShapeDtypeStruct((bsz, t, c), dt)
    out_shape = [sd(D, BF16), sd(DP, F32), sd(1024, BF16), sd(1024, BF16), sd(512, BF16)] + [sd(RW, F32)] * 6
    out_specs = [tok(D), tok(DP), tok(1024), tok(1024), tok(512)] + [tok(RW)] * 6
    return pl.pallas_call(
        body, name="pre_fwd", grid=(bsz, nt), out_shape=out_shape, in_specs=in_specs, out_specs=out_specs,
        scratch_shapes=[pltpu.VMEM((8, NRW), F32)],
        compiler_params=_cparams(("arbitrary", "arbitrary")),
    )(*ins)


def _attn_fwd(q, k, v):
    bsz, t, _ = q.shape
    nq = t // TQ

    hps = HEADS

    def body(q_ref, k_ref, v_ref, o_ref, lse_ref):
        i = pl.program_id(2)

        def step(j, carry, diagonal):
            at = pl.ds(pl.multiple_of(j * TQ, TQ), TQ)
            out = []
            for hh in range(hps):
                m, l, acc = carry[hh]
                s = _dot_nt(q_ref[0, :, 256 * hh:256 * (hh + 1)], k_ref[0, at, 256 * hh:256 * (hh + 1)]) * SCALE
                if diagonal:
                    s = jnp.where(lax.broadcasted_iota(jnp.int32, (TQ, TQ), 1)
                                  <= lax.broadcasted_iota(jnp.int32, (TQ, TQ), 0), s, -1e30)
                mn = jnp.maximum(m, jnp.max(s, axis=1, keepdims=True))
                p = jnp.exp(s - mn)
                al = jnp.exp(m - mn)
                l = al * l + jnp.sum(p, axis=1, keepdims=True)
                acc = al * acc + _dot(p.astype(BF16), v_ref[0, at, LANES * hh:LANES * (hh + 1)])
                out.append((mn, l, acc))
            return tuple(out)

        start = (jnp.full((TQ, 1), -1e30, F32), jnp.zeros((TQ, 1), F32), jnp.zeros((TQ, LANES), F32))
        before = lax.fori_loop(0, i, lambda j, carry: step(j, carry, False), (start,) * hps)
        for hh, (m, l, acc) in enumerate(step(i, before, True)):
            o_ref[0, :, LANES * hh:LANES * (hh + 1)] = acc / l
            lse_ref[0, hh] = jnp.broadcast_to(m + jnp.log(l), (TQ, LANES))

    return pl.pallas_call(
        body, name="attn_fwd", grid=(bsz, HEADS // hps, nq),
        out_shape=[jax.ShapeDtypeStruct((bsz, t, 512), F32), jax.ShapeDtypeStruct((bsz, HEADS, t, LANES), F32)],
        in_specs=[pl.BlockSpec((1, TQ, 256 * hps), lambda b, h, i: (b, i, h)),
                  pl.BlockSpec((1, t, 256 * hps), lambda b, h, i: (b, 0, h)),
                  pl.BlockSpec((1, t, LANES * hps), lambda b, h, i: (b, 0, h))],
        out_specs=[pl.BlockSpec((1, TQ, LANES * hps), lambda b, h, i: (b, i, h)),
                   pl.BlockSpec((1, hps, TQ, LANES), lambda b, h, i: (b, h, i, 0))],
        compiler_params=_cparams(("parallel", "parallel", "arbitrary")),
    )(q, k, v)


def _attn_bwd(q, k, v, o, lse, do):
    bsz, t, _ = q.shape
    nq = t // TQ

    def body(q_ref, k_ref, v_ref, o_ref, lse_ref, do_ref, dq_ref, dk_ref, dv_ref, dl_ref):
        j = pl.program_id(2)

        @pl.when(j == 0)
        def _():
            def prep(i, _):
                at = pl.ds(pl.multiple_of(i * TQ, TQ), TQ)
                for hh in range(2):
                    lanes = slice(LANES * hh, LANES * (hh + 1))
                    dl_ref[hh, at, :] = jnp.broadcast_to(
                        jnp.sum(do_ref[0, at, lanes] * o_ref[0, at, lanes], axis=1, keepdims=True), (TQ, LANES))
                return 0

            lax.fori_loop(0, nq, prep, 0)
            dq_ref[0] = jnp.zeros((t, 512), F32)

        def q_tile(i, carry, diagonal):
            atq = pl.ds(pl.multiple_of(i * TQ, TQ), TQ)
            out = []
            for hh in range(2):
                dk, dv = carry[hh]
                wide, narrow = slice(256 * hh, 256 * (hh + 1)), slice(LANES * hh, LANES * (hh + 1))
                qt, kt, vt = q_ref[0, atq, wide], k_ref[0, :, wide], v_ref[0, :, narrow]
                dob = do_ref[0, atq, narrow].astype(BF16)
                s = _dot_nt(qt, kt) * SCALE
                if diagonal:
                    s = jnp.where(lax.broadcasted_iota(jnp.int32, (TQ, TQ), 1)
                                  <= lax.broadcasted_iota(jnp.int32, (TQ, TQ), 0), s, -1e30)
                p = jnp.exp(s - lse_ref[0, hh, atq, :][:, 0:1])
                dv = dv + _dot_tn(p.astype(BF16), dob)
                dp = _dot_nt(dob, vt)
                ds = (p * (dp - dl_ref[hh, atq, :][:, 0:1]) * SCALE).astype(BF16)
                dk = dk + _dot_tn(ds, qt)
                dq_ref[0, atq, wide] += _dot(ds, kt)
                out.append((dk, dv))
            return tuple(out)

        zero = (jnp.zeros((TQ, 256), F32), jnp.zeros((TQ, LANES), F32))
        first = q_tile(j, (zero, zero), True)
        done = lax.fori_loop(j + 1, nq, lambda i, carry: q_tile(i, carry, False), first)
        for hh, (dk, dv) in enumerate(done):
            dk_ref[0, :, 256 * hh:256 * (hh + 1)] = dk
            dv_ref[0, :, LANES * hh:LANES * (hh + 1)] = dv

    whole = lambda c: pl.BlockSpec((1, t, c), lambda b, h, j: (b, 0, h))
    tile = lambda c: pl.BlockSpec((1, TQ, c), lambda b, h, j: (b, j, h))
    return pl.pallas_call(
        body, name="attn_bwd", grid=(bsz, HEADS // 2, nq),
        out_shape=[jax.ShapeDtypeStruct((bsz, t, 1024), F32), jax.ShapeDtypeStruct((bsz, t, 1024), F32),
                   jax.ShapeDtypeStruct((bsz, t, 512), F32)],
        in_specs=[whole(512), tile(512), tile(256), whole(256),
                  pl.BlockSpec((1, 2, t, LANES), lambda b, h, j: (b, h, 0, 0)), whole(256)],
        out_specs=[whole(512), tile(512), tile(256)],
        scratch_shapes=[pltpu.VMEM((2, t, LANES), F32)],
        compiler_params=_cparams(("parallel", "parallel", "arbitrary")),
    )(q, k, v, o, lse, do)


RW_HEADS = 8
CH = 32


def _lane_split(bsz):
    vs = LANES // (bsz * RW_HEADS)
    return vs, 64 // vs


def _gather_matrix(bsz):
    group = bsz * RW_HEADS
    vs = LANES // group
    half = (RW_HEADS // 2) * bsz * SPREAD_STEPS
    p = np.zeros((SPREAD_STEPS // vs * LANES, 2 * half), np.float32)
    for g2 in range(SPREAD_STEPS // vs):
        for j in range(vs):
            for b in range(bsz):
                for h in range(RW_HEADS):
                    hp, hpar = h // 2, h % 2
                    p[g2 * LANES + j * group + b * RW_HEADS + h,
                      hpar * half + (hp * bsz + b) * SPREAD_STEPS + g2 * vs + j] = 1.0
    return jnp.asarray(np.concatenate([p] * 3, axis=0), BF16)


def _gather_k(ys, bsz):
    vs = LANES // (bsz * RW_HEADS)
    assert (RW_HEADS // 2) * bsz * SPREAD_STEPS == LANES, "the transposed tile must be 128 lanes wide"
    tg = ys[0].shape[0]
    n = len(ys)
    ngrp = GATHER_BLOCK // SPREAD_STEPS
    per = SPREAD_STEPS // vs

    def body(*refs):
        pm = refs[n][...]
        for y_ref, o_ref in zip(refs[:n], refs[n + 1:]):
            lhs = jnp.concatenate(
                [jnp.concatenate(_split3(jnp.concatenate([y_ref[per * m + g2] for g2 in range(per)], axis=1)), axis=1)
                 for m in range(ngrp)], axis=0)
            a = _dot(lhs, pm)
            for m in range(ngrp):
                am = a[64 * m:64 * (m + 1)]
                bt = jnp.concatenate([am[:, 0:LANES], am[:, LANES:2 * LANES]], axis=0).T
                for hp in range(RW_HEADS // 2):
                    for b in range(bsz):
                        at = (hp * bsz + b) * SPREAD_STEPS
                        o_ref[b, SPREAD_STEPS * m:SPREAD_STEPS * (m + 1), LANES * hp:LANES * (hp + 1)] = \
                            bt[at:at + SPREAD_STEPS]

    pm = _gather_matrix(bsz)
    return pl.pallas_call(
        body, name="wkv_gather", grid=(tg * vs // GATHER_BLOCK,),
        out_shape=[jax.ShapeDtypeStruct((bsz, tg * vs, RW), F32)] * n,
        in_specs=[pl.BlockSpec((GATHER_BLOCK // vs, 64, LANES), lambda i: (i, 0, 0))] * n + [_full(pm.shape)],
        out_specs=[pl.BlockSpec((bsz, GATHER_BLOCK, RW), lambda i: (0, i, 0))] * n,
        compiler_params=_cparams(("parallel",)),
    )(*ys, pm)


def _to_v(x):
    bsz, t, _ = x.shape
    vs, vq = _lane_split(bsz)
    return jnp.transpose(x.reshape(bsz, t, RW_HEADS, vs, vq), (1, 4, 3, 0, 2)).reshape(t, vq, LANES)


def _from_v(y, bsz):
    t = y.shape[0]
    vs, vq = _lane_split(bsz)
    return jnp.transpose(y.reshape(t, vq, vs, bsz, RW_HEADS), (3, 0, 4, 2, 1)).reshape(bsz, t, RW)


def _ksum(a):
    return jnp.sum(a, axis=0, keepdims=True)


def _fold(a, group):
    sh = LANES // 2
    while sh >= group:
        a = a + pltpu.roll(a, sh, 1)
        sh //= 2
    return a


def _lane_group(shape, group):
    return lax.broadcasted_iota(jnp.int32, shape, 1) // group


SPREAD_STEPS = 8
SPREAD_BLOCK = 64
GATHER_BLOCK = 128


def _spread_matrix(bsz):
    group = bsz * RW_HEADS
    vs = LANES // group
    rows = (RW_HEADS // 2) * bsz * SPREAD_STEPS
    q = np.zeros((2, rows, SPREAD_STEPS * LANES), np.float32)
    for hpar in range(2):
        for hp in range(RW_HEADS // 2):
            for b in range(bsz):
                for st in range(SPREAD_STEPS):
                    row = (hp * bsz + b) * SPREAD_STEPS + st
                    for s in range(vs):
                        q[hpar, row, st * LANES + s * group + b * RW_HEADS + 2 * hp + hpar] = 1.0
    return jnp.asarray(np.concatenate([q[0], q[1]] * 3, axis=0), BF16)


def _spread_k(xs):
    bsz, t, _ = xs[0].shape
    assert (RW_HEADS // 2) * bsz * SPREAD_STEPS == LANES, "the transposed tile must be 128 lanes wide"
    n = len(xs)
    ngrp = SPREAD_BLOCK // SPREAD_STEPS

    def body(*refs):
        qm = refs[n][...]
        for x_ref, o_ref in zip(refs[:n], refs[n + 1:]):
            cols = [[] for _ in range(6)]
            for m in range(ngrp):
                at = slice(SPREAD_STEPS * m, SPREAD_STEPS * (m + 1))
                x8 = jnp.concatenate([x_ref[b, at, LANES * hp:LANES * (hp + 1)]
                                      for hp in range(RW_HEADS // 2) for b in range(bsz)], axis=0)
                for pi, piece in enumerate(_split3(x8.T)):
                    cols[2 * pi].append(piece[0:64])
                    cols[2 * pi + 1].append(piece[64:128])
            lhs = jnp.concatenate([jnp.concatenate(c, axis=0) for c in cols], axis=1)
            y = _dot(lhs, qm)
            for m in range(ngrp):
                for st in range(SPREAD_STEPS):
                    o_ref[SPREAD_STEPS * m + st] = y[64 * m:64 * (m + 1), LANES * st:LANES * (st + 1)]

    qm = _spread_matrix(bsz)
    return pl.pallas_call(
        body, name="wkv_spread", grid=(t // SPREAD_BLOCK,),
        out_shape=[jax.ShapeDtypeStruct((t, 64, LANES), F32)] * n,
        in_specs=[pl.BlockSpec((bsz, SPREAD_BLOCK, RW), lambda i: (0, i, 0))] * n + [_full(qm.shape)],
        out_specs=[pl.BlockSpec((SPREAD_BLOCK, 64, LANES), lambda i: (i, 0, 0))] * n,
        compiler_params=_cparams(("parallel",)),
    )(*xs, qm)


def _wkv_fwd(r, w, kp, al, be, v):
    t, vq = v.shape[0], v.shape[1]

    def body(r_ref, w_ref, kp_ref, al_ref, be_ref, v_ref, y_ref, a_ref, u_ref, st_ref):
        @pl.when(pl.program_id(0) == 0)
        def _():
            st_ref[...] = jnp.zeros(st_ref.shape, F32)

        def step(tl, _):
            rv, wv, kv, av, bv = r_ref[tl], w_ref[tl], kp_ref[tl], al_ref[tl], be_ref[tl]
            vals = v_ref[tl]
            yrows, urows = [], []
            for q in range(vq):
                s = st_ref[q]
                u = _ksum(s * av)
                s = s * wv + bv * u + kv * vals[q:q + 1]
                st_ref[q] = s
                a_ref[tl, q] = s
                urows.append(u)
                yrows.append(_ksum(s * rv))
            y_ref[tl] = jnp.concatenate(yrows, axis=0)
            u_ref[tl] = jnp.concatenate(urows, axis=0)
            return 0

        lax.fori_loop(0, CH, step, 0)

    kspec = pl.BlockSpec((CH, 64, LANES), lambda i: (i, 0, 0))
    vspec = pl.BlockSpec((CH, vq, LANES), lambda i: (i, 0, 0))
    vsd = jax.ShapeDtypeStruct((t, vq, LANES), F32)
    return pl.pallas_call(
        body, name="wkv_fwd", grid=(t // CH,),
        out_shape=[vsd, jax.ShapeDtypeStruct((t, vq, 64, LANES), F32), vsd],
        in_specs=[kspec] * 5 + [vspec],
        out_specs=[vspec, pl.BlockSpec((CH, vq, 64, LANES), lambda i: (i, 0, 0, 0)), vspec],
        scratch_shapes=[pltpu.VMEM((vq, 64, LANES), F32)],
        compiler_params=_cparams(("arbitrary",)),
    )(r, w, kp, al, be, v)


def _wkv_bwd(r, w, kp, al, be, v, dy, states, u):
    t, vq = v.shape[0], v.shape[1]
    vs = 64 // vq
    group = LANES // vs
    n = t // CH
    ng = CH // vs

    def body(r_ref, w_ref, kp_ref, al_ref, be_ref, v_ref, dy_ref, u_ref, a_ref, ap_ref,
             dr_ref, dw_ref, dkp_ref, dal_ref, dbe_ref, dv_ref, ds_ref):
        @pl.when(pl.program_id(0) == 0)
        def _():
            ds_ref[...] = jnp.zeros(ds_ref.shape, F32)

        earliest = pl.program_id(0) == n - 1

        def reverse(i, _):
            g = ng - 1 - i
            grp = _lane_group((64, LANES), group)
            outs = None
            for j in reversed(range(vs)):
                tl = g * vs + j
                rv, wv, kv, av, bv = r_ref[tl], w_ref[tl], kp_ref[tl], al_ref[tl], be_ref[tl]
                vals, dys, us = v_ref[tl], dy_ref[tl], u_ref[tl]
                acc = None
                dvrows = []
                for q in range(vq):
                    if j > 0:
                        s_prev = a_ref[tl - 1, q]
                    else:
                        before = jnp.where(earliest, 0.0, ap_ref[0, q])
                        s_prev = jnp.where(g == 0, before, a_ref[jnp.maximum(tl - 1, 0), q])
                    dyq = dys[q:q + 1]
                    ds = ds_ref[q] + rv * dyq
                    c = _ksum(ds * bv)
                    dvrows.append(_ksum(ds * kv))
                    terms = (a_ref[tl, q] * dyq, ds * s_prev, ds * vals[q:q + 1], s_prev * c, ds * us[q:q + 1])
                    acc = terms if acc is None else tuple(a + b for a, b in zip(acc, terms))
                    ds_ref[q] = ds * wv + av * c
                dv_ref[tl] = jnp.concatenate(dvrows, axis=0)
                summed = [_fold(a, group) for a in acc]
                outs = summed if outs is None else [jnp.where(grp == j, f, o) for f, o in zip(summed, outs)]
            for ref, o in zip((dr_ref, dw_ref, dkp_ref, dal_ref, dbe_ref), outs):
                ref[g] = o
            return 0

        lax.fori_loop(0, ng, reverse, 0)

    kspec = pl.BlockSpec((CH, 64, LANES), lambda i: (n - 1 - i, 0, 0))
    gspec = pl.BlockSpec((ng, 64, LANES), lambda i: (n - 1 - i, 0, 0))
    vspec = pl.BlockSpec((CH, vq, LANES), lambda i: (n - 1 - i, 0, 0))
    ksd = jax.ShapeDtypeStruct((t // vs, 64, LANES), F32)
    return pl.pallas_call(
        body, name="wkv_bwd", grid=(n,),
        out_shape=[ksd] * 5 + [jax.ShapeDtypeStruct((t, vq, LANES), F32)],
        in_specs=[kspec] * 5 + [vspec, vspec, vspec,
                                pl.BlockSpec((CH, vq, 64, LANES), lambda i: (n - 1 - i, 0, 0, 0)),
                                pl.BlockSpec((1, vq, 64, LANES), lambda i: (jnp.maximum((n - 1 - i) * CH - 1, 0), 0, 0, 0))],
        out_specs=[gspec] * 5 + [vspec],
        scratch_shapes=[pltpu.VMEM((vq, 64, LANES), F32)],
        compiler_params=_cparams(("arbitrary",)),
    )(r, w, kp, al, be, v, dy, u, states, states)


def _post(x, tgt, pp, o, yw, r, kp, v, ln_g, ln_b, r_k, wo, wot, gpost, bo):
    bsz, t, _ = x.shape
    tt = TT_VPU
    nt = t // tt

    def body(x_ref, tgt_ref, z_ref, o_ref, yw_ref, r_ref, kp_ref, v_ref, lng_ref, lnb_ref, rk_ref, wo_ref, wot_ref,
             gpost_ref, bo_ref,
             dh_ref, dz_ref, dym_ref, dyw_ref, dbon_ref, loss_ref, dwo_ref, dgpost_ref, dlng_ref, dlnb_ref, drk_ref):
        first = (pl.program_id(0) == 0) & (pl.program_id(1) == 0)

        @pl.when(first)
        def _():
            for ref in (loss_ref, dwo_ref, dgpost_ref, dlng_ref, dlnb_ref, drk_ref):
                ref[...] = jnp.zeros(ref.shape, F32)

        bo_m = bo_ref[...]
        seg = lambda a: _seg(a, bo_m)
        rowsum = lambda a: jnp.sum(a, axis=0, keepdims=True)
        ywv, rv, kpv, vv = yw_ref[0], r_ref[0], kp_ref[0], v_ref[0]
        ln_g, r_k = lng_ref[...], rk_ref[...]
        mean = seg(ywv) * (1.0 / 64)
        yc = ywv - mean
        rstd = lax.rsqrt(seg(yc * yc) * (1.0 / 64) + GN_EPS)
        yhat = yc * rstd
        sb = seg(rv * kpv * r_k)
        y_rw = yhat * ln_g + lnb_ref[...] + sb * vv
        z = z_ref[0]
        sig = _sigmoid(z)
        sz = z * sig
        ycat = jnp.concatenate([o_ref[0], y_rw], axis=1)
        ycg = (ycat * sz).astype(BF16)
        out = _dot(ycg, wo_ref[...])
        hn, nx, rstd_o = _rms(out, gpost_ref[...], D)
        err = x_ref[0] + hn - tgt_ref[0]
        loss_ref[...] += jnp.sum(err * err) * (0.5 / D)
        dh = err * (1.0 / D)
        dh_ref[0] = dh
        dout, dgp = _rms_bwd(dh, nx, rstd_o, gpost_ref[...], D)
        dgpost_ref[...] += dgp
        doutb = dout.astype(BF16)
        dwo_ref[...] += _dot_tn(ycg, doutb)
        dycg = _dot(doutb, wot_ref[...])
        dz_ref[0] = dycg * ycat * (sig * (1.0 + z * (1.0 - sig)))
        dycat = dycg * sz
        dym_ref[0] = dycat[:, 0:512]
        dy_rw = dycat[:, 512:1024]
        dlnb_ref[...] += rowsum(dy_rw)
        dlng_ref[...] += rowsum(dy_rw * yhat)
        dyhat = dy_rw * ln_g
        dyw_ref[0] = rstd * (dyhat - seg(dyhat) * (1.0 / 64) - yhat * (seg(dyhat * yhat) * (1.0 / 64)))
        dsb = seg(dy_rw * vv)
        drk_ref[...] += rowsum(dsb * rv * kpv)
        dbon_ref[0, :, 0:512] = dsb * kpv * r_k
        dbon_ref[0, :, 512:1024] = dsb * rv * r_k
        dbon_ref[0, :, 1024:1536] = dy_rw * sb

    tok = lambda c: pl.BlockSpec((1, tt, c), lambda b, i: (b, i, 0))
    full = lambda a: _full(a.shape)
    ins = (x, tgt, pp, o, yw, r, kp, v, ln_g, ln_b, r_k, wo, wot, gpost, bo)
    in_specs = [tok(D), tok(D), tok(1024)] + [tok(512)] * 5 + [full(a) for a in ins[8:]]
    sd = lambda c: jax.ShapeDtypeStruct((bsz, t, c), F32)
    vec = lambda c: jax.ShapeDtypeStruct((1, c), F32)
    out_shape = [sd(D), sd(1024), sd(512), sd(512), sd(1536), jax.ShapeDtypeStruct((8, LANES), F32),
                 jax.ShapeDtypeStruct((1024, 1024), F32), vec(D), vec(512), vec(512), vec(512)]
    out_specs = [tok(D), tok(1024), tok(512), tok(512), tok(1536), _resident((8, LANES)), _resident((1024, 1024)),
                 _resident((1, D)), _resident((1, 512)), _resident((1, 512)), _resident((1, 512))]
    return pl.pallas_call(
        body, name="post", grid=(bsz, nt), out_shape=out_shape, in_specs=in_specs, out_specs=out_specs,
        compiler_params=_cparams(("arbitrary", "arbitrary")),
    )(*ins)


def _pre_bwd_a(pp, pos, invf, cqkv_w, mu, w0, w2p, w2pt, a0, a2p, a2pt, k_k, k_a, bo,
               dq, dk, dva, dwkv, dbon):
    gq, wuqt, gkv, wukvt = cqkv_w
    bsz, t, _ = pp.shape
    tt = TT_VPU
    nt = t // tt
    dr_w, dw_w, dkp_w, dv_w, dal_w, dbe_w = dwkv

    def body(pp_ref, pos_ref, invf_ref, gq_ref, wuqt_ref, gkv_ref, wukvt_ref, mu_ref, w0_ref, w2p_ref, w2pt_ref,
             a0_ref, a2p_ref, a2pt_ref, kk_ref, ka_ref, bo_ref, dq_ref, dk_ref, dva_ref,
             dr_ref, dw_ref, dkp_ref, dv_ref, dal_ref, dbe_ref, dbon_ref,
             da_ref, dwuq_ref, dwukv_ref, dw2p_ref, da2p_ref, dgq_ref, dgkv_ref, dmu_ref, dw0_ref, da0_ref,
             dkk_ref, dka_ref, carry):
        i = pl.program_id(1)
        first = (pl.program_id(0) == 0) & (i == 0)

        @pl.when(first)
        def _():
            for ref in (dwuq_ref, dwukv_ref, dw2p_ref, da2p_ref, dgq_ref, dgkv_ref, dmu_ref, dw0_ref, da0_ref,
                        dkk_ref, dka_ref):
                ref[...] = jnp.zeros(ref.shape, F32)

        bo_m = bo_ref[...]
        rowsum = lambda a: jnp.sum(a, axis=0, keepdims=True)
        prw = pp_ref[0, :, RW0:DP]

        @pl.when(i == 0)
        def _():
            carry[...] = jnp.zeros(carry.shape, F32)

        ps, sh = _shift_mix(prw, carry[7:8, :], mu_ref[...])
        carry[...] = prw[tt - 8:tt, :]
        k_k, k_a = kk_ref[...], ka_ref[...]
        g = _rw_gates(ps, w0_ref[...], w2p_ref[...], a0_ref[...], a2p_ref[...], k_k, k_a, bo_m)
        a, kk, k = g["a"], g["kk"], g["k"]
        dr = dr_ref[0] + dbon_ref[0, :, 0:512]
        dkp = dkp_ref[0] + dbon_ref[0, :, 512:1024]
        dv = dv_ref[0] + dbon_ref[0, :, 1024:1536]
        dbe = dbe_ref[0]
        dkk = dbe * a - dal_ref[0]
        da = dbe * kk + dkp * k * k_a
        dka_ref[...] += rowsum(dkp * k * (a - 1.0))
        dm = (dkk - kk * _seg(dkk * kk, bo_m)) / g["nrm"]
        dkk_ref[...] += rowsum(dm * k)
        dk_tot = dkp * (1.0 + (a - 1.0) * k_a) + dm * k_k
        dapre = da * a * (1.0 - a)
        da0_ref[...] += rowsum(dapre)
        dapb = dapre.astype(BF16)
        da2p_ref[...] += _dot_tn(g["misc"].astype(BF16), dapb)
        dwpre = dw_ref[0] * g["w"] * (-g["e"]) * _sigmoid(-g["wpre"])
        dw0_ref[...] += rowsum(dwpre)
        dwpb = dwpre.astype(BF16)
        th = g["th"]
        dw2p_ref[...] += _dot_tn(th.astype(BF16), dwpb)
        dmisc = _dot(dapb, a2pt_ref[...]) + _dot(dwpb, w2pt_ref[...]) * (1.0 - th * th)
        ang = pos_ref[0] * invf_ref[...]
        cs, sn = jnp.cos(ang), jnp.sin(ang)
        unrope = lambda gr: gr * cs - _rot(gr * sn)
        lane = lax.broadcasted_iota(jnp.int32, cs.shape, 1)
        dkr = dk_ref[0, :, 128:256]
        for h in range(1, HEADS):
            dkr = dkr + dk_ref[0, :, 256 * h + 128:256 * h + 256]
        dkr = jnp.where(lane < 64, unrope(dkr), 0.0)
        dmisc = dmisc + jnp.concatenate([dkr, jnp.zeros_like(dkr)], axis=1)
        dqp = jnp.concatenate(
            [blk for h in range(HEADS)
             for blk in (dq_ref[0, :, 256 * h:256 * h + 128], unrope(dq_ref[0, :, 256 * h + 128:256 * h + 256]))],
            axis=1).astype(BF16)
        dkvp = jnp.concatenate([dk_ref[0, :, 256 * h:256 * h + 128] for h in range(HEADS)] + [dva_ref[0]],
                               axis=1).astype(BF16)
        cqn, cq_nx, cq_rstd = _rms(pp_ref[0, :, CQ0:CQ0 + 256], gq_ref[...], 256)
        ckvn, ckv_nx, ckv_rstd = _rms(pp_ref[0, :, CKV0:CKV0 + 128], gkv_ref[...], 128)
        dwuq_ref[...] += _dot_tn(cqn.astype(BF16), dqp)
        dwukv_ref[...] += _dot_tn(ckvn.astype(BF16), dkvp)
        dcq, dgq = _rms_bwd(_dot(dqp, wuqt_ref[...]), cq_nx, cq_rstd, gq_ref[...], 256)
        dckv, dgkv = _rms_bwd(_dot(dkvp, wukvt_ref[...]), ckv_nx, ckv_rstd, gkv_ref[...], 128)
        dgq_ref[...] += dgq
        dgkv_ref[...] += dgkv
        dps = jnp.concatenate([dr, dk_tot, dv, dmisc], axis=1)
        dmu_ref[...] += rowsum(dps * (sh - prw))
        da_ref[0, :, 0:256] = dcq
        da_ref[0, :, 256:384] = dckv
        da_ref[0, :, 384:384 + NRW] = dps

    tok = lambda c: pl.BlockSpec((1, tt, c), lambda b, i: (b, i, 0))
    full = lambda a: _full(a.shape)
    ins = (pp, pos, invf, gq, wuqt, gkv, wukvt, mu, w0, w2p, w2pt, a0, a2p, a2pt, k_k, k_a, bo,
           dq, dk, dva, dr_w, dw_w, dkp_w, dv_w, dal_w, dbe_w, dbon)
    in_specs = ([tok(DP), tok(1)] + [full(a) for a in ins[2:17]] + [tok(1024), tok(1024), tok(512)]
                + [tok(512)] * 6 + [tok(1536)])
    shp = lambda *s: jax.ShapeDtypeStruct(s, F32)
    out_shape = [shp(bsz, t, 384 + NRW), shp(256, 1024), shp(128, 1024), shp(256, 512), shp(256, 512),
                 shp(1, 256), shp(1, 128), shp(1, NRW), shp(1, 512), shp(1, 512), shp(1, 512), shp(1, 512)]
    out_specs = [tok(384 + NRW)] + [_resident(s.shape) for s in out_shape[1:]]
    return pl.pallas_call(
        body, name="pre_bwd_a", grid=(bsz, nt), out_shape=out_shape, in_specs=in_specs, out_specs=out_specs,
        scratch_shapes=[pltpu.VMEM((8, NRW), F32)],
        compiler_params=_cparams(("arbitrary", "arbitrary")),
    )(*ins)


def _pre_bwd_b(x, dh, dz, da, mu, wpt, gpre):
    bsz, t, _ = x.shape
    nt = t // TT
    nblk = t // 8

    def body(x_ref, dh_ref, dz_ref, da_ref, nxt_ref, mu_ref, wpt_ref, gpre_ref, gx_ref, dp_ref, dgpre_ref):
        i = pl.program_id(1)
        first = (pl.program_id(0) == 0) & (i == 0)

        @pl.when(first)
        def _():
            dgpre_ref[...] = jnp.zeros(dgpre_ref.shape, F32)

        mu_v = mu_ref[...]
        dps = da_ref[0, :, 384:384 + NRW]
        nxt = jnp.where(i < nt - 1, nxt_ref[0, 0:1, 384:384 + NRW], 0.0)
        row = lax.broadcasted_iota(jnp.int32, dps.shape, 0)
        up = jnp.where(row == TT - 1, nxt, pltpu.roll(dps, TT - 1, 0))
        dprw = dps * (1.0 - mu_v) + up * mu_v
        dp = jnp.concatenate([dz_ref[0], da_ref[0, :, 0:384], dprw], axis=1).astype(BF16)
        dp_ref[0] = dp
        du = _dot(dp, wpt_ref[...])
        _, nx, rstd = _rms(x_ref[0], gpre_ref[...], D)
        dx, dg = _rms_bwd(du, nx, rstd, gpre_ref[...], D)
        dgpre_ref[...] += dg
        gx_ref[0] = dh_ref[0] + dx

    tok = lambda c: pl.BlockSpec((1, TT, c), lambda b, i: (b, i, 0))
    nxt_spec = pl.BlockSpec((1, 8, 384 + NRW), lambda b, i: (b, jnp.minimum((i + 1) * (TT // 8), nblk - 1), 0))
    ins = (x, dh, dz, da, da, mu, wpt, gpre)
    return pl.pallas_call(
        body, name="pre_bwd_b", grid=(bsz, nt),
        out_shape=[jax.ShapeDtypeStruct((bsz, t, D), F32), jax.ShapeDtypeStruct((bsz, t, DP), BF16),
                   jax.ShapeDtypeStruct((1, D), F32)],
        in_specs=[tok(D), tok(D), tok(1024), tok(384 + NRW), nxt_spec, _full(mu.shape), _full(wpt.shape),
                  _full(gpre.shape)],
        out_specs=[tok(D), tok(DP), _resident((1, D))],
        compiler_params=_cparams(("arbitrary", "arbitrary")),
    )(*ins)


def _tn_matmul(a, b, bn, name, bk=512):
    kdim, m = a.shape
    _, n = b.shape
    nk = kdim // bk

    def body(a_ref, b_ref, o_ref):
        @pl.when(pl.program_id(1) == 0)
        def _():
            o_ref[...] = jnp.zeros(o_ref.shape, F32)

        o_ref[...] += _dot_tn(a_ref[...], b_ref[...])

    return pl.pallas_call(
        body, name=name, grid=(n // bn, nk),
        out_shape=jax.ShapeDtypeStruct((m, n), F32),
        in_specs=[pl.BlockSpec((bk, m), lambda j, kk: (kk, 0)), pl.BlockSpec((bk, bn), lambda j, kk: (kk, j))],
        out_specs=pl.BlockSpec((m, bn), lambda j, kk: (0, j)),
        compiler_params=_cparams(("parallel", "arbitrary")),
    )(a, b)


SHARDED = ("w_in", "mla_w_uq", "mla_w_ukv", "rw_w2", "rw_a2", "w_out")
SMALL = ("norm_pre_g", "mla_q_norm_g", "mla_kv_norm_g", "rw_mu", "rw_w0", "rw_a0", "rw_k_k", "rw_k_a", "rw_r_k",
         "rw_ln_g", "rw_ln_b", "norm_post_g")
WEIGHTS = ("norm_pre_g", "w_in", "mla_q_norm_g", "mla_w_uq", "mla_kv_norm_g", "mla_w_ukv", "rw_mu", "rw_w0", "rw_w2",
           "rw_a0", "rw_a2", "rw_k_k", "rw_k_a", "rw_r_k", "rw_ln_g", "rw_ln_b", "w_out", "norm_post_g")


def _unpack_shard(packed, like):
    out, at = {}, 0
    for n, rows in zip(SHARDED[1:], PACK_ROWS):
        out[n] = packed[at:at + rows].reshape(like[n].shape)
        at += rows
    return out


def _constants():
    bo = np.kron(np.eye(2, dtype=np.float32), np.ones((64, 64), np.float32))
    inv = ROPE_THETA ** (-np.arange(0, 64, 2, dtype=np.float32) / 64)
    invf = np.concatenate([inv, inv, np.zeros(64, np.float32)]).astype(np.float32)[None, :]
    return jnp.asarray(bo, BF16), jnp.asarray(invf)


def kernel(x, positions, norm_pre_g, w_in, mla_q_norm_g, mla_w_uq, mla_kv_norm_g, mla_w_ukv, rw_mu, rw_w0, rw_w2, rw_a0, rw_a2, rw_k_k, rw_k_a, rw_r_k, rw_ln_g, rw_ln_b, w_out, norm_post_g, loss_target, m_norm_pre_g, m_w_in, m_mla_q_norm_g, m_mla_w_uq, m_mla_kv_norm_g, m_mla_w_ukv, m_rw_mu, m_rw_w0, m_rw_w2, m_rw_a0, m_rw_a2, m_rw_k_k, m_rw_k_a, m_rw_r_k, m_rw_ln_g, m_rw_ln_b, m_w_out, m_norm_post_g, v_norm_pre_g, v_w_in, v_mla_q_norm_g, v_mla_w_uq, v_mla_kv_norm_g, v_mla_w_ukv, v_rw_mu, v_rw_w0, v_rw_w2, v_rw_a0, v_rw_a2, v_rw_k_k, v_rw_k_a, v_rw_r_k, v_rw_ln_g, v_rw_ln_b, v_w_out, v_norm_post_g):
    wts = dict(norm_pre_g=norm_pre_g, w_in=w_in, mla_q_norm_g=mla_q_norm_g, mla_w_uq=mla_w_uq,
               mla_kv_norm_g=mla_kv_norm_g, mla_w_ukv=mla_w_ukv, rw_mu=rw_mu, rw_w0=rw_w0, rw_w2=rw_w2, rw_a0=rw_a0,
               rw_a2=rw_a2, rw_k_k=rw_k_k, rw_k_a=rw_k_a, rw_r_k=rw_r_k, rw_ln_g=rw_ln_g, rw_ln_b=rw_ln_b, w_out=w_out,
               norm_post_g=norm_post_g)
    mom_m = dict(norm_pre_g=m_norm_pre_g, w_in=m_w_in, mla_q_norm_g=m_mla_q_norm_g, mla_w_uq=m_mla_w_uq,
                 mla_kv_norm_g=m_mla_kv_norm_g, mla_w_ukv=m_mla_w_ukv, rw_mu=m_rw_mu, rw_w0=m_rw_w0, rw_w2=m_rw_w2,
                 rw_a0=m_rw_a0, rw_a2=m_rw_a2, rw_k_k=m_rw_k_k, rw_k_a=m_rw_k_a, rw_r_k=m_rw_r_k, rw_ln_g=m_rw_ln_g,
                 rw_ln_b=m_rw_ln_b, w_out=m_w_out, norm_post_g=m_norm_post_g)
    mom_v = dict(norm_pre_g=v_norm_pre_g, w_in=v_w_in, mla_q_norm_g=v_mla_q_norm_g, mla_w_uq=v_mla_w_uq,
                 mla_kv_norm_g=v_mla_kv_norm_g, mla_w_ukv=v_mla_w_ukv, rw_mu=v_rw_mu, rw_w0=v_rw_w0, rw_w2=v_rw_w2,
                 rw_a0=v_rw_a0, rw_a2=v_rw_a2, rw_k_k=v_rw_k_k, rw_k_a=v_rw_k_a, rw_r_k=v_rw_r_k, rw_ln_g=v_rw_ln_g,
                 rw_ln_b=v_rw_ln_b, w_out=v_w_out, norm_post_g=v_norm_post_g)
    bsz, t, _ = x.shape
    bo, invf = _constants()

    g_in, g_uq, g_ukv, g_w2, g_a2, g_out = _ag_weights([wts[n][0] for n in SHARDED])
    w_in_f = jnp.transpose(g_in, (1, 0, 2)).reshape(D, D_IN)
    wp = jnp.concatenate([w_in_f[:, 2112:3136], w_in_f[:, 0:384], w_in_f[:, 448:1984], w_in_f[:, 384:448],
                          w_in_f[:, 1984:2112], jnp.zeros((D, 64), BF16)], axis=1)
    wuq = jnp.pad(jnp.transpose(g_uq, (1, 0, 2)).reshape(256, HEADS, 192), ((0, 0), (0, 0), (0, 64))).reshape(256, 1024)
    wukv = jnp.transpose(jnp.transpose(g_ukv, (1, 0, 2)).reshape(128, HEADS, 2, 128), (0, 2, 1, 3)).reshape(128, 1024)
    w2 = jnp.transpose(g_w2, (1, 0, 2)).reshape(64, RW)
    a2 = jnp.transpose(g_a2, (1, 0, 2)).reshape(64, RW)
    w2p = jnp.pad(w2, ((64, 128), (0, 0)))
    a2p = jnp.pad(a2, ((128, 64), (0, 0)))
    wo = g_out.reshape(D, D)
    mu = jnp.concatenate([rw_mu[:, 0:1536], jnp.zeros((1, 64), F32), rw_mu[:, 1536:1664], jnp.zeros((1, 64), F32)],
                         axis=1)
    r_k = rw_r_k.reshape(1, RW)
    pos = positions.astype(F32)[:, :, None]

    (u, pp, q_att, k_att, v_att, r, w, kp, v, al, be) = _pre_fwd(
        x, pos, invf, norm_pre_g, wp, mla_q_norm_g, wuq, mla_kv_norm_g, wukv, mu, rw_w0, w2p, rw_a0, a2p, rw_k_k,
        rw_k_a, bo)
    o, lse = _attn_fwd(q_att, k_att, v_att)
    rw_k = _spread_k([r, w, kp, al, be])
    v_v = _to_v(v)
    yw_v, states, u_v = _wkv_fwd(*rw_k, v_v)
    yw = _from_v(yw_v, bsz)

    (dh, dz, dym, dyw, dbon, loss_acc, d_wo, d_gpost, d_lng, d_lnb, d_rk) = _post(
        x, loss_target, pp, o, yw, r, kp, v, rw_ln_g, rw_ln_b, r_k, wo, wo.T, norm_post_g, bo)

    d_k = _wkv_bwd(*rw_k, v_v, _to_v(dyw), states, u_v)
    dr_w, dw_w, dkp_w, dal_w, dbe_w = _gather_k(d_k[:5], bsz)
    dwkv = (dr_w, dw_w, dkp_w, _from_v(d_k[5], bsz), dal_w, dbe_w)
    dq, dk, dva = _attn_bwd(q_att, k_att, v_att, o, lse, dym)

    (da, d_wuq, d_wukv, d_w2p, d_a2p, d_gq, d_gkv, d_mu, d_w0, d_a0, d_kk, d_ka) = _pre_bwd_a(
        pp, pos, invf, (mla_q_norm_g, wuq.T, mla_kv_norm_g, wukv.T), mu, rw_w0, w2p, w2p.T, rw_a0, a2p, a2p.T,
        rw_k_k, rw_k_a, bo, dq, dk, dva, dwkv, dbon)
    grad_x, dpb, d_gpre = _pre_bwd_b(x, dh, dz, da, mu, wp.T, norm_pre_g)
    d_wp = _tn_matmul(u.reshape(bsz * t, D), dpb.reshape(bsz * t, DP), DP, "dw_in", bk=1024)

    full_g = {
        "w_in": jnp.concatenate([d_wp[:, 1024:1408], d_wp[:, 2944:3008], d_wp[:, 1408:2944], d_wp[:, 3008:3136],
                                 d_wp[:, 0:1024]], axis=1),
        "mla_w_uq": d_wuq.reshape(256, HEADS, 256)[:, :, :192].reshape(256, 768),
        "mla_w_ukv": jnp.transpose(d_wukv.reshape(128, 2, HEADS, 128), (0, 2, 1, 3)).reshape(128, 1024),
        "rw_w2": d_w2p[64:128],
        "rw_a2": d_a2p[128:192],
        "w_out": d_wo,
    }
    small_g = {
        "norm_pre_g": d_gpre, "mla_q_norm_g": d_gq, "mla_kv_norm_g": d_gkv,
        "rw_mu": jnp.concatenate([d_mu[:, 0:1536], d_mu[:, 1600:1728]], axis=1),
        "rw_w0": d_w0, "rw_a0": d_a0, "rw_k_k": d_kk, "rw_k_a": d_ka, "rw_r_k": d_rk, "rw_ln_g": d_lng,
        "rw_ln_b": d_lnb, "norm_post_g": d_gpost,
    }

    def by_shard(name, g):
        if name == "w_out":
            return g.reshape(N_SHARD, -1, LANES)
        rows, cols = g.shape
        return jnp.transpose(g.reshape(rows, N_SHARD, cols // N_SHARD), (1, 0, 2))

    g_in = by_shard("w_in", full_g["w_in"])
    packed = jnp.concatenate([by_shard(n, full_g[n]).reshape(N_SHARD, -1, LANES) for n in SHARDED[1:]], axis=1)
    halves = [a.reshape(N_SHARD, 2, a.shape[1] // 2, a.shape[2]) for a in (g_in, packed)]
    red_in, red_rest = _rs_chips(*_rs_pairs(halves))
    g_shard = red_rest.reshape(PACK_REST, LANES)

    flat = lambda a: a.reshape(1, -1)
    g_small = _small_allreduce([flat(small_g[n]) for n in SMALL], loss_acc)
    loss = g_small[SMALL_USED, 0]

    g_sharded = _unpack_shard(g_shard, {n: wts[n][0] for n in SHARDED})
    g_sharded["w_in"] = red_in.reshape(wts["w_in"][0].shape)
    sh = _adamw([wts[n][0] for n in SHARDED], [g_sharded[n] for n in SHARDED], [mom_m[n][0] for n in SHARDED],
                [mom_v[n][0] for n in SHARDED], "adamw_sharded")
    sm = _adamw_small([flat(wts[n]) for n in SMALL], g_small, [flat(mom_m[n]) for n in SMALL],
                      [flat(mom_v[n]) for n in SMALL])

    def outputs(sharded, small):
        out = {n: a[None] for n, a in zip(SHARDED, sharded)}
        out.update({n: a.reshape(wts[n].shape) for n, a in zip(SMALL, small)})
        return out

    grads = outputs([g_sharded[n] for n in SHARDED], sm[0])
    deltas, new_m, new_v = (outputs(sh[k], sm[k + 1]) for k in range(3))
    return (loss, grad_x, *[grads[n] for n in WEIGHTS], *[deltas[n] for n in WEIGHTS],
            *[new_m[n] for n in WEIGHTS], *[new_v[n] for n in WEIGHTS])
```

```python
import numpy as np
import jax
import jax.numpy as jnp
from jax import lax
from jax.experimental import pallas as pl
from jax.experimental.pallas import tpu as pltpu

F32, BF16 = jnp.float32, jnp.bfloat16
MESH = pl.DeviceIdType.MESH

D = 1024
HEADS = 4
RW = 512
NORM_EPS = 1e-6
GN_EPS = 64e-5
ROPE_THETA = 10000.0
SCALE = (128 + 64) ** -0.5
D_IN = 3136
LR, B1, B2, ADAM_EPS, WD, STEP = 0.001, 0.9, 0.999, 1e-08, 0.01, 10

Z0, CQ0, CKV0, RW0, DP = 0, 1024, 1280, 1408, 3200
NRW = DP - RW0

LANES = 128
SUBLANES = 8
VMEM_LIMIT = 56 * 1024 * 1024

TT = 512
TT_VPU = 256
TQ = 512

N_SHARD = 4
PACK_ROWS = (256 * 192 // 128, 128 * 256 // 128, 64, 64)
PACK_REST = sum(PACK_ROWS)
SMALL_ROWS = 64
SMALL_USED = 60


def _cparams(sem=None):
    return pltpu.CompilerParams(dimension_semantics=sem, vmem_limit_bytes=VMEM_LIMIT)


def _full(shape):
    n = len(shape)
    return pl.BlockSpec(shape, lambda *_: (0,) * n, pipeline_mode=pl.Buffered(1))


def _resident(shape):
    n = len(shape)
    return pl.BlockSpec(shape, lambda *_: (0,) * n)


def _dot(a, b):
    return jnp.dot(a, b, preferred_element_type=F32)


def _dot_nt(a, b):
    return lax.dot_general(a, b, (((1,), (1,)), ((), ())), preferred_element_type=F32)


def _dot_tn(a, b):
    return lax.dot_general(a, b, (((0,), (0,)), ((), ())), preferred_element_type=F32)


def _split3(x):
    hi = x.astype(BF16)
    r1 = x - hi.astype(F32)
    mid = r1.astype(BF16)
    lo = (r1 - mid.astype(F32)).astype(BF16)
    return hi, mid, lo


def _seg(x, bo):
    rows, nblk = x.shape[0], x.shape[1] // LANES
    pieces = [p for i in range(nblk) for p in _split3(x[:, LANES * i:LANES * (i + 1)])]
    res = _dot(jnp.concatenate(pieces, axis=0), bo)
    parts = [res[(3 * i) * rows:(3 * i + 1) * rows] + res[(3 * i + 1) * rows:(3 * i + 2) * rows]
             + res[(3 * i + 2) * rows:(3 * i + 3) * rows] for i in range(nblk)]
    return parts[0] if nblk == 1 else jnp.concatenate(parts, axis=1)


def _rms(x, g, n):
    rstd = lax.rsqrt(jnp.sum(x * x, axis=-1, keepdims=True) * (1.0 / n) + NORM_EPS)
    nx = x * rstd
    return nx * g, nx, rstd


def _rms_bwd(dy, nx, rstd, g, n):
    dn = dy * g
    dx = rstd * (dn - nx * (jnp.sum(dn * nx, axis=-1, keepdims=True) * (1.0 / n)))
    return dx, jnp.sum(dy * nx, axis=0, keepdims=True)


def _rot(x):
    lane = lax.broadcasted_iota(jnp.int32, x.shape, 1)
    return jnp.where((lane % 64) < 32, -pltpu.roll(x, x.shape[1] - 32, 1), pltpu.roll(x, 32, 1))


def _sigmoid(x):
    return 1.0 / (1.0 + jnp.exp(-x))


def _softplus(x):
    return jnp.maximum(x, 0.0) + jnp.log(1.0 + jnp.exp(-jnp.abs(x)))


def _rw_gates(ps, w0, w2p, a0, a2p, k_k, k_a, bo):
    r, k, v, misc = ps[:, 0:512], ps[:, 512:1024], ps[:, 1024:1536], ps[:, 1536:NRW]
    th = jnp.tanh(misc)
    wpre = w0 + _dot(th.astype(BF16), w2p)
    e = jnp.exp(-_softplus(-wpre) - 0.5)
    w = jnp.exp(-e)
    a = _sigmoid(a0 + _dot(misc.astype(BF16), a2p))
    m = k * k_k
    nrm = jnp.maximum(jnp.sqrt(_seg(m * m, bo)), 1e-12)
    kk = m / nrm
    kp = k * (1.0 + (a - 1.0) * k_a)
    return dict(r=r, k=k, v=v, misc=misc, th=th, wpre=wpre, e=e, w=w, a=a, nrm=nrm, kk=kk, kp=kp)


def _shift_mix(prw, prev_row, mu):
    row = lax.broadcasted_iota(jnp.int32, prw.shape, 0)
    sh = jnp.where(row == 0, prev_row, pltpu.roll(prw, 1, 0))
    return prw + (sh - prw) * mu, sh


def _ag_weights(shards):
    n = len(shards)

    def body(*refs):
        ins, outs = refs[:n], refs[n:2 * n]
        ici_send, ici_recv, d2d_send, d2d_recv = refs[2 * n:2 * n + 4]
        x, y, c = lax.axis_index("x"), lax.axis_index("y"), lax.axis_index("c")
        mine = 2 * x + y
        for w in range(n):
            outs[w][mine] = ins[w][...].astype(BF16)
        flips = ((1, 0), (0, 1), (1, 1))

        def half(w, shard, cc):
            rows = outs[w].shape[1] // 2
            return outs[w].at[shard, pl.ds(pl.multiple_of(cc * rows, 16), rows)]

        def ici(w, k, shard):
            fx, fy = flips[k]
            return pltpu.make_async_remote_copy(
                src_ref=half(w, shard, c), dst_ref=half(w, shard, c),
                send_sem=ici_send.at[w * 3 + k], recv_sem=ici_recv.at[w * 3 + k],
                device_id=(x ^ fx, y ^ fy, c), device_id_type=MESH)

        def d2d(w, k, cc):
            fx, fy = flips[k]
            theirs = 2 * (x ^ fx) + (y ^ fy)
            return pltpu.make_async_remote_copy(
                src_ref=half(w, theirs, cc), dst_ref=half(w, theirs, cc),
                send_sem=d2d_send.at[w * 3 + k], recv_sem=d2d_recv.at[w * 3 + k],
                device_id=(x, y, 1 - c), device_id_type=MESH)

        for w in range(n):
            for k in range(3):
                ici(w, k, mine).start()
        for w in range(n):
            for k in range(3):
                fx, fy = flips[k]
                ici(w, k, 2 * (x ^ fx) + (y ^ fy)).wait_recv()
                d2d(w, k, c).start()
        for w in range(n):
            for k in range(3):
                d2d(w, k, 1 - c).wait_recv()
        for w in range(n):
            for k in range(3):
                ici(w, k, mine).wait_send()
                d2d(w, k, c).wait_send()

    vm = pl.BlockSpec(memory_space=pltpu.VMEM)
    return pl.pallas_call(
        body, name="ag_weights",
        out_shape=[jax.ShapeDtypeStruct((N_SHARD,) + s.shape, BF16) for s in shards],
        in_specs=[vm] * n, out_specs=[vm] * n,
        scratch_shapes=[pltpu.SemaphoreType.DMA((3 * n,))] * 4,
        compiler_params=pltpu.CompilerParams(vmem_limit_bytes=VMEM_LIMIT),
    )(*shards)


def _rs_pairs(halves):
    n = len(halves)

    def body(*refs):
        h_refs, sum_refs, sumb_refs, recvs = (refs[k * n:(k + 1) * n] for k in range(4))
        send_sem, recv_sem = refs[4 * n:]
        x, y, c = lax.axis_index("x"), lax.axis_index("y"), lax.axis_index("c")
        cps = [pltpu.make_async_remote_copy(src_ref=h_refs[i].at[s, 1 - c], dst_ref=recvs[i].at[s],
                                            send_sem=send_sem.at[i * N_SHARD + s], recv_sem=recv_sem.at[i * N_SHARD + s],
                                            device_id=(x, y, 1 - c), device_id_type=MESH)
               for i in range(n) for s in range(N_SHARD)]
        for cp in cps:
            cp.start()
        for i in range(n):
            for s in range(N_SHARD):
                cps[i * N_SHARD + s].wait_recv()
                acc = h_refs[i][s, c] + recvs[i][s]
                sum_refs[i][s] = acc
                sumb_refs[i][s] = acc.astype(BF16)
        for cp in cps:
            cp.wait_send()

    vm = pl.BlockSpec(memory_space=pltpu.VMEM)
    shapes = [(N_SHARD,) + h.shape[2:] for h in halves]
    outs = pl.pallas_call(
        body, name="rs_pairs",
        out_shape=[jax.ShapeDtypeStruct(sh, F32) for sh in shapes] + [jax.ShapeDtypeStruct(sh, BF16) for sh in shapes],
        in_specs=[vm] * n, out_specs=[vm] * (2 * n),
        scratch_shapes=[pltpu.VMEM(sh, F32) for sh in shapes] + [pltpu.SemaphoreType.DMA((n * N_SHARD,)),
                                                                 pltpu.SemaphoreType.DMA((n * N_SHARD,))],
        compiler_params=pltpu.CompilerParams(vmem_limit_bytes=VMEM_LIMIT),
    )(*halves)
    return outs[:n], outs[n:]


def _rs_chips(part_f32, part_bf16):
    n = len(part_f32)

    def body(*refs):
        own_refs, src_refs, out_refs, recvs = (refs[k * n:(k + 1) * n] for k in range(4))
        ici_send, ici_recv, d2d_send, d2d_recv = refs[4 * n:]
        x, y, c = lax.axis_index("x"), lax.axis_index("y"), lax.axis_index("c")
        mine = 2 * x + y
        flips = ((1, 0), (0, 1), (1, 1))
        cps = []
        for i in range(n):
            for k, (fx, fy) in enumerate(flips):
                theirs = 2 * (x ^ fx) + (y ^ fy)
                cps.append(pltpu.make_async_remote_copy(
                    src_ref=src_refs[i].at[theirs], dst_ref=recvs[i].at[k],
                    send_sem=ici_send.at[3 * i + k], recv_sem=ici_recv.at[3 * i + k],
                    device_id=(x ^ fx, y ^ fy, c), device_id_type=MESH))
        for cp in cps:
            cp.start()
        handed = []
        for i in range(n):
            acc = own_refs[i][mine]
            for k in range(3):
                cps[3 * i + k].wait_recv()
                acc = acc + recvs[i][k].astype(F32)
            out_refs[i][c] = acc
            to_sibling = pltpu.make_async_remote_copy(
                src_ref=out_refs[i].at[c], dst_ref=out_refs[i].at[c], send_sem=d2d_send.at[i], recv_sem=d2d_recv.at[i],
                device_id=(x, y, 1 - c), device_id_type=MESH)
            to_sibling.start()
            handed.append(to_sibling)
        for i in range(n):
            pltpu.make_async_remote_copy(
                src_ref=out_refs[i].at[1 - c], dst_ref=out_refs[i].at[1 - c], send_sem=d2d_send.at[i],
                recv_sem=d2d_recv.at[i], device_id=(x, y, 1 - c), device_id_type=MESH).wait_recv()
        for cp in handed + cps:
            cp.wait_send()

    vm = pl.BlockSpec(memory_space=pltpu.VMEM)
    return pl.pallas_call(
        body, name="rs_chips",
        out_shape=[jax.ShapeDtypeStruct((2,) + p.shape[1:], F32) for p in part_f32],
        in_specs=[vm] * (2 * n), out_specs=[vm] * n,
        scratch_shapes=[pltpu.VMEM((3,) + p.shape[1:], BF16) for p in part_bf16]
        + [pltpu.SemaphoreType.DMA((3 * n,)), pltpu.SemaphoreType.DMA((3 * n,)), pltpu.SemaphoreType.DMA((n,)),
           pltpu.SemaphoreType.DMA((n,))],
        compiler_params=pltpu.CompilerParams(vmem_limit_bytes=VMEM_LIMIT),
    )(*part_f32, *part_bf16)


def _small_rows(vecs):
    out, at = [], 0
    for vec in vecs:
        rows = vec.shape[1] // LANES
        out.append((rows, at))
        at += rows
    assert at == SMALL_USED
    return out


def _small_allreduce(vecs, loss_acc):
    n = len(vecs)
    layout = _small_rows(vecs)

    def body(*refs):
        loss_ref, out_ref, stage, recv, send_sems, recv_sems = refs[n:]
        for vec_ref, (rows, at) in zip(refs[:n], layout):
            for j in range(rows):
                stage[at + j:at + j + 1, :] = vec_ref[0:1, LANES * j:LANES * (j + 1)]
        stage[SMALL_USED:SMALL_ROWS, :] = loss_ref[0:SMALL_ROWS - SMALL_USED, :]
        x, y, c = lax.axis_index("x"), lax.axis_index("y"), lax.axis_index("c")
        me = 4 * x + 2 * y + c
        cps = []
        for k in range(1, 8):
            fx, fy, fc = (k >> 2) & 1, (k >> 1) & 1, k & 1
            cps.append(pltpu.make_async_remote_copy(
                src_ref=stage, dst_ref=recv.at[k - 1],
                send_sem=send_sems.at[k - 1], recv_sem=recv_sems.at[k - 1],
                device_id=(x ^ fx, y ^ fy, c ^ fc), device_id_type=MESH))
        for cp in cps:
            cp.start()
        for cp in cps:
            cp.wait()
        acc = jnp.zeros(stage.shape, F32)
        for j in range(8):
            slot = jnp.maximum((me ^ j) - 1, 0)
            acc = acc + jnp.where(me == j, stage[...], recv[slot])
        out_ref[...] = acc

    vm = pl.BlockSpec(memory_space=pltpu.VMEM)
    shape = (SMALL_ROWS, LANES)
    return pl.pallas_call(
        body, name="small_allreduce",
        out_shape=jax.ShapeDtypeStruct(shape, F32),
        in_specs=[vm] * (n + 1), out_specs=vm,
        scratch_shapes=[pltpu.VMEM(shape, F32), pltpu.VMEM((7,) + shape, F32), pltpu.SemaphoreType.DMA((7,)),
                        pltpu.SemaphoreType.DMA((7,))],
    )(*vecs, loss_acc)


def _adamw_small(ws, g_packed, ms, vs):
    n = len(ws)
    layout = _small_rows(ws)

    def body(*refs):
        g_ref = refs[3 * n]
        outs = refs[3 * n + 1:]
        for i, (rows, at) in enumerate(layout):
            w_ref, m_ref, v_ref = refs[i], refs[n + i], refs[2 * n + i]
            go_ref, d_ref, nm_ref, nv_ref = (outs[k * n + i] for k in range(4))
            for j in range(rows):
                lanes = slice(LANES * j, LANES * (j + 1))
                gg = g_ref[at + j:at + j + 1, :]
                nm = B1 * m_ref[0:1, lanes] + (1.0 - B1) * gg
                nv = B2 * v_ref[0:1, lanes] + (1.0 - B2) * (gg * gg)
                m_hat = nm / (1.0 - B1 ** STEP)
                v_hat = nv / (1.0 - B2 ** STEP)
                go_ref[0:1, lanes] = gg
                d_ref[0:1, lanes] = -LR * (m_hat / (jnp.sqrt(v_hat) + ADAM_EPS) + WD * w_ref[0:1, lanes])
                nm_ref[0:1, lanes] = nm
                nv_ref[0:1, lanes] = nv

    vm = pl.BlockSpec(memory_space=pltpu.VMEM)
    sds = [jax.ShapeDtypeStruct(w.shape, F32) for w in ws]
    outs = pl.pallas_call(
        body, name="adamw_small", out_shape=sds * 4, in_specs=[vm] * (3 * n + 1), out_specs=[vm] * (4 * n),
    )(*ws, *ms, *vs, g_packed)
    return outs[:n], outs[n:2 * n], outs[2 * n:3 * n], outs[3 * n:]


ADAM_ROWS = 64


def _adamw(ws, gs, ms, vs, name):
    n = len(ws)

    def body(*refs):
        for i in range(n):
            w_ref, g_ref, m_ref, v_ref = (refs[k * n + i] for k in range(4))
            d_ref, nm_ref, nv_ref = (refs[(4 + k) * n + i] for k in range(3))
            rows = min(ADAM_ROWS, w_ref.shape[0])

            def chunk(r, _):
                at = pl.ds(pl.multiple_of(r * rows, SUBLANES), rows)
                gg = g_ref[at, :]
                nm = B1 * m_ref[at, :] + (1.0 - B1) * gg
                nv = B2 * v_ref[at, :] + (1.0 - B2) * (gg * gg)
                m_hat = nm / (1.0 - B1 ** STEP)
                v_hat = nv / (1.0 - B2 ** STEP)
                d_ref[at, :] = -LR * (m_hat / (jnp.sqrt(v_hat) + ADAM_EPS) + WD * w_ref[at, :])
                nm_ref[at, :] = nm
                nv_ref[at, :] = nv
                return 0

            lax.fori_loop(0, w_ref.shape[0] // rows, chunk, 0)

    vm = pl.BlockSpec(memory_space=pltpu.VMEM)
    sds = [jax.ShapeDtypeStruct(w.shape, F32) for w in ws]
    outs = pl.pallas_call(
        body, name=name, out_shape=sds * 3, in_specs=[vm] * (4 * n), out_specs=[vm] * (3 * n),
        compiler_params=pltpu.CompilerParams(vmem_limit_bytes=VMEM_LIMIT),
    )(*ws, *gs, *ms, *vs)
    return outs[:n], outs[n:2 * n], outs[2 * n:]


def _pre_fwd(x, pos, invf, gpre, wp, gq, wuq, gkv, wukv, mu, w0, w2p, a0, a2p, k_k, k_a, bo):
    bsz, t, _ = x.shape
    nt = t // TT

    def body(x_ref, pos_ref, invf_ref, gpre_ref, wp_ref, gq_ref, wuq_ref, gkv_ref, wukv_ref, mu_ref, w0_ref,
             w2p_ref, a0_ref, a2p_ref, kk_ref, ka_ref, bo_ref,
             u_ref, pp_ref, q_ref, k_ref, v_ref, r_o, w_o, kp_o, vv_o, al_o, be_o, carry):
        i = pl.program_id(1)
        u, _, _ = _rms(x_ref[0], gpre_ref[...], D)
        ub = u.astype(BF16)
        u_ref[0] = ub
        p = _dot(ub, wp_ref[...])
        pp_ref[0] = p
        prw = p[:, RW0:DP]

        @pl.when(i == 0)
        def _():
            carry[...] = jnp.zeros(carry.shape, F32)

        ps, _ = _shift_mix(prw, carry[7:8, :], mu_ref[...])
        carry[...] = prw[TT - 8:TT, :]

        g = _rw_gates(ps, w0_ref[...], w2p_ref[...], a0_ref[...], a2p_ref[...], kk_ref[...], ka_ref[...],
                      bo_ref[...])
        r_o[0] = g["r"]
        w_o[0] = g["w"]
        kp_o[0] = g["kp"]
        vv_o[0] = g["v"]
        al_o[0] = -g["kk"]
        be_o[0] = g["kk"] * g["a"]

        cqn, _, _ = _rms(p[:, CQ0:CQ0 + 256], gq_ref[...], 256)
        q = _dot(cqn.astype(BF16), wuq_ref[...])
        ckvn, _, _ = _rms(p[:, CKV0:CKV0 + 128], gkv_ref[...], 128)
        kv = _dot(ckvn.astype(BF16), wukv_ref[...])
        ang = pos_ref[0] * invf_ref[...]
        cs, sn = jnp.cos(ang), jnp.sin(ang)
        lane = lax.broadcasted_iota(jnp.int32, cs.shape, 1)
        kr = ps[:, 1536:1536 + LANES]
        kr = jnp.where(lane < 64, kr * cs + _rot(kr) * sn, 0.0).astype(BF16)
        for h in range(HEADS):
            qr = q[:, 256 * h + 128:256 * h + 256]
            q_ref[0, :, 256 * h:256 * h + 128] = q[:, 256 * h:256 * h + 128].astype(BF16)
            q_ref[0, :, 256 * h + 128:256 * h + 256] = (qr * cs + _rot(qr) * sn).astype(BF16)
            k_ref[0, :, 256 * h:256 * h + 128] = kv[:, 128 * h:128 * h + 128].astype(BF16)
            k_ref[0, :, 256 * h + 128:256 * h + 256] = kr
        v_ref[0] = kv[:, 512:1024].astype(BF16)

    tok = lambda c: pl.BlockSpec((1, TT, c), lambda b, i: (b, i, 0))
    full = lambda a: _full(a.shape)
    ins = (x, pos, invf, gpre, wp, gq, wuq, gkv, wukv, mu, w0, w2p, a0, a2p, k_k, k_a, bo)
    in_specs = [tok(D), tok(1)] + [full(a) for a in ins[2:]]
    sd = lambda c, dt: jax.ShapeDtypeStruct((bsz, t, c), dt)
    out_shape = [sd(D, BF16), sd(DP, F32), sd(1024, BF16), sd(1024, BF16), sd(512, BF16)] + [sd(RW, F32)] * 6
    out_specs = [tok(D), tok(DP), tok(1024), tok(1024), tok(512)] + [tok(RW)] * 6
    return pl.pallas_call(
        body, name="pre_fwd", grid=(bsz, nt), out_shape=out_shape, in_specs=in_specs, out_specs=out_specs,
        scratch_shapes=[pltpu.VMEM((8, NRW), F32)],
        compiler_params=_cparams(("arbitrary", "arbitrary")),
    )(*ins)


def _attn_fwd(q, k, v):
    bsz, t, _ = q.shape
    nq = t // TQ

    hps = HEADS

    def body(q_ref, k_ref, v_ref, o_ref, lse_ref):
        i = pl.program_id(2)

        def step(j, carry, diagonal):
            at = pl.ds(pl.multiple_of(j * TQ, TQ), TQ)
            out = []
            for hh in range(hps):
                m, l, acc = carry[hh]
                s = _dot_nt(q_ref[0, :, 256 * hh:256 * (hh + 1)], k_ref[0, at, 256 * hh:256 * (hh + 1)]) * SCALE
                if diagonal:
                    s = jnp.where(lax.broadcasted_iota(jnp.int32, (TQ, TQ), 1)
                                  <= lax.broadcasted_iota(jnp.int32, (TQ, TQ), 0), s, -1e30)
                mn = jnp.maximum(m, jnp.max(s, axis=1, keepdims=True))
                p = jnp.exp(s - mn)
                al = jnp.exp(m - mn)
                l = al * l + jnp.sum(p, axis=1, keepdims=True)
                acc = al * acc + _dot(p.astype(BF16), v_ref[0, at, LANES * hh:LANES * (hh + 1)])
                out.append((mn, l, acc))
            return tuple(out)

        start = (jnp.full((TQ, 1), -1e30, F32), jnp.zeros((TQ, 1), F32), jnp.zeros((TQ, LANES), F32))
        before = lax.fori_loop(0, i, lambda j, carry: step(j, carry, False), (start,) * hps)
        for hh, (m, l, acc) in enumerate(step(i, before, True)):
            o_ref[0, :, LANES * hh:LANES * (hh + 1)] = acc / l
            lse_ref[0, hh] = jnp.broadcast_to(m + jnp.log(l), (TQ, LANES))

    return pl.pallas_call(
        body, name="attn_fwd", grid=(bsz, HEADS // hps, nq),
        out_shape=[jax.ShapeDtypeStruct((bsz, t, 512), F32), jax.ShapeDtypeStruct((bsz, HEADS, t, LANES), F32)],
        in_specs=[pl.BlockSpec((1, TQ, 256 * hps), lambda b, h, i: (b, i, h)),
                  pl.BlockSpec((1, t, 256 * hps), lambda b, h, i: (b, 0, h)),
                  pl.BlockSpec((1, t, LANES * hps), lambda b, h, i: (b, 0, h))],
        out_specs=[pl.BlockSpec((1, TQ, LANES * hps), lambda b, h, i: (b, i, h)),
                   pl.BlockSpec((1, hps, TQ, LANES), lambda b, h, i: (b, h, i, 0))],
        compiler_params=_cparams(("parallel", "parallel", "arbitrary")),
    )(q, k, v)


def _attn_bwd(q, k, v, o, lse, do):
    bsz, t, _ = q.shape
    nq = t // TQ

    def body(q_ref, k_ref, v_ref, o_ref, lse_ref, do_ref, dq_ref, dk_ref, dv_ref, dl_ref):
        j = pl.program_id(2)

        @pl.when(j == 0)
        def _():
            def prep(i, _):
                at = pl.ds(pl.multiple_of(i * TQ, TQ), TQ)
                for hh in range(2):
                    lanes = slice(LANES * hh, LANES * (hh + 1))
                    dl_ref[hh, at, :] = jnp.broadcast_to(
                        jnp.sum(do_ref[0, at, lanes] * o_ref[0, at, lanes], axis=1, keepdims=True), (TQ, LANES))
                return 0

            lax.fori_loop(0, nq, prep, 0)
            dq_ref[0] = jnp.zeros((t, 512), F32)

        def q_tile(i, carry, diagonal):
            atq = pl.ds(pl.multiple_of(i * TQ, TQ), TQ)
            out = []
            for hh in range(2):
                dk, dv = carry[hh]
                wide, narrow = slice(256 * hh, 256 * (hh + 1)), slice(LANES * hh, LANES * (hh + 1))
                qt, kt, vt = q_ref[0, atq, wide], k_ref[0, :, wide], v_ref[0, :, narrow]
                dob = do_ref[0, atq, narrow].astype(BF16)
                s = _dot_nt(qt, kt) * SCALE
                if diagonal:
                    s = jnp.where(lax.broadcasted_iota(jnp.int32, (TQ, TQ), 1)
                                  <= lax.broadcasted_iota(jnp.int32, (TQ, TQ), 0), s, -1e30)
                p = jnp.exp(s - lse_ref[0, hh, atq, :][:, 0:1])
                dv = dv + _dot_tn(p.astype(BF16), dob)
                dp = _dot_nt(dob, vt)
                ds = (p * (dp - dl_ref[hh, atq, :][:, 0:1]) * SCALE).astype(BF16)
                dk = dk + _dot_tn(ds, qt)
                dq_ref[0, atq, wide] += _dot(ds, kt)
                out.append((dk, dv))
            return tuple(out)

        zero = (jnp.zeros((TQ, 256), F32), jnp.zeros((TQ, LANES), F32))
        first = q_tile(j, (zero, zero), True)
        done = lax.fori_loop(j + 1, nq, lambda i, carry: q_tile(i, carry, False), first)
        for hh, (dk, dv) in enumerate(done):
            dk_ref[0, :, 256 * hh:256 * (hh + 1)] = dk
            dv_ref[0, :, LANES * hh:LANES * (hh + 1)] = dv

    whole = lambda c: pl.BlockSpec((1, t, c), lambda b, h, j: (b, 0, h))
    tile = lambda c: pl.BlockSpec((1, TQ, c), lambda b, h, j: (b, j, h))
    return pl.pallas_call(
        body, name="attn_bwd", grid=(bsz, HEADS // 2, nq),
        out_shape=[jax.ShapeDtypeStruct((bsz, t, 1024), F32), jax.ShapeDtypeStruct((bsz, t, 1024), F32),
                   jax.ShapeDtypeStruct((bsz, t, 512), F32)],
        in_specs=[whole(512), tile(512), tile(256), whole(256),
                  pl.BlockSpec((1, 2, t, LANES), lambda b, h, j: (b, h, 0, 0)), whole(256)],
        out_specs=[whole(512), tile(512), tile(256)],
        scratch_shapes=[pltpu.VMEM((2, t, LANES), F32)],
        compiler_params=_cparams(("parallel", "parallel", "arbitrary")),
    )(q, k, v, o, lse, do)


RW_HEADS = 8
CH = 32


def _lane_split(bsz):
    vs = LANES // (bsz * RW_HEADS)
    return vs, 64 // vs


def _gather_matrix(bsz):
    group = bsz * RW_HEADS
    vs = LANES // group
    half = (RW_HEADS // 2) * bsz * SPREAD_STEPS
    p = np.zeros((SPREAD_STEPS // vs * LANES, 2 * half), np.float32)
    for g2 in range(SPREAD_STEPS // vs):
        for j in range(vs):
            for b in range(bsz):
                for h in range(RW_HEADS):
                    hp, hpar = h // 2, h % 2
                    p[g2 * LANES + j * group + b * RW_HEADS + h,
                      hpar * half + (hp * bsz + b) * SPREAD_STEPS + g2 * vs + j] = 1.0
    return jnp.asarray(np.concatenate([p] * 3, axis=0), BF16)


def _gather_k(ys, bsz):
    vs = LANES // (bsz * RW_HEADS)
    assert (RW_HEADS // 2) * bsz * SPREAD_STEPS == LANES, "the transposed tile must be 128 lanes wide"
    tg = ys[0].shape[0]
    n = len(ys)
    ngrp = GATHER_BLOCK // SPREAD_STEPS
    per = SPREAD_STEPS // vs

    def body(*refs):
        pm = refs[n][...]
        for y_ref, o_ref in zip(refs[:n], refs[n + 1:]):
            lhs = jnp.concatenate(
                [jnp.concatenate(_split3(jnp.concatenate([y_ref[per * m + g2] for g2 in range(per)], axis=1)), axis=1)
                 for m in range(ngrp)], axis=0)
            a = _dot(lhs, pm)
            for m in range(ngrp):
                am = a[64 * m:64 * (m + 1)]
                bt = jnp.concatenate([am[:, 0:LANES], am[:, LANES:2 * LANES]], axis=0).T
                for hp in range(RW_HEADS // 2):
                    for b in range(bsz):
                        at = (hp * bsz + b) * SPREAD_STEPS
                        o_ref[b, SPREAD_STEPS * m:SPREAD_STEPS * (m + 1), LANES * hp:LANES * (hp + 1)] = \
                            bt[at:at + SPREAD_STEPS]

    pm = _gather_matrix(bsz)
    return pl.pallas_call(
        body, name="wkv_gather", grid=(tg * vs // GATHER_BLOCK,),
        out_shape=[jax.ShapeDtypeStruct((bsz, tg * vs, RW), F32)] * n,
        in_specs=[pl.BlockSpec((GATHER_BLOCK // vs, 64, LANES), lambda i: (i, 0, 0))] * n + [_full(pm.shape)],
        out_specs=[pl.BlockSpec((bsz, GATHER_BLOCK, RW), lambda i: (0, i, 0))] * n,
        compiler_params=_cparams(("parallel",)),
    )(*ys, pm)


def _to_v(x):
    bsz, t, _ = x.shape
    vs, vq = _lane_split(bsz)
    return jnp.transpose(x.reshape(bsz, t, RW_HEADS, vs, vq), (1, 4, 3, 0, 2)).reshape(t, vq, LANES)


def _from_v(y, bsz):
    t = y.shape[0]
    vs, vq = _lane_split(bsz)
    return jnp.transpose(y.reshape(t, vq, vs, bsz, RW_HEADS), (3, 0, 4, 2, 1)).reshape(bsz, t, RW)


def _ksum(a):
    return jnp.sum(a, axis=0, keepdims=True)


def _fold(a, group):
    sh = LANES // 2
    while sh >= group:
        a = a + pltpu.roll(a, sh, 1)
        sh //= 2
    return a


def _lane_group(shape, group):
    return lax.broadcasted_iota(jnp.int32, shape, 1) // group


SPREAD_STEPS = 8
SPREAD_BLOCK = 64
GATHER_BLOCK = 128


def _spread_matrix(bsz):
    group = bsz * RW_HEADS
    vs = LANES // group
    rows = (RW_HEADS // 2) * bsz * SPREAD_STEPS
    q = np.zeros((2, rows, SPREAD_STEPS * LANES), np.float32)
    for hpar in range(2):
        for hp in range(RW_HEADS // 2):
            for b in range(bsz):
                for st in range(SPREAD_STEPS):
                    row = (hp * bsz + b) * SPREAD_STEPS + st
                    for s in range(vs):
                        q[hpar, row, st * LANES + s * group + b * RW_HEADS + 2 * hp + hpar] = 1.0
    return jnp.asarray(np.concatenate([q[0], q[1]] * 3, axis=0), BF16)


def _spread_k(xs):
    bsz, t, _ = xs[0].shape
    assert (RW_HEADS // 2) * bsz * SPREAD_STEPS == LANES, "the transposed tile must be 128 lanes wide"
    n = len(xs)
    ngrp = SPREAD_BLOCK // SPREAD_STEPS

    def body(*refs):
        qm = refs[n][...]
        for x_ref, o_ref in zip(refs[:n], refs[n + 1:]):
            cols = [[] for _ in range(6)]
            for m in range(ngrp):
                at = slice(SPREAD_STEPS * m, SPREAD_STEPS * (m + 1))
                x8 = jnp.concatenate([x_ref[b, at, LANES * hp:LANES * (hp + 1)]
                                      for hp in range(RW_HEADS // 2) for b in range(bsz)], axis=0)
                for pi, piece in enumerate(_split3(x8.T)):
                    cols[2 * pi].append(piece[0:64])
                    cols[2 * pi + 1].append(piece[64:128])
            lhs = jnp.concatenate([jnp.concatenate(c, axis=0) for c in cols], axis=1)
            y = _dot(lhs, qm)
            for m in range(ngrp):
                for st in range(SPREAD_STEPS):
                    o_ref[SPREAD_STEPS * m + st] = y[64 * m:64 * (m + 1), LANES * st:LANES * (st + 1)]

    qm = _spread_matrix(bsz)
    return pl.pallas_call(
        body, name="wkv_spread", grid=(t // SPREAD_BLOCK,),
        out_shape=[jax.ShapeDtypeStruct((t, 64, LANES), F32)] * n,
        in_specs=[pl.BlockSpec((bsz, SPREAD_BLOCK, RW), lambda i: (0, i, 0))] * n + [_full(qm.shape)],
        out_specs=[pl.BlockSpec((SPREAD_BLOCK, 64, LANES), lambda i: (i, 0, 0))] * n,
        compiler_params=_cparams(("parallel",)),
    )(*xs, qm)


def _wkv_fwd(r, w, kp, al, be, v):
    t, vq = v.shape[0], v.shape[1]

    def body(r_ref, w_ref, kp_ref, al_ref, be_ref, v_ref, y_ref, a_ref, u_ref, st_ref):
        @pl.when(pl.program_id(0) == 0)
        def _():
            st_ref[...] = jnp.zeros(st_ref.shape, F32)

        def step(tl, _):
            rv, wv, kv, av, bv = r_ref[tl], w_ref[tl], kp_ref[tl], al_ref[tl], be_ref[tl]
            vals = v_ref[tl]
            yrows, urows = [], []
            for q in range(vq):
                s = st_ref[q]
                u = _ksum(s * av)
                s = s * wv + bv * u + kv * vals[q:q + 1]
                st_ref[q] = s
                a_ref[tl, q] = s
                urows.append(u)
                yrows.append(_ksum(s * rv))
            y_ref[tl] = jnp.concatenate(yrows, axis=0)
            u_ref[tl] = jnp.concatenate(urows, axis=0)
            return 0

        lax.fori_loop(0, CH, step, 0)

    kspec = pl.BlockSpec((CH, 64, LANES), lambda i: (i, 0, 0))
    vspec = pl.BlockSpec((CH, vq, LANES), lambda i: (i, 0, 0))
    vsd = jax.ShapeDtypeStruct((t, vq, LANES), F32)
    return pl.pallas_call(
        body, name="wkv_fwd", grid=(t // CH,),
        out_shape=[vsd, jax.ShapeDtypeStruct((t, vq, 64, LANES), F32), vsd],
        in_specs=[kspec] * 5 + [vspec],
        out_specs=[vspec, pl.BlockSpec((CH, vq, 64, LANES), lambda i: (i, 0, 0, 0)), vspec],
        scratch_shapes=[pltpu.VMEM((vq, 64, LANES), F32)],
        compiler_params=_cparams(("arbitrary",)),
    )(r, w, kp, al, be, v)


def _wkv_bwd(r, w, kp, al, be, v, dy, states, u):
    t, vq = v.shape[0], v.shape[1]
    vs = 64 // vq
    group = LANES // vs
    n = t // CH
    ng = CH // vs

    def body(r_ref, w_ref, kp_ref, al_ref, be_ref, v_ref, dy_ref, u_ref, a_ref, ap_ref,
             dr_ref, dw_ref, dkp_ref, dal_ref, dbe_ref, dv_ref, ds_ref):
        @pl.when(pl.program_id(0) == 0)
        def _():
            ds_ref[...] = jnp.zeros(ds_ref.shape, F32)

        earliest = pl.program_id(0) == n - 1

        def reverse(i, _):
            g = ng - 1 - i
            grp = _lane_group((64, LANES), group)
            outs = None
            for j in reversed(range(vs)):
                tl = g * vs + j
                rv, wv, kv, av, bv = r_ref[tl], w_ref[tl], kp_ref[tl], al_ref[tl], be_ref[tl]
                vals, dys, us = v_ref[tl], dy_ref[tl], u_ref[tl]
                acc = None
                dvrows = []
                for q in range(vq):
                    if j > 0:
                        s_prev = a_ref[tl - 1, q]
                    else:
                        before = jnp.where(earliest, 0.0, ap_ref[0, q])
                        s_prev = jnp.where(g == 0, before, a_ref[jnp.maximum(tl - 1, 0), q])
                    dyq = dys[q:q + 1]
                    ds = ds_ref[q] + rv * dyq
                    c = _ksum(ds * bv)
                    dvrows.append(_ksum(ds * kv))
                    terms = (a_ref[tl, q] * dyq, ds * s_prev, ds * vals[q:q + 1], s_prev * c, ds * us[q:q + 1])
                    acc = terms if acc is None else tuple(a + b for a, b in zip(acc, terms))
                    ds_ref[q] = ds * wv + av * c
                dv_ref[tl] = jnp.concatenate(dvrows, axis=0)
                summed = [_fold(a, group) for a in acc]
                outs = summed if outs is None else [jnp.where(grp == j, f, o) for f, o in zip(summed, outs)]
            for ref, o in zip((dr_ref, dw_ref, dkp_ref, dal_ref, dbe_ref), outs):
                ref[g] = o
            return 0

        lax.fori_loop(0, ng, reverse, 0)

    kspec = pl.BlockSpec((CH, 64, LANES), lambda i: (n - 1 - i, 0, 0))
    gspec = pl.BlockSpec((ng, 64, LANES), lambda i: (n - 1 - i, 0, 0))
    vspec = pl.BlockSpec((CH, vq, LANES), lambda i: (n - 1 - i, 0, 0))
    ksd = jax.ShapeDtypeStruct((t // vs, 64, LANES), F32)
    return pl.pallas_call(
        body, name="wkv_bwd", grid=(n,),
        out_shape=[ksd] * 5 + [jax.ShapeDtypeStruct((t, vq, LANES), F32)],
        in_specs=[kspec] * 5 + [vspec, vspec, vspec,
                                pl.BlockSpec((CH, vq, 64, LANES), lambda i: (n - 1 - i, 0, 0, 0)),
                                pl.BlockSpec((1, vq, 64, LANES), lambda i: (jnp.maximum((n - 1 - i) * CH - 1, 0), 0, 0, 0))],
        out_specs=[gspec] * 5 + [vspec],
        scratch_shapes=[pltpu.VMEM((vq, 64, LANES), F32)],
        compiler_params=_cparams(("arbitrary",)),
    )(r, w, kp, al, be, v, dy, u, states, states)


def _post(x, tgt, pp, o, yw, r, kp, v, ln_g, ln_b, r_k, wo, wot, gpost, bo):
    bsz, t, _ = x.shape
    tt = TT_VPU
    nt = t // tt

    def body(x_ref, tgt_ref, z_ref, o_ref, yw_ref, r_ref, kp_ref, v_ref, lng_ref, lnb_ref, rk_ref, wo_ref, wot_ref,
             gpost_ref, bo_ref,
             dh_ref, dz_ref, dym_ref, dyw_ref, dbon_ref, loss_ref, dwo_ref, dgpost_ref, dlng_ref, dlnb_ref, drk_ref):
        first = (pl.program_id(0) == 0) & (pl.program_id(1) == 0)

        @pl.when(first)
        def _():
            for ref in (loss_ref, dwo_ref, dgpost_ref, dlng_ref, dlnb_ref, drk_ref):
                ref[...] = jnp.zeros(ref.shape, F32)

        bo_m = bo_ref[...]
        seg = lambda a: _seg(a, bo_m)
        rowsum = lambda a: jnp.sum(a, axis=0, keepdims=True)
        ywv, rv, kpv, vv = yw_ref[0], r_ref[0], kp_ref[0], v_ref[0]
        ln_g, r_k = lng_ref[...], rk_ref[...]
        mean = seg(ywv) * (1.0 / 64)
        yc = ywv - mean
        rstd = lax.rsqrt(seg(yc * yc) * (1.0 / 64) + GN_EPS)
        yhat = yc * rstd
        sb = seg(rv * kpv * r_k)
        y_rw = yhat * ln_g + lnb_ref[...] + sb * vv
        z = z_ref[0]
        sig = _sigmoid(z)
        sz = z * sig
        ycat = jnp.concatenate([o_ref[0], y_rw], axis=1)
        ycg = (ycat * sz).astype(BF16)
        out = _dot(ycg, wo_ref[...])
        hn, nx, rstd_o = _rms(out, gpost_ref[...], D)
        err = x_ref[0] + hn - tgt_ref[0]
        loss_ref[...] += jnp.sum(err * err) * (0.5 / D)
        dh = err * (1.0 / D)
        dh_ref[0] = dh
        dout, dgp = _rms_bwd(dh, nx, rstd_o, gpost_ref[...], D)
        dgpost_ref[...] += dgp
        doutb = dout.astype(BF16)
        dwo_ref[...] += _dot_tn(ycg, doutb)
        dycg = _dot(doutb, wot_ref[...])
        dz_ref[0] = dycg * ycat * (sig * (1.0 + z * (1.0 - sig)))
        dycat = dycg * sz
        dym_ref[0] = dycat[:, 0:512]
        dy_rw = dycat[:, 512:1024]
        dlnb_ref[...] += rowsum(dy_rw)
        dlng_ref[...] += rowsum(dy_rw * yhat)
        dyhat = dy_rw * ln_g
        dyw_ref[0] = rstd * (dyhat - seg(dyhat) * (1.0 / 64) - yhat * (seg(dyhat * yhat) * (1.0 / 64)))
        dsb = seg(dy_rw * vv)
        drk_ref[...] += rowsum(dsb * rv * kpv)
        dbon_ref[0, :, 0:512] = dsb * kpv * r_k
        dbon_ref[0, :, 512:1024] = dsb * rv * r_k
        dbon_ref[0, :, 1024:1536] = dy_rw * sb

    tok = lambda c: pl.BlockSpec((1, tt, c), lambda b, i: (b, i, 0))
    full = lambda a: _full(a.shape)
    ins = (x, tgt, pp, o, yw, r, kp, v, ln_g, ln_b, r_k, wo, wot, gpost, bo)
    in_specs = [tok(D), tok(D), tok(1024)] + [tok(512)] * 5 + [full(a) for a in ins[8:]]
    sd = lambda c: jax.ShapeDtypeStruct((bsz, t, c), F32)
    vec = lambda c: jax.ShapeDtypeStruct((1, c), F32)
    out_shape = [sd(D), sd(1024), sd(512), sd(512), sd(1536), jax.ShapeDtypeStruct((8, LANES), F32),
                 jax.ShapeDtypeStruct((1024, 1024), F32), vec(D), vec(512), vec(512), vec(512)]
    out_specs = [tok(D), tok(1024), tok(512), tok(512), tok(1536), _resident((8, LANES)), _resident((1024, 1024)),
                 _resident((1, D)), _resident((1, 512)), _resident((1, 512)), _resident((1, 512))]
    return pl.pallas_call(
        body, name="post", grid=(bsz, nt), out_shape=out_shape, in_specs=in_specs, out_specs=out_specs,
        compiler_params=_cparams(("arbitrary", "arbitrary")),
    )(*ins)


def _pre_bwd_a(pp, pos, invf, cqkv_w, mu, w0, w2p, w2pt, a0, a2p, a2pt, k_k, k_a, bo,
               dq, dk, dva, dwkv, dbon):
    gq, wuqt, gkv, wukvt = cqkv_w
    bsz, t, _ = pp.shape
    tt = TT_VPU
    nt = t // tt
    dr_w, dw_w, dkp_w, dv_w, dal_w, dbe_w = dwkv

    def body(pp_ref, pos_ref, invf_ref, gq_ref, wuqt_ref, gkv_ref, wukvt_ref, mu_ref, w0_ref, w2p_ref, w2pt_ref,
             a0_ref, a2p_ref, a2pt_ref, kk_ref, ka_ref, bo_ref, dq_ref, dk_ref, dva_ref,
             dr_ref, dw_ref, dkp_ref, dv_ref, dal_ref, dbe_ref, dbon_ref,
             da_ref, dwuq_ref, dwukv_ref, dw2p_ref, da2p_ref, dgq_ref, dgkv_ref, dmu_ref, dw0_ref, da0_ref,
             dkk_ref, dka_ref, carry):
        i = pl.program_id(1)
        first = (pl.program_id(0) == 0) & (i == 0)

        @pl.when(first)
        def _():
            for ref in (dwuq_ref, dwukv_ref, dw2p_ref, da2p_ref, dgq_ref, dgkv_ref, dmu_ref, dw0_ref, da0_ref,
                        dkk_ref, dka_ref):
                ref[...] = jnp.zeros(ref.shape, F32)

        bo_m = bo_ref[...]
        rowsum = lambda a: jnp.sum(a, axis=0, keepdims=True)
        prw = pp_ref[0, :, RW0:DP]

        @pl.when(i == 0)
        def _():
            carry[...] = jnp.zeros(carry.shape, F32)

        ps, sh = _shift_mix(prw, carry[7:8, :], mu_ref[...])
        carry[...] = prw[tt - 8:tt, :]
        k_k, k_a = kk_ref[...], ka_ref[...]
        g = _rw_gates(ps, w0_ref[...], w2p_ref[...], a0_ref[...], a2p_ref[...], k_k, k_a, bo_m)
        a, kk, k = g["a"], g["kk"], g["k"]
        dr = dr_ref[0] + dbon_ref[0, :, 0:512]
        dkp = dkp_ref[0] + dbon_ref[0, :, 512:1024]
        dv = dv_ref[0] + dbon_ref[0, :, 1024:1536]
        dbe = dbe_ref[0]
        dkk = dbe * a - dal_ref[0]
        da = dbe * kk + dkp * k * k_a
        dka_ref[...] += rowsum(dkp * k * (a - 1.0))
        dm = (dkk - kk * _seg(dkk * kk, bo_m)) / g["nrm"]
        dkk_ref[...] += rowsum(dm * k)
        dk_tot = dkp * (1.0 + (a - 1.0) * k_a) + dm * k_k
        dapre = da * a * (1.0 - a)
        da0_ref[...] += rowsum(dapre)
        dapb = dapre.astype(BF16)
        da2p_ref[...] += _dot_tn(g["misc"].astype(BF16), dapb)
        dwpre = dw_ref[0] * g["w"] * (-g["e"]) * _sigmoid(-g["wpre"])
        dw0_ref[...] += rowsum(dwpre)
        dwpb = dwpre.astype(BF16)
        th = g["th"]
        dw2p_ref[...] += _dot_tn(th.astype(BF16), dwpb)
        dmisc = _dot(dapb, a2pt_ref[...]) + _dot(dwpb, w2pt_ref[...]) * (1.0 - th * th)
        ang = pos_ref[0] * invf_ref[...]
        cs, sn = jnp.cos(ang), jnp.sin(ang)
        unrope = lambda gr: gr * cs - _rot(gr * sn)
        lane = lax.broadcasted_iota(jnp.int32, cs.shape, 1)
        dkr = dk_ref[0, :, 128:256]
        for h in range(1, HEADS):
            dkr = dkr + dk_ref[0, :, 256 * h + 128:256 * h + 256]
        dkr = jnp.where(lane < 64, unrope(dkr), 0.0)
        dmisc = dmisc + jnp.concatenate([dkr, jnp.zeros_like(dkr)], axis=1)
        dqp = jnp.concatenate(
            [blk for h in range(HEADS)
             for blk in (dq_ref[0, :, 256 * h:256 * h + 128], unrope(dq_ref[0, :, 256 * h + 128:256 * h + 256]))],
            axis=1).astype(BF16)
        dkvp = jnp.concatenate([dk_ref[0, :, 256 * h:256 * h + 128] for h in range(HEADS)] + [dva_ref[0]],
                               axis=1).astype(BF16)
        cqn, cq_nx, cq_rstd = _rms(pp_ref[0, :, CQ0:CQ0 + 256], gq_ref[...], 256)
        ckvn, ckv_nx, ckv_rstd = _rms(pp_ref[0, :, CKV0:CKV0 + 128], gkv_ref[...], 128)
        dwuq_ref[...] += _dot_tn(cqn.astype(BF16), dqp)
        dwukv_ref[...] += _dot_tn(ckvn.astype(BF16), dkvp)
        dcq, dgq = _rms_bwd(_dot(dqp, wuqt_ref[...]), cq_nx, cq_rstd, gq_ref[...], 256)
        dckv, dgkv = _rms_bwd(_dot(dkvp, wukvt_ref[...]), ckv_nx, ckv_rstd, gkv_ref[...], 128)
        dgq_ref[...] += dgq
        dgkv_ref[...] += dgkv
        dps = jnp.concatenate([dr, dk_tot, dv, dmisc], axis=1)
        dmu_ref[...] += rowsum(dps * (sh - prw))
        da_ref[0, :, 0:256] = dcq
        da_ref[0, :, 256:384] = dckv
        da_ref[0, :, 384:384 + NRW] = dps

    tok = lambda c: pl.BlockSpec((1, tt, c), lambda b, i: (b, i, 0))
    full = lambda a: _full(a.shape)
    ins = (pp, pos, invf, gq, wuqt, gkv, wukvt, mu, w0, w2p, w2pt, a0, a2p, a2pt, k_k, k_a, bo,
           dq, dk, dva, dr_w, dw_w, dkp_w, dv_w, dal_w, dbe_w, dbon)
    in_specs = ([tok(DP), tok(1)] + [full(a) for a in ins[2:17]] + [tok(1024), tok(1024), tok(512)]
                + [tok(512)] * 6 + [tok(1536)])
    shp = lambda *s: jax.ShapeDtypeStruct(s, F32)
    out_shape = [shp(bsz, t, 384 + NRW), shp(256, 1024), shp(128, 1024), shp(256, 512), shp(256, 512),
                 shp(1, 256), shp(1, 128), shp(1, NRW), shp(1, 512), shp(1, 512), shp(1, 512), shp(1, 512)]
    out_specs = [tok(384 + NRW)] + [_resident(s.shape) for s in out_shape[1:]]
    return pl.pallas_call(
        body, name="pre_bwd_a", grid=(bsz, nt), out_shape=out_shape, in_specs=in_specs, out_specs=out_specs,
        scratch_shapes=[pltpu.VMEM((8, NRW), F32)],
        compiler_params=_cparams(("arbitrary", "arbitrary")),
    )(*ins)


def _pre_bwd_b(x, dh, dz, da, mu, wpt, gpre):
    bsz, t, _ = x.shape
    nt = t // TT
    nblk = t // 8

    def body(x_ref, dh_ref, dz_ref, da_ref, nxt_ref, mu_ref, wpt_ref, gpre_ref, gx_ref, dp_ref, dgpre_ref):
        i = pl.program_id(1)
        first = (pl.program_id(0) == 0) & (i == 0)

        @pl.when(first)
        def _():
            dgpre_ref[...] = jnp.zeros(dgpre_ref.shape, F32)

        mu_v = mu_ref[...]
        dps = da_ref[0, :, 384:384 + NRW]
        nxt = jnp.where(i < nt - 1, nxt_ref[0, 0:1, 384:384 + NRW], 0.0)
        row = lax.broadcasted_iota(jnp.int32, dps.shape, 0)
        up = jnp.where(row == TT - 1, nxt, pltpu.roll(dps, TT - 1, 0))
        dprw = dps * (1.0 - mu_v) + up * mu_v
        dp = jnp.concatenate([dz_ref[0], da_ref[0, :, 0:384], dprw], axis=1).astype(BF16)
        dp_ref[0] = dp
        du = _dot(dp, wpt_ref[...])
        _, nx, rstd = _rms(x_ref[0], gpre_ref[...], D)
        dx, dg = _rms_bwd(du, nx, rstd, gpre_ref[...], D)
        dgpre_ref[...] += dg
        gx_ref[0] = dh_ref[0] + dx

    tok = lambda c: pl.BlockSpec((1, TT, c), lambda b, i: (b, i, 0))
    nxt_spec = pl.BlockSpec((1, 8, 384 + NRW), lambda b, i: (b, jnp.minimum((i + 1) * (TT // 8), nblk - 1), 0))
    ins = (x, dh, dz, da, da, mu, wpt, gpre)
    return pl.pallas_call(
        body, name="pre_bwd_b", grid=(bsz, nt),
        out_shape=[jax.ShapeDtypeStruct((bsz, t, D), F32), jax.ShapeDtypeStruct((bsz, t, DP), BF16),
                   jax.ShapeDtypeStruct((1, D), F32)],
        in_specs=[tok(D), tok(D), tok(1024), tok(384 + NRW), nxt_spec, _full(mu.shape), _full(wpt.shape),
                  _full(gpre.shape)],
        out_specs=[tok(D), tok(DP), _resident((1, D))],
        compiler_params=_cparams(("arbitrary", "arbitrary")),
    )(*ins)


def _tn_matmul(a, b, bn, name, bk=512):
    kdim, m = a.shape
    _, n = b.shape
    nk = kdim // bk

    def body(a_ref, b_ref, o_ref):
        @pl.when(pl.program_id(1) == 0)
        def _():
            o_ref[...] = jnp.zeros(o_ref.shape, F32)

        o_ref[...] += _dot_tn(a_ref[...], b_ref[...])

    return pl.pallas_call(
        body, name=name, grid=(n // bn, nk),
        out_shape=jax.ShapeDtypeStruct((m, n), F32),
        in_specs=[pl.BlockSpec((bk, m), lambda j, kk: (kk, 0)), pl.BlockSpec((bk, bn), lambda j, kk: (kk, j))],
        out_specs=pl.BlockSpec((m, bn), lambda j, kk: (0, j)),
        compiler_params=_cparams(("parallel", "arbitrary")),
    )(a, b)


SHARDED = ("w_in", "mla_w_uq", "mla_w_ukv", "rw_w2", "rw_a2", "w_out")
SMALL = ("norm_pre_g", "mla_q_norm_g", "mla_kv_norm_g", "rw_mu", "rw_w0", "rw_a0", "rw_k_k", "rw_k_a", "rw_r_k",
         "rw_ln_g", "rw_ln_b", "norm_post_g")
WEIGHTS = ("norm_pre_g", "w_in", "mla_q_norm_g", "mla_w_uq", "mla_kv_norm_g", "mla_w_ukv", "rw_mu", "rw_w0", "rw_w2",
           "rw_a0", "rw_a2", "rw_k_k", "rw_k_a", "rw_r_k", "rw_ln_g", "rw_ln_b", "w_out", "norm_post_g")


def _unpack_shard(packed, like):
    out, at = {}, 0
    for n, rows in zip(SHARDED[1:5], PACK_ROWS):
        out[n] = packed[at:at + rows].reshape(like[n].shape)
        at += rows
    return out


def _constants():
    bo = np.kron(np.eye(2, dtype=np.float32), np.ones((64, 64), np.float32))
    inv = ROPE_THETA ** (-np.arange(0, 64, 2, dtype=np.float32) / 64)
    invf = np.concatenate([inv, inv, np.zeros(64, np.float32)]).astype(np.float32)[None, :]
    return jnp.asarray(bo, BF16), jnp.asarray(invf)


def kernel(x, positions, norm_pre_g, w_in, mla_q_norm_g, mla_w_uq, mla_kv_norm_g, mla_w_ukv, rw_mu, rw_w0, rw_w2, rw_a0, rw_a2, rw_k_k, rw_k_a, rw_r_k, rw_ln_g, rw_ln_b, w_out, norm_post_g, loss_target, m_norm_pre_g, m_w_in, m_mla_q_norm_g, m_mla_w_uq, m_mla_kv_norm_g, m_mla_w_ukv, m_rw_mu, m_rw_w0, m_rw_w2, m_rw_a0, m_rw_a2, m_rw_k_k, m_rw_k_a, m_rw_r_k, m_rw_ln_g, m_rw_ln_b, m_w_out, m_norm_post_g, v_norm_pre_g, v_w_in, v_mla_q_norm_g, v_mla_w_uq, v_mla_kv_norm_g, v_mla_w_ukv, v_rw_mu, v_rw_w0, v_rw_w2, v_rw_a0, v_rw_a2, v_rw_k_k, v_rw_k_a, v_rw_r_k, v_rw_ln_g, v_rw_ln_b, v_w_out, v_norm_post_g):
    wts = dict(norm_pre_g=norm_pre_g, w_in=w_in, mla_q_norm_g=mla_q_norm_g, mla_w_uq=mla_w_uq,
               mla_kv_norm_g=mla_kv_norm_g, mla_w_ukv=mla_w_ukv, rw_mu=rw_mu, rw_w0=rw_w0, rw_w2=rw_w2, rw_a0=rw_a0,
               rw_a2=rw_a2, rw_k_k=rw_k_k, rw_k_a=rw_k_a, rw_r_k=rw_r_k, rw_ln_g=rw_ln_g, rw_ln_b=rw_ln_b, w_out=w_out,
               norm_post_g=norm_post_g)
    mom_m = dict(norm_pre_g=m_norm_pre_g, w_in=m_w_in, mla_q_norm_g=m_mla_q_norm_g, mla_w_uq=m_mla_w_uq,
                 mla_kv_norm_g=m_mla_kv_norm_g, mla_w_ukv=m_mla_w_ukv, rw_mu=m_rw_mu, rw_w0=m_rw_w0, rw_w2=m_rw_w2,
                 rw_a0=m_rw_a0, rw_a2=m_rw_a2, rw_k_k=m_rw_k_k, rw_k_a=m_rw_k_a, rw_r_k=m_rw_r_k, rw_ln_g=m_rw_ln_g,
                 rw_ln_b=m_rw_ln_b, w_out=m_w_out, norm_post_g=m_norm_post_g)
    mom_v = dict(norm_pre_g=v_norm_pre_g, w_in=v_w_in, mla_q_norm_g=v_mla_q_norm_g, mla_w_uq=v_mla_w_uq,
                 mla_kv_norm_g=v_mla_kv_norm_g, mla_w_ukv=v_mla_w_ukv, rw_mu=v_rw_mu, rw_w0=v_rw_w0, rw_w2=v_rw_w2,
                 rw_a0=v_rw_a0, rw_a2=v_rw_a2, rw_k_k=v_rw_k_k, rw_k_a=v_rw_k_a, rw_r_k=v_rw_r_k, rw_ln_g=v_rw_ln_g,
                 rw_ln_b=v_rw_ln_b, w_out=v_w_out, norm_post_g=v_norm_post_g)
    bsz, t, _ = x.shape
    bo, invf = _constants()

    g_in, g_uq, g_ukv, g_w2, g_a2, g_out = _ag_weights([wts[n][0] for n in SHARDED])
    w_in_f = jnp.transpose(g_in, (1, 0, 2)).reshape(D, D_IN)
    wp = jnp.concatenate([w_in_f[:, 2112:3136], w_in_f[:, 0:384], w_in_f[:, 448:1984], w_in_f[:, 384:448],
                          w_in_f[:, 1984:2112], jnp.zeros((D, 64), BF16)], axis=1)
    wuq = jnp.pad(jnp.transpose(g_uq, (1, 0, 2)).reshape(256, HEADS, 192), ((0, 0), (0, 0), (0, 64))).reshape(256, 1024)
    wukv = jnp.transpose(jnp.transpose(g_ukv, (1, 0, 2)).reshape(128, HEADS, 2, 128), (0, 2, 1, 3)).reshape(128, 1024)
    w2 = jnp.transpose(g_w2, (1, 0, 2)).reshape(64, RW)
    a2 = jnp.transpose(g_a2, (1, 0, 2)).reshape(64, RW)
    w2p = jnp.pad(w2, ((64, 128), (0, 0)))
    a2p = jnp.pad(a2, ((128, 64), (0, 0)))
    wo = g_out.reshape(D, D)
    mu = jnp.concatenate([rw_mu[:, 0:1536], jnp.zeros((1, 64), F32), rw_mu[:, 1536:1664], jnp.zeros((1, 64), F32)],
                         axis=1)
    r_k = rw_r_k.reshape(1, RW)
    pos = positions.astype(F32)[:, :, None]

    (u, pp, q_att, k_att, v_att, r, w, kp, v, al, be) = _pre_fwd(
        x, pos, invf, norm_pre_g, wp, mla_q_norm_g, wuq, mla_kv_norm_g, wukv, mu, rw_w0, w2p, rw_a0, a2p, rw_k_k,
        rw_k_a, bo)
    o, lse = _attn_fwd(q_att, k_att, v_att)
    rw_k = _spread_k([r, w, kp, al, be])
    v_v = _to_v(v)
    yw_v, states, u_v = _wkv_fwd(*rw_k, v_v)
    yw = _from_v(yw_v, bsz)

    (dh, dz, dym, dyw, dbon, loss_acc, d_wo, d_gpost, d_lng, d_lnb, d_rk) = _post(
        x, loss_target, pp, o, yw, r, kp, v, rw_ln_g, rw_ln_b, r_k, wo, wo.T, norm_post_g, bo)

    d_k = _wkv_bwd(*rw_k, v_v, _to_v(dyw), states, u_v)
    dr_w, dw_w, dkp_w, dal_w, dbe_w = _gather_k(d_k[:5], bsz)
    dwkv = (dr_w, dw_w, dkp_w, _from_v(d_k[5], bsz), dal_w, dbe_w)
    dq, dk, dva = _attn_bwd(q_att, k_att, v_att, o, lse, dym)

    (da, d_wuq, d_wukv, d_w2p, d_a2p, d_gq, d_gkv, d_mu, d_w0, d_a0, d_kk, d_ka) = _pre_bwd_a(
        pp, pos, invf, (mla_q_norm_g, wuq.T, mla_kv_norm_g, wukv.T), mu, rw_w0, w2p, w2p.T, rw_a0, a2p, a2p.T,
        rw_k_k, rw_k_a, bo, dq, dk, dva, dwkv, dbon)
    grad_x, dpb, d_gpre = _pre_bwd_b(x, dh, dz, da, mu, wp.T, norm_pre_g)
    d_wp = _tn_matmul(u.reshape(bsz * t, D), dpb.reshape(bsz * t, DP), DP, "dw_in", bk=1024)

    full_g = {
        "w_in": jnp.concatenate([d_wp[:, 1024:1408], d_wp[:, 2944:3008], d_wp[:, 1408:2944], d_wp[:, 3008:3136],
                                 d_wp[:, 0:1024]], axis=1),
        "mla_w_uq": d_wuq.reshape(256, HEADS, 256)[:, :, :192].reshape(256, 768),
        "mla_w_ukv": jnp.transpose(d_wukv.reshape(128, 2, HEADS, 128), (0, 2, 1, 3)).reshape(128, 1024),
        "rw_w2": d_w2p[64:128],
        "rw_a2": d_a2p[128:192],
        "w_out": d_wo,
    }
    small_g = {
        "norm_pre_g": d_gpre, "mla_q_norm_g": d_gq, "mla_kv_norm_g": d_gkv,
        "rw_mu": jnp.concatenate([d_mu[:, 0:1536], d_mu[:, 1600:1728]], axis=1),
        "rw_w0": d_w0, "rw_a0": d_a0, "rw_k_k": d_kk, "rw_k_a": d_ka, "rw_r_k": d_rk, "rw_ln_g": d_lng,
        "rw_ln_b": d_lnb, "norm_post_g": d_gpost,
    }

    def by_shard(g):
        rows, cols = g.shape
        return jnp.transpose(g.reshape(rows, N_SHARD, cols // N_SHARD), (1, 0, 2))

    g_in = by_shard(full_g["w_in"])
    g_out = full_g["w_out"].reshape(N_SHARD, D // N_SHARD, D)
    packed = jnp.concatenate([by_shard(full_g[n]).reshape(N_SHARD, -1, LANES) for n in SHARDED[1:5]], axis=1)
    halves = [a.reshape(N_SHARD, 2, a.shape[1] // 2, a.shape[2]) for a in (g_in, g_out, packed)]
    red_in, red_out, red_rest = _rs_chips(*_rs_pairs(halves))
    g_shard = red_rest.reshape(PACK_REST, LANES)

    flat = lambda a: a.reshape(1, -1)
    g_small = _small_allreduce([flat(small_g[n]) for n in SMALL], loss_acc)
    loss = g_small[SMALL_USED, 0]

    g_sharded = _unpack_shard(g_shard, {n: wts[n][0] for n in SHARDED})
    g_sharded["w_in"] = red_in.reshape(wts["w_in"][0].shape)
    g_sharded["w_out"] = red_out.reshape(wts["w_out"][0].shape)
    sh = _adamw([wts[n][0] for n in SHARDED], [g_sharded[n] for n in SHARDED], [mom_m[n][0] for n in SHARDED],
                [mom_v[n][0] for n in SHARDED], "adamw_sharded")
    sm = _adamw_small([flat(wts[n]) for n in SMALL], g_small, [flat(mom_m[n]) for n in SMALL],
                      [flat(mom_v[n]) for n in SMALL])

    def outputs(sharded, small):
        out = {n: a[None] for n, a in zip(SHARDED, sharded)}
        out.update({n: a.reshape(wts[n].shape) for n, a in zip(SMALL, small)})
        return out

    grads = outputs([g_sharded[n] for n in SHARDED], sm[0])
    deltas, new_m, new_v = (outputs(sh[k], sm[k + 1]) for k in range(3))
    return (loss, grad_x, *[grads[n] for n in WEIGHTS], *[deltas[n] for n in WEIGHTS],
            *[new_m[n] for n in WEIGHTS], *[new_v[n] for n in WEIGHTS])
```

```python
import numpy as np
import jax
import jax.numpy as jnp
from jax import lax
from jax.experimental import pallas as pl
from jax.experimental.pallas import tpu as pltpu

F32, BF16 = jnp.float32, jnp.bfloat16
MESH = pl.DeviceIdType.MESH

D = 1024
HEADS = 4
RW = 512
NORM_EPS = 1e-6
GN_EPS = 64e-5
ROPE_THETA = 10000.0
SCALE = (128 + 64) ** -0.5
D_IN = 3136
LR, B1, B2, ADAM_EPS, WD, STEP = 0.001, 0.9, 0.999, 1e-08, 0.01, 10

Z0, CQ0, CKV0, RW0, DP = 0, 1024, 1280, 1408, 3200
NRW = DP - RW0

LANES = 128
SUBLANES = 8
VMEM_LIMIT = 56 * 1024 * 1024

TT = 512
TT_VPU = 256
TQ = 512

N_SHARD = 4
PACK_ROWS = (256 * 192 // 128, 128 * 256 // 128, 64, 64)
PACK_REST = sum(PACK_ROWS)
SMALL_ROWS = 64
SMALL_USED = 60


def _cparams(sem=None):
    return pltpu.CompilerParams(dimension_semantics=sem, vmem_limit_bytes=VMEM_LIMIT)


def _full(shape):
    n = len(shape)
    return pl.BlockSpec(shape, lambda *_: (0,) * n, pipeline_mode=pl.Buffered(1))


def _resident(shape):
    n = len(shape)
    return pl.BlockSpec(shape, lambda *_: (0,) * n)


def _dot(a, b):
    return jnp.dot(a, b, preferred_element_type=F32)


def _dot_nt(a, b):
    return lax.dot_general(a, b, (((1,), (1,)), ((), ())), preferred_element_type=F32)


def _dot_tn(a, b):
    return lax.dot_general(a, b, (((0,), (0,)), ((), ())), preferred_element_type=F32)


def _split3(x):
    hi = x.astype(BF16)
    r1 = x - hi.astype(F32)
    mid = r1.astype(BF16)
    lo = (r1 - mid.astype(F32)).astype(BF16)
    return hi, mid, lo


def _seg(x, bo):
    rows, nblk = x.shape[0], x.shape[1] // LANES
    pieces = [p for i in range(nblk) for p in _split3(x[:, LANES * i:LANES * (i + 1)])]
    res = _dot(jnp.concatenate(pieces, axis=0), bo)
    parts = [res[(3 * i) * rows:(3 * i + 1) * rows] + res[(3 * i + 1) * rows:(3 * i + 2) * rows]
             + res[(3 * i + 2) * rows:(3 * i + 3) * rows] for i in range(nblk)]
    return parts[0] if nblk == 1 else jnp.concatenate(parts, axis=1)


def _rms(x, g, n):
    rstd = lax.rsqrt(jnp.sum(x * x, axis=-1, keepdims=True) * (1.0 / n) + NORM_EPS)
    nx = x * rstd
    return nx * g, nx, rstd


def _rms_bwd(dy, nx, rstd, g, n):
    dn = dy * g
    dx = rstd * (dn - nx * (jnp.sum(dn * nx, axis=-1, keepdims=True) * (1.0 / n)))
    return dx, jnp.sum(dy * nx, axis=0, keepdims=True)


def _rot(x):
    lane = lax.broadcasted_iota(jnp.int32, x.shape, 1)
    return jnp.where((lane % 64) < 32, -pltpu.roll(x, x.shape[1] - 32, 1), pltpu.roll(x, 32, 1))


def _sigmoid(x):
    return 1.0 / (1.0 + jnp.exp(-x))


def _softplus(x):
    return jnp.maximum(x, 0.0) + jnp.log(1.0 + jnp.exp(-jnp.abs(x)))


def _rw_gates(ps, w0, w2p, a0, a2p, k_k, k_a, bo):
    r, k, v, misc = ps[:, 0:512], ps[:, 512:1024], ps[:, 1024:1536], ps[:, 1536:NRW]
    th = jnp.tanh(misc)
    wpre = w0 + _dot(th.astype(BF16), w2p)
    e = jnp.exp(-_softplus(-wpre) - 0.5)
    w = jnp.exp(-e)
    a = _sigmoid(a0 + _dot(misc.astype(BF16), a2p))
    m = k * k_k
    nrm = jnp.maximum(jnp.sqrt(_seg(m * m, bo)), 1e-12)
    kk = m / nrm
    kp = k * (1.0 + (a - 1.0) * k_a)
    return dict(r=r, k=k, v=v, misc=misc, th=th, wpre=wpre, e=e, w=w, a=a, nrm=nrm, kk=kk, kp=kp)


def _shift_mix(prw, prev_row, mu):
    row = lax.broadcasted_iota(jnp.int32, prw.shape, 0)
    sh = jnp.where(row == 0, prev_row, pltpu.roll(prw, 1, 0))
    return prw + (sh - prw) * mu, sh


def _ag_weights(shards):
    n = len(shards)

    def body(*refs):
        ins, outs = refs[:n], refs[n:2 * n]
        ici_send, ici_recv, d2d_send, d2d_recv = refs[2 * n:2 * n + 4]
        x, y, c = lax.axis_index("x"), lax.axis_index("y"), lax.axis_index("c")
        mine = 2 * x + y
        for w in range(n):
            outs[w][mine] = ins[w][...].astype(BF16)
        flips = ((1, 0), (0, 1), (1, 1))

        def half(w, shard, cc):
            rows = outs[w].shape[1] // 2
            return outs[w].at[shard, pl.ds(pl.multiple_of(cc * rows, 16), rows)]

        def ici(w, k, shard):
            fx, fy = flips[k]
            return pltpu.make_async_remote_copy(
                src_ref=half(w, shard, c), dst_ref=half(w, shard, c),
                send_sem=ici_send.at[w * 3 + k], recv_sem=ici_recv.at[w * 3 + k],
                device_id=(x ^ fx, y ^ fy, c), device_id_type=MESH)

        def d2d(w, k, cc):
            fx, fy = flips[k]
            theirs = 2 * (x ^ fx) + (y ^ fy)
            return pltpu.make_async_remote_copy(
                src_ref=half(w, theirs, cc), dst_ref=half(w, theirs, cc),
                send_sem=d2d_send.at[w * 3 + k], recv_sem=d2d_recv.at[w * 3 + k],
                device_id=(x, y, 1 - c), device_id_type=MESH)

        for w in range(n):
            for k in range(3):
                ici(w, k, mine).start()
        for w in range(n):
            for k in range(3):
                fx, fy = flips[k]
                ici(w, k, 2 * (x ^ fx) + (y ^ fy)).wait_recv()
                d2d(w, k, c).start()
        for w in range(n):
            for k in range(3):
                d2d(w, k, 1 - c).wait_recv()
        for w in range(n):
            for k in range(3):
                ici(w, k, mine).wait_send()
                d2d(w, k, c).wait_send()

    vm = pl.BlockSpec(memory_space=pltpu.VMEM)
    return pl.pallas_call(
        body, name="ag_weights",
        out_shape=[jax.ShapeDtypeStruct((N_SHARD,) + s.shape, BF16) for s in shards],
        in_specs=[vm] * n, out_specs=[vm] * n,
        scratch_shapes=[pltpu.SemaphoreType.DMA((3 * n,))] * 4,
        compiler_params=pltpu.CompilerParams(vmem_limit_bytes=VMEM_LIMIT),
    )(*shards)


def _rs_pairs(halves):
    n = len(halves)

    def body(*refs):
        h_refs, sum_refs, sumb_refs, recvs = (refs[k * n:(k + 1) * n] for k in range(4))
        send_sem, recv_sem = refs[4 * n:]
        x, y, c = lax.axis_index("x"), lax.axis_index("y"), lax.axis_index("c")
        cps = [pltpu.make_async_remote_copy(src_ref=h_refs[i].at[s, 1 - c], dst_ref=recvs[i].at[s],
                                            send_sem=send_sem.at[i * N_SHARD + s], recv_sem=recv_sem.at[i * N_SHARD + s],
                                            device_id=(x, y, 1 - c), device_id_type=MESH)
               for i in range(n) for s in range(N_SHARD)]
        for cp in cps:
            cp.start()
        for i in range(n):
            for s in range(N_SHARD):
                cps[i * N_SHARD + s].wait_recv()
                acc = h_refs[i][s, c] + recvs[i][s]
                sum_refs[i][s] = acc
                sumb_refs[i][s] = acc.astype(BF16)
        for cp in cps:
            cp.wait_send()

    vm = pl.BlockSpec(memory_space=pltpu.VMEM)
    shapes = [(N_SHARD,) + h.shape[2:] for h in halves]
    outs = pl.pallas_call(
        body, name="rs_pairs",
        out_shape=[jax.ShapeDtypeStruct(sh, F32) for sh in shapes] + [jax.ShapeDtypeStruct(sh, BF16) for sh in shapes],
        in_specs=[vm] * n, out_specs=[vm] * (2 * n),
        scratch_shapes=[pltpu.VMEM(sh, F32) for sh in shapes] + [pltpu.SemaphoreType.DMA((n * N_SHARD,)),
                                                                 pltpu.SemaphoreType.DMA((n * N_SHARD,))],
        compiler_params=pltpu.CompilerParams(vmem_limit_bytes=VMEM_LIMIT),
    )(*halves)
    return outs[:n], outs[n:]


def _rs_chips(part_f32, part_bf16):
    n = len(part_f32)

    def body(*refs):
        own_refs, src_refs, out_refs, recvs = (refs[k * n:(k + 1) * n] for k in range(4))
        ici_send, ici_recv, d2d_send, d2d_recv = refs[4 * n:]
        x, y, c = lax.axis_index("x"), lax.axis_index("y"), lax.axis_index("c")
        mine = 2 * x + y
        flips = ((1, 0), (0, 1), (1, 1))
        cps = []
        for i in range(n):
            for k, (fx, fy) in enumerate(flips):
                theirs = 2 * (x ^ fx) + (y ^ fy)
                cps.append(pltpu.make_async_remote_copy(
                    src_ref=src_refs[i].at[theirs], dst_ref=recvs[i].at[k],
                    send_sem=ici_send.at[3 * i + k], recv_sem=ici_recv.at[3 * i + k],
                    device_id=(x ^ fx, y ^ fy, c), device_id_type=MESH))
        for cp in cps:
            cp.start()
        handed = []
        for i in range(n):
            acc = own_refs[i][mine]
            for k in range(3):
                cps[3 * i + k].wait_recv()
                acc = acc + recvs[i][k].astype(F32)
            out_refs[i][c] = acc
            to_sibling = pltpu.make_async_remote_copy(
                src_ref=out_refs[i].at[c], dst_ref=out_refs[i].at[c], send_sem=d2d_send.at[i], recv_sem=d2d_recv.at[i],
                device_id=(x, y, 1 - c), device_id_type=MESH)
            to_sibling.start()
            handed.append(to_sibling)
        for i in range(n):
            pltpu.make_async_remote_copy(
                src_ref=out_refs[i].at[1 - c], dst_ref=out_refs[i].at[1 - c], send_sem=d2d_send.at[i],
                recv_sem=d2d_recv.at[i], device_id=(x, y, 1 - c), device_id_type=MESH).wait_recv()
        for cp in handed + cps:
            cp.wait_send()

    vm = pl.BlockSpec(memory_space=pltpu.VMEM)
    return pl.pallas_call(
        body, name="rs_chips",
        out_shape=[jax.ShapeDtypeStruct((2,) + p.shape[1:], F32) for p in part_f32],
        in_specs=[vm] * (2 * n), out_specs=[vm] * n,
        scratch_shapes=[pltpu.VMEM((3,) + p.shape[1:], BF16) for p in part_bf16]
        + [pltpu.SemaphoreType.DMA((3 * n,)), pltpu.SemaphoreType.DMA((3 * n,)), pltpu.SemaphoreType.DMA((n,)),
           pltpu.SemaphoreType.DMA((n,))],
        compiler_params=pltpu.CompilerParams(vmem_limit_bytes=VMEM_LIMIT),
    )(*part_f32, *part_bf16)


def _small_rows(vecs):
    out, at = [], 0
    for vec in vecs:
        rows = vec.shape[1] // LANES
        out.append((rows, at))
        at += rows
    assert at == SMALL_USED
    return out


def _small_allreduce(vecs, loss_acc):
    n = len(vecs)
    layout = _small_rows(vecs)

    def body(*refs):
        loss_ref, out_ref, stage, recv, send_sems, recv_sems = refs[n:]
        for vec_ref, (rows, at) in zip(refs[:n], layout):
            for j in range(rows):
                stage[at + j:at + j + 1, :] = vec_ref[0:1, LANES * j:LANES * (j + 1)]
        stage[SMALL_USED:SMALL_ROWS, :] = loss_ref[0:SMALL_ROWS - SMALL_USED, :]
        x, y, c = lax.axis_index("x"), lax.axis_index("y"), lax.axis_index("c")
        me = 4 * x + 2 * y + c
        cps = []
        for k in range(1, 8):
            fx, fy, fc = (k >> 2) & 1, (k >> 1) & 1, k & 1
            cps.append(pltpu.make_async_remote_copy(
                src_ref=stage, dst_ref=recv.at[k - 1],
                send_sem=send_sems.at[k - 1], recv_sem=recv_sems.at[k - 1],
                device_id=(x ^ fx, y ^ fy, c ^ fc), device_id_type=MESH))
        for cp in cps:
            cp.start()
        for cp in cps:
            cp.wait()
        acc = jnp.zeros(stage.shape, F32)
        for j in range(8):
            slot = jnp.maximum((me ^ j) - 1, 0)
            acc = acc + jnp.where(me == j, stage[...], recv[slot])
        out_ref[...] = acc

    vm = pl.BlockSpec(memory_space=pltpu.VMEM)
    shape = (SMALL_ROWS, LANES)
    return pl.pallas_call(
        body, name="small_allreduce",
        out_shape=jax.ShapeDtypeStruct(shape, F32),
        in_specs=[vm] * (n + 1), out_specs=vm,
        scratch_shapes=[pltpu.VMEM(shape, F32), pltpu.VMEM((7,) + shape, F32), pltpu.SemaphoreType.DMA((7,)),
                        pltpu.SemaphoreType.DMA((7,))],
    )(*vecs, loss_acc)


def _adamw_small(ws, g_packed, ms, vs):
    n = len(ws)
    layout = _small_rows(ws)

    def body(*refs):
        g_ref = refs[3 * n]
        outs = refs[3 * n + 1:]
        for i, (rows, at) in enumerate(layout):
            w_ref, m_ref, v_ref = refs[i], refs[n + i], refs[2 * n + i]
            go_ref, d_ref, nm_ref, nv_ref = (outs[k * n + i] for k in range(4))
            for j in range(rows):
                lanes = slice(LANES * j, LANES * (j + 1))
                gg = g_ref[at + j:at + j + 1, :]
                nm = B1 * m_ref[0:1, lanes] + (1.0 - B1) * gg
                nv = B2 * v_ref[0:1, lanes] + (1.0 - B2) * (gg * gg)
                m_hat = nm / (1.0 - B1 ** STEP)
                v_hat = nv / (1.0 - B2 ** STEP)
                go_ref[0:1, lanes] = gg
                d_ref[0:1, lanes] = -LR * (m_hat / (jnp.sqrt(v_hat) + ADAM_EPS) + WD * w_ref[0:1, lanes])
                nm_ref[0:1, lanes] = nm
                nv_ref[0:1, lanes] = nv

    vm = pl.BlockSpec(memory_space=pltpu.VMEM)
    sds = [jax.ShapeDtypeStruct(w.shape, F32) for w in ws]
    outs = pl.pallas_call(
        body, name="adamw_small", out_shape=sds * 4, in_specs=[vm] * (3 * n + 1), out_specs=[vm] * (4 * n),
    )(*ws, *ms, *vs, g_packed)
    return outs[:n], outs[n:2 * n], outs[2 * n:3 * n], outs[3 * n:]


ADAM_ROWS = 64


def _adamw(ws, gs, ms, vs, name):
    n = len(ws)

    def body(*refs):
        for i in range(n):
            w_ref, g_ref, m_ref, v_ref = (refs[k * n + i] for k in range(4))
            d_ref, nm_ref, nv_ref = (refs[(4 + k) * n + i] for k in range(3))
            rows = min(ADAM_ROWS, w_ref.shape[0])

            def chunk(r, _):
                at = pl.ds(pl.multiple_of(r * rows, SUBLANES), rows)
                gg = g_ref[at, :]
                nm = B1 * m_ref[at, :] + (1.0 - B1) * gg
                nv = B2 * v_ref[at, :] + (1.0 - B2) * (gg * gg)
                m_hat = nm / (1.0 - B1 ** STEP)
                v_hat = nv / (1.0 - B2 ** STEP)
                d_ref[at, :] = -LR * (m_hat / (jnp.sqrt(v_hat) + ADAM_EPS) + WD * w_ref[at, :])
                nm_ref[at, :] = nm
                nv_ref[at, :] = nv
                return 0

            lax.fori_loop(0, w_ref.shape[0] // rows, chunk, 0)

    vm = pl.BlockSpec(memory_space=pltpu.VMEM)
    sds = [jax.ShapeDtypeStruct(w.shape, F32) for w in ws]
    outs = pl.pallas_call(
        body, name=name, out_shape=sds * 3, in_specs=[vm] * (4 * n), out_specs=[vm] * (3 * n),
        compiler_params=pltpu.CompilerParams(vmem_limit_bytes=VMEM_LIMIT),
    )(*ws, *gs, *ms, *vs)
    return outs[:n], outs[n:2 * n], outs[2 * n:]


def _pre_fwd(x, pos, invf, gpre, wp, gq, wuq, gkv, wukv, mu, w0, w2p, a0, a2p, k_k, k_a, bo):
    bsz, t, _ = x.shape
    nt = t // TT

    def body(x_ref, pos_ref, invf_ref, gpre_ref, wp_ref, gq_ref, wuq_ref, gkv_ref, wukv_ref, mu_ref, w0_ref,
             w2p_ref, a0_ref, a2p_ref, kk_ref, ka_ref, bo_ref,
             u_ref, pp_ref, q_ref, k_ref, v_ref, r_o, w_o, kp_o, vv_o, al_o, be_o, carry):
        i = pl.program_id(1)
        u, _, _ = _rms(x_ref[0], gpre_ref[...], D)
        ub = u.astype(BF16)
        u_ref[0] = ub
        p = _dot(ub, wp_ref[...])
        pp_ref[0] = p
        prw = p[:, RW0:DP]

        @pl.when(i == 0)
        def _():
            carry[...] = jnp.zeros(carry.shape, F32)

        ps, _ = _shift_mix(prw, carry[7:8, :], mu_ref[...])
        carry[...] = prw[TT - 8:TT, :]

        g = _rw_gates(ps, w0_ref[...], w2p_ref[...], a0_ref[...], a2p_ref[...], kk_ref[...], ka_ref[...],
                      bo_ref[...])
        r_o[0] = g["r"]
        w_o[0] = g["w"]
        kp_o[0] = g["kp"]
        vv_o[0] = g["v"]
        al_o[0] = -g["kk"]
        be_o[0] = g["kk"] * g["a"]

        cqn, _, _ = _rms(p[:, CQ0:CQ0 + 256], gq_ref[...], 256)
        q = _dot(cqn.astype(BF16), wuq_ref[...])
        ckvn, _, _ = _rms(p[:, CKV0:CKV0 + 128], gkv_ref[...], 128)
        kv = _dot(ckvn.astype(BF16), wukv_ref[...])
        ang = pos_ref[0] * invf_ref[...]
        cs, sn = jnp.cos(ang), jnp.sin(ang)
        lane = lax.broadcasted_iota(jnp.int32, cs.shape, 1)
        kr = ps[:, 1536:1536 + LANES]
        kr = jnp.where(lane < 64, kr * cs + _rot(kr) * sn, 0.0).astype(BF16)
        for h in range(HEADS):
            qr = q[:, 256 * h + 128:256 * h + 256]
            q_ref[0, :, 256 * h:256 * h + 128] = q[:, 256 * h:256 * h + 128].astype(BF16)
            q_ref[0, :, 256 * h + 128:256 * h + 256] = (qr * cs + _rot(qr) * sn).astype(BF16)
            k_ref[0, :, 256 * h:256 * h + 128] = kv[:, 128 * h:128 * h + 128].astype(BF16)
            k_ref[0, :, 256 * h + 128:256 * h + 256] = kr
        v_ref[0] = kv[:, 512:1024].astype(BF16)

    tok = lambda c: pl.BlockSpec((1, TT, c), lambda b, i: (b, i, 0))
    full = lambda a: _full(a.shape)
    ins = (x, pos, invf, gpre, wp, gq, wuq, gkv, wukv, mu, w0, w2p, a0, a2p, k_k, k_a, bo)
    in_specs = [tok(D), tok(1)] + [full(a) for a in ins[2:]]
    sd = lambda c, dt: jax.ShapeDtypeStruct((bsz, t, c), dt)
    out_shape = [sd(D, BF16), sd(DP, F32), sd(1024, BF16), sd(1024, BF16), sd(512, BF16)] + [sd(RW, F32)] * 6
    out_specs = [tok(D), tok(DP), tok(1024), tok(1024), tok(512)] + [tok(RW)] * 6
    return pl.pallas_call(
        body, name="pre_fwd", grid=(bsz, nt), out_shape=out_shape, in_specs=in_specs, out_specs=out_specs,
        scratch_shapes=[pltpu.VMEM((8, NRW), F32)],
        compiler_params=_cparams(("arbitrary", "arbitrary")),
    )(*ins)


def _attn_fwd(q, k, v):
    bsz, t, _ = q.shape
    nq = t // TQ

    hps = HEADS

    def body(q_ref, k_ref, v_ref, o_ref, lse_ref):
        i = pl.program_id(2)

        def step(j, carry, diagonal):
            at = pl.ds(pl.multiple_of(j * TQ, TQ), TQ)
            out = []
            for hh in range(hps):
                m, l, acc = carry[hh]
                s = _dot_nt(q_ref[0, :, 256 * hh:256 * (hh + 1)], k_ref[0, at, 256 * hh:256 * (hh + 1)]) * SCALE
                if diagonal:
                    s = jnp.where(lax.broadcasted_iota(jnp.int32, (TQ, TQ), 1)
                                  <= lax.broadcasted_iota(jnp.int32, (TQ, TQ), 0), s, -1e30)
                mn = jnp.maximum(m, jnp.max(s, axis=1, keepdims=True))
                p = jnp.exp(s - mn)
                al = jnp.exp(m - mn)
                l = al * l + jnp.sum(p, axis=1, keepdims=True)
                acc = al * acc + _dot(p.astype(BF16), v_ref[0, at, LANES * hh:LANES * (hh + 1)])
                out.append((mn, l, acc))
            return tuple(out)

        start = (jnp.full((TQ, 1), -1e30, F32), jnp.zeros((TQ, 1), F32), jnp.zeros((TQ, LANES), F32))
        before = lax.fori_loop(0, i, lambda j, carry: step(j, carry, False), (start,) * hps)
        for hh, (m, l, acc) in enumerate(step(i, before, True)):
            o_ref[0, :, LANES * hh:LANES * (hh + 1)] = acc / l
            lse_ref[0, hh] = jnp.broadcast_to(m + jnp.log(l), (TQ, LANES))

    return pl.pallas_call(
        body, name="attn_fwd", grid=(bsz, HEADS // hps, nq),
        out_shape=[jax.ShapeDtypeStruct((bsz, t, 512), F32), jax.ShapeDtypeStruct((bsz, HEADS, t, LANES), F32)],
        in_specs=[pl.BlockSpec((1, TQ, 256 * hps), lambda b, h, i: (b, i, h)),
                  pl.BlockSpec((1, t, 256 * hps), lambda b, h, i: (b, 0, h)),
                  pl.BlockSpec((1, t, LANES * hps), lambda b, h, i: (b, 0, h))],
        out_specs=[pl.BlockSpec((1, TQ, LANES * hps), lambda b, h, i: (b, i, h)),
                   pl.BlockSpec((1, hps, TQ, LANES), lambda b, h, i: (b, h, i, 0))],
        compiler_params=_cparams(("parallel", "parallel", "arbitrary")),
    )(q, k, v)


def _attn_bwd(q, k, v, o, lse, do):
    bsz, t, _ = q.shape
    nq = t // TQ

    def body(q_ref, k_ref, v_ref, o_ref, lse_ref, do_ref, dq_ref, dk_ref, dv_ref, dl_ref):
        j = pl.program_id(2)

        @pl.when(j == 0)
        def _():
            def prep(i, _):
                at = pl.ds(pl.multiple_of(i * TQ, TQ), TQ)
                for hh in range(2):
                    lanes = slice(LANES * hh, LANES * (hh + 1))
                    dl_ref[hh, at, :] = jnp.broadcast_to(
                        jnp.sum(do_ref[0, at, lanes] * o_ref[0, at, lanes], axis=1, keepdims=True), (TQ, LANES))
                return 0

            lax.fori_loop(0, nq, prep, 0)
            dq_ref[0] = jnp.zeros((t, 512), F32)

        def q_tile(i, carry, diagonal):
            atq = pl.ds(pl.multiple_of(i * TQ, TQ), TQ)
            out = []
            for hh in range(2):
                dk, dv = carry[hh]
                wide, narrow = slice(256 * hh, 256 * (hh + 1)), slice(LANES * hh, LANES * (hh + 1))
                qt, kt, vt = q_ref[0, atq, wide], k_ref[0, :, wide], v_ref[0, :, narrow]
                dob = do_ref[0, atq, narrow].astype(BF16)
                s = _dot_nt(qt, kt) * SCALE
                if diagonal:
                    s = jnp.where(lax.broadcasted_iota(jnp.int32, (TQ, TQ), 1)
                                  <= lax.broadcasted_iota(jnp.int32, (TQ, TQ), 0), s, -1e30)
                p = jnp.exp(s - lse_ref[0, hh, atq, :][:, 0:1])
                dv = dv + _dot_tn(p.astype(BF16), dob)
                dp = _dot_nt(dob, vt)
                ds = (p * (dp - dl_ref[hh, atq, :][:, 0:1]) * SCALE).astype(BF16)
                dk = dk + _dot_tn(ds, qt)
                dq_ref[0, atq, wide] += _dot(ds, kt)
                out.append((dk, dv))
            return tuple(out)

        zero = (jnp.zeros((TQ, 256), F32), jnp.zeros((TQ, LANES), F32))
        first = q_tile(j, (zero, zero), True)
        done = lax.fori_loop(j + 1, nq, lambda i, carry: q_tile(i, carry, False), first)
        for hh, (dk, dv) in enumerate(done):
            dk_ref[0, :, 256 * hh:256 * (hh + 1)] = dk
            dv_ref[0, :, LANES * hh:LANES * (hh + 1)] = dv

    whole = lambda c: pl.BlockSpec((1, t, c), lambda b, h, j: (b, 0, h))
    tile = lambda c: pl.BlockSpec((1, TQ, c), lambda b, h, j: (b, j, h))
    return pl.pallas_call(
        body, name="attn_bwd", grid=(bsz, HEADS // 2, nq),
        out_shape=[jax.ShapeDtypeStruct((bsz, t, 1024), F32), jax.ShapeDtypeStruct((bsz, t, 1024), F32),
                   jax.ShapeDtypeStruct((bsz, t, 512), F32)],
        in_specs=[whole(512), tile(512), tile(256), whole(256),
                  pl.BlockSpec((1, 2, t, LANES), lambda b, h, j: (b, h, 0, 0)), whole(256)],
        out_specs=[whole(512), tile(512), tile(256)],
        scratch_shapes=[pltpu.VMEM((2, t, LANES), F32)],
        compiler_params=_cparams(("parallel", "parallel", "arbitrary")),
    )(q, k, v, o, lse, do)


RW_HEADS = 8
CH = 32


def _lane_split(bsz):
    vs = LANES // (bsz * RW_HEADS)
    return vs, 64 // vs


def _gather_matrix(bsz):
    group = bsz * RW_HEADS
    vs = LANES // group
    half = (RW_HEADS // 2) * bsz * SPREAD_STEPS
    p = np.zeros((SPREAD_STEPS // vs * LANES, 2 * half), np.float32)
    for g2 in range(SPREAD_STEPS // vs):
        for j in range(vs):
            for b in range(bsz):
                for h in range(RW_HEADS):
                    hp, hpar = h // 2, h % 2
                    p[g2 * LANES + j * group + b * RW_HEADS + h,
                      hpar * half + (hp * bsz + b) * SPREAD_STEPS + g2 * vs + j] = 1.0
    return jnp.asarray(np.concatenate([p] * 3, axis=0), BF16)


def _gather_k(ys, bsz):
    vs = LANES // (bsz * RW_HEADS)
    assert (RW_HEADS // 2) * bsz * SPREAD_STEPS == LANES, "the transposed tile must be 128 lanes wide"
    tg = ys[0].shape[0]
    n = len(ys)
    ngrp = GATHER_BLOCK // SPREAD_STEPS
    per = SPREAD_STEPS // vs

    def body(*refs):
        pm = refs[n][...]
        for y_ref, o_ref in zip(refs[:n], refs[n + 1:]):
            lhs = jnp.concatenate(
                [jnp.concatenate(_split3(jnp.concatenate([y_ref[per * m + g2] for g2 in range(per)], axis=1)), axis=1)
                 for m in range(ngrp)], axis=0)
            a = _dot(lhs, pm)
            for m in range(ngrp):
                am = a[64 * m:64 * (m + 1)]
                bt = jnp.concatenate([am[:, 0:LANES], am[:, LANES:2 * LANES]], axis=0).T
                for hp in range(RW_HEADS // 2):
                    for b in range(bsz):
                        at = (hp * bsz + b) * SPREAD_STEPS
                        o_ref[b, SPREAD_STEPS * m:SPREAD_STEPS * (m + 1), LANES * hp:LANES * (hp + 1)] = \
                            bt[at:at + SPREAD_STEPS]

    pm = _gather_matrix(bsz)
    return pl.pallas_call(
        body, name="wkv_gather", grid=(tg * vs // GATHER_BLOCK,),
        out_shape=[jax.ShapeDtypeStruct((bsz, tg * vs, RW), F32)] * n,
        in_specs=[pl.BlockSpec((GATHER_BLOCK // vs, 64, LANES), lambda i: (i, 0, 0))] * n + [_full(pm.shape)],
        out_specs=[pl.BlockSpec((bsz, GATHER_BLOCK, RW), lambda i: (0, i, 0))] * n,
        compiler_params=_cparams(("parallel",)),
    )(*ys, pm)


def _to_v(x):
    bsz, t, _ = x.shape
    vs, vq = _lane_split(bsz)
    return jnp.transpose(x.reshape(bsz, t, RW_HEADS, vs, vq), (1, 4, 3, 0, 2)).reshape(t, vq, LANES)


def _from_v(y, bsz):
    t = y.shape[0]
    vs, vq = _lane_split(bsz)
    return jnp.transpose(y.reshape(t, vq, vs, bsz, RW_HEADS), (3, 0, 4, 2, 1)).reshape(bsz, t, RW)


def _ksum(a):
    return jnp.sum(a, axis=0, keepdims=True)


def _fold(a, group):
    sh = LANES // 2
    while sh >= group:
        a = a + pltpu.roll(a, sh, 1)
        sh //= 2
    return a


def _lane_group(shape, group):
    return lax.broadcasted_iota(jnp.int32, shape, 1) // group


SPREAD_STEPS = 8
SPREAD_BLOCK = 64
GATHER_BLOCK = 128


def _spread_matrix(bsz):
    group = bsz * RW_HEADS
    vs = LANES // group
    rows = (RW_HEADS // 2) * bsz * SPREAD_STEPS
    q = np.zeros((2, rows, SPREAD_STEPS * LANES), np.float32)
    for hpar in range(2):
        for hp in range(RW_HEADS // 2):
            for b in range(bsz):
                for st in range(SPREAD_STEPS):
                    row = (hp * bsz + b) * SPREAD_STEPS + st
                    for s in range(vs):
                        q[hpar, row, st * LANES + s * group + b * RW_HEADS + 2 * hp + hpar] = 1.0
    return jnp.asarray(np.concatenate([q[0], q[1]] * 3, axis=0), BF16)


def _spread_k(xs):
    bsz, t, _ = xs[0].shape
    assert (RW_HEADS // 2) * bsz * SPREAD_STEPS == LANES, "the transposed tile must be 128 lanes wide"
    n = len(xs)
    ngrp = SPREAD_BLOCK // SPREAD_STEPS

    def body(*refs):
        qm = refs[n][...]
        for x_ref, o_ref in zip(refs[:n], refs[n + 1:]):
            cols = [[] for _ in range(6)]
            for m in range(ngrp):
                at = slice(SPREAD_STEPS * m, SPREAD_STEPS * (m + 1))
                x8 = jnp.concatenate([x_ref[b, at, LANES * hp:LANES * (hp + 1)]
                                      for hp in range(RW_HEADS // 2) for b in range(bsz)], axis=0)
                for pi, piece in enumerate(_split3(x8.T)):
                    cols[2 * pi].append(piece[0:64])
                    cols[2 * pi + 1].append(piece[64:128])
            lhs = jnp.concatenate([jnp.concatenate(c, axis=0) for c in cols], axis=1)
            y = _dot(lhs, qm)
            for m in range(ngrp):
                for st in range(SPREAD_STEPS):
                    o_ref[SPREAD_STEPS * m + st] = y[64 * m:64 * (m + 1), LANES * st:LANES * (st + 1)]

    qm = _spread_matrix(bsz)
    return pl.pallas_call(
        body, name="wkv_spread", grid=(t // SPREAD_BLOCK,),
        out_shape=[jax.ShapeDtypeStruct((t, 64, LANES), F32)] * n,
        in_specs=[pl.BlockSpec((bsz, SPREAD_BLOCK, RW), lambda i: (0, i, 0))] * n + [_full(qm.shape)],
        out_specs=[pl.BlockSpec((SPREAD_BLOCK, 64, LANES), lambda i: (i, 0, 0))] * n,
        compiler_params=_cparams(("parallel",)),
    )(*xs, qm)


def _wkv_fwd(r, w, kp, al, be, v):
    t, vq = v.shape[0], v.shape[1]

    def body(r_ref, w_ref, kp_ref, al_ref, be_ref, v_ref, y_ref, a_ref, u_ref, st_ref):
        @pl.when(pl.program_id(0) == 0)
        def _():
            st_ref[...] = jnp.zeros(st_ref.shape, F32)

        def step(tl, _):
            rv, wv, kv, av, bv = r_ref[tl], w_ref[tl], kp_ref[tl], al_ref[tl], be_ref[tl]
            vals = v_ref[tl]
            yrows, urows = [], []
            for q in range(vq):
                s = st_ref[q]
                u = _ksum(s * av)
                s = s * wv + bv * u + kv * vals[q:q + 1]
                st_ref[q] = s
                a_ref[tl, q] = s
                urows.append(u)
                yrows.append(_ksum(s * rv))
            y_ref[tl] = jnp.concatenate(yrows, axis=0)
            u_ref[tl] = jnp.concatenate(urows, axis=0)
            return 0

        lax.fori_loop(0, CH, step, 0)

    kspec = pl.BlockSpec((CH, 64, LANES), lambda i: (i, 0, 0))
    vspec = pl.BlockSpec((CH, vq, LANES), lambda i: (i, 0, 0))
    vsd = jax.ShapeDtypeStruct((t, vq, LANES), F32)
    return pl.pallas_call(
        body, name="wkv_fwd", grid=(t // CH,),
        out_shape=[vsd, jax.ShapeDtypeStruct((t, vq, 64, LANES), F32), vsd],
        in_specs=[kspec] * 5 + [vspec],
        out_specs=[vspec, pl.BlockSpec((CH, vq, 64, LANES), lambda i: (i, 0, 0, 0)), vspec],
        scratch_shapes=[pltpu.VMEM((vq, 64, LANES), F32)],
        compiler_params=_cparams(("arbitrary",)),
    )(r, w, kp, al, be, v)


def _wkv_bwd(r, w, kp, al, be, v, dy, states, u):
    t, vq = v.shape[0], v.shape[1]
    vs = 64 // vq
    group = LANES // vs
    n = t // CH
    ng = CH // vs

    def body(r_ref, w_ref, kp_ref, al_ref, be_ref, v_ref, dy_ref, u_ref, a_ref, ap_ref,
             dr_ref, dw_ref, dkp_ref, dal_ref, dbe_ref, dv_ref, ds_ref):
        @pl.when(pl.program_id(0) == 0)
        def _():
            ds_ref[...] = jnp.zeros(ds_ref.shape, F32)

        earliest = pl.program_id(0) == n - 1

        def reverse(i, _):
            g = ng - 1 - i
            grp = _lane_group((64, LANES), group)
            outs = None
            for j in reversed(range(vs)):
                tl = g * vs + j
                rv, wv, kv, av, bv = r_ref[tl], w_ref[tl], kp_ref[tl], al_ref[tl], be_ref[tl]
                vals, dys, us = v_ref[tl], dy_ref[tl], u_ref[tl]
                acc = None
                dvrows = []
                for q in range(vq):
                    if j > 0:
                        s_prev = a_ref[tl - 1, q]
                    else:
                        before = jnp.where(earliest, 0.0, ap_ref[0, q])
                        s_prev = jnp.where(g == 0, before, a_ref[jnp.maximum(tl - 1, 0), q])
                    dyq = dys[q:q + 1]
                    ds = ds_ref[q] + rv * dyq
                    c = _ksum(ds * bv)
                    dvrows.append(_ksum(ds * kv))
                    terms = (a_ref[tl, q] * dyq, ds * s_prev, ds * vals[q:q + 1], s_prev * c, ds * us[q:q + 1])
                    acc = terms if acc is None else tuple(a + b for a, b in zip(acc, terms))
                    ds_ref[q] = ds * wv + av * c
                dv_ref[tl] = jnp.concatenate(dvrows, axis=0)
                summed = [_fold(a, group) for a in acc]
                outs = summed if outs is None else [jnp.where(grp == j, f, o) for f, o in zip(summed, outs)]
            for ref, o in zip((dr_ref, dw_ref, dkp_ref, dal_ref, dbe_ref), outs):
                ref[g] = o
            return 0

        lax.fori_loop(0, ng, reverse, 0)

    kspec = pl.BlockSpec((CH, 64, LANES), lambda i: (n - 1 - i, 0, 0))
    gspec = pl.BlockSpec((ng, 64, LANES), lambda i: (n - 1 - i, 0, 0))
    vspec = pl.BlockSpec((CH, vq, LANES), lambda i: (n - 1 - i, 0, 0))
    ksd = jax.ShapeDtypeStruct((t // vs, 64, LANES), F32)
    return pl.pallas_call(
        body, name="wkv_bwd", grid=(n,),
        out_shape=[ksd] * 5 + [jax.ShapeDtypeStruct((t, vq, LANES), F32)],
        in_specs=[kspec] * 5 + [vspec, vspec, vspec,
                                pl.BlockSpec((CH, vq, 64, LANES), lambda i: (n - 1 - i, 0, 0, 0)),
                                pl.BlockSpec((1, vq, 64, LANES), lambda i: (jnp.maximum((n - 1 - i) * CH - 1, 0), 0, 0, 0))],
        out_specs=[gspec] * 5 + [vspec],
        scratch_shapes=[pltpu.VMEM((vq, 64, LANES), F32)],
        compiler_params=_cparams(("arbitrary",)),
    )(r, w, kp, al, be, v, dy, u, states, states)


def _post(x, tgt, pp, o, yw, r, kp, v, ln_g, ln_b, r_k, wo, wot, gpost, bo):
    bsz, t, _ = x.shape
    tt = TT_VPU
    nt = t // tt

    def body(x_ref, tgt_ref, z_ref, o_ref, yw_ref, r_ref, kp_ref, v_ref, lng_ref, lnb_ref, rk_ref, wo_ref, wot_ref,
             gpost_ref, bo_ref,
             dh_ref, dz_ref, dym_ref, dyw_ref, dbon_ref, loss_ref, dwo_ref, dgpost_ref, dlng_ref, dlnb_ref, drk_ref):
        first = (pl.program_id(0) == 0) & (pl.program_id(1) == 0)

        @pl.when(first)
        def _():
            for ref in (loss_ref, dwo_ref, dgpost_ref, dlng_ref, dlnb_ref, drk_ref):
                ref[...] = jnp.zeros(ref.shape, F32)

        bo_m = bo_ref[...]
        seg = lambda a: _seg(a, bo_m)
        rowsum = lambda a: jnp.sum(a, axis=0, keepdims=True)
        ywv, rv, kpv, vv = yw_ref[0], r_ref[0], kp_ref[0], v_ref[0]
        ln_g, r_k = lng_ref[...], rk_ref[...]
        mean = seg(ywv) * (1.0 / 64)
        yc = ywv - mean
        rstd = lax.rsqrt(seg(yc * yc) * (1.0 / 64) + GN_EPS)
        yhat = yc * rstd
        sb = seg(rv * kpv * r_k)
        y_rw = yhat * ln_g + lnb_ref[...] + sb * vv
        z = z_ref[0]
        sig = _sigmoid(z)
        sz = z * sig
        ycat = jnp.concatenate([o_ref[0], y_rw], axis=1)
        ycg = (ycat * sz).astype(BF16)
        out = _dot(ycg, wo_ref[...])
        hn, nx, rstd_o = _rms(out, gpost_ref[...], D)
        err = x_ref[0] + hn - tgt_ref[0]
        loss_ref[...] += jnp.sum(err * err) * (0.5 / D)
        dh = err * (1.0 / D)
        dh_ref[0] = dh
        dout, dgp = _rms_bwd(dh, nx, rstd_o, gpost_ref[...], D)
        dgpost_ref[...] += dgp
        doutb = dout.astype(BF16)
        dwo_ref[...] += _dot_tn(ycg, doutb)
        dycg = _dot(doutb, wot_ref[...])
        dz_ref[0] = dycg * ycat * (sig * (1.0 + z * (1.0 - sig)))
        dycat = dycg * sz
        dym_ref[0] = dycat[:, 0:512]
        dy_rw = dycat[:, 512:1024]
        dlnb_ref[...] += rowsum(dy_rw)
        dlng_ref[...] += rowsum(dy_rw * yhat)
        dyhat = dy_rw * ln_g
        dyw_ref[0] = rstd * (dyhat - seg(dyhat) * (1.0 / 64) - yhat * (seg(dyhat * yhat) * (1.0 / 64)))
        dsb = seg(dy_rw * vv)
        drk_ref[...] += rowsum(dsb * rv * kpv)
        dbon_ref[0, :, 0:512] = dsb * kpv * r_k
        dbon_ref[0, :, 512:1024] = dsb * rv * r_k
        dbon_ref[0, :, 1024:1536] = dy_rw * sb

    tok = lambda c: pl.BlockSpec((1, tt, c), lambda b, i: (b, i, 0))
    full = lambda a: _full(a.shape)
    ins = (x, tgt, pp, o, yw, r, kp, v, ln_g, ln_b, r_k, wo, wot, gpost, bo)
    in_specs = [tok(D), tok(D), tok(1024)] + [tok(512)] * 5 + [full(a) for a in ins[8:]]
    sd = lambda c: jax.ShapeDtypeStruct((bsz, t, c), F32)
    vec = lambda c: jax.ShapeDtypeStruct((1, c), F32)
    out_shape = [sd(D), sd(1024), sd(512), sd(512), sd(1536), jax.ShapeDtypeStruct((8, LANES), F32),
                 jax.ShapeDtypeStruct((1024, 1024), F32), vec(D), vec(512), vec(512), vec(512)]
    out_specs = [tok(D), tok(1024), tok(512), tok(512), tok(1536), _resident((8, LANES)), _resident((1024, 1024)),
                 _resident((1, D)), _resident((1, 512)), _resident((1, 512)), _resident((1, 512))]
    return pl.pallas_call(
        body, name="post", grid=(bsz, nt), out_shape=out_shape, in_specs=in_specs, out_specs=out_specs,
        compiler_params=_cparams(("arbitrary", "arbitrary")),
    )(*ins)


def _pre_bwd_a(pp, pos, invf, cqkv_w, mu, w0, w2p, w2pt, a0, a2p, a2pt, k_k, k_a, bo,
               dq, dk, dva, dwkv, dbon):
    gq, wuqt, gkv, wukvt = cqkv_w
    bsz, t, _ = pp.shape
    tt = TT_VPU
    nt = t // tt
    dr_w, dw_w, dkp_w, dv_w, dal_w, dbe_w = dwkv

    def body(pp_ref, pos_ref, invf_ref, gq_ref, wuqt_ref, gkv_ref, wukvt_ref, mu_ref, w0_ref, w2p_ref, w2pt_ref,
             a0_ref, a2p_ref, a2pt_ref, kk_ref, ka_ref, bo_ref, dq_ref, dk_ref, dva_ref,
             dr_ref, dw_ref, dkp_ref, dv_ref, dal_ref, dbe_ref, dbon_ref,
             da_ref, dwuq_ref, dwukv_ref, dw2p_ref, da2p_ref, dgq_ref, dgkv_ref, dmu_ref, dw0_ref, da0_ref,
             dkk_ref, dka_ref, carry):
        i = pl.program_id(1)
        first = (pl.program_id(0) == 0) & (i == 0)

        @pl.when(first)
        def _():
            for ref in (dwuq_ref, dwukv_ref, dw2p_ref, da2p_ref, dgq_ref, dgkv_ref, dmu_ref, dw0_ref, da0_ref,
                        dkk_ref, dka_ref):
                ref[...] = jnp.zeros(ref.shape, F32)

        bo_m = bo_ref[...]
        rowsum = lambda a: jnp.sum(a, axis=0, keepdims=True)
        prw = pp_ref[0, :, RW0:DP]

        @pl.when(i == 0)
        def _():
            carry[...] = jnp.zeros(carry.shape, F32)

        ps, sh = _shift_mix(prw, carry[7:8, :], mu_ref[...])
        carry[...] = prw[tt - 8:tt, :]
        k_k, k_a = kk_ref[...], ka_ref[...]
        g = _rw_gates(ps, w0_ref[...], w2p_ref[...], a0_ref[...], a2p_ref[...], k_k, k_a, bo_m)
        a, kk, k = g["a"], g["kk"], g["k"]
        dr = dr_ref[0] + dbon_ref[0, :, 0:512]
        dkp = dkp_ref[0] + dbon_ref[0, :, 512:1024]
        dv = dv_ref[0] + dbon_ref[0, :, 1024:1536]
        dbe = dbe_ref[0]
        dkk = dbe * a - dal_ref[0]
        da = dbe * kk + dkp * k * k_a
        dka_ref[...] += rowsum(dkp * k * (a - 1.0))
        dm = (dkk - kk * _seg(dkk * kk, bo_m)) / g["nrm"]
        dkk_ref[...] += rowsum(dm * k)
        dk_tot = dkp * (1.0 + (a - 1.0) * k_a) + dm * k_k
        dapre = da * a * (1.0 - a)
        da0_ref[...] += rowsum(dapre)
        dapb = dapre.astype(BF16)
        da2p_ref[...] += _dot_tn(g["misc"].astype(BF16), dapb)
        dwpre = dw_ref[0] * g["w"] * (-g["e"]) * _sigmoid(-g["wpre"])
        dw0_ref[...] += rowsum(dwpre)
        dwpb = dwpre.astype(BF16)
        th = g["th"]
        dw2p_ref[...] += _dot_tn(th.astype(BF16), dwpb)
        dmisc = _dot(dapb, a2pt_ref[...]) + _dot(dwpb, w2pt_ref[...]) * (1.0 - th * th)
        ang = pos_ref[0] * invf_ref[...]
        cs, sn = jnp.cos(ang), jnp.sin(ang)
        unrope = lambda gr: gr * cs - _rot(gr * sn)
        lane = lax.broadcasted_iota(jnp.int32, cs.shape, 1)
        dkr = dk_ref[0, :, 128:256]
        for h in range(1, HEADS):
            dkr = dkr + dk_ref[0, :, 256 * h + 128:256 * h + 256]
        dkr = jnp.where(lane < 64, unrope(dkr), 0.0)
        dmisc = dmisc + jnp.concatenate([dkr, jnp.zeros_like(dkr)], axis=1)
        dqp = jnp.concatenate(
            [blk for h in range(HEADS)
             for blk in (dq_ref[0, :, 256 * h:256 * h + 128], unrope(dq_ref[0, :, 256 * h + 128:256 * h + 256]))],
            axis=1).astype(BF16)
        dkvp = jnp.concatenate([dk_ref[0, :, 256 * h:256 * h + 128] for h in range(HEADS)] + [dva_ref[0]],
                               axis=1).astype(BF16)
        cqn, cq_nx, cq_rstd = _rms(pp_ref[0, :, CQ0:CQ0 + 256], gq_ref[...], 256)
        ckvn, ckv_nx, ckv_rstd = _rms(pp_ref[0, :, CKV0:CKV0 + 128], gkv_ref[...], 128)
        dwuq_ref[...] += _dot_tn(cqn.astype(BF16), dqp)
        dwukv_ref[...] += _dot_tn(ckvn.astype(BF16), dkvp)
        dcq, dgq = _rms_bwd(_dot(dqp, wuqt_ref[...]), cq_nx, cq_rstd, gq_ref[...], 256)
        dckv, dgkv = _rms_bwd(_dot(dkvp, wukvt_ref[...]), ckv_nx, ckv_rstd, gkv_ref[...], 128)
        dgq_ref[...] += dgq
        dgkv_ref[...] += dgkv
        dps = jnp.concatenate([dr, dk_tot, dv, dmisc], axis=1)
        dmu_ref[...] += rowsum(dps * (sh - prw))
        da_ref[0, :, 0:256] = dcq
        da_ref[0, :, 256:384] = dckv
        da_ref[0, :, 384:384 + NRW] = dps

    tok = lambda c: pl.BlockSpec((1, tt, c), lambda b, i: (b, i, 0))
    full = lambda a: _full(a.shape)
    ins = (pp, pos, invf, gq, wuqt, gkv, wukvt, mu, w0, w2p, w2pt, a0, a2p, a2pt, k_k, k_a, bo,
           dq, dk, dva, dr_w, dw_w, dkp_w, dv_w, dal_w, dbe_w, dbon)
    in_specs = ([tok(DP), tok(1)] + [full(a) for a in ins[2:17]] + [tok(1024), tok(1024), tok(512)]
                + [tok(512)] * 6 + [tok(1536)])
    shp = lambda *s: jax.ShapeDtypeStruct(s, F32)
    out_shape = [shp(bsz, t, 384 + NRW), shp(256, 1024), shp(128, 1024), shp(256, 512), shp(256, 512),
                 shp(1, 256), shp(1, 128), shp(1, NRW), shp(1, 512), shp(1, 512), shp(1, 512), shp(1, 512)]
    out_specs = [tok(384 + NRW)] + [_resident(s.shape) for s in out_shape[1:]]
    return pl.pallas_call(
        body, name="pre_bwd_a", grid=(bsz, nt), out_shape=out_shape, in_specs=in_specs, out_specs=out_specs,
        scratch_shapes=[pltpu.VMEM((8, NRW), F32)],
        compiler_params=_cparams(("arbitrary", "arbitrary")),
    )(*ins)


def _pre_bwd_b(x, dh, dz, da, mu, wpt, gpre):
    bsz, t, _ = x.shape
    nt = t // TT
    nblk = t // 8

    def body(x_ref, dh_ref, dz_ref, da_ref, nxt_ref, mu_ref, wpt_ref, gpre_ref, gx_ref, dp_ref, dgpre_ref):
        i = pl.program_id(1)
        first = (pl.program_id(0) == 0) & (i == 0)

        @pl.when(first)
        def _():
            dgpre_ref[...] = jnp.zeros(dgpre_ref.shape, F32)

        mu_v = mu_ref[...]
        dps = da_ref[0, :, 384:384 + NRW]
        nxt = jnp.where(i < nt - 1, nxt_ref[0, 0:1, 384:384 + NRW], 0.0)
        row = lax.broadcasted_iota(jnp.int32, dps.shape, 0)
        up = jnp.where(row == TT - 1, nxt, pltpu.roll(dps, TT - 1, 0))
        dprw = dps * (1.0 - mu_v) + up * mu_v
        dp = jnp.concatenate([dz_ref[0], da_ref[0, :, 0:384], dprw], axis=1).astype(BF16)
        dp_ref[0] = dp
        du = _dot(dp, wpt_ref[...])
        _, nx, rstd = _rms(x_ref[0], gpre_ref[...], D)
        dx, dg = _rms_bwd(du, nx, rstd, gpre_ref[...], D)
        dgpre_ref[...] += dg
        gx_ref[0] = dh_ref[0] + dx

    tok = lambda c: pl.BlockSpec((1, TT, c), lambda b, i: (b, i, 0))
    nxt_spec = pl.BlockSpec((1, 8, 384 + NRW), lambda b, i: (b, jnp.minimum((i + 1) * (TT // 8), nblk - 1), 0))
    ins = (x, dh, dz, da, da, mu, wpt, gpre)
    return pl.pallas_call(
        body, name="pre_bwd_b", grid=(bsz, nt),
        out_shape=[jax.ShapeDtypeStruct((bsz, t, D), F32), jax.ShapeDtypeStruct((bsz, t, DP), BF16),
                   jax.ShapeDtypeStruct((1, D), F32)],
        in_specs=[tok(D), tok(D), tok(1024), tok(384 + NRW), nxt_spec, _full(mu.shape), _full(wpt.shape),
                  _full(gpre.shape)],
        out_specs=[tok(D), tok(DP), _resident((1, D))],
        compiler_params=_cparams(("arbitrary", "arbitrary")),
    )(*ins)


def _tn_matmul(a, b, bn, name, bk=512):
    kdim, m = a.shape
    _, n = b.shape
    nk = kdim // bk

    def body(a_ref, b_ref, o_ref):
        @pl.when(pl.program_id(1) == 0)
        def _():
            o_ref[...] = jnp.zeros(o_ref.shape, F32)

        o_ref[...] += _dot_tn(a_ref[...], b_ref[...])

    return pl.pallas_call(
        body, name=name, grid=(n // bn, nk),
        out_shape=jax.ShapeDtypeStruct((m, n), F32),
        in_specs=[pl.BlockSpec((bk, m), lambda j, kk: (kk, 0)), pl.BlockSpec((bk, bn), lambda j, kk: (kk, j))],
        out_specs=pl.BlockSpec((m, bn), lambda j, kk: (0, j)),
        compiler_params=_cparams(("parallel", "arbitrary")),
    )(a, b)


SHARDED = ("w_in", "mla_w_uq", "mla_w_ukv", "rw_w2", "rw_a2", "w_out")
SMALL = ("norm_pre_g", "mla_q_norm_g", "mla_kv_norm_g", "rw_mu", "rw_w0", "rw_a0", "rw_k_k", "rw_k_a", "rw_r_k",
         "rw_ln_g", "rw_ln_b", "norm_post_g")
WEIGHTS = ("norm_pre_g", "w_in", "mla_q_norm_g", "mla_w_uq", "mla_kv_norm_g", "mla_w_ukv", "rw_mu", "rw_w0", "rw_w2",
           "rw_a0", "rw_a2", "rw_k_k", "rw_k_a", "rw_r_k", "rw_ln_g", "rw_ln_b", "w_out", "norm_post_g")


W_IN_PIECES = ((0, CQ0, 384), (384, RW0 + 1536, 64), (448, RW0, 1536), (1984, RW0 + 1600, 128), (2112, Z0, 1024))


def _w_in_from_shards(g):
    cs = g.shape[2]
    parts = []
    for o, _, width in sorted(W_IN_PIECES, key=lambda piece: piece[1]):
        at = o
        while at < o + width:
            s = at // cs
            end = min(o + width, (s + 1) * cs)
            parts.append(g[s][:, at - s * cs:end - s * cs])
            at = end
    parts.append(jnp.zeros((g.shape[1], DP - D_IN), g.dtype))
    return jnp.concatenate(parts, axis=1)


def _w_in_grad_to_shards(d_wp):
    cs = D_IN // N_SHARD
    shards = []
    for s in range(N_SHARD):
        parts = []
        for o, p, width in sorted(W_IN_PIECES):
            lo, hi = max(o, s * cs), min(o + width, (s + 1) * cs)
            if lo < hi:
                parts.append(d_wp[:, p + lo - o:p + hi - o])
        shards.append(jnp.concatenate(parts, axis=1))
    return jnp.stack(shards)


def _unpack_shard(packed, like):
    out, at = {}, 0
    for n, rows in zip(SHARDED[1:5], PACK_ROWS):
        out[n] = packed[at:at + rows].reshape(like[n].shape)
        at += rows
    return out


def _constants():
    bo = np.kron(np.eye(2, dtype=np.float32), np.ones((64, 64), np.float32))
    inv = ROPE_THETA ** (-np.arange(0, 64, 2, dtype=np.float32) / 64)
    invf = np.concatenate([inv, inv, np.zeros(64, np.float32)]).astype(np.float32)[None, :]
    return jnp.asarray(bo, BF16), jnp.asarray(invf)


def kernel(x, positions, norm_pre_g, w_in, mla_q_norm_g, mla_w_uq, mla_kv_norm_g, mla_w_ukv, rw_mu, rw_w0, rw_w2, rw_a0, rw_a2, rw_k_k, rw_k_a, rw_r_k, rw_ln_g, rw_ln_b, w_out, norm_post_g, loss_target, m_norm_pre_g, m_w_in, m_mla_q_norm_g, m_mla_w_uq, m_mla_kv_norm_g, m_mla_w_ukv, m_rw_mu, m_rw_w0, m_rw_w2, m_rw_a0, m_rw_a2, m_rw_k_k, m_rw_k_a, m_rw_r_k, m_rw_ln_g, m_rw_ln_b, m_w_out, m_norm_post_g, v_norm_pre_g, v_w_in, v_mla_q_norm_g, v_mla_w_uq, v_mla_kv_norm_g, v_mla_w_ukv, v_rw_mu, v_rw_w0, v_rw_w2, v_rw_a0, v_rw_a2, v_rw_k_k, v_rw_k_a, v_rw_r_k, v_rw_ln_g, v_rw_ln_b, v_w_out, v_norm_post_g):
    wts = dict(norm_pre_g=norm_pre_g, w_in=w_in, mla_q_norm_g=mla_q_norm_g, mla_w_uq=mla_w_uq,
               mla_kv_norm_g=mla_kv_norm_g, mla_w_ukv=mla_w_ukv, rw_mu=rw_mu, rw_w0=rw_w0, rw_w2=rw_w2, rw_a0=rw_a0,
               rw_a2=rw_a2, rw_k_k=rw_k_k, rw_k_a=rw_k_a, rw_r_k=rw_r_k, rw_ln_g=rw_ln_g, rw_ln_b=rw_ln_b, w_out=w_out,
               norm_post_g=norm_post_g)
    mom_m = dict(norm_pre_g=m_norm_pre_g, w_in=m_w_in, mla_q_norm_g=m_mla_q_norm_g, mla_w_uq=m_mla_w_uq,
                 mla_kv_norm_g=m_mla_kv_norm_g, mla_w_ukv=m_mla_w_ukv, rw_mu=m_rw_mu, rw_w0=m_rw_w0, rw_w2=m_rw_w2,
                 rw_a0=m_rw_a0, rw_a2=m_rw_a2, rw_k_k=m_rw_k_k, rw_k_a=m_rw_k_a, rw_r_k=m_rw_r_k, rw_ln_g=m_rw_ln_g,
                 rw_ln_b=m_rw_ln_b, w_out=m_w_out, norm_post_g=m_norm_post_g)
    mom_v = dict(norm_pre_g=v_norm_pre_g, w_in=v_w_in, mla_q_norm_g=v_mla_q_norm_g, mla_w_uq=v_mla_w_uq,
                 mla_kv_norm_g=v_mla_kv_norm_g, mla_w_ukv=v_mla_w_ukv, rw_mu=v_rw_mu, rw_w0=v_rw_w0, rw_w2=v_rw_w2,
                 rw_a0=v_rw_a0, rw_a2=v_rw_a2, rw_k_k=v_rw_k_k, rw_k_a=v_rw_k_a, rw_r_k=v_rw_r_k, rw_ln_g=v_rw_ln_g,
                 rw_ln_b=v_rw_ln_b, w_out=v_w_out, norm_post_g=v_norm_post_g)
    bsz, t, _ = x.shape
    bo, invf = _constants()

    g_in, g_uq, g_ukv, g_w2, g_a2, g_out = _ag_weights([wts[n][0] for n in SHARDED])
    wp = _w_in_from_shards(g_in)
    wuq = jnp.pad(jnp.transpose(g_uq, (1, 0, 2)).reshape(256, HEADS, 192), ((0, 0), (0, 0), (0, 64))).reshape(256, 1024)
    wukv = jnp.transpose(jnp.transpose(g_ukv, (1, 0, 2)).reshape(128, HEADS, 2, 128), (0, 2, 1, 3)).reshape(128, 1024)
    w2 = jnp.transpose(g_w2, (1, 0, 2)).reshape(64, RW)
    a2 = jnp.transpose(g_a2, (1, 0, 2)).reshape(64, RW)
    w2p = jnp.pad(w2, ((64, 128), (0, 0)))
    a2p = jnp.pad(a2, ((128, 64), (0, 0)))
    wo = g_out.reshape(D, D)
    mu = jnp.concatenate([rw_mu[:, 0:1536], jnp.zeros((1, 64), F32), rw_mu[:, 1536:1664], jnp.zeros((1, 64), F32)],
                         axis=1)
    r_k = rw_r_k.reshape(1, RW)
    pos = positions.astype(F32)[:, :, None]

    (u, pp, q_att, k_att, v_att, r, w, kp, v, al, be) = _pre_fwd(
        x, pos, invf, norm_pre_g, wp, mla_q_norm_g, wuq, mla_kv_norm_g, wukv, mu, rw_w0, w2p, rw_a0, a2p, rw_k_k,
        rw_k_a, bo)
    o, lse = _attn_fwd(q_att, k_att, v_att)
    rw_k = _spread_k([r, w, kp, al, be])
    v_v = _to_v(v)
    yw_v, states, u_v = _wkv_fwd(*rw_k, v_v)
    yw = _from_v(yw_v, bsz)

    (dh, dz, dym, dyw, dbon, loss_acc, d_wo, d_gpost, d_lng, d_lnb, d_rk) = _post(
        x, loss_target, pp, o, yw, r, kp, v, rw_ln_g, rw_ln_b, r_k, wo, wo.T, norm_post_g, bo)

    d_k = _wkv_bwd(*rw_k, v_v, _to_v(dyw), states, u_v)
    dr_w, dw_w, dkp_w, dal_w, dbe_w = _gather_k(d_k[:5], bsz)
    dwkv = (dr_w, dw_w, dkp_w, _from_v(d_k[5], bsz), dal_w, dbe_w)
    dq, dk, dva = _attn_bwd(q_att, k_att, v_att, o, lse, dym)

    (da, d_wuq, d_wukv, d_w2p, d_a2p, d_gq, d_gkv, d_mu, d_w0, d_a0, d_kk, d_ka) = _pre_bwd_a(
        pp, pos, invf, (mla_q_norm_g, wuq.T, mla_kv_norm_g, wukv.T), mu, rw_w0, w2p, w2p.T, rw_a0, a2p, a2p.T,
        rw_k_k, rw_k_a, bo, dq, dk, dva, dwkv, dbon)
    grad_x, dpb, d_gpre = _pre_bwd_b(x, dh, dz, da, mu, wp.T, norm_pre_g)
    d_wp = _tn_matmul(u.reshape(bsz * t, D), dpb.reshape(bsz * t, DP), DP, "dw_in", bk=1024)

    full_g = {
        "mla_w_uq": d_wuq.reshape(256, HEADS, 256)[:, :, :192].reshape(256, 768),
        "mla_w_ukv": jnp.transpose(d_wukv.reshape(128, 2, HEADS, 128), (0, 2, 1, 3)).reshape(128, 1024),
        "rw_w2": d_w2p[64:128],
        "rw_a2": d_a2p[128:192],
        "w_out": d_wo,
    }
    small_g = {
        "norm_pre_g": d_gpre, "mla_q_norm_g": d_gq, "mla_kv_norm_g": d_gkv,
        "rw_mu": jnp.concatenate([d_mu[:, 0:1536], d_mu[:, 1600:1728]], axis=1),
        "rw_w0": d_w0, "rw_a0": d_a0, "rw_k_k": d_kk, "rw_k_a": d_ka, "rw_r_k": d_rk, "rw_ln_g": d_lng,
        "rw_ln_b": d_lnb, "norm_post_g": d_gpost,
    }

    def by_shard(g):
        rows, cols = g.shape
        return jnp.transpose(g.reshape(rows, N_SHARD, cols // N_SHARD), (1, 0, 2))

    g_in = _w_in_grad_to_shards(d_wp)
    g_out = full_g["w_out"].reshape(N_SHARD, D // N_SHARD, D)
    packed = jnp.concatenate([by_shard(full_g[n]).reshape(N_SHARD, -1, LANES) for n in SHARDED[1:5]], axis=1)
    halves = [a.reshape(N_SHARD, 2, a.shape[1] // 2, a.shape[2]) for a in (g_in, g_out, packed)]
    red_in, red_out, red_rest = _rs_chips(*_rs_pairs(halves))
    g_shard = red_rest.reshape(PACK_REST, LANES)

    flat = lambda a: a.reshape(1, -1)
    g_small = _small_allreduce([flat(small_g[n]) for n in SMALL], loss_acc)
    loss = g_small[SMALL_USED, 0]

    g_sharded = _unpack_shard(g_shard, {n: wts[n][0] for n in SHARDED})
    g_sharded["w_in"] = red_in.reshape(wts["w_in"][0].shape)
    g_sharded["w_out"] = red_out.reshape(wts["w_out"][0].shape)
    sh = _adamw([wts[n][0] for n in SHARDED], [g_sharded[n] for n in SHARDED], [mom_m[n][0] for n in SHARDED],
                [mom_v[n][0] for n in SHARDED], "adamw_sharded")
    sm = _adamw_small([flat(wts[n]) for n in SMALL], g_small, [flat(mom_m[n]) for n in SMALL],
                      [flat(mom_v[n]) for n in SMALL])

    def outputs(sharded, small):
        out = {n: a[None] for n, a in zip(SHARDED, sharded)}
        out.update({n: a.reshape(wts[n].shape) for n, a in zip(SMALL, small)})
        return out

    grads = outputs([g_sharded[n] for n in SHARDED], sm[0])
    deltas, new_m, new_v = (outputs(sh[k], sm[k + 1]) for k in range(3))
    return (loss, grad_x, *[grads[n] for n in WEIGHTS], *[deltas[n] for n in WEIGHTS],
            *[new_m[n] for n in WEIGHTS], *[new_v[n] for n in WEIGHTS])
```

```python
import numpy as np
import jax
import jax.numpy as jnp
from jax import lax
from jax.experimental import pallas as pl
from jax.experimental.pallas import tpu as pltpu

F32, BF16 = jnp.float32, jnp.bfloat16
MESH = pl.DeviceIdType.MESH

D = 1024
HEADS = 4
RW = 512
NORM_EPS = 1e-6
GN_EPS = 64e-5
ROPE_THETA = 10000.0
SCALE = (128 + 64) ** -0.5
D_IN = 3136
LR, B1, B2, ADAM_EPS, WD, STEP = 0.001, 0.9, 0.999, 1e-08, 0.01, 10

Z0, CQ0, CKV0, RW0, DP = 0, 1024, 1280, 1408, 3200
NRW = DP - RW0

LANES = 128
SUBLANES = 8
VMEM_LIMIT = 56 * 1024 * 1024

TT = 512
TT_VPU = 256
TQ = 512

N_SHARD = 4
PACK_ROWS = (256 * 192 // 128, 128 * 256 // 128, 64, 64)
PACK_REST = sum(PACK_ROWS)
SMALL_ROWS = 64
SMALL_USED = 60


def _cparams(sem=None):
    return pltpu.CompilerParams(dimension_semantics=sem, vmem_limit_bytes=VMEM_LIMIT)


def _full(shape):
    n = len(shape)
    return pl.BlockSpec(shape, lambda *_: (0,) * n, pipeline_mode=pl.Buffered(1))


def _resident(shape):
    n = len(shape)
    return pl.BlockSpec(shape, lambda *_: (0,) * n)


def _dot(a, b):
    return jnp.dot(a, b, preferred_element_type=F32)


def _dot_nt(a, b):
    return lax.dot_general(a, b, (((1,), (1,)), ((), ())), preferred_element_type=F32)


def _dot_tn(a, b):
    return lax.dot_general(a, b, (((0,), (0,)), ((), ())), preferred_element_type=F32)


def _split3(x):
    hi = x.astype(BF16)
    r1 = x - hi.astype(F32)
    mid = r1.astype(BF16)
    lo = (r1 - mid.astype(F32)).astype(BF16)
    return hi, mid, lo


def _seg(x, bo):
    rows, nblk = x.shape[0], x.shape[1] // LANES
    pieces = [p for i in range(nblk) for p in _split3(x[:, LANES * i:LANES * (i + 1)])]
    res = _dot(jnp.concatenate(pieces, axis=0), bo)
    parts = [res[(3 * i) * rows:(3 * i + 1) * rows] + res[(3 * i + 1) * rows:(3 * i + 2) * rows]
             + res[(3 * i + 2) * rows:(3 * i + 3) * rows] for i in range(nblk)]
    return parts[0] if nblk == 1 else jnp.concatenate(parts, axis=1)


def _rms(x, g, n):
    rstd = lax.rsqrt(jnp.sum(x * x, axis=-1, keepdims=True) * (1.0 / n) + NORM_EPS)
    nx = x * rstd
    return nx * g, nx, rstd


def _rms_bwd(dy, nx, rstd, g, n):
    dn = dy * g
    dx = rstd * (dn - nx * (jnp.sum(dn * nx, axis=-1, keepdims=True) * (1.0 / n)))
    return dx, jnp.sum(dy * nx, axis=0, keepdims=True)


def _rot(x):
    lane = lax.broadcasted_iota(jnp.int32, x.shape, 1)
    return jnp.where((lane % 64) < 32, -pltpu.roll(x, x.shape[1] - 32, 1), pltpu.roll(x, 32, 1))


def _sigmoid(x):
    return 1.0 / (1.0 + jnp.exp(-x))


def _softplus(x):
    return jnp.maximum(x, 0.0) + jnp.log(1.0 + jnp.exp(-jnp.abs(x)))


def _rw_gates(ps, w0, w2p, a0, a2p, k_k, k_a, bo):
    r, k, v, misc = ps[:, 0:512], ps[:, 512:1024], ps[:, 1024:1536], ps[:, 1536:NRW]
    th = jnp.tanh(misc)
    wpre = w0 + _dot(th.astype(BF16), w2p)
    e = jnp.exp(-_softplus(-wpre) - 0.5)
    w = jnp.exp(-e)
    a = _sigmoid(a0 + _dot(misc.astype(BF16), a2p))
    m = k * k_k
    nrm = jnp.maximum(jnp.sqrt(_seg(m * m, bo)), 1e-12)
    kk = m / nrm
    kp = k * (1.0 + (a - 1.0) * k_a)
    return dict(r=r, k=k, v=v, misc=misc, th=th, wpre=wpre, e=e, w=w, a=a, nrm=nrm, kk=kk, kp=kp)


def _shift_mix(prw, prev_row, mu):
    row = lax.broadcasted_iota(jnp.int32, prw.shape, 0)
    sh = jnp.where(row == 0, prev_row, pltpu.roll(prw, 1, 0))
    return prw + (sh - prw) * mu, sh


def _ag_weights(shards):
    n = len(shards)

    def body(*refs):
        ins, outs = refs[:n], refs[n:2 * n]
        ici_send, ici_recv, d2d_send, d2d_recv = refs[2 * n:2 * n + 4]
        x, y, c = lax.axis_index("x"), lax.axis_index("y"), lax.axis_index("c")
        mine = 2 * x + y
        for w in range(n):
            outs[w][mine] = ins[w][...].astype(BF16)
        flips = ((1, 0), (0, 1), (1, 1))

        def half(w, shard, cc):
            rows = outs[w].shape[1] // 2
            return outs[w].at[shard, pl.ds(pl.multiple_of(cc * rows, 16), rows)]

        def ici(w, k, shard):
            fx, fy = flips[k]
            return pltpu.make_async_remote_copy(
                src_ref=half(w, shard, c), dst_ref=half(w, shard, c),
                send_sem=ici_send.at[w * 3 + k], recv_sem=ici_recv.at[w * 3 + k],
                device_id=(x ^ fx, y ^ fy, c), device_id_type=MESH)

        def d2d(w, k, cc):
            fx, fy = flips[k]
            theirs = 2 * (x ^ fx) + (y ^ fy)
            return pltpu.make_async_remote_copy(
                src_ref=half(w, theirs, cc), dst_ref=half(w, theirs, cc),
                send_sem=d2d_send.at[w * 3 + k], recv_sem=d2d_recv.at[w * 3 + k],
                device_id=(x, y, 1 - c), device_id_type=MESH)

        for w in range(n):
            for k in range(3):
                ici(w, k, mine).start()
        for w in range(n):
            for k in range(3):
                fx, fy = flips[k]
                ici(w, k, 2 * (x ^ fx) + (y ^ fy)).wait_recv()
                d2d(w, k, c).start()
        for w in range(n):
            for k in range(3):
                d2d(w, k, 1 - c).wait_recv()
        for w in range(n):
            for k in range(3):
                ici(w, k, mine).wait_send()
                d2d(w, k, c).wait_send()

    vm = pl.BlockSpec(memory_space=pltpu.VMEM)
    return pl.pallas_call(
        body, name="ag_weights",
        out_shape=[jax.ShapeDtypeStruct((N_SHARD,) + s.shape, BF16) for s in shards],
        in_specs=[vm] * n, out_specs=[vm] * n,
        scratch_shapes=[pltpu.SemaphoreType.DMA((3 * n,))] * 4,
        compiler_params=pltpu.CompilerParams(vmem_limit_bytes=VMEM_LIMIT),
    )(*shards)


def _rs_pairs(halves):
    n = len(halves)

    def body(*refs):
        h_refs, sum_refs, sumb_refs, recvs = (refs[k * n:(k + 1) * n] for k in range(4))
        send_sem, recv_sem = refs[4 * n:]
        x, y, c = lax.axis_index("x"), lax.axis_index("y"), lax.axis_index("c")
        cps = [pltpu.make_async_remote_copy(src_ref=h_refs[i].at[s, 1 - c], dst_ref=recvs[i].at[s],
                                            send_sem=send_sem.at[i * N_SHARD + s], recv_sem=recv_sem.at[i * N_SHARD + s],
                                            device_id=(x, y, 1 - c), device_id_type=MESH)
               for i in range(n) for s in range(N_SHARD)]
        for cp in cps:
            cp.start()
        for i in range(n):
            for s in range(N_SHARD):
                cps[i * N_SHARD + s].wait_recv()
                acc = h_refs[i][s, c] + recvs[i][s]
                sum_refs[i][s] = acc
                sumb_refs[i][s] = acc.astype(BF16)
        for cp in cps:
            cp.wait_send()

    vm = pl.BlockSpec(memory_space=pltpu.VMEM)
    shapes = [(N_SHARD,) + h.shape[2:] for h in halves]
    outs = pl.pallas_call(
        body, name="rs_pairs",
        out_shape=[jax.ShapeDtypeStruct(sh, F32) for sh in shapes] + [jax.ShapeDtypeStruct(sh, BF16) for sh in shapes],
        in_specs=[vm] * n, out_specs=[vm] * (2 * n),
        scratch_shapes=[pltpu.VMEM(sh, F32) for sh in shapes] + [pltpu.SemaphoreType.DMA((n * N_SHARD,)),
                                                                 pltpu.SemaphoreType.DMA((n * N_SHARD,))],
        compiler_params=pltpu.CompilerParams(vmem_limit_bytes=VMEM_LIMIT),
    )(*halves)
    return outs[:n], outs[n:]


def _rs_chips(part_f32, part_bf16):
    n = len(part_f32)

    def body(*refs):
        own_refs, src_refs, out_refs, recvs = (refs[k * n:(k + 1) * n] for k in range(4))
        ici_send, ici_recv, d2d_send, d2d_recv = refs[4 * n:]
        x, y, c = lax.axis_index("x"), lax.axis_index("y"), lax.axis_index("c")
        mine = 2 * x + y
        flips = ((1, 0), (0, 1), (1, 1))
        cps = []
        for i in range(n):
            for k, (fx, fy) in enumerate(flips):
                theirs = 2 * (x ^ fx) + (y ^ fy)
                cps.append(pltpu.make_async_remote_copy(
                    src_ref=src_refs[i].at[theirs], dst_ref=recvs[i].at[k],
                    send_sem=ici_send.at[3 * i + k], recv_sem=ici_recv.at[3 * i + k],
                    device_id=(x ^ fx, y ^ fy, c), device_id_type=MESH))
        for cp in cps:
            cp.start()
        handed = []
        for i in range(n):
            acc = own_refs[i][mine]
            for k in range(3):
                cps[3 * i + k].wait_recv()
                acc = acc + recvs[i][k].astype(F32)
            out_refs[i][c] = acc
            to_sibling = pltpu.make_async_remote_copy(
                src_ref=out_refs[i].at[c], dst_ref=out_refs[i].at[c], send_sem=d2d_send.at[i], recv_sem=d2d_recv.at[i],
                device_id=(x, y, 1 - c), device_id_type=MESH)
            to_sibling.start()
            handed.append(to_sibling)
        for i in range(n):
            pltpu.make_async_remote_copy(
                src_ref=out_refs[i].at[1 - c], dst_ref=out_refs[i].at[1 - c], send_sem=d2d_send.at[i],
                recv_sem=d2d_recv.at[i], device_id=(x, y, 1 - c), device_id_type=MESH).wait_recv()
        for cp in handed + cps:
            cp.wait_send()

    vm = pl.BlockSpec(memory_space=pltpu.VMEM)
    return pl.pallas_call(
        body, name="rs_chips",
        out_shape=[jax.ShapeDtypeStruct((2,) + p.shape[1:], F32) for p in part_f32],
        in_specs=[vm] * (2 * n), out_specs=[vm] * n,
        scratch_shapes=[pltpu.VMEM((3,) + p.shape[1:], BF16) for p in part_bf16]
        + [pltpu.SemaphoreType.DMA((3 * n,)), pltpu.SemaphoreType.DMA((3 * n,)), pltpu.SemaphoreType.DMA((n,)),
           pltpu.SemaphoreType.DMA((n,))],
        compiler_params=pltpu.CompilerParams(vmem_limit_bytes=VMEM_LIMIT),
    )(*part_f32, *part_bf16)


def _small_rows(vecs):
    out, at = [], 0
    for vec in vecs:
        rows = vec.shape[1] // LANES
        out.append((rows, at))
        at += rows
    assert at == SMALL_USED
    return out


def _small_allreduce(vecs, loss_acc):
    n = len(vecs)
    layout = _small_rows(vecs)

    def body(*refs):
        loss_ref, out_ref, stage, recv, send_sems, recv_sems = refs[n:]
        for vec_ref, (rows, at) in zip(refs[:n], layout):
            for j in range(rows):
                stage[at + j:at + j + 1, :] = vec_ref[0:1, LANES * j:LANES * (j + 1)]
        stage[SMALL_USED:SMALL_ROWS, :] = loss_ref[0:SMALL_ROWS - SMALL_USED, :]
        x, y, c = lax.axis_index("x"), lax.axis_index("y"), lax.axis_index("c")
        me = 4 * x + 2 * y + c
        cps = []
        for k in range(1, 8):
            fx, fy, fc = (k >> 2) & 1, (k >> 1) & 1, k & 1
            cps.append(pltpu.make_async_remote_copy(
                src_ref=stage, dst_ref=recv.at[k - 1],
                send_sem=send_sems.at[k - 1], recv_sem=recv_sems.at[k - 1],
                device_id=(x ^ fx, y ^ fy, c ^ fc), device_id_type=MESH))
        for cp in cps:
            cp.start()
        for cp in cps:
            cp.wait()
        acc = jnp.zeros(stage.shape, F32)
        for j in range(8):
            slot = jnp.maximum((me ^ j) - 1, 0)
            acc = acc + jnp.where(me == j, stage[...], recv[slot])
        out_ref[...] = acc

    vm = pl.BlockSpec(memory_space=pltpu.VMEM)
    shape = (SMALL_ROWS, LANES)
    return pl.pallas_call(
        body, name="small_allreduce",
        out_shape=jax.ShapeDtypeStruct(shape, F32),
        in_specs=[vm] * (n + 1), out_specs=vm,
        scratch_shapes=[pltpu.VMEM(shape, F32), pltpu.VMEM((7,) + shape, F32), pltpu.SemaphoreType.DMA((7,)),
                        pltpu.SemaphoreType.DMA((7,))],
    )(*vecs, loss_acc)


def _adamw_small(ws, g_packed, ms, vs):
    n = len(ws)
    layout = _small_rows(ws)

    def body(*refs):
        g_ref = refs[3 * n]
        outs = refs[3 * n + 1:]
        for i, (rows, at) in enumerate(layout):
            w_ref, m_ref, v_ref = refs[i], refs[n + i], refs[2 * n + i]
            go_ref, d_ref, nm_ref, nv_ref = (outs[k * n + i] for k in range(4))
            for j in range(rows):
                lanes = slice(LANES * j, LANES * (j + 1))
                gg = g_ref[at + j:at + j + 1, :]
                nm = B1 * m_ref[0:1, lanes] + (1.0 - B1) * gg
                nv = B2 * v_ref[0:1, lanes] + (1.0 - B2) * (gg * gg)
                m_hat = nm / (1.0 - B1 ** STEP)
                v_hat = nv / (1.0 - B2 ** STEP)
                go_ref[0:1, lanes] = gg
                d_ref[0:1, lanes] = -LR * (m_hat / (jnp.sqrt(v_hat) + ADAM_EPS) + WD * w_ref[0:1, lanes])
                nm_ref[0:1, lanes] = nm
                nv_ref[0:1, lanes] = nv

    vm = pl.BlockSpec(memory_space=pltpu.VMEM)
    sds = [jax.ShapeDtypeStruct(w.shape, F32) for w in ws]
    outs = pl.pallas_call(
        body, name="adamw_small", out_shape=sds * 4, in_specs=[vm] * (3 * n + 1), out_specs=[vm] * (4 * n),
    )(*ws, *ms, *vs, g_packed)
    return outs[:n], outs[n:2 * n], outs[2 * n:3 * n], outs[3 * n:]


ADAM_ROWS = 64


def _adamw(ws, gs, ms, vs, name):
    n = len(ws)

    def body(*refs):
        for i in range(n):
            w_ref, g_ref, m_ref, v_ref = (refs[k * n + i] for k in range(4))
            d_ref, nm_ref, nv_ref = (refs[(4 + k) * n + i] for k in range(3))
            rows = min(ADAM_ROWS, w_ref.shape[0])

            def chunk(r, _):
                at = pl.ds(pl.multiple_of(r * rows, SUBLANES), rows)
                gg = g_ref[at, :]
                nm = B1 * m_ref[at, :] + (1.0 - B1) * gg
                nv = B2 * v_ref[at, :] + (1.0 - B2) * (gg * gg)
                m_hat = nm / (1.0 - B1 ** STEP)
                v_hat = nv / (1.0 - B2 ** STEP)
                d_ref[at, :] = -LR * (m_hat / (jnp.sqrt(v_hat) + ADAM_EPS) + WD * w_ref[at, :])
                nm_ref[at, :] = nm
                nv_ref[at, :] = nv
                return 0

            lax.fori_loop(0, w_ref.shape[0] // rows, chunk, 0)

    vm = pl.BlockSpec(memory_space=pltpu.VMEM)
    sds = [jax.ShapeDtypeStruct(w.shape, F32) for w in ws]
    outs = pl.pallas_call(
        body, name=name, out_shape=sds * 3, in_specs=[vm] * (4 * n), out_specs=[vm] * (3 * n),
        compiler_params=pltpu.CompilerParams(vmem_limit_bytes=VMEM_LIMIT),
    )(*ws, *gs, *ms, *vs)
    return outs[:n], outs[n:2 * n], outs[2 * n:]


def _pre_fwd(x, pos, invf, gpre, wp, gq, wuq, gkv, wukv, mu, w0, w2p, a0, a2p, k_k, k_a, bo):
    bsz, t, _ = x.shape
    nt = t // TT

    def body(x_ref, pos_ref, invf_ref, gpre_ref, wp_ref, gq_ref, wuq_ref, gkv_ref, wukv_ref, mu_ref, w0_ref,
             w2p_ref, a0_ref, a2p_ref, kk_ref, ka_ref, bo_ref,
             u_ref, pp_ref, q_ref, k_ref, v_ref, r_o, w_o, kp_o, vv_o, al_o, be_o, carry):
        i = pl.program_id(1)
        u, _, _ = _rms(x_ref[0], gpre_ref[...], D)
        ub = u.astype(BF16)
        u_ref[0] = ub
        p = _dot(ub, wp_ref[...])
        pp_ref[0] = p
        prw = p[:, RW0:DP]

        @pl.when(i == 0)
        def _():
            carry[...] = jnp.zeros(carry.shape, F32)

        ps, _ = _shift_mix(prw, carry[7:8, :], mu_ref[...])
        carry[...] = prw[TT - 8:TT, :]

        g = _rw_gates(ps, w0_ref[...], w2p_ref[...], a0_ref[...], a2p_ref[...], kk_ref[...], ka_ref[...],
                      bo_ref[...])
        r_o[0] = g["r"]
        w_o[0] = g["w"]
        kp_o[0] = g["kp"]
        vv_o[0] = g["v"]
        al_o[0] = -g["kk"]
        be_o[0] = g["kk"] * g["a"]

        cqn, _, _ = _rms(p[:, CQ0:CQ0 + 256], gq_ref[...], 256)
        q = _dot(cqn.astype(BF16), wuq_ref[...])
        ckvn, _, _ = _rms(p[:, CKV0:CKV0 + 128], gkv_ref[...], 128)
        kv = _dot(ckvn.astype(BF16), wukv_ref[...])
        ang = pos_ref[0] * invf_ref[...]
        cs, sn = jnp.cos(ang), jnp.sin(ang)
        lane = lax.broadcasted_iota(jnp.int32, cs.shape, 1)
        kr = ps[:, 1536:1536 + LANES]
        kr = jnp.where(lane < 64, kr * cs + _rot(kr) * sn, 0.0).astype(BF16)
        for h in range(HEADS):
            qr = q[:, 256 * h + 128:256 * h + 256]
            q_ref[0, :, 256 * h:256 * h + 128] = q[:, 256 * h:256 * h + 128].astype(BF16)
            q_ref[0, :, 256 * h + 128:256 * h + 256] = (qr * cs + _rot(qr) * sn).astype(BF16)
            k_ref[0, :, 256 * h:256 * h + 128] = kv[:, 128 * h:128 * h + 128].astype(BF16)
            k_ref[0, :, 256 * h + 128:256 * h + 256] = kr
        v_ref[0] = kv[:, 512:1024].astype(BF16)

    tok = lambda c: pl.BlockSpec((1, TT, c), lambda b, i: (b, i, 0))
    full = lambda a: _full(a.shape)
    ins = (x, pos, invf, gpre, wp, gq, wuq, gkv, wukv, mu, w0, w2p, a0, a2p, k_k, k_a, bo)
    in_specs = [tok(D), tok(1)] + [full(a) for a in ins[2:]]
    sd = lambda c, dt: jax.ShapeDtypeStruct((bsz, t, c), dt)
    out_shape = [sd(D, BF16), sd(DP, F32), sd(1024, BF16), sd(1024, BF16), sd(512, BF16)] + [sd(RW, F32)] * 6
    out_specs = [tok(D), tok(DP), tok(1024), tok(1024), tok(512)] + [tok(RW)] * 6
    return pl.pallas_call(
        body, name="pre_fwd", grid=(bsz, nt), out_shape=out_shape, in_specs=in_specs, out_specs=out_specs,
        scratch_shapes=[pltpu.VMEM((8, NRW), F32)],
        compiler_params=_cparams(("arbitrary", "arbitrary")),
    )(*ins)


def _attn_fwd(q, k, v):
    bsz, t, _ = q.shape
    nq = t // TQ

    hps = HEADS

    def body(q_ref, k_ref, v_ref, o_ref, lse_ref):
        i = pl.program_id(2)

        def step(j, carry, diagonal):
            at = pl.ds(pl.multiple_of(j * TQ, TQ), TQ)
            out = []
            for hh in range(hps):
                m, l, acc = carry[hh]
                s = _dot_nt(q_ref[0, :, 256 * hh:256 * (hh + 1)], k_ref[0, at, 256 * hh:256 * (hh + 1)]) * SCALE
                if diagonal:
                    s = jnp.where(lax.broadcasted_iota(jnp.int32, (TQ, TQ), 1)
                                  <= lax.broadcasted_iota(jnp.int32, (TQ, TQ), 0), s, -1e30)
                mn = jnp.maximum(m, jnp.max(s, axis=1, keepdims=True))
                p = jnp.exp(s - mn)
                al = jnp.exp(m - mn)
                l = al * l + jnp.sum(p, axis=1, keepdims=True)
                acc = al * acc + _dot(p.astype(BF16), v_ref[0, at, LANES * hh:LANES * (hh + 1)])
                out.append((mn, l, acc))
            return tuple(out)

        start = (jnp.full((TQ, 1), -1e30, F32), jnp.zeros((TQ, 1), F32), jnp.zeros((TQ, LANES), F32))
        before = lax.fori_loop(0, i, lambda j, carry: step(j, carry, False), (start,) * hps)
        for hh, (m, l, acc) in enumerate(step(i, before, True)):
            o_ref[0, :, LANES * hh:LANES * (hh + 1)] = acc / l
            lse_ref[0, hh] = jnp.broadcast_to(m + jnp.log(l), (TQ, LANES))

    return pl.pallas_call(
        body, name="attn_fwd", grid=(bsz, HEADS // hps, nq),
        out_shape=[jax.ShapeDtypeStruct((bsz, t, 512), F32), jax.ShapeDtypeStruct((bsz, HEADS, t, LANES), F32)],
        in_specs=[pl.BlockSpec((1, TQ, 256 * hps), lambda b, h, i: (b, i, h)),
                  pl.BlockSpec((1, t, 256 * hps), lambda b, h, i: (b, 0, h)),
                  pl.BlockSpec((1, t, LANES * hps), lambda b, h, i: (b, 0, h))],
        out_specs=[pl.BlockSpec((1, TQ, LANES * hps), lambda b, h, i: (b, i, h)),
                   pl.BlockSpec((1, hps, TQ, LANES), lambda b, h, i: (b, h, i, 0))],
        compiler_params=_cparams(("parallel", "parallel", "arbitrary")),
    )(q, k, v)


def _attn_bwd(q, k, v, o, lse, do):
    bsz, t, _ = q.shape
    nq = t // TQ

    def body(q_ref, k_ref, v_ref, o_ref, lse_ref, do_ref, dq_ref, dk_ref, dv_ref, dl_ref):
        j = nq - 1 - pl.program_id(2)

        @pl.when(pl.program_id(2) == 0)
        def _():
            def prep(i, _):
                at = pl.ds(pl.multiple_of(i * TQ, TQ), TQ)
                for hh in range(2):
                    lanes = slice(LANES * hh, LANES * (hh + 1))
                    dl_ref[hh, at, :] = jnp.broadcast_to(
                        jnp.sum(do_ref[0, at, lanes] * o_ref[0, at, lanes], axis=1, keepdims=True), (TQ, LANES))
                return 0

            lax.fori_loop(0, nq, prep, 0)
            dq_ref[0] = jnp.zeros((t, 512), F32)

        def q_tile(i, carry, diagonal):
            atq = pl.ds(pl.multiple_of(i * TQ, TQ), TQ)
            out = []
            for hh in range(2):
                dk, dv = carry[hh]
                wide, narrow = slice(256 * hh, 256 * (hh + 1)), slice(LANES * hh, LANES * (hh + 1))
                qt, kt, vt = q_ref[0, atq, wide], k_ref[0, :, wide], v_ref[0, :, narrow]
                dob = do_ref[0, atq, narrow].astype(BF16)
                s = _dot_nt(qt, kt) * SCALE
                if diagonal:
                    s = jnp.where(lax.broadcasted_iota(jnp.int32, (TQ, TQ), 1)
                                  <= lax.broadcasted_iota(jnp.int32, (TQ, TQ), 0), s, -1e30)
                p = jnp.exp(s - lse_ref[0, hh, atq, :][:, 0:1])
                dv = dv + _dot_tn(p.astype(BF16), dob)
                dp = _dot_nt(dob, vt)
                ds = (p * (dp - dl_ref[hh, atq, :][:, 0:1]) * SCALE).astype(BF16)
                dk = dk + _dot_tn(ds, qt)
                dq_ref[0, atq, wide] += _dot(ds, kt)
                out.append((dk, dv))
            return tuple(out)

        zero = (jnp.zeros((TQ, 256), F32), jnp.zeros((TQ, LANES), F32))
        first = q_tile(j, (zero, zero), True)
        done = lax.fori_loop(j + 1, nq, lambda i, carry: q_tile(i, carry, False), first)
        for hh, (dk, dv) in enumerate(done):
            dk_ref[0, :, 256 * hh:256 * (hh + 1)] = dk
            dv_ref[0, :, LANES * hh:LANES * (hh + 1)] = dv

    whole = lambda c: pl.BlockSpec((1, t, c), lambda b, h, j: (b, 0, h))
    tile = lambda c: pl.BlockSpec((1, TQ, c), lambda b, h, j: (b, nq - 1 - j, h))
    return pl.pallas_call(
        body, name="attn_bwd", grid=(bsz, HEADS // 2, nq),
        out_shape=[jax.ShapeDtypeStruct((bsz, t, 1024), F32), jax.ShapeDtypeStruct((bsz, t, 1024), F32),
                   jax.ShapeDtypeStruct((bsz, t, 512), F32)],
        in_specs=[whole(512), tile(512), tile(256), whole(256),
                  pl.BlockSpec((1, 2, t, LANES), lambda b, h, j: (b, h, 0, 0)), whole(256)],
        out_specs=[whole(512), tile(512), tile(256)],
        scratch_shapes=[pltpu.VMEM((2, t, LANES), F32)],
        compiler_params=_cparams(("parallel", "parallel", "arbitrary")),
    )(q, k, v, o, lse, do)


RW_HEADS = 8
CH = 32


def _lane_split(bsz):
    vs = LANES // (bsz * RW_HEADS)
    return vs, 64 // vs


def _gather_matrix(bsz):
    group = bsz * RW_HEADS
    vs = LANES // group
    half = (RW_HEADS // 2) * bsz * SPREAD_STEPS
    p = np.zeros((SPREAD_STEPS // vs * LANES, 2 * half), np.float32)
    for g2 in range(SPREAD_STEPS // vs):
        for j in range(vs):
            for b in range(bsz):
                for h in range(RW_HEADS):
                    hp, hpar = h // 2, h % 2
                    p[g2 * LANES + j * group + b * RW_HEADS + h,
                      hpar * half + (hp * bsz + b) * SPREAD_STEPS + g2 * vs + j] = 1.0
    return jnp.asarray(np.concatenate([p] * 3, axis=0), BF16)


def _gather_k(ys, bsz):
    vs = LANES // (bsz * RW_HEADS)
    assert (RW_HEADS // 2) * bsz * SPREAD_STEPS == LANES, "the transposed tile must be 128 lanes wide"
    tg = ys[0].shape[0]
    n = len(ys)
    ngrp = GATHER_BLOCK // SPREAD_STEPS
    per = SPREAD_STEPS // vs

    def body(*refs):
        pm = refs[n][...]
        for y_ref, o_ref in zip(refs[:n], refs[n + 1:]):
            lhs = jnp.concatenate(
                [jnp.concatenate(_split3(jnp.concatenate([y_ref[per * m + g2] for g2 in range(per)], axis=1)), axis=1)
                 for m in range(ngrp)], axis=0)
            a = _dot(lhs, pm)
            for m in range(ngrp):
                am = a[64 * m:64 * (m + 1)]
                bt = jnp.concatenate([am[:, 0:LANES], am[:, LANES:2 * LANES]], axis=0).T
                for hp in range(RW_HEADS // 2):
                    for b in range(bsz):
                        at = (hp * bsz + b) * SPREAD_STEPS
                        o_ref[b, SPREAD_STEPS * m:SPREAD_STEPS * (m + 1), LANES * hp:LANES * (hp + 1)] = \
                            bt[at:at + SPREAD_STEPS]

    pm = _gather_matrix(bsz)
    return pl.pallas_call(
        body, name="wkv_gather", grid=(tg * vs // GATHER_BLOCK,),
        out_shape=[jax.ShapeDtypeStruct((bsz, tg * vs, RW), F32)] * n,
        in_specs=[pl.BlockSpec((GATHER_BLOCK // vs, 64, LANES), lambda i: (i, 0, 0))] * n + [_full(pm.shape)],
        out_specs=[pl.BlockSpec((bsz, GATHER_BLOCK, RW), lambda i: (0, i, 0))] * n,
        compiler_params=_cparams(("parallel",)),
    )(*ys, pm)


def _to_v(x):
    bsz, t, _ = x.shape
    vs, vq = _lane_split(bsz)
    return jnp.transpose(x.reshape(bsz, t, RW_HEADS, vs, vq), (1, 4, 3, 0, 2)).reshape(t, vq, LANES)


def _from_v(y, bsz):
    t = y.shape[0]
    vs, vq = _lane_split(bsz)
    return jnp.transpose(y.reshape(t, vq, vs, bsz, RW_HEADS), (3, 0, 4, 2, 1)).reshape(bsz, t, RW)


def _ksum(a):
    return jnp.sum(a, axis=0, keepdims=True)


def _fold(a, group):
    sh = LANES // 2
    while sh >= group:
        a = a + pltpu.roll(a, sh, 1)
        sh //= 2
    return a


def _lane_group(shape, group):
    return lax.broadcasted_iota(jnp.int32, shape, 1) // group


SPREAD_STEPS = 8
SPREAD_BLOCK = 64
GATHER_BLOCK = 128


def _spread_matrix(bsz):
    group = bsz * RW_HEADS
    vs = LANES // group
    rows = (RW_HEADS // 2) * bsz * SPREAD_STEPS
    q = np.zeros((2, rows, SPREAD_STEPS * LANES), np.float32)
    for hpar in range(2):
        for hp in range(RW_HEADS // 2):
            for b in range(bsz):
                for st in range(SPREAD_STEPS):
                    row = (hp * bsz + b) * SPREAD_STEPS + st
                    for s in range(vs):
                        q[hpar, row, st * LANES + s * group + b * RW_HEADS + 2 * hp + hpar] = 1.0
    return jnp.asarray(np.concatenate([q[0], q[1]] * 3, axis=0), BF16)


def _spread_k(xs):
    bsz, t, _ = xs[0].shape
    assert (RW_HEADS // 2) * bsz * SPREAD_STEPS == LANES, "the transposed tile must be 128 lanes wide"
    n = len(xs)
    ngrp = SPREAD_BLOCK // SPREAD_STEPS

    def body(*refs):
        qm = refs[n][...]
        for x_ref, o_ref in zip(refs[:n], refs[n + 1:]):
            cols = [[] for _ in range(6)]
            for m in range(ngrp):
                at = slice(SPREAD_STEPS * m, SPREAD_STEPS * (m + 1))
                x8 = jnp.concatenate([x_ref[b, at, LANES * hp:LANES * (hp + 1)]
                                      for hp in range(RW_HEADS // 2) for b in range(bsz)], axis=0)
                for pi, piece in enumerate(_split3(x8.T)):
                    cols[2 * pi].append(piece[0:64])
                    cols[2 * pi + 1].append(piece[64:128])
            lhs = jnp.concatenate([jnp.concatenate(c, axis=0) for c in cols], axis=1)
            y = _dot(lhs, qm)
            for m in range(ngrp):
                for st in range(SPREAD_STEPS):
                    o_ref[SPREAD_STEPS * m + st] = y[64 * m:64 * (m + 1), LANES * st:LANES * (st + 1)]

    qm = _spread_matrix(bsz)
    return pl.pallas_call(
        body, name="wkv_spread", grid=(t // SPREAD_BLOCK,),
        out_shape=[jax.ShapeDtypeStruct((t, 64, LANES), F32)] * n,
        in_specs=[pl.BlockSpec((bsz, SPREAD_BLOCK, RW), lambda i: (0, i, 0))] * n + [_full(qm.shape)],
        out_specs=[pl.BlockSpec((SPREAD_BLOCK, 64, LANES), lambda i: (i, 0, 0))] * n,
        compiler_params=_cparams(("parallel",)),
    )(*xs, qm)


def _wkv_fwd(r, w, kp, al, be, v):
    t, vq = v.shape[0], v.shape[1]

    def body(r_ref, w_ref, kp_ref, al_ref, be_ref, v_ref, y_ref, a_ref, u_ref, st_ref):
        @pl.when(pl.program_id(0) == 0)
        def _():
            st_ref[...] = jnp.zeros(st_ref.shape, F32)

        def step(tl, _):
            rv, wv, kv, av, bv = r_ref[tl], w_ref[tl], kp_ref[tl], al_ref[tl], be_ref[tl]
            vals = v_ref[tl]
            yrows, urows = [], []
            for q in range(vq):
                s = st_ref[q]
                u = _ksum(s * av)
                s = s * wv + bv * u + kv * vals[q:q + 1]
                st_ref[q] = s
                a_ref[tl, q] = s
                urows.append(u)
                yrows.append(_ksum(s * rv))
            y_ref[tl] = jnp.concatenate(yrows, axis=0)
            u_ref[tl] = jnp.concatenate(urows, axis=0)
            return 0

        lax.fori_loop(0, CH, step, 0)

    kspec = pl.BlockSpec((CH, 64, LANES), lambda i: (i, 0, 0))
    vspec = pl.BlockSpec((CH, vq, LANES), lambda i: (i, 0, 0))
    vsd = jax.ShapeDtypeStruct((t, vq, LANES), F32)
    return pl.pallas_call(
        body, name="wkv_fwd", grid=(t // CH,),
        out_shape=[vsd, jax.ShapeDtypeStruct((t, vq, 64, LANES), F32), vsd],
        in_specs=[kspec] * 5 + [vspec],
        out_specs=[vspec, pl.BlockSpec((CH, vq, 64, LANES), lambda i: (i, 0, 0, 0)), vspec],
        scratch_shapes=[pltpu.VMEM((vq, 64, LANES), F32)],
        compiler_params=_cparams(("arbitrary",)),
    )(r, w, kp, al, be, v)


def _wkv_bwd(r, w, kp, al, be, v, dy, states, u):
    t, vq = v.shape[0], v.shape[1]
    vs = 64 // vq
    group = LANES // vs
    n = t // CH
    ng = CH // vs

    def body(r_ref, w_ref, kp_ref, al_ref, be_ref, v_ref, dy_ref, u_ref, a_ref, ap_ref,
             dr_ref, dw_ref, dkp_ref, dal_ref, dbe_ref, dv_ref, ds_ref):
        @pl.when(pl.program_id(0) == 0)
        def _():
            ds_ref[...] = jnp.zeros(ds_ref.shape, F32)

        earliest = pl.program_id(0) == n - 1

        def reverse(i, _):
            g = ng - 1 - i
            grp = _lane_group((64, LANES), group)
            outs = None
            for j in reversed(range(vs)):
                tl = g * vs + j
                rv, wv, kv, av, bv = r_ref[tl], w_ref[tl], kp_ref[tl], al_ref[tl], be_ref[tl]
                vals, dys, us = v_ref[tl], dy_ref[tl], u_ref[tl]
                acc = None
                dvrows = []
                for q in range(vq):
                    if j > 0:
                        s_prev = a_ref[tl - 1, q]
                    else:
                        before = jnp.where(earliest, 0.0, ap_ref[0, q])
                        s_prev = jnp.where(g == 0, before, a_ref[jnp.maximum(tl - 1, 0), q])
                    dyq = dys[q:q + 1]
                    ds = ds_ref[q] + rv * dyq
                    c = _ksum(ds * bv)
                    dvrows.append(_ksum(ds * kv))
                    terms = (a_ref[tl, q] * dyq, ds * s_prev, ds * vals[q:q + 1], s_prev * c, ds * us[q:q + 1])
                    acc = terms if acc is None else tuple(a + b for a, b in zip(acc, terms))
                    ds_ref[q] = ds * wv + av * c
                dv_ref[tl] = jnp.concatenate(dvrows, axis=0)
                summed = [_fold(a, group) for a in acc]
                outs = summed if outs is None else [jnp.where(grp == j, f, o) for f, o in zip(summed, outs)]
            for ref, o in zip((dr_ref, dw_ref, dkp_ref, dal_ref, dbe_ref), outs):
                ref[g] = o
            return 0

        lax.fori_loop(0, ng, reverse, 0)

    kspec = pl.BlockSpec((CH, 64, LANES), lambda i: (n - 1 - i, 0, 0))
    gspec = pl.BlockSpec((ng, 64, LANES), lambda i: (n - 1 - i, 0, 0))
    vspec = pl.BlockSpec((CH, vq, LANES), lambda i: (n - 1 - i, 0, 0))
    ksd = jax.ShapeDtypeStruct((t // vs, 64, LANES), F32)
    return pl.pallas_call(
        body, name="wkv_bwd", grid=(n,),
        out_shape=[ksd] * 5 + [jax.ShapeDtypeStruct((t, vq, LANES), F32)],
        in_specs=[kspec] * 5 + [vspec, vspec, vspec,
                                pl.BlockSpec((CH, vq, 64, LANES), lambda i: (n - 1 - i, 0, 0, 0)),
                                pl.BlockSpec((1, vq, 64, LANES), lambda i: (jnp.maximum((n - 1 - i) * CH - 1, 0), 0, 0, 0))],
        out_specs=[gspec] * 5 + [vspec],
        scratch_shapes=[pltpu.VMEM((vq, 64, LANES), F32)],
        compiler_params=_cparams(("arbitrary",)),
    )(r, w, kp, al, be, v, dy, u, states, states)


def _post(x, tgt, pp, o, yw, r, kp, v, ln_g, ln_b, r_k, wo, wot, gpost, bo):
    bsz, t, _ = x.shape
    tt = TT_VPU
    nt = t // tt

    def body(x_ref, tgt_ref, z_ref, o_ref, yw_ref, r_ref, kp_ref, v_ref, lng_ref, lnb_ref, rk_ref, wo_ref, wot_ref,
             gpost_ref, bo_ref,
             dh_ref, dz_ref, dym_ref, dyw_ref, dbon_ref, loss_ref, dwo_ref, dgpost_ref, dlng_ref, dlnb_ref, drk_ref):
        first = (pl.program_id(0) == 0) & (pl.program_id(1) == 0)

        @pl.when(first)
        def _():
            for ref in (loss_ref, dwo_ref, dgpost_ref, dlng_ref, dlnb_ref, drk_ref):
                ref[...] = jnp.zeros(ref.shape, F32)

        bo_m = bo_ref[...]
        seg = lambda a: _seg(a, bo_m)
        rowsum = lambda a: jnp.sum(a, axis=0, keepdims=True)
        ywv, rv, kpv, vv = yw_ref[0], r_ref[0], kp_ref[0], v_ref[0]
        ln_g, r_k = lng_ref[...], rk_ref[...]
        mean = seg(ywv) * (1.0 / 64)
        yc = ywv - mean
        rstd = lax.rsqrt(seg(yc * yc) * (1.0 / 64) + GN_EPS)
        yhat = yc * rstd
        sb = seg(rv * kpv * r_k)
        y_rw = yhat * ln_g + lnb_ref[...] + sb * vv
        z = z_ref[0]
        sig = _sigmoid(z)
        sz = z * sig
        ycat = jnp.concatenate([o_ref[0], y_rw], axis=1)
        ycg = (ycat * sz).astype(BF16)
        out = _dot(ycg, wo_ref[...])
        hn, nx, rstd_o = _rms(out, gpost_ref[...], D)
        err = x_ref[0] + hn - tgt_ref[0]
        loss_ref[...] += jnp.sum(err * err) * (0.5 / D)
        dh = err * (1.0 / D)
        dh_ref[0] = dh
        dout, dgp = _rms_bwd(dh, nx, rstd_o, gpost_ref[...], D)
        dgpost_ref[...] += dgp
        doutb = dout.astype(BF16)
        dwo_ref[...] += _dot_tn(ycg, doutb)
        dycg = _dot(doutb, wot_ref[...])
        dz_ref[0] = dycg * ycat * (sig * (1.0 + z * (1.0 - sig)))
        dycat = dycg * sz
        dym_ref[0] = dycat[:, 0:512]
        dy_rw = dycat[:, 512:1024]
        dlnb_ref[...] += rowsum(dy_rw)
        dlng_ref[...] += rowsum(dy_rw * yhat)
        dyhat = dy_rw * ln_g
        dyw_ref[0] = rstd * (dyhat - seg(dyhat) * (1.0 / 64) - yhat * (seg(dyhat * yhat) * (1.0 / 64)))
        dsb = seg(dy_rw * vv)
        drk_ref[...] += rowsum(dsb * rv * kpv)
        dbon_ref[0, :, 0:512] = dsb * kpv * r_k
        dbon_ref[0, :, 512:1024] = dsb * rv * r_k
        dbon_ref[0, :, 1024:1536] = dy_rw * sb

    tok = lambda c: pl.BlockSpec((1, tt, c), lambda b, i: (b, i, 0))
    full = lambda a: _full(a.shape)
    ins = (x, tgt, pp, o, yw, r, kp, v, ln_g, ln_b, r_k, wo, wot, gpost, bo)
    in_specs = [tok(D), tok(D), tok(1024)] + [tok(512)] * 5 + [full(a) for a in ins[8:]]
    sd = lambda c: jax.ShapeDtypeStruct((bsz, t, c), F32)
    vec = lambda c: jax.ShapeDtypeStruct((1, c), F32)
    out_shape = [sd(D), sd(1024), sd(512), sd(512), sd(1536), jax.ShapeDtypeStruct((8, LANES), F32),
                 jax.ShapeDtypeStruct((1024, 1024), F32), vec(D), vec(512), vec(512), vec(512)]
    out_specs = [tok(D), tok(1024), tok(512), tok(512), tok(1536), _resident((8, LANES)), _resident((1024, 1024)),
                 _resident((1, D)), _resident((1, 512)), _resident((1, 512)), _resident((1, 512))]
    return pl.pallas_call(
        body, name="post", grid=(bsz, nt), out_shape=out_shape, in_specs=in_specs, out_specs=out_specs,
        compiler_params=_cparams(("arbitrary", "arbitrary")),
    )(*ins)


def _pre_bwd_a(pp, pos, invf, cqkv_w, mu, w0, w2p, w2pt, a0, a2p, a2pt, k_k, k_a, bo,
               dq, dk, dva, dwkv, dbon):
    gq, wuqt, gkv, wukvt = cqkv_w
    bsz, t, _ = pp.shape
    tt = TT_VPU
    nt = t // tt
    dr_w, dw_w, dkp_w, dv_w, dal_w, dbe_w = dwkv

    def body(pp_ref, pos_ref, invf_ref, gq_ref, wuqt_ref, gkv_ref, wukvt_ref, mu_ref, w0_ref, w2p_ref, w2pt_ref,
             a0_ref, a2p_ref, a2pt_ref, kk_ref, ka_ref, bo_ref, dq_ref, dk_ref, dva_ref,
             dr_ref, dw_ref, dkp_ref, dv_ref, dal_ref, dbe_ref, dbon_ref,
             da_ref, dwuq_ref, dwukv_ref, dw2p_ref, da2p_ref, dgq_ref, dgkv_ref, dmu_ref, dw0_ref, da0_ref,
             dkk_ref, dka_ref, carry):
        i = pl.program_id(1)
        first = (pl.program_id(0) == 0) & (i == 0)

        @pl.when(first)
        def _():
            for ref in (dwuq_ref, dwukv_ref, dw2p_ref, da2p_ref, dgq_ref, dgkv_ref, dmu_ref, dw0_ref, da0_ref,
                        dkk_ref, dka_ref):
                ref[...] = jnp.zeros(ref.shape, F32)

        bo_m = bo_ref[...]
        rowsum = lambda a: jnp.sum(a, axis=0, keepdims=True)
        prw = pp_ref[0, :, RW0:DP]

        @pl.when(i == 0)
        def _():
            carry[...] = jnp.zeros(carry.shape, F32)

        ps, sh = _shift_mix(prw, carry[7:8, :], mu_ref[...])
        carry[...] = prw[tt - 8:tt, :]
        k_k, k_a = kk_ref[...], ka_ref[...]
        g = _rw_gates(ps, w0_ref[...], w2p_ref[...], a0_ref[...], a2p_ref[...], k_k, k_a, bo_m)
        a, kk, k = g["a"], g["kk"], g["k"]
        dr = dr_ref[0] + dbon_ref[0, :, 0:512]
        dkp = dkp_ref[0] + dbon_ref[0, :, 512:1024]
        dv = dv_ref[0] + dbon_ref[0, :, 1024:1536]
        dbe = dbe_ref[0]
        dkk = dbe * a - dal_ref[0]
        da = dbe * kk + dkp * k * k_a
        dka_ref[...] += rowsum(dkp * k * (a - 1.0))
        dm = (dkk - kk * _seg(dkk * kk, bo_m)) / g["nrm"]
        dkk_ref[...] += rowsum(dm * k)
        dk_tot = dkp * (1.0 + (a - 1.0) * k_a) + dm * k_k
        dapre = da * a * (1.0 - a)
        da0_ref[...] += rowsum(dapre)
        dapb = dapre.astype(BF16)
        da2p_ref[...] += _dot_tn(g["misc"].astype(BF16), dapb)
        dwpre = dw_ref[0] * g["w"] * (-g["e"]) * _sigmoid(-g["wpre"])
        dw0_ref[...] += rowsum(dwpre)
        dwpb = dwpre.astype(BF16)
        th = g["th"]
        dw2p_ref[...] += _dot_tn(th.astype(BF16), dwpb)
        dmisc = _dot(dapb, a2pt_ref[...]) + _dot(dwpb, w2pt_ref[...]) * (1.0 - th * th)
        ang = pos_ref[0] * invf_ref[...]
        cs, sn = jnp.cos(ang), jnp.sin(ang)
        unrope = lambda gr: gr * cs - _rot(gr * sn)
        lane = lax.broadcasted_iota(jnp.int32, cs.shape, 1)
        dkr = dk_ref[0, :, 128:256]
        for h in range(1, HEADS):
            dkr = dkr + dk_ref[0, :, 256 * h + 128:256 * h + 256]
        dkr = jnp.where(lane < 64, unrope(dkr), 0.0)
        dmisc = dmisc + jnp.concatenate([dkr, jnp.zeros_like(dkr)], axis=1)
        dqp = jnp.concatenate(
            [blk for h in range(HEADS)
             for blk in (dq_ref[0, :, 256 * h:256 * h + 128], unrope(dq_ref[0, :, 256 * h + 128:256 * h + 256]))],
            axis=1).astype(BF16)
        dkvp = jnp.concatenate([dk_ref[0, :, 256 * h:256 * h + 128] for h in range(HEADS)] + [dva_ref[0]],
                               axis=1).astype(BF16)
        cqn, cq_nx, cq_rstd = _rms(pp_ref[0, :, CQ0:CQ0 + 256], gq_ref[...], 256)
        ckvn, ckv_nx, ckv_rstd = _rms(pp_ref[0, :, CKV0:CKV0 + 128], gkv_ref[...], 128)
        dwuq_ref[...] += _dot_tn(cqn.astype(BF16), dqp)
        dwukv_ref[...] += _dot_tn(ckvn.astype(BF16), dkvp)
        dcq, dgq = _rms_bwd(_dot(dqp, wuqt_ref[...]), cq_nx, cq_rstd, gq_ref[...], 256)
        dckv, dgkv = _rms_bwd(_dot(dkvp, wukvt_ref[...]), ckv_nx, ckv_rstd, gkv_ref[...], 128)
        dgq_ref[...] += dgq
        dgkv_ref[...] += dgkv
        dps = jnp.concatenate([dr, dk_tot, dv, dmisc], axis=1)
        dmu_ref[...] += rowsum(dps * (sh - prw))
        da_ref[0, :, 0:256] = dcq
        da_ref[0, :, 256:384] = dckv
        da_ref[0, :, 384:384 + NRW] = dps

    tok = lambda c: pl.BlockSpec((1, tt, c), lambda b, i: (b, i, 0))
    full = lambda a: _full(a.shape)
    ins = (pp, pos, invf, gq, wuqt, gkv, wukvt, mu, w0, w2p, w2pt, a0, a2p, a2pt, k_k, k_a, bo,
           dq, dk, dva, dr_w, dw_w, dkp_w, dv_w, dal_w, dbe_w, dbon)
    in_specs = ([tok(DP), tok(1)] + [full(a) for a in ins[2:17]] + [tok(1024), tok(1024), tok(512)]
                + [tok(512)] * 6 + [tok(1536)])
    shp = lambda *s: jax.ShapeDtypeStruct(s, F32)
    out_shape = [shp(bsz, t, 384 + NRW), shp(256, 1024), shp(128, 1024), shp(256, 512), shp(256, 512),
                 shp(1, 256), shp(1, 128), shp(1, NRW), shp(1, 512), shp(1, 512), shp(1, 512), shp(1, 512)]
    out_specs = [tok(384 + NRW)] + [_resident(s.shape) for s in out_shape[1:]]
    return pl.pallas_call(
        body, name="pre_bwd_a", grid=(bsz, nt), out_shape=out_shape, in_specs=in_specs, out_specs=out_specs,
        scratch_shapes=[pltpu.VMEM((8, NRW), F32)],
        compiler_params=_cparams(("arbitrary", "arbitrary")),
    )(*ins)


def _pre_bwd_b(x, dh, dz, da, mu, wpt, gpre):
    bsz, t, _ = x.shape
    nt = t // TT
    nblk = t // 8

    def body(x_ref, dh_ref, dz_ref, da_ref, nxt_ref, mu_ref, wpt_ref, gpre_ref, gx_ref, dp_ref, dgpre_ref):
        i = pl.program_id(1)
        first = (pl.program_id(0) == 0) & (i == 0)

        @pl.when(first)
        def _():
            dgpre_ref[...] = jnp.zeros(dgpre_ref.shape, F32)

        mu_v = mu_ref[...]
        dps = da_ref[0, :, 384:384 + NRW]
        nxt = jnp.where(i < nt - 1, nxt_ref[0, 0:1, 384:384 + NRW], 0.0)
        row = lax.broadcasted_iota(jnp.int32, dps.shape, 0)
        up = jnp.where(row == TT - 1, nxt, pltpu.roll(dps, TT - 1, 0))
        dprw = dps * (1.0 - mu_v) + up * mu_v
        dp = jnp.concatenate([dz_ref[0], da_ref[0, :, 0:384], dprw], axis=1).astype(BF16)
        dp_ref[0] = dp
        du = _dot(dp, wpt_ref[...])
        _, nx, rstd = _rms(x_ref[0], gpre_ref[...], D)
        dx, dg = _rms_bwd(du, nx, rstd, gpre_ref[...], D)
        dgpre_ref[...] += dg
        gx_ref[0] = dh_ref[0] + dx

    tok = lambda c: pl.BlockSpec((1, TT, c), lambda b, i: (b, i, 0))
    nxt_spec = pl.BlockSpec((1, 8, 384 + NRW), lambda b, i: (b, jnp.minimum((i + 1) * (TT // 8), nblk - 1), 0))
    ins = (x, dh, dz, da, da, mu, wpt, gpre)
    return pl.pallas_call(
        body, name="pre_bwd_b", grid=(bsz, nt),
        out_shape=[jax.ShapeDtypeStruct((bsz, t, D), F32), jax.ShapeDtypeStruct((bsz, t, DP), BF16),
                   jax.ShapeDtypeStruct((1, D), F32)],
        in_specs=[tok(D), tok(D), tok(1024), tok(384 + NRW), nxt_spec, _full(mu.shape), _full(wpt.shape),
                  _full(gpre.shape)],
        out_specs=[tok(D), tok(DP), _resident((1, D))],
        compiler_params=_cparams(("arbitrary", "arbitrary")),
    )(*ins)


def _tn_matmul(a, b, bn, name, bk=512):
    kdim, m = a.shape
    _, n = b.shape
    nk = kdim // bk

    def body(a_ref, b_ref, o_ref):
        @pl.when(pl.program_id(1) == 0)
        def _():
            o_ref[...] = jnp.zeros(o_ref.shape, F32)

        o_ref[...] += _dot_tn(a_ref[...], b_ref[...])

    return pl.pallas_call(
        body, name=name, grid=(n // bn, nk),
        out_shape=jax.ShapeDtypeStruct((m, n), F32),
        in_specs=[pl.BlockSpec((bk, m), lambda j, kk: (kk, 0)), pl.BlockSpec((bk, bn), lambda j, kk: (kk, j))],
        out_specs=pl.BlockSpec((m, bn), lambda j, kk: (0, j)),
        compiler_params=_cparams(("parallel", "arbitrary")),
    )(a, b)


SHARDED = ("w_in", "mla_w_uq", "mla_w_ukv", "rw_w2", "rw_a2", "w_out")
SMALL = ("norm_pre_g", "mla_q_norm_g", "mla_kv_norm_g", "rw_mu", "rw_w0", "rw_a0", "rw_k_k", "rw_k_a", "rw_r_k",
         "rw_ln_g", "rw_ln_b", "norm_post_g")
WEIGHTS = ("norm_pre_g", "w_in", "mla_q_norm_g", "mla_w_uq", "mla_kv_norm_g", "mla_w_ukv", "rw_mu", "rw_w0", "rw_w2",
           "rw_a0", "rw_a2", "rw_k_k", "rw_k_a", "rw_r_k", "rw_ln_g", "rw_ln_b", "w_out", "norm_post_g")


def _unpack_shard(packed, like):
    out, at = {}, 0
    for n, rows in zip(SHARDED[1:5], PACK_ROWS):
        out[n] = packed[at:at + rows].reshape(like[n].shape)
        at += rows
    return out


def _constants():
    bo = np.kron(np.eye(2, dtype=np.float32), np.ones((64, 64), np.float32))
    inv = ROPE_THETA ** (-np.arange(0, 64, 2, dtype=np.float32) / 64)
    invf = np.concatenate([inv, inv, np.zeros(64, np.float32)]).astype(np.float32)[None, :]
    return jnp.asarray(bo, BF16), jnp.asarray(invf)


def kernel(x, positions, norm_pre_g, w_in, mla_q_norm_g, mla_w_uq, mla_kv_norm_g, mla_w_ukv, rw_mu, rw_w0, rw_w2, rw_a0, rw_a2, rw_k_k, rw_k_a, rw_r_k, rw_ln_g, rw_ln_b, w_out, norm_post_g, loss_target, m_norm_pre_g, m_w_in, m_mla_q_norm_g, m_mla_w_uq, m_mla_kv_norm_g, m_mla_w_ukv, m_rw_mu, m_rw_w0, m_rw_w2, m_rw_a0, m_rw_a2, m_rw_k_k, m_rw_k_a, m_rw_r_k, m_rw_ln_g, m_rw_ln_b, m_w_out, m_norm_post_g, v_norm_pre_g, v_w_in, v_mla_q_norm_g, v_mla_w_uq, v_mla_kv_norm_g, v_mla_w_ukv, v_rw_mu, v_rw_w0, v_rw_w2, v_rw_a0, v_rw_a2, v_rw_k_k, v_rw_k_a, v_rw_r_k, v_rw_ln_g, v_rw_ln_b, v_w_out, v_norm_post_g):
    wts = dict(norm_pre_g=norm_pre_g, w_in=w_in, mla_q_norm_g=mla_q_norm_g, mla_w_uq=mla_w_uq,
               mla_kv_norm_g=mla_kv_norm_g, mla_w_ukv=mla_w_ukv, rw_mu=rw_mu, rw_w0=rw_w0, rw_w2=rw_w2, rw_a0=rw_a0,
               rw_a2=rw_a2, rw_k_k=rw_k_k, rw_k_a=rw_k_a, rw_r_k=rw_r_k, rw_ln_g=rw_ln_g, rw_ln_b=rw_ln_b, w_out=w_out,
               norm_post_g=norm_post_g)
    mom_m = dict(norm_pre_g=m_norm_pre_g, w_in=m_w_in, mla_q_norm_g=m_mla_q_norm_g, mla_w_uq=m_mla_w_uq,
                 mla_kv_norm_g=m_mla_kv_norm_g, mla_w_ukv=m_mla_w_ukv, rw_mu=m_rw_mu, rw_w0=m_rw_w0, rw_w2=m_rw_w2,
                 rw_a0=m_rw_a0, rw_a2=m_rw_a2, rw_k_k=m_rw_k_k, rw_k_a=m_rw_k_a, rw_r_k=m_rw_r_k, rw_ln_g=m_rw_ln_g,
                 rw_ln_b=m_rw_ln_b, w_out=m_w_out, norm_post_g=m_norm_post_g)
    mom_v = dict(norm_pre_g=v_norm_pre_g, w_in=v_w_in, mla_q_norm_g=v_mla_q_norm_g, mla_w_uq=v_mla_w_uq,
                 mla_kv_norm_g=v_mla_kv_norm_g, mla_w_ukv=v_mla_w_ukv, rw_mu=v_rw_mu, rw_w0=v_rw_w0, rw_w2=v_rw_w2,
                 rw_a0=v_rw_a0, rw_a2=v_rw_a2, rw_k_k=v_rw_k_k, rw_k_a=v_rw_k_a, rw_r_k=v_rw_r_k, rw_ln_g=v_rw_ln_g,
                 rw_ln_b=v_rw_ln_b, w_out=v_w_out, norm_post_g=v_norm_post_g)
    bsz, t, _ = x.shape
    bo, invf = _constants()

    g_in, g_uq, g_ukv, g_w2, g_a2, g_out = _ag_weights([wts[n][0] for n in SHARDED])
    w_in_f = jnp.transpose(g_in, (1, 0, 2)).reshape(D, D_IN)
    wp = jnp.concatenate([w_in_f[:, 2112:3136], w_in_f[:, 0:384], w_in_f[:, 448:1984], w_in_f[:, 384:448],
                          w_in_f[:, 1984:2112], jnp.zeros((D, 64), BF16)], axis=1)
    wuq = jnp.pad(jnp.transpose(g_uq, (1, 0, 2)).reshape(256, HEADS, 192), ((0, 0), (0, 0), (0, 64))).reshape(256, 1024)
    wukv = jnp.transpose(jnp.transpose(g_ukv, (1, 0, 2)).reshape(128, HEADS, 2, 128), (0, 2, 1, 3)).reshape(128, 1024)
    w2 = jnp.transpose(g_w2, (1, 0, 2)).reshape(64, RW)
    a2 = jnp.transpose(g_a2, (1, 0, 2)).reshape(64, RW)
    w2p = jnp.pad(w2, ((64, 128), (0, 0)))
    a2p = jnp.pad(a2, ((128, 64), (0, 0)))
    wo = g_out.reshape(D, D)
    mu = jnp.concatenate([rw_mu[:, 0:1536], jnp.zeros((1, 64), F32), rw_mu[:, 1536:1664], jnp.zeros((1, 64), F32)],
                         axis=1)
    r_k = rw_r_k.reshape(1, RW)
    pos = positions.astype(F32)[:, :, None]

    (u, pp, q_att, k_att, v_att, r, w, kp, v, al, be) = _pre_fwd(
        x, pos, invf, norm_pre_g, wp, mla_q_norm_g, wuq, mla_kv_norm_g, wukv, mu, rw_w0, w2p, rw_a0, a2p, rw_k_k,
        rw_k_a, bo)
    o, lse = _attn_fwd(q_att, k_att, v_att)
    rw_k = _spread_k([r, w, kp, al, be])
    v_v = _to_v(v)
    yw_v, states, u_v = _wkv_fwd(*rw_k, v_v)
    yw = _from_v(yw_v, bsz)

    (dh, dz, dym, dyw, dbon, loss_acc, d_wo, d_gpost, d_lng, d_lnb, d_rk) = _post(
        x, loss_target, pp, o, yw, r, kp, v, rw_ln_g, rw_ln_b, r_k, wo, wo.T, norm_post_g, bo)

    d_k = _wkv_bwd(*rw_k, v_v, _to_v(dyw), states, u_v)
    dr_w, dw_w, dkp_w, dal_w, dbe_w = _gather_k(d_k[:5], bsz)
    dwkv = (dr_w, dw_w, dkp_w, _from_v(d_k[5], bsz), dal_w, dbe_w)
    dq, dk, dva = _attn_bwd(q_att, k_att, v_att, o, lse, dym)

    (da, d_wuq, d_wukv, d_w2p, d_a2p, d_gq, d_gkv, d_mu, d_w0, d_a0, d_kk, d_ka) = _pre_bwd_a(
        pp, pos, invf, (mla_q_norm_g, wuq.T, mla_kv_norm_g, wukv.T), mu, rw_w0, w2p, w2p.T, rw_a0, a2p, a2p.T,
        rw_k_k, rw_k_a, bo, dq, dk, dva, dwkv, dbon)
    grad_x, dpb, d_gpre = _pre_bwd_b(x, dh, dz, da, mu, wp.T, norm_pre_g)
    d_wp = _tn_matmul(u.reshape(bsz * t, D), dpb.reshape(bsz * t, DP), DP, "dw_in", bk=1024)

    full_g = {
        "w_in": jnp.concatenate([d_wp[:, 1024:1408], d_wp[:, 2944:3008], d_wp[:, 1408:2944], d_wp[:, 3008:3136],
                                 d_wp[:, 0:1024]], axis=1),
        "mla_w_uq": d_wuq.reshape(256, HEADS, 256)[:, :, :192].reshape(256, 768),
        "mla_w_ukv": jnp.transpose(d_wukv.reshape(128, 2, HEADS, 128), (0, 2, 1, 3)).reshape(128, 1024),
        "rw_w2": d_w2p[64:128],
        "rw_a2": d_a2p[128:192],
        "w_out": d_wo,
    }
    small_g = {
        "norm_pre_g": d_gpre, "mla_q_norm_g": d_gq, "mla_kv_norm_g": d_gkv,
        "rw_mu": jnp.concatenate([d_mu[:, 0:1536], d_mu[:, 1600:1728]], axis=1),
        "rw_w0": d_w0, "rw_a0": d_a0, "rw_k_k": d_kk, "rw_k_a": d_ka, "rw_r_k": d_rk, "rw_ln_g": d_lng,
        "rw_ln_b": d_lnb, "norm_post_g": d_gpost,
    }

    def by_shard(g):
        rows, cols = g.shape
        return jnp.transpose(g.reshape(rows, N_SHARD, cols // N_SHARD), (1, 0, 2))

    g_in = by_shard(full_g["w_in"])
    g_out = full_g["w_out"].reshape(N_SHARD, D // N_SHARD, D)
    packed = jnp.concatenate([by_shard(full_g[n]).reshape(N_SHARD, -1, LANES) for n in SHARDED[1:5]], axis=1)
    halves = [a.reshape(N_SHARD, 2, a.shape[1] // 2, a.shape[2]) for a in (g_in, g_out, packed)]
    red_in, red_out, red_rest = _rs_chips(*_rs_pairs(halves))
    g_shard = red_rest.reshape(PACK_REST, LANES)

    flat = lambda a: a.reshape(1, -1)
    g_small = _small_allreduce([flat(small_g[n]) for n in SMALL], loss_acc)
    loss = g_small[SMALL_USED, 0]

    g_sharded = _unpack_shard(g_shard, {n: wts[n][0] for n in SHARDED})
    g_sharded["w_in"] = red_in.reshape(wts["w_in"][0].shape)
    g_sharded["w_out"] = red_out.reshape(wts["w_out"][0].shape)
    sh = _adamw([wts[n][0] for n in SHARDED], [g_sharded[n] for n in SHARDED], [mom_m[n][0] for n in SHARDED],
                [mom_v[n][0] for n in SHARDED], "adamw_sharded")
    sm = _adamw_small([flat(wts[n]) for n in SMALL], g_small, [flat(mom_m[n]) for n in SMALL],
                      [flat(mom_v[n]) for n in SMALL])

    def outputs(sharded, small):
        out = {n: a[None] for n, a in zip(SHARDED, sharded)}
        out.update({n: a.reshape(wts[n].shape) for n, a in zip(SMALL, small)})
        return out

    grads = outputs([g_sharded[n] for n in SHARDED], sm[0])
    deltas, new_m, new_v = (outputs(sh[k], sm[k + 1]) for k in range(3))
    return (loss, grad_x, *[grads[n] for n in WEIGHTS], *[deltas[n] for n in WEIGHTS],
            *[new_m[n] for n in WEIGHTS], *[new_v[n] for n in WEIGHTS])
```

```python
import numpy as np
import jax
import jax.numpy as jnp
from jax import lax
from jax.experimental import pallas as pl
from jax.experimental.pallas import tpu as pltpu

F32, BF16 = jnp.float32, jnp.bfloat16
MESH = pl.DeviceIdType.MESH

D = 1024
HEADS = 4
RW = 512
NORM_EPS = 1e-6
GN_EPS = 64e-5
ROPE_THETA = 10000.0
SCALE = (128 + 64) ** -0.5
D_IN = 3136
LR, B1, B2, ADAM_EPS, WD, STEP = 0.001, 0.9, 0.999, 1e-08, 0.01, 10

Z0, CQ0, CKV0, RW0, DP = 0, 1024, 1280, 1408, 3200
NRW = DP - RW0

LANES = 128
SUBLANES = 8
VMEM_LIMIT = 56 * 1024 * 1024

TT = 512
TT_VPU = 256
TQ = 512

N_SHARD = 4
PACK_ROWS = (256 * 192 // 128, 128 * 256 // 128, 64, 64)
PACK_REST = sum(PACK_ROWS)
SMALL_ROWS = 64
SMALL_USED = 60


def _cparams(sem=None):
    return pltpu.CompilerParams(dimension_semantics=sem, vmem_limit_bytes=VMEM_LIMIT)


def _full(shape):
    n = len(shape)
    return pl.BlockSpec(shape, lambda *_: (0,) * n, pipeline_mode=pl.Buffered(1))


def _resident(shape):
    n = len(shape)
    return pl.BlockSpec(shape, lambda *_: (0,) * n)


def _dot(a, b):
    return jnp.dot(a, b, preferred_element_type=F32)


def _dot_nt(a, b):
    return lax.dot_general(a, b, (((1,), (1,)), ((), ())), preferred_element_type=F32)


def _dot_tn(a, b):
    return lax.dot_general(a, b, (((0,), (0,)), ((), ())), preferred_element_type=F32)


def _split3(x):
    hi = x.astype(BF16)
    r1 = x - hi.astype(F32)
    mid = r1.astype(BF16)
    lo = (r1 - mid.astype(F32)).astype(BF16)
    return hi, mid, lo


def _seg(x, bo):
    rows, nblk = x.shape[0], x.shape[1] // LANES
    pieces = [p for i in range(nblk) for p in _split3(x[:, LANES * i:LANES * (i + 1)])]
    res = _dot(jnp.concatenate(pieces, axis=0), bo)
    parts = [res[(3 * i) * rows:(3 * i + 1) * rows] + res[(3 * i + 1) * rows:(3 * i + 2) * rows]
             + res[(3 * i + 2) * rows:(3 * i + 3) * rows] for i in range(nblk)]
    return parts[0] if nblk == 1 else jnp.concatenate(parts, axis=1)


def _rms(x, g, n):
    rstd = lax.rsqrt(jnp.sum(x * x, axis=-1, keepdims=True) * (1.0 / n) + NORM_EPS)
    nx = x * rstd
    return nx * g, nx, rstd


def _rms_bwd(dy, nx, rstd, g, n):
    dn = dy * g
    dx = rstd * (dn - nx * (jnp.sum(dn * nx, axis=-1, keepdims=True) * (1.0 / n)))
    return dx, jnp.sum(dy * nx, axis=0, keepdims=True)


def _rot(x):
    lane = lax.broadcasted_iota(jnp.int32, x.shape, 1)
    return jnp.where((lane % 64) < 32, -pltpu.roll(x, x.shape[1] - 32, 1), pltpu.roll(x, 32, 1))


def _sigmoid(x):
    return 1.0 / (1.0 + jnp.exp(-x))


def _softplus(x):
    return jnp.maximum(x, 0.0) + jnp.log(1.0 + jnp.exp(-jnp.abs(x)))


def _rw_gates(ps, w0, w2p, a0, a2p, k_k, k_a, bo):
    r, k, v, misc = ps[:, 0:512], ps[:, 512:1024], ps[:, 1024:1536], ps[:, 1536:NRW]
    th = jnp.tanh(misc)
    wpre = w0 + _dot(th.astype(BF16), w2p)
    e = jnp.exp(-_softplus(-wpre) - 0.5)
    w = jnp.exp(-e)
    a = _sigmoid(a0 + _dot(misc.astype(BF16), a2p))
    m = k * k_k
    nrm = jnp.maximum(jnp.sqrt(_seg(m * m, bo)), 1e-12)
    kk = m / nrm
    kp = k * (1.0 + (a - 1.0) * k_a)
    return dict(r=r, k=k, v=v, misc=misc, th=th, wpre=wpre, e=e, w=w, a=a, nrm=nrm, kk=kk, kp=kp)


def _shift_mix(prw, prev_row, mu):
    row = lax.broadcasted_iota(jnp.int32, prw.shape, 0)
    sh = jnp.where(row == 0, prev_row, pltpu.roll(prw, 1, 0))
    return prw + (sh - prw) * mu, sh


def _ag_weights(shards):
    n = len(shards)

    def body(*refs):
        ins, outs = refs[:n], refs[n:2 * n]
        ici_send, ici_recv, d2d_send, d2d_recv = refs[2 * n:2 * n + 4]
        x, y, c = lax.axis_index("x"), lax.axis_index("y"), lax.axis_index("c")
        mine = 2 * x + y
        for w in range(n):
            outs[w][mine] = ins[w][...].astype(BF16)
        flips = ((1, 0), (0, 1), (1, 1))

        def half(w, shard, cc):
            rows = outs[w].shape[1] // 2
            return outs[w].at[shard, pl.ds(pl.multiple_of(cc * rows, 16), rows)]

        def ici(w, k, shard):
            fx, fy = flips[k]
            return pltpu.make_async_remote_copy(
                src_ref=half(w, shard, c), dst_ref=half(w, shard, c),
                send_sem=ici_send.at[w * 3 + k], recv_sem=ici_recv.at[w * 3 + k],
                device_id=(x ^ fx, y ^ fy, c), device_id_type=MESH)

        def d2d(w, k, cc):
            fx, fy = flips[k]
            theirs = 2 * (x ^ fx) + (y ^ fy)
            return pltpu.make_async_remote_copy(
                src_ref=half(w, theirs, cc), dst_ref=half(w, theirs, cc),
                send_sem=d2d_send.at[w * 3 + k], recv_sem=d2d_recv.at[w * 3 + k],
                device_id=(x, y, 1 - c), device_id_type=MESH)

        for w in range(n):
            for k in range(3):
                ici(w, k, mine).start()
        for w in range(n):
            for k in range(3):
                fx, fy = flips[k]
                ici(w, k, 2 * (x ^ fx) + (y ^ fy)).wait_recv()
                d2d(w, k, c).start()
        for w in range(n):
            for k in range(3):
                d2d(w, k, 1 - c).wait_recv()
        for w in range(n):
            for k in range(3):
                ici(w, k, mine).wait_send()
                d2d(w, k, c).wait_send()

    vm = pl.BlockSpec(memory_space=pltpu.VMEM)
    return pl.pallas_call(
        body, name="ag_weights",
        out_shape=[jax.ShapeDtypeStruct((N_SHARD,) + s.shape, BF16) for s in shards],
        in_specs=[vm] * n, out_specs=[vm] * n,
        scratch_shapes=[pltpu.SemaphoreType.DMA((3 * n,))] * 4,
        compiler_params=pltpu.CompilerParams(vmem_limit_bytes=VMEM_LIMIT),
    )(*shards)


def _rs_pairs(halves):
    n = len(halves)

    def body(*refs):
        h_refs, sum_refs, sumb_refs, recvs = (refs[k * n:(k + 1) * n] for k in range(4))
        send_sem, recv_sem = refs[4 * n:]
        x, y, c = lax.axis_index("x"), lax.axis_index("y"), lax.axis_index("c")
        cps = [pltpu.make_async_remote_copy(src_ref=h_refs[i].at[s, 1 - c], dst_ref=recvs[i].at[s],
                                            send_sem=send_sem.at[i * N_SHARD + s], recv_sem=recv_sem.at[i * N_SHARD + s],
                                            device_id=(x, y, 1 - c), device_id_type=MESH)
               for i in range(n) for s in range(N_SHARD)]
        for cp in cps:
            cp.start()
        for i in range(n):
            for s in range(N_SHARD):
                cps[i * N_SHARD + s].wait_recv()
                acc = h_refs[i][s, c] + recvs[i][s]
                sum_refs[i][s] = acc
                sumb_refs[i][s] = acc.astype(BF16)
        for cp in cps:
            cp.wait_send()

    vm = pl.BlockSpec(memory_space=pltpu.VMEM)
    shapes = [(N_SHARD,) + h.shape[2:] for h in halves]
    outs = pl.pallas_call(
        body, name="rs_pairs",
        out_shape=[jax.ShapeDtypeStruct(sh, F32) for sh in shapes] + [jax.ShapeDtypeStruct(sh, BF16) for sh in shapes],
        in_specs=[vm] * n, out_specs=[vm] * (2 * n),
        scratch_shapes=[pltpu.VMEM(sh, F32) for sh in shapes] + [pltpu.SemaphoreType.DMA((n * N_SHARD,)),
                                                                 pltpu.SemaphoreType.DMA((n * N_SHARD,))],
        compiler_params=pltpu.CompilerParams(vmem_limit_bytes=VMEM_LIMIT),
    )(*halves)
    return outs[:n], outs[n:]


def _rs_chips(part_f32, part_bf16):
    n = len(part_f32)

    def body(*refs):
        own_refs, src_refs, out_refs, recvs = (refs[k * n:(k + 1) * n] for k in range(4))
        ici_send, ici_recv, d2d_send, d2d_recv = refs[4 * n:]
        x, y, c = lax.axis_index("x"), lax.axis_index("y"), lax.axis_index("c")
        mine = 2 * x + y
        flips = ((1, 0), (0, 1), (1, 1))
        cps = []
        for i in range(n):
            for k, (fx, fy) in enumerate(flips):
                theirs = 2 * (x ^ fx) + (y ^ fy)
                cps.append(pltpu.make_async_remote_copy(
                    src_ref=src_refs[i].at[theirs], dst_ref=recvs[i].at[k],
                    send_sem=ici_send.at[3 * i + k], recv_sem=ici_recv.at[3 * i + k],
                    device_id=(x ^ fx, y ^ fy, c), device_id_type=MESH))
        for cp in cps:
            cp.start()
        handed = []
        for i in range(n):
            acc = own_refs[i][mine]
            for k in range(3):
                cps[3 * i + k].wait_recv()
                acc = acc + recvs[i][k].astype(F32)
            out_refs[i][c] = acc
            to_sibling = pltpu.make_async_remote_copy(
                src_ref=out_refs[i].at[c], dst_ref=out_refs[i].at[c], send_sem=d2d_send.at[i], recv_sem=d2d_recv.at[i],
                device_id=(x, y, 1 - c), device_id_type=MESH)
            to_sibling.start()
            handed.append(to_sibling)
        for i in range(n):
            pltpu.make_async_remote_copy(
                src_ref=out_refs[i].at[1 - c], dst_ref=out_refs[i].at[1 - c], send_sem=d2d_send.at[i],
                recv_sem=d2d_recv.at[i], device_id=(x, y, 1 - c), device_id_type=MESH).wait_recv()
        for cp in handed + cps:
            cp.wait_send()

    vm = pl.BlockSpec(memory_space=pltpu.VMEM)
    return pl.pallas_call(
        body, name="rs_chips",
        out_shape=[jax.ShapeDtypeStruct((2,) + p.shape[1:], F32) for p in part_f32],
        in_specs=[vm] * (2 * n), out_specs=[vm] * n,
        scratch_shapes=[pltpu.VMEM((3,) + p.shape[1:], BF16) for p in part_bf16]
        + [pltpu.SemaphoreType.DMA((3 * n,)), pltpu.SemaphoreType.DMA((3 * n,)), pltpu.SemaphoreType.DMA((n,)),
           pltpu.SemaphoreType.DMA((n,))],
        compiler_params=pltpu.CompilerParams(vmem_limit_bytes=VMEM_LIMIT),
    )(*part_f32, *part_bf16)


def _small_rows(vecs):
    out, at = [], 0
    for vec in vecs:
        rows = vec.shape[1] // LANES
        out.append((rows, at))
        at += rows
    assert at == SMALL_USED
    return out


def _small_allreduce(vecs, loss_acc):
    n = len(vecs)
    layout = _small_rows(vecs)

    def body(*refs):
        loss_ref, out_ref, stage, recv, send_sems, recv_sems = refs[n:]
        for vec_ref, (rows, at) in zip(refs[:n], layout):
            for j in range(rows):
                stage[at + j:at + j + 1, :] = vec_ref[0:1, LANES * j:LANES * (j + 1)]
        stage[SMALL_USED:SMALL_ROWS, :] = loss_ref[0:SMALL_ROWS - SMALL_USED, :]
        x, y, c = lax.axis_index("x"), lax.axis_index("y"), lax.axis_index("c")
        me = 4 * x + 2 * y + c
        cps = []
        for k in range(1, 8):
            fx, fy, fc = (k >> 2) & 1, (k >> 1) & 1, k & 1
            cps.append(pltpu.make_async_remote_copy(
                src_ref=stage, dst_ref=recv.at[k - 1],
                send_sem=send_sems.at[k - 1], recv_sem=recv_sems.at[k - 1],
                device_id=(x ^ fx, y ^ fy, c ^ fc), device_id_type=MESH))
        for cp in cps:
            cp.start()
        for cp in cps:
            cp.wait()
        acc = jnp.zeros(stage.shape, F32)
        for j in range(8):
            slot = jnp.maximum((me ^ j) - 1, 0)
            acc = acc + jnp.where(me == j, stage[...], recv[slot])
        out_ref[...] = acc

    vm = pl.BlockSpec(memory_space=pltpu.VMEM)
    shape = (SMALL_ROWS, LANES)
    return pl.pallas_call(
        body, name="small_allreduce",
        out_shape=jax.ShapeDtypeStruct(shape, F32),
        in_specs=[vm] * (n + 1), out_specs=vm,
        scratch_shapes=[pltpu.VMEM(shape, F32), pltpu.VMEM((7,) + shape, F32), pltpu.SemaphoreType.DMA((7,)),
                        pltpu.SemaphoreType.DMA((7,))],
    )(*vecs, loss_acc)


def _adamw_small(ws, g_packed, ms, vs):
    n = len(ws)
    layout = _small_rows(ws)

    def body(*refs):
        g_ref = refs[3 * n]
        outs = refs[3 * n + 1:]
        for i, (rows, at) in enumerate(layout):
            w_ref, m_ref, v_ref = refs[i], refs[n + i], refs[2 * n + i]
            go_ref, d_ref, nm_ref, nv_ref = (outs[k * n + i] for k in range(4))
            for j in range(rows):
                lanes = slice(LANES * j, LANES * (j + 1))
                gg = g_ref[at + j:at + j + 1, :]
                nm = B1 * m_ref[0:1, lanes] + (1.0 - B1) * gg
                nv = B2 * v_ref[0:1, lanes] + (1.0 - B2) * (gg * gg)
                m_hat = nm / (1.0 - B1 ** STEP)
                v_hat = nv / (1.0 - B2 ** STEP)
                go_ref[0:1, lanes] = gg
                d_ref[0:1, lanes] = -LR * (m_hat / (jnp.sqrt(v_hat) + ADAM_EPS) + WD * w_ref[0:1, lanes])
                nm_ref[0:1, lanes] = nm
                nv_ref[0:1, lanes] = nv

    vm = pl.BlockSpec(memory_space=pltpu.VMEM)
    sds = [jax.ShapeDtypeStruct(w.shape, F32) for w in ws]
    outs = pl.pallas_call(
        body, name="adamw_small", out_shape=sds * 4, in_specs=[vm] * (3 * n + 1), out_specs=[vm] * (4 * n),
    )(*ws, *ms, *vs, g_packed)
    return outs[:n], outs[n:2 * n], outs[2 * n:3 * n], outs[3 * n:]


ADAM_ROWS = 64


def _adamw(ws, gs, ms, vs, name):
    n = len(ws)

    def body(*refs):
        for i in range(n):
            w_ref, g_ref, m_ref, v_ref = (refs[k * n + i] for k in range(4))
            d_ref, nm_ref, nv_ref = (refs[(4 + k) * n + i] for k in range(3))
            rows = min(ADAM_ROWS, w_ref.shape[0])

            def chunk(r, _):
                at = pl.ds(pl.multiple_of(r * rows, SUBLANES), rows)
                gg = g_ref[at, :]
                nm = B1 * m_ref[at, :] + (1.0 - B1) * gg
                nv = B2 * v_ref[at, :] + (1.0 - B2) * (gg * gg)
                m_hat = nm / (1.0 - B1 ** STEP)
                v_hat = nv / (1.0 - B2 ** STEP)
                d_ref[at, :] = -LR * (m_hat / (jnp.sqrt(v_hat) + ADAM_EPS) + WD * w_ref[at, :])
                nm_ref[at, :] = nm
                nv_ref[at, :] = nv
                return 0

            lax.fori_loop(0, w_ref.shape[0] // rows, chunk, 0)

    vm = pl.BlockSpec(memory_space=pltpu.VMEM)
    sds = [jax.ShapeDtypeStruct(w.shape, F32) for w in ws]
    outs = pl.pallas_call(
        body, name=name, out_shape=sds * 3, in_specs=[vm] * (4 * n), out_specs=[vm] * (3 * n),
        compiler_params=pltpu.CompilerParams(vmem_limit_bytes=VMEM_LIMIT),
    )(*ws, *gs, *ms, *vs)
    return outs[:n], outs[n:2 * n], outs[2 * n:]


def _pre_fwd(x, pos, invf, gpre, wp, gq, wuq, gkv, wukv, mu, w0, w2p, a0, a2p, k_k, k_a, bo):
    bsz, t, _ = x.shape
    nt = t // TT

    def body(x_ref, pos_ref, invf_ref, gpre_ref, wp_ref, gq_ref, wuq_ref, gkv_ref, wukv_ref, mu_ref, w0_ref,
             w2p_ref, a0_ref, a2p_ref, kk_ref, ka_ref, bo_ref,
             u_ref, pz_ref, pr_ref, q_ref, k_ref, v_ref, r_o, w_o, kp_o, vv_o, al_o, be_o, carry):
        i = pl.program_id(1)
        u, _, _ = _rms(x_ref[0], gpre_ref[...], D)
        ub = u.astype(BF16)
        u_ref[0] = ub
        p = _dot(ub, wp_ref[...])
        pz_ref[0] = p[:, Z0:CQ0]
        pr_ref[0] = p[:, CQ0:DP]
        prw = p[:, RW0:DP]

        @pl.when(i == 0)
        def _():
            carry[...] = jnp.zeros(carry.shape, F32)

        ps, _ = _shift_mix(prw, carry[7:8, :], mu_ref[...])
        carry[...] = prw[TT - 8:TT, :]

        g = _rw_gates(ps, w0_ref[...], w2p_ref[...], a0_ref[...], a2p_ref[...], kk_ref[...], ka_ref[...],
                      bo_ref[...])
        r_o[0] = g["r"]
        w_o[0] = g["w"]
        kp_o[0] = g["kp"]
        vv_o[0] = g["v"]
        al_o[0] = -g["kk"]
        be_o[0] = g["kk"] * g["a"]

        cqn, _, _ = _rms(p[:, CQ0:CQ0 + 256], gq_ref[...], 256)
        q = _dot(cqn.astype(BF16), wuq_ref[...])
        ckvn, _, _ = _rms(p[:, CKV0:CKV0 + 128], gkv_ref[...], 128)
        kv = _dot(ckvn.astype(BF16), wukv_ref[...])
        ang = pos_ref[0] * invf_ref[...]
        cs, sn = jnp.cos(ang), jnp.sin(ang)
        lane = lax.broadcasted_iota(jnp.int32, cs.shape, 1)
        kr = ps[:, 1536:1536 + LANES]
        kr = jnp.where(lane < 64, kr * cs + _rot(kr) * sn, 0.0).astype(BF16)
        for h in range(HEADS):
            qr = q[:, 256 * h + 128:256 * h + 256]
            q_ref[0, :, 256 * h:256 * h + 128] = q[:, 256 * h:256 * h + 128].astype(BF16)
            q_ref[0, :, 256 * h + 128:256 * h + 256] = (qr * cs + _rot(qr) * sn).astype(BF16)
            k_ref[0, :, 256 * h:256 * h + 128] = kv[:, 128 * h:128 * h + 128].astype(BF16)
            k_ref[0, :, 256 * h + 128:256 * h + 256] = kr
        v_ref[0] = kv[:, 512:1024].astype(BF16)

    tok = lambda c: pl.BlockSpec((1, TT, c), lambda b, i: (b, i, 0))
    full = lambda a: _full(a.shape)
    ins = (x, pos, invf, gpre, wp, gq, wuq, gkv, wukv, mu, w0, w2p, a0, a2p, k_k, k_a, bo)
    in_specs = [tok(D), tok(1)] + [full(a) for a in ins[2:]]
    sd = lambda c, dt: jax.ShapeDtypeStruct((bsz, t, c), dt)
    out_shape = ([sd(D, BF16), sd(CQ0 - Z0, F32), sd(DP - CQ0, F32), sd(1024, BF16), sd(1024, BF16), sd(512, BF16)]
                 + [sd(RW, F32)] * 6)
    out_specs = [tok(D), tok(CQ0 - Z0), tok(DP - CQ0), tok(1024), tok(1024), tok(512)] + [tok(RW)] * 6
    return pl.pallas_call(
        body, name="pre_fwd", grid=(bsz, nt), out_shape=out_shape, in_specs=in_specs, out_specs=out_specs,
        scratch_shapes=[pltpu.VMEM((8, NRW), F32)],
        compiler_params=_cparams(("arbitrary", "arbitrary")),
    )(*ins)


def _attn_fwd(q, k, v):
    bsz, t, _ = q.shape
    nq = t // TQ

    hps = HEADS

    def body(q_ref, k_ref, v_ref, o_ref, lse_ref):
        i = pl.program_id(2)

        def step(j, carry, diagonal):
            at = pl.ds(pl.multiple_of(j * TQ, TQ), TQ)
            out = []
            for hh in range(hps):
                m, l, acc = carry[hh]
                s = _dot_nt(q_ref[0, :, 256 * hh:256 * (hh + 1)], k_ref[0, at, 256 * hh:256 * (hh + 1)]) * SCALE
                if diagonal:
                    s = jnp.where(lax.broadcasted_iota(jnp.int32, (TQ, TQ), 1)
                                  <= lax.broadcasted_iota(jnp.int32, (TQ, TQ), 0), s, -1e30)
                mn = jnp.maximum(m, jnp.max(s, axis=1, keepdims=True))
                p = jnp.exp(s - mn)
                al = jnp.exp(m - mn)
                l = al * l + jnp.sum(p, axis=1, keepdims=True)
                acc = al * acc + _dot(p.astype(BF16), v_ref[0, at, LANES * hh:LANES * (hh + 1)])
                out.append((mn, l, acc))
            return tuple(out)

        start = (jnp.full((TQ, 1), -1e30, F32), jnp.zeros((TQ, 1), F32), jnp.zeros((TQ, LANES), F32))
        before = lax.fori_loop(0, i, lambda j, carry: step(j, carry, False), (start,) * hps)
        for hh, (m, l, acc) in enumerate(step(i, before, True)):
            o_ref[0, :, LANES * hh:LANES * (hh + 1)] = acc / l
            lse_ref[0, hh] = jnp.broadcast_to(m + jnp.log(l), (TQ, LANES))

    return pl.pallas_call(
        body, name="attn_fwd", grid=(bsz, HEADS // hps, nq),
        out_shape=[jax.ShapeDtypeStruct((bsz, t, 512), F32), jax.ShapeDtypeStruct((bsz, HEADS, t, LANES), F32)],
        in_specs=[pl.BlockSpec((1, TQ, 256 * hps), lambda b, h, i: (b, i, h)),
                  pl.BlockSpec((1, t, 256 * hps), lambda b, h, i: (b, 0, h)),
                  pl.BlockSpec((1, t, LANES * hps), lambda b, h, i: (b, 0, h))],
        out_specs=[pl.BlockSpec((1, TQ, LANES * hps), lambda b, h, i: (b, i, h)),
                   pl.BlockSpec((1, hps, TQ, LANES), lambda b, h, i: (b, h, i, 0))],
        compiler_params=_cparams(("parallel", "parallel", "arbitrary")),
    )(q, k, v)


def _attn_bwd(q, k, v, o, lse, do):
    bsz, t, _ = q.shape
    nq = t // TQ

    def body(q_ref, k_ref, v_ref, o_ref, lse_ref, do_ref, dq_ref, dk_ref, dv_ref, dl_ref):
        j = nq - 1 - pl.program_id(2)

        @pl.when(pl.program_id(2) == 0)
        def _():
            def prep(i, _):
                at = pl.ds(pl.multiple_of(i * TQ, TQ), TQ)
                for hh in range(2):
                    lanes = slice(LANES * hh, LANES * (hh + 1))
                    dl_ref[hh, at, :] = jnp.broadcast_to(
                        jnp.sum(do_ref[0, at, lanes] * o_ref[0, at, lanes], axis=1, keepdims=True), (TQ, LANES))
                return 0

            lax.fori_loop(0, nq, prep, 0)
            dq_ref[0] = jnp.zeros((t, 512), F32)

        def q_tile(i, carry, diagonal):
            atq = pl.ds(pl.multiple_of(i * TQ, TQ), TQ)
            out = []
            for hh in range(2):
                dk, dv = carry[hh]
                wide, narrow = slice(256 * hh, 256 * (hh + 1)), slice(LANES * hh, LANES * (hh + 1))
                qt, kt, vt = q_ref[0, atq, wide], k_ref[0, :, wide], v_ref[0, :, narrow]
                dob = do_ref[0, atq, narrow].astype(BF16)
                s = _dot_nt(qt, kt) * SCALE
                if diagonal:
                    s = jnp.where(lax.broadcasted_iota(jnp.int32, (TQ, TQ), 1)
                                  <= lax.broadcasted_iota(jnp.int32, (TQ, TQ), 0), s, -1e30)
                p = jnp.exp(s - lse_ref[0, hh, atq, :][:, 0:1])
                dv = dv + _dot_tn(p.astype(BF16), dob)
                dp = _dot_nt(dob, vt)
                ds = (p * (dp - dl_ref[hh, atq, :][:, 0:1]) * SCALE).astype(BF16)
                dk = dk + _dot_tn(ds, qt)
                dq_ref[0, atq, wide] += _dot(ds, kt)
                out.append((dk, dv))
            return tuple(out)

        zero = (jnp.zeros((TQ, 256), F32), jnp.zeros((TQ, LANES), F32))
        first = q_tile(j, (zero, zero), True)
        done = lax.fori_loop(j + 1, nq, lambda i, carry: q_tile(i, carry, False), first)
        for hh, (dk, dv) in enumerate(done):
            dk_ref[0, :, 256 * hh:256 * (hh + 1)] = dk
            dv_ref[0, :, LANES * hh:LANES * (hh + 1)] = dv

    whole = lambda c: pl.BlockSpec((1, t, c), lambda b, h, j: (b, 0, h))
    tile = lambda c: pl.BlockSpec((1, TQ, c), lambda b, h, j: (b, nq - 1 - j, h))
    return pl.pallas_call(
        body, name="attn_bwd", grid=(bsz, HEADS // 2, nq),
        out_shape=[jax.ShapeDtypeStruct((bsz, t, 1024), F32), jax.ShapeDtypeStruct((bsz, t, 1024), F32),
                   jax.ShapeDtypeStruct((bsz, t, 512), F32)],
        in_specs=[whole(512), tile(512), tile(256), whole(256),
                  pl.BlockSpec((1, 2, t, LANES), lambda b, h, j: (b, h, 0, 0)), whole(256)],
        out_specs=[whole(512), tile(512), tile(256)],
        scratch_shapes=[pltpu.VMEM((2, t, LANES), F32)],
        compiler_params=_cparams(("parallel", "parallel", "arbitrary")),
    )(q, k, v, o, lse, do)


RW_HEADS = 8
CH = 32


def _lane_split(bsz):
    vs = LANES // (bsz * RW_HEADS)
    return vs, 64 // vs


def _gather_matrix(bsz):
    group = bsz * RW_HEADS
    vs = LANES // group
    half = (RW_HEADS // 2) * bsz * SPREAD_STEPS
    p = np.zeros((SPREAD_STEPS // vs * LANES, 2 * half), np.float32)
    for g2 in range(SPREAD_STEPS // vs):
        for j in range(vs):
            for b in range(bsz):
                for h in range(RW_HEADS):
                    hp, hpar = h // 2, h % 2
                    p[g2 * LANES + j * group + b * RW_HEADS + h,
                      hpar * half + (hp * bsz + b) * SPREAD_STEPS + g2 * vs + j] = 1.0
    return jnp.asarray(np.concatenate([p] * 3, axis=0), BF16)


def _gather_k(ys, bsz):
    vs = LANES // (bsz * RW_HEADS)
    assert (RW_HEADS // 2) * bsz * SPREAD_STEPS == LANES, "the transposed tile must be 128 lanes wide"
    tg = ys[0].shape[0]
    n = len(ys)
    ngrp = GATHER_BLOCK // SPREAD_STEPS
    per = SPREAD_STEPS // vs

    def body(*refs):
        pm = refs[n][...]
        for y_ref, o_ref in zip(refs[:n], refs[n + 1:]):
            lhs = jnp.concatenate(
                [jnp.concatenate(_split3(jnp.concatenate([y_ref[per * m + g2] for g2 in range(per)], axis=1)), axis=1)
                 for m in range(ngrp)], axis=0)
            a = _dot(lhs, pm)
            for m in range(ngrp):
                am = a[64 * m:64 * (m + 1)]
                bt = jnp.concatenate([am[:, 0:LANES], am[:, LANES:2 * LANES]], axis=0).T
                for hp in range(RW_HEADS // 2):
                    for b in range(bsz):
                        at = (hp * bsz + b) * SPREAD_STEPS
                        o_ref[b, SPREAD_STEPS * m:SPREAD_STEPS * (m + 1), LANES * hp:LANES * (hp + 1)] = \
                            bt[at:at + SPREAD_STEPS]

    pm = _gather_matrix(bsz)
    return pl.pallas_call(
        body, name="wkv_gather", grid=(tg * vs // GATHER_BLOCK,),
        out_shape=[jax.ShapeDtypeStruct((bsz, tg * vs, RW), F32)] * n,
        in_specs=[pl.BlockSpec((GATHER_BLOCK // vs, 64, LANES), lambda i: (i, 0, 0))] * n + [_full(pm.shape)],
        out_specs=[pl.BlockSpec((bsz, GATHER_BLOCK, RW), lambda i: (0, i, 0))] * n,
        compiler_params=_cparams(("parallel",)),
    )(*ys, pm)


def _to_v(x):
    bsz, t, _ = x.shape
    vs, vq = _lane_split(bsz)
    return jnp.transpose(x.reshape(bsz, t, RW_HEADS, vs, vq), (1, 4, 3, 0, 2)).reshape(t, vq, LANES)


def _from_v(y, bsz):
    t = y.shape[0]
    vs, vq = _lane_split(bsz)
    return jnp.transpose(y.reshape(t, vq, vs, bsz, RW_HEADS), (3, 0, 4, 2, 1)).reshape(bsz, t, RW)


def _ksum(a):
    return jnp.sum(a, axis=0, keepdims=True)


def _fold(a, group):
    sh = LANES // 2
    while sh >= group:
        a = a + pltpu.roll(a, sh, 1)
        sh //= 2
    return a


def _lane_group(shape, group):
    return lax.broadcasted_iota(jnp.int32, shape, 1) // group


SPREAD_STEPS = 8
SPREAD_BLOCK = 64
GATHER_BLOCK = 128


def _spread_matrix(bsz):
    group = bsz * RW_HEADS
    vs = LANES // group
    rows = (RW_HEADS // 2) * bsz * SPREAD_STEPS
    q = np.zeros((2, rows, SPREAD_STEPS * LANES), np.float32)
    for hpar in range(2):
        for hp in range(RW_HEADS // 2):
            for b in range(bsz):
                for st in range(SPREAD_STEPS):
                    row = (hp * bsz + b) * SPREAD_STEPS + st
                    for s in range(vs):
                        q[hpar, row, st * LANES + s * group + b * RW_HEADS + 2 * hp + hpar] = 1.0
    return jnp.asarray(np.concatenate([q[0], q[1]] * 3, axis=0), BF16)


def _spread_k(xs):
    bsz, t, _ = xs[0].shape
    assert (RW_HEADS // 2) * bsz * SPREAD_STEPS == LANES, "the transposed tile must be 128 lanes wide"
    n = len(xs)
    ngrp = SPREAD_BLOCK // SPREAD_STEPS

    def body(*refs):
        qm = refs[n][...]
        for x_ref, o_ref in zip(refs[:n], refs[n + 1:]):
            cols = [[] for _ in range(6)]
            for m in range(ngrp):
                at = slice(SPREAD_STEPS * m, SPREAD_STEPS * (m + 1))
                x8 = jnp.concatenate([x_ref[b, at, LANES * hp:LANES * (hp + 1)]
                                      for hp in range(RW_HEADS // 2) for b in range(bsz)], axis=0)
                for pi, piece in enumerate(_split3(x8.T)):
                    cols[2 * pi].append(piece[0:64])
                    cols[2 * pi + 1].append(piece[64:128])
            lhs = jnp.concatenate([jnp.concatenate(c, axis=0) for c in cols], axis=1)
            y = _dot(lhs, qm)
            for m in range(ngrp):
                for st in range(SPREAD_STEPS):
                    o_ref[SPREAD_STEPS * m + st] = y[64 * m:64 * (m + 1), LANES * st:LANES * (st + 1)]

    qm = _spread_matrix(bsz)
    return pl.pallas_call(
        body, name="wkv_spread", grid=(t // SPREAD_BLOCK,),
        out_shape=[jax.ShapeDtypeStruct((t, 64, LANES), F32)] * n,
        in_specs=[pl.BlockSpec((bsz, SPREAD_BLOCK, RW), lambda i: (0, i, 0))] * n + [_full(qm.shape)],
        out_specs=[pl.BlockSpec((SPREAD_BLOCK, 64, LANES), lambda i: (i, 0, 0))] * n,
        compiler_params=_cparams(("parallel",)),
    )(*xs, qm)


def _wkv_fwd(r, w, kp, al, be, v):
    t, vq = v.shape[0], v.shape[1]

    def body(r_ref, w_ref, kp_ref, al_ref, be_ref, v_ref, y_ref, a_ref, u_ref, st_ref):
        @pl.when(pl.program_id(0) == 0)
        def _():
            st_ref[...] = jnp.zeros(st_ref.shape, F32)

        def step(tl, _):
            rv, wv, kv, av, bv = r_ref[tl], w_ref[tl], kp_ref[tl], al_ref[tl], be_ref[tl]
            vals = v_ref[tl]
            yrows, urows = [], []
            for q in range(vq):
                s = st_ref[q]
                u = _ksum(s * av)
                s = s * wv + bv * u + kv * vals[q:q + 1]
                st_ref[q] = s
                a_ref[tl, q] = s
                urows.append(u)
                yrows.append(_ksum(s * rv))
            y_ref[tl] = jnp.concatenate(yrows, axis=0)
            u_ref[tl] = jnp.concatenate(urows, axis=0)
            return 0

        lax.fori_loop(0, CH, step, 0)

    kspec = pl.BlockSpec((CH, 64, LANES), lambda i: (i, 0, 0))
    vspec = pl.BlockSpec((CH, vq, LANES), lambda i: (i, 0, 0))
    vsd = jax.ShapeDtypeStruct((t, vq, LANES), F32)
    return pl.pallas_call(
        body, name="wkv_fwd", grid=(t // CH,),
        out_shape=[vsd, jax.ShapeDtypeStruct((t, vq, 64, LANES), F32), vsd],
        in_specs=[kspec] * 5 + [vspec],
        out_specs=[vspec, pl.BlockSpec((CH, vq, 64, LANES), lambda i: (i, 0, 0, 0)), vspec],
        scratch_shapes=[pltpu.VMEM((vq, 64, LANES), F32)],
        compiler_params=_cparams(("arbitrary",)),
    )(r, w, kp, al, be, v)


def _wkv_bwd(r, w, kp, al, be, v, dy, states, u):
    t, vq = v.shape[0], v.shape[1]
    vs = 64 // vq
    group = LANES // vs
    n = t // CH
    ng = CH // vs

    def body(r_ref, w_ref, kp_ref, al_ref, be_ref, v_ref, dy_ref, u_ref, a_ref, ap_ref,
             dr_ref, dw_ref, dkp_ref, dal_ref, dbe_ref, dv_ref, ds_ref):
        @pl.when(pl.program_id(0) == 0)
        def _():
            ds_ref[...] = jnp.zeros(ds_ref.shape, F32)

        earliest = pl.program_id(0) == n - 1

        def reverse(i, _):
            g = ng - 1 - i
            grp = _lane_group((64, LANES), group)
            outs = None
            for j in reversed(range(vs)):
                tl = g * vs + j
                rv, wv, kv, av, bv = r_ref[tl], w_ref[tl], kp_ref[tl], al_ref[tl], be_ref[tl]
                vals, dys, us = v_ref[tl], dy_ref[tl], u_ref[tl]
                acc = None
                dvrows = []
                for q in range(vq):
                    if j > 0:
                        s_prev = a_ref[tl - 1, q]
                    else:
                        before = jnp.where(earliest, 0.0, ap_ref[0, q])
                        s_prev = jnp.where(g == 0, before, a_ref[jnp.maximum(tl - 1, 0), q])
                    dyq = dys[q:q + 1]
                    ds = ds_ref[q] + rv * dyq
                    c = _ksum(ds * bv)
                    dvrows.append(_ksum(ds * kv))
                    terms = (a_ref[tl, q] * dyq, ds * s_prev, ds * vals[q:q + 1], s_prev * c, ds * us[q:q + 1])
                    acc = terms if acc is None else tuple(a + b for a, b in zip(acc, terms))
                    ds_ref[q] = ds * wv + av * c
                dv_ref[tl] = jnp.concatenate(dvrows, axis=0)
                summed = [_fold(a, group) for a in acc]
                outs = summed if outs is None else [jnp.where(grp == j, f, o) for f, o in zip(summed, outs)]
            for ref, o in zip((dr_ref, dw_ref, dkp_ref, dal_ref, dbe_ref), outs):
                ref[g] = o
            return 0

        lax.fori_loop(0, ng, reverse, 0)

    kspec = pl.BlockSpec((CH, 64, LANES), lambda i: (n - 1 - i, 0, 0))
    gspec = pl.BlockSpec((ng, 64, LANES), lambda i: (n - 1 - i, 0, 0))
    vspec = pl.BlockSpec((CH, vq, LANES), lambda i: (n - 1 - i, 0, 0))
    ksd = jax.ShapeDtypeStruct((t // vs, 64, LANES), F32)
    return pl.pallas_call(
        body, name="wkv_bwd", grid=(n,),
        out_shape=[ksd] * 5 + [jax.ShapeDtypeStruct((t, vq, LANES), F32)],
        in_specs=[kspec] * 5 + [vspec, vspec, vspec,
                                pl.BlockSpec((CH, vq, 64, LANES), lambda i: (n - 1 - i, 0, 0, 0)),
                                pl.BlockSpec((1, vq, 64, LANES), lambda i: (jnp.maximum((n - 1 - i) * CH - 1, 0), 0, 0, 0))],
        out_specs=[gspec] * 5 + [vspec],
        scratch_shapes=[pltpu.VMEM((vq, 64, LANES), F32)],
        compiler_params=_cparams(("arbitrary",)),
    )(r, w, kp, al, be, v, dy, u, states, states)


def _post(x, tgt, pp, o, yw, r, kp, v, ln_g, ln_b, r_k, wo, wot, gpost, bo):
    bsz, t, _ = x.shape
    tt = TT_VPU
    nt = t // tt

    def body(x_ref, tgt_ref, z_ref, o_ref, yw_ref, r_ref, kp_ref, v_ref, lng_ref, lnb_ref, rk_ref, wo_ref, wot_ref,
             gpost_ref, bo_ref,
             dh_ref, dz_ref, dym_ref, dyw_ref, dbon_ref, loss_ref, dwo_ref, dgpost_ref, dlng_ref, dlnb_ref, drk_ref):
        first = (pl.program_id(0) == 0) & (pl.program_id(1) == 0)

        @pl.when(first)
        def _():
            for ref in (loss_ref, dwo_ref, dgpost_ref, dlng_ref, dlnb_ref, drk_ref):
                ref[...] = jnp.zeros(ref.shape, F32)

        bo_m = bo_ref[...]
        seg = lambda a: _seg(a, bo_m)
        rowsum = lambda a: jnp.sum(a, axis=0, keepdims=True)
        ywv, rv, kpv, vv = yw_ref[0], r_ref[0], kp_ref[0], v_ref[0]
        ln_g, r_k = lng_ref[...], rk_ref[...]
        mean = seg(ywv) * (1.0 / 64)
        yc = ywv - mean
        rstd = lax.rsqrt(seg(yc * yc) * (1.0 / 64) + GN_EPS)
        yhat = yc * rstd
        sb = seg(rv * kpv * r_k)
        y_rw = yhat * ln_g + lnb_ref[...] + sb * vv
        z = z_ref[0]
        sig = _sigmoid(z)
        sz = z * sig
        ycat = jnp.concatenate([o_ref[0], y_rw], axis=1)
        ycg = (ycat * sz).astype(BF16)
        out = _dot(ycg, wo_ref[...])
        hn, nx, rstd_o = _rms(out, gpost_ref[...], D)
        err = x_ref[0] + hn - tgt_ref[0]
        loss_ref[...] += jnp.sum(err * err) * (0.5 / D)
        dh = err * (1.0 / D)
        dh_ref[0] = dh
        dout, dgp = _rms_bwd(dh, nx, rstd_o, gpost_ref[...], D)
        dgpost_ref[...] += dgp
        doutb = dout.astype(BF16)
        dwo_ref[...] += _dot_tn(ycg, doutb)
        dycg = _dot(doutb, wot_ref[...])
        dz_ref[0] = dycg * ycat * (sig * (1.0 + z * (1.0 - sig)))
        dycat = dycg * sz
        dym_ref[0] = dycat[:, 0:512]
        dy_rw = dycat[:, 512:1024]
        dlnb_ref[...] += rowsum(dy_rw)
        dlng_ref[...] += rowsum(dy_rw * yhat)
        dyhat = dy_rw * ln_g
        dyw_ref[0] = rstd * (dyhat - seg(dyhat) * (1.0 / 64) - yhat * (seg(dyhat * yhat) * (1.0 / 64)))
        dsb = seg(dy_rw * vv)
        drk_ref[...] += rowsum(dsb * rv * kpv)
        dbon_ref[0, :, 0:512] = dsb * kpv * r_k
        dbon_ref[0, :, 512:1024] = dsb * rv * r_k
        dbon_ref[0, :, 1024:1536] = dy_rw * sb

    tok = lambda c: pl.BlockSpec((1, tt, c), lambda b, i: (b, i, 0))
    full = lambda a: _full(a.shape)
    ins = (x, tgt, pp, o, yw, r, kp, v, ln_g, ln_b, r_k, wo, wot, gpost, bo)
    in_specs = [tok(D), tok(D), tok(1024)] + [tok(512)] * 5 + [full(a) for a in ins[8:]]
    sd = lambda c: jax.ShapeDtypeStruct((bsz, t, c), F32)
    vec = lambda c: jax.ShapeDtypeStruct((1, c), F32)
    out_shape = [sd(D), sd(1024), sd(512), sd(512), sd(1536), jax.ShapeDtypeStruct((8, LANES), F32),
                 jax.ShapeDtypeStruct((1024, 1024), F32), vec(D), vec(512), vec(512), vec(512)]
    out_specs = [tok(D), tok(1024), tok(512), tok(512), tok(1536), _resident((8, LANES)), _resident((1024, 1024)),
                 _resident((1, D)), _resident((1, 512)), _resident((1, 512)), _resident((1, 512))]
    return pl.pallas_call(
        body, name="post", grid=(bsz, nt), out_shape=out_shape, in_specs=in_specs, out_specs=out_specs,
        compiler_params=_cparams(("arbitrary", "arbitrary")),
    )(*ins)


def _pre_bwd_a(pp, pos, invf, cqkv_w, mu, w0, w2p, w2pt, a0, a2p, a2pt, k_k, k_a, bo,
               dq, dk, dva, dwkv, dbon):
    gq, wuqt, gkv, wukvt = cqkv_w
    bsz, t, _ = pp.shape
    tt = TT_VPU
    nt = t // tt
    dr_w, dw_w, dkp_w, dv_w, dal_w, dbe_w = dwkv

    def body(pp_ref, pos_ref, invf_ref, gq_ref, wuqt_ref, gkv_ref, wukvt_ref, mu_ref, w0_ref, w2p_ref, w2pt_ref,
             a0_ref, a2p_ref, a2pt_ref, kk_ref, ka_ref, bo_ref, dq_ref, dk_ref, dva_ref,
             dr_ref, dw_ref, dkp_ref, dv_ref, dal_ref, dbe_ref, dbon_ref,
             da_ref, dwuq_ref, dwukv_ref, dw2p_ref, da2p_ref, dgq_ref, dgkv_ref, dmu_ref, dw0_ref, da0_ref,
             dkk_ref, dka_ref, carry):
        i = pl.program_id(1)
        first = (pl.program_id(0) == 0) & (i == 0)

        @pl.when(first)
        def _():
            for ref in (dwuq_ref, dwukv_ref, dw2p_ref, da2p_ref, dgq_ref, dgkv_ref, dmu_ref, dw0_ref, da0_ref,
                        dkk_ref, dka_ref):
                ref[...] = jnp.zeros(ref.shape, F32)

        bo_m = bo_ref[...]
        rowsum = lambda a: jnp.sum(a, axis=0, keepdims=True)
        prw = pp_ref[0, :, RW0 - CQ0:DP - CQ0]

        @pl.when(i == 0)
        def _():
            carry[...] = jnp.zeros(carry.shape, F32)

        ps, sh = _shift_mix(prw, carry[7:8, :], mu_ref[...])
        carry[...] = prw[tt - 8:tt, :]
        k_k, k_a = kk_ref[...], ka_ref[...]
        g = _rw_gates(ps, w0_ref[...], w2p_ref[...], a0_ref[...], a2p_ref[...], k_k, k_a, bo_m)
        a, kk, k = g["a"], g["kk"], g["k"]
        dr = dr_ref[0] + dbon_ref[0, :, 0:512]
        dkp = dkp_ref[0] + dbon_ref[0, :, 512:1024]
        dv = dv_ref[0] + dbon_ref[0, :, 1024:1536]
        dbe = dbe_ref[0]
        dkk = dbe * a - dal_ref[0]
        da = dbe * kk + dkp * k * k_a
        dka_ref[...] += rowsum(dkp * k * (a - 1.0))
        dm = (dkk - kk * _seg(dkk * kk, bo_m)) / g["nrm"]
        dkk_ref[...] += rowsum(dm * k)
        dk_tot = dkp * (1.0 + (a - 1.0) * k_a) + dm * k_k
        dapre = da * a * (1.0 - a)
        da0_ref[...] += rowsum(dapre)
        dapb = dapre.astype(BF16)
        da2p_ref[...] += _dot_tn(g["misc"].astype(BF16), dapb)
        dwpre = dw_ref[0] * g["w"] * (-g["e"]) * _sigmoid(-g["wpre"])
        dw0_ref[...] += rowsum(dwpre)
        dwpb = dwpre.astype(BF16)
        th = g["th"]
        dw2p_ref[...] += _dot_tn(th.astype(BF16), dwpb)
        dmisc = _dot(dapb, a2pt_ref[...]) + _dot(dwpb, w2pt_ref[...]) * (1.0 - th * th)
        ang = pos_ref[0] * invf_ref[...]
        cs, sn = jnp.cos(ang), jnp.sin(ang)
        unrope = lambda gr: gr * cs - _rot(gr * sn)
        lane = lax.broadcasted_iota(jnp.int32, cs.shape, 1)
        dkr = dk_ref[0, :, 128:256]
        for h in range(1, HEADS):
            dkr = dkr + dk_ref[0, :, 256 * h + 128:256 * h + 256]
        dkr = jnp.where(lane < 64, unrope(dkr), 0.0)
        dmisc = dmisc + jnp.concatenate([dkr, jnp.zeros_like(dkr)], axis=1)
        dqp = jnp.concatenate(
            [blk for h in range(HEADS)
             for blk in (dq_ref[0, :, 256 * h:256 * h + 128], unrope(dq_ref[0, :, 256 * h + 128:256 * h + 256]))],
            axis=1).astype(BF16)
        dkvp = jnp.concatenate([dk_ref[0, :, 256 * h:256 * h + 128] for h in range(HEADS)] + [dva_ref[0]],
                               axis=1).astype(BF16)
        cqn, cq_nx, cq_rstd = _rms(pp_ref[0, :, 0:256], gq_ref[...], 256)
        ckvn, ckv_nx, ckv_rstd = _rms(pp_ref[0, :, CKV0 - CQ0:CKV0 - CQ0 + 128], gkv_ref[...], 128)
        dwuq_ref[...] += _dot_tn(cqn.astype(BF16), dqp)
        dwukv_ref[...] += _dot_tn(ckvn.astype(BF16), dkvp)
        dcq, dgq = _rms_bwd(_dot(dqp, wuqt_ref[...]), cq_nx, cq_rstd, gq_ref[...], 256)
        dckv, dgkv = _rms_bwd(_dot(dkvp, wukvt_ref[...]), ckv_nx, ckv_rstd, gkv_ref[...], 128)
        dgq_ref[...] += dgq
        dgkv_ref[...] += dgkv
        dps = jnp.concatenate([dr, dk_tot, dv, dmisc], axis=1)
        dmu_ref[...] += rowsum(dps * (sh - prw))
        da_ref[0, :, 0:256] = dcq
        da_ref[0, :, 256:384] = dckv
        da_ref[0, :, 384:384 + NRW] = dps

    tok = lambda c: pl.BlockSpec((1, tt, c), lambda b, i: (b, i, 0))
    full = lambda a: _full(a.shape)
    ins = (pp, pos, invf, gq, wuqt, gkv, wukvt, mu, w0, w2p, w2pt, a0, a2p, a2pt, k_k, k_a, bo,
           dq, dk, dva, dr_w, dw_w, dkp_w, dv_w, dal_w, dbe_w, dbon)
    in_specs = ([tok(DP - CQ0), tok(1)] + [full(a) for a in ins[2:17]] + [tok(1024), tok(1024), tok(512)]
                + [tok(512)] * 6 + [tok(1536)])
    shp = lambda *s: jax.ShapeDtypeStruct(s, F32)
    out_shape = [shp(bsz, t, 384 + NRW), shp(256, 1024), shp(128, 1024), shp(256, 512), shp(256, 512),
                 shp(1, 256), shp(1, 128), shp(1, NRW), shp(1, 512), shp(1, 512), shp(1, 512), shp(1, 512)]
    out_specs = [tok(384 + NRW)] + [_resident(s.shape) for s in out_shape[1:]]
    return pl.pallas_call(
        body, name="pre_bwd_a", grid=(bsz, nt), out_shape=out_shape, in_specs=in_specs, out_specs=out_specs,
        scratch_shapes=[pltpu.VMEM((8, NRW), F32)],
        compiler_params=_cparams(("arbitrary", "arbitrary")),
    )(*ins)


def _pre_bwd_b(x, dh, dz, da, mu, wpt, gpre):
    bsz, t, _ = x.shape
    nt = t // TT
    nblk = t // 8

    def body(x_ref, dh_ref, dz_ref, da_ref, nxt_ref, mu_ref, wpt_ref, gpre_ref, gx_ref, dp_ref, dgpre_ref):
        i = pl.program_id(1)
        first = (pl.program_id(0) == 0) & (i == 0)

        @pl.when(first)
        def _():
            dgpre_ref[...] = jnp.zeros(dgpre_ref.shape, F32)

        mu_v = mu_ref[...]
        dps = da_ref[0, :, 384:384 + NRW]
        nxt = jnp.where(i < nt - 1, nxt_ref[0, 0:1, 384:384 + NRW], 0.0)
        row = lax.broadcasted_iota(jnp.int32, dps.shape, 0)
        up = jnp.where(row == TT - 1, nxt, pltpu.roll(dps, TT - 1, 0))
        dprw = dps * (1.0 - mu_v) + up * mu_v
        dp = jnp.concatenate([dz_ref[0], da_ref[0, :, 0:384], dprw], axis=1).astype(BF16)
        dp_ref[0] = dp
        du = _dot(dp, wpt_ref[...])
        _, nx, rstd = _rms(x_ref[0], gpre_ref[...], D)
        dx, dg = _rms_bwd(du, nx, rstd, gpre_ref[...], D)
        dgpre_ref[...] += dg
        gx_ref[0] = dh_ref[0] + dx

    tok = lambda c: pl.BlockSpec((1, TT, c), lambda b, i: (b, i, 0))
    nxt_spec = pl.BlockSpec((1, 8, 384 + NRW), lambda b, i: (b, jnp.minimum((i + 1) * (TT // 8), nblk - 1), 0))
    ins = (x, dh, dz, da, da, mu, wpt, gpre)
    return pl.pallas_call(
        body, name="pre_bwd_b", grid=(bsz, nt),
        out_shape=[jax.ShapeDtypeStruct((bsz, t, D), F32), jax.ShapeDtypeStruct((bsz, t, DP), BF16),
                   jax.ShapeDtypeStruct((1, D), F32)],
        in_specs=[tok(D), tok(D), tok(1024), tok(384 + NRW), nxt_spec, _full(mu.shape), _full(wpt.shape),
                  _full(gpre.shape)],
        out_specs=[tok(D), tok(DP), _resident((1, D))],
        compiler_params=_cparams(("arbitrary", "arbitrary")),
    )(*ins)


def _tn_matmul(a, b, bn, name, bk=512):
    kdim, m = a.shape
    _, n = b.shape
    nk = kdim // bk

    def body(a_ref, b_ref, o_ref):
        @pl.when(pl.program_id(1) == 0)
        def _():
            o_ref[...] = jnp.zeros(o_ref.shape, F32)

        o_ref[...] += _dot_tn(a_ref[...], b_ref[...])

    return pl.pallas_call(
        body, name=name, grid=(n // bn, nk),
        out_shape=jax.ShapeDtypeStruct((m, n), F32),
        in_specs=[pl.BlockSpec((bk, m), lambda j, kk: (kk, 0)), pl.BlockSpec((bk, bn), lambda j, kk: (kk, j))],
        out_specs=pl.BlockSpec((m, bn), lambda j, kk: (0, j)),
        compiler_params=_cparams(("parallel", "arbitrary")),
    )(a, b)


SHARDED = ("w_in", "mla_w_uq", "mla_w_ukv", "rw_w2", "rw_a2", "w_out")
SMALL = ("norm_pre_g", "mla_q_norm_g", "mla_kv_norm_g", "rw_mu", "rw_w0", "rw_a0", "rw_k_k", "rw_k_a", "rw_r_k",
         "rw_ln_g", "rw_ln_b", "norm_post_g")
WEIGHTS = ("norm_pre_g", "w_in", "mla_q_norm_g", "mla_w_uq", "mla_kv_norm_g", "mla_w_ukv", "rw_mu", "rw_w0", "rw_w2",
           "rw_a0", "rw_a2", "rw_k_k", "rw_k_a", "rw_r_k", "rw_ln_g", "rw_ln_b", "w_out", "norm_post_g")


def _unpack_shard(packed, like):
    out, at = {}, 0
    for n, rows in zip(SHARDED[1:5], PACK_ROWS):
        out[n] = packed[at:at + rows].reshape(like[n].shape)
        at += rows
    return out


def _constants():
    bo = np.kron(np.eye(2, dtype=np.float32), np.ones((64, 64), np.float32))
    inv = ROPE_THETA ** (-np.arange(0, 64, 2, dtype=np.float32) / 64)
    invf = np.concatenate([inv, inv, np.zeros(64, np.float32)]).astype(np.float32)[None, :]
    return jnp.asarray(bo, BF16), jnp.asarray(invf)


def kernel(x, positions, norm_pre_g, w_in, mla_q_norm_g, mla_w_uq, mla_kv_norm_g, mla_w_ukv, rw_mu, rw_w0, rw_w2, rw_a0, rw_a2, rw_k_k, rw_k_a, rw_r_k, rw_ln_g, rw_ln_b, w_out, norm_post_g, loss_target, m_norm_pre_g, m_w_in, m_mla_q_norm_g, m_mla_w_uq, m_mla_kv_norm_g, m_mla_w_ukv, m_rw_mu, m_rw_w0, m_rw_w2, m_rw_a0, m_rw_a2, m_rw_k_k, m_rw_k_a, m_rw_r_k, m_rw_ln_g, m_rw_ln_b, m_w_out, m_norm_post_g, v_norm_pre_g, v_w_in, v_mla_q_norm_g, v_mla_w_uq, v_mla_kv_norm_g, v_mla_w_ukv, v_rw_mu, v_rw_w0, v_rw_w2, v_rw_a0, v_rw_a2, v_rw_k_k, v_rw_k_a, v_rw_r_k, v_rw_ln_g, v_rw_ln_b, v_w_out, v_norm_post_g):
    wts = dict(norm_pre_g=norm_pre_g, w_in=w_in, mla_q_norm_g=mla_q_norm_g, mla_w_uq=mla_w_uq,
               mla_kv_norm_g=mla_kv_norm_g, mla_w_ukv=mla_w_ukv, rw_mu=rw_mu, rw_w0=rw_w0, rw_w2=rw_w2, rw_a0=rw_a0,
               rw_a2=rw_a2, rw_k_k=rw_k_k, rw_k_a=rw_k_a, rw_r_k=rw_r_k, rw_ln_g=rw_ln_g, rw_ln_b=rw_ln_b, w_out=w_out,
               norm_post_g=norm_post_g)
    mom_m = dict(norm_pre_g=m_norm_pre_g, w_in=m_w_in, mla_q_norm_g=m_mla_q_norm_g, mla_w_uq=m_mla_w_uq,
                 mla_kv_norm_g=m_mla_kv_norm_g, mla_w_ukv=m_mla_w_ukv, rw_mu=m_rw_mu, rw_w0=m_rw_w0, rw_w2=m_rw_w2,
                 rw_a0=m_rw_a0, rw_a2=m_rw_a2, rw_k_k=m_rw_k_k, rw_k_a=m_rw_k_a, rw_r_k=m_rw_r_k, rw_ln_g=m_rw_ln_g,
                 rw_ln_b=m_rw_ln_b, w_out=m_w_out, norm_post_g=m_norm_post_g)
    mom_v = dict(norm_pre_g=v_norm_pre_g, w_in=v_w_in, mla_q_norm_g=v_mla_q_norm_g, mla_w_uq=v_mla_w_uq,
                 mla_kv_norm_g=v_mla_kv_norm_g, mla_w_ukv=v_mla_w_ukv, rw_mu=v_rw_mu, rw_w0=v_rw_w0, rw_w2=v_rw_w2,
                 rw_a0=v_rw_a0, rw_a2=v_rw_a2, rw_k_k=v_rw_k_k, rw_k_a=v_rw_k_a, rw_r_k=v_rw_r_k, rw_ln_g=v_rw_ln_g,
                 rw_ln_b=v_rw_ln_b, w_out=v_w_out, norm_post_g=v_norm_post_g)
    bsz, t, _ = x.shape
    bo, invf = _constants()

    g_in, g_uq, g_ukv, g_w2, g_a2, g_out = _ag_weights([wts[n][0] for n in SHARDED])
    w_in_f = jnp.transpose(g_in, (1, 0, 2)).reshape(D, D_IN)
    wp = jnp.concatenate([w_in_f[:, 2112:3136], w_in_f[:, 0:384], w_in_f[:, 448:1984], w_in_f[:, 384:448],
                          w_in_f[:, 1984:2112], jnp.zeros((D, 64), BF16)], axis=1)
    wuq = jnp.pad(jnp.transpose(g_uq, (1, 0, 2)).reshape(256, HEADS, 192), ((0, 0), (0, 0), (0, 64))).reshape(256, 1024)
    wukv = jnp.transpose(jnp.transpose(g_ukv, (1, 0, 2)).reshape(128, HEADS, 2, 128), (0, 2, 1, 3)).reshape(128, 1024)
    w2 = jnp.transpose(g_w2, (1, 0, 2)).reshape(64, RW)
    a2 = jnp.transpose(g_a2, (1, 0, 2)).reshape(64, RW)
    w2p = jnp.pad(w2, ((64, 128), (0, 0)))
    a2p = jnp.pad(a2, ((128, 64), (0, 0)))
    wo = g_out.reshape(D, D)
    mu = jnp.concatenate([rw_mu[:, 0:1536], jnp.zeros((1, 64), F32), rw_mu[:, 1536:1664], jnp.zeros((1, 64), F32)],
                         axis=1)
    r_k = rw_r_k.reshape(1, RW)
    pos = positions.astype(F32)[:, :, None]

    (u, pz, pr, q_att, k_att, v_att, r, w, kp, v, al, be) = _pre_fwd(
        x, pos, invf, norm_pre_g, wp, mla_q_norm_g, wuq, mla_kv_norm_g, wukv, mu, rw_w0, w2p, rw_a0, a2p, rw_k_k,
        rw_k_a, bo)
    o, lse = _attn_fwd(q_att, k_att, v_att)
    rw_k = _spread_k([r, w, kp, al, be])
    v_v = _to_v(v)
    yw_v, states, u_v = _wkv_fwd(*rw_k, v_v)
    yw = _from_v(yw_v, bsz)

    (dh, dz, dym, dyw, dbon, loss_acc, d_wo, d_gpost, d_lng, d_lnb, d_rk) = _post(
        x, loss_target, pz, o, yw, r, kp, v, rw_ln_g, rw_ln_b, r_k, wo, wo.T, norm_post_g, bo)

    d_k = _wkv_bwd(*rw_k, v_v, _to_v(dyw), states, u_v)
    dr_w, dw_w, dkp_w, dal_w, dbe_w = _gather_k(d_k[:5], bsz)
    dwkv = (dr_w, dw_w, dkp_w, _from_v(d_k[5], bsz), dal_w, dbe_w)
    dq, dk, dva = _attn_bwd(q_att, k_att, v_att, o, lse, dym)

    (da, d_wuq, d_wukv, d_w2p, d_a2p, d_gq, d_gkv, d_mu, d_w0, d_a0, d_kk, d_ka) = _pre_bwd_a(
        pr, pos, invf, (mla_q_norm_g, wuq.T, mla_kv_norm_g, wukv.T), mu, rw_w0, w2p, w2p.T, rw_a0, a2p, a2p.T,
        rw_k_k, rw_k_a, bo, dq, dk, dva, dwkv, dbon)
    grad_x, dpb, d_gpre = _pre_bwd_b(x, dh, dz, da, mu, wp.T, norm_pre_g)
    d_wp = _tn_matmul(u.reshape(bsz * t, D), dpb.reshape(bsz * t, DP), DP, "dw_in", bk=1024)

    full_g = {
        "w_in": jnp.concatenate([d_wp[:, 1024:1408], d_wp[:, 2944:3008], d_wp[:, 1408:2944], d_wp[:, 3008:3136],
                                 d_wp[:, 0:1024]], axis=1),
        "mla_w_uq": d_wuq.reshape(256, HEADS, 256)[:, :, :192].reshape(256, 768),
        "mla_w_ukv": jnp.transpose(d_wukv.reshape(128, 2, HEADS, 128), (0, 2, 1, 3)).reshape(128, 1024),
        "rw_w2": d_w2p[64:128],
        "rw_a2": d_a2p[128:192],
        "w_out": d_wo,
    }
    small_g = {
        "norm_pre_g": d_gpre, "mla_q_norm_g": d_gq, "mla_kv_norm_g": d_gkv,
        "rw_mu": jnp.concatenate([d_mu[:, 0:1536], d_mu[:, 1600:1728]], axis=1),
        "rw_w0": d_w0, "rw_a0": d_a0, "rw_k_k": d_kk, "rw_k_a": d_ka, "rw_r_k": d_rk, "rw_ln_g": d_lng,
        "rw_ln_b": d_lnb, "norm_post_g": d_gpost,
    }

    def by_shard(g):
        rows, cols = g.shape
        return jnp.transpose(g.reshape(rows, N_SHARD, cols // N_SHARD), (1, 0, 2))

    g_in = by_shard(full_g["w_in"])
    g_out = full_g["w_out"].reshape(N_SHARD, D // N_SHARD, D)
    packed = jnp.concatenate([by_shard(full_g[n]).reshape(N_SHARD, -1, LANES) for n in SHARDED[1:5]], axis=1)
    halves = [a.reshape(N_SHARD, 2, a.shape[1] // 2, a.shape[2]) for a in (g_in, g_out, packed)]
    red_in, red_out, red_rest = _rs_chips(*_rs_pairs(halves))
    g_shard = red_rest.reshape(PACK_REST, LANES)

    flat = lambda a: a.reshape(1, -1)
    g_small = _small_allreduce([flat(small_g[n]) for n in SMALL], loss_acc)
    loss = g_small[SMALL_USED, 0]

    g_sharded = _unpack_shard(g_shard, {n: wts[n][0] for n in SHARDED})
    g_sharded["w_in"] = red_in.reshape(wts["w_in"][0].shape)
    g_sharded["w_out"] = red_out.reshape(wts["w_out"][0].shape)
    sh = _adamw([wts[n][0] for n in SHARDED], [g_sharded[n] for n in SHARDED], [mom_m[n][0] for n in SHARDED],
                [mom_v[n][0] for n in SHARDED], "adamw_sharded")
    sm = _adamw_small([flat(wts[n]) for n in SMALL], g_small, [flat(mom_m[n]) for n in SMALL],
                      [flat(mom_v[n]) for n in SMALL])

    def outputs(sharded, small):
        out = {n: a[None] for n, a in zip(SHARDED, sharded)}
        out.update({n: a.reshape(wts[n].shape) for n, a in zip(SMALL, small)})
        return out

    grads = outputs([g_sharded[n] for n in SHARDED], sm[0])
    deltas, new_m, new_v = (outputs(sh[k], sm[k + 1]) for k in range(3))
    return (loss, grad_x, *[grads[n] for n in WEIGHTS], *[deltas[n] for n in WEIGHTS],
            *[new_m[n] for n in WEIGHTS], *[new_v[n] for n in WEIGHTS])
```

```python
import numpy as np
import jax
import jax.numpy as jnp
from jax import lax
from jax.experimental import pallas as pl
from jax.experimental.pallas import tpu as pltpu

F32, BF16 = jnp.float32, jnp.bfloat16
MESH = pl.DeviceIdType.MESH

D = 1024
HEADS = 4
RW = 512
NORM_EPS = 1e-6
GN_EPS = 64e-5
ROPE_THETA = 10000.0
SCALE = (128 + 64) ** -0.5
D_IN = 3136
LR, B1, B2, ADAM_EPS, WD, STEP = 0.001, 0.9, 0.999, 1e-08, 0.01, 10

Z0, CQ0, CKV0, RW0, DP = 0, 1024, 1280, 1408, 3200
NRW = DP - RW0

LANES = 128
SUBLANES = 8
VMEM_LIMIT = 56 * 1024 * 1024

TT = 512
TT_VPU = 256
TQ = 512

N_SHARD = 4
PACK_ROWS = (256 * 192 // 128, 128 * 256 // 128, 64, 64)
PACK_REST = sum(PACK_ROWS)
SMALL_ROWS = 64
SMALL_USED = 60


def _cparams(sem=None):
    return pltpu.CompilerParams(dimension_semantics=sem, vmem_limit_bytes=VMEM_LIMIT)


def _full(shape):
    n = len(shape)
    return pl.BlockSpec(shape, lambda *_: (0,) * n, pipeline_mode=pl.Buffered(1))


def _resident(shape):
    n = len(shape)
    return pl.BlockSpec(shape, lambda *_: (0,) * n)


def _dot(a, b):
    return jnp.dot(a, b, preferred_element_type=F32)


def _dot_nt(a, b):
    return lax.dot_general(a, b, (((1,), (1,)), ((), ())), preferred_element_type=F32)


def _dot_tn(a, b):
    return lax.dot_general(a, b, (((0,), (0,)), ((), ())), preferred_element_type=F32)


def _split3(x):
    hi = x.astype(BF16)
    r1 = x - hi.astype(F32)
    mid = r1.astype(BF16)
    lo = (r1 - mid.astype(F32)).astype(BF16)
    return hi, mid, lo


def _seg(x, bo):
    rows, nblk = x.shape[0], x.shape[1] // LANES
    pieces = [p for i in range(nblk) for p in _split3(x[:, LANES * i:LANES * (i + 1)])]
    res = _dot(jnp.concatenate(pieces, axis=0), bo)
    parts = [res[(3 * i) * rows:(3 * i + 1) * rows] + res[(3 * i + 1) * rows:(3 * i + 2) * rows]
             + res[(3 * i + 2) * rows:(3 * i + 3) * rows] for i in range(nblk)]
    return parts[0] if nblk == 1 else jnp.concatenate(parts, axis=1)


def _rms(x, g, n):
    rstd = lax.rsqrt(jnp.sum(x * x, axis=-1, keepdims=True) * (1.0 / n) + NORM_EPS)
    nx = x * rstd
    return nx * g, nx, rstd


def _rms_bwd(dy, nx, rstd, g, n):
    dn = dy * g
    dx = rstd * (dn - nx * (jnp.sum(dn * nx, axis=-1, keepdims=True) * (1.0 / n)))
    return dx, jnp.sum(dy * nx, axis=0, keepdims=True)


def _rot(x):
    lane = lax.broadcasted_iota(jnp.int32, x.shape, 1)
    return jnp.where((lane % 64) < 32, -pltpu.roll(x, x.shape[1] - 32, 1), pltpu.roll(x, 32, 1))


def _sigmoid(x):
    return 1.0 / (1.0 + jnp.exp(-x))


def _softplus(x):
    return jnp.maximum(x, 0.0) + jnp.log(1.0 + jnp.exp(-jnp.abs(x)))


def _rw_gates(ps, w0, w2p, a0, a2p, k_k, k_a, bo):
    r, k, v, misc = ps[:, 0:512], ps[:, 512:1024], ps[:, 1024:1536], ps[:, 1536:NRW]
    th = jnp.tanh(misc)
    wpre = w0 + _dot(th.astype(BF16), w2p)
    e = jnp.exp(-_softplus(-wpre) - 0.5)
    w = jnp.exp(-e)
    a = _sigmoid(a0 + _dot(misc.astype(BF16), a2p))
    m = k * k_k
    nrm = jnp.maximum(jnp.sqrt(_seg(m * m, bo)), 1e-12)
    kk = m / nrm
    kp = k * (1.0 + (a - 1.0) * k_a)
    return dict(r=r, k=k, v=v, misc=misc, th=th, wpre=wpre, e=e, w=w, a=a, nrm=nrm, kk=kk, kp=kp)


def _shift_mix(prw, prev_row, mu):
    row = lax.broadcasted_iota(jnp.int32, prw.shape, 0)
    sh = jnp.where(row == 0, prev_row, pltpu.roll(prw, 1, 0))
    return prw + (sh - prw) * mu, sh


def _ag_weights(shards):
    n = len(shards)

    def body(*refs):
        ins, outs = refs[:n], refs[n:2 * n]
        ici_send, ici_recv, d2d_send, d2d_recv = refs[2 * n:2 * n + 4]
        x, y, c = lax.axis_index("x"), lax.axis_index("y"), lax.axis_index("c")
        mine = 2 * x + y
        for w in range(n):
            outs[w][mine] = ins[w][...].astype(BF16)
        flips = ((1, 0), (0, 1), (1, 1))

        def half(w, shard, cc):
            rows = outs[w].shape[1] // 2
            return outs[w].at[shard, pl.ds(pl.multiple_of(cc * rows, 16), rows)]

        def ici(w, k, shard):
            fx, fy = flips[k]
            return pltpu.make_async_remote_copy(
                src_ref=half(w, shard, c), dst_ref=half(w, shard, c),
                send_sem=ici_send.at[w * 3 + k], recv_sem=ici_recv.at[w * 3 + k],
                device_id=(x ^ fx, y ^ fy, c), device_id_type=MESH)

        def d2d(w, k, cc):
            fx, fy = flips[k]
            theirs = 2 * (x ^ fx) + (y ^ fy)
            return pltpu.make_async_remote_copy(
                src_ref=half(w, theirs, cc), dst_ref=half(w, theirs, cc),
                send_sem=d2d_send.at[w * 3 + k], recv_sem=d2d_recv.at[w * 3 + k],
                device_id=(x, y, 1 - c), device_id_type=MESH)

        for w in range(n):
            for k in range(3):
                ici(w, k, mine).start()
        for w in range(n):
            for k in range(3):
                fx, fy = flips[k]
                ici(w, k, 2 * (x ^ fx) + (y ^ fy)).wait_recv()
                d2d(w, k, c).start()
        for w in range(n):
            for k in range(3):
                d2d(w, k, 1 - c).wait_recv()
        for w in range(n):
            for k in range(3):
                ici(w, k, mine).wait_send()
                d2d(w, k, c).wait_send()

    vm = pl.BlockSpec(memory_space=pltpu.VMEM)
    return pl.pallas_call(
        body, name="ag_weights",
        out_shape=[jax.ShapeDtypeStruct((N_SHARD,) + s.shape, BF16) for s in shards],
        in_specs=[vm] * n, out_specs=[vm] * n,
        scratch_shapes=[pltpu.SemaphoreType.DMA((3 * n,))] * 4,
        compiler_params=pltpu.CompilerParams(vmem_limit_bytes=VMEM_LIMIT),
    )(*shards)


def _rs_pairs(halves):
    n = len(halves)

    def body(*refs):
        h_refs, sum_refs, sumb_refs, recvs = (refs[k * n:(k + 1) * n] for k in range(4))
        send_sem, recv_sem = refs[4 * n:]
        x, y, c = lax.axis_index("x"), lax.axis_index("y"), lax.axis_index("c")
        cps = [pltpu.make_async_remote_copy(src_ref=h_refs[i].at[s, 1 - c], dst_ref=recvs[i].at[s],
                                            send_sem=send_sem.at[i * N_SHARD + s], recv_sem=recv_sem.at[i * N_SHARD + s],
                                            device_id=(x, y, 1 - c), device_id_type=MESH)
               for i in range(n) for s in range(N_SHARD)]
        for cp in cps:
            cp.start()
        for i in range(n):
            for s in range(N_SHARD):
                cps[i * N_SHARD + s].wait_recv()
                acc = h_refs[i][s, c] + recvs[i][s]
                sum_refs[i][s] = acc
                sumb_refs[i][s] = acc.astype(BF16)
        for cp in cps:
            cp.wait_send()

    vm = pl.BlockSpec(memory_space=pltpu.VMEM)
    shapes = [(N_SHARD,) + h.shape[2:] for h in halves]
    outs = pl.pallas_call(
        body, name="rs_pairs",
        out_shape=[jax.ShapeDtypeStruct(sh, F32) for sh in shapes] + [jax.ShapeDtypeStruct(sh, BF16) for sh in shapes],
        in_specs=[vm] * n, out_specs=[vm] * (2 * n),
        scratch_shapes=[pltpu.VMEM(sh, F32) for sh in shapes] + [pltpu.SemaphoreType.DMA((n * N_SHARD,)),
                                                                 pltpu.SemaphoreType.DMA((n * N_SHARD,))],
        compiler_params=pltpu.CompilerParams(vmem_limit_bytes=VMEM_LIMIT),
    )(*halves)
    return outs[:n], outs[n:]


def _rs_chips(part_f32, part_bf16):
    n = len(part_f32)

    def body(*refs):
        own_refs, src_refs, out_refs, recvs = (refs[k * n:(k + 1) * n] for k in range(4))
        ici_send, ici_recv, d2d_send, d2d_recv = refs[4 * n:]
        x, y, c = lax.axis_index("x"), lax.axis_index("y"), lax.axis_index("c")
        mine = 2 * x + y
        flips = ((1, 0), (0, 1), (1, 1))
        cps = []
        for i in range(n):
            for k, (fx, fy) in enumerate(flips):
                theirs = 2 * (x ^ fx) + (y ^ fy)
                cps.append(pltpu.make_async_remote_copy(
                    src_ref=src_refs[i].at[theirs], dst_ref=recvs[i].at[k],
                    send_sem=ici_send.at[3 * i + k], recv_sem=ici_recv.at[3 * i + k],
                    device_id=(x ^ fx, y ^ fy, c), device_id_type=MESH))
        for cp in cps:
            cp.start()
        handed = []
        for i in range(n):
            acc = own_refs[i][mine]
            for k in range(3):
                cps[3 * i + k].wait_recv()
                acc = acc + recvs[i][k].astype(F32)
            out_refs[i][c] = acc
            to_sibling = pltpu.make_async_remote_copy(
                src_ref=out_refs[i].at[c], dst_ref=out_refs[i].at[c], send_sem=d2d_send.at[i], recv_sem=d2d_recv.at[i],
                device_id=(x, y, 1 - c), device_id_type=MESH)
            to_sibling.start()
            handed.append(to_sibling)
        for i in range(n):
            pltpu.make_async_remote_copy(
                src_ref=out_refs[i].at[1 - c], dst_ref=out_refs[i].at[1 - c], send_sem=d2d_send.at[i],
                recv_sem=d2d_recv.at[i], device_id=(x, y, 1 - c), device_id_type=MESH).wait_recv()
        for cp in handed + cps:
            cp.wait_send()

    vm = pl.BlockSpec(memory_space=pltpu.VMEM)
    return pl.pallas_call(
        body, name="rs_chips",
        out_shape=[jax.ShapeDtypeStruct((2,) + p.shape[1:], F32) for p in part_f32],
        in_specs=[vm] * (2 * n), out_specs=[vm] * n,
        scratch_shapes=[pltpu.VMEM((3,) + p.shape[1:], BF16) for p in part_bf16]
        + [pltpu.SemaphoreType.DMA((3 * n,)), pltpu.SemaphoreType.DMA((3 * n,)), pltpu.SemaphoreType.DMA((n,)),
           pltpu.SemaphoreType.DMA((n,))],
        compiler_params=pltpu.CompilerParams(vmem_limit_bytes=VMEM_LIMIT),
    )(*part_f32, *part_bf16)


def _small_rows(vecs):
    out, at = [], 0
    for vec in vecs:
        rows = vec.shape[1] // LANES
        out.append((rows, at))
        at += rows
    assert at == SMALL_USED
    return out


def _small_allreduce(vecs, loss_acc):
    n = len(vecs)
    layout = _small_rows(vecs)

    def body(*refs):
        loss_ref, out_ref, stage, recv, send_sems, recv_sems = refs[n:]
        for vec_ref, (rows, at) in zip(refs[:n], layout):
            for j in range(rows):
                stage[at + j:at + j + 1, :] = vec_ref[0:1, LANES * j:LANES * (j + 1)]
        stage[SMALL_USED:SMALL_ROWS, :] = loss_ref[0:SMALL_ROWS - SMALL_USED, :]
        x, y, c = lax.axis_index("x"), lax.axis_index("y"), lax.axis_index("c")
        me = 4 * x + 2 * y + c
        cps = []
        for k in range(1, 8):
            fx, fy, fc = (k >> 2) & 1, (k >> 1) & 1, k & 1
            cps.append(pltpu.make_async_remote_copy(
                src_ref=stage, dst_ref=recv.at[k - 1],
                send_sem=send_sems.at[k - 1], recv_sem=recv_sems.at[k - 1],
                device_id=(x ^ fx, y ^ fy, c ^ fc), device_id_type=MESH))
        for cp in cps:
            cp.start()
        for cp in cps:
            cp.wait()
        acc = jnp.zeros(stage.shape, F32)
        for j in range(8):
            slot = jnp.maximum((me ^ j) - 1, 0)
            acc = acc + jnp.where(me == j, stage[...], recv[slot])
        out_ref[...] = acc

    vm = pl.BlockSpec(memory_space=pltpu.VMEM)
    shape = (SMALL_ROWS, LANES)
    return pl.pallas_call(
        body, name="small_allreduce",
        out_shape=jax.ShapeDtypeStruct(shape, F32),
        in_specs=[vm] * (n + 1), out_specs=vm,
        scratch_shapes=[pltpu.VMEM(shape, F32), pltpu.VMEM((7,) + shape, F32), pltpu.SemaphoreType.DMA((7,)),
                        pltpu.SemaphoreType.DMA((7,))],
    )(*vecs, loss_acc)


def _adamw_small(ws, g_packed, ms, vs):
    n = len(ws)
    layout = _small_rows(ws)

    def body(*refs):
        g_ref = refs[3 * n]
        outs = refs[3 * n + 1:]
        for i, (rows, at) in enumerate(layout):
            w_ref, m_ref, v_ref = refs[i], refs[n + i], refs[2 * n + i]
            go_ref, d_ref, nm_ref, nv_ref = (outs[k * n + i] for k in range(4))
            for j in range(rows):
                lanes = slice(LANES * j, LANES * (j + 1))
                gg = g_ref[at + j:at + j + 1, :]
                nm = B1 * m_ref[0:1, lanes] + (1.0 - B1) * gg
                nv = B2 * v_ref[0:1, lanes] + (1.0 - B2) * (gg * gg)
                m_hat = nm / (1.0 - B1 ** STEP)
                v_hat = nv / (1.0 - B2 ** STEP)
                go_ref[0:1, lanes] = gg
                d_ref[0:1, lanes] = -LR * (m_hat / (jnp.sqrt(v_hat) + ADAM_EPS) + WD * w_ref[0:1, lanes])
                nm_ref[0:1, lanes] = nm
                nv_ref[0:1, lanes] = nv

    vm = pl.BlockSpec(memory_space=pltpu.VMEM)
    sds = [jax.ShapeDtypeStruct(w.shape, F32) for w in ws]
    outs = pl.pallas_call(
        body, name="adamw_small", out_shape=sds * 4, in_specs=[vm] * (3 * n + 1), out_specs=[vm] * (4 * n),
    )(*ws, *ms, *vs, g_packed)
    return outs[:n], outs[n:2 * n], outs[2 * n:3 * n], outs[3 * n:]


ADAM_ROWS = 64


def _adamw(ws, gs, ms, vs, name):
    n = len(ws)

    def body(*refs):
        for i in range(n):
            w_ref, g_ref, m_ref, v_ref = (refs[k * n + i] for k in range(4))
            d_ref, nm_ref, nv_ref = (refs[(4 + k) * n + i] for k in range(3))
            rows = min(ADAM_ROWS, w_ref.shape[0])

            def chunk(r, _):
                at = pl.ds(pl.multiple_of(r * rows, SUBLANES), rows)
                gg = g_ref[at, :]
                nm = B1 * m_ref[at, :] + (1.0 - B1) * gg
                nv = B2 * v_ref[at, :] + (1.0 - B2) * (gg * gg)
                m_hat = nm / (1.0 - B1 ** STEP)
                v_hat = nv / (1.0 - B2 ** STEP)
                d_ref[at, :] = -LR * (m_hat / (jnp.sqrt(v_hat) + ADAM_EPS) + WD * w_ref[at, :])
                nm_ref[at, :] = nm
                nv_ref[at, :] = nv
                return 0

            lax.fori_loop(0, w_ref.shape[0] // rows, chunk, 0)

    vm = pl.BlockSpec(memory_space=pltpu.VMEM)
    sds = [jax.ShapeDtypeStruct(w.shape, F32) for w in ws]
    outs = pl.pallas_call(
        body, name=name, out_shape=sds * 3, in_specs=[vm] * (4 * n), out_specs=[vm] * (3 * n),
        compiler_params=pltpu.CompilerParams(vmem_limit_bytes=VMEM_LIMIT),
    )(*ws, *gs, *ms, *vs)
    return outs[:n], outs[n:2 * n], outs[2 * n:]


def _pre_fwd(x, pos, invf, gpre, wp, gq, wuq, gkv, wukv, mu, w0, w2p, a0, a2p, k_k, k_a, bo):
    bsz, t, _ = x.shape
    nt = t // TT

    def body(x_ref, pos_ref, invf_ref, gpre_ref, wp_ref, gq_ref, wuq_ref, gkv_ref, wukv_ref, mu_ref, w0_ref,
             w2p_ref, a0_ref, a2p_ref, kk_ref, ka_ref, bo_ref,
             u_ref, pz_ref, pr_ref, q_ref, k_ref, v_ref, r_o, w_o, kp_o, vv_o, al_o, be_o, carry):
        i = pl.program_id(1)
        u, _, _ = _rms(x_ref[0], gpre_ref[...], D)
        ub = u.astype(BF16)
        u_ref[0] = ub
        p = _dot(ub, wp_ref[...])
        pz_ref[0] = p[:, Z0:CQ0]
        pr_ref[0] = p[:, CQ0:DP]
        prw = p[:, RW0:DP]

        @pl.when(i == 0)
        def _():
            carry[...] = jnp.zeros(carry.shape, F32)

        ps, _ = _shift_mix(prw, carry[7:8, :], mu_ref[...])
        carry[...] = prw[TT - 8:TT, :]

        g = _rw_gates(ps, w0_ref[...], w2p_ref[...], a0_ref[...], a2p_ref[...], kk_ref[...], ka_ref[...],
                      bo_ref[...])
        r_o[0] = g["r"]
        w_o[0] = g["w"]
        kp_o[0] = g["kp"]
        vv_o[0] = g["v"]
        al_o[0] = -g["kk"]
        be_o[0] = g["kk"] * g["a"]

        cqn, _, _ = _rms(p[:, CQ0:CQ0 + 256], gq_ref[...], 256)
        q = _dot(cqn.astype(BF16), wuq_ref[...])
        ckvn, _, _ = _rms(p[:, CKV0:CKV0 + 128], gkv_ref[...], 128)
        kv = _dot(ckvn.astype(BF16), wukv_ref[...])
        ang = pos_ref[0] * invf_ref[...]
        cs, sn = jnp.cos(ang), jnp.sin(ang)
        lane = lax.broadcasted_iota(jnp.int32, cs.shape, 1)
        kr = ps[:, 1536:1536 + LANES]
        kr = jnp.where(lane < 64, kr * cs + _rot(kr) * sn, 0.0).astype(BF16)
        for h in range(HEADS):
            qr = q[:, 256 * h + 128:256 * h + 256]
            q_ref[0, :, 256 * h:256 * h + 128] = q[:, 256 * h:256 * h + 128].astype(BF16)
            q_ref[0, :, 256 * h + 128:256 * h + 256] = (qr * cs + _rot(qr) * sn).astype(BF16)
            k_ref[0, :, 256 * h:256 * h + 128] = kv[:, 128 * h:128 * h + 128].astype(BF16)
            k_ref[0, :, 256 * h + 128:256 * h + 256] = kr
        v_ref[0] = kv[:, 512:1024].astype(BF16)

    tok = lambda c: pl.BlockSpec((1, TT, c), lambda b, i: (b, i, 0))
    full = lambda a: _full(a.shape)
    ins = (x, pos, invf, gpre, wp, gq, wuq, gkv, wukv, mu, w0, w2p, a0, a2p, k_k, k_a, bo)
    in_specs = [tok(D), tok(1)] + [full(a) for a in ins[2:]]
    sd = lambda c, dt: jax.ShapeDtypeStruct((bsz, t, c), dt)
    out_shape = ([sd(D, BF16), sd(CQ0 - Z0, F32), sd(DP - CQ0, F32), sd(1024, BF16), sd(1024, BF16), sd(512, BF16)]
                 + [sd(RW, F32)] * 6)
    out_specs = [tok(D), tok(CQ0 - Z0), tok(DP - CQ0), tok(1024), tok(1024), tok(512)] + [tok(RW)] * 6
    return pl.pallas_call(
        body, name="pre_fwd", grid=(bsz, nt), out_shape=out_shape, in_specs=in_specs, out_specs=out_specs,
        scratch_shapes=[pltpu.VMEM((8, NRW), F32)],
        compiler_params=_cparams(("arbitrary", "arbitrary")),
    )(*ins)


def _attn_fwd(q, k, v):
    bsz, t, _ = q.shape
    nq = t // TQ

    hps = HEADS

    def body(q_ref, k_ref, v_ref, o_ref, lse_ref):
        i = pl.program_id(2)

        def step(j, carry, diagonal):
            at = pl.ds(pl.multiple_of(j * TQ, TQ), TQ)
            out = []
            for hh in range(hps):
                m, l, acc = carry[hh]
                s = _dot_nt(q_ref[0, :, 256 * hh:256 * (hh + 1)], k_ref[0, at, 256 * hh:256 * (hh + 1)]) * SCALE
                if diagonal:
                    s = jnp.where(lax.broadcasted_iota(jnp.int32, (TQ, TQ), 1)
                                  <= lax.broadcasted_iota(jnp.int32, (TQ, TQ), 0), s, -1e30)
                mn = jnp.maximum(m, jnp.max(s, axis=1, keepdims=True))
                p = jnp.exp(s - mn)
                al = jnp.exp(m - mn)
                l = al * l + jnp.sum(p, axis=1, keepdims=True)
                acc = al * acc + _dot(p.astype(BF16), v_ref[0, at, LANES * hh:LANES * (hh + 1)])
                out.append((mn, l, acc))
            return tuple(out)

        start = (jnp.full((TQ, 1), -1e30, F32), jnp.zeros((TQ, 1), F32), jnp.zeros((TQ, LANES), F32))
        before = lax.fori_loop(0, i, lambda j, carry: step(j, carry, False), (start,) * hps)
        for hh, (m, l, acc) in enumerate(step(i, before, True)):
            o_ref[0, :, LANES * hh:LANES * (hh + 1)] = acc / l
            lse_ref[0, hh] = jnp.broadcast_to(m + jnp.log(l), (TQ, LANES))

    return pl.pallas_call(
        body, name="attn_fwd", grid=(bsz, HEADS // hps, nq),
        out_shape=[jax.ShapeDtypeStruct((bsz, t, 512), F32), jax.ShapeDtypeStruct((bsz, HEADS, t, LANES), F32)],
        in_specs=[pl.BlockSpec((1, TQ, 256 * hps), lambda b, h, i: (b, i, h)),
                  pl.BlockSpec((1, t, 256 * hps), lambda b, h, i: (b, 0, h)),
                  pl.BlockSpec((1, t, LANES * hps), lambda b, h, i: (b, 0, h))],
        out_specs=[pl.BlockSpec((1, TQ, LANES * hps), lambda b, h, i: (b, i, h)),
                   pl.BlockSpec((1, hps, TQ, LANES), lambda b, h, i: (b, h, i, 0))],
        compiler_params=_cparams(("parallel", "parallel", "arbitrary")),
    )(q, k, v)


def _attn_bwd(q, k, v, o, lse, do):
    bsz, t, _ = q.shape
    nq = t // TQ

    def body(q_ref, k_ref, v_ref, o_ref, lse_ref, do_ref, dq_ref, dk_ref, dv_ref, dl_ref):
        j = nq - 1 - pl.program_id(2)

        @pl.when(pl.program_id(2) == 0)
        def _():
            def prep(i, _):
                at = pl.ds(pl.multiple_of(i * TQ, TQ), TQ)
                for hh in range(2):
                    lanes = slice(LANES * hh, LANES * (hh + 1))
                    dl_ref[hh, at, :] = jnp.broadcast_to(
                        jnp.sum(do_ref[0, at, lanes] * o_ref[0, at, lanes], axis=1, keepdims=True), (TQ, LANES))
                return 0

            lax.fori_loop(0, nq, prep, 0)
            dq_ref[0] = jnp.zeros((t, 512), F32)

        def q_tile(i, carry, diagonal):
            atq = pl.ds(pl.multiple_of(i * TQ, TQ), TQ)
            out = []
            for hh in range(2):
                dk, dv = carry[hh]
                wide, narrow = slice(256 * hh, 256 * (hh + 1)), slice(LANES * hh, LANES * (hh + 1))
                qt, kt, vt = q_ref[0, atq, wide], k_ref[0, :, wide], v_ref[0, :, narrow]
                dob = do_ref[0, atq, narrow].astype(BF16)
                s = _dot_nt(qt, kt) * SCALE
                if diagonal:
                    s = jnp.where(lax.broadcasted_iota(jnp.int32, (TQ, TQ), 1)
                                  <= lax.broadcasted_iota(jnp.int32, (TQ, TQ), 0), s, -1e30)
                p = jnp.exp(s - lse_ref[0, hh, atq, :][:, 0:1])
                dv = dv + _dot_tn(p.astype(BF16), dob)
                dp = _dot_nt(dob, vt)
                ds = (p * (dp - dl_ref[hh, atq, :][:, 0:1]) * SCALE).astype(BF16)
                dk = dk + _dot_tn(ds, qt)
                dq_ref[0, atq, wide] += _dot(ds, kt)
                out.append((dk, dv))
            return tuple(out)

        zero = (jnp.zeros((TQ, 256), F32), jnp.zeros((TQ, LANES), F32))
        first = q_tile(j, (zero, zero), True)
        done = lax.fori_loop(j + 1, nq, lambda i, carry: q_tile(i, carry, False), first)
        for hh, (dk, dv) in enumerate(done):
            dk_ref[0, :, 256 * hh:256 * (hh + 1)] = dk
            dv_ref[0, :, LANES * hh:LANES * (hh + 1)] = dv

    whole = lambda c: pl.BlockSpec((1, t, c), lambda b, h, j: (b, 0, h))
    tile = lambda c: pl.BlockSpec((1, TQ, c), lambda b, h, j: (b, nq - 1 - j, h))
    return pl.pallas_call(
        body, name="attn_bwd", grid=(bsz, HEADS // 2, nq),
        out_shape=[jax.ShapeDtypeStruct((bsz, t, 1024), F32), jax.ShapeDtypeStruct((bsz, t, 1024), F32),
                   jax.ShapeDtypeStruct((bsz, t, 512), F32)],
        in_specs=[whole(512), tile(512), tile(256), whole(256),
                  pl.BlockSpec((1, 2, t, LANES), lambda b, h, j: (b, h, 0, 0)), whole(256)],
        out_specs=[whole(512), tile(512), tile(256)],
        scratch_shapes=[pltpu.VMEM((2, t, LANES), F32)],
        compiler_params=_cparams(("parallel", "parallel", "arbitrary")),
    )(q, k, v, o, lse, do)


RW_HEADS = 8
CH = 32


def _lane_split(bsz):
    vs = LANES // (bsz * RW_HEADS)
    return vs, 64 // vs


def _gather_matrix(bsz):
    group = bsz * RW_HEADS
    vs = LANES // group
    half = (RW_HEADS // 2) * bsz * SPREAD_STEPS
    p = np.zeros((SPREAD_STEPS // vs * LANES, 2 * half), np.float32)
    for g2 in range(SPREAD_STEPS // vs):
        for j in range(vs):
            for b in range(bsz):
                for h in range(RW_HEADS):
                    hp, hpar = h // 2, h % 2
                    p[g2 * LANES + j * group + b * RW_HEADS + h,
                      hpar * half + (hp * bsz + b) * SPREAD_STEPS + g2 * vs + j] = 1.0
    return jnp.asarray(np.concatenate([p] * 3, axis=0), BF16)


def _gather_k(ys, bsz):
    vs = LANES // (bsz * RW_HEADS)
    assert (RW_HEADS // 2) * bsz * SPREAD_STEPS == LANES, "the transposed tile must be 128 lanes wide"
    tg = ys[0].shape[0]
    n = len(ys)
    ngrp = GATHER_BLOCK // SPREAD_STEPS
    per = SPREAD_STEPS // vs

    def body(*refs):
        pm = refs[n][...]
        for y_ref, o_ref in zip(refs[:n], refs[n + 1:]):
            lhs = jnp.concatenate(
                [jnp.concatenate(_split3(jnp.concatenate([y_ref[per * m + g2] for g2 in range(per)], axis=1)), axis=1)
                 for m in range(ngrp)], axis=0)
            a = _dot(lhs, pm)
            for m in range(ngrp):
                am = a[64 * m:64 * (m + 1)]
                bt = jnp.concatenate([am[:, 0:LANES], am[:, LANES:2 * LANES]], axis=0).T
                for hp in range(RW_HEADS // 2):
                    for b in range(bsz):
                        at = (hp * bsz + b) * SPREAD_STEPS
                        o_ref[b, SPREAD_STEPS * m:SPREAD_STEPS * (m + 1), LANES * hp:LANES * (hp + 1)] = \
                            bt[at:at + SPREAD_STEPS]

    pm = _gather_matrix(bsz)
    return pl.pallas_call(
        body, name="wkv_gather", grid=(tg * vs // GATHER_BLOCK,),
        out_shape=[jax.ShapeDtypeStruct((bsz, tg * vs, RW), F32)] * n,
        in_specs=[pl.BlockSpec((GATHER_BLOCK // vs, 64, LANES), lambda i: (i, 0, 0))] * n + [_full(pm.shape)],
        out_specs=[pl.BlockSpec((bsz, GATHER_BLOCK, RW), lambda i: (0, i, 0))] * n,
        compiler_params=_cparams(("parallel",)),
    )(*ys, pm)


def _to_v(x):
    bsz, t, _ = x.shape
    vs, vq = _lane_split(bsz)
    return jnp.transpose(x.reshape(bsz, t, RW_HEADS, vs, vq), (1, 4, 3, 0, 2)).reshape(t, vq, LANES)


def _from_v(y, bsz):
    t = y.shape[0]
    vs, vq = _lane_split(bsz)
    return jnp.transpose(y.reshape(t, vq, vs, bsz, RW_HEADS), (3, 0, 4, 2, 1)).reshape(bsz, t, RW)


def _ksum(a):
    return jnp.sum(a, axis=0, keepdims=True)


def _fold(a, group):
    sh = LANES // 2
    while sh >= group:
        a = a + pltpu.roll(a, sh, 1)
        sh //= 2
    return a


def _lane_group(shape, group):
    return lax.broadcasted_iota(jnp.int32, shape, 1) // group


SPREAD_STEPS = 8
SPREAD_BLOCK = 64
GATHER_BLOCK = 128


def _spread_matrix(bsz):
    group = bsz * RW_HEADS
    vs = LANES // group
    rows = (RW_HEADS // 2) * bsz * SPREAD_STEPS
    q = np.zeros((2, rows, SPREAD_STEPS * LANES), np.float32)
    for hpar in range(2):
        for hp in range(RW_HEADS // 2):
            for b in range(bsz):
                for st in range(SPREAD_STEPS):
                    row = (hp * bsz + b) * SPREAD_STEPS + st
                    for s in range(vs):
                        q[hpar, row, st * LANES + s * group + b * RW_HEADS + 2 * hp + hpar] = 1.0
    return jnp.asarray(np.concatenate([q[0], q[1]] * 3, axis=0), BF16)


def _spread_k(xs):
    bsz, t, _ = xs[0].shape
    assert (RW_HEADS // 2) * bsz * SPREAD_STEPS == LANES, "the transposed tile must be 128 lanes wide"
    n = len(xs)
    ngrp = SPREAD_BLOCK // SPREAD_STEPS

    def body(*refs):
        qm = refs[n][...]
        for x_ref, o_ref in zip(refs[:n], refs[n + 1:]):
            cols = [[] for _ in range(6)]
            for m in range(ngrp):
                at = slice(SPREAD_STEPS * m, SPREAD_STEPS * (m + 1))
                x8 = jnp.concatenate([x_ref[b, at, LANES * hp:LANES * (hp + 1)]
                                      for hp in range(RW_HEADS // 2) for b in range(bsz)], axis=0)
                for pi, piece in enumerate(_split3(x8.T)):
                    cols[2 * pi].append(piece[0:64])
                    cols[2 * pi + 1].append(piece[64:128])
            lhs = jnp.concatenate([jnp.concatenate(c, axis=0) for c in cols], axis=1)
            y = _dot(lhs, qm)
            for m in range(ngrp):
                for st in range(SPREAD_STEPS):
                    o_ref[SPREAD_STEPS * m + st] = y[64 * m:64 * (m + 1), LANES * st:LANES * (st + 1)]

    qm = _spread_matrix(bsz)
    return pl.pallas_call(
        body, name="wkv_spread", grid=(t // SPREAD_BLOCK,),
        out_shape=[jax.ShapeDtypeStruct((t, 64, LANES), F32)] * n,
        in_specs=[pl.BlockSpec((bsz, SPREAD_BLOCK, RW), lambda i: (0, i, 0))] * n + [_full(qm.shape)],
        out_specs=[pl.BlockSpec((SPREAD_BLOCK, 64, LANES), lambda i: (i, 0, 0))] * n,
        compiler_params=_cparams(("parallel",)),
    )(*xs, qm)


def _wkv_fwd(r, w, kp, al, be, v):
    t, vq = v.shape[0], v.shape[1]

    def body(r_ref, w_ref, kp_ref, al_ref, be_ref, v_ref, y_ref, a_ref, u_ref, st_ref):
        @pl.when(pl.program_id(0) == 0)
        def _():
            st_ref[...] = jnp.zeros(st_ref.shape, F32)

        def step(tl, _):
            rv, wv, kv, av, bv = r_ref[tl], w_ref[tl], kp_ref[tl], al_ref[tl], be_ref[tl]
            vals = v_ref[tl]
            yrows, urows = [], []
            for q in range(vq):
                s = st_ref[q]
                u = _ksum(s * av)
                s = s * wv + bv * u + kv * vals[q:q + 1]
                st_ref[q] = s
                a_ref[tl, q] = s
                urows.append(u)
                yrows.append(_ksum(s * rv))
            y_ref[tl] = jnp.concatenate(yrows, axis=0)
            u_ref[tl] = jnp.concatenate(urows, axis=0)
            return 0

        lax.fori_loop(0, CH, step, 0)

    kspec = pl.BlockSpec((CH, 64, LANES), lambda i: (i, 0, 0))
    vspec = pl.BlockSpec((CH, vq, LANES), lambda i: (i, 0, 0))
    vsd = jax.ShapeDtypeStruct((t, vq, LANES), F32)
    return pl.pallas_call(
        body, name="wkv_fwd", grid=(t // CH,),
        out_shape=[vsd, jax.ShapeDtypeStruct((t, vq, 64, LANES), F32), vsd],
        in_specs=[kspec] * 5 + [vspec],
        out_specs=[vspec, pl.BlockSpec((CH, vq, 64, LANES), lambda i: (i, 0, 0, 0)), vspec],
        scratch_shapes=[pltpu.VMEM((vq, 64, LANES), F32)],
        compiler_params=_cparams(("arbitrary",)),
    )(r, w, kp, al, be, v)


def _wkv_bwd(r, w, kp, al, be, v, dy, states, u):
    t, vq = v.shape[0], v.shape[1]
    vs = 64 // vq
    group = LANES // vs
    n = t // CH
    ng = CH // vs

    def body(r_ref, w_ref, kp_ref, al_ref, be_ref, v_ref, dy_ref, u_ref, a_ref, ap_ref,
             dr_ref, dw_ref, dkp_ref, dal_ref, dbe_ref, dv_ref, ds_ref):
        @pl.when(pl.program_id(0) == 0)
        def _():
            ds_ref[...] = jnp.zeros(ds_ref.shape, F32)

        earliest = pl.program_id(0) == n - 1

        def reverse(i, _):
            g = ng - 1 - i
            grp = _lane_group((64, LANES), group)
            outs = None
            for j in reversed(range(vs)):
                tl = g * vs + j
                rv, wv, kv, av, bv = r_ref[tl], w_ref[tl], kp_ref[tl], al_ref[tl], be_ref[tl]
                vals, dys, us = v_ref[tl], dy_ref[tl], u_ref[tl]
                acc = None
                dvrows = []
                for q in range(vq):
                    if j > 0:
                        s_prev = a_ref[tl - 1, q]
                    else:
                        before = jnp.where(earliest, 0.0, ap_ref[0, q])
                        s_prev = jnp.where(g == 0, before, a_ref[jnp.maximum(tl - 1, 0), q])
                    dyq = dys[q:q + 1]
                    ds = ds_ref[q] + rv * dyq
                    c = _ksum(ds * bv)
                    dvrows.append(_ksum(ds * kv))
                    terms = (a_ref[tl, q] * dyq, ds * s_prev, ds * vals[q:q + 1], s_prev * c, ds * us[q:q + 1])
                    acc = terms if acc is None else tuple(a + b for a, b in zip(acc, terms))
                    ds_ref[q] = ds * wv + av * c
                dv_ref[tl] = jnp.concatenate(dvrows, axis=0)
                summed = [_fold(a, group) for a in acc]
                outs = summed if outs is None else [jnp.where(grp == j, f, o) for f, o in zip(summed, outs)]
            for ref, o in zip((dr_ref, dw_ref, dkp_ref, dal_ref, dbe_ref), outs):
                ref[g] = o
            return 0

        lax.fori_loop(0, ng, reverse, 0)

    kspec = pl.BlockSpec((CH, 64, LANES), lambda i: (n - 1 - i, 0, 0))
    gspec = pl.BlockSpec((ng, 64, LANES), lambda i: (n - 1 - i, 0, 0))
    vspec = pl.BlockSpec((CH, vq, LANES), lambda i: (n - 1 - i, 0, 0))
    ksd = jax.ShapeDtypeStruct((t // vs, 64, LANES), F32)
    return pl.pallas_call(
        body, name="wkv_bwd", grid=(n,),
        out_shape=[ksd] * 5 + [jax.ShapeDtypeStruct((t, vq, LANES), F32)],
        in_specs=[kspec] * 5 + [vspec, vspec, vspec,
                                pl.BlockSpec((CH, vq, 64, LANES), lambda i: (n - 1 - i, 0, 0, 0)),
                                pl.BlockSpec((1, vq, 64, LANES), lambda i: (jnp.maximum((n - 1 - i) * CH - 1, 0), 0, 0, 0))],
        out_specs=[gspec] * 5 + [vspec],
        scratch_shapes=[pltpu.VMEM((vq, 64, LANES), F32)],
        compiler_params=_cparams(("arbitrary",)),
    )(r, w, kp, al, be, v, dy, u, states, states)


def _post(x, tgt, pp, o, yw, r, kp, v, ln_g, ln_b, r_k, wo, wot, gpost, bo):
    bsz, t, _ = x.shape
    tt = TT_VPU
    nt = t // tt

    def body(x_ref, tgt_ref, z_ref, o_ref, yw_ref, r_ref, kp_ref, v_ref, lng_ref, lnb_ref, rk_ref, wo_ref, wot_ref,
             gpost_ref, bo_ref,
             dh_ref, dz_ref, dym_ref, dyw_ref, dbon_ref, loss_ref, dwo_ref, dgpost_ref, dlng_ref, dlnb_ref, drk_ref):
        first = (pl.program_id(0) == 0) & (pl.program_id(1) == 0)

        @pl.when(first)
        def _():
            for ref in (loss_ref, dwo_ref, dgpost_ref, dlng_ref, dlnb_ref, drk_ref):
                ref[...] = jnp.zeros(ref.shape, F32)

        bo_m = bo_ref[...]
        seg = lambda a: _seg(a, bo_m)
        rowsum = lambda a: jnp.sum(a, axis=0, keepdims=True)
        ywv, rv, kpv, vv = yw_ref[0], r_ref[0], kp_ref[0], v_ref[0]
        ln_g, r_k = lng_ref[...], rk_ref[...]
        mean = seg(ywv) * (1.0 / 64)
        yc = ywv - mean
        rstd = lax.rsqrt(seg(yc * yc) * (1.0 / 64) + GN_EPS)
        yhat = yc * rstd
        sb = seg(rv * kpv * r_k)
        y_rw = yhat * ln_g + lnb_ref[...] + sb * vv
        z = z_ref[0]
        sig = _sigmoid(z)
        sz = z * sig
        ycat = jnp.concatenate([o_ref[0], y_rw], axis=1)
        ycg = (ycat * sz).astype(BF16)
        out = _dot(ycg, wo_ref[...])
        hn, nx, rstd_o = _rms(out, gpost_ref[...], D)
        err = x_ref[0] + hn - tgt_ref[0]
        loss_ref[...] += jnp.sum(err * err) * (0.5 / D)
        dh = err * (1.0 / D)
        dh_ref[0] = dh
        dout, dgp = _rms_bwd(dh, nx, rstd_o, gpost_ref[...], D)
        dgpost_ref[...] += dgp
        doutb = dout.astype(BF16)
        dwo_ref[...] += _dot_tn(ycg, doutb)
        dycg = _dot(doutb, wot_ref[...])
        dz_ref[0] = dycg * ycat * (sig * (1.0 + z * (1.0 - sig)))
        dycat = dycg * sz
        dym_ref[0] = dycat[:, 0:512]
        dy_rw = dycat[:, 512:1024]
        dlnb_ref[...] += rowsum(dy_rw)
        dlng_ref[...] += rowsum(dy_rw * yhat)
        dyhat = dy_rw * ln_g
        dyw_ref[0] = rstd * (dyhat - seg(dyhat) * (1.0 / 64) - yhat * (seg(dyhat * yhat) * (1.0 / 64)))
        dsb = seg(dy_rw * vv)
        drk_ref[...] += rowsum(dsb * rv * kpv)
        dbon_ref[0, :, 0:512] = dsb * r_k
        dbon_ref[0, :, 512:1024] = dy_rw * sb

    tok = lambda c: pl.BlockSpec((1, tt, c), lambda b, i: (b, i, 0))
    full = lambda a: _full(a.shape)
    ins = (x, tgt, pp, o, yw, r, kp, v, ln_g, ln_b, r_k, wo, wot, gpost, bo)
    in_specs = [tok(D), tok(D), tok(1024)] + [tok(512)] * 5 + [full(a) for a in ins[8:]]
    sd = lambda c: jax.ShapeDtypeStruct((bsz, t, c), F32)
    vec = lambda c: jax.ShapeDtypeStruct((1, c), F32)
    out_shape = [sd(D), sd(1024), sd(512), sd(512), sd(1024), jax.ShapeDtypeStruct((8, LANES), F32),
                 jax.ShapeDtypeStruct((1024, 1024), F32), vec(D), vec(512), vec(512), vec(512)]
    out_specs = [tok(D), tok(1024), tok(512), tok(512), tok(1024), _resident((8, LANES)), _resident((1024, 1024)),
                 _resident((1, D)), _resident((1, 512)), _resident((1, 512)), _resident((1, 512))]
    return pl.pallas_call(
        body, name="post", grid=(bsz, nt), out_shape=out_shape, in_specs=in_specs, out_specs=out_specs,
        compiler_params=_cparams(("arbitrary", "arbitrary")),
    )(*ins)


def _pre_bwd_a(pp, pos, invf, cqkv_w, mu, w0, w2p, w2pt, a0, a2p, a2pt, k_k, k_a, bo,
               dq, dk, dva, dwkv, dbon):
    gq, wuqt, gkv, wukvt = cqkv_w
    bsz, t, _ = pp.shape
    tt = TT_VPU
    nt = t // tt
    dr_w, dw_w, dkp_w, dv_w, dal_w, dbe_w = dwkv

    def body(pp_ref, pos_ref, invf_ref, gq_ref, wuqt_ref, gkv_ref, wukvt_ref, mu_ref, w0_ref, w2p_ref, w2pt_ref,
             a0_ref, a2p_ref, a2pt_ref, kk_ref, ka_ref, bo_ref, dq_ref, dk_ref, dva_ref,
             dr_ref, dw_ref, dkp_ref, dv_ref, dal_ref, dbe_ref, dbon_ref,
             da_ref, dwuq_ref, dwukv_ref, dw2p_ref, da2p_ref, dgq_ref, dgkv_ref, dmu_ref, dw0_ref, da0_ref,
             dkk_ref, dka_ref, carry):
        i = pl.program_id(1)
        first = (pl.program_id(0) == 0) & (i == 0)

        @pl.when(first)
        def _():
            for ref in (dwuq_ref, dwukv_ref, dw2p_ref, da2p_ref, dgq_ref, dgkv_ref, dmu_ref, dw0_ref, da0_ref,
                        dkk_ref, dka_ref):
                ref[...] = jnp.zeros(ref.shape, F32)

        bo_m = bo_ref[...]
        rowsum = lambda a: jnp.sum(a, axis=0, keepdims=True)
        prw = pp_ref[0, :, RW0 - CQ0:DP - CQ0]

        @pl.when(i == 0)
        def _():
            carry[...] = jnp.zeros(carry.shape, F32)

        ps, sh = _shift_mix(prw, carry[7:8, :], mu_ref[...])
        carry[...] = prw[tt - 8:tt, :]
        k_k, k_a = kk_ref[...], ka_ref[...]
        g = _rw_gates(ps, w0_ref[...], w2p_ref[...], a0_ref[...], a2p_ref[...], k_k, k_a, bo_m)
        a, kk, k = g["a"], g["kk"], g["k"]
        dr = dr_ref[0] + dbon_ref[0, :, 0:512] * g["kp"]
        dkp = dkp_ref[0] + dbon_ref[0, :, 0:512] * g["r"]
        dv = dv_ref[0] + dbon_ref[0, :, 512:1024]
        dbe = dbe_ref[0]
        dkk = dbe * a - dal_ref[0]
        da = dbe * kk + dkp * k * k_a
        dka_ref[...] += rowsum(dkp * k * (a - 1.0))
        dm = (dkk - kk * _seg(dkk * kk, bo_m)) / g["nrm"]
        dkk_ref[...] += rowsum(dm * k)
        dk_tot = dkp * (1.0 + (a - 1.0) * k_a) + dm * k_k
        dapre = da * a * (1.0 - a)
        da0_ref[...] += rowsum(dapre)
        dapb = dapre.astype(BF16)
        da2p_ref[...] += _dot_tn(g["misc"].astype(BF16), dapb)
        dwpre = dw_ref[0] * g["w"] * (-g["e"]) * _sigmoid(-g["wpre"])
        dw0_ref[...] += rowsum(dwpre)
        dwpb = dwpre.astype(BF16)
        th = g["th"]
        dw2p_ref[...] += _dot_tn(th.astype(BF16), dwpb)
        dmisc = _dot(dapb, a2pt_ref[...]) + _dot(dwpb, w2pt_ref[...]) * (1.0 - th * th)
        ang = pos_ref[0] * invf_ref[...]
        cs, sn = jnp.cos(ang), jnp.sin(ang)
        unrope = lambda gr: gr * cs - _rot(gr * sn)
        lane = lax.broadcasted_iota(jnp.int32, cs.shape, 1)
        dkr = dk_ref[0, :, 128:256]
        for h in range(1, HEADS):
            dkr = dkr + dk_ref[0, :, 256 * h + 128:256 * h + 256]
        dkr = jnp.where(lane < 64, unrope(dkr), 0.0)
        dmisc = dmisc + jnp.concatenate([dkr, jnp.zeros_like(dkr)], axis=1)
        dqp = jnp.concatenate(
            [blk for h in range(HEADS)
             for blk in (dq_ref[0, :, 256 * h:256 * h + 128], unrope(dq_ref[0, :, 256 * h + 128:256 * h + 256]))],
            axis=1).astype(BF16)
        dkvp = jnp.concatenate([dk_ref[0, :, 256 * h:256 * h + 128] for h in range(HEADS)] + [dva_ref[0]],
                               axis=1).astype(BF16)
        cqn, cq_nx, cq_rstd = _rms(pp_ref[0, :, 0:256], gq_ref[...], 256)
        ckvn, ckv_nx, ckv_rstd = _rms(pp_ref[0, :, CKV0 - CQ0:CKV0 - CQ0 + 128], gkv_ref[...], 128)
        dwuq_ref[...] += _dot_tn(cqn.astype(BF16), dqp)
        dwukv_ref[...] += _dot_tn(ckvn.astype(BF16), dkvp)
        dcq, dgq = _rms_bwd(_dot(dqp, wuqt_ref[...]), cq_nx, cq_rstd, gq_ref[...], 256)
        dckv, dgkv = _rms_bwd(_dot(dkvp, wukvt_ref[...]), ckv_nx, ckv_rstd, gkv_ref[...], 128)
        dgq_ref[...] += dgq
        dgkv_ref[...] += dgkv
        dps = jnp.concatenate([dr, dk_tot, dv, dmisc], axis=1)
        dmu_ref[...] += rowsum(dps * (sh - prw))
        da_ref[0, :, 0:256] = dcq
        da_ref[0, :, 256:384] = dckv
        da_ref[0, :, 384:384 + NRW] = dps

    tok = lambda c: pl.BlockSpec((1, tt, c), lambda b, i: (b, i, 0))
    full = lambda a: _full(a.shape)
    ins = (pp, pos, invf, gq, wuqt, gkv, wukvt, mu, w0, w2p, w2pt, a0, a2p, a2pt, k_k, k_a, bo,
           dq, dk, dva, dr_w, dw_w, dkp_w, dv_w, dal_w, dbe_w, dbon)
    in_specs = ([tok(DP - CQ0), tok(1)] + [full(a) for a in ins[2:17]] + [tok(1024), tok(1024), tok(512)]
                + [tok(512)] * 6 + [tok(1024)])
    shp = lambda *s: jax.ShapeDtypeStruct(s, F32)
    out_shape = [shp(bsz, t, 384 + NRW), shp(256, 1024), shp(128, 1024), shp(256, 512), shp(256, 512),
                 shp(1, 256), shp(1, 128), shp(1, NRW), shp(1, 512), shp(1, 512), shp(1, 512), shp(1, 512)]
    out_specs = [tok(384 + NRW)] + [_resident(s.shape) for s in out_shape[1:]]
    return pl.pallas_call(
        body, name="pre_bwd_a", grid=(bsz, nt), out_shape=out_shape, in_specs=in_specs, out_specs=out_specs,
        scratch_shapes=[pltpu.VMEM((8, NRW), F32)],
        compiler_params=_cparams(("arbitrary", "arbitrary")),
    )(*ins)


def _pre_bwd_b(x, dh, dz, da, mu, wpt, gpre):
    bsz, t, _ = x.shape
    nt = t // TT
    nblk = t // 8

    def body(x_ref, dh_ref, dz_ref, da_ref, nxt_ref, mu_ref, wpt_ref, gpre_ref, gx_ref, dp_ref, dgpre_ref):
        i = pl.program_id(1)
        first = (pl.program_id(0) == 0) & (i == 0)

        @pl.when(first)
        def _():
            dgpre_ref[...] = jnp.zeros(dgpre_ref.shape, F32)

        mu_v = mu_ref[...]
        dps = da_ref[0, :, 384:384 + NRW]
        nxt = jnp.where(i < nt - 1, nxt_ref[0, 0:1, 384:384 + NRW], 0.0)
        row = lax.broadcasted_iota(jnp.int32, dps.shape, 0)
        up = jnp.where(row == TT - 1, nxt, pltpu.roll(dps, TT - 1, 0))
        dprw = dps * (1.0 - mu_v) + up * mu_v
        dp = jnp.concatenate([dz_ref[0], da_ref[0, :, 0:384], dprw], axis=1).astype(BF16)
        dp_ref[0] = dp
        du = _dot(dp, wpt_ref[...])
        _, nx, rstd = _rms(x_ref[0], gpre_ref[...], D)
        dx, dg = _rms_bwd(du, nx, rstd, gpre_ref[...], D)
        dgpre_ref[...] += dg
        gx_ref[0] = dh_ref[0] + dx

    tok = lambda c: pl.BlockSpec((1, TT, c), lambda b, i: (b, i, 0))
    nxt_spec = pl.BlockSpec((1, 8, 384 + NRW), lambda b, i: (b, jnp.minimum((i + 1) * (TT // 8), nblk - 1), 0))
    ins = (x, dh, dz, da, da, mu, wpt, gpre)
    return pl.pallas_call(
        body, name="pre_bwd_b", grid=(bsz, nt),
        out_shape=[jax.ShapeDtypeStruct((bsz, t, D), F32), jax.ShapeDtypeStruct((bsz, t, DP), BF16),
                   jax.ShapeDtypeStruct((1, D), F32)],
        in_specs=[tok(D), tok(D), tok(1024), tok(384 + NRW), nxt_spec, _full(mu.shape), _full(wpt.shape),
                  _full(gpre.shape)],
        out_specs=[tok(D), tok(DP), _resident((1, D))],
        compiler_params=_cparams(("arbitrary", "arbitrary")),
    )(*ins)


def _tn_matmul(a, b, bn, name, bk=512):
    kdim, m = a.shape
    _, n = b.shape
    nk = kdim // bk

    def body(a_ref, b_ref, o_ref):
        @pl.when(pl.program_id(1) == 0)
        def _():
            o_ref[...] = jnp.zeros(o_ref.shape, F32)

        o_ref[...] += _dot_tn(a_ref[...], b_ref[...])

    return pl.pallas_call(
        body, name=name, grid=(n // bn, nk),
        out_shape=jax.ShapeDtypeStruct((m, n), F32),
        in_specs=[pl.BlockSpec((bk, m), lambda j, kk: (kk, 0)), pl.BlockSpec((bk, bn), lambda j, kk: (kk, j))],
        out_specs=pl.BlockSpec((m, bn), lambda j, kk: (0, j)),
        compiler_params=_cparams(("parallel", "arbitrary")),
    )(a, b)


SHARDED = ("w_in", "mla_w_uq", "mla_w_ukv", "rw_w2", "rw_a2", "w_out")
SMALL = ("norm_pre_g", "mla_q_norm_g", "mla_kv_norm_g", "rw_mu", "rw_w0", "rw_a0", "rw_k_k", "rw_k_a", "rw_r_k",
         "rw_ln_g", "rw_ln_b", "norm_post_g")
WEIGHTS = ("norm_pre_g", "w_in", "mla_q_norm_g", "mla_w_uq", "mla_kv_norm_g", "mla_w_ukv", "rw_mu", "rw_w0", "rw_w2",
           "rw_a0", "rw_a2", "rw_k_k", "rw_k_a", "rw_r_k", "rw_ln_g", "rw_ln_b", "w_out", "norm_post_g")


def _unpack_shard(packed, like):
    out, at = {}, 0
    for n, rows in zip(SHARDED[1:5], PACK_ROWS):
        out[n] = packed[at:at + rows].reshape(like[n].shape)
        at += rows
    return out


def _constants():
    bo = np.kron(np.eye(2, dtype=np.float32), np.ones((64, 64), np.float32))
    inv = ROPE_THETA ** (-np.arange(0, 64, 2, dtype=np.float32) / 64)
    invf = np.concatenate([inv, inv, np.zeros(64, np.float32)]).astype(np.float32)[None, :]
    return jnp.asarray(bo, BF16), jnp.asarray(invf)


def kernel(x, positions, norm_pre_g, w_in, mla_q_norm_g, mla_w_uq, mla_kv_norm_g, mla_w_ukv, rw_mu, rw_w0, rw_w2, rw_a0, rw_a2, rw_k_k, rw_k_a, rw_r_k, rw_ln_g, rw_ln_b, w_out, norm_post_g, loss_target, m_norm_pre_g, m_w_in, m_mla_q_norm_g, m_mla_w_uq, m_mla_kv_norm_g, m_mla_w_ukv, m_rw_mu, m_rw_w0, m_rw_w2, m_rw_a0, m_rw_a2, m_rw_k_k, m_rw_k_a, m_rw_r_k, m_rw_ln_g, m_rw_ln_b, m_w_out, m_norm_post_g, v_norm_pre_g, v_w_in, v_mla_q_norm_g, v_mla_w_uq, v_mla_kv_norm_g, v_mla_w_ukv, v_rw_mu, v_rw_w0, v_rw_w2, v_rw_a0, v_rw_a2, v_rw_k_k, v_rw_k_a, v_rw_r_k, v_rw_ln_g, v_rw_ln_b, v_w_out, v_norm_post_g):
    wts = dict(norm_pre_g=norm_pre_g, w_in=w_in, mla_q_norm_g=mla_q_norm_g, mla_w_uq=mla_w_uq,
               mla_kv_norm_g=mla_kv_norm_g, mla_w_ukv=mla_w_ukv, rw_mu=rw_mu, rw_w0=rw_w0, rw_w2=rw_w2, rw_a0=rw_a0,
               rw_a2=rw_a2, rw_k_k=rw_k_k, rw_k_a=rw_k_a, rw_r_k=rw_r_k, rw_ln_g=rw_ln_g, rw_ln_b=rw_ln_b, w_out=w_out,
               norm_post_g=norm_post_g)
    mom_m = dict(norm_pre_g=m_norm_pre_g, w_in=m_w_in, mla_q_norm_g=m_mla_q_norm_g, mla_w_uq=m_mla_w_uq,
                 mla_kv_norm_g=m_mla_kv_norm_g, mla_w_ukv=m_mla_w_ukv, rw_mu=m_rw_mu, rw_w0=m_rw_w0, rw_w2=m_rw_w2,
                 rw_a0=m_rw_a0, rw_a2=m_rw_a2, rw_k_k=m_rw_k_k, rw_k_a=m_rw_k_a, rw_r_k=m_rw_r_k, rw_ln_g=m_rw_ln_g,
                 rw_ln_b=m_rw_ln_b, w_out=m_w_out, norm_post_g=m_norm_post_g)
    mom_v = dict(norm_pre_g=v_norm_pre_g, w_in=v_w_in, mla_q_norm_g=v_mla_q_norm_g, mla_w_uq=v_mla_w_uq,
                 mla_kv_norm_g=v_mla_kv_norm_g, mla_w_ukv=v_mla_w_ukv, rw_mu=v_rw_mu, rw_w0=v_rw_w0, rw_w2=v_rw_w2,
                 rw_a0=v_rw_a0, rw_a2=v_rw_a2, rw_k_k=v_rw_k_k, rw_k_a=v_rw_k_a, rw_r_k=v_rw_r_k, rw_ln_g=v_rw_ln_g,
                 rw_ln_b=v_rw_ln_b, w_out=v_w_out, norm_post_g=v_norm_post_g)
    bsz, t, _ = x.shape
    bo, invf = _constants()

    g_in, g_uq, g_ukv, g_w2, g_a2, g_out = _ag_weights([wts[n][0] for n in SHARDED])
    w_in_f = jnp.transpose(g_in, (1, 0, 2)).reshape(D, D_IN)
    wp = jnp.concatenate([w_in_f[:, 2112:3136], w_in_f[:, 0:384], w_in_f[:, 448:1984], w_in_f[:, 384:448],
                          w_in_f[:, 1984:2112], jnp.zeros((D, 64), BF16)], axis=1)
    wuq = jnp.pad(jnp.transpose(g_uq, (1, 0, 2)).reshape(256, HEADS, 192), ((0, 0), (0, 0), (0, 64))).reshape(256, 1024)
    wukv = jnp.transpose(jnp.transpose(g_ukv, (1, 0, 2)).reshape(128, HEADS, 2, 128), (0, 2, 1, 3)).reshape(128, 1024)
    w2 = jnp.transpose(g_w2, (1, 0, 2)).reshape(64, RW)
    a2 = jnp.transpose(g_a2, (1, 0, 2)).reshape(64, RW)
    w2p = jnp.pad(w2, ((64, 128), (0, 0)))
    a2p = jnp.pad(a2, ((128, 64), (0, 0)))
    wo = g_out.reshape(D, D)
    mu = jnp.concatenate([rw_mu[:, 0:1536], jnp.zeros((1, 64), F32), rw_mu[:, 1536:1664], jnp.zeros((1, 64), F32)],
                         axis=1)
    r_k = rw_r_k.reshape(1, RW)
    pos = positions.astype(F32)[:, :, None]

    (u, pz, pr, q_att, k_att, v_att, r, w, kp, v, al, be) = _pre_fwd(
        x, pos, invf, norm_pre_g, wp, mla_q_norm_g, wuq, mla_kv_norm_g, wukv, mu, rw_w0, w2p, rw_a0, a2p, rw_k_k,
        rw_k_a, bo)
    o, lse = _attn_fwd(q_att, k_att, v_att)
    rw_k = _spread_k([r, w, kp, al, be])
    v_v = _to_v(v)
    yw_v, states, u_v = _wkv_fwd(*rw_k, v_v)
    yw = _from_v(yw_v, bsz)

    (dh, dz, dym, dyw, dbon, loss_acc, d_wo, d_gpost, d_lng, d_lnb, d_rk) = _post(
        x, loss_target, pz, o, yw, r, kp, v, rw_ln_g, rw_ln_b, r_k, wo, wo.T, norm_post_g, bo)

    d_k = _wkv_bwd(*rw_k, v_v, _to_v(dyw), states, u_v)
    dr_w, dw_w, dkp_w, dal_w, dbe_w = _gather_k(d_k[:5], bsz)
    dwkv = (dr_w, dw_w, dkp_w, _from_v(d_k[5], bsz), dal_w, dbe_w)
    dq, dk, dva = _attn_bwd(q_att, k_att, v_att, o, lse, dym)

    (da, d_wuq, d_wukv, d_w2p, d_a2p, d_gq, d_gkv, d_mu, d_w0, d_a0, d_kk, d_ka) = _pre_bwd_a(
        pr, pos, invf, (mla_q_norm_g, wuq.T, mla_kv_norm_g, wukv.T), mu, rw_w0, w2p, w2p.T, rw_a0, a2p, a2p.T,
        rw_k_k, rw_k_a, bo, dq, dk, dva, dwkv, dbon)
    grad_x, dpb, d_gpre = _pre_bwd_b(x, dh, dz, da, mu, wp.T, norm_pre_g)
    d_wp = _tn_matmul(u.reshape(bsz * t, D), dpb.reshape(bsz * t, DP), DP, "dw_in", bk=1024)

    full_g = {
        "w_in": jnp.concatenate([d_wp[:, 1024:1408], d_wp[:, 2944:3008], d_wp[:, 1408:2944], d_wp[:, 3008:3136],
                                 d_wp[:, 0:1024]], axis=1),
        "mla_w_uq": d_wuq.reshape(256, HEADS, 256)[:, :, :192].reshape(256, 768),
        "mla_w_ukv": jnp.transpose(d_wukv.reshape(128, 2, HEADS, 128), (0, 2, 1, 3)).reshape(128, 1024),
        "rw_w2": d_w2p[64:128],
        "rw_a2": d_a2p[128:192],
        "w_out": d_wo,
    }
    small_g = {
        "norm_pre_g": d_gpre, "mla_q_norm_g": d_gq, "mla_kv_norm_g": d_gkv,
        "rw_mu": jnp.concatenate([d_mu[:, 0:1536], d_mu[:, 1600:1728]], axis=1),
        "rw_w0": d_w0, "rw_a0": d_a0, "rw_k_k": d_kk, "rw_k_a": d_ka, "rw_r_k": d_rk, "rw_ln_g": d_lng,
        "rw_ln_b": d_lnb, "norm_post_g": d_gpost,
    }

    def by_shard(g):
        rows, cols = g.shape
        return jnp.transpose(g.reshape(rows, N_SHARD, cols // N_SHARD), (1, 0, 2))

    g_in = by_shard(full_g["w_in"])
    g_out = full_g["w_out"].reshape(N_SHARD, D // N_SHARD, D)
    packed = jnp.concatenate([by_shard(full_g[n]).reshape(N_SHARD, -1, LANES) for n in SHARDED[1:5]], axis=1)
    halves = [a.reshape(N_SHARD, 2, a.shape[1] // 2, a.shape[2]) for a in (g_in, g_out, packed)]
    red_in, red_out, red_rest = _rs_chips(*_rs_pairs(halves))
    g_shard = red_rest.reshape(PACK_REST, LANES)

    flat = lambda a: a.reshape(1, -1)
    g_small = _small_allreduce([flat(small_g[n]) for n in SMALL], loss_acc)
    loss = g_small[SMALL_USED, 0]

    g_sharded = _unpack_shard(g_shard, {n: wts[n][0] for n in SHARDED})
    g_sharded["w_in"] = red_in.reshape(wts["w_in"][0].shape)
    g_sharded["w_out"] = red_out.reshape(wts["w_out"][0].shape)
    sh = _adamw([wts[n][0] for n in SHARDED], [g_sharded[n] for n in SHARDED], [mom_m[n][0] for n in SHARDED],
                [mom_v[n][0] for n in SHARDED], "adamw_sharded")
    sm = _adamw_small([flat(wts[n]) for n in SMALL], g_small, [flat(mom_m[n]) for n in SMALL],
                      [flat(mom_v[n]) for n in SMALL])

    def outputs(sharded, small):
        out = {n: a[None] for n, a in zip(SHARDED, sharded)}
        out.update({n: a.reshape(wts[n].shape) for n, a in zip(SMALL, small)})
        return out

    grads = outputs([g_sharded[n] for n in SHARDED], sm[0])
    deltas, new_m, new_v = (outputs(sh[k], sm[k + 1]) for k in range(3))
    return (loss, grad_x, *[grads[n] for n in WEIGHTS], *[deltas[n] for n in WEIGHTS],
            *[new_m[n] for n in WEIGHTS], *[new_v[n] for n in WEIGHTS])
```
